```python
import math
import jax, jax.numpy as jnp
from jax import lax
import numpy as np

D_MODEL = 1024
BATCH = 8
SEQ = 4096
DEPTH = 4

N_MIXERS = 2
SB_HEADS = 16
SB_HEAD_DIM = D_MODEL // SB_HEADS
Q_BLOCK = 128
HG_EXPAND = 128
HG_HEADS = D_MODEL // HG_EXPAND
HG_KEY_DIM = HG_EXPAND
HG_VAL_DIM = D_MODEL // HG_HEADS
HG_CHUNK = 64
D_FF = 4 * D_MODEL
N_SB = (DEPTH + N_MIXERS - 1) // N_MIXERS
N_HG = DEPTH // N_MIXERS
EPS = 1e-6

kernel_name = "stick_breaking_hgrn2_hybrid"


def rmsnorm(x, gain):
    xf = x.astype(jnp.float32)
    y = xf * lax.rsqrt(jnp.mean(xf * xf, axis=-1, keepdims=True) + EPS)
    return (y * gain.astype(jnp.float32)).astype(x.dtype)


def stick_breaking_attention(q, k, v):
    seq = q.shape[2]
    scale = 1.0 / math.sqrt(q.shape[-1])
    outs = []
    for t0 in range(0, seq, Q_BLOCK):
        t1 = t0 + Q_BLOCK
        z = jnp.einsum('bhqd,bhkd->bhqk', q[:, :, t0:t1], k[:, :, :t1]).astype(jnp.float32) * scale
        causal = jnp.arange(t1)[None, :] < (t0 + jnp.arange(Q_BLOCK))[:, None]
        log_stay = jnp.where(causal, jax.nn.log_sigmoid(-z), 0.0)
        log_between = lax.cumsum(log_stay, axis=3, reverse=True) - log_stay
        weights = jnp.where(causal, jnp.exp(jax.nn.log_sigmoid(z) + log_between), 0.0)
        outs.append(jnp.einsum('bhqk,bhkd->bhqd', weights.astype(v.dtype), v[:, :, :t1]))
    return jnp.concatenate(outs, axis=2)


def stick_breaking_mixer(h, w_qkv, q_gain, k_gain, w_o):
    bsz, seq, _ = h.shape
    qkv = h @ w_qkv
    q, k, v = jnp.split(qkv, 3, axis=-1)
    q = rmsnorm(q.reshape(bsz, seq, SB_HEADS, SB_HEAD_DIM), q_gain)
    k = rmsnorm(k.reshape(bsz, seq, SB_HEADS, SB_HEAD_DIM), k_gain)
    v = v.reshape(bsz, seq, SB_HEADS, SB_HEAD_DIM)
    q, k, v = (jnp.transpose(a, (0, 2, 1, 3)) for a in (q, k, v))
    o = stick_breaking_attention(q, k, v)
    o = jnp.transpose(o, (0, 2, 1, 3)).reshape(bsz, seq, D_MODEL)
    return o @ w_o


def hgrn2_chunk_scan(q, k, v, log_f):
    bsz, nh, seq, dk = q.shape
    dv = v.shape[-1]
    n_chunks = seq // HG_CHUNK

    def to_chunks(a):
        return jnp.moveaxis(a.astype(jnp.float32).reshape(bsz, nh, n_chunks, HG_CHUNK, a.shape[-1]), 2, 0)

    incl = jnp.tril(jnp.ones((HG_CHUNK, HG_CHUNK), dtype=bool))

    def step(state, inp):
        qc, kc, vc, lfc = inp
        b = jnp.cumsum(lfc, axis=2)
        inter = jnp.einsum('bhck,bhkv->bhcv', qc * jnp.exp(b), state)
        diff = b[:, :, :, None, :] - b[:, :, None, :, :]
        decay = jnp.where(incl[:, :, None], jnp.exp(jnp.minimum(diff, 0.0)), 0.0)
        scores = jnp.einsum('bhtk,bhsk,bhtsk->bhts', qc, kc, decay)
        intra = jnp.einsum('bhts,bhsv->bhtv', scores, vc)
        b_last = b[:, :, -1:, :]
        new_state = (jnp.exp(b_last[:, :, 0, :])[..., None] * state
                     + jnp.einsum('bhsk,bhsv->bhkv', kc * jnp.exp(b_last - b), vc))
        return new_state, inter + intra

    init = jnp.zeros((bsz, nh, dk, dv), jnp.float32)
    _, ys = lax.scan(step, init, (to_chunks(q), to_chunks(k), to_chunks(v), to_chunks(log_f)))
    return jnp.moveaxis(ys, 0, 2).reshape(bsz, nh, seq, dv)


def hgrn2_mixer(h, w_in, lower_bound, norm_gain, w_o):
    bsz, seq, _ = h.shape
    proj = h @ w_in
    q, f, i, g = jnp.split(proj, 4, axis=-1)
    q = jax.nn.silu(q)
    lb = lower_bound.astype(jnp.float32)
    forget = lb + (1.0 - lb) * jax.nn.sigmoid(f.astype(jnp.float32))
    log_f = jnp.log(forget)
    k = -jnp.expm1(log_f)

    def heads(a, d):
        return jnp.transpose(a.reshape(bsz, seq, HG_HEADS, d), (0, 2, 1, 3))

    o = hgrn2_chunk_scan(heads(q, HG_KEY_DIM), heads(k, HG_KEY_DIM),
                         heads(i, HG_VAL_DIM), heads(log_f, HG_KEY_DIM))
    o = jnp.transpose(o, (0, 2, 1, 3)).astype(h.dtype)
    o = rmsnorm(o, norm_gain).reshape(bsz, seq, D_MODEL)
    o = o * jax.nn.sigmoid(g)
    return o @ w_o


def squared_relu_mlp(h, w1, w2):
    a = jax.nn.relu(h @ w1)
    return (a * a) @ w2


def _fwd_setup_inputs(seed: int = 0) -> dict:
    key = jax.random.key(seed)
    ks = jax.random.split(key, 12)
    d_in = D_MODEL ** -0.5
    res = (2 * DEPTH) ** -0.5
    nrm = jax.random.normal
    return {
        "x": nrm(ks[0], (BATCH, SEQ, D_MODEL), jnp.float32),
        "norm_gains": 1.0 + 0.02 * nrm(ks[1], (DEPTH, 2, D_MODEL), jnp.float32),
        "sb_w_qkv": nrm(ks[2], (N_SB, D_MODEL, 3 * D_MODEL), jnp.float32) * d_in,
        "sb_q_gain": 1.0 + 0.02 * nrm(ks[3], (N_SB, SB_HEAD_DIM), jnp.float32),
        "sb_k_gain": 1.0 + 0.02 * nrm(ks[4], (N_SB, SB_HEAD_DIM), jnp.float32),
        "sb_w_o": nrm(ks[5], (N_SB, D_MODEL, D_MODEL), jnp.float32) * d_in * res,
        "hg_w_in": nrm(ks[6], (N_HG, D_MODEL, 4 * D_MODEL), jnp.float32) * d_in,
        "hg_lb_logits": 0.5 * nrm(ks[7], (N_HG, D_MODEL), jnp.float32),
        "hg_norm_gain": 1.0 + 0.02 * nrm(ks[8], (N_HG, HG_VAL_DIM), jnp.float32),
        "hg_w_o": nrm(ks[9], (N_HG, D_MODEL, D_MODEL), jnp.float32) * d_in * res,
        "mlp_w1": nrm(ks[10], (DEPTH, D_MODEL, D_FF), jnp.float32) * d_in,
        "mlp_w2": nrm(ks[11], (DEPTH, D_FF, D_MODEL), jnp.float32) * (D_FF ** -0.5) * res,
    }


def _fwd_reference(x, norm_gains, sb_w_qkv, sb_q_gain, sb_k_gain, sb_w_o,
              hg_w_in, hg_lb_logits, hg_norm_gain, hg_w_o, mlp_w1, mlp_w2):
    p = jax.nn.softmax(hg_lb_logits.astype(jnp.float32), axis=0)
    lower_bounds = jnp.cumsum(p, axis=0) - p[0:1]
    for layer in range(DEPTH):
        j = layer // N_MIXERS
        h = rmsnorm(x, norm_gains[layer, 0])
        if layer % N_MIXERS == 0:
            x = x + stick_breaking_mixer(h, sb_w_qkv[j], sb_q_gain[j], sb_k_gain[j], sb_w_o[j])
        else:
            x = x + hgrn2_mixer(h, hg_w_in[j], lower_bounds[j], hg_norm_gain[j], hg_w_o[j])
        h = rmsnorm(x, norm_gains[layer, 1])
        x = x + squared_relu_mlp(h, mlp_w1[layer], mlp_w2[layer])
    return x


import jax as _jax
import jax.numpy as _jnp

TWIN_FORMAT = 'train_step'
FWD_PARAMS = ['x', 'norm_gains', 'sb_w_qkv', 'sb_q_gain', 'sb_k_gain', 'sb_w_o', 'hg_w_in', 'hg_lb_logits', 'hg_norm_gain', 'hg_w_o', 'mlp_w1', 'mlp_w2']
TWIN_WEIGHTS = ['norm_gains', 'sb_w_qkv', 'sb_q_gain', 'sb_k_gain', 'sb_w_o', 'hg_w_in', 'hg_lb_logits', 'hg_norm_gain', 'hg_w_o', 'mlp_w1', 'mlp_w2']
TWIN_DIFF_INPUT = 'x'
TWIN_INPUTS = ['x', 'norm_gains', 'sb_w_qkv', 'sb_q_gain', 'sb_k_gain', 'sb_w_o', 'hg_w_in', 'hg_lb_logits', 'hg_norm_gain', 'hg_w_o', 'mlp_w1', 'mlp_w2', 'loss_target', 'm_norm_gains', 'm_sb_w_qkv', 'm_sb_q_gain', 'm_sb_k_gain', 'm_sb_w_o', 'm_hg_w_in', 'm_hg_lb_logits', 'm_hg_norm_gain', 'm_hg_w_o', 'm_mlp_w1', 'm_mlp_w2', 'v_norm_gains', 'v_sb_w_qkv', 'v_sb_q_gain', 'v_sb_k_gain', 'v_sb_w_o', 'v_hg_w_in', 'v_hg_lb_logits', 'v_hg_norm_gain', 'v_hg_w_o', 'v_mlp_w1', 'v_mlp_w2']
TWIN_OUTPUTS = ['loss', 'grad_x', 'grad_norm_gains', 'grad_sb_w_qkv', 'grad_sb_q_gain', 'grad_sb_k_gain', 'grad_sb_w_o', 'grad_hg_w_in', 'grad_hg_lb_logits', 'grad_hg_norm_gain', 'grad_hg_w_o', 'grad_mlp_w1', 'grad_mlp_w2', 'delta_norm_gains', 'delta_sb_w_qkv', 'delta_sb_q_gain', 'delta_sb_k_gain', 'delta_sb_w_o', 'delta_hg_w_in', 'delta_hg_lb_logits', 'delta_hg_norm_gain', 'delta_hg_w_o', 'delta_mlp_w1', 'delta_mlp_w2', 'new_m_norm_gains', 'new_m_sb_w_qkv', 'new_m_sb_q_gain', 'new_m_sb_k_gain', 'new_m_sb_w_o', 'new_m_hg_w_in', 'new_m_hg_lb_logits', 'new_m_hg_norm_gain', 'new_m_hg_w_o', 'new_m_mlp_w1', 'new_m_mlp_w2', 'new_v_norm_gains', 'new_v_sb_w_qkv', 'new_v_sb_q_gain', 'new_v_sb_k_gain', 'new_v_sb_w_o', 'new_v_hg_w_in', 'new_v_hg_lb_logits', 'new_v_hg_norm_gain', 'new_v_hg_w_o', 'new_v_mlp_w1', 'new_v_mlp_w2']
TWIN_LEAF_KINDS = {'loss': 'loss', 'grad_x': 'grad_x', 'grad_norm_gains': 'grad_w', 'grad_sb_w_qkv': 'grad_w', 'grad_sb_q_gain': 'grad_w', 'grad_sb_k_gain': 'grad_w', 'grad_sb_w_o': 'grad_w', 'grad_hg_w_in': 'grad_w', 'grad_hg_lb_logits': 'grad_w', 'grad_hg_norm_gain': 'grad_w', 'grad_hg_w_o': 'grad_w', 'grad_mlp_w1': 'grad_w', 'grad_mlp_w2': 'grad_w', 'delta_norm_gains': 'delta_w', 'delta_sb_w_qkv': 'delta_w', 'delta_sb_q_gain': 'delta_w', 'delta_sb_k_gain': 'delta_w', 'delta_sb_w_o': 'delta_w', 'delta_hg_w_in': 'delta_w', 'delta_hg_lb_logits': 'delta_w', 'delta_hg_norm_gain': 'delta_w', 'delta_hg_w_o': 'delta_w', 'delta_mlp_w1': 'delta_w', 'delta_mlp_w2': 'delta_w', 'new_m_norm_gains': 'new_m', 'new_m_sb_w_qkv': 'new_m', 'new_m_sb_q_gain': 'new_m', 'new_m_sb_k_gain': 'new_m', 'new_m_sb_w_o': 'new_m', 'new_m_hg_w_in': 'new_m', 'new_m_hg_lb_logits': 'new_m', 'new_m_hg_norm_gain': 'new_m', 'new_m_hg_w_o': 'new_m', 'new_m_mlp_w1': 'new_m', 'new_m_mlp_w2': 'new_m', 'new_v_norm_gains': 'new_v', 'new_v_sb_w_qkv': 'new_v', 'new_v_sb_q_gain': 'new_v', 'new_v_sb_k_gain': 'new_v', 'new_v_sb_w_o': 'new_v', 'new_v_hg_w_in': 'new_v', 'new_v_hg_lb_logits': 'new_v', 'new_v_hg_norm_gain': 'new_v', 'new_v_hg_w_o': 'new_v', 'new_v_mlp_w1': 'new_v', 'new_v_mlp_w2': 'new_v'}


def _forward(args):
    return _fwd_reference(*[args[k] for k in FWD_PARAMS])


def _output_shape():
    out = _jax.eval_shape(lambda: _forward(_fwd_setup_inputs(0)))
    return out.shape, out.dtype

N_MICROBATCH = 1
ADAM_LR = 0.001
ADAM_B1 = 0.9
ADAM_B2 = 0.999
ADAM_EPS = 1e-08
ADAM_WD = 0.01
ADAM_STEP = 10
PER_EXAMPLE_BATCH_AXIS = {'x': 0, 'loss_target': 0}
SHARED_INPUTS = []
_WEIGHT_DTYPES = {'norm_gains': _jnp.float32, 'sb_w_qkv': _jnp.float32, 'sb_q_gain': _jnp.float32, 'sb_k_gain': _jnp.float32, 'sb_w_o': _jnp.float32, 'hg_w_in': _jnp.float32, 'hg_lb_logits': _jnp.float32, 'hg_norm_gain': _jnp.float32, 'hg_w_o': _jnp.float32, 'mlp_w1': _jnp.float32, 'mlp_w2': _jnp.float32}
MOMENT_SCALE = {'norm_gains': 8.531149e+00, 'sb_w_qkv': 3.794470e-01, 'sb_q_gain': 3.641715e+00, 'sb_k_gain': 3.647119e+00, 'sb_w_o': 1.832367e+00, 'hg_w_in': 4.965777e-01, 'hg_lb_logits': 1.671726e-02, 'hg_norm_gain': 9.841363e+00, 'hg_w_o': 2.860019e+00, 'mlp_w1': 4.312553e-01, 'mlp_w2': 7.396206e+00}


def _to_microbatches(a, axis):
    t = _jnp.moveaxis(a, axis, 0)
    t = t.reshape((N_MICROBATCH, t.shape[0] // N_MICROBATCH) + t.shape[1:])
    return _jnp.moveaxis(t, 1, axis + 1)


def setup_inputs(seed: int = 0) -> dict:
    inp = _fwd_setup_inputs(seed)
    key = _jax.random.fold_in(_jax.random.key(seed), 7919)
    shape, _ = _output_shape()
    out = dict(inp)
    out["loss_target"] = _jax.random.normal(_jax.random.fold_in(key, 0), shape, _jnp.float32)
    for i, name in enumerate(TWIN_WEIGHTS):
        w = inp[name].astype(_jnp.float32)
        if MOMENT_SCALE is None:
            s = _jnp.sqrt(_jnp.mean(_jnp.square(w)) + 1e-30)
        else:
            s = MOMENT_SCALE[name]
        km, kv = _jax.random.split(_jax.random.fold_in(key, i + 1))
        out[name] = w
        out["m_" + name] = s * _jax.random.normal(km, w.shape, _jnp.float32)
        out["v_" + name] = (s * s) * _jax.random.uniform(kv, w.shape, _jnp.float32, 0.5, 1.5)
    if N_MICROBATCH > 1:
        for name, axis in PER_EXAMPLE_BATCH_AXIS.items():
            out[name] = _to_microbatches(out[name], axis)
    return {'x': out['x'], 'norm_gains': out['norm_gains'], 'sb_w_qkv': out['sb_w_qkv'], 'sb_q_gain': out['sb_q_gain'], 'sb_k_gain': out['sb_k_gain'], 'sb_w_o': out['sb_w_o'], 'hg_w_in': out['hg_w_in'], 'hg_lb_logits': out['hg_lb_logits'], 'hg_norm_gain': out['hg_norm_gain'], 'hg_w_o': out['hg_w_o'], 'mlp_w1': out['mlp_w1'], 'mlp_w2': out['mlp_w2'], 'loss_target': out['loss_target'], 'm_norm_gains': out['m_norm_gains'], 'm_sb_w_qkv': out['m_sb_w_qkv'], 'm_sb_q_gain': out['m_sb_q_gain'], 'm_sb_k_gain': out['m_sb_k_gain'], 'm_sb_w_o': out['m_sb_w_o'], 'm_hg_w_in': out['m_hg_w_in'], 'm_hg_lb_logits': out['m_hg_lb_logits'], 'm_hg_norm_gain': out['m_hg_norm_gain'], 'm_hg_w_o': out['m_hg_w_o'], 'm_mlp_w1': out['m_mlp_w1'], 'm_mlp_w2': out['m_mlp_w2'], 'v_norm_gains': out['v_norm_gains'], 'v_sb_w_qkv': out['v_sb_w_qkv'], 'v_sb_q_gain': out['v_sb_q_gain'], 'v_sb_k_gain': out['v_sb_k_gain'], 'v_sb_w_o': out['v_sb_w_o'], 'v_hg_w_in': out['v_hg_w_in'], 'v_hg_lb_logits': out['v_hg_lb_logits'], 'v_hg_norm_gain': out['v_hg_norm_gain'], 'v_hg_w_o': out['v_hg_w_o'], 'v_mlp_w1': out['v_mlp_w1'], 'v_mlp_w2': out['v_mlp_w2']}


def _loss(weights, diff, rest, loss_target):
    with _jax.named_scope("forward"):
        args = {**rest, TWIN_DIFF_INPUT: diff, **{k: w.astype(_WEIGHT_DTYPES[k]) for k, w in weights.items()}}
        y = _forward(args)
    with _jax.named_scope("loss_head"):
        err = _jnp.square(y.astype(_jnp.float32) - loss_target)
        return 0.5 * _jnp.sum(_jnp.mean(err, axis=-1)) if err.ndim else 0.5 * err


def _adamw(w, g, m, v):
    m = ADAM_B1 * m + (1.0 - ADAM_B1) * g
    v = ADAM_B2 * v + (1.0 - ADAM_B2) * _jnp.square(g)
    m_hat = m / (1.0 - ADAM_B1 ** ADAM_STEP)
    v_hat = v / (1.0 - ADAM_B2 ** ADAM_STEP)
    delta = -ADAM_LR * (m_hat / (_jnp.sqrt(v_hat) + ADAM_EPS) + ADAM_WD * w)
    return delta, m, v


def reference(x, norm_gains, sb_w_qkv, sb_q_gain, sb_k_gain, sb_w_o, hg_w_in, hg_lb_logits, hg_norm_gain, hg_w_o, mlp_w1, mlp_w2, loss_target, m_norm_gains, m_sb_w_qkv, m_sb_q_gain, m_sb_k_gain, m_sb_w_o, m_hg_w_in, m_hg_lb_logits, m_hg_norm_gain, m_hg_w_o, m_mlp_w1, m_mlp_w2, v_norm_gains, v_sb_w_qkv, v_sb_q_gain, v_sb_k_gain, v_sb_w_o, v_hg_w_in, v_hg_lb_logits, v_hg_norm_gain, v_hg_w_o, v_mlp_w1, v_mlp_w2):
    given = dict(x=x, norm_gains=norm_gains, sb_w_qkv=sb_w_qkv, sb_q_gain=sb_q_gain, sb_k_gain=sb_k_gain, sb_w_o=sb_w_o, hg_w_in=hg_w_in, hg_lb_logits=hg_lb_logits, hg_norm_gain=hg_norm_gain, hg_w_o=hg_w_o, mlp_w1=mlp_w1, mlp_w2=mlp_w2, loss_target=loss_target, m_norm_gains=m_norm_gains, m_sb_w_qkv=m_sb_w_qkv, m_sb_q_gain=m_sb_q_gain, m_sb_k_gain=m_sb_k_gain, m_sb_w_o=m_sb_w_o, m_hg_w_in=m_hg_w_in, m_hg_lb_logits=m_hg_lb_logits, m_hg_norm_gain=m_hg_norm_gain, m_hg_w_o=m_hg_w_o, m_mlp_w1=m_mlp_w1, m_mlp_w2=m_mlp_w2, v_norm_gains=v_norm_gains, v_sb_w_qkv=v_sb_w_qkv, v_sb_q_gain=v_sb_q_gain, v_sb_k_gain=v_sb_k_gain, v_sb_w_o=v_sb_w_o, v_hg_w_in=v_hg_w_in, v_hg_lb_logits=v_hg_lb_logits, v_hg_norm_gain=v_hg_norm_gain, v_hg_w_o=v_hg_w_o, v_mlp_w1=v_mlp_w1, v_mlp_w2=v_mlp_w2)
    weights = {n: given[n] for n in TWIN_WEIGHTS}
    shared = {n: given[n] for n in SHARED_INPUTS}
    per_example = {n: given[n] for n in ['x']}
    grad_fn = _jax.value_and_grad(_loss, argnums=(0, 1))

    def one_microbatch(ex, loss_target):
        ex = dict(ex)
        diff = ex.pop(TWIN_DIFF_INPUT)
        return grad_fn(weights, diff, {**shared, **ex}, loss_target)

    if N_MICROBATCH == 1:
        loss, (grad_w, grad_x) = one_microbatch(per_example, given["loss_target"])
    else:
        def body(carry, xs):
            loss_sum, grad_sum = carry
            l_k, (gw_k, gx_k) = one_microbatch(xs[0], xs[1])
            with _jax.named_scope("update"):
                return (loss_sum + l_k, _jax.tree.map(_jnp.add, grad_sum, gw_k)), gx_k

        init = (_jnp.zeros((), _jnp.float32), _jax.tree.map(_jnp.zeros_like, weights))
        (loss, grad_w), grad_x = _jax.lax.scan(body, init, (per_example, given["loss_target"]))
    with _jax.named_scope("update"):
        delta_w, new_m, new_v = {}, {}, {}
        for n in TWIN_WEIGHTS:
            delta_w[n], new_m[n], new_v[n] = _adamw(weights[n], grad_w[n], given["m_" + n], given["v_" + n])
    return (loss, grad_x, *[grad_w[n] for n in TWIN_WEIGHTS], *[delta_w[n] for n in TWIN_WEIGHTS],
            *[new_m[n] for n in TWIN_WEIGHTS], *[new_v[n] for n in TWIN_WEIGHTS])
```

```python
import functools
import math

import numpy as np
import jax
import jax.numpy as jnp
from jax import lax
from jax.experimental import pallas as pl
from jax.experimental.pallas import tpu as pltpu

F32 = jnp.float32
BF16 = jnp.bfloat16

NORM_EPS = 1e-6
SB_HEAD_DIM = 64
HG_HEAD_DIM = 128
HG_CHUNK = 64
LANES = 128
VMEM_LIMIT_BYTES = 56 * 2 ** 20
N_CHIPS = 4
N_DEVICES = 8

ADAM_LR = 0.001
ADAM_B1 = 0.9
ADAM_B2 = 0.999
ADAM_EPS = 1e-08
ADAM_WD = 0.01
ADAM_STEP = 10

MESH = pl.DeviceIdType.MESH
HBM_SPEC = pl.BlockSpec(memory_space=pltpu.HBM)

NN = (((1,), (0,)), ((), ()))
NT = (((1,), (1,)), ((), ()))
TN = (((0,), (0,)), ((), ()))


def _params(sem=None):
    return pltpu.CompilerParams(dimension_semantics=sem, vmem_limit_bytes=VMEM_LIMIT_BYTES)


def _pick(dim, pref):
    for t in (1024, 768, 512, 384, 256, 128, 64, 32, 16, 8):
        if t <= pref and dim % t == 0:
            return t
    return dim


def _dot(a, b, dims=NN):
    return lax.dot_general(a, b, dims, preferred_element_type=F32)


def _sigmoid(x):
    e = jnp.exp(-jnp.abs(x))
    return jnp.where(x >= 0, 1.0, e) / (1.0 + e)


def _matmul(a, b, *, mode, grid, a_block, a_map, b_block, b_map, o_block, o_map, out_shape, out_dtype, name,
            a_fn=None, epi_fn=None, epi_args=()):
    nk = grid[2]
    dims = {"nn": NN, "nt": NT, "tn": TN}[mode]
    n_epi = len(epi_args)

    def body(a_ref, b_ref, *rest):
        epi_refs = rest[:n_epi]
        o_ref = rest[n_epi]
        acc_ref = rest[n_epi + 1]
        kk = pl.program_id(2)

        @pl.when(kk == 0)
        def _():
            acc_ref[...] = jnp.zeros_like(acc_ref)

        av = a_ref[...]
        if a_fn is not None:
            av = a_fn(av)
        acc_ref[...] += _dot(av.astype(BF16), b_ref[...].astype(BF16), dims)

        @pl.when(kk == nk - 1)
        def _():
            r = acc_ref[...]
            if epi_fn is not None:
                r = epi_fn(r, *[e[...] for e in epi_refs])
            o_ref[...] = r.astype(o_ref.dtype)

    acc_shape = tuple(d for d in o_block if d is not None)
    in_specs = [pl.BlockSpec(a_block, a_map), pl.BlockSpec(b_block, b_map)]
    in_specs += [pl.BlockSpec(o_block, o_map) for _ in epi_args]
    return pl.pallas_call(
        body, grid=grid, in_specs=in_specs, out_specs=pl.BlockSpec(o_block, o_map),
        out_shape=jax.ShapeDtypeStruct(out_shape, out_dtype),
        scratch_shapes=[pltpu.VMEM(acc_shape, F32)],
        compiler_params=_params(("parallel", "parallel", "arbitrary")), name=name,
    )(a, b, *epi_args)


def _relu2(u):
    r = jnp.maximum(u, 0.0)
    return r * r


def _add(r, res):
    return r + res


def _mm_fwd_cols(a, wg, layer, *, name):
    s, k = a.shape
    ncs = wg.shape[3]
    tm, tk, tn = _pick(s, 1024), _pick(k, 1024), _pick(ncs, 1024)
    npb = ncs // tn
    return _matmul(a, wg, mode="nn", grid=(s // tm, N_CHIPS * npb, k // tk),
                   a_block=(tm, tk), a_map=lambda i, j, kk: (i, kk),
                   b_block=(None, None, tk, tn), b_map=lambda i, j, kk: (j // npb, layer, kk, j % npb),
                   o_block=(tm, tn), o_map=lambda i, j, kk: (i, j),
                   out_shape=(s, N_CHIPS * ncs), out_dtype=F32, name=name)


def _mm_fwd_rows(a, wg, layer, *, residual, name, a_fn=None):
    s = a.shape[0]
    krs, n = wg.shape[2], wg.shape[3]
    tm, tk, tn = _pick(s, 1024), _pick(krs, 1024), _pick(n, 1024)
    kpb = krs // tk
    return _matmul(a, wg, mode="nn", grid=(s // tm, n // tn, N_CHIPS * kpb),
                   a_block=(tm, tk), a_map=lambda i, j, kk: (i, kk),
                   b_block=(None, None, tk, tn), b_map=lambda i, j, kk: (kk // kpb, layer, kk % kpb, j),
                   o_block=(tm, tn), o_map=lambda i, j, kk: (i, j),
                   out_shape=(s, n), out_dtype=F32, name=name, a_fn=a_fn, epi_fn=_add, epi_args=(residual,))


def _mm_bwd_cols(dy, wg, layer, *, name, out_dtype=F32):
    s = dy.shape[0]
    kw, ncs = wg.shape[2], wg.shape[3]
    tm, tn, tk = _pick(s, 1024), _pick(kw, 1024), _pick(ncs, 1024)
    kpb = ncs // tk
    return _matmul(dy, wg, mode="nt", grid=(s // tm, kw // tn, N_CHIPS * kpb),
                   a_block=(tm, tk), a_map=lambda i, j, kk: (i, kk),
                   b_block=(None, None, tn, tk), b_map=lambda i, j, kk: (kk // kpb, layer, j, kk % kpb),
                   o_block=(tm, tn), o_map=lambda i, j, kk: (i, j),
                   out_shape=(s, kw), out_dtype=out_dtype, name=name)


def _mm_bwd_rows(dy, wg, layer, *, name, out_dtype=F32, epi_fn=None, epi_args=()):
    s, n = dy.shape
    krs = wg.shape[2]
    tm, tn, tk = _pick(s, 1024), _pick(krs, 1024), _pick(n, 1024)
    npb = krs // tn
    return _matmul(dy, wg, mode="nt", grid=(s // tm, N_CHIPS * npb, n // tk),
                   a_block=(tm, tk), a_map=lambda i, j, kk: (i, kk),
                   b_block=(None, None, tn, tk), b_map=lambda i, j, kk: (j // npb, layer, j % npb, kk),
                   o_block=(tm, tn), o_map=lambda i, j, kk: (i, j),
                   out_shape=(s, N_CHIPS * krs), out_dtype=out_dtype, name=name, epi_fn=epi_fn, epi_args=epi_args)


def _mm_dw_cols(xa, dy, *, name):
    s, kx = xa.shape
    ncs = dy.shape[1] // N_CHIPS
    tm, tn, tk = _pick(kx, 1024), _pick(ncs, 1024), _pick(s, 1024)
    npb = ncs // tn
    return _matmul(xa, dy, mode="tn", grid=(kx // tm, N_CHIPS * npb, s // tk),
                   a_block=(tk, tm), a_map=lambda i, j, kk: (kk, i),
                   b_block=(tk, tn), b_map=lambda i, j, kk: (kk, j),
                   o_block=(None, tm, tn), o_map=lambda i, j, kk: (j // npb, i, j % npb),
                   out_shape=(N_CHIPS, kx, ncs), out_dtype=F32, name=name)


def _mm_dw_rows(xa, dy, *, name, a_fn=None):
    s, n = dy.shape
    krs = xa.shape[1] // N_CHIPS
    tm, tn, tk = _pick(krs, 1024), _pick(n, 1024), _pick(s, 1024)
    mpb = krs // tm
    return _matmul(xa, dy, mode="tn", grid=(N_CHIPS * mpb, n // tn, s // tk),
                   a_block=(tk, tm), a_map=lambda i, j, kk: (kk, i),
                   b_block=(tk, tn), b_map=lambda i, j, kk: (kk, j),
                   o_block=(None, tm, tn), o_map=lambda i, j, kk: (i // mpb, i % mpb, j),
                   out_shape=(N_CHIPS, krs, n), out_dtype=F32, name=name, a_fn=a_fn)


def _rmsnorm_fwd(x, gain_row):
    s, d = x.shape
    ts = _pick(s, 512)

    def body(x_ref, g_ref, h_ref):
        xv = x_ref[...]
        r = lax.rsqrt(jnp.mean(xv * xv, axis=-1, keepdims=True) + NORM_EPS)
        h_ref[...] = (xv * r * g_ref[...]).astype(h_ref.dtype)

    return pl.pallas_call(
        body, grid=(s // ts,),
        in_specs=[pl.BlockSpec((ts, d), lambda i: (i, 0)), pl.BlockSpec((1, d), lambda i: (0, 0))],
        out_specs=pl.BlockSpec((ts, d), lambda i: (i, 0)),
        out_shape=jax.ShapeDtypeStruct((s, d), BF16),
        compiler_params=_params(("parallel",)), name="rmsnorm_fwd",
    )(x, gain_row)


def _rmsnorm_bwd(dh, x, gain_row, dx_res):
    s, d = x.shape
    ts = _pick(s, 512)

    def body(dh_ref, x_ref, g_ref, res_ref, dx_ref, dg_ref):
        i = pl.program_id(0)
        xv = x_ref[...]
        r = lax.rsqrt(jnp.mean(xv * xv, axis=-1, keepdims=True) + NORM_EPS)
        xhat = xv * r
        dh_v = dh_ref[...]
        dxhat = dh_v * g_ref[...]
        dx = r * (dxhat - xhat * jnp.mean(dxhat * xhat, axis=-1, keepdims=True))
        dx_ref[...] = res_ref[...] + dx
        part = jnp.sum(dh_v * xhat, axis=0, keepdims=True)

        @pl.when(i == 0)
        def _():
            dg_ref[...] = part

        @pl.when(i > 0)
        def _():
            dg_ref[...] += part

    return pl.pallas_call(
        body, grid=(s // ts,),
        in_specs=[pl.BlockSpec((ts, d), lambda i: (i, 0)), pl.BlockSpec((ts, d), lambda i: (i, 0)),
                  pl.BlockSpec((1, d), lambda i: (0, 0)), pl.BlockSpec((ts, d), lambda i: (i, 0))],
        out_specs=[pl.BlockSpec((ts, d), lambda i: (i, 0)), pl.BlockSpec((1, d), lambda i: (0, 0))],
        out_shape=[jax.ShapeDtypeStruct((s, d), F32), jax.ShapeDtypeStruct((1, d), F32)],
        compiler_params=_params(("arbitrary",)), name="rmsnorm_bwd",
    )(dh, x, gain_row, dx_res)


def _loss_head(y, target):
    s, d = y.shape
    ts = _pick(s, 512)

    def body(y_ref, t_ref, sq_ref, dy_ref):
        i = pl.program_id(0)
        err = y_ref[...] - t_ref[...]
        dy_ref[...] = err / d
        part = jnp.sum(err * err, axis=0, keepdims=True)

        @pl.when(i == 0)
        def _():
            sq_ref[...] = part

        @pl.when(i > 0)
        def _():
            sq_ref[...] += part

    return pl.pallas_call(
        body, grid=(s // ts,),
        in_specs=[pl.BlockSpec((ts, d), lambda i: (i, 0)), pl.BlockSpec((ts, d), lambda i: (i, 0))],
        out_specs=[pl.BlockSpec((1, d), lambda i: (0, 0)), pl.BlockSpec((ts, d), lambda i: (i, 0))],
        out_shape=[jax.ShapeDtypeStruct((1, d), F32), jax.ShapeDtypeStruct((s, d), F32)],
        compiler_params=_params(("arbitrary",)), name="loss_head",
    )(y, target)


def _pair_mean(val, low_half):
    s0 = jnp.sum(jnp.where(low_half, val, 0.0), axis=-1, keepdims=True)
    s1 = jnp.sum(jnp.where(low_half, 0.0, val), axis=-1, keepdims=True)
    return jnp.where(low_half, s0, s1) * (1.0 / SB_HEAD_DIM)


def _qk_norm_fwd(qkv, qgain_row, kgain_row):
    s, d3 = qkv.shape
    d = d3 // 3
    ts = _pick(s, 512)
    groups = d // LANES

    def body(q_ref, k_ref, v_ref, qg_ref, kg_ref, qn_ref, kn_ref, vb_ref):
        low_half = lax.broadcasted_iota(jnp.int32, (ts, LANES), 1) < SB_HEAD_DIM
        for src, gain, dst in ((q_ref, qg_ref, qn_ref), (k_ref, kg_ref, kn_ref)):
            for p in range(groups):
                cols = slice(p * LANES, (p + 1) * LANES)
                xp = src[:, cols]
                r = lax.rsqrt(_pair_mean(xp * xp, low_half) + NORM_EPS)
                dst[:, cols] = (xp * r * gain[:, cols]).astype(dst.dtype)
        vb_ref[...] = v_ref[...].astype(vb_ref.dtype)

    tok = lambda c: pl.BlockSpec((ts, d), lambda i: (i, c))
    row = pl.BlockSpec((1, d), lambda i: (0, 0))
    return pl.pallas_call(
        body, grid=(s // ts,),
        in_specs=[tok(0), tok(1), tok(2), row, row],
        out_specs=[tok(0), tok(0), tok(0)],
        out_shape=[jax.ShapeDtypeStruct((s, d), BF16)] * 3,
        compiler_params=_params(("parallel",)), name="qk_norm_fwd",
    )(qkv, qkv, qkv, qgain_row, kgain_row)


def _qk_norm_bwd(qkv, qgain_row, kgain_row, dqn, dkn, dv):
    s, d3 = qkv.shape
    d = d3 // 3
    ts = _pick(s, 512)
    groups = d // LANES

    def body(q_ref, k_ref, qg_ref, kg_ref, dqn_ref, dkn_ref, dv_ref, dqkv_ref, dqg_ref, dkg_ref):
        i = pl.program_id(0)
        low_half = lax.broadcasted_iota(jnp.int32, (ts, LANES), 1) < SB_HEAD_DIM
        for which, (src, gain, dsrc, dgain) in enumerate(((q_ref, qg_ref, dqn_ref, dqg_ref),
                                                          (k_ref, kg_ref, dkn_ref, dkg_ref))):
            for p in range(groups):
                cols = slice(p * LANES, (p + 1) * LANES)
                xp = src[:, cols]
                r = lax.rsqrt(_pair_mean(xp * xp, low_half) + NORM_EPS)
                xhat = xp * r
                dy = dsrc[:, cols]
                dxhat = dy * gain[:, cols]
                dx = r * (dxhat - xhat * _pair_mean(dxhat * xhat, low_half))
                dqkv_ref[:, which * d + p * LANES: which * d + (p + 1) * LANES] = dx.astype(dqkv_ref.dtype)
                part = jnp.sum(dy * xhat, axis=0, keepdims=True)

                @pl.when(i == 0)
                def _():
                    dgain[:, cols] = part

                @pl.when(i > 0)
                def _():
                    dgain[:, cols] += part
        dqkv_ref[:, 2 * d:] = dv_ref[...].astype(dqkv_ref.dtype)

    tok = lambda c: pl.BlockSpec((ts, d), lambda i: (i, c))
    row = pl.BlockSpec((1, d), lambda i: (0, 0))
    return pl.pallas_call(
        body, grid=(s // ts,),
        in_specs=[tok(0), tok(1), row, row, tok(0), tok(0), tok(0)],
        out_specs=[pl.BlockSpec((ts, d3), lambda i: (i, 0)), row, row],
        out_shape=[jax.ShapeDtypeStruct((s, d3), BF16), jax.ShapeDtypeStruct((1, d), F32),
                   jax.ShapeDtypeStruct((1, d), F32)],
        compiler_params=_params(("arbitrary",)), name="qk_norm_bwd",
    )(qkv, qkv, qgain_row, kgain_row, dqn, dkn, dv)


def _split2(x):
    hi = x.astype(BF16)
    lo = (x - hi.astype(F32)).astype(BF16)
    return hi, lo


def _keep_lanes(x, keep):
    return jnp.where(keep, x.astype(F32), 0.0).astype(BF16)


def _sb_tile(qh, kblk, scale, valid, tri_ge, run):
    z = _dot(qh, kblk, NT) * scale
    e = jnp.exp(-jnp.abs(z))
    lstay = jnp.where(valid, -(jnp.maximum(z, 0.0) + jnp.log(1.0 + e)), 0.0)
    hi, lo = _split2(lstay)
    cin = _dot(hi, tri_ge) + _dot(lo, tri_ge)
    w = jnp.where(valid, jnp.exp(z + cin + run), 0.0)
    return z, e, w, cin


def _sb_consts(t):
    j = np.arange(t)
    tri_ge = (j[:, None] >= j[None, :]).astype(np.float32)
    tri_le = (j[:, None] <= j[None, :]).astype(np.float32)
    return jnp.asarray(tri_ge, BF16), jnp.asarray(tri_le, BF16)


def _sb_attn_fwd(qn, kn, vb):
    s, d = qn.shape
    t = _pick(s, 128)
    nq = s // t
    npairs = d // LANES
    scale = 1.0 / math.sqrt(SB_HEAD_DIM)
    tri_ge, _ = _sb_consts(t)

    def body(q_ref, k_ref, v_ref, tri_ref, o_ref):
        qi = pl.program_id(1)
        lane = lax.broadcasted_iota(jnp.int32, (t, LANES), 1)
        strict = lax.broadcasted_iota(jnp.int32, (t, t), 1) < lax.broadcasted_iota(jnp.int32, (t, t), 0)
        tri = tri_ref[...]
        qv = q_ref[...]
        out = jnp.zeros((t, LANES), F32)
        for h in range(2):
            in_head = (lane >= h * SB_HEAD_DIM) & (lane < (h + 1) * SB_HEAD_DIM)
            qh = _keep_lanes(qv, in_head)

            def step(it, carry, qh=qh):
                acc, run = carry
                kb = qi - it
                koff = pl.multiple_of(kb * t, t)
                valid = strict | (kb != qi)
                _, _, w, cin = _sb_tile(qh, k_ref[pl.ds(koff, t), :], scale, valid, tri, run)
                acc = acc + _dot(w.astype(BF16), v_ref[pl.ds(koff, t), :])
                return acc, run + cin[:, 0:1]

            acc, _ = lax.fori_loop(0, qi + 1, step, (jnp.zeros((t, LANES), F32), jnp.zeros((t, 1), F32)))
            out = jnp.where(in_head, acc, out)
        o_ref[...] = out

    blk = pl.BlockSpec((t, LANES), lambda p, i: (i, p))
    full = pl.BlockSpec((s, LANES), lambda p, i: (0, p))
    return pl.pallas_call(
        body, grid=(npairs, nq),
        in_specs=[blk, full, full, pl.BlockSpec((t, t), lambda p, i: (0, 0))],
        out_specs=blk, out_shape=jax.ShapeDtypeStruct((s, d), F32),
        compiler_params=_params(("parallel", "arbitrary")), name="sb_attn_fwd",
    )(qn, kn, vb, tri_ge)


def _sb_attn_bwd(qn, kn, vb, do):
    s, d = qn.shape
    t = _pick(s, 128)
    nq = s // t
    npairs = d // LANES
    scale = 1.0 / math.sqrt(SB_HEAD_DIM)
    tri_ge, tri_le = _sb_consts(t)

    def body(q_ref, k_ref, v_ref, do_ref, tge_ref, tle_ref, dq_ref, dk_ref, dv_ref, g_cache, s_cache):
        qi = pl.program_id(1)

        @pl.when(qi == 0)
        def _():
            dk_ref[...] = jnp.zeros_like(dk_ref)
            dv_ref[...] = jnp.zeros_like(dv_ref)

        lane = lax.broadcasted_iota(jnp.int32, (t, LANES), 1)
        strict = lax.broadcasted_iota(jnp.int32, (t, t), 1) < lax.broadcasted_iota(jnp.int32, (t, t), 0)
        tge = tge_ref[...]
        tle = tle_ref[...]
        qv = q_ref[...]
        dov = do_ref[...]
        dq = jnp.zeros((t, LANES), F32)
        for h in range(2):
            in_head = (lane >= h * SB_HEAD_DIM) & (lane < (h + 1) * SB_HEAD_DIM)
            qh = _keep_lanes(qv, in_head)
            doh = _keep_lanes(dov, in_head)

            def pass_a(it, run, qh=qh, doh=doh):
                kb = qi - it
                koff = pl.multiple_of(kb * t, t)
                valid = strict | (kb != qi)
                vblk = v_ref[pl.ds(koff, t), :]
                z, e, w, cin = _sb_tile(qh, k_ref[pl.ds(koff, t), :], scale, valid, tge, run)
                g_cache[kb] = w * _dot(doh, vblk, NT)
                s_cache[kb] = jnp.where(z >= 0, 1.0, e) / (1.0 + e)
                dv_ref[pl.ds(koff, t), :] += _dot(w.astype(BF16), doh, TN)
                return run + cin[:, 0:1]

            lax.fori_loop(0, qi + 1, pass_a, jnp.zeros((t, 1), F32))

            def pass_b(kb, carry, qh=qh):
                dq_acc, run = carry
                koff = pl.multiple_of(kb * t, t)
                valid = strict | (kb != qi)
                g = g_cache[kb]
                hi, lo = _split2(g)
                pin = _dot(hi, tle) + _dot(lo, tle) + run
                dz = jnp.where(valid, (g - s_cache[kb] * pin) * scale, 0.0).astype(BF16)
                dq_acc = dq_acc + _dot(dz, k_ref[pl.ds(koff, t), :])
                dk_ref[pl.ds(koff, t), :] += _dot(dz, qh, TN)
                return dq_acc, pin[:, t - 1:t]

            dq_h, _ = lax.fori_loop(0, qi + 1, pass_b, (jnp.zeros((t, LANES), F32), jnp.zeros((t, 1), F32)))
            dq = jnp.where(in_head, dq_h, dq)
        dq_ref[...] = dq

    blk = pl.BlockSpec((t, LANES), lambda p, i: (i, p))
    full = pl.BlockSpec((s, LANES), lambda p, i: (0, p))
    tri = pl.BlockSpec((t, t), lambda p, i: (0, 0))
    return pl.pallas_call(
        body, grid=(npairs, nq),
        in_specs=[blk, full, full, blk, tri, tri],
        out_specs=[blk, full, full],
        out_shape=[jax.ShapeDtypeStruct((s, d), F32)] * 3,
        scratch_shapes=[pltpu.VMEM((nq, t, t), F32), pltpu.VMEM((nq, t, t), F32)],
        compiler_params=_params(("parallel", "arbitrary")), name="sb_attn_bwd",
    )(qn, kn, vb, do, tri_ge, tri_le)


def _hg_consts(c):
    levels = []
    h = c // 2
    while h >= 1:
        levels.append(h)
        h //= 2
    t = np.arange(c)
    j = t[None, :]
    rows, masks = [], []
    for h in levels:
        blk = t // (2 * h)
        mid = blk * 2 * h + h - 1
        second = (t % (2 * h)) >= h
        rows.append(second[:, None] & (j > mid[:, None]) & (j <= t[:, None]))
        rows.append((~second)[:, None] & (j > t[:, None]) & (j <= mid[:, None]))
        masks.append((blk[:, None] == blk[None, :]) & second[:, None] & (~second)[None, :])
    rows.append(j <= t[:, None])
    rows.append(j > t[:, None])
    masks.append(t[:, None] == t[None, :])
    m_all = np.concatenate(rows, axis=0).astype(np.float32)
    mask_all = np.stack(masks, axis=0).astype(np.float32)
    suffix = (t[None, :] >= t[:, None]).astype(np.float32)
    return len(levels), jnp.asarray(m_all, BF16), jnp.asarray(mask_all, F32), jnp.asarray(suffix, BF16)


def _split3(x):
    hi = x.astype(BF16)
    r1 = x - hi.astype(F32)
    mid = r1.astype(BF16)
    lo = (r1 - mid.astype(F32)).astype(BF16)
    return jnp.concatenate([hi, mid, lo], axis=1)


def _join3(e):
    n = e.shape[1] // 3
    return e[:, :n] + e[:, n:2 * n] + e[:, 2 * n:]


def _hg_gates(qr, fr, lb):
    sq = _sigmoid(qr)
    sf = _sigmoid(fr)
    forget = lb + (1.0 - lb) * sf
    return qr * sq, sq, sf, forget, jnp.log(forget), 1.0 - forget


def _hg_scores(q, k, expo, masks, nlev, c):
    qb, kb = q.astype(BF16), k.astype(BF16)
    a = masks[nlev] * _dot(qb, kb, NT)
    scaled = []
    for li in range(nlev):
        fq = jnp.exp(expo[(2 * li) * c:(2 * li + 1) * c])
        fk = jnp.exp(expo[(2 * li + 1) * c:(2 * li + 2) * c])
        qs, ks = (q * fq).astype(BF16), (k * fk).astype(BF16)
        a = a + masks[li] * _dot(qs, ks, NT)
        scaled.append((qs, ks, fq, fk))
    return a, scaled, qb, kb


def _hg_fwd(proj, lb_row, gain_row):
    s, d4 = proj.shape
    d = d4 // 4
    nh = d // HG_HEAD_DIM
    c = min(HG_CHUNK, s)
    tb = _pick(s, 512)
    ncb = tb // c
    nlev, m_all, mask_all, _ = _hg_consts(c)
    nrow = m_all.shape[0]

    def body(q_ref, f_ref, i_ref, g_ref, lb_ref, gain_ref, mall_ref, mask_ref, y_ref, o_ref, st_out_ref, st_ref):
        b = pl.program_id(1)

        @pl.when(b == 0)
        def _():
            st_ref[...] = jnp.zeros_like(st_ref)

        lb = lb_ref[...]
        gain = gain_ref[...]

        def chunk(ci, carry):
            off = pl.multiple_of(ci * c, c)
            rows = pl.ds(off, c)
            q, _, _, _, lf, k = _hg_gates(q_ref[rows, :], f_ref[rows, :], lb)
            v = i_ref[rows, :].astype(BF16)
            expo = _join3(_dot(mall_ref[...], _split3(lf)))
            masks = mask_ref[...]
            st = st_ref[...]
            st_out_ref[ci] = st
            a, _, _, _ = _hg_scores(q, k, expo, masks, nlev, c)
            b_cum = expo[2 * nlev * c:(2 * nlev + 1) * c]
            e_tail = expo[(2 * nlev + 1) * c:(2 * nlev + 2) * c]
            q_in = (q * jnp.exp(b_cum)).astype(BF16)
            o = _dot(q_in, st.astype(BF16), NT) + _dot(a.astype(BF16), v)
            k_dec = (k * jnp.exp(e_tail)).astype(BF16)
            st_ref[...] = st * jnp.exp(b_cum[c - 1:c, :]) + _dot(v, k_dec, TN)
            o_ref[rows, :] = o
            r = lax.rsqrt(jnp.mean(o * o, axis=-1, keepdims=True) + NORM_EPS)
            y_ref[rows, :] = (o * r * gain * _sigmoid(g_ref[rows, :])).astype(y_ref.dtype)
            return carry

        lax.fori_loop(0, ncb, chunk, 0)

    part = lambda k: pl.BlockSpec((tb, HG_HEAD_DIM), lambda h, b: (b, k * nh + h))
    head_row = pl.BlockSpec((1, HG_HEAD_DIM), lambda h, b: (0, h))
    tok = pl.BlockSpec((tb, HG_HEAD_DIM), lambda h, b: (b, h))
    return pl.pallas_call(
        body, grid=(nh, s // tb),
        in_specs=[part(0), part(1), part(2), part(3), head_row,
                  pl.BlockSpec((1, HG_HEAD_DIM), lambda h, b: (0, 0)),
                  pl.BlockSpec((nrow, c), lambda h, b: (0, 0)),
                  pl.BlockSpec((nlev + 1, c, c), lambda h, b: (0, 0, 0))],
        out_specs=[tok, tok, pl.BlockSpec((ncb, None, HG_HEAD_DIM, HG_HEAD_DIM), lambda h, b: (b, h, 0, 0))],
        out_shape=[jax.ShapeDtypeStruct((s, d), BF16), jax.ShapeDtypeStruct((s, d), F32),
                   jax.ShapeDtypeStruct((s // c, nh, HG_HEAD_DIM, HG_HEAD_DIM), F32)],
        scratch_shapes=[pltpu.VMEM((HG_HEAD_DIM, HG_HEAD_DIM), F32)],
        compiler_params=_params(("parallel", "arbitrary")), name="hg_fwd",
    )(proj, proj, proj, proj, lb_row, gain_row, m_all, mask_all)


def _hg_bwd(proj, lb_row, gain_row, o_saved, states, dy):
    s, d4 = proj.shape
    d = d4 // 4
    nh = d // HG_HEAD_DIM
    c = min(HG_CHUNK, s)
    tb = _pick(s, 512)
    ncb = tb // c
    nb = s // tb
    nlev, m_all, mask_all, suffix = _hg_consts(c)
    nrow = m_all.shape[0]

    def body(q_ref, f_ref, i_ref, g_ref, lb_ref, gain_ref, o_ref, st_in_ref, dy_ref, mall_ref, mask_ref, suf_ref,
             dq_ref, df_ref, di_ref, dg_ref, dlb_ref, dgain_ref, dst_ref, run_ref):
        b = pl.program_id(1)

        @pl.when(b == 0)
        def _():
            dst_ref[...] = jnp.zeros_like(dst_ref)
            run_ref[...] = jnp.zeros_like(run_ref)
            dlb_ref[...] = jnp.zeros_like(dlb_ref)
            dgain_ref[...] = jnp.zeros_like(dgain_ref)

        lb = lb_ref[...]
        gain = gain_ref[...]

        def chunk(it, carry):
            ci = ncb - 1 - it
            off = pl.multiple_of(ci * c, c)
            rows = pl.ds(off, c)
            qr, fr = q_ref[rows, :], f_ref[rows, :]
            q, sq, sf, forget, lf, k = _hg_gates(qr, fr, lb)
            v = i_ref[rows, :].astype(BF16)
            expo = _join3(_dot(mall_ref[...], _split3(lf)))
            masks = mask_ref[...]
            o = o_ref[rows, :]
            dyv = dy_ref[rows, :]
            sg = _sigmoid(g_ref[rows, :])
            r = lax.rsqrt(jnp.mean(o * o, axis=-1, keepdims=True) + NORM_EPS)
            ohat = o * r
            dyn = dyv * sg
            dg_ref[rows, :] = (dyv * ohat * gain * sg * (1.0 - sg)).astype(dg_ref.dtype)
            dgain_ref[...] += jnp.sum(dyn * ohat, axis=0, keepdims=True)
            dohat = dyn * gain
            do = (r * (dohat - ohat * jnp.mean(dohat * ohat, axis=-1, keepdims=True))).astype(BF16)
            dst = dst_ref[...]
            dstb = dst.astype(BF16)
            a, scaled, qb, kb = _hg_scores(q, k, expo, masks, nlev, c)
            f_cum = jnp.exp(expo[2 * nlev * c:(2 * nlev + 1) * c])
            f_tail = jnp.exp(expo[(2 * nlev + 1) * c:(2 * nlev + 2) * c])
            q_in = (q * f_cum).astype(BF16)
            k_dec = (k * f_tail).astype(BF16)
            t_in = _join3(_dot(do, _split3(st_in_ref[ci])))
            t_st = _join3(_dot(v, _split3(dst)))
            da = _dot(do, v, NT)
            dam = (masks[nlev] * da).astype(BF16)
            dq = t_in * f_cum + _dot(dam, kb)
            dk = t_st * f_tail + _dot(dam, qb, TN)
            db = q_in.astype(F32) * t_in - k_dec.astype(F32) * t_st
            for li in range(nlev):
                qs, ks, fq, fk = scaled[li]
                dam = (masks[li] * da).astype(BF16)
                t_q = _dot(dam, ks)
                t_k = _dot(dam, qs, TN)
                dq = dq + t_q * fq
                dk = dk + t_k * fk
                db = db + (qs.astype(F32) * t_q - ks.astype(F32) * t_k)
            dv = _dot(a.astype(BF16), do, TN) + _dot(k_dec, dstb, NT)
            dst_ref[...] = dst * f_cum[c - 1:c, :] + _dot(do, q_in, TN)
            dlf = _join3(_dot(suf_ref[...], _split3(db))) + run_ref[...]
            run_ref[...] = dlf[0:1, :]
            dforget = dlf / forget - dk
            dlb_ref[...] += jnp.sum(dforget * (1.0 - sf), axis=0, keepdims=True)
            df_ref[rows, :] = (dforget * (1.0 - lb) * sf * (1.0 - sf)).astype(df_ref.dtype)
            dq_ref[rows, :] = (dq * sq * (1.0 + qr * (1.0 - sq))).astype(dq_ref.dtype)
            di_ref[rows, :] = dv.astype(di_ref.dtype)
            return carry

        lax.fori_loop(0, ncb, chunk, 0)

    part = lambda k: pl.BlockSpec((tb, HG_HEAD_DIM), lambda h, b: (nb - 1 - b, k * nh + h))
    head_row = pl.BlockSpec((1, HG_HEAD_DIM), lambda h, b: (0, h))
    tok = pl.BlockSpec((tb, HG_HEAD_DIM), lambda h, b: (nb - 1 - b, h))
    const2 = lambda shape: pl.BlockSpec(shape, lambda h, b: (0, 0))
    return pl.pallas_call(
        body, grid=(nh, nb),
        in_specs=[part(0), part(1), part(2), part(3), head_row, const2((1, HG_HEAD_DIM)), tok,
                  pl.BlockSpec((ncb, None, HG_HEAD_DIM, HG_HEAD_DIM), lambda h, b: (nb - 1 - b, h, 0, 0)),
                  tok, const2((nrow, c)), pl.BlockSpec((nlev + 1, c, c), lambda h, b: (0, 0, 0)), const2((c, c))],
        out_specs=[tok, tok, tok, tok, head_row, head_row],
        out_shape=[jax.ShapeDtypeStruct((s, d), BF16)] * 4 + [jax.ShapeDtypeStruct((1, d), F32)] * 2,
        scratch_shapes=[pltpu.VMEM((HG_HEAD_DIM, HG_HEAD_DIM), F32), pltpu.VMEM((1, HG_HEAD_DIM), F32)],
        compiler_params=_params(("parallel", "arbitrary")), name="hg_bwd",
    )(proj, proj, proj, proj, lb_row, gain_row, o_saved, states, dy, m_all, mask_all, suffix)


def _lb_fwd(logits):
    n, d = logits.shape

    def body(l_ref, lb_ref, p_ref):
        rows = [l_ref[i:i + 1, :] for i in range(n)]
        m = functools.reduce(jnp.maximum, rows)
        es = [jnp.exp(r - m) for r in rows]
        tot = functools.reduce(lambda a, b: a + b, es)
        ps = [e / tot for e in es]
        run = jnp.zeros_like(ps[0])
        for i in range(n):
            run = run + ps[i]
            lb_ref[i:i + 1, :] = run - ps[0]
            p_ref[i:i + 1, :] = ps[i]

    return pl.pallas_call(
        body, out_shape=[jax.ShapeDtypeStruct((n, d), F32)] * 2, name="lb_fwd",
    )(logits)


def _lb_bwd(p, dlb):
    n, d = p.shape

    def body(p_ref, dlb_ref, dl_ref):
        ps = [p_ref[i:i + 1, :] for i in range(n)]
        ds = [dlb_ref[i:i + 1, :] for i in range(n)]
        total = functools.reduce(lambda a, b: a + b, ds)
        dps = []
        for i in range(n):
            dp = functools.reduce(lambda a, b: a + b, ds[i:])
            dps.append(dp - total if i == 0 else dp)
        inner = functools.reduce(lambda a, b: a + b, [pi * di for pi, di in zip(ps, dps)])
        for i in range(n):
            dl_ref[i:i + 1, :] = ps[i] * (dps[i] - inner)

    return pl.pallas_call(body, out_shape=jax.ShapeDtypeStruct((n, d), F32), name="lb_bwd")(p, dlb)


def _as2d(a):
    return a.reshape(-1, a.shape[-1])


def _adamw(w, m, v, grads):
    shape = w.shape
    w2, m2, v2 = _as2d(w), _as2d(m), _as2d(v)
    g2 = [_as2d(g) for g in grads]
    rows, cols = w2.shape
    tr = _pick(rows, 512)
    ng = len(g2)
    bc1 = 1.0 - ADAM_B1 ** ADAM_STEP
    bc2 = 1.0 - ADAM_B2 ** ADAM_STEP

    def body(w_ref, m_ref, v_ref, *rest):
        g = rest[0][...]
        for extra in rest[1:ng]:
            g = g + extra[...]
        g_out, d_out, m_out, v_out = rest[ng:]
        mn = ADAM_B1 * m_ref[...] + (1.0 - ADAM_B1) * g
        vn = ADAM_B2 * v_ref[...] + (1.0 - ADAM_B2) * (g * g)
        m_hat = mn / bc1
        v_hat = vn / bc2
        g_out[...] = g
        d_out[...] = -ADAM_LR * (m_hat / (jnp.sqrt(v_hat) + ADAM_EPS) + ADAM_WD * w_ref[...])
        m_out[...] = mn
        v_out[...] = vn

    spec = pl.BlockSpec((tr, cols), lambda i: (i, 0))
    outs = pl.pallas_call(
        body, grid=(rows // tr,), in_specs=[spec] * (3 + ng), out_specs=[spec] * 4,
        out_shape=[jax.ShapeDtypeStruct((rows, cols), F32)] * 4,
        compiler_params=_params(("parallel",)), name="adamw",
    )(w2, m2, v2, *g2)
    return tuple(o.reshape(shape) for o in outs)


def _sum_slots(parts, recv, chip):
    _, rows, cols = parts.shape
    tr = _pick(rows, 512)

    def body(chip_ref, own_ref, r0_ref, r1_ref, r2_ref, o_ref):
        o_ref[...] = ((own_ref[...] + r0_ref[...]) + r1_ref[...]) + r2_ref[...]

    grid_spec = pltpu.PrefetchScalarGridSpec(
        num_scalar_prefetch=1, grid=(rows // tr,),
        in_specs=[pl.BlockSpec((None, tr, cols), lambda i, chip_ref: (chip_ref[0], i, 0))]
        + [pl.BlockSpec((None, tr, cols), functools.partial(lambda i, chip_ref, k: (k, i, 0), k=k)) for k in range(3)],
        out_specs=pl.BlockSpec((tr, cols), lambda i, chip_ref: (i, 0)))
    return pl.pallas_call(
        body, grid_spec=grid_spec, out_shape=jax.ShapeDtypeStruct((rows, cols), F32),
        compiler_params=_params(("parallel",)), name="sum_slots",
    )(chip, parts, recv, recv, recv)


def _sum_devices(gathered):
    n, rows, cols = gathered.shape

    def body(g_ref, o_ref):
        acc = g_ref[0]
        for i in range(1, n):
            acc = acc + g_ref[i]
        o_ref[...] = acc

    return pl.pallas_call(body, out_shape=jax.ShapeDtypeStruct((rows, cols), F32), name="sum_devices")(gathered)


def _coords():
    return lax.axis_index("x"), lax.axis_index("y"), lax.axis_index("c")


def _chip_peers(x, y, c):
    out = []
    for fx, fy in ((0, 1), (1, 0), (1, 1)):
        px = 1 - x if fx else x
        py = 1 - y if fy else y
        out.append(((px, py, c), 2 * px + py))
    return out


def _gather_chips(shards):
    n = len(shards)

    def body(*refs):
        ins, outs = refs[:n], refs[n:2 * n]
        send_sems, recv_sems, local_sems = refs[2 * n:]
        x, y, c = _coords()
        me = 2 * x + y
        waits = []
        for t in range(n):
            own = pltpu.make_async_copy(ins[t], outs[t].at[me], local_sems.at[t])
            own.start()
            waits.append(own.wait)
            for k, (peer, peer_chip) in enumerate(_chip_peers(x, y, c)):
                sem = 3 * t + k
                send = pltpu.make_async_remote_copy(
                    src_ref=ins[t], dst_ref=outs[t].at[me], send_sem=send_sems.at[sem], recv_sem=recv_sems.at[sem],
                    device_id=peer, device_id_type=MESH)
                send.start()
                recv = pltpu.make_async_remote_copy(
                    src_ref=ins[t], dst_ref=outs[t].at[peer_chip], send_sem=send_sems.at[sem],
                    recv_sem=recv_sems.at[sem], device_id=peer, device_id_type=MESH)
                waits += [send.wait_send, recv.wait_recv]
        for w in waits:
            w()

    return pl.pallas_call(
        body, in_specs=[HBM_SPEC] * n, out_specs=[HBM_SPEC] * n,
        out_shape=[jax.ShapeDtypeStruct((N_CHIPS,) + s.shape, s.dtype) for s in shards],
        scratch_shapes=[pltpu.SemaphoreType.DMA((3 * n,)), pltpu.SemaphoreType.DMA((3 * n,)),
                        pltpu.SemaphoreType.DMA((n,))],
        name="gather_chips",
    )(*shards)


def _scatter_chips(parts):
    n = len(parts)

    def body(*refs):
        ins, outs = refs[:n], refs[n:2 * n]
        send_sems, recv_sems = refs[2 * n:]
        x, y, c = _coords()
        copies = []
        for t in range(n):
            for k, (peer, peer_chip) in enumerate(_chip_peers(x, y, c)):
                sem = 3 * t + k
                cp = pltpu.make_async_remote_copy(
                    src_ref=ins[t].at[peer_chip], dst_ref=outs[t].at[k], send_sem=send_sems.at[sem],
                    recv_sem=recv_sems.at[sem], device_id=peer, device_id_type=MESH)
                cp.start()
                copies.append(cp)
        for cp in copies:
            cp.wait()

    return pl.pallas_call(
        body, in_specs=[HBM_SPEC] * n, out_specs=[HBM_SPEC] * n,
        out_shape=[jax.ShapeDtypeStruct((3,) + p.shape[1:], p.dtype) for p in parts],
        scratch_shapes=[pltpu.SemaphoreType.DMA((3 * n,)), pltpu.SemaphoreType.DMA((3 * n,))],
        name="scatter_chips",
    )(*parts)


def _swap_cores(arrs):
    n = len(arrs)

    def body(*refs):
        ins, outs = refs[:n], refs[n:2 * n]
        send_sems, recv_sems = refs[2 * n:]
        x, y, c = _coords()
        copies = []
        for t in range(n):
            cp = pltpu.make_async_remote_copy(
                src_ref=ins[t], dst_ref=outs[t], send_sem=send_sems.at[t], recv_sem=recv_sems.at[t],
                device_id=(x, y, 1 - c), device_id_type=MESH)
            cp.start()
            copies.append(cp)
        for cp in copies:
            cp.wait()

    return pl.pallas_call(
        body, in_specs=[HBM_SPEC] * n, out_specs=[HBM_SPEC] * n,
        out_shape=[jax.ShapeDtypeStruct(a.shape, a.dtype) for a in arrs],
        scratch_shapes=[pltpu.SemaphoreType.DMA((n,)), pltpu.SemaphoreType.DMA((n,))],
        name="swap_cores",
    )(*arrs)


def _gather_devices(a):
    def body(in_ref, out_ref, send_sems, recv_sems, local_sem):
        x, y, c = _coords()
        me = 4 * x + 2 * y + c
        own = pltpu.make_async_copy(in_ref, out_ref.at[me], local_sem)
        own.start()
        waits = [own.wait]
        for k in range(1, N_DEVICES):
            px = 1 - x if k & 4 else x
            py = 1 - y if k & 2 else y
            pc = 1 - c if k & 1 else c
            peer = (px, py, pc)
            send = pltpu.make_async_remote_copy(
                src_ref=in_ref, dst_ref=out_ref.at[me], send_sem=send_sems.at[k - 1], recv_sem=recv_sems.at[k - 1],
                device_id=peer, device_id_type=MESH)
            send.start()
            recv = pltpu.make_async_remote_copy(
                src_ref=in_ref, dst_ref=out_ref.at[4 * px + 2 * py + pc], send_sem=send_sems.at[k - 1],
                recv_sem=recv_sems.at[k - 1], device_id=peer, device_id_type=MESH)
            waits += [send.wait_send, recv.wait_recv]
        for w in waits:
            w()

    return pl.pallas_call(
        body, in_specs=[HBM_SPEC], out_specs=HBM_SPEC,
        out_shape=jax.ShapeDtypeStruct((N_DEVICES,) + a.shape, a.dtype),
        scratch_shapes=[pltpu.SemaphoreType.DMA((N_DEVICES - 1,)), pltpu.SemaphoreType.DMA((N_DEVICES - 1,)),
                        pltpu.SemaphoreType.DMA],
        name="gather_devices",
    )(a)


def _mlp_grad_epilogue(r, u):
    return r * (2.0 * jnp.maximum(u, 0.0))


def kernel(x, norm_gains, sb_w_qkv, sb_q_gain, sb_k_gain, sb_w_o, hg_w_in, hg_lb_logits, hg_norm_gain, hg_w_o, mlp_w1, mlp_w2, loss_target, m_norm_gains, m_sb_w_qkv, m_sb_q_gain, m_sb_k_gain, m_sb_w_o, m_hg_w_in, m_hg_lb_logits, m_hg_norm_gain, m_hg_w_o, m_mlp_w1, m_mlp_w2, v_norm_gains, v_sb_w_qkv, v_sb_q_gain, v_sb_k_gain, v_sb_w_o, v_hg_w_in, v_hg_lb_logits, v_hg_norm_gain, v_hg_w_o, v_mlp_w1, v_mlp_w2):
    depth = norm_gains.shape[0]
    n_sb, n_hg = sb_w_qkv.shape[0], hg_w_in.shape[0]
    xs, tgt = x[0], loss_target[0]
    s, d = xs.shape
    dq = d // N_CHIPS
    cx, cy, cc = _coords()
    chip = 2 * cx + cy
    chip_arr = jnp.reshape(chip, (1,)).astype(jnp.int32)

    wq_g, wso_g, win_g, who_g, w1_g, w2_g, ng_g, lbl_g = _gather_chips(
        [sb_w_qkv.astype(BF16), sb_w_o.astype(BF16), hg_w_in.astype(BF16), hg_w_o.astype(BF16),
         mlp_w1.astype(BF16), mlp_w2.astype(BF16), norm_gains, hg_lb_logits])
    gains = jnp.transpose(ng_g, (1, 2, 0, 3)).reshape(depth, 2, d)
    logits = jnp.transpose(lbl_g, (1, 0, 2)).reshape(n_hg, d)
    lbs, lb_p = _lb_fwd(logits)
    qg_rows = [jnp.tile(sb_q_gain[j], d // SB_HEAD_DIM)[None] for j in range(n_sb)]
    kg_rows = [jnp.tile(sb_k_gain[j], d // SB_HEAD_DIM)[None] for j in range(n_sb)]

    saved = []
    xc = xs
    for layer in range(depth):
        j = layer // 2
        h1 = _rmsnorm_fwd(xc, gains[layer, 0][None])
        if layer % 2 == 0:
            qkv = _mm_fwd_cols(h1, wq_g, j, name="sb_qkv")
            qn, kn, vb = _qk_norm_fwd(qkv, qg_rows[j], kg_rows[j])
            o = _sb_attn_fwd(qn, kn, vb)
            x_mid = _mm_fwd_rows(o, wso_g, j, residual=xc, name="sb_out")
            mix = (qkv, qn, kn, vb, o)
        else:
            proj = _mm_fwd_cols(h1, win_g, j, name="hg_in")
            y, o, states = _hg_fwd(proj, lbs[j][None], hg_norm_gain[j][None])
            x_mid = _mm_fwd_rows(y, who_g, j, residual=xc, name="hg_out")
            mix = (proj, y, o, states)
        h2 = _rmsnorm_fwd(x_mid, gains[layer, 1][None])
        u = _mm_fwd_cols(h2, w1_g, layer, name="mlp_up")
        x_out = _mm_fwd_rows(u, w2_g, layer, residual=x_mid, a_fn=_relu2, name="mlp_down")
        saved.append((xc, h1, mix, x_mid, h2, u))
        xc = x_out

    sq, dx = _loss_head(xc, tgt)
    loss = lax.psum(jnp.sum(sq) * (0.5 / d), ("x", "y", "c"))

    g_w1, g_w2 = [None] * depth, [None] * depth
    g_qkv, g_so = [None] * n_sb, [None] * n_sb
    g_in, g_ho = [None] * n_hg, [None] * n_hg
    dgains = [[None, None] for _ in range(depth)]
    dqg, dkg = [None] * n_sb, [None] * n_sb
    dhgain, dlb = [None] * n_hg, [None] * n_hg
    for layer in reversed(range(depth)):
        j = layer // 2
        x_in, h1, mix, x_mid, h2, u = saved[layer]
        du = _mm_bwd_rows(dx, w2_g, layer, name="mlp_down_dx", out_dtype=BF16,
                          epi_fn=_mlp_grad_epilogue, epi_args=(u,))
        g_w2[layer] = _mm_dw_rows(u, dx, a_fn=_relu2, name="mlp_down_dw")
        dh2 = _mm_bwd_cols(du, w1_g, layer, name="mlp_up_dx")
        g_w1[layer] = _mm_dw_cols(h2, du, name="mlp_up_dw")
        dx, dgains[layer][1] = _rmsnorm_bwd(dh2, x_mid, gains[layer, 1][None], dx)
        if layer % 2 == 0:
            qkv, qn, kn, vb, o = mix
            do = _mm_bwd_rows(dx, wso_g, j, name="sb_out_dx")
            g_so[j] = _mm_dw_rows(o, dx, name="sb_out_dw")
            dqn, dkn, dv = _sb_attn_bwd(qn, kn, vb, do)
            dqkv, dqg_lane, dkg_lane = _qk_norm_bwd(qkv, qg_rows[j], kg_rows[j], dqn, dkn, dv)
            dqg[j] = jnp.sum(dqg_lane.reshape(-1, SB_HEAD_DIM), axis=0)
            dkg[j] = jnp.sum(dkg_lane.reshape(-1, SB_HEAD_DIM), axis=0)
            dh1 = _mm_bwd_cols(dqkv, wq_g, j, name="sb_qkv_dx")
            g_qkv[j] = _mm_dw_cols(h1, dqkv, name="sb_qkv_dw")
        else:
            proj, y, o, states = mix
            dy = _mm_bwd_rows(dx, who_g, j, name="hg_out_dx")
            g_ho[j] = _mm_dw_rows(y, dx, name="hg_out_dw")
            dq_raw, df_raw, di, dg, dlb_row, dgain_heads = _hg_bwd(
                proj, lbs[j][None], hg_norm_gain[j][None], o, states, dy)
            dlb[j] = dlb_row
            dhgain[j] = jnp.sum(dgain_heads.reshape(-1, HG_HEAD_DIM), axis=0)
            dproj = jnp.concatenate([dq_raw, df_raw, di, dg], axis=1)
            dh1 = _mm_bwd_cols(dproj, win_g, j, name="hg_in_dx")
            g_in[j] = _mm_dw_cols(h1, dproj, name="hg_in_dw")
        dx, dgains[layer][0] = _rmsnorm_bwd(dh1, x_in, gains[layer, 0][None], dx)
    grad_x = dx[None]
    dlogits = _lb_bwd(lb_p, jnp.concatenate(dlb, axis=0))

    big_w = [sb_w_qkv, sb_w_o, hg_w_in, hg_w_o, mlp_w1, mlp_w2]
    big_m = [m_sb_w_qkv, m_sb_w_o, m_hg_w_in, m_hg_w_o, m_mlp_w1, m_mlp_w2]
    big_v = [v_sb_w_qkv, v_sb_w_o, v_hg_w_in, v_hg_w_o, v_mlp_w1, v_mlp_w2]
    parts = [jnp.stack(g, axis=1) for g in (g_qkv, g_so, g_in, g_ho, g_w1, g_w2)]
    parts = [p.reshape(N_CHIPS, -1, p.shape[-1]) for p in parts]
    recv = _scatter_chips(parts)
    chip_sums = [_sum_slots(p, r, chip_arr) for p, r in zip(parts, recv)]
    other_core = _swap_cores(chip_sums)
    big = [_adamw(w, m, v, [a.reshape(w.shape), b.reshape(w.shape)])
           for w, m, v, a, b in zip(big_w, big_m, big_v, chip_sums, other_core)]

    n_small = 2 * depth + n_hg + 3
    small_rows = -(-n_small // 8) * 8
    small = jnp.zeros((small_rows, d), F32)
    small = small.at[0:2 * depth].set(jnp.concatenate([r for pair in dgains for r in pair], axis=0))
    small = small.at[2 * depth:2 * depth + n_hg].set(dlogits)
    base = 2 * depth + n_hg
    small = small.at[base, 0:n_sb * SB_HEAD_DIM].set(jnp.concatenate(dqg))
    small = small.at[base + 1, 0:n_sb * SB_HEAD_DIM].set(jnp.concatenate(dkg))
    small = small.at[base + 2, 0:n_hg * HG_HEAD_DIM].set(jnp.concatenate(dhgain))
    small = _sum_devices(_gather_devices(small))
    my_cols = lambda a: lax.dynamic_slice_in_dim(a, chip * dq, dq, axis=1)
    g_ng = my_cols(small[0:2 * depth]).reshape(norm_gains.shape)
    g_lbl = my_cols(small[2 * depth:base])
    g_qg = small[base, 0:n_sb * SB_HEAD_DIM].reshape(sb_q_gain.shape)
    g_kg = small[base + 1, 0:n_sb * SB_HEAD_DIM].reshape(sb_k_gain.shape)
    g_hgn = small[base + 2, 0:n_hg * HG_HEAD_DIM].reshape(hg_norm_gain.shape)
    r_ng = _adamw(norm_gains, m_norm_gains, v_norm_gains, [g_ng])
    r_qg = _adamw(sb_q_gain, m_sb_q_gain, v_sb_q_gain, [g_qg])
    r_kg = _adamw(sb_k_gain, m_sb_k_gain, v_sb_k_gain, [g_kg])
    r_lbl = _adamw(hg_lb_logits, m_hg_lb_logits, v_hg_lb_logits, [g_lbl])
    r_hgn = _adamw(hg_norm_gain, m_hg_norm_gain, v_hg_norm_gain, [g_hgn])

    per_weight = [r_ng, big[0], r_qg, r_kg, big[1], big[2], r_lbl, r_hgn, big[3], big[4], big[5]]
    outs = [loss, grad_x]
    for field in range(4):
        outs += [r[field] for r in per_weight]
    return tuple(outs)
```

```python
import functools
import math

import numpy as np
import jax
import jax.numpy as jnp
from jax import lax
from jax.experimental import pallas as pl
from jax.experimental.pallas import tpu as pltpu

F32 = jnp.float32
BF16 = jnp.bfloat16

NORM_EPS = 1e-6
SB_HEAD_DIM = 64
HG_HEAD_DIM = 128
HG_CHUNK = 64
LANES = 128
VMEM_LIMIT_BYTES = 56 * 2 ** 20
N_CHIPS = 4
N_DEVICES = 8

ADAM_LR = 0.001
ADAM_B1 = 0.9
ADAM_B2 = 0.999
ADAM_EPS = 1e-08
ADAM_WD = 0.01
ADAM_STEP = 10

MESH = pl.DeviceIdType.MESH
HBM_SPEC = pl.BlockSpec(memory_space=pltpu.HBM)

NN = (((1,), (0,)), ((), ()))
NT = (((1,), (1,)), ((), ()))
TN = (((0,), (0,)), ((), ()))


def _params(sem=None):
    return pltpu.CompilerParams(dimension_semantics=sem, vmem_limit_bytes=VMEM_LIMIT_BYTES)


def _pick(dim, pref):
    for t in (1024, 768, 512, 384, 256, 128, 64, 32, 16, 8):
        if t <= pref and dim % t == 0:
            return t
    return dim


def _dot(a, b, dims=NN):
    return lax.dot_general(a, b, dims, preferred_element_type=F32)


def _sigmoid(x):
    e = jnp.exp(-jnp.abs(x))
    return jnp.where(x >= 0, 1.0, e) / (1.0 + e)


def _matmul(a, b, *, mode, grid, a_block, a_map, b_block, b_map, o_block, o_map, out_shape, out_dtype, name,
            a_fn=None, epi_fn=None, epi_args=()):
    nk = grid[2]
    dims = {"nn": NN, "nt": NT, "tn": TN}[mode]
    n_epi = len(epi_args)

    def body(a_ref, b_ref, *rest):
        epi_refs = rest[:n_epi]
        o_ref = rest[n_epi]
        acc_ref = rest[n_epi + 1]
        kk = pl.program_id(2)

        @pl.when(kk == 0)
        def _():
            acc_ref[...] = jnp.zeros_like(acc_ref)

        av = a_ref[...]
        if a_fn is not None:
            av = a_fn(av)
        acc_ref[...] += _dot(av.astype(BF16), b_ref[...].astype(BF16), dims)

        @pl.when(kk == nk - 1)
        def _():
            r = acc_ref[...]
            if epi_fn is not None:
                r = epi_fn(r, *[e[...] for e in epi_refs])
            o_ref[...] = r.astype(o_ref.dtype)

    acc_shape = tuple(d for d in o_block if d is not None)
    in_specs = [pl.BlockSpec(a_block, a_map), pl.BlockSpec(b_block, b_map)]
    in_specs += [pl.BlockSpec(o_block, o_map) for _ in epi_args]
    return pl.pallas_call(
        body, grid=grid, in_specs=in_specs, out_specs=pl.BlockSpec(o_block, o_map),
        out_shape=jax.ShapeDtypeStruct(out_shape, out_dtype),
        scratch_shapes=[pltpu.VMEM(acc_shape, F32)],
        compiler_params=_params(("parallel", "parallel", "arbitrary")), name=name,
    )(a, b, *epi_args)


def _relu2(u):
    r = jnp.maximum(u, 0.0)
    return r * r


def _add(r, res):
    return r + res


def _mm_fwd_cols(a, wg, layer, *, name):
    s, k = a.shape
    ncs = wg.shape[3]
    tm, tk, tn = _pick(s, 1024), _pick(k, 1024), _pick(ncs, 1024)
    npb = ncs // tn
    return _matmul(a, wg, mode="nn", grid=(s // tm, N_CHIPS * npb, k // tk),
                   a_block=(tm, tk), a_map=lambda i, j, kk: (i, kk),
                   b_block=(None, None, tk, tn), b_map=lambda i, j, kk: (j // npb, layer, kk, j % npb),
                   o_block=(tm, tn), o_map=lambda i, j, kk: (i, j),
                   out_shape=(s, N_CHIPS * ncs), out_dtype=F32, name=name)


def _mm_fwd_rows(a, wg, layer, *, residual, name, a_fn=None):
    s = a.shape[0]
    krs, n = wg.shape[2], wg.shape[3]
    tm, tk, tn = _pick(s, 1024), _pick(krs, 1024), _pick(n, 1024)
    kpb = krs // tk
    return _matmul(a, wg, mode="nn", grid=(s // tm, n // tn, N_CHIPS * kpb),
                   a_block=(tm, tk), a_map=lambda i, j, kk: (i, kk),
                   b_block=(None, None, tk, tn), b_map=lambda i, j, kk: (kk // kpb, layer, kk % kpb, j),
                   o_block=(tm, tn), o_map=lambda i, j, kk: (i, j),
                   out_shape=(s, n), out_dtype=F32, name=name, a_fn=a_fn, epi_fn=_add, epi_args=(residual,))


def _mm_bwd_cols(dy, wg, layer, *, name, out_dtype=F32):
    s = dy.shape[0]
    kw, ncs = wg.shape[2], wg.shape[3]
    tm, tn, tk = _pick(s, 1024), _pick(kw, 1024), _pick(ncs, 1024)
    kpb = ncs // tk
    return _matmul(dy, wg, mode="nt", grid=(s // tm, kw // tn, N_CHIPS * kpb),
                   a_block=(tm, tk), a_map=lambda i, j, kk: (i, kk),
                   b_block=(None, None, tn, tk), b_map=lambda i, j, kk: (kk // kpb, layer, j, kk % kpb),
                   o_block=(tm, tn), o_map=lambda i, j, kk: (i, j),
                   out_shape=(s, kw), out_dtype=out_dtype, name=name)


def _mm_bwd_rows(dy, wg, layer, *, name, out_dtype=F32, epi_fn=None, epi_args=()):
    s, n = dy.shape
    krs = wg.shape[2]
    tm, tn, tk = _pick(s, 1024), _pick(krs, 1024), _pick(n, 1024)
    npb = krs // tn
    return _matmul(dy, wg, mode="nt", grid=(s // tm, N_CHIPS * npb, n // tk),
                   a_block=(tm, tk), a_map=lambda i, j, kk: (i, kk),
                   b_block=(None, None, tn, tk), b_map=lambda i, j, kk: (j // npb, layer, j % npb, kk),
                   o_block=(tm, tn), o_map=lambda i, j, kk: (i, j),
                   out_shape=(s, N_CHIPS * krs), out_dtype=out_dtype, name=name, epi_fn=epi_fn, epi_args=epi_args)


def _mm_dw_cols(xa, dy, *, name):
    s, kx = xa.shape
    ncs = dy.shape[1] // N_CHIPS
    tm, tn, tk = _pick(kx, 1024), _pick(ncs, 1024), _pick(s, 1024)
    npb = ncs // tn
    return _matmul(xa, dy, mode="tn", grid=(kx // tm, N_CHIPS * npb, s // tk),
                   a_block=(tk, tm), a_map=lambda i, j, kk: (kk, i),
                   b_block=(tk, tn), b_map=lambda i, j, kk: (kk, j),
                   o_block=(None, tm, tn), o_map=lambda i, j, kk: (j // npb, i, j % npb),
                   out_shape=(N_CHIPS, kx, ncs), out_dtype=F32, name=name)


def _mm_dw_rows(xa, dy, *, name, a_fn=None):
    s, n = dy.shape
    krs = xa.shape[1] // N_CHIPS
    tm, tn, tk = _pick(krs, 1024), _pick(n, 1024), _pick(s, 1024)
    mpb = krs // tm
    return _matmul(xa, dy, mode="tn", grid=(N_CHIPS * mpb, n // tn, s // tk),
                   a_block=(tk, tm), a_map=lambda i, j, kk: (kk, i),
                   b_block=(tk, tn), b_map=lambda i, j, kk: (kk, j),
                   o_block=(None, tm, tn), o_map=lambda i, j, kk: (i // mpb, i % mpb, j),
                   out_shape=(N_CHIPS, krs, n), out_dtype=F32, name=name, a_fn=a_fn)


def _rmsnorm_fwd(x, gain_row):
    s, d = x.shape
    ts = _pick(s, 512)

    def body(x_ref, g_ref, h_ref):
        xv = x_ref[...]
        r = lax.rsqrt(jnp.mean(xv * xv, axis=-1, keepdims=True) + NORM_EPS)
        h_ref[...] = (xv * r * g_ref[...]).astype(h_ref.dtype)

    return pl.pallas_call(
        body, grid=(s // ts,),
        in_specs=[pl.BlockSpec((ts, d), lambda i: (i, 0)), pl.BlockSpec((1, d), lambda i: (0, 0))],
        out_specs=pl.BlockSpec((ts, d), lambda i: (i, 0)),
        out_shape=jax.ShapeDtypeStruct((s, d), BF16),
        compiler_params=_params(("parallel",)), name="rmsnorm_fwd",
    )(x, gain_row)


def _rmsnorm_bwd(dh, x, gain_row, dx_res):
    s, d = x.shape
    ts = _pick(s, 512)

    def body(dh_ref, x_ref, g_ref, res_ref, dx_ref, dg_ref):
        i = pl.program_id(0)
        xv = x_ref[...]
        r = lax.rsqrt(jnp.mean(xv * xv, axis=-1, keepdims=True) + NORM_EPS)
        xhat = xv * r
        dh_v = dh_ref[...]
        dxhat = dh_v * g_ref[...]
        dx = r * (dxhat - xhat * jnp.mean(dxhat * xhat, axis=-1, keepdims=True))
        dx_ref[...] = res_ref[...] + dx
        part = jnp.sum(dh_v * xhat, axis=0, keepdims=True)

        @pl.when(i == 0)
        def _():
            dg_ref[...] = part

        @pl.when(i > 0)
        def _():
            dg_ref[...] += part

    return pl.pallas_call(
        body, grid=(s // ts,),
        in_specs=[pl.BlockSpec((ts, d), lambda i: (i, 0)), pl.BlockSpec((ts, d), lambda i: (i, 0)),
                  pl.BlockSpec((1, d), lambda i: (0, 0)), pl.BlockSpec((ts, d), lambda i: (i, 0))],
        out_specs=[pl.BlockSpec((ts, d), lambda i: (i, 0)), pl.BlockSpec((1, d), lambda i: (0, 0))],
        out_shape=[jax.ShapeDtypeStruct((s, d), F32), jax.ShapeDtypeStruct((1, d), F32)],
        compiler_params=_params(("arbitrary",)), name="rmsnorm_bwd",
    )(dh, x, gain_row, dx_res)


def _loss_head(y, target):
    s, d = y.shape
    ts = _pick(s, 512)

    def body(y_ref, t_ref, sq_ref, dy_ref):
        i = pl.program_id(0)
        err = y_ref[...] - t_ref[...]
        dy_ref[...] = err / d
        part = jnp.sum(err * err, axis=0, keepdims=True)

        @pl.when(i == 0)
        def _():
            sq_ref[...] = part

        @pl.when(i > 0)
        def _():
            sq_ref[...] += part

    return pl.pallas_call(
        body, grid=(s // ts,),
        in_specs=[pl.BlockSpec((ts, d), lambda i: (i, 0)), pl.BlockSpec((ts, d), lambda i: (i, 0))],
        out_specs=[pl.BlockSpec((1, d), lambda i: (0, 0)), pl.BlockSpec((ts, d), lambda i: (i, 0))],
        out_shape=[jax.ShapeDtypeStruct((1, d), F32), jax.ShapeDtypeStruct((s, d), F32)],
        compiler_params=_params(("arbitrary",)), name="loss_head",
    )(y, target)


def _pair_mean(val, low_half):
    s0 = jnp.sum(jnp.where(low_half, val, 0.0), axis=-1, keepdims=True)
    s1 = jnp.sum(jnp.where(low_half, 0.0, val), axis=-1, keepdims=True)
    return jnp.where(low_half, s0, s1) * (1.0 / SB_HEAD_DIM)


def _qk_norm_fwd(qkv, qgain_row, kgain_row):
    s, d3 = qkv.shape
    d = d3 // 3
    ts = _pick(s, 512)
    groups = d // LANES

    def body(q_ref, k_ref, v_ref, qg_ref, kg_ref, qn_ref, kn_ref, vb_ref):
        low_half = lax.broadcasted_iota(jnp.int32, (ts, LANES), 1) < SB_HEAD_DIM
        for src, gain, dst in ((q_ref, qg_ref, qn_ref), (k_ref, kg_ref, kn_ref)):
            for p in range(groups):
                cols = slice(p * LANES, (p + 1) * LANES)
                xp = src[:, cols]
                r = lax.rsqrt(_pair_mean(xp * xp, low_half) + NORM_EPS)
                dst[:, cols] = (xp * r * gain[:, cols]).astype(dst.dtype)
        vb_ref[...] = v_ref[...].astype(vb_ref.dtype)

    tok = lambda c: pl.BlockSpec((ts, d), lambda i: (i, c))
    row = pl.BlockSpec((1, d), lambda i: (0, 0))
    return pl.pallas_call(
        body, grid=(s // ts,),
        in_specs=[tok(0), tok(1), tok(2), row, row],
        out_specs=[tok(0), tok(0), tok(0)],
        out_shape=[jax.ShapeDtypeStruct((s, d), BF16)] * 3,
        compiler_params=_params(("parallel",)), name="qk_norm_fwd",
    )(qkv, qkv, qkv, qgain_row, kgain_row)


def _qk_norm_bwd(qkv, qgain_row, kgain_row, dqn, dkn, dv):
    s, d3 = qkv.shape
    d = d3 // 3
    ts = _pick(s, 512)
    groups = d // LANES

    def body(q_ref, k_ref, qg_ref, kg_ref, dqn_ref, dkn_ref, dv_ref, dqkv_ref, dqg_ref, dkg_ref):
        i = pl.program_id(0)
        low_half = lax.broadcasted_iota(jnp.int32, (ts, LANES), 1) < SB_HEAD_DIM
        for which, (src, gain, dsrc, dgain) in enumerate(((q_ref, qg_ref, dqn_ref, dqg_ref),
                                                          (k_ref, kg_ref, dkn_ref, dkg_ref))):
            for p in range(groups):
                cols = slice(p * LANES, (p + 1) * LANES)
                xp = src[:, cols]
                r = lax.rsqrt(_pair_mean(xp * xp, low_half) + NORM_EPS)
                xhat = xp * r
                dy = dsrc[:, cols]
                dxhat = dy * gain[:, cols]
                dx = r * (dxhat - xhat * _pair_mean(dxhat * xhat, low_half))
                dqkv_ref[:, which * d + p * LANES: which * d + (p + 1) * LANES] = dx.astype(dqkv_ref.dtype)
                part = jnp.sum(dy * xhat, axis=0, keepdims=True)

                @pl.when(i == 0)
                def _():
                    dgain[:, cols] = part

                @pl.when(i > 0)
                def _():
                    dgain[:, cols] += part
        dqkv_ref[:, 2 * d:] = dv_ref[...].astype(dqkv_ref.dtype)

    tok = lambda c: pl.BlockSpec((ts, d), lambda i: (i, c))
    row = pl.BlockSpec((1, d), lambda i: (0, 0))
    return pl.pallas_call(
        body, grid=(s // ts,),
        in_specs=[tok(0), tok(1), row, row, tok(0), tok(0), tok(0)],
        out_specs=[pl.BlockSpec((ts, d3), lambda i: (i, 0)), row, row],
        out_shape=[jax.ShapeDtypeStruct((s, d3), BF16), jax.ShapeDtypeStruct((1, d), F32),
                   jax.ShapeDtypeStruct((1, d), F32)],
        compiler_params=_params(("arbitrary",)), name="qk_norm_bwd",
    )(qkv, qkv, qgain_row, kgain_row, dqn, dkn, dv)


def _split2(x):
    hi = x.astype(BF16)
    lo = (x - hi.astype(F32)).astype(BF16)
    return hi, lo


SB_TK = 128


def _sb_consts(tk):
    j = np.arange(tk)
    ones = np.ones((tk, tk), np.float32)
    out = []
    for tri in ((j[:, None] >= j[None, :]), (j[:, None] <= j[None, :])):
        half = np.concatenate([tri.astype(np.float32), ones], axis=1)
        out.append(jnp.asarray(np.concatenate([half, half], axis=0), BF16))
    return out


def _head_stack(blk, low_half):
    f = blk.astype(F32)
    return jnp.concatenate([jnp.where(low_half, f, 0.0), jnp.where(low_half, 0.0, f)], axis=0).astype(BF16)


def _sb_tile_sums(z, valid, tri2):
    e = jnp.exp(-jnp.abs(z))
    lstay = jnp.minimum(-z, 0.0) - jnp.log(1.0 + e)
    if valid is not None:
        lstay = jnp.where(valid, lstay, 0.0)
    hi, lo = _split2(lstay)
    return e, _dot(jnp.concatenate([hi, lo], axis=1), tri2)


def _sb_weights(z, c2, valid, run):
    w = jnp.exp(z + c2[:, :SB_TK] + run)
    return w if valid is None else jnp.where(valid, w, 0.0)


def _sb_attn_fwd(qn, kn, vb):
    s, d = qn.shape
    tk = SB_TK
    tq = _pick(s, 256)
    nq, ndiag = s // tq, tq // tk
    assert tq % (2 * tk) == 0, "tiles below the diagonal are taken two at a time"
    npairs = d // LANES
    scale = 1.0 / math.sqrt(SB_HEAD_DIM)
    tri_ge2, _ = _sb_consts(tk)

    def body(q_ref, k_ref, v_ref, tri_ref, o_ref, acc_ref, run_ref):
        qi = pl.program_id(1)
        low_half = lax.broadcasted_iota(jnp.int32, (tk, LANES), 1) < SB_HEAD_DIM
        row = lax.broadcasted_iota(jnp.int32, (tq, tk), 0)
        col = lax.broadcasted_iota(jnp.int32, (tq, tk), 1)
        qs = (q_ref[...].astype(F32) * scale).astype(BF16)
        acc_ref[...] = jnp.zeros_like(acc_ref)
        run_ref[...] = jnp.zeros_like(run_ref)
        n_full = qi * ndiag

        def sums(kb, dd):
            koff = pl.multiple_of(kb * tk, tk)
            kcat = _head_stack(k_ref[pl.ds(koff, tk), :], low_half)
            vcat = _head_stack(v_ref[pl.ds(koff, tk), :], low_half)
            z2 = _dot(qs, kcat, NT)
            valid = None if dd is None else row > col + dd * tk
            zs = [z2[:, h * tk:(h + 1) * tk] for h in range(2)]
            return zs, [_sb_tile_sums(z, valid, tri_ref[...])[1] for z in zs], valid, vcat

        def finish(zs, c2s, valid, vcat):
            ws = []
            for h in range(2):
                ws.append(_sb_weights(zs[h], c2s[h], valid, run_ref[h]).astype(BF16))
                run_ref[h] += c2s[h][:, tk:]
            acc_ref[...] += _dot(jnp.concatenate(ws, axis=1), vcat)

        for pre in [sums(n_full + dd, dd) for dd in reversed(range(ndiag))]:
            finish(*pre)

        def two_tiles(it, carry):
            kb = n_full - 1 - 2 * it
            first, second = sums(kb, None), sums(kb - 1, None)
            finish(*first)
            finish(*second)
            return carry

        lax.fori_loop(0, n_full // 2, two_tiles, 0)
        o_ref[...] = acc_ref[...]

    blk = pl.BlockSpec((tq, LANES), lambda p, i: (i, p))
    full = pl.BlockSpec((s, LANES), lambda p, i: (0, p))
    return pl.pallas_call(
        body, grid=(npairs, nq),
        in_specs=[blk, full, full, pl.BlockSpec((2 * tk, 2 * tk), lambda p, i: (0, 0))],
        out_specs=blk, out_shape=jax.ShapeDtypeStruct((s, d), F32),
        scratch_shapes=[pltpu.VMEM((tq, LANES), F32), pltpu.VMEM((2, tq, tk), F32)],
        compiler_params=_params(("parallel", "arbitrary")), name="sb_attn_fwd",
    )(qn, kn, vb, tri_ge2)


def _sb_attn_bwd(qn, kn, vb, do):
    s, d = qn.shape
    tk = SB_TK
    tq = _pick(s, 256)
    nq, ndiag = s // tq, tq // tk
    assert tq % (2 * tk) == 0, "tiles below the diagonal are taken two at a time"
    npairs = d // LANES
    scale = 1.0 / math.sqrt(SB_HEAD_DIM)
    tri_ge2, tri_le2 = _sb_consts(tk)

    def body(q_ref, k_ref, v_ref, do_ref, tge_ref, tle_ref, dq_ref, dk_ref, dv_ref,
             g_cache, s_cache, run_ref, dq_acc):
        qi = pl.program_id(1)

        @pl.when(qi == 0)
        def _():
            dk_ref[...] = jnp.zeros_like(dk_ref)
            dv_ref[...] = jnp.zeros_like(dv_ref)

        low_half = lax.broadcasted_iota(jnp.int32, (tk, LANES), 1) < SB_HEAD_DIM
        row = lax.broadcasted_iota(jnp.int32, (tq, tk), 0)
        col = lax.broadcasted_iota(jnp.int32, (tq, tk), 1)
        qs = (q_ref[...].astype(F32) * scale).astype(BF16)
        dob = do_ref[...].astype(BF16)
        n_full = qi * ndiag

        def a_sums(kb, dd):
            koff = pl.multiple_of(kb * tk, tk)
            kcat = _head_stack(k_ref[pl.ds(koff, tk), :], low_half)
            vcat = _head_stack(v_ref[pl.ds(koff, tk), :], low_half)
            z2 = _dot(qs, kcat, NT)
            dw2 = _dot(dob, vcat, NT)
            valid = None if dd is None else row > col + dd * tk
            c2s = []
            for h in range(2):
                cols = slice(h * tk, (h + 1) * tk)
                z = z2[:, cols]
                e, c2 = _sb_tile_sums(z, valid, tge_ref[...])
                s_cache[kb, :, cols] = jnp.where(z >= 0, 1.0, e) / (1.0 + e)
                c2s.append(c2)
            return kb, koff, z2, dw2, c2s, valid

        def a_finish(kb, koff, z2, dw2, c2s, valid):
            ws = []
            for h in range(2):
                cols = slice(h * tk, (h + 1) * tk)
                w = _sb_weights(z2[:, cols], c2s[h], valid, run_ref[h])
                run_ref[h] += c2s[h][:, tk:]
                g_cache[kb, :, cols] = w * dw2[:, cols]
                ws.append(w.astype(BF16))
            dv2 = _dot(jnp.concatenate(ws, axis=1), dob, TN)
            dv_ref[pl.ds(koff, tk), :] += jnp.where(low_half, dv2[:tk], dv2[tk:])

        def b_sums(kb, dd):
            gs = [g_cache[kb, :, h * tk:(h + 1) * tk] for h in range(2)]
            p2s = [_dot(jnp.concatenate(_split2(g), axis=1), tle_ref[...]) for g in gs]
            return kb, gs, p2s, (None if dd is None else row > col + dd * tk)

        def b_finish(kb, gs, p2s, valid):
            koff = pl.multiple_of(kb * tk, tk)
            dzs = []
            for h in range(2):
                dz = gs[h] - s_cache[kb, :, h * tk:(h + 1) * tk] * (p2s[h][:, :tk] + run_ref[h])
                if valid is not None:
                    dz = jnp.where(valid, dz, 0.0)
                run_ref[h] += p2s[h][:, tk:]
                dzs.append(dz.astype(BF16))
            dzcat = jnp.concatenate(dzs, axis=1)
            dq_acc[...] += _dot(dzcat, _head_stack(k_ref[pl.ds(koff, tk), :], low_half))
            dk2 = _dot(dzcat, qs, TN)
            dk_ref[pl.ds(koff, tk), :] += jnp.where(low_half, dk2[:tk], dk2[tk:])

        run_ref[...] = jnp.zeros_like(run_ref)
        for pre in [a_sums(n_full + dd, dd) for dd in reversed(range(ndiag))]:
            a_finish(*pre)

        def two_a(it, carry):
            kb = n_full - 1 - 2 * it
            first, second = a_sums(kb, None), a_sums(kb - 1, None)
            a_finish(*first)
            a_finish(*second)
            return carry

        lax.fori_loop(0, n_full // 2, two_a, 0)

        run_ref[...] = jnp.zeros_like(run_ref)
        dq_acc[...] = jnp.zeros_like(dq_acc)

        def two_b(it, carry):
            first, second = b_sums(2 * it, None), b_sums(2 * it + 1, None)
            b_finish(*first)
            b_finish(*second)
            return carry

        lax.fori_loop(0, n_full // 2, two_b, 0)
        for pre in [b_sums(n_full + dd, dd) for dd in range(ndiag)]:
            b_finish(*pre)
        dq_ref[...] = dq_acc[...] * scale

    blk = pl.BlockSpec((tq, LANES), lambda p, i: (i, p))
    full = pl.BlockSpec((s, LANES), lambda p, i: (0, p))
    tri = pl.BlockSpec((2 * tk, 2 * tk), lambda p, i: (0, 0))
    return pl.pallas_call(
        body, grid=(npairs, nq),
        in_specs=[blk, full, full, blk, tri, tri],
        out_specs=[blk, full, full],
        out_shape=[jax.ShapeDtypeStruct((s, d), F32)] * 3,
        scratch_shapes=[pltpu.VMEM((s // tk, tq, 2 * tk), F32), pltpu.VMEM((s // tk, tq, 2 * tk), F32),
                        pltpu.VMEM((2, tq, tk), F32), pltpu.VMEM((tq, LANES), F32)],
        compiler_params=_params(("parallel", "arbitrary")), name="sb_attn_bwd",
    )(qn, kn, vb, do, tri_ge2, tri_le2)


def _hg_consts(c):
    levels = []
    h = c // 2
    while h >= 1:
        levels.append(h)
        h //= 2
    t = np.arange(c)
    j = t[None, :]
    rows, masks = [], []
    for h in levels:
        blk = t // (2 * h)
        mid = blk * 2 * h + h - 1
        second = (t % (2 * h)) >= h
        rows.append(second[:, None] & (j > mid[:, None]) & (j <= t[:, None]))
        rows.append((~second)[:, None] & (j > t[:, None]) & (j <= mid[:, None]))
        masks.append((blk[:, None] == blk[None, :]) & second[:, None] & (~second)[None, :])
    rows.append(j <= t[:, None])
    rows.append(j > t[:, None])
    masks.append(t[:, None] == t[None, :])
    m_all = np.concatenate(rows, axis=0).astype(np.float32)
    mask_all = np.stack(masks, axis=0).astype(np.float32)
    suffix = (t[None, :] >= t[:, None]).astype(np.float32)
    return len(levels), jnp.asarray(m_all, BF16), jnp.asarray(mask_all, F32), jnp.asarray(suffix, BF16)


def _split3(x):
    hi = x.astype(BF16)
    r1 = x - hi.astype(F32)
    mid = r1.astype(BF16)
    lo = (r1 - mid.astype(F32)).astype(BF16)
    return jnp.concatenate([hi, mid, lo], axis=1)


def _join3(e):
    n = e.shape[1] // 3
    return e[:, :n] + e[:, n:2 * n] + e[:, 2 * n:]


def _hg_gates(qr, fr, lb):
    sq = _sigmoid(qr)
    sf = _sigmoid(fr)
    forget = lb + (1.0 - lb) * sf
    return qr * sq, sq, sf, forget, jnp.log(forget), 1.0 - forget


def _hg_scores(q, k, expo, masks, nlev, c):
    qb, kb = q.astype(BF16), k.astype(BF16)
    a = masks[nlev] * _dot(qb, kb, NT)
    scaled = []
    for li in range(nlev):
        fq = jnp.exp(expo[(2 * li) * c:(2 * li + 1) * c])
        fk = jnp.exp(expo[(2 * li + 1) * c:(2 * li + 2) * c])
        qs, ks = (q * fq).astype(BF16), (k * fk).astype(BF16)
        a = a + masks[li] * _dot(qs, ks, NT)
        scaled.append((qs, ks, fq, fk))
    return a, scaled, qb, kb


def _hg_fwd(proj, lb_row, gain_row):
    s, d4 = proj.shape
    d = d4 // 4
    nh = d // HG_HEAD_DIM
    c = min(HG_CHUNK, s)
    tb = _pick(s, 512)
    ncb = tb // c
    nlev, m_all, mask_all, _ = _hg_consts(c)
    nrow = m_all.shape[0]

    def body(q_ref, f_ref, i_ref, g_ref, lb_ref, gain_ref, mall_ref, mask_ref, y_ref, o_ref, st_out_ref, st_ref):
        b = pl.program_id(1)

        @pl.when(b == 0)
        def _():
            st_ref[...] = jnp.zeros_like(st_ref)

        lb = lb_ref[...]
        gain = gain_ref[...]

        def chunk(ci, carry):
            off = pl.multiple_of(ci * c, c)
            rows = pl.ds(off, c)
            q, _, _, _, lf, k = _hg_gates(q_ref[rows, :], f_ref[rows, :], lb)
            v = i_ref[rows, :].astype(BF16)
            expo = _join3(_dot(mall_ref[...], _split3(lf)))
            masks = mask_ref[...]
            st = st_ref[...]
            st_out_ref[ci] = st
            a, _, _, _ = _hg_scores(q, k, expo, masks, nlev, c)
            b_cum = expo[2 * nlev * c:(2 * nlev + 1) * c]
            e_tail = expo[(2 * nlev + 1) * c:(2 * nlev + 2) * c]
            q_in = (q * jnp.exp(b_cum)).astype(BF16)
            o = _dot(q_in, st.astype(BF16), NT) + _dot(a.astype(BF16), v)
            k_dec = (k * jnp.exp(e_tail)).astype(BF16)
            st_ref[...] = st * jnp.exp(b_cum[c - 1:c, :]) + _dot(v, k_dec, TN)
            o_ref[rows, :] = o
            r = lax.rsqrt(jnp.mean(o * o, axis=-1, keepdims=True) + NORM_EPS)
            y_ref[rows, :] = (o * r * gain * _sigmoid(g_ref[rows, :])).astype(y_ref.dtype)
            return carry

        lax.fori_loop(0, ncb, chunk, 0)

    part = lambda k: pl.BlockSpec((tb, HG_HEAD_DIM), lambda h, b: (b, k * nh + h))
    head_row = pl.BlockSpec((1, HG_HEAD_DIM), lambda h, b: (0, h))
    tok = pl.BlockSpec((tb, HG_HEAD_DIM), lambda h, b: (b, h))
    return pl.pallas_call(
        body, grid=(nh, s // tb),
        in_specs=[part(0), part(1), part(2), part(3), head_row,
                  pl.BlockSpec((1, HG_HEAD_DIM), lambda h, b: (0, 0)),
                  pl.BlockSpec((nrow, c), lambda h, b: (0, 0)),
                  pl.BlockSpec((nlev + 1, c, c), lambda h, b: (0, 0, 0))],
        out_specs=[tok, tok, pl.BlockSpec((ncb, None, HG_HEAD_DIM, HG_HEAD_DIM), lambda h, b: (b, h, 0, 0))],
        out_shape=[jax.ShapeDtypeStruct((s, d), BF16), jax.ShapeDtypeStruct((s, d), F32),
                   jax.ShapeDtypeStruct((s // c, nh, HG_HEAD_DIM, HG_HEAD_DIM), F32)],
        scratch_shapes=[pltpu.VMEM((HG_HEAD_DIM, HG_HEAD_DIM), F32)],
        compiler_params=_params(("parallel", "arbitrary")), name="hg_fwd",
    )(proj, proj, proj, proj, lb_row, gain_row, m_all, mask_all)


def _hg_bwd(proj, lb_row, gain_row, o_saved, states, dy):
    s, d4 = proj.shape
    d = d4 // 4
    nh = d // HG_HEAD_DIM
    c = min(HG_CHUNK, s)
    tb = _pick(s, 512)
    ncb = tb // c
    nb = s // tb
    nlev, m_all, mask_all, suffix = _hg_consts(c)
    nrow = m_all.shape[0]

    def body(q_ref, f_ref, i_ref, g_ref, lb_ref, gain_ref, o_ref, st_in_ref, dy_ref, mall_ref, mask_ref, suf_ref,
             dq_ref, df_ref, di_ref, dg_ref, dlb_ref, dgain_ref, dst_ref, run_ref):
        b = pl.program_id(1)

        @pl.when(b == 0)
        def _():
            dst_ref[...] = jnp.zeros_like(dst_ref)
            run_ref[...] = jnp.zeros_like(run_ref)
            dlb_ref[...] = jnp.zeros_like(dlb_ref)
            dgain_ref[...] = jnp.zeros_like(dgain_ref)

        lb = lb_ref[...]
        gain = gain_ref[...]

        def chunk(it, carry):
            ci = ncb - 1 - it
            off = pl.multiple_of(ci * c, c)
            rows = pl.ds(off, c)
            qr, fr = q_ref[rows, :], f_ref[rows, :]
            q, sq, sf, forget, lf, k = _hg_gates(qr, fr, lb)
            v = i_ref[rows, :].astype(BF16)
            expo = _join3(_dot(mall_ref[...], _split3(lf)))
            masks = mask_ref[...]
            o = o_ref[rows, :]
            dyv = dy_ref[rows, :]
            sg = _sigmoid(g_ref[rows, :])
            r = lax.rsqrt(jnp.mean(o * o, axis=-1, keepdims=True) + NORM_EPS)
            ohat = o * r
            dyn = dyv * sg
            dg_ref[rows, :] = (dyv * ohat * gain * sg * (1.0 - sg)).astype(dg_ref.dtype)
            dgain_ref[...] += jnp.sum(dyn * ohat, axis=0, keepdims=True)
            dohat = dyn * gain
            do = (r * (dohat - ohat * jnp.mean(dohat * ohat, axis=-1, keepdims=True))).astype(BF16)
            dst = dst_ref[...]
            dstb = dst.astype(BF16)
            a, scaled, qb, kb = _hg_scores(q, k, expo, masks, nlev, c)
            f_cum = jnp.exp(expo[2 * nlev * c:(2 * nlev + 1) * c])
            f_tail = jnp.exp(expo[(2 * nlev + 1) * c:(2 * nlev + 2) * c])
            q_in = (q * f_cum).astype(BF16)
            k_dec = (k * f_tail).astype(BF16)
            t_in = _join3(_dot(do, _split3(st_in_ref[ci])))
            t_st = _join3(_dot(v, _split3(dst)))
            da = _dot(do, v, NT)
            dam = (masks[nlev] * da).astype(BF16)
            dq = t_in * f_cum + _dot(dam, kb)
            dk = t_st * f_tail + _dot(dam, qb, TN)
            db = q_in.astype(F32) * t_in - k_dec.astype(F32) * t_st
            for li in range(nlev):
                qs, ks, fq, fk = scaled[li]
                dam = (masks[li] * da).astype(BF16)
                t_q = _dot(dam, ks)
                t_k = _dot(dam, qs, TN)
                dq = dq + t_q * fq
                dk = dk + t_k * fk
                db = db + (qs.astype(F32) * t_q - ks.astype(F32) * t_k)
            dv = _dot(a.astype(BF16), do, TN) + _dot(k_dec, dstb, NT)
            dst_ref[...] = dst * f_cum[c - 1:c, :] + _dot(do, q_in, TN)
            dlf = _join3(_dot(suf_ref[...], _split3(db))) + run_ref[...]
            run_ref[...] = dlf[0:1, :]
            dforget = dlf / forget - dk
            dlb_ref[...] += jnp.sum(dforget * (1.0 - sf), axis=0, keepdims=True)
            df_ref[rows, :] = (dforget * (1.0 - lb) * sf * (1.0 - sf)).astype(df_ref.dtype)
            dq_ref[rows, :] = (dq * sq * (1.0 + qr * (1.0 - sq))).astype(dq_ref.dtype)
            di_ref[rows, :] = dv.astype(di_ref.dtype)
            return carry

        lax.fori_loop(0, ncb, chunk, 0)

    part = lambda k: pl.BlockSpec((tb, HG_HEAD_DIM), lambda h, b: (nb - 1 - b, k * nh + h))
    head_row = pl.BlockSpec((1, HG_HEAD_DIM), lambda h, b: (0, h))
    tok = pl.BlockSpec((tb, HG_HEAD_DIM), lambda h, b: (nb - 1 - b, h))
    const2 = lambda shape: pl.BlockSpec(shape, lambda h, b: (0, 0))
    return pl.pallas_call(
        body, grid=(nh, nb),
        in_specs=[part(0), part(1), part(2), part(3), head_row, const2((1, HG_HEAD_DIM)), tok,
                  pl.BlockSpec((ncb, None, HG_HEAD_DIM, HG_HEAD_DIM), lambda h, b: (nb - 1 - b, h, 0, 0)),
                  tok, const2((nrow, c)), pl.BlockSpec((nlev + 1, c, c), lambda h, b: (0, 0, 0)), const2((c, c))],
        out_specs=[tok, tok, tok, tok, head_row, head_row],
        out_shape=[jax.ShapeDtypeStruct((s, d), BF16)] * 4 + [jax.ShapeDtypeStruct((1, d), F32)] * 2,
        scratch_shapes=[pltpu.VMEM((HG_HEAD_DIM, HG_HEAD_DIM), F32), pltpu.VMEM((1, HG_HEAD_DIM), F32)],
        compiler_params=_params(("parallel", "arbitrary")), name="hg_bwd",
    )(proj, proj, proj, proj, lb_row, gain_row, o_saved, states, dy, m_all, mask_all, suffix)


def _lb_fwd(logits):
    n, d = logits.shape

    def body(l_ref, lb_ref, p_ref):
        rows = [l_ref[i:i + 1, :] for i in range(n)]
        m = functools.reduce(jnp.maximum, rows)
        es = [jnp.exp(r - m) for r in rows]
        tot = functools.reduce(lambda a, b: a + b, es)
        ps = [e / tot for e in es]
        run = jnp.zeros_like(ps[0])
        for i in range(n):
            run = run + ps[i]
            lb_ref[i:i + 1, :] = run - ps[0]
            p_ref[i:i + 1, :] = ps[i]

    return pl.pallas_call(
        body, out_shape=[jax.ShapeDtypeStruct((n, d), F32)] * 2, name="lb_fwd",
    )(logits)


def _lb_bwd(p, dlb):
    n, d = p.shape

    def body(p_ref, dlb_ref, dl_ref):
        ps = [p_ref[i:i + 1, :] for i in range(n)]
        ds = [dlb_ref[i:i + 1, :] for i in range(n)]
        total = functools.reduce(lambda a, b: a + b, ds)
        dps = []
        for i in range(n):
            dp = functools.reduce(lambda a, b: a + b, ds[i:])
            dps.append(dp - total if i == 0 else dp)
        inner = functools.reduce(lambda a, b: a + b, [pi * di for pi, di in zip(ps, dps)])
        for i in range(n):
            dl_ref[i:i + 1, :] = ps[i] * (dps[i] - inner)

    return pl.pallas_call(body, out_shape=jax.ShapeDtypeStruct((n, d), F32), name="lb_bwd")(p, dlb)


def _as2d(a):
    return a.reshape(-1, a.shape[-1])


def _adamw(w, m, v, grads):
    shape = w.shape
    w2, m2, v2 = _as2d(w), _as2d(m), _as2d(v)
    g2 = [_as2d(g) for g in grads]
    rows, cols = w2.shape
    tr = _pick(rows, 512)
    ng = len(g2)
    bc1 = 1.0 - ADAM_B1 ** ADAM_STEP
    bc2 = 1.0 - ADAM_B2 ** ADAM_STEP

    def body(w_ref, m_ref, v_ref, *rest):
        g = rest[0][...]
        for extra in rest[1:ng]:
            g = g + extra[...]
        g_out, d_out, m_out, v_out = rest[ng:]
        mn = ADAM_B1 * m_ref[...] + (1.0 - ADAM_B1) * g
        vn = ADAM_B2 * v_ref[...] + (1.0 - ADAM_B2) * (g * g)
        m_hat = mn / bc1
        v_hat = vn / bc2
        g_out[...] = g
        d_out[...] = -ADAM_LR * (m_hat / (jnp.sqrt(v_hat) + ADAM_EPS) + ADAM_WD * w_ref[...])
        m_out[...] = mn
        v_out[...] = vn

    spec = pl.BlockSpec((tr, cols), lambda i: (i, 0))
    outs = pl.pallas_call(
        body, grid=(rows // tr,), in_specs=[spec] * (3 + ng), out_specs=[spec] * 4,
        out_shape=[jax.ShapeDtypeStruct((rows, cols), F32)] * 4,
        compiler_params=_params(("parallel",)), name="adamw",
    )(w2, m2, v2, *g2)
    return tuple(o.reshape(shape) for o in outs)


def _sum_slots(parts, recv, chip):
    _, rows, cols = parts.shape
    tr = _pick(rows, 512)

    def body(chip_ref, own_ref, r0_ref, r1_ref, r2_ref, o_ref):
        o_ref[...] = ((own_ref[...] + r0_ref[...]) + r1_ref[...]) + r2_ref[...]

    grid_spec = pltpu.PrefetchScalarGridSpec(
        num_scalar_prefetch=1, grid=(rows // tr,),
        in_specs=[pl.BlockSpec((None, tr, cols), lambda i, chip_ref: (chip_ref[0], i, 0))]
        + [pl.BlockSpec((None, tr, cols), functools.partial(lambda i, chip_ref, k: (k, i, 0), k=k)) for k in range(3)],
        out_specs=pl.BlockSpec((tr, cols), lambda i, chip_ref: (i, 0)))
    return pl.pallas_call(
        body, grid_spec=grid_spec, out_shape=jax.ShapeDtypeStruct((rows, cols), F32),
        compiler_params=_params(("parallel",)), name="sum_slots",
    )(chip, parts, recv, recv, recv)


def _sum_devices(gathered):
    n, rows, cols = gathered.shape

    def body(g_ref, o_ref):
        acc = g_ref[0]
        for i in range(1, n):
            acc = acc + g_ref[i]
        o_ref[...] = acc

    return pl.pallas_call(body, out_shape=jax.ShapeDtypeStruct((rows, cols), F32), name="sum_devices")(gathered)


def _coords():
    return lax.axis_index("x"), lax.axis_index("y"), lax.axis_index("c")


def _chip_peers(x, y, c):
    out = []
    for fx, fy in ((0, 1), (1, 0), (1, 1)):
        px = 1 - x if fx else x
        py = 1 - y if fy else y
        out.append(((px, py, c), 2 * px + py))
    return out


def _gather_chips(shards):
    n = len(shards)

    def body(*refs):
        ins, outs = refs[:n], refs[n:2 * n]
        send_sems, recv_sems, local_sems = refs[2 * n:]
        x, y, c = _coords()
        me = 2 * x + y
        waits = []
        for t in range(n):
            own = pltpu.make_async_copy(ins[t], outs[t].at[me], local_sems.at[t])
            own.start()
            waits.append(own.wait)
            for k, (peer, peer_chip) in enumerate(_chip_peers(x, y, c)):
                sem = 3 * t + k
                send = pltpu.make_async_remote_copy(
                    src_ref=ins[t], dst_ref=outs[t].at[me], send_sem=send_sems.at[sem], recv_sem=recv_sems.at[sem],
                    device_id=peer, device_id_type=MESH)
                send.start()
                recv = pltpu.make_async_remote_copy(
                    src_ref=ins[t], dst_ref=outs[t].at[peer_chip], send_sem=send_sems.at[sem],
                    recv_sem=recv_sems.at[sem], device_id=peer, device_id_type=MESH)
                waits += [send.wait_send, recv.wait_recv]
        for w in waits:
            w()

    return pl.pallas_call(
        body, in_specs=[HBM_SPEC] * n, out_specs=[HBM_SPEC] * n,
        out_shape=[jax.ShapeDtypeStruct((N_CHIPS,) + s.shape, s.dtype) for s in shards],
        scratch_shapes=[pltpu.SemaphoreType.DMA((3 * n,)), pltpu.SemaphoreType.DMA((3 * n,)),
                        pltpu.SemaphoreType.DMA((n,))],
        name="gather_chips",
    )(*shards)


def _scatter_chips(parts):
    n = len(parts)

    def body(*refs):
        ins, outs = refs[:n], refs[n:2 * n]
        send_sems, recv_sems = refs[2 * n:]
        x, y, c = _coords()
        copies = []
        for t in range(n):
            for k, (peer, peer_chip) in enumerate(_chip_peers(x, y, c)):
                sem = 3 * t + k
                cp = pltpu.make_async_remote_copy(
                    src_ref=ins[t].at[peer_chip], dst_ref=outs[t].at[k], send_sem=send_sems.at[sem],
                    recv_sem=recv_sems.at[sem], device_id=peer, device_id_type=MESH)
                cp.start()
                copies.append(cp)
        for cp in copies:
            cp.wait()

    return pl.pallas_call(
        body, in_specs=[HBM_SPEC] * n, out_specs=[HBM_SPEC] * n,
        out_shape=[jax.ShapeDtypeStruct((3,) + p.shape[1:], p.dtype) for p in parts],
        scratch_shapes=[pltpu.SemaphoreType.DMA((3 * n,)), pltpu.SemaphoreType.DMA((3 * n,))],
        name="scatter_chips",
    )(*parts)


def _swap_cores(arrs):
    n = len(arrs)

    def body(*refs):
        ins, outs = refs[:n], refs[n:2 * n]
        send_sems, recv_sems = refs[2 * n:]
        x, y, c = _coords()
        copies = []
        for t in range(n):
            cp = pltpu.make_async_remote_copy(
                src_ref=ins[t], dst_ref=outs[t], send_sem=send_sems.at[t], recv_sem=recv_sems.at[t],
                device_id=(x, y, 1 - c), device_id_type=MESH)
            cp.start()
            copies.append(cp)
        for cp in copies:
            cp.wait()

    return pl.pallas_call(
        body, in_specs=[HBM_SPEC] * n, out_specs=[HBM_SPEC] * n,
        out_shape=[jax.ShapeDtypeStruct(a.shape, a.dtype) for a in arrs],
        scratch_shapes=[pltpu.SemaphoreType.DMA((n,)), pltpu.SemaphoreType.DMA((n,))],
        name="swap_cores",
    )(*arrs)


def _gather_devices(a):
    def body(in_ref, out_ref, send_sems, recv_sems, local_sem):
        x, y, c = _coords()
        me = 4 * x + 2 * y + c
        own = pltpu.make_async_copy(in_ref, out_ref.at[me], local_sem)
        own.start()
        waits = [own.wait]
        for k in range(1, N_DEVICES):
            px = 1 - x if k & 4 else x
            py = 1 - y if k & 2 else y
            pc = 1 - c if k & 1 else c
            peer = (px, py, pc)
            send = pltpu.make_async_remote_copy(
                src_ref=in_ref, dst_ref=out_ref.at[me], send_sem=send_sems.at[k - 1], recv_sem=recv_sems.at[k - 1],
                device_id=peer, device_id_type=MESH)
            send.start()
            recv = pltpu.make_async_remote_copy(
                src_ref=in_ref, dst_ref=out_ref.at[4 * px + 2 * py + pc], send_sem=send_sems.at[k - 1],
                recv_sem=recv_sems.at[k - 1], device_id=peer, device_id_type=MESH)
            waits += [send.wait_send, recv.wait_recv]
        for w in waits:
            w()

    return pl.pallas_call(
        body, in_specs=[HBM_SPEC], out_specs=HBM_SPEC,
        out_shape=jax.ShapeDtypeStruct((N_DEVICES,) + a.shape, a.dtype),
        scratch_shapes=[pltpu.SemaphoreType.DMA((N_DEVICES - 1,)), pltpu.SemaphoreType.DMA((N_DEVICES - 1,)),
                        pltpu.SemaphoreType.DMA],
        name="gather_devices",
    )(a)


def _mlp_grad_epilogue(r, u):
    return r * (2.0 * jnp.maximum(u, 0.0))


def kernel(x, norm_gains, sb_w_qkv, sb_q_gain, sb_k_gain, sb_w_o, hg_w_in, hg_lb_logits, hg_norm_gain, hg_w_o, mlp_w1, mlp_w2, loss_target, m_norm_gains, m_sb_w_qkv, m_sb_q_gain, m_sb_k_gain, m_sb_w_o, m_hg_w_in, m_hg_lb_logits, m_hg_norm_gain, m_hg_w_o, m_mlp_w1, m_mlp_w2, v_norm_gains, v_sb_w_qkv, v_sb_q_gain, v_sb_k_gain, v_sb_w_o, v_hg_w_in, v_hg_lb_logits, v_hg_norm_gain, v_hg_w_o, v_mlp_w1, v_mlp_w2):
    depth = norm_gains.shape[0]
    n_sb, n_hg = sb_w_qkv.shape[0], hg_w_in.shape[0]
    xs, tgt = x[0], loss_target[0]
    s, d = xs.shape
    dq = d // N_CHIPS
    cx, cy, cc = _coords()
    chip = 2 * cx + cy
    chip_arr = jnp.reshape(chip, (1,)).astype(jnp.int32)

    wq_g, wso_g, win_g, who_g, w1_g, w2_g, ng_g, lbl_g = _gather_chips(
        [sb_w_qkv.astype(BF16), sb_w_o.astype(BF16), hg_w_in.astype(BF16), hg_w_o.astype(BF16),
         mlp_w1.astype(BF16), mlp_w2.astype(BF16), norm_gains, hg_lb_logits])
    gains = jnp.transpose(ng_g, (1, 2, 0, 3)).reshape(depth, 2, d)
    logits = jnp.transpose(lbl_g, (1, 0, 2)).reshape(n_hg, d)
    lbs, lb_p = _lb_fwd(logits)
    qg_rows = [jnp.tile(sb_q_gain[j], d // SB_HEAD_DIM)[None] for j in range(n_sb)]
    kg_rows = [jnp.tile(sb_k_gain[j], d // SB_HEAD_DIM)[None] for j in range(n_sb)]

    saved = []
    xc = xs
    for layer in range(depth):
        j = layer // 2
        h1 = _rmsnorm_fwd(xc, gains[layer, 0][None])
        if layer % 2 == 0:
            qkv = _mm_fwd_cols(h1, wq_g, j, name="sb_qkv")
            qn, kn, vb = _qk_norm_fwd(qkv, qg_rows[j], kg_rows[j])
            o = _sb_attn_fwd(qn, kn, vb)
            x_mid = _mm_fwd_rows(o, wso_g, j, residual=xc, name="sb_out")
            mix = (qkv, qn, kn, vb, o)
        else:
            proj = _mm_fwd_cols(h1, win_g, j, name="hg_in")
            y, o, states = _hg_fwd(proj, lbs[j][None], hg_norm_gain[j][None])
            x_mid = _mm_fwd_rows(y, who_g, j, residual=xc, name="hg_out")
            mix = (proj, y, o, states)
        h2 = _rmsnorm_fwd(x_mid, gains[layer, 1][None])
        u = _mm_fwd_cols(h2, w1_g, layer, name="mlp_up")
        x_out = _mm_fwd_rows(u, w2_g, layer, residual=x_mid, a_fn=_relu2, name="mlp_down")
        saved.append((xc, h1, mix, x_mid, h2, u))
        xc = x_out

    sq, dx = _loss_head(xc, tgt)
    loss = lax.psum(jnp.sum(sq) * (0.5 / d), ("x", "y", "c"))

    g_w1, g_w2 = [None] * depth, [None] * depth
    g_qkv, g_so = [None] * n_sb, [None] * n_sb
    g_in, g_ho = [None] * n_hg, [None] * n_hg
    dgains = [[None, None] for _ in range(depth)]
    dqg, dkg = [None] * n_sb, [None] * n_sb
    dhgain, dlb = [None] * n_hg, [None] * n_hg
    for layer in reversed(range(depth)):
        j = layer // 2
        x_in, h1, mix, x_mid, h2, u = saved[layer]
        du = _mm_bwd_rows(dx, w2_g, layer, name="mlp_down_dx", out_dtype=BF16,
                          epi_fn=_mlp_grad_epilogue, epi_args=(u,))
        g_w2[layer] = _mm_dw_rows(u, dx, a_fn=_relu2, name="mlp_down_dw")
        dh2 = _mm_bwd_cols(du, w1_g, layer, name="mlp_up_dx")
        g_w1[layer] = _mm_dw_cols(h2, du, name="mlp_up_dw")
        dx, dgains[layer][1] = _rmsnorm_bwd(dh2, x_mid, gains[layer, 1][None], dx)
        if layer % 2 == 0:
            qkv, qn, kn, vb, o = mix
            do = _mm_bwd_rows(dx, wso_g, j, name="sb_out_dx")
            g_so[j] = _mm_dw_rows(o, dx, name="sb_out_dw")
            dqn, dkn, dv = _sb_attn_bwd(qn, kn, vb, do)
            dqkv, dqg_lane, dkg_lane = _qk_norm_bwd(qkv, qg_rows[j], kg_rows[j], dqn, dkn, dv)
            dqg[j] = jnp.sum(dqg_lane.reshape(-1, SB_HEAD_DIM), axis=0)
            dkg[j] = jnp.sum(dkg_lane.reshape(-1, SB_HEAD_DIM), axis=0)
            dh1 = _mm_bwd_cols(dqkv, wq_g, j, name="sb_qkv_dx")
            g_qkv[j] = _mm_dw_cols(h1, dqkv, name="sb_qkv_dw")
        else:
            proj, y, o, states = mix
            dy = _mm_bwd_rows(dx, who_g, j, name="hg_out_dx")
            g_ho[j] = _mm_dw_rows(y, dx, name="hg_out_dw")
            dq_raw, df_raw, di, dg, dlb_row, dgain_heads = _hg_bwd(
                proj, lbs[j][None], hg_norm_gain[j][None], o, states, dy)
            dlb[j] = dlb_row
            dhgain[j] = jnp.sum(dgain_heads.reshape(-1, HG_HEAD_DIM), axis=0)
            dproj = jnp.concatenate([dq_raw, df_raw, di, dg], axis=1)
            dh1 = _mm_bwd_cols(dproj, win_g, j, name="hg_in_dx")
            g_in[j] = _mm_dw_cols(h1, dproj, name="hg_in_dw")
        dx, dgains[layer][0] = _rmsnorm_bwd(dh1, x_in, gains[layer, 0][None], dx)
    grad_x = dx[None]
    dlogits = _lb_bwd(lb_p, jnp.concatenate(dlb, axis=0))

    big_w = [sb_w_qkv, sb_w_o, hg_w_in, hg_w_o, mlp_w1, mlp_w2]
    big_m = [m_sb_w_qkv, m_sb_w_o, m_hg_w_in, m_hg_w_o, m_mlp_w1, m_mlp_w2]
    big_v = [v_sb_w_qkv, v_sb_w_o, v_hg_w_in, v_hg_w_o, v_mlp_w1, v_mlp_w2]
    parts = [jnp.stack(g, axis=1) for g in (g_qkv, g_so, g_in, g_ho, g_w1, g_w2)]
    parts = [p.reshape(N_CHIPS, -1, p.shape[-1]) for p in parts]
    recv = _scatter_chips(parts)
    chip_sums = [_sum_slots(p, r, chip_arr) for p, r in zip(parts, recv)]
    other_core = _swap_cores(chip_sums)
    big = [_adamw(w, m, v, [a.reshape(w.shape), b.reshape(w.shape)])
           for w, m, v, a, b in zip(big_w, big_m, big_v, chip_sums, other_core)]

    n_small = 2 * depth + n_hg + 3
    small_rows = -(-n_small // 8) * 8
    small = jnp.zeros((small_rows, d), F32)
    small = small.at[0:2 * depth].set(jnp.concatenate([r for pair in dgains for r in pair], axis=0))
    small = small.at[2 * depth:2 * depth + n_hg].set(dlogits)
    base = 2 * depth + n_hg
    small = small.at[base, 0:n_sb * SB_HEAD_DIM].set(jnp.concatenate(dqg))
    small = small.at[base + 1, 0:n_sb * SB_HEAD_DIM].set(jnp.concatenate(dkg))
    small = small.at[base + 2, 0:n_hg * HG_HEAD_DIM].set(jnp.concatenate(dhgain))
    small = _sum_devices(_gather_devices(small))
    my_cols = lambda a: lax.dynamic_slice_in_dim(a, chip * dq, dq, axis=1)
    g_ng = my_cols(small[0:2 * depth]).reshape(norm_gains.shape)
    g_lbl = my_cols(small[2 * depth:base])
    g_qg = small[base, 0:n_sb * SB_HEAD_DIM].reshape(sb_q_gain.shape)
    g_kg = small[base + 1, 0:n_sb * SB_HEAD_DIM].reshape(sb_k_gain.shape)
    g_hgn = small[base + 2, 0:n_hg * HG_HEAD_DIM].reshape(hg_norm_gain.shape)
    r_ng = _adamw(norm_gains, m_norm_gains, v_norm_gains, [g_ng])
    r_qg = _adamw(sb_q_gain, m_sb_q_gain, v_sb_q_gain, [g_qg])
    r_kg = _adamw(sb_k_gain, m_sb_k_gain, v_sb_k_gain, [g_kg])
    r_lbl = _adamw(hg_lb_logits, m_hg_lb_logits, v_hg_lb_logits, [g_lbl])
    r_hgn = _adamw(hg_norm_gain, m_hg_norm_gain, v_hg_norm_gain, [g_hgn])

    per_weight = [r_ng, big[0], r_qg, r_kg, big[1], big[2], r_lbl, r_hgn, big[3], big[4], big[5]]
    outs = [loss, grad_x]
    for field in range(4):
        outs += [r[field] for r in per_weight]
    return tuple(outs)
```

```python
import functools
import math

import numpy as np
import jax
import jax.numpy as jnp
from jax import lax
from jax.experimental import pallas as pl
from jax.experimental.pallas import tpu as pltpu

F32 = jnp.float32
BF16 = jnp.bfloat16

NORM_EPS = 1e-6
SB_HEAD_DIM = 64
HG_HEAD_DIM = 128
HG_CHUNK = 64
LANES = 128
VMEM_LIMIT_BYTES = 56 * 2 ** 20
N_CHIPS = 4
N_DEVICES = 8

ADAM_LR = 0.001
ADAM_B1 = 0.9
ADAM_B2 = 0.999
ADAM_EPS = 1e-08
ADAM_WD = 0.01
ADAM_STEP = 10

MESH = pl.DeviceIdType.MESH
HBM_SPEC = pl.BlockSpec(memory_space=pltpu.HBM)

NN = (((1,), (0,)), ((), ()))
NT = (((1,), (1,)), ((), ()))
TN = (((0,), (0,)), ((), ()))


def _params(sem=None):
    return pltpu.CompilerParams(dimension_semantics=sem, vmem_limit_bytes=VMEM_LIMIT_BYTES)


def _pick(dim, pref):
    for t in (1024, 768, 512, 384, 256, 128, 64, 32, 16, 8):
        if t <= pref and dim % t == 0:
            return t
    return dim


def _dot(a, b, dims=NN):
    return lax.dot_general(a, b, dims, preferred_element_type=F32)


def _sigmoid(x):
    e = jnp.exp(-jnp.abs(x))
    return jnp.where(x >= 0, 1.0, e) / (1.0 + e)


def _matmul(a, b, *, mode, grid, a_block, a_map, b_block, b_map, o_block, o_map, out_shape, out_dtype, name,
            a_fn=None, epi_fn=None, epi_args=()):
    nk = grid[2]
    dims = {"nn": NN, "nt": NT, "tn": TN}[mode]
    n_epi = len(epi_args)

    def body(a_ref, b_ref, *rest):
        epi_refs = rest[:n_epi]
        o_ref = rest[n_epi]
        acc_ref = rest[n_epi + 1]
        kk = pl.program_id(2)

        @pl.when(kk == 0)
        def _():
            acc_ref[...] = jnp.zeros_like(acc_ref)

        av = a_ref[...]
        if a_fn is not None:
            av = a_fn(av)
        acc_ref[...] += _dot(av.astype(BF16), b_ref[...].astype(BF16), dims)

        @pl.when(kk == nk - 1)
        def _():
            r = acc_ref[...]
            if epi_fn is not None:
                r = epi_fn(r, *[e[...] for e in epi_refs])
            o_ref[...] = r.astype(o_ref.dtype)

    acc_shape = tuple(d for d in o_block if d is not None)
    in_specs = [pl.BlockSpec(a_block, a_map), pl.BlockSpec(b_block, b_map)]
    in_specs += [pl.BlockSpec(o_block, o_map) for _ in epi_args]
    return pl.pallas_call(
        body, grid=grid, in_specs=in_specs, out_specs=pl.BlockSpec(o_block, o_map),
        out_shape=jax.ShapeDtypeStruct(out_shape, out_dtype),
        scratch_shapes=[pltpu.VMEM(acc_shape, F32)],
        compiler_params=_params(("parallel", "parallel", "arbitrary")), name=name,
    )(a, b, *epi_args)


def _relu2(u):
    r = jnp.maximum(u, 0.0)
    return r * r


def _add(r, res):
    return r + res


def _mm_fwd_cols(a, wg, *, name):
    s, k = a.shape
    ncs = wg.shape[2]
    tm, tk, tn = _pick(s, 1024), _pick(k, 1024), _pick(ncs, 1024)
    npb = ncs // tn
    return _matmul(a, wg, mode="nn", grid=(s // tm, N_CHIPS * npb, k // tk),
                   a_block=(tm, tk), a_map=lambda i, j, kk: (i, kk),
                   b_block=(None, tk, tn), b_map=lambda i, j, kk: (j // npb, kk, j % npb),
                   o_block=(tm, tn), o_map=lambda i, j, kk: (i, j),
                   out_shape=(s, N_CHIPS * ncs), out_dtype=F32, name=name)


def _mm_fwd_rows(a, wg, *, residual, name, a_fn=None):
    s = a.shape[0]
    krs, n = wg.shape[1], wg.shape[2]
    tm, tk, tn = _pick(s, 1024), _pick(krs, 1024), _pick(n, 1024)
    kpb = krs // tk
    return _matmul(a, wg, mode="nn", grid=(s // tm, n // tn, N_CHIPS * kpb),
                   a_block=(tm, tk), a_map=lambda i, j, kk: (i, kk),
                   b_block=(None, tk, tn), b_map=lambda i, j, kk: (kk // kpb, kk % kpb, j),
                   o_block=(tm, tn), o_map=lambda i, j, kk: (i, j),
                   out_shape=(s, n), out_dtype=F32, name=name, a_fn=a_fn, epi_fn=_add, epi_args=(residual,))


def _mm_bwd_cols(dy, wg, *, name, out_dtype=F32):
    s = dy.shape[0]
    kw, ncs = wg.shape[1], wg.shape[2]
    tm, tn, tk = _pick(s, 1024), _pick(kw, 1024), _pick(ncs, 1024)
    kpb = ncs // tk
    return _matmul(dy, wg, mode="nt", grid=(s // tm, kw // tn, N_CHIPS * kpb),
                   a_block=(tm, tk), a_map=lambda i, j, kk: (i, kk),
                   b_block=(None, tn, tk), b_map=lambda i, j, kk: (kk // kpb, j, kk % kpb),
                   o_block=(tm, tn), o_map=lambda i, j, kk: (i, j),
                   out_shape=(s, kw), out_dtype=out_dtype, name=name)


def _mm_bwd_rows(dy, wg, *, name, out_dtype=F32, epi_fn=None, epi_args=()):
    s, n = dy.shape
    krs = wg.shape[1]
    tm, tn, tk = _pick(s, 1024), _pick(krs, 1024), _pick(n, 1024)
    npb = krs // tn
    return _matmul(dy, wg, mode="nt", grid=(s // tm, N_CHIPS * npb, n // tk),
                   a_block=(tm, tk), a_map=lambda i, j, kk: (i, kk),
                   b_block=(None, tn, tk), b_map=lambda i, j, kk: (j // npb, j % npb, kk),
                   o_block=(tm, tn), o_map=lambda i, j, kk: (i, j),
                   out_shape=(s, N_CHIPS * krs), out_dtype=out_dtype, name=name, epi_fn=epi_fn, epi_args=epi_args)


def _mm_dw_cols(xa, dy, *, name):
    s, kx = xa.shape
    ncs = dy.shape[1] // N_CHIPS
    tm, tn, tk = _pick(kx, 1024), _pick(ncs, 1024), _pick(s, 1024)
    npb = ncs // tn
    return _matmul(xa, dy, mode="tn", grid=(kx // tm, N_CHIPS * npb, s // tk),
                   a_block=(tk, tm), a_map=lambda i, j, kk: (kk, i),
                   b_block=(tk, tn), b_map=lambda i, j, kk: (kk, j),
                   o_block=(None, tm, tn), o_map=lambda i, j, kk: (j // npb, i, j % npb),
                   out_shape=(N_CHIPS, kx, ncs), out_dtype=F32, name=name)


def _mm_dw_rows(xa, dy, *, name, a_fn=None):
    s, n = dy.shape
    krs = xa.shape[1] // N_CHIPS
    tm, tn, tk = _pick(krs, 1024), _pick(n, 1024), _pick(s, 1024)
    mpb = krs // tm
    return _matmul(xa, dy, mode="tn", grid=(N_CHIPS * mpb, n // tn, s // tk),
                   a_block=(tk, tm), a_map=lambda i, j, kk: (kk, i),
                   b_block=(tk, tn), b_map=lambda i, j, kk: (kk, j),
                   o_block=(None, tm, tn), o_map=lambda i, j, kk: (i // mpb, i % mpb, j),
                   out_shape=(N_CHIPS, krs, n), out_dtype=F32, name=name, a_fn=a_fn)


def _rmsnorm_fwd(x, gain_row):
    s, d = x.shape
    ts = _pick(s, 512)

    def body(x_ref, g_ref, h_ref):
        xv = x_ref[...]
        r = lax.rsqrt(jnp.mean(xv * xv, axis=-1, keepdims=True) + NORM_EPS)
        h_ref[...] = (xv * r * g_ref[...]).astype(h_ref.dtype)

    return pl.pallas_call(
        body, grid=(s // ts,),
        in_specs=[pl.BlockSpec((ts, d), lambda i: (i, 0)), pl.BlockSpec((1, d), lambda i: (0, 0))],
        out_specs=pl.BlockSpec((ts, d), lambda i: (i, 0)),
        out_shape=jax.ShapeDtypeStruct((s, d), BF16),
        compiler_params=_params(("parallel",)), name="rmsnorm_fwd",
    )(x, gain_row)


def _rmsnorm_bwd(dh, x, gain_row, dx_res):
    s, d = x.shape
    ts = _pick(s, 512)

    def body(dh_ref, x_ref, g_ref, res_ref, dx_ref, dg_ref):
        i = pl.program_id(0)
        xv = x_ref[...]
        r = lax.rsqrt(jnp.mean(xv * xv, axis=-1, keepdims=True) + NORM_EPS)
        xhat = xv * r
        dh_v = dh_ref[...]
        dxhat = dh_v * g_ref[...]
        dx = r * (dxhat - xhat * jnp.mean(dxhat * xhat, axis=-1, keepdims=True))
        dx_ref[...] = res_ref[...] + dx
        part = jnp.sum(dh_v * xhat, axis=0, keepdims=True)

        @pl.when(i == 0)
        def _():
            dg_ref[...] = part

        @pl.when(i > 0)
        def _():
            dg_ref[...] += part

    return pl.pallas_call(
        body, grid=(s // ts,),
        in_specs=[pl.BlockSpec((ts, d), lambda i: (i, 0)), pl.BlockSpec((ts, d), lambda i: (i, 0)),
                  pl.BlockSpec((1, d), lambda i: (0, 0)), pl.BlockSpec((ts, d), lambda i: (i, 0))],
        out_specs=[pl.BlockSpec((ts, d), lambda i: (i, 0)), pl.BlockSpec((1, d), lambda i: (0, 0))],
        out_shape=[jax.ShapeDtypeStruct((s, d), F32), jax.ShapeDtypeStruct((1, d), F32)],
        compiler_params=_params(("arbitrary",)), name="rmsnorm_bwd",
    )(dh, x, gain_row, dx_res)


def _loss_head(y, target):
    s, d = y.shape
    ts = _pick(s, 512)

    def body(y_ref, t_ref, sq_ref, dy_ref):
        i = pl.program_id(0)
        err = y_ref[...] - t_ref[...]
        dy_ref[...] = err / d
        part = jnp.sum(err * err, axis=0, keepdims=True)

        @pl.when(i == 0)
        def _():
            sq_ref[...] = part

        @pl.when(i > 0)
        def _():
            sq_ref[...] += part

    return pl.pallas_call(
        body, grid=(s // ts,),
        in_specs=[pl.BlockSpec((ts, d), lambda i: (i, 0)), pl.BlockSpec((ts, d), lambda i: (i, 0))],
        out_specs=[pl.BlockSpec((1, d), lambda i: (0, 0)), pl.BlockSpec((ts, d), lambda i: (i, 0))],
        out_shape=[jax.ShapeDtypeStruct((1, d), F32), jax.ShapeDtypeStruct((s, d), F32)],
        compiler_params=_params(("arbitrary",)), name="loss_head",
    )(y, target)


def _pair_mean(val, low_half):
    s0 = jnp.sum(jnp.where(low_half, val, 0.0), axis=-1, keepdims=True)
    s1 = jnp.sum(jnp.where(low_half, 0.0, val), axis=-1, keepdims=True)
    return jnp.where(low_half, s0, s1) * (1.0 / SB_HEAD_DIM)


def _qk_norm_fwd(qkv, qgain_row, kgain_row):
    s, d3 = qkv.shape
    d = d3 // 3
    ts = _pick(s, 512)
    groups = d // LANES

    def body(q_ref, k_ref, v_ref, qg_ref, kg_ref, qn_ref, kn_ref, vb_ref):
        low_half = lax.broadcasted_iota(jnp.int32, (ts, LANES), 1) < SB_HEAD_DIM
        for src, gain, dst in ((q_ref, qg_ref, qn_ref), (k_ref, kg_ref, kn_ref)):
            for p in range(groups):
                cols = slice(p * LANES, (p + 1) * LANES)
                xp = src[:, cols]
                r = lax.rsqrt(_pair_mean(xp * xp, low_half) + NORM_EPS)
                dst[:, cols] = (xp * r * gain[:, cols]).astype(dst.dtype)
        vb_ref[...] = v_ref[...].astype(vb_ref.dtype)

    tok = lambda c: pl.BlockSpec((ts, d), lambda i: (i, c))
    row = pl.BlockSpec((1, d), lambda i: (0, 0))
    return pl.pallas_call(
        body, grid=(s // ts,),
        in_specs=[tok(0), tok(1), tok(2), row, row],
        out_specs=[tok(0), tok(0), tok(0)],
        out_shape=[jax.ShapeDtypeStruct((s, d), BF16)] * 3,
        compiler_params=_params(("parallel",)), name="qk_norm_fwd",
    )(qkv, qkv, qkv, qgain_row, kgain_row)


def _qk_norm_bwd(qkv, qgain_row, kgain_row, dqn, dkn, dv):
    s, d3 = qkv.shape
    d = d3 // 3
    ts = _pick(s, 512)
    groups = d // LANES

    def body(q_ref, k_ref, qg_ref, kg_ref, dqn_ref, dkn_ref, dv_ref, dqkv_ref, dqg_ref, dkg_ref):
        i = pl.program_id(0)
        low_half = lax.broadcasted_iota(jnp.int32, (ts, LANES), 1) < SB_HEAD_DIM
        for which, (src, gain, dsrc, dgain) in enumerate(((q_ref, qg_ref, dqn_ref, dqg_ref),
                                                          (k_ref, kg_ref, dkn_ref, dkg_ref))):
            for p in range(groups):
                cols = slice(p * LANES, (p + 1) * LANES)
                xp = src[:, cols]
                r = lax.rsqrt(_pair_mean(xp * xp, low_half) + NORM_EPS)
                xhat = xp * r
                dy = dsrc[:, cols]
                dxhat = dy * gain[:, cols]
                dx = r * (dxhat - xhat * _pair_mean(dxhat * xhat, low_half))
                dqkv_ref[:, which * d + p * LANES: which * d + (p + 1) * LANES] = dx.astype(dqkv_ref.dtype)
                part = jnp.sum(dy * xhat, axis=0, keepdims=True)

                @pl.when(i == 0)
                def _():
                    dgain[:, cols] = part

                @pl.when(i > 0)
                def _():
                    dgain[:, cols] += part
        dqkv_ref[:, 2 * d:] = dv_ref[...].astype(dqkv_ref.dtype)

    tok = lambda c: pl.BlockSpec((ts, d), lambda i: (i, c))
    row = pl.BlockSpec((1, d), lambda i: (0, 0))
    return pl.pallas_call(
        body, grid=(s // ts,),
        in_specs=[tok(0), tok(1), row, row, tok(0), tok(0), tok(0)],
        out_specs=[pl.BlockSpec((ts, d3), lambda i: (i, 0)), row, row],
        out_shape=[jax.ShapeDtypeStruct((s, d3), BF16), jax.ShapeDtypeStruct((1, d), F32),
                   jax.ShapeDtypeStruct((1, d), F32)],
        compiler_params=_params(("arbitrary",)), name="qk_norm_bwd",
    )(qkv, qkv, qgain_row, kgain_row, dqn, dkn, dv)


def _split2(x):
    hi = x.astype(BF16)
    lo = (x - hi.astype(F32)).astype(BF16)
    return hi, lo


SB_TK = 128


def _sb_consts(tk):
    j = np.arange(tk)
    ones = np.ones((tk, tk), np.float32)
    out = []
    for tri in ((j[:, None] >= j[None, :]), (j[:, None] <= j[None, :])):
        half = np.concatenate([tri.astype(np.float32), ones], axis=1)
        out.append(jnp.asarray(np.concatenate([half, half], axis=0), BF16))
    return out


def _head_stack(blk, low_half):
    f = blk.astype(F32)
    return jnp.concatenate([jnp.where(low_half, f, 0.0), jnp.where(low_half, 0.0, f)], axis=0).astype(BF16)


def _sb_tile_sums(z, valid, tri2):
    e = jnp.exp(-jnp.abs(z))
    lstay = jnp.minimum(-z, 0.0) - jnp.log(1.0 + e)
    if valid is not None:
        lstay = jnp.where(valid, lstay, 0.0)
    hi, lo = _split2(lstay)
    return e, _dot(jnp.concatenate([hi, lo], axis=1), tri2)


def _sb_weights(z, c2, valid, run):
    w = jnp.exp(z + c2[:, :SB_TK] + run)
    return w if valid is None else jnp.where(valid, w, 0.0)


def _sb_attn_fwd(qn, kn, vb, exchange=None):
    s, d = qn.shape
    tk = SB_TK
    tq = _pick(s, 256)
    nq, ndiag = s // tq, tq // tk
    assert tq % (2 * tk) == 0, "tiles below the diagonal are taken two at a time"
    npairs = d // LANES
    scale = 1.0 / math.sqrt(SB_HEAD_DIM)
    tri_ge2, _ = _sb_consts(tk)

    def body(q_ref, k_ref, v_ref, tri_ref, o_ref, acc_ref, run_ref):
        qi = pl.program_id(1)
        low_half = lax.broadcasted_iota(jnp.int32, (tk, LANES), 1) < SB_HEAD_DIM
        row = lax.broadcasted_iota(jnp.int32, (tq, tk), 0)
        col = lax.broadcasted_iota(jnp.int32, (tq, tk), 1)
        qs = (q_ref[...].astype(F32) * scale).astype(BF16)
        acc_ref[...] = jnp.zeros_like(acc_ref)
        run_ref[...] = jnp.zeros_like(run_ref)
        n_full = qi * ndiag

        def sums(kb, dd):
            koff = pl.multiple_of(kb * tk, tk)
            kcat = _head_stack(k_ref[pl.ds(koff, tk), :], low_half)
            vcat = _head_stack(v_ref[pl.ds(koff, tk), :], low_half)
            z2 = _dot(qs, kcat, NT)
            valid = None if dd is None else row > col + dd * tk
            zs = [z2[:, h * tk:(h + 1) * tk] for h in range(2)]
            return zs, [_sb_tile_sums(z, valid, tri_ref[...])[1] for z in zs], valid, vcat

        def finish(zs, c2s, valid, vcat):
            ws = []
            for h in range(2):
                ws.append(_sb_weights(zs[h], c2s[h], valid, run_ref[h]).astype(BF16))
                run_ref[h] += c2s[h][:, tk:]
            acc_ref[...] += _dot(jnp.concatenate(ws, axis=1), vcat)

        for pre in [sums(n_full + dd, dd) for dd in reversed(range(ndiag))]:
            finish(*pre)

        def two_tiles(it, carry):
            kb = n_full - 1 - 2 * it
            first, second = sums(kb, None), sums(kb - 1, None)
            finish(*first)
            finish(*second)
            return carry

        lax.fori_loop(0, n_full // 2, two_tiles, 0)
        o_ref[...] = acc_ref[...]

    blk = pl.BlockSpec((tq, LANES), lambda p, i: (i, p))
    full = pl.BlockSpec((s, LANES), lambda p, i: (0, p))
    (o,), moved = _call_with_exchange(
        body, exchange, grid=(npairs, nq),
        in_specs=[blk, full, full, pl.BlockSpec((2 * tk, 2 * tk), lambda p, i: (0, 0))],
        out_specs=[blk], out_shape=[jax.ShapeDtypeStruct((s, d), F32)],
        scratch_shapes=[pltpu.VMEM((tq, LANES), F32), pltpu.VMEM((2, tq, tk), F32)],
        name="sb_attn_fwd", args=(qn, kn, vb, tri_ge2))
    return o, moved


def _sb_attn_bwd(qn, kn, vb, do, exchange=None):
    s, d = qn.shape
    tk = SB_TK
    tq = _pick(s, 256)
    nq, ndiag = s // tq, tq // tk
    assert tq % (2 * tk) == 0, "tiles below the diagonal are taken two at a time"
    npairs = d // LANES
    scale = 1.0 / math.sqrt(SB_HEAD_DIM)
    tri_ge2, tri_le2 = _sb_consts(tk)

    def body(q_ref, k_ref, v_ref, do_ref, tge_ref, tle_ref, dq_ref, dk_ref, dv_ref,
             g_cache, s_cache, run_ref, dq_acc):
        qi = pl.program_id(1)

        @pl.when(qi == 0)
        def _():
            dk_ref[...] = jnp.zeros_like(dk_ref)
            dv_ref[...] = jnp.zeros_like(dv_ref)

        low_half = lax.broadcasted_iota(jnp.int32, (tk, LANES), 1) < SB_HEAD_DIM
        row = lax.broadcasted_iota(jnp.int32, (tq, tk), 0)
        col = lax.broadcasted_iota(jnp.int32, (tq, tk), 1)
        qs = (q_ref[...].astype(F32) * scale).astype(BF16)
        dob = do_ref[...].astype(BF16)
        n_full = qi * ndiag

        def a_sums(kb, dd):
            koff = pl.multiple_of(kb * tk, tk)
            kcat = _head_stack(k_ref[pl.ds(koff, tk), :], low_half)
            vcat = _head_stack(v_ref[pl.ds(koff, tk), :], low_half)
            z2 = _dot(qs, kcat, NT)
            dw2 = _dot(dob, vcat, NT)
            valid = None if dd is None else row > col + dd * tk
            c2s = []
            for h in range(2):
                cols = slice(h * tk, (h + 1) * tk)
                z = z2[:, cols]
                e, c2 = _sb_tile_sums(z, valid, tge_ref[...])
                s_cache[kb, :, cols] = jnp.where(z >= 0, 1.0, e) / (1.0 + e)
                c2s.append(c2)
            return kb, koff, z2, dw2, c2s, valid

        def a_finish(kb, koff, z2, dw2, c2s, valid):
            ws = []
            for h in range(2):
                cols = slice(h * tk, (h + 1) * tk)
                w = _sb_weights(z2[:, cols], c2s[h], valid, run_ref[h])
                run_ref[h] += c2s[h][:, tk:]
                g_cache[kb, :, cols] = w * dw2[:, cols]
                ws.append(w.astype(BF16))
            dv2 = _dot(jnp.concatenate(ws, axis=1), dob, TN)
            dv_ref[pl.ds(koff, tk), :] += jnp.where(low_half, dv2[:tk], dv2[tk:])

        def b_sums(kb, dd):
            gs = [g_cache[kb, :, h * tk:(h + 1) * tk] for h in range(2)]
            p2s = [_dot(jnp.concatenate(_split2(g), axis=1), tle_ref[...]) for g in gs]
            return kb, gs, p2s, (None if dd is None else row > col + dd * tk)

        def b_finish(kb, gs, p2s, valid):
            koff = pl.multiple_of(kb * tk, tk)
            dzs = []
            for h in range(2):
                dz = gs[h] - s_cache[kb, :, h * tk:(h + 1) * tk] * (p2s[h][:, :tk] + run_ref[h])
                if valid is not None:
                    dz = jnp.where(valid, dz, 0.0)
                run_ref[h] += p2s[h][:, tk:]
                dzs.append(dz.astype(BF16))
            dzcat = jnp.concatenate(dzs, axis=1)
            dq_acc[...] += _dot(dzcat, _head_stack(k_ref[pl.ds(koff, tk), :], low_half))
            dk2 = _dot(dzcat, qs, TN)
            dk_ref[pl.ds(koff, tk), :] += jnp.where(low_half, dk2[:tk], dk2[tk:])

        run_ref[...] = jnp.zeros_like(run_ref)
        for pre in [a_sums(n_full + dd, dd) for dd in reversed(range(ndiag))]:
            a_finish(*pre)

        def two_a(it, carry):
            kb = n_full - 1 - 2 * it
            first, second = a_sums(kb, None), a_sums(kb - 1, None)
            a_finish(*first)
            a_finish(*second)
            return carry

        lax.fori_loop(0, n_full // 2, two_a, 0)

        run_ref[...] = jnp.zeros_like(run_ref)
        dq_acc[...] = jnp.zeros_like(dq_acc)

        def two_b(it, carry):
            first, second = b_sums(2 * it, None), b_sums(2 * it + 1, None)
            b_finish(*first)
            b_finish(*second)
            return carry

        lax.fori_loop(0, n_full // 2, two_b, 0)
        for pre in [b_sums(n_full + dd, dd) for dd in range(ndiag)]:
            b_finish(*pre)
        dq_ref[...] = dq_acc[...] * scale

    blk = pl.BlockSpec((tq, LANES), lambda p, i: (i, p))
    full = pl.BlockSpec((s, LANES), lambda p, i: (0, p))
    tri = pl.BlockSpec((2 * tk, 2 * tk), lambda p, i: (0, 0))
    return _call_with_exchange(
        body, exchange, grid=(npairs, nq),
        in_specs=[blk, full, full, blk, tri, tri],
        out_specs=[blk, full, full],
        out_shape=[jax.ShapeDtypeStruct((s, d), F32)] * 3,
        scratch_shapes=[pltpu.VMEM((s // tk, tq, 2 * tk), F32), pltpu.VMEM((s // tk, tq, 2 * tk), F32),
                        pltpu.VMEM((2, tq, tk), F32), pltpu.VMEM((tq, LANES), F32)],
        name="sb_attn_bwd", args=(qn, kn, vb, do, tri_ge2, tri_le2))


def _hg_consts(c):
    levels = []
    h = c // 2
    while h >= 1:
        levels.append(h)
        h //= 2
    t = np.arange(c)
    j = t[None, :]
    rows, masks = [], []
    for h in levels:
        blk = t // (2 * h)
        mid = blk * 2 * h + h - 1
        second = (t % (2 * h)) >= h
        rows.append(second[:, None] & (j > mid[:, None]) & (j <= t[:, None]))
        rows.append((~second)[:, None] & (j > t[:, None]) & (j <= mid[:, None]))
        masks.append((blk[:, None] == blk[None, :]) & second[:, None] & (~second)[None, :])
    rows.append(j <= t[:, None])
    rows.append(j > t[:, None])
    masks.append(t[:, None] == t[None, :])
    m_all = np.concatenate(rows, axis=0).astype(np.float32)
    mask_all = np.stack(masks, axis=0).astype(np.float32)
    suffix = (t[None, :] >= t[:, None]).astype(np.float32)
    return len(levels), jnp.asarray(m_all, BF16), jnp.asarray(mask_all, F32), jnp.asarray(suffix, BF16)


def _split3(x):
    hi = x.astype(BF16)
    r1 = x - hi.astype(F32)
    mid = r1.astype(BF16)
    lo = (r1 - mid.astype(F32)).astype(BF16)
    return jnp.concatenate([hi, mid, lo], axis=1)


def _join3(e):
    n = e.shape[1] // 3
    return e[:, :n] + e[:, n:2 * n] + e[:, 2 * n:]


def _hg_gates(qr, fr, lb):
    sq = _sigmoid(qr)
    sf = _sigmoid(fr)
    forget = lb + (1.0 - lb) * sf
    return qr * sq, sq, sf, forget, jnp.log(forget), 1.0 - forget


def _hg_scores(q, k, expo, masks, nlev, c):
    qb, kb = q.astype(BF16), k.astype(BF16)
    a = masks[nlev] * _dot(qb, kb, NT)
    scaled = []
    for li in range(nlev):
        fq = jnp.exp(expo[(2 * li) * c:(2 * li + 1) * c])
        fk = jnp.exp(expo[(2 * li + 1) * c:(2 * li + 2) * c])
        qs, ks = (q * fq).astype(BF16), (k * fk).astype(BF16)
        a = a + masks[li] * _dot(qs, ks, NT)
        scaled.append((qs, ks, fq, fk))
    return a, scaled, qb, kb


def _hg_fwd(proj, lb_row, gain_row, exchange=None):
    s, d4 = proj.shape
    d = d4 // 4
    nh = d // HG_HEAD_DIM
    c = min(HG_CHUNK, s)
    tb = _pick(s, 512)
    ncb = tb // c
    nlev, m_all, mask_all, _ = _hg_consts(c)
    nrow = m_all.shape[0]

    def body(q_ref, f_ref, i_ref, g_ref, lb_ref, gain_ref, mall_ref, mask_ref, y_ref, o_ref, st_out_ref, st_ref):
        b = pl.program_id(1)

        @pl.when(b == 0)
        def _():
            st_ref[...] = jnp.zeros_like(st_ref)

        lb = lb_ref[...]
        gain = gain_ref[...]

        def chunk(ci, carry):
            off = pl.multiple_of(ci * c, c)
            rows = pl.ds(off, c)
            q, _, _, _, lf, k = _hg_gates(q_ref[rows, :], f_ref[rows, :], lb)
            v = i_ref[rows, :].astype(BF16)
            expo = _join3(_dot(mall_ref[...], _split3(lf)))
            masks = mask_ref[...]
            st = st_ref[...]
            st_out_ref[ci] = st
            a, _, _, _ = _hg_scores(q, k, expo, masks, nlev, c)
            b_cum = expo[2 * nlev * c:(2 * nlev + 1) * c]
            e_tail = expo[(2 * nlev + 1) * c:(2 * nlev + 2) * c]
            q_in = (q * jnp.exp(b_cum)).astype(BF16)
            o = _dot(q_in, st.astype(BF16), NT) + _dot(a.astype(BF16), v)
            k_dec = (k * jnp.exp(e_tail)).astype(BF16)
            st_ref[...] = st * jnp.exp(b_cum[c - 1:c, :]) + _dot(v, k_dec, TN)
            o_ref[rows, :] = o
            r = lax.rsqrt(jnp.mean(o * o, axis=-1, keepdims=True) + NORM_EPS)
            y_ref[rows, :] = (o * r * gain * _sigmoid(g_ref[rows, :])).astype(y_ref.dtype)
            return carry

        lax.fori_loop(0, ncb, chunk, 0)

    part = lambda k: pl.BlockSpec((tb, HG_HEAD_DIM), lambda h, b: (b, k * nh + h))
    head_row = pl.BlockSpec((1, HG_HEAD_DIM), lambda h, b: (0, h))
    tok = pl.BlockSpec((tb, HG_HEAD_DIM), lambda h, b: (b, h))
    return _call_with_exchange(
        body, exchange, grid=(nh, s // tb),
        in_specs=[part(0), part(1), part(2), part(3), head_row,
                  pl.BlockSpec((1, HG_HEAD_DIM), lambda h, b: (0, 0)),
                  pl.BlockSpec((nrow, c), lambda h, b: (0, 0)),
                  pl.BlockSpec((nlev + 1, c, c), lambda h, b: (0, 0, 0))],
        out_specs=[tok, tok, pl.BlockSpec((ncb, None, HG_HEAD_DIM, HG_HEAD_DIM), lambda h, b: (b, h, 0, 0))],
        out_shape=[jax.ShapeDtypeStruct((s, d), BF16), jax.ShapeDtypeStruct((s, d), F32),
                   jax.ShapeDtypeStruct((s // c, nh, HG_HEAD_DIM, HG_HEAD_DIM), F32)],
        scratch_shapes=[pltpu.VMEM((HG_HEAD_DIM, HG_HEAD_DIM), F32)],
        name="hg_fwd", args=(proj, proj, proj, proj, lb_row, gain_row, m_all, mask_all))


def _hg_bwd(proj, lb_row, gain_row, o_saved, states, dy, exchange=None):
    s, d4 = proj.shape
    d = d4 // 4
    nh = d // HG_HEAD_DIM
    c = min(HG_CHUNK, s)
    tb = _pick(s, 512)
    ncb = tb // c
    nb = s // tb
    nlev, m_all, mask_all, suffix = _hg_consts(c)
    nrow = m_all.shape[0]

    def body(q_ref, f_ref, i_ref, g_ref, lb_ref, gain_ref, o_ref, st_in_ref, dy_ref, mall_ref, mask_ref, suf_ref,
             dq_ref, df_ref, di_ref, dg_ref, dlb_ref, dgain_ref, dst_ref, run_ref):
        b = pl.program_id(1)

        @pl.when(b == 0)
        def _():
            dst_ref[...] = jnp.zeros_like(dst_ref)
            run_ref[...] = jnp.zeros_like(run_ref)
            dlb_ref[...] = jnp.zeros_like(dlb_ref)
            dgain_ref[...] = jnp.zeros_like(dgain_ref)

        lb = lb_ref[...]
        gain = gain_ref[...]

        def chunk(it, carry):
            ci = ncb - 1 - it
            off = pl.multiple_of(ci * c, c)
            rows = pl.ds(off, c)
            qr, fr = q_ref[rows, :], f_ref[rows, :]
            q, sq, sf, forget, lf, k = _hg_gates(qr, fr, lb)
            v = i_ref[rows, :].astype(BF16)
            expo = _join3(_dot(mall_ref[...], _split3(lf)))
            masks = mask_ref[...]
            o = o_ref[rows, :]
            dyv = dy_ref[rows, :]
            sg = _sigmoid(g_ref[rows, :])
            r = lax.rsqrt(jnp.mean(o * o, axis=-1, keepdims=True) + NORM_EPS)
            ohat = o * r
            dyn = dyv * sg
            dg_ref[rows, :] = (dyv * ohat * gain * sg * (1.0 - sg)).astype(dg_ref.dtype)
            dgain_ref[...] += jnp.sum(dyn * ohat, axis=0, keepdims=True)
            dohat = dyn * gain
            do = (r * (dohat - ohat * jnp.mean(dohat * ohat, axis=-1, keepdims=True))).astype(BF16)
            dst = dst_ref[...]
            dstb = dst.astype(BF16)
            a, scaled, qb, kb = _hg_scores(q, k, expo, masks, nlev, c)
            f_cum = jnp.exp(expo[2 * nlev * c:(2 * nlev + 1) * c])
            f_tail = jnp.exp(expo[(2 * nlev + 1) * c:(2 * nlev + 2) * c])
            q_in = (q * f_cum).astype(BF16)
            k_dec = (k * f_tail).astype(BF16)
            t_in = _join3(_dot(do, _split3(st_in_ref[ci])))
            t_st = _join3(_dot(v, _split3(dst)))
            da = _dot(do, v, NT)
            dam = (masks[nlev] * da).astype(BF16)
            dq = t_in * f_cum + _dot(dam, kb)
            dk = t_st * f_tail + _dot(dam, qb, TN)
            db = q_in.astype(F32) * t_in - k_dec.astype(F32) * t_st
            for li in range(nlev):
                qs, ks, fq, fk = scaled[li]
                dam = (masks[li] * da).astype(BF16)
                t_q = _dot(dam, ks)
                t_k = _dot(dam, qs, TN)
                dq = dq + t_q * fq
                dk = dk + t_k * fk
                db = db + (qs.astype(F32) * t_q - ks.astype(F32) * t_k)
            dv = _dot(a.astype(BF16), do, TN) + _dot(k_dec, dstb, NT)
            dst_ref[...] = dst * f_cum[c - 1:c, :] + _dot(do, q_in, TN)
            dlf = _join3(_dot(suf_ref[...], _split3(db))) + run_ref[...]
            run_ref[...] = dlf[0:1, :]
            dforget = dlf / forget - dk
            dlb_ref[...] += jnp.sum(dforget * (1.0 - sf), axis=0, keepdims=True)
            df_ref[rows, :] = (dforget * (1.0 - lb) * sf * (1.0 - sf)).astype(df_ref.dtype)
            dq_ref[rows, :] = (dq * sq * (1.0 + qr * (1.0 - sq))).astype(dq_ref.dtype)
            di_ref[rows, :] = dv.astype(di_ref.dtype)
            return carry

        lax.fori_loop(0, ncb, chunk, 0)

    part = lambda k: pl.BlockSpec((tb, HG_HEAD_DIM), lambda h, b: (nb - 1 - b, k * nh + h))
    head_row = pl.BlockSpec((1, HG_HEAD_DIM), lambda h, b: (0, h))
    tok = pl.BlockSpec((tb, HG_HEAD_DIM), lambda h, b: (nb - 1 - b, h))
    const2 = lambda shape: pl.BlockSpec(shape, lambda h, b: (0, 0))
    return _call_with_exchange(
        body, exchange, grid=(nh, nb),
        in_specs=[part(0), part(1), part(2), part(3), head_row, const2((1, HG_HEAD_DIM)), tok,
                  pl.BlockSpec((ncb, None, HG_HEAD_DIM, HG_HEAD_DIM), lambda h, b: (nb - 1 - b, h, 0, 0)),
                  tok, const2((nrow, c)), pl.BlockSpec((nlev + 1, c, c), lambda h, b: (0, 0, 0)), const2((c, c))],
        out_specs=[tok, tok, tok, tok, head_row, head_row],
        out_shape=[jax.ShapeDtypeStruct((s, d), BF16)] * 4 + [jax.ShapeDtypeStruct((1, d), F32)] * 2,
        scratch_shapes=[pltpu.VMEM((HG_HEAD_DIM, HG_HEAD_DIM), F32), pltpu.VMEM((1, HG_HEAD_DIM), F32)],
        name="hg_bwd", args=(proj, proj, proj, proj, lb_row, gain_row, o_saved, states, dy, m_all, mask_all, suffix))


def _lb_fwd(logits):
    n, d = logits.shape

    def body(l_ref, lb_ref, p_ref):
        rows = [l_ref[i:i + 1, :] for i in range(n)]
        m = functools.reduce(jnp.maximum, rows)
        es = [jnp.exp(r - m) for r in rows]
        tot = functools.reduce(lambda a, b: a + b, es)
        ps = [e / tot for e in es]
        run = jnp.zeros_like(ps[0])
        for i in range(n):
            run = run + ps[i]
            lb_ref[i:i + 1, :] = run - ps[0]
            p_ref[i:i + 1, :] = ps[i]

    return pl.pallas_call(
        body, out_shape=[jax.ShapeDtypeStruct((n, d), F32)] * 2, name="lb_fwd",
    )(logits)


def _lb_bwd(p, dlb):
    n, d = p.shape

    def body(p_ref, dlb_ref, dl_ref):
        ps = [p_ref[i:i + 1, :] for i in range(n)]
        ds = [dlb_ref[i:i + 1, :] for i in range(n)]
        total = functools.reduce(lambda a, b: a + b, ds)
        dps = []
        for i in range(n):
            dp = functools.reduce(lambda a, b: a + b, ds[i:])
            dps.append(dp - total if i == 0 else dp)
        inner = functools.reduce(lambda a, b: a + b, [pi * di for pi, di in zip(ps, dps)])
        for i in range(n):
            dl_ref[i:i + 1, :] = ps[i] * (dps[i] - inner)

    return pl.pallas_call(body, out_shape=jax.ShapeDtypeStruct((n, d), F32), name="lb_bwd")(p, dlb)


def _as2d(a):
    return a.reshape(-1, a.shape[-1])


def _adamw(w, m, v, grads):
    shape = w.shape
    w2, m2, v2 = _as2d(w), _as2d(m), _as2d(v)
    g2 = [_as2d(g) for g in grads]
    rows, cols = w2.shape
    tr = _pick(rows, 512)
    ng = len(g2)
    bc1 = 1.0 - ADAM_B1 ** ADAM_STEP
    bc2 = 1.0 - ADAM_B2 ** ADAM_STEP

    def body(w_ref, m_ref, v_ref, *rest):
        g = rest[0][...]
        for extra in rest[1:ng]:
            g = g + extra[...]
        g_out, d_out, m_out, v_out = rest[ng:]
        mn = ADAM_B1 * m_ref[...] + (1.0 - ADAM_B1) * g
        vn = ADAM_B2 * v_ref[...] + (1.0 - ADAM_B2) * (g * g)
        m_hat = mn / bc1
        v_hat = vn / bc2
        g_out[...] = g
        d_out[...] = -ADAM_LR * (m_hat / (jnp.sqrt(v_hat) + ADAM_EPS) + ADAM_WD * w_ref[...])
        m_out[...] = mn
        v_out[...] = vn

    spec = pl.BlockSpec((tr, cols), lambda i: (i, 0))
    outs = pl.pallas_call(
        body, grid=(rows // tr,), in_specs=[spec] * (3 + ng), out_specs=[spec] * 4,
        out_shape=[jax.ShapeDtypeStruct((rows, cols), F32)] * 4,
        compiler_params=_params(("parallel",)), name="adamw",
    )(w2, m2, v2, *g2)
    return tuple(o.reshape(shape) for o in outs)


def _sum_slots(parts, recv, chip):
    _, rows, cols = parts.shape
    tr = _pick(rows, 512)

    def body(chip_ref, own_ref, r0_ref, r1_ref, r2_ref, o_ref):
        o_ref[...] = ((own_ref[...] + r0_ref[...]) + r1_ref[...]) + r2_ref[...]

    grid_spec = pltpu.PrefetchScalarGridSpec(
        num_scalar_prefetch=1, grid=(rows // tr,),
        in_specs=[pl.BlockSpec((None, tr, cols), lambda i, chip_ref: (chip_ref[0], i, 0))]
        + [pl.BlockSpec((None, tr, cols), functools.partial(lambda i, chip_ref, k: (k, i, 0), k=k)) for k in range(3)],
        out_specs=pl.BlockSpec((tr, cols), lambda i, chip_ref: (i, 0)))
    return pl.pallas_call(
        body, grid_spec=grid_spec, out_shape=jax.ShapeDtypeStruct((rows, cols), F32),
        compiler_params=_params(("parallel",)), name="sum_slots",
    )(chip, parts, recv, recv, recv)


def _sum_devices(gathered):
    n, rows, cols = gathered.shape

    def body(g_ref, o_ref):
        acc = g_ref[0]
        for i in range(1, n):
            acc = acc + g_ref[i]
        o_ref[...] = acc

    return pl.pallas_call(body, out_shape=jax.ShapeDtypeStruct((rows, cols), F32), name="sum_devices")(gathered)


def _coords():
    return lax.axis_index("x"), lax.axis_index("y"), lax.axis_index("c")


def _chip_peers(x, y, c):
    out = []
    for fx, fy in ((0, 1), (1, 0), (1, 1)):
        px = 1 - x if fx else x
        py = 1 - y if fy else y
        out.append(((px, py, c), 2 * px + py))
    return out


class _ChipExchange:
    def __init__(self, kind, arrays):
        self.kind, self.arrays, self.n = kind, list(arrays), len(arrays)
        lead = lambda a: (N_CHIPS,) + a.shape if kind == "gather" else (3,) + a.shape[1:]
        self.out_shape = [jax.ShapeDtypeStruct(lead(a), a.dtype) for a in self.arrays]
        self.scratch = [pltpu.SemaphoreType.DMA((3 * self.n,)), pltpu.SemaphoreType.DMA((3 * self.n,)),
                        pltpu.SemaphoreType.DMA((self.n,))]

    def copies(self, ins, outs, send_sems, recv_sems, local_sems):
        x, y, c = _coords()
        me = 2 * x + y
        starts, waits = [], []
        for t in range(self.n):
            if self.kind == "gather":
                own = pltpu.make_async_copy(ins[t], outs[t].at[me], local_sems.at[t])
                starts.append(own.start)
                waits.append(own.wait)
            for k, (peer, peer_chip) in enumerate(_chip_peers(x, y, c)):
                sems = dict(send_sem=send_sems.at[3 * t + k], recv_sem=recv_sems.at[3 * t + k],
                            device_id=peer, device_id_type=MESH)
                if self.kind == "gather":
                    send = pltpu.make_async_remote_copy(src_ref=ins[t], dst_ref=outs[t].at[me], **sems)
                    recv = pltpu.make_async_remote_copy(src_ref=ins[t], dst_ref=outs[t].at[peer_chip], **sems)
                else:
                    send = pltpu.make_async_remote_copy(src_ref=ins[t].at[peer_chip], dst_ref=outs[t].at[k], **sems)
                    recv = send
                starts.append(send.start)
                waits += [send.wait_send, recv.wait_recv]
        return starts, waits

    def run(self, name):
        n = self.n

        def body(*refs):
            starts, waits = self.copies(refs[:n], refs[n:2 * n], *refs[2 * n:])
            for f in starts + waits:
                f()

        return pl.pallas_call(body, in_specs=[HBM_SPEC] * n, out_specs=[HBM_SPEC] * n, out_shape=self.out_shape,
                              scratch_shapes=self.scratch, name=name)(*self.arrays)


def _call_with_exchange(body, exchange, *, grid, in_specs, out_specs, out_shape, scratch_shapes, name, args):
    if exchange is None:
        outs = pl.pallas_call(body, grid=grid, in_specs=in_specs, out_specs=out_specs, out_shape=out_shape,
                              scratch_shapes=scratch_shapes,
                              compiler_params=_params(("parallel",) + ("arbitrary",) * (len(grid) - 1)),
                              name=name)(*args)
        return outs, []
    n_in, n_out, n_scr, n = len(in_specs), len(out_specs), len(scratch_shapes), exchange.n

    def wrapped(*refs):
        ins, ex_in = refs[:n_in], refs[n_in:n_in + n]
        outs = refs[n_in + n:n_in + n + n_out]
        ex_out = refs[n_in + n + n_out:n_in + 2 * n + n_out]
        scr = refs[n_in + 2 * n + n_out:n_in + 2 * n + n_out + n_scr]
        sems = refs[n_in + 2 * n + n_out + n_scr:]
        ids = [pl.program_id(a) for a in range(len(grid))]
        first = functools.reduce(jnp.logical_and, [i == 0 for i in ids])
        last = functools.reduce(jnp.logical_and, [i == g - 1 for i, g in zip(ids, grid)])

        @pl.when(first)
        def _():
            for f in exchange.copies(ex_in, ex_out, *sems)[0]:
                f()

        body(*ins, *outs, *scr)

        @pl.when(last)
        def _():
            for f in exchange.copies(ex_in, ex_out, *sems)[1]:
                f()

    res = pl.pallas_call(
        wrapped, grid=grid, in_specs=list(in_specs) + [HBM_SPEC] * n, out_specs=list(out_specs) + [HBM_SPEC] * n,
        out_shape=list(out_shape) + exchange.out_shape, scratch_shapes=list(scratch_shapes) + exchange.scratch,
        compiler_params=_params(("arbitrary",) * len(grid)), name=name + "_" + exchange.kind,
    )(*args, *exchange.arrays)
    return res[:n_out], res[n_out:]


def _swap_cores(arrs):
    n = len(arrs)

    def body(*refs):
        ins, outs = refs[:n], refs[n:2 * n]
        send_sems, recv_sems = refs[2 * n:]
        x, y, c = _coords()
        copies = []
        for t in range(n):
            cp = pltpu.make_async_remote_copy(
                src_ref=ins[t], dst_ref=outs[t], send_sem=send_sems.at[t], recv_sem=recv_sems.at[t],
                device_id=(x, y, 1 - c), device_id_type=MESH)
            cp.start()
            copies.append(cp)
        for cp in copies:
            cp.wait()

    return pl.pallas_call(
        body, in_specs=[HBM_SPEC] * n, out_specs=[HBM_SPEC] * n,
        out_shape=[jax.ShapeDtypeStruct(a.shape, a.dtype) for a in arrs],
        scratch_shapes=[pltpu.SemaphoreType.DMA((n,)), pltpu.SemaphoreType.DMA((n,))],
        name="swap_cores",
    )(*arrs)


def _gather_devices(a):
    def body(in_ref, out_ref, send_sems, recv_sems, local_sem):
        x, y, c = _coords()
        me = 4 * x + 2 * y + c
        own = pltpu.make_async_copy(in_ref, out_ref.at[me], local_sem)
        own.start()
        waits = [own.wait]
        for k in range(1, N_DEVICES):
            px = 1 - x if k & 4 else x
            py = 1 - y if k & 2 else y
            pc = 1 - c if k & 1 else c
            peer = (px, py, pc)
            send = pltpu.make_async_remote_copy(
                src_ref=in_ref, dst_ref=out_ref.at[me], send_sem=send_sems.at[k - 1], recv_sem=recv_sems.at[k - 1],
                device_id=peer, device_id_type=MESH)
            send.start()
            recv = pltpu.make_async_remote_copy(
                src_ref=in_ref, dst_ref=out_ref.at[4 * px + 2 * py + pc], send_sem=send_sems.at[k - 1],
                recv_sem=recv_sems.at[k - 1], device_id=peer, device_id_type=MESH)
            waits += [send.wait_send, recv.wait_recv]
        for w in waits:
            w()

    return pl.pallas_call(
        body, in_specs=[HBM_SPEC], out_specs=HBM_SPEC,
        out_shape=jax.ShapeDtypeStruct((N_DEVICES,) + a.shape, a.dtype),
        scratch_shapes=[pltpu.SemaphoreType.DMA((N_DEVICES - 1,)), pltpu.SemaphoreType.DMA((N_DEVICES - 1,)),
                        pltpu.SemaphoreType.DMA],
        name="gather_devices",
    )(a)


def _mlp_grad_epilogue(r, u):
    return r * (2.0 * jnp.maximum(u, 0.0))


def kernel(x, norm_gains, sb_w_qkv, sb_q_gain, sb_k_gain, sb_w_o, hg_w_in, hg_lb_logits, hg_norm_gain, hg_w_o, mlp_w1, mlp_w2, loss_target, m_norm_gains, m_sb_w_qkv, m_sb_q_gain, m_sb_k_gain, m_sb_w_o, m_hg_w_in, m_hg_lb_logits, m_hg_norm_gain, m_hg_w_o, m_mlp_w1, m_mlp_w2, v_norm_gains, v_sb_w_qkv, v_sb_q_gain, v_sb_k_gain, v_sb_w_o, v_hg_w_in, v_hg_lb_logits, v_hg_norm_gain, v_hg_w_o, v_mlp_w1, v_mlp_w2):
    depth = norm_gains.shape[0]
    n_sb, n_hg = sb_w_qkv.shape[0], hg_w_in.shape[0]
    xs, tgt = x[0], loss_target[0]
    s, d = xs.shape
    dq = d // N_CHIPS
    cx, cy, cc = _coords()
    chip = 2 * cx + cy
    chip_arr = jnp.reshape(chip, (1,)).astype(jnp.int32)

    def mixer_weights(layer):
        j = layer // 2
        return (sb_w_qkv[j], sb_w_o[j]) if layer % 2 == 0 else (hg_w_in[j], hg_w_o[j])

    w_in_g, ng_g, lbl_g = _ChipExchange(
        "gather", [mixer_weights(0)[0].astype(BF16), norm_gains, hg_lb_logits]).run("gather_first")
    gains = jnp.transpose(ng_g, (1, 2, 0, 3)).reshape(depth, 2, d)
    logits = jnp.transpose(lbl_g, (1, 0, 2)).reshape(n_hg, d)
    lbs, lb_p = _lb_fwd(logits)
    qg_rows = [jnp.tile(sb_q_gain[j], d // SB_HEAD_DIM)[None] for j in range(n_sb)]
    kg_rows = [jnp.tile(sb_k_gain[j], d // SB_HEAD_DIM)[None] for j in range(n_sb)]

    saved, wts = [], []
    xc = xs
    for layer in range(depth):
        j = layer // 2
        ahead = [mixer_weights(layer)[1], mlp_w1[layer], mlp_w2[layer]]
        if layer + 1 < depth:
            ahead.append(mixer_weights(layer + 1)[0])
        gather = _ChipExchange("gather", [a.astype(BF16) for a in ahead])
        h1 = _rmsnorm_fwd(xc, gains[layer, 0][None])
        if layer % 2 == 0:
            qkv = _mm_fwd_cols(h1, w_in_g, name="sb_qkv")
            qn, kn, vb = _qk_norm_fwd(qkv, qg_rows[j], kg_rows[j])
            o, moved = _sb_attn_fwd(qn, kn, vb, gather)
            x_mid = _mm_fwd_rows(o, moved[0], residual=xc, name="sb_out")
            mix = (qkv, qn, kn, vb, o)
        else:
            proj = _mm_fwd_cols(h1, w_in_g, name="hg_in")
            (y, o, states), moved = _hg_fwd(proj, lbs[j][None], hg_norm_gain[j][None], gather)
            x_mid = _mm_fwd_rows(y, moved[0], residual=xc, name="hg_out")
            mix = (proj, y, o, states)
        w_out_g, w1_g, w2_g = moved[:3]
        h2 = _rmsnorm_fwd(x_mid, gains[layer, 1][None])
        u = _mm_fwd_cols(h2, w1_g, name="mlp_up")
        x_out = _mm_fwd_rows(u, w2_g, residual=x_mid, a_fn=_relu2, name="mlp_down")
        saved.append((xc, h1, mix, x_mid, h2, u))
        wts.append((w_in_g, w_out_g, w1_g, w2_g))
        w_in_g = moved[3] if layer + 1 < depth else None
        xc = x_out

    sq, dx = _loss_head(xc, tgt)
    loss = lax.psum(jnp.sum(sq) * (0.5 / d), ("x", "y", "c"))

    dgains = [[None, None] for _ in range(depth)]
    dqg, dkg = [None] * n_sb, [None] * n_sb
    dhgain, dlb = [None] * n_hg, [None] * n_hg
    grads, received, pending = {}, {}, []

    def ready(key, parts):
        grads[key] = parts
        pending.append(key)

    def scatter_of(keys):
        return _ChipExchange("scatter", [grads[k] for k in keys]) if keys else None

    def sent(keys, moved):
        for k, r in zip(keys, moved):
            received[k] = r
            pending.remove(k)

    for layer in reversed(range(depth)):
        j = layer // 2
        x_in, h1, mix, x_mid, h2, u = saved[layer]
        w_in_g, w_out_g, w1_g, w2_g = wts[layer]
        du = _mm_bwd_rows(dx, w2_g, name="mlp_down_dx", out_dtype=BF16, epi_fn=_mlp_grad_epilogue, epi_args=(u,))
        ready(("w2", layer), _mm_dw_rows(u, dx, a_fn=_relu2, name="mlp_down_dw"))
        dh2 = _mm_bwd_cols(du, w1_g, name="mlp_up_dx")
        ready(("w1", layer), _mm_dw_cols(h2, du, name="mlp_up_dw"))
        dx, dgains[layer][1] = _rmsnorm_bwd(dh2, x_mid, gains[layer, 1][None], dx)
        if layer % 2 == 0:
            qkv, qn, kn, vb, o = mix
            do = _mm_bwd_rows(dx, w_out_g, name="sb_out_dx")
            ready(("out", layer), _mm_dw_rows(o, dx, name="sb_out_dw"))
            keys = list(pending)
            (dqn, dkn, dv), moved = _sb_attn_bwd(qn, kn, vb, do, scatter_of(keys))
            sent(keys, moved)
            dqkv, dqg_lane, dkg_lane = _qk_norm_bwd(qkv, qg_rows[j], kg_rows[j], dqn, dkn, dv)
            dqg[j] = jnp.sum(dqg_lane.reshape(-1, SB_HEAD_DIM), axis=0)
            dkg[j] = jnp.sum(dkg_lane.reshape(-1, SB_HEAD_DIM), axis=0)
            dh1 = _mm_bwd_cols(dqkv, w_in_g, name="sb_qkv_dx")
            ready(("in", layer), _mm_dw_cols(h1, dqkv, name="sb_qkv_dw"))
        else:
            proj, y, o, states = mix
            dy = _mm_bwd_rows(dx, w_out_g, name="hg_out_dx")
            ready(("out", layer), _mm_dw_rows(y, dx, name="hg_out_dw"))
            keys = list(pending)
            (dq_raw, df_raw, di, dg, dlb_row, dgain_heads), moved = _hg_bwd(
                proj, lbs[j][None], hg_norm_gain[j][None], o, states, dy, scatter_of(keys))
            sent(keys, moved)
            dlb[j] = dlb_row
            dhgain[j] = jnp.sum(dgain_heads.reshape(-1, HG_HEAD_DIM), axis=0)
            dproj = jnp.concatenate([dq_raw, df_raw, di, dg], axis=1)
            dh1 = _mm_bwd_cols(dproj, w_in_g, name="hg_in_dx")
            ready(("in", layer), _mm_dw_cols(h1, dproj, name="hg_in_dw"))
        dx, dgains[layer][0] = _rmsnorm_bwd(dh1, x_in, gains[layer, 0][None], dx)
    keys = list(pending)
    sent(keys, scatter_of(keys).run("scatter_last"))
    grad_x = dx[None]
    dlogits = _lb_bwd(lb_p, jnp.concatenate(dlb, axis=0))

    def chip_sum(kind, layers):
        return jnp.stack([_sum_slots(grads[kind, l], received[kind, l], chip_arr) for l in layers])

    sb_layers, hg_layers = range(0, depth, 2), range(1, depth, 2)
    big_w = [sb_w_qkv, sb_w_o, hg_w_in, hg_w_o, mlp_w1, mlp_w2]
    big_m = [m_sb_w_qkv, m_sb_w_o, m_hg_w_in, m_hg_w_o, m_mlp_w1, m_mlp_w2]
    big_v = [v_sb_w_qkv, v_sb_w_o, v_hg_w_in, v_hg_w_o, v_mlp_w1, v_mlp_w2]
    chip_sums = [chip_sum("in", sb_layers), chip_sum("out", sb_layers), chip_sum("in", hg_layers),
                 chip_sum("out", hg_layers), chip_sum("w1", range(depth)), chip_sum("w2", range(depth))]
    other_core = _swap_cores(chip_sums)
    big = [_adamw(w, m, v, [a, b]) for w, m, v, a, b in zip(big_w, big_m, big_v, chip_sums, other_core)]

    n_small = 2 * depth + n_hg + 3
    small_rows = -(-n_small // 8) * 8
    small = jnp.zeros((small_rows, d), F32)
    small = small.at[0:2 * depth].set(jnp.concatenate([r for pair in dgains for r in pair], axis=0))
    small = small.at[2 * depth:2 * depth + n_hg].set(dlogits)
    base = 2 * depth + n_hg
    small = small.at[base, 0:n_sb * SB_HEAD_DIM].set(jnp.concatenate(dqg))
    small = small.at[base + 1, 0:n_sb * SB_HEAD_DIM].set(jnp.concatenate(dkg))
    small = small.at[base + 2, 0:n_hg * HG_HEAD_DIM].set(jnp.concatenate(dhgain))
    small = _sum_devices(_gather_devices(small))
    my_cols = lambda a: lax.dynamic_slice_in_dim(a, chip * dq, dq, axis=1)
    g_ng = my_cols(small[0:2 * depth]).reshape(norm_gains.shape)
    g_lbl = my_cols(small[2 * depth:base])
    g_qg = small[base, 0:n_sb * SB_HEAD_DIM].reshape(sb_q_gain.shape)
    g_kg = small[base + 1, 0:n_sb * SB_HEAD_DIM].reshape(sb_k_gain.shape)
    g_hgn = small[base + 2, 0:n_hg * HG_HEAD_DIM].reshape(hg_norm_gain.shape)
    r_ng = _adamw(norm_gains, m_norm_gains, v_norm_gains, [g_ng])
    r_qg = _adamw(sb_q_gain, m_sb_q_gain, v_sb_q_gain, [g_qg])
    r_kg = _adamw(sb_k_gain, m_sb_k_gain, v_sb_k_gain, [g_kg])
    r_lbl = _adamw(hg_lb_logits, m_hg_lb_logits, v_hg_lb_logits, [g_lbl])
    r_hgn = _adamw(hg_norm_gain, m_hg_norm_gain, v_hg_norm_gain, [g_hgn])

    per_weight = [r_ng, big[0], r_qg, r_kg, big[1], big[2], r_lbl, r_hgn, big[3], big[4], big[5]]
    outs = [loss, grad_x]
    for field in range(4):
        outs += [r[field] for r in per_weight]
    return tuple(outs)
```

```python
import functools
import math

import numpy as np
import jax
import jax.numpy as jnp
from jax import lax
from jax.experimental import pallas as pl
from jax.experimental.pallas import tpu as pltpu

F32 = jnp.float32
BF16 = jnp.bfloat16

NORM_EPS = 1e-6
SB_HEAD_DIM = 64
HG_HEAD_DIM = 128
HG_CHUNK = 64
LANES = 128
VMEM_LIMIT_BYTES = 56 * 2 ** 20
N_CHIPS = 4
N_DEVICES = 8

ADAM_LR = 0.001
ADAM_B1 = 0.9
ADAM_B2 = 0.999
ADAM_EPS = 1e-08
ADAM_WD = 0.01
ADAM_STEP = 10

MESH = pl.DeviceIdType.MESH
HBM_SPEC = pl.BlockSpec(memory_space=pltpu.HBM)

NN = (((1,), (0,)), ((), ()))
NT = (((1,), (1,)), ((), ()))
TN = (((0,), (0,)), ((), ()))


def _params(sem=None):
    return pltpu.CompilerParams(dimension_semantics=sem, vmem_limit_bytes=VMEM_LIMIT_BYTES)


def _pick(dim, pref):
    for t in (1024, 768, 512, 384, 256, 128, 64, 32, 16, 8):
        if t <= pref and dim % t == 0:
            return t
    return dim


def _dot(a, b, dims=NN):
    return lax.dot_general(a, b, dims, preferred_element_type=F32)


def _sigmoid(x):
    e = jnp.exp(-jnp.abs(x))
    return jnp.where(x >= 0, 1.0, e) / (1.0 + e)


def _matmul(a, b, *, mode, grid, a_block, a_map, b_block, b_map, o_block, o_map, out_shape, out_dtype, name,
            a_fn=None, epi_fn=None, epi_args=()):
    nk = grid[2]
    dims = {"nn": NN, "nt": NT, "tn": TN}[mode]
    n_epi = len(epi_args)

    def body(a_ref, b_ref, *rest):
        epi_refs = rest[:n_epi]
        o_ref = rest[n_epi]
        acc_ref = rest[n_epi + 1]
        kk = pl.program_id(2)

        @pl.when(kk == 0)
        def _():
            acc_ref[...] = jnp.zeros_like(acc_ref)

        av = a_ref[...]
        if a_fn is not None:
            av = a_fn(av)
        acc_ref[...] += _dot(av.astype(BF16), b_ref[...].astype(BF16), dims)

        @pl.when(kk == nk - 1)
        def _():
            r = acc_ref[...]
            if epi_fn is not None:
                r = epi_fn(r, *[e[...] for e in epi_refs])
            o_ref[...] = r.astype(o_ref.dtype)

    acc_shape = tuple(d for d in o_block if d is not None)
    in_specs = [pl.BlockSpec(a_block, a_map), pl.BlockSpec(b_block, b_map)]
    in_specs += [pl.BlockSpec(o_block, o_map) for _ in epi_args]
    return pl.pallas_call(
        body, grid=grid, in_specs=in_specs, out_specs=pl.BlockSpec(o_block, o_map),
        out_shape=jax.ShapeDtypeStruct(out_shape, out_dtype),
        scratch_shapes=[pltpu.VMEM(acc_shape, F32)],
        compiler_params=_params(("parallel", "parallel", "arbitrary")), name=name,
    )(a, b, *epi_args)


def _relu2(u):
    r = jnp.maximum(u, 0.0)
    return r * r


def _add(r, res):
    return r + res


def _mm_fwd_cols(a, wg, *, name):
    s, k = a.shape
    ncs = wg.shape[2]
    tm, tk, tn = _pick(s, 1024), _pick(k, 1024), _pick(ncs, 1024)
    npb = ncs // tn
    return _matmul(a, wg, mode="nn", grid=(s // tm, N_CHIPS * npb, k // tk),
                   a_block=(tm, tk), a_map=lambda i, j, kk: (i, kk),
                   b_block=(None, tk, tn), b_map=lambda i, j, kk: (j // npb, kk, j % npb),
                   o_block=(tm, tn), o_map=lambda i, j, kk: (i, j),
                   out_shape=(s, N_CHIPS * ncs), out_dtype=F32, name=name)


def _mm_fwd_rows(a, wg, *, residual, name, a_fn=None):
    s = a.shape[0]
    krs, n = wg.shape[1], wg.shape[2]
    tm, tk, tn = _pick(s, 1024), _pick(krs, 1024), _pick(n, 1024)
    kpb = krs // tk
    return _matmul(a, wg, mode="nn", grid=(s // tm, n // tn, N_CHIPS * kpb),
                   a_block=(tm, tk), a_map=lambda i, j, kk: (i, kk),
                   b_block=(None, tk, tn), b_map=lambda i, j, kk: (kk // kpb, kk % kpb, j),
                   o_block=(tm, tn), o_map=lambda i, j, kk: (i, j),
                   out_shape=(s, n), out_dtype=F32, name=name, a_fn=a_fn, epi_fn=_add, epi_args=(residual,))


def _mm_bwd_cols(dy, wg, *, name, out_dtype=F32):
    s = dy.shape[0]
    kw, ncs = wg.shape[1], wg.shape[2]
    tm, tn, tk = _pick(s, 1024), _pick(kw, 1024), _pick(ncs, 1024)
    kpb = ncs // tk
    return _matmul(dy, wg, mode="nt", grid=(s // tm, kw // tn, N_CHIPS * kpb),
                   a_block=(tm, tk), a_map=lambda i, j, kk: (i, kk),
                   b_block=(None, tn, tk), b_map=lambda i, j, kk: (kk // kpb, j, kk % kpb),
                   o_block=(tm, tn), o_map=lambda i, j, kk: (i, j),
                   out_shape=(s, kw), out_dtype=out_dtype, name=name)


def _mm_bwd_rows(dy, wg, *, name, out_dtype=F32, epi_fn=None, epi_args=()):
    s, n = dy.shape
    krs = wg.shape[1]
    tm, tn, tk = _pick(s, 1024), _pick(krs, 1024), _pick(n, 1024)
    npb = krs // tn
    return _matmul(dy, wg, mode="nt", grid=(s // tm, N_CHIPS * npb, n // tk),
                   a_block=(tm, tk), a_map=lambda i, j, kk: (i, kk),
                   b_block=(None, tn, tk), b_map=lambda i, j, kk: (j // npb, j % npb, kk),
                   o_block=(tm, tn), o_map=lambda i, j, kk: (i, j),
                   out_shape=(s, N_CHIPS * krs), out_dtype=out_dtype, name=name, epi_fn=epi_fn, epi_args=epi_args)


def _mm_dw_cols(xa, dy, *, name):
    s, kx = xa.shape
    ncs = dy.shape[1] // N_CHIPS
    tm, tn, tk = _pick(kx, 1024), _pick(ncs, 1024), _pick(s, 1024)
    npb = ncs // tn
    return _matmul(xa, dy, mode="tn", grid=(kx // tm, N_CHIPS * npb, s // tk),
                   a_block=(tk, tm), a_map=lambda i, j, kk: (kk, i),
                   b_block=(tk, tn), b_map=lambda i, j, kk: (kk, j),
                   o_block=(None, tm, tn), o_map=lambda i, j, kk: (j // npb, i, j % npb),
                   out_shape=(N_CHIPS, kx, ncs), out_dtype=F32, name=name)


def _mm_dw_rows(xa, dy, *, name, a_fn=None):
    s, n = dy.shape
    krs = xa.shape[1] // N_CHIPS
    tm, tn, tk = _pick(krs, 1024), _pick(n, 1024), _pick(s, 1024)
    mpb = krs // tm
    return _matmul(xa, dy, mode="tn", grid=(N_CHIPS * mpb, n // tn, s // tk),
                   a_block=(tk, tm), a_map=lambda i, j, kk: (kk, i),
                   b_block=(tk, tn), b_map=lambda i, j, kk: (kk, j),
                   o_block=(None, tm, tn), o_map=lambda i, j, kk: (i // mpb, i % mpb, j),
                   out_shape=(N_CHIPS, krs, n), out_dtype=F32, name=name, a_fn=a_fn)


def _rmsnorm_fwd(x, gain_row):
    s, d = x.shape
    ts = _pick(s, 512)

    def body(x_ref, g_ref, h_ref):
        xv = x_ref[...]
        r = lax.rsqrt(jnp.mean(xv * xv, axis=-1, keepdims=True) + NORM_EPS)
        h_ref[...] = (xv * r * g_ref[...]).astype(h_ref.dtype)

    return pl.pallas_call(
        body, grid=(s // ts,),
        in_specs=[pl.BlockSpec((ts, d), lambda i: (i, 0)), pl.BlockSpec((1, d), lambda i: (0, 0))],
        out_specs=pl.BlockSpec((ts, d), lambda i: (i, 0)),
        out_shape=jax.ShapeDtypeStruct((s, d), BF16),
        compiler_params=_params(("parallel",)), name="rmsnorm_fwd",
    )(x, gain_row)


def _rmsnorm_bwd(dh, x, gain_row, dx_res):
    s, d = x.shape
    ts = _pick(s, 512)

    def body(dh_ref, x_ref, g_ref, res_ref, dx_ref, dg_ref):
        i = pl.program_id(0)
        xv = x_ref[...]
        r = lax.rsqrt(jnp.mean(xv * xv, axis=-1, keepdims=True) + NORM_EPS)
        xhat = xv * r
        dh_v = dh_ref[...]
        dxhat = dh_v * g_ref[...]
        dx = r * (dxhat - xhat * jnp.mean(dxhat * xhat, axis=-1, keepdims=True))
        dx_ref[...] = res_ref[...] + dx
        part = jnp.sum(dh_v * xhat, axis=0, keepdims=True)

        @pl.when(i == 0)
        def _():
            dg_ref[...] = part

        @pl.when(i > 0)
        def _():
            dg_ref[...] += part

    return pl.pallas_call(
        body, grid=(s // ts,),
        in_specs=[pl.BlockSpec((ts, d), lambda i: (i, 0)), pl.BlockSpec((ts, d), lambda i: (i, 0)),
                  pl.BlockSpec((1, d), lambda i: (0, 0)), pl.BlockSpec((ts, d), lambda i: (i, 0))],
        out_specs=[pl.BlockSpec((ts, d), lambda i: (i, 0)), pl.BlockSpec((1, d), lambda i: (0, 0))],
        out_shape=[jax.ShapeDtypeStruct((s, d), F32), jax.ShapeDtypeStruct((1, d), F32)],
        compiler_params=_params(("arbitrary",)), name="rmsnorm_bwd",
    )(dh, x, gain_row, dx_res)


def _loss_head(y, target):
    s, d = y.shape
    ts = _pick(s, 512)

    def body(y_ref, t_ref, sq_ref, dy_ref):
        i = pl.program_id(0)
        err = y_ref[...] - t_ref[...]
        dy_ref[...] = err / d
        part = jnp.sum(err * err, axis=0, keepdims=True)

        @pl.when(i == 0)
        def _():
            sq_ref[...] = part

        @pl.when(i > 0)
        def _():
            sq_ref[...] += part

    return pl.pallas_call(
        body, grid=(s // ts,),
        in_specs=[pl.BlockSpec((ts, d), lambda i: (i, 0)), pl.BlockSpec((ts, d), lambda i: (i, 0))],
        out_specs=[pl.BlockSpec((1, d), lambda i: (0, 0)), pl.BlockSpec((ts, d), lambda i: (i, 0))],
        out_shape=[jax.ShapeDtypeStruct((1, d), F32), jax.ShapeDtypeStruct((s, d), F32)],
        compiler_params=_params(("arbitrary",)), name="loss_head",
    )(y, target)


def _pair_mean(val, low_half):
    s0 = jnp.sum(jnp.where(low_half, val, 0.0), axis=-1, keepdims=True)
    s1 = jnp.sum(jnp.where(low_half, 0.0, val), axis=-1, keepdims=True)
    return jnp.where(low_half, s0, s1) * (1.0 / SB_HEAD_DIM)


def _qk_norm_fwd(qkv, qgain_row, kgain_row):
    s, d3 = qkv.shape
    d = d3 // 3
    ts = _pick(s, 512)
    groups = d // LANES

    def body(q_ref, k_ref, v_ref, qg_ref, kg_ref, qn_ref, kn_ref, vb_ref):
        low_half = lax.broadcasted_iota(jnp.int32, (ts, LANES), 1) < SB_HEAD_DIM
        for src, gain, dst in ((q_ref, qg_ref, qn_ref), (k_ref, kg_ref, kn_ref)):
            for p in range(groups):
                cols = slice(p * LANES, (p + 1) * LANES)
                xp = src[:, cols]
                r = lax.rsqrt(_pair_mean(xp * xp, low_half) + NORM_EPS)
                dst[:, cols] = (xp * r * gain[:, cols]).astype(dst.dtype)
        vb_ref[...] = v_ref[...].astype(vb_ref.dtype)

    tok = lambda c: pl.BlockSpec((ts, d), lambda i: (i, c))
    row = pl.BlockSpec((1, d), lambda i: (0, 0))
    return pl.pallas_call(
        body, grid=(s // ts,),
        in_specs=[tok(0), tok(1), tok(2), row, row],
        out_specs=[tok(0), tok(0), tok(0)],
        out_shape=[jax.ShapeDtypeStruct((s, d), BF16)] * 3,
        compiler_params=_params(("parallel",)), name="qk_norm_fwd",
    )(qkv, qkv, qkv, qgain_row, kgain_row)


def _qk_norm_bwd(qkv, qgain_row, kgain_row, dqn, dkn, dv):
    s, d3 = qkv.shape
    d = d3 // 3
    ts = _pick(s, 512)
    groups = d // LANES

    def body(q_ref, k_ref, qg_ref, kg_ref, dqn_ref, dkn_ref, dv_ref, dqkv_ref, dqg_ref, dkg_ref):
        i = pl.program_id(0)
        low_half = lax.broadcasted_iota(jnp.int32, (ts, LANES), 1) < SB_HEAD_DIM
        for which, (src, gain, dsrc, dgain) in enumerate(((q_ref, qg_ref, dqn_ref, dqg_ref),
                                                          (k_ref, kg_ref, dkn_ref, dkg_ref))):
            for p in range(groups):
                cols = slice(p * LANES, (p + 1) * LANES)
                xp = src[:, cols]
                r = lax.rsqrt(_pair_mean(xp * xp, low_half) + NORM_EPS)
                xhat = xp * r
                dy = dsrc[:, cols]
                dxhat = dy * gain[:, cols]
                dx = r * (dxhat - xhat * _pair_mean(dxhat * xhat, low_half))
                dqkv_ref[:, which * d + p * LANES: which * d + (p + 1) * LANES] = dx.astype(dqkv_ref.dtype)
                part = jnp.sum(dy * xhat, axis=0, keepdims=True)

                @pl.when(i == 0)
                def _():
                    dgain[:, cols] = part

                @pl.when(i > 0)
                def _():
                    dgain[:, cols] += part
        dqkv_ref[:, 2 * d:] = dv_ref[...].astype(dqkv_ref.dtype)

    tok = lambda c: pl.BlockSpec((ts, d), lambda i: (i, c))
    row = pl.BlockSpec((1, d), lambda i: (0, 0))
    return pl.pallas_call(
        body, grid=(s // ts,),
        in_specs=[tok(0), tok(1), row, row, tok(0), tok(0), tok(0)],
        out_specs=[pl.BlockSpec((ts, d3), lambda i: (i, 0)), row, row],
        out_shape=[jax.ShapeDtypeStruct((s, d3), BF16), jax.ShapeDtypeStruct((1, d), F32),
                   jax.ShapeDtypeStruct((1, d), F32)],
        compiler_params=_params(("arbitrary",)), name="qk_norm_bwd",
    )(qkv, qkv, qgain_row, kgain_row, dqn, dkn, dv)


def _split2(x):
    hi = x.astype(BF16)
    lo = (x - hi.astype(F32)).astype(BF16)
    return hi, lo


SB_TK = 128


def _sb_consts(tk):
    j = np.arange(tk)
    ones = np.ones((tk, tk), np.float32)
    out = []
    for tri in ((j[:, None] >= j[None, :]), (j[:, None] <= j[None, :])):
        half = np.concatenate([tri.astype(np.float32), ones], axis=1)
        out.append(jnp.asarray(np.concatenate([half, half], axis=0), BF16))
    return out


def _head_stack(blk, low_half):
    f = blk.astype(F32)
    return jnp.concatenate([jnp.where(low_half, f, 0.0), jnp.where(low_half, 0.0, f)], axis=0).astype(BF16)


def _sb_tile_sums(z, valid, tri2):
    e = jnp.exp(-jnp.abs(z))
    lstay = jnp.minimum(-z, 0.0) - jnp.log(1.0 + e)
    if valid is not None:
        lstay = jnp.where(valid, lstay, 0.0)
    hi, lo = _split2(lstay)
    return e, _dot(jnp.concatenate([hi, lo], axis=1), tri2)


def _sb_weights(z, c2, valid, run):
    w = jnp.exp(z + c2[:, :SB_TK] + run)
    return w if valid is None else jnp.where(valid, w, 0.0)


EXP_IS_ZERO_BELOW = -110.0


def _max_row_norm(x):
    f = x.astype(F32)
    return jnp.sqrt(jnp.max(jnp.sum(f * f, axis=-1, keepdims=True)))


def _sb_score_bound(qs, kmax_ref):
    return _max_row_norm(qs) * jnp.max(kmax_ref[...]) * 1.01 + 1.0


def _sb_rest_is_zero(run_ref, bound):
    return jnp.max(jnp.maximum(run_ref[0], run_ref[1])) + bound < EXP_IS_ZERO_BELOW


def _sb_attn_fwd(qn, kn, vb, exchange=None):
    s, d = qn.shape
    tk = SB_TK
    tq = _pick(s, 256)
    nq, ndiag = s // tq, tq // tk
    assert tq % (2 * tk) == 0, "tiles below the diagonal are taken two at a time"
    npairs = d // LANES
    scale = 1.0 / math.sqrt(SB_HEAD_DIM)
    tri_ge2, _ = _sb_consts(tk)

    def body(q_ref, k_ref, v_ref, tri_ref, o_ref, acc_ref, run_ref, kmax_ref):
        qi = pl.program_id(1)

        @pl.when(qi == 0)
        def _():
            kmax_ref[...] = jnp.full(kmax_ref.shape, _max_row_norm(k_ref[...]), F32)

        low_half = lax.broadcasted_iota(jnp.int32, (tk, LANES), 1) < SB_HEAD_DIM
        row = lax.broadcasted_iota(jnp.int32, (tq, tk), 0)
        col = lax.broadcasted_iota(jnp.int32, (tq, tk), 1)
        qs = (q_ref[...].astype(F32) * scale).astype(BF16)
        bound = _sb_score_bound(qs, kmax_ref)
        acc_ref[...] = jnp.zeros_like(acc_ref)
        run_ref[...] = jnp.zeros_like(run_ref)
        n_full = qi * ndiag

        def sums(kb, dd):
            koff = pl.multiple_of(kb * tk, tk)
            kcat = _head_stack(k_ref[pl.ds(koff, tk), :], low_half)
            vcat = _head_stack(v_ref[pl.ds(koff, tk), :], low_half)
            z2 = _dot(qs, kcat, NT)
            valid = None if dd is None else row > col + dd * tk
            zs = [z2[:, h * tk:(h + 1) * tk] for h in range(2)]
            return zs, [_sb_tile_sums(z, valid, tri_ref[...])[1] for z in zs], valid, vcat

        def finish(zs, c2s, valid, vcat):
            ws = []
            for h in range(2):
                ws.append(_sb_weights(zs[h], c2s[h], valid, run_ref[h]).astype(BF16))
                run_ref[h] += c2s[h][:, tk:]
            acc_ref[...] += _dot(jnp.concatenate(ws, axis=1), vcat)

        for pre in [sums(n_full + dd, dd) for dd in reversed(range(ndiag))]:
            finish(*pre)

        def two_tiles(carry):
            it, _ = carry
            kb = n_full - 1 - 2 * it
            first, second = sums(kb, None), sums(kb - 1, None)
            finish(*first)
            finish(*second)
            return it + 1, _sb_rest_is_zero(run_ref, bound)

        lax.while_loop(lambda c: jnp.logical_and(c[0] < n_full // 2, jnp.logical_not(c[1])), two_tiles,
                       (jnp.int32(0), _sb_rest_is_zero(run_ref, bound)))
        o_ref[...] = acc_ref[...]

    blk = pl.BlockSpec((tq, LANES), lambda p, i: (i, p))
    full = pl.BlockSpec((s, LANES), lambda p, i: (0, p))
    (o,), moved = _call_with_exchange(
        body, exchange, grid=(npairs, nq),
        in_specs=[blk, full, full, pl.BlockSpec((2 * tk, 2 * tk), lambda p, i: (0, 0))],
        out_specs=[blk], out_shape=[jax.ShapeDtypeStruct((s, d), F32)],
        scratch_shapes=[pltpu.VMEM((tq, LANES), F32), pltpu.VMEM((2, tq, tk), F32), pltpu.VMEM((8, LANES), F32)],
        name="sb_attn_fwd", args=(qn, kn, vb, tri_ge2))
    return o, moved


def _sb_attn_bwd(qn, kn, vb, do, exchange=None):
    s, d = qn.shape
    tk = SB_TK
    tq = _pick(s, 256)
    nq, ndiag = s // tq, tq // tk
    assert tq % (2 * tk) == 0, "tiles below the diagonal are taken two at a time"
    npairs = d // LANES
    scale = 1.0 / math.sqrt(SB_HEAD_DIM)
    tri_ge2, tri_le2 = _sb_consts(tk)

    def body(q_ref, k_ref, v_ref, do_ref, tge_ref, tle_ref, dq_ref, dk_ref, dv_ref,
             g_cache, s_cache, run_ref, dq_acc, kmax_ref):
        qi = pl.program_id(1)

        @pl.when(qi == 0)
        def _():
            dk_ref[...] = jnp.zeros_like(dk_ref)
            dv_ref[...] = jnp.zeros_like(dv_ref)
            kmax_ref[...] = jnp.full(kmax_ref.shape, _max_row_norm(k_ref[...]), F32)

        low_half = lax.broadcasted_iota(jnp.int32, (tk, LANES), 1) < SB_HEAD_DIM
        row = lax.broadcasted_iota(jnp.int32, (tq, tk), 0)
        col = lax.broadcasted_iota(jnp.int32, (tq, tk), 1)
        qs = (q_ref[...].astype(F32) * scale).astype(BF16)
        bound = _sb_score_bound(qs, kmax_ref)
        dob = do_ref[...].astype(BF16)
        n_full = qi * ndiag

        def a_sums(kb, dd):
            koff = pl.multiple_of(kb * tk, tk)
            kcat = _head_stack(k_ref[pl.ds(koff, tk), :], low_half)
            vcat = _head_stack(v_ref[pl.ds(koff, tk), :], low_half)
            z2 = _dot(qs, kcat, NT)
            dw2 = _dot(dob, vcat, NT)
            valid = None if dd is None else row > col + dd * tk
            c2s = []
            for h in range(2):
                cols = slice(h * tk, (h + 1) * tk)
                z = z2[:, cols]
                e, c2 = _sb_tile_sums(z, valid, tge_ref[...])
                s_cache[kb, :, cols] = jnp.where(z >= 0, 1.0, e) / (1.0 + e)
                c2s.append(c2)
            return kb, koff, z2, dw2, c2s, valid

        def a_finish(kb, koff, z2, dw2, c2s, valid):
            ws = []
            for h in range(2):
                cols = slice(h * tk, (h + 1) * tk)
                w = _sb_weights(z2[:, cols], c2s[h], valid, run_ref[h])
                run_ref[h] += c2s[h][:, tk:]
                g_cache[kb, :, cols] = w * dw2[:, cols]
                ws.append(w.astype(BF16))
            dv2 = _dot(jnp.concatenate(ws, axis=1), dob, TN)
            dv_ref[pl.ds(koff, tk), :] += jnp.where(low_half, dv2[:tk], dv2[tk:])

        def b_sums(kb, dd):
            gs = [g_cache[kb, :, h * tk:(h + 1) * tk] for h in range(2)]
            p2s = [_dot(jnp.concatenate(_split2(g), axis=1), tle_ref[...]) for g in gs]
            return kb, gs, p2s, (None if dd is None else row > col + dd * tk)

        def b_finish(kb, gs, p2s, valid):
            koff = pl.multiple_of(kb * tk, tk)
            dzs = []
            for h in range(2):
                dz = gs[h] - s_cache[kb, :, h * tk:(h + 1) * tk] * (p2s[h][:, :tk] + run_ref[h])
                if valid is not None:
                    dz = jnp.where(valid, dz, 0.0)
                run_ref[h] += p2s[h][:, tk:]
                dzs.append(dz.astype(BF16))
            dzcat = jnp.concatenate(dzs, axis=1)
            dq_acc[...] += _dot(dzcat, _head_stack(k_ref[pl.ds(koff, tk), :], low_half))
            dk2 = _dot(dzcat, qs, TN)
            dk_ref[pl.ds(koff, tk), :] += jnp.where(low_half, dk2[:tk], dk2[tk:])

        run_ref[...] = jnp.zeros_like(run_ref)
        for pre in [a_sums(n_full + dd, dd) for dd in reversed(range(ndiag))]:
            a_finish(*pre)

        def two_a(carry):
            it, _ = carry
            kb = n_full - 1 - 2 * it
            first, second = a_sums(kb, None), a_sums(kb - 1, None)
            a_finish(*first)
            a_finish(*second)
            return it + 1, _sb_rest_is_zero(run_ref, bound)

        trips, _ = lax.while_loop(lambda c: jnp.logical_and(c[0] < n_full // 2, jnp.logical_not(c[1])), two_a,
                                  (jnp.int32(0), _sb_rest_is_zero(run_ref, bound)))

        run_ref[...] = jnp.zeros_like(run_ref)
        dq_acc[...] = jnp.zeros_like(dq_acc)
        kb_first = n_full - 2 * trips

        def two_b(it, carry):
            first, second = b_sums(kb_first + 2 * it, None), b_sums(kb_first + 2 * it + 1, None)
            b_finish(*first)
            b_finish(*second)
            return carry

        lax.fori_loop(0, trips, two_b, 0)
        for pre in [b_sums(n_full + dd, dd) for dd in range(ndiag)]:
            b_finish(*pre)
        dq_ref[...] = dq_acc[...] * scale

    blk = pl.BlockSpec((tq, LANES), lambda p, i: (i, p))
    full = pl.BlockSpec((s, LANES), lambda p, i: (0, p))
    tri = pl.BlockSpec((2 * tk, 2 * tk), lambda p, i: (0, 0))
    return _call_with_exchange(
        body, exchange, grid=(npairs, nq),
        in_specs=[blk, full, full, blk, tri, tri],
        out_specs=[blk, full, full],
        out_shape=[jax.ShapeDtypeStruct((s, d), F32)] * 3,
        scratch_shapes=[pltpu.VMEM((s // tk, tq, 2 * tk), F32), pltpu.VMEM((s // tk, tq, 2 * tk), F32),
                        pltpu.VMEM((2, tq, tk), F32), pltpu.VMEM((tq, LANES), F32), pltpu.VMEM((8, LANES), F32)],
        name="sb_attn_bwd", args=(qn, kn, vb, do, tri_ge2, tri_le2))


def _hg_consts(c):
    levels = []
    h = c // 2
    while h >= 1:
        levels.append(h)
        h //= 2
    t = np.arange(c)
    j = t[None, :]
    rows, masks = [], []
    for h in levels:
        blk = t // (2 * h)
        mid = blk * 2 * h + h - 1
        second = (t % (2 * h)) >= h
        rows.append(second[:, None] & (j > mid[:, None]) & (j <= t[:, None]))
        rows.append((~second)[:, None] & (j > t[:, None]) & (j <= mid[:, None]))
        masks.append((blk[:, None] == blk[None, :]) & second[:, None] & (~second)[None, :])
    rows.append(j <= t[:, None])
    rows.append(j > t[:, None])
    masks.append(t[:, None] == t[None, :])
    m_all = np.concatenate(rows, axis=0).astype(np.float32)
    mask_all = np.stack(masks, axis=0).astype(np.float32)
    suffix = (t[None, :] >= t[:, None]).astype(np.float32)
    return len(levels), jnp.asarray(m_all, BF16), jnp.asarray(mask_all, F32), jnp.asarray(suffix, BF16)


def _split3(x):
    hi = x.astype(BF16)
    r1 = x - hi.astype(F32)
    mid = r1.astype(BF16)
    lo = (r1 - mid.astype(F32)).astype(BF16)
    return jnp.concatenate([hi, mid, lo], axis=1)


def _join3(e):
    n = e.shape[1] // 3
    return e[:, :n] + e[:, n:2 * n] + e[:, 2 * n:]


def _hg_gates(qr, fr, lb):
    sq = _sigmoid(qr)
    sf = _sigmoid(fr)
    forget = lb + (1.0 - lb) * sf
    return qr * sq, sq, sf, forget, jnp.log(forget), 1.0 - forget


def _hg_scores(q, k, expo, masks, nlev, c):
    qb, kb = q.astype(BF16), k.astype(BF16)
    a = masks[nlev] * _dot(qb, kb, NT)
    scaled = []
    for li in range(nlev):
        fq = jnp.exp(expo[(2 * li) * c:(2 * li + 1) * c])
        fk = jnp.exp(expo[(2 * li + 1) * c:(2 * li + 2) * c])
        qs, ks = (q * fq).astype(BF16), (k * fk).astype(BF16)
        a = a + masks[li] * _dot(qs, ks, NT)
        scaled.append((qs, ks, fq, fk))
    return a, scaled, qb, kb


def _hg_fwd(proj, lb_row, gain_row, exchange=None):
    s, d4 = proj.shape
    d = d4 // 4
    nh = d // HG_HEAD_DIM
    c = min(HG_CHUNK, s)
    tb = _pick(s, 512)
    ncb = tb // c
    nlev, m_all, mask_all, _ = _hg_consts(c)
    nrow = m_all.shape[0]

    def body(q_ref, f_ref, i_ref, g_ref, lb_ref, gain_ref, mall_ref, mask_ref, y_ref, o_ref, st_out_ref, st_ref):
        b = pl.program_id(1)

        @pl.when(b == 0)
        def _():
            st_ref[...] = jnp.zeros_like(st_ref)

        lb = lb_ref[...]
        gain = gain_ref[...]

        def chunk(ci, carry):
            off = pl.multiple_of(ci * c, c)
            rows = pl.ds(off, c)
            q, _, _, _, lf, k = _hg_gates(q_ref[rows, :], f_ref[rows, :], lb)
            v = i_ref[rows, :].astype(BF16)
            expo = _join3(_dot(mall_ref[...], _split3(lf)))
            masks = mask_ref[...]
            st = st_ref[...]
            st_out_ref[ci] = st
            a, _, _, _ = _hg_scores(q, k, expo, masks, nlev, c)
            b_cum = expo[2 * nlev * c:(2 * nlev + 1) * c]
            e_tail = expo[(2 * nlev + 1) * c:(2 * nlev + 2) * c]
            q_in = (q * jnp.exp(b_cum)).astype(BF16)
            o = _dot(q_in, st.astype(BF16), NT) + _dot(a.astype(BF16), v)
            k_dec = (k * jnp.exp(e_tail)).astype(BF16)
            st_ref[...] = st * jnp.exp(b_cum[c - 1:c, :]) + _dot(v, k_dec, TN)
            o_ref[rows, :] = o
            r = lax.rsqrt(jnp.mean(o * o, axis=-1, keepdims=True) + NORM_EPS)
            y_ref[rows, :] = (o * r * gain * _sigmoid(g_ref[rows, :])).astype(y_ref.dtype)
            return carry

        lax.fori_loop(0, ncb, chunk, 0)

    part = lambda k: pl.BlockSpec((tb, HG_HEAD_DIM), lambda h, b: (b, k * nh + h))
    head_row = pl.BlockSpec((1, HG_HEAD_DIM), lambda h, b: (0, h))
    tok = pl.BlockSpec((tb, HG_HEAD_DIM), lambda h, b: (b, h))
    return _call_with_exchange(
        body, exchange, grid=(nh, s // tb),
        in_specs=[part(0), part(1), part(2), part(3), head_row,
                  pl.BlockSpec((1, HG_HEAD_DIM), lambda h, b: (0, 0)),
                  pl.BlockSpec((nrow, c), lambda h, b: (0, 0)),
                  pl.BlockSpec((nlev + 1, c, c), lambda h, b: (0, 0, 0))],
        out_specs=[tok, tok, pl.BlockSpec((ncb, None, HG_HEAD_DIM, HG_HEAD_DIM), lambda h, b: (b, h, 0, 0))],
        out_shape=[jax.ShapeDtypeStruct((s, d), BF16), jax.ShapeDtypeStruct((s, d), F32),
                   jax.ShapeDtypeStruct((s // c, nh, HG_HEAD_DIM, HG_HEAD_DIM), F32)],
        scratch_shapes=[pltpu.VMEM((HG_HEAD_DIM, HG_HEAD_DIM), F32)],
        name="hg_fwd", args=(proj, proj, proj, proj, lb_row, gain_row, m_all, mask_all))


def _hg_bwd(proj, lb_row, gain_row, o_saved, states, dy, exchange=None):
    s, d4 = proj.shape
    d = d4 // 4
    nh = d // HG_HEAD_DIM
    c = min(HG_CHUNK, s)
    tb = _pick(s, 512)
    ncb = tb // c
    nb = s // tb
    nlev, m_all, mask_all, suffix = _hg_consts(c)
    nrow = m_all.shape[0]

    def body(q_ref, f_ref, i_ref, g_ref, lb_ref, gain_ref, o_ref, st_in_ref, dy_ref, mall_ref, mask_ref, suf_ref,
             dq_ref, df_ref, di_ref, dg_ref, dlb_ref, dgain_ref, dst_ref, run_ref):
        b = pl.program_id(1)

        @pl.when(b == 0)
        def _():
            dst_ref[...] = jnp.zeros_like(dst_ref)
            run_ref[...] = jnp.zeros_like(run_ref)
            dlb_ref[...] = jnp.zeros_like(dlb_ref)
            dgain_ref[...] = jnp.zeros_like(dgain_ref)

        lb = lb_ref[...]
        gain = gain_ref[...]

        def chunk(it, carry):
            ci = ncb - 1 - it
            off = pl.multiple_of(ci * c, c)
            rows = pl.ds(off, c)
            qr, fr = q_ref[rows, :], f_ref[rows, :]
            q, sq, sf, forget, lf, k = _hg_gates(qr, fr, lb)
            v = i_ref[rows, :].astype(BF16)
            expo = _join3(_dot(mall_ref[...], _split3(lf)))
            masks = mask_ref[...]
            o = o_ref[rows, :]
            dyv = dy_ref[rows, :]
            sg = _sigmoid(g_ref[rows, :])
            r = lax.rsqrt(jnp.mean(o * o, axis=-1, keepdims=True) + NORM_EPS)
            ohat = o * r
            dyn = dyv * sg
            dg_ref[rows, :] = (dyv * ohat * gain * sg * (1.0 - sg)).astype(dg_ref.dtype)
            dgain_ref[...] += jnp.sum(dyn * ohat, axis=0, keepdims=True)
            dohat = dyn * gain
            do = (r * (dohat - ohat * jnp.mean(dohat * ohat, axis=-1, keepdims=True))).astype(BF16)
            dst = dst_ref[...]
            dstb = dst.astype(BF16)
            a, scaled, qb, kb = _hg_scores(q, k, expo, masks, nlev, c)
            f_cum = jnp.exp(expo[2 * nlev * c:(2 * nlev + 1) * c])
            f_tail = jnp.exp(expo[(2 * nlev + 1) * c:(2 * nlev + 2) * c])
            q_in = (q * f_cum).astype(BF16)
            k_dec = (k * f_tail).astype(BF16)
            t_in = _join3(_dot(do, _split3(st_in_ref[ci])))
            t_st = _join3(_dot(v, _split3(dst)))
            da = _dot(do, v, NT)
            dam = (masks[nlev] * da).astype(BF16)
            dq = t_in * f_cum + _dot(dam, kb)
            dk = t_st * f_tail + _dot(dam, qb, TN)
            db = q_in.astype(F32) * t_in - k_dec.astype(F32) * t_st
            for li in range(nlev):
                qs, ks, fq, fk = scaled[li]
                dam = (masks[li] * da).astype(BF16)
                t_q = _dot(dam, ks)
                t_k = _dot(dam, qs, TN)
                dq = dq + t_q * fq
                dk = dk + t_k * fk
                db = db + (qs.astype(F32) * t_q - ks.astype(F32) * t_k)
            dv = _dot(a.astype(BF16), do, TN) + _dot(k_dec, dstb, NT)
            dst_ref[...] = dst * f_cum[c - 1:c, :] + _dot(do, q_in, TN)
            dlf = _join3(_dot(suf_ref[...], _split3(db))) + run_ref[...]
            run_ref[...] = dlf[0:1, :]
            dforget = dlf / forget - dk
            dlb_ref[...] += jnp.sum(dforget * (1.0 - sf), axis=0, keepdims=True)
            df_ref[rows, :] = (dforget * (1.0 - lb) * sf * (1.0 - sf)).astype(df_ref.dtype)
            dq_ref[rows, :] = (dq * sq * (1.0 + qr * (1.0 - sq))).astype(dq_ref.dtype)
            di_ref[rows, :] = dv.astype(di_ref.dtype)
            return carry

        lax.fori_loop(0, ncb, chunk, 0)

    part = lambda k: pl.BlockSpec((tb, HG_HEAD_DIM), lambda h, b: (nb - 1 - b, k * nh + h))
    head_row = pl.BlockSpec((1, HG_HEAD_DIM), lambda h, b: (0, h))
    tok = pl.BlockSpec((tb, HG_HEAD_DIM), lambda h, b: (nb - 1 - b, h))
    const2 = lambda shape: pl.BlockSpec(shape, lambda h, b: (0, 0))
    return _call_with_exchange(
        body, exchange, grid=(nh, nb),
        in_specs=[part(0), part(1), part(2), part(3), head_row, const2((1, HG_HEAD_DIM)), tok,
                  pl.BlockSpec((ncb, None, HG_HEAD_DIM, HG_HEAD_DIM), lambda h, b: (nb - 1 - b, h, 0, 0)),
                  tok, const2((nrow, c)), pl.BlockSpec((nlev + 1, c, c), lambda h, b: (0, 0, 0)), const2((c, c))],
        out_specs=[tok, tok, tok, tok, head_row, head_row],
        out_shape=[jax.ShapeDtypeStruct((s, d), BF16)] * 4 + [jax.ShapeDtypeStruct((1, d), F32)] * 2,
        scratch_shapes=[pltpu.VMEM((HG_HEAD_DIM, HG_HEAD_DIM), F32), pltpu.VMEM((1, HG_HEAD_DIM), F32)],
        name="hg_bwd", args=(proj, proj, proj, proj, lb_row, gain_row, o_saved, states, dy, m_all, mask_all, suffix))


def _lb_fwd(logits):
    n, d = logits.shape

    def body(l_ref, lb_ref, p_ref):
        rows = [l_ref[i:i + 1, :] for i in range(n)]
        m = functools.reduce(jnp.maximum, rows)
        es = [jnp.exp(r - m) for r in rows]
        tot = functools.reduce(lambda a, b: a + b, es)
        ps = [e / tot for e in es]
        run = jnp.zeros_like(ps[0])
        for i in range(n):
            run = run + ps[i]
            lb_ref[i:i + 1, :] = run - ps[0]
            p_ref[i:i + 1, :] = ps[i]

    return pl.pallas_call(
        body, out_shape=[jax.ShapeDtypeStruct((n, d), F32)] * 2, name="lb_fwd",
    )(logits)


def _lb_bwd(p, dlb):
    n, d = p.shape

    def body(p_ref, dlb_ref, dl_ref):
        ps = [p_ref[i:i + 1, :] for i in range(n)]
        ds = [dlb_ref[i:i + 1, :] for i in range(n)]
        total = functools.reduce(lambda a, b: a + b, ds)
        dps = []
        for i in range(n):
            dp = functools.reduce(lambda a, b: a + b, ds[i:])
            dps.append(dp - total if i == 0 else dp)
        inner = functools.reduce(lambda a, b: a + b, [pi * di for pi, di in zip(ps, dps)])
        for i in range(n):
            dl_ref[i:i + 1, :] = ps[i] * (dps[i] - inner)

    return pl.pallas_call(body, out_shape=jax.ShapeDtypeStruct((n, d), F32), name="lb_bwd")(p, dlb)


def _as2d(a):
    return a.reshape(-1, a.shape[-1])


def _adamw(w, m, v, grads):
    shape = w.shape
    w2, m2, v2 = _as2d(w), _as2d(m), _as2d(v)
    g2 = [_as2d(g) for g in grads]
    rows, cols = w2.shape
    tr = _pick(rows, 512)
    ng = len(g2)
    bc1 = 1.0 - ADAM_B1 ** ADAM_STEP
    bc2 = 1.0 - ADAM_B2 ** ADAM_STEP

    def body(w_ref, m_ref, v_ref, *rest):
        g = rest[0][...]
        for extra in rest[1:ng]:
            g = g + extra[...]
        g_out, d_out, m_out, v_out = rest[ng:]
        mn = ADAM_B1 * m_ref[...] + (1.0 - ADAM_B1) * g
        vn = ADAM_B2 * v_ref[...] + (1.0 - ADAM_B2) * (g * g)
        m_hat = mn / bc1
        v_hat = vn / bc2
        g_out[...] = g
        d_out[...] = -ADAM_LR * (m_hat / (jnp.sqrt(v_hat) + ADAM_EPS) + ADAM_WD * w_ref[...])
        m_out[...] = mn
        v_out[...] = vn

    spec = pl.BlockSpec((tr, cols), lambda i: (i, 0))
    outs = pl.pallas_call(
        body, grid=(rows // tr,), in_specs=[spec] * (3 + ng), out_specs=[spec] * 4,
        out_shape=[jax.ShapeDtypeStruct((rows, cols), F32)] * 4,
        compiler_params=_params(("parallel",)), name="adamw",
    )(w2, m2, v2, *g2)
    return tuple(o.reshape(shape) for o in outs)


def _sum_slots(parts, recv, chip):
    _, rows, cols = parts.shape
    tr = _pick(rows, 512)

    def body(chip_ref, own_ref, r0_ref, r1_ref, r2_ref, o_ref):
        o_ref[...] = ((own_ref[...] + r0_ref[...]) + r1_ref[...]) + r2_ref[...]

    grid_spec = pltpu.PrefetchScalarGridSpec(
        num_scalar_prefetch=1, grid=(rows // tr,),
        in_specs=[pl.BlockSpec((None, tr, cols), lambda i, chip_ref: (chip_ref[0], i, 0))]
        + [pl.BlockSpec((None, tr, cols), functools.partial(lambda i, chip_ref, k: (k, i, 0), k=k)) for k in range(3)],
        out_specs=pl.BlockSpec((tr, cols), lambda i, chip_ref: (i, 0)))
    return pl.pallas_call(
        body, grid_spec=grid_spec, out_shape=jax.ShapeDtypeStruct((rows, cols), F32),
        compiler_params=_params(("parallel",)), name="sum_slots",
    )(chip, parts, recv, recv, recv)


def _sum_devices(gathered):
    n, rows, cols = gathered.shape

    def body(g_ref, o_ref):
        acc = g_ref[0]
        for i in range(1, n):
            acc = acc + g_ref[i]
        o_ref[...] = acc

    return pl.pallas_call(body, out_shape=jax.ShapeDtypeStruct((rows, cols), F32), name="sum_devices")(gathered)


def _coords():
    return lax.axis_index("x"), lax.axis_index("y"), lax.axis_index("c")


def _chip_peers(x, y, c):
    out = []
    for fx, fy in ((0, 1), (1, 0), (1, 1)):
        px = 1 - x if fx else x
        py = 1 - y if fy else y
        out.append(((px, py, c), 2 * px + py))
    return out


class _ChipExchange:
    def __init__(self, kind, arrays):
        self.kind, self.arrays, self.n = kind, list(arrays), len(arrays)
        lead = lambda a: (N_CHIPS,) + a.shape if kind == "gather" else (3,) + a.shape[1:]
        self.out_shape = [jax.ShapeDtypeStruct(lead(a), a.dtype) for a in self.arrays]
        self.scratch = [pltpu.SemaphoreType.DMA((3 * self.n,)), pltpu.SemaphoreType.DMA((3 * self.n,)),
                        pltpu.SemaphoreType.DMA((self.n,))]

    def copies(self, ins, outs, send_sems, recv_sems, local_sems):
        x, y, c = _coords()
        me = 2 * x + y
        starts, waits = [], []
        for t in range(self.n):
            if self.kind == "gather":
                own = pltpu.make_async_copy(ins[t], outs[t].at[me], local_sems.at[t])
                starts.append(own.start)
                waits.append(own.wait)
            for k, (peer, peer_chip) in enumerate(_chip_peers(x, y, c)):
                sems = dict(send_sem=send_sems.at[3 * t + k], recv_sem=recv_sems.at[3 * t + k],
                            device_id=peer, device_id_type=MESH)
                if self.kind == "gather":
                    send = pltpu.make_async_remote_copy(src_ref=ins[t], dst_ref=outs[t].at[me], **sems)
                    recv = pltpu.make_async_remote_copy(src_ref=ins[t], dst_ref=outs[t].at[peer_chip], **sems)
                else:
                    send = pltpu.make_async_remote_copy(src_ref=ins[t].at[peer_chip], dst_ref=outs[t].at[k], **sems)
                    recv = send
                starts.append(send.start)
                waits += [send.wait_send, recv.wait_recv]
        return starts, waits

    def run(self, name):
        n = self.n

        def body(*refs):
            starts, waits = self.copies(refs[:n], refs[n:2 * n], *refs[2 * n:])
            for f in starts + waits:
                f()

        return pl.pallas_call(body, in_specs=[HBM_SPEC] * n, out_specs=[HBM_SPEC] * n, out_shape=self.out_shape,
                              scratch_shapes=self.scratch, name=name)(*self.arrays)


def _call_with_exchange(body, exchange, *, grid, in_specs, out_specs, out_shape, scratch_shapes, name, args):
    if exchange is None:
        outs = pl.pallas_call(body, grid=grid, in_specs=in_specs, out_specs=out_specs, out_shape=out_shape,
                              scratch_shapes=scratch_shapes,
                              compiler_params=_params(("parallel",) + ("arbitrary",) * (len(grid) - 1)),
                              name=name)(*args)
        return outs, []
    n_in, n_out, n_scr, n = len(in_specs), len(out_specs), len(scratch_shapes), exchange.n

    def wrapped(*refs):
        ins, ex_in = refs[:n_in], refs[n_in:n_in + n]
        outs = refs[n_in + n:n_in + n + n_out]
        ex_out = refs[n_in + n + n_out:n_in + 2 * n + n_out]
        scr = refs[n_in + 2 * n + n_out:n_in + 2 * n + n_out + n_scr]
        sems = refs[n_in + 2 * n + n_out + n_scr:]
        ids = [pl.program_id(a) for a in range(len(grid))]
        first = functools.reduce(jnp.logical_and, [i == 0 for i in ids])
        last = functools.reduce(jnp.logical_and, [i == g - 1 for i, g in zip(ids, grid)])

        @pl.when(first)
        def _():
            for f in exchange.copies(ex_in, ex_out, *sems)[0]:
                f()

        body(*ins, *outs, *scr)

        @pl.when(last)
        def _():
            for f in exchange.copies(ex_in, ex_out, *sems)[1]:
                f()

    res = pl.pallas_call(
        wrapped, grid=grid, in_specs=list(in_specs) + [HBM_SPEC] * n, out_specs=list(out_specs) + [HBM_SPEC] * n,
        out_shape=list(out_shape) + exchange.out_shape, scratch_shapes=list(scratch_shapes) + exchange.scratch,
        compiler_params=_params(("arbitrary",) * len(grid)), name=name + "_" + exchange.kind,
    )(*args, *exchange.arrays)
    return res[:n_out], res[n_out:]


def _swap_cores(arrs):
    n = len(arrs)

    def body(*refs):
        ins, outs = refs[:n], refs[n:2 * n]
        send_sems, recv_sems = refs[2 * n:]
        x, y, c = _coords()
        copies = []
        for t in range(n):
            cp = pltpu.make_async_remote_copy(
                src_ref=ins[t], dst_ref=outs[t], send_sem=send_sems.at[t], recv_sem=recv_sems.at[t],
                device_id=(x, y, 1 - c), device_id_type=MESH)
            cp.start()
            copies.append(cp)
        for cp in copies:
            cp.wait()

    return pl.pallas_call(
        body, in_specs=[HBM_SPEC] * n, out_specs=[HBM_SPEC] * n,
        out_shape=[jax.ShapeDtypeStruct(a.shape, a.dtype) for a in arrs],
        scratch_shapes=[pltpu.SemaphoreType.DMA((n,)), pltpu.SemaphoreType.DMA((n,))],
        name="swap_cores",
    )(*arrs)


def _gather_devices(a):
    def body(in_ref, out_ref, send_sems, recv_sems, local_sem):
        x, y, c = _coords()
        me = 4 * x + 2 * y + c
        own = pltpu.make_async_copy(in_ref, out_ref.at[me], local_sem)
        own.start()
        waits = [own.wait]
        for k in range(1, N_DEVICES):
            px = 1 - x if k & 4 else x
            py = 1 - y if k & 2 else y
            pc = 1 - c if k & 1 else c
            peer = (px, py, pc)
            send = pltpu.make_async_remote_copy(
                src_ref=in_ref, dst_ref=out_ref.at[me], send_sem=send_sems.at[k - 1], recv_sem=recv_sems.at[k - 1],
                device_id=peer, device_id_type=MESH)
            send.start()
            recv = pltpu.make_async_remote_copy(
                src_ref=in_ref, dst_ref=out_ref.at[4 * px + 2 * py + pc], send_sem=send_sems.at[k - 1],
                recv_sem=recv_sems.at[k - 1], device_id=peer, device_id_type=MESH)
            waits += [send.wait_send, recv.wait_recv]
        for w in waits:
            w()

    return pl.pallas_call(
        body, in_specs=[HBM_SPEC], out_specs=HBM_SPEC,
        out_shape=jax.ShapeDtypeStruct((N_DEVICES,) + a.shape, a.dtype),
        scratch_shapes=[pltpu.SemaphoreType.DMA((N_DEVICES - 1,)), pltpu.SemaphoreType.DMA((N_DEVICES - 1,)),
                        pltpu.SemaphoreType.DMA],
        name="gather_devices",
    )(a)


def _mlp_grad_epilogue(r, u):
    return r * (2.0 * jnp.maximum(u, 0.0))


def kernel(x, norm_gains, sb_w_qkv, sb_q_gain, sb_k_gain, sb_w_o, hg_w_in, hg_lb_logits, hg_norm_gain, hg_w_o, mlp_w1, mlp_w2, loss_target, m_norm_gains, m_sb_w_qkv, m_sb_q_gain, m_sb_k_gain, m_sb_w_o, m_hg_w_in, m_hg_lb_logits, m_hg_norm_gain, m_hg_w_o, m_mlp_w1, m_mlp_w2, v_norm_gains, v_sb_w_qkv, v_sb_q_gain, v_sb_k_gain, v_sb_w_o, v_hg_w_in, v_hg_lb_logits, v_hg_norm_gain, v_hg_w_o, v_mlp_w1, v_mlp_w2):
    depth = norm_gains.shape[0]
    n_sb, n_hg = sb_w_qkv.shape[0], hg_w_in.shape[0]
    xs, tgt = x[0], loss_target[0]
    s, d = xs.shape
    dq = d // N_CHIPS
    cx, cy, cc = _coords()
    chip = 2 * cx + cy
    chip_arr = jnp.reshape(chip, (1,)).astype(jnp.int32)

    def mixer_weights(layer):
        j = layer // 2
        return (sb_w_qkv[j], sb_w_o[j]) if layer % 2 == 0 else (hg_w_in[j], hg_w_o[j])

    w_in_g, ng_g, lbl_g = _ChipExchange(
        "gather", [mixer_weights(0)[0].astype(BF16), norm_gains, hg_lb_logits]).run("gather_first")
    gains = jnp.transpose(ng_g, (1, 2, 0, 3)).reshape(depth, 2, d)
    logits = jnp.transpose(lbl_g, (1, 0, 2)).reshape(n_hg, d)
    lbs, lb_p = _lb_fwd(logits)
    qg_rows = [jnp.tile(sb_q_gain[j], d // SB_HEAD_DIM)[None] for j in range(n_sb)]
    kg_rows = [jnp.tile(sb_k_gain[j], d // SB_HEAD_DIM)[None] for j in range(n_sb)]

    saved, wts = [], []
    xc = xs
    for layer in range(depth):
        j = layer // 2
        ahead = [mixer_weights(layer)[1], mlp_w1[layer], mlp_w2[layer]]
        if layer + 1 < depth:
            ahead.append(mixer_weights(layer + 1)[0])
        gather = _ChipExchange("gather", [a.astype(BF16) for a in ahead])
        h1 = _rmsnorm_fwd(xc, gains[layer, 0][None])
        if layer % 2 == 0:
            qkv = _mm_fwd_cols(h1, w_in_g, name="sb_qkv")
            qn, kn, vb = _qk_norm_fwd(qkv, qg_rows[j], kg_rows[j])
            o, moved = _sb_attn_fwd(qn, kn, vb, gather)
            x_mid = _mm_fwd_rows(o, moved[0], residual=xc, name="sb_out")
            mix = (qkv, qn, kn, vb, o)
        else:
            proj = _mm_fwd_cols(h1, w_in_g, name="hg_in")
            (y, o, states), moved = _hg_fwd(proj, lbs[j][None], hg_norm_gain[j][None], gather)
            x_mid = _mm_fwd_rows(y, moved[0], residual=xc, name="hg_out")
            mix = (proj, y, o, states)
        w_out_g, w1_g, w2_g = moved[:3]
        h2 = _rmsnorm_fwd(x_mid, gains[layer, 1][None])
        u = _mm_fwd_cols(h2, w1_g, name="mlp_up")
        x_out = _mm_fwd_rows(u, w2_g, residual=x_mid, a_fn=_relu2, name="mlp_down")
        saved.append((xc, h1, mix, x_mid, h2, u))
        wts.append((w_in_g, w_out_g, w1_g, w2_g))
        w_in_g = moved[3] if layer + 1 < depth else None
        xc = x_out

    sq, dx = _loss_head(xc, tgt)
    loss = lax.psum(jnp.sum(sq) * (0.5 / d), ("x", "y", "c"))

    dgains = [[None, None] for _ in range(depth)]
    dqg, dkg = [None] * n_sb, [None] * n_sb
    dhgain, dlb = [None] * n_hg, [None] * n_hg
    grads, received, pending = {}, {}, []

    def ready(key, parts):
        grads[key] = parts
        pending.append(key)

    def scatter_of(keys):
        return _ChipExchange("scatter", [grads[k] for k in keys]) if keys else None

    def sent(keys, moved):
        for k, r in zip(keys, moved):
            received[k] = r
            pending.remove(k)

    for layer in reversed(range(depth)):
        j = layer // 2
        x_in, h1, mix, x_mid, h2, u = saved[layer]
        w_in_g, w_out_g, w1_g, w2_g = wts[layer]
        du = _mm_bwd_rows(dx, w2_g, name="mlp_down_dx", out_dtype=BF16, epi_fn=_mlp_grad_epilogue, epi_args=(u,))
        ready(("w2", layer), _mm_dw_rows(u, dx, a_fn=_relu2, name="mlp_down_dw"))
        dh2 = _mm_bwd_cols(du, w1_g, name="mlp_up_dx")
        ready(("w1", layer), _mm_dw_cols(h2, du, name="mlp_up_dw"))
        dx, dgains[layer][1] = _rmsnorm_bwd(dh2, x_mid, gains[layer, 1][None], dx)
        if layer % 2 == 0:
            qkv, qn, kn, vb, o = mix
            do = _mm_bwd_rows(dx, w_out_g, name="sb_out_dx")
            ready(("out", layer), _mm_dw_rows(o, dx, name="sb_out_dw"))
            keys = list(pending)
            (dqn, dkn, dv), moved = _sb_attn_bwd(qn, kn, vb, do, scatter_of(keys))
            sent(keys, moved)
            dqkv, dqg_lane, dkg_lane = _qk_norm_bwd(qkv, qg_rows[j], kg_rows[j], dqn, dkn, dv)
            dqg[j] = jnp.sum(dqg_lane.reshape(-1, SB_HEAD_DIM), axis=0)
            dkg[j] = jnp.sum(dkg_lane.reshape(-1, SB_HEAD_DIM), axis=0)
            dh1 = _mm_bwd_cols(dqkv, w_in_g, name="sb_qkv_dx")
            ready(("in", layer), _mm_dw_cols(h1, dqkv, name="sb_qkv_dw"))
        else:
            proj, y, o, states = mix
            dy = _mm_bwd_rows(dx, w_out_g, name="hg_out_dx")
            ready(("out", layer), _mm_dw_rows(y, dx, name="hg_out_dw"))
            keys = list(pending)
            (dq_raw, df_raw, di, dg, dlb_row, dgain_heads), moved = _hg_bwd(
                proj, lbs[j][None], hg_norm_gain[j][None], o, states, dy, scatter_of(keys))
            sent(keys, moved)
            dlb[j] = dlb_row
            dhgain[j] = jnp.sum(dgain_heads.reshape(-1, HG_HEAD_DIM), axis=0)
            dproj = jnp.concatenate([dq_raw, df_raw, di, dg], axis=1)
            dh1 = _mm_bwd_cols(dproj, w_in_g, name="hg_in_dx")
            ready(("in", layer), _mm_dw_cols(h1, dproj, name="hg_in_dw"))
        dx, dgains[layer][0] = _rmsnorm_bwd(dh1, x_in, gains[layer, 0][None], dx)
    keys = list(pending)
    sent(keys, scatter_of(keys).run("scatter_last"))
    grad_x = dx[None]
    dlogits = _lb_bwd(lb_p, jnp.concatenate(dlb, axis=0))

    def chip_sum(kind, layers):
        return jnp.stack([_sum_slots(grads[kind, l], received[kind, l], chip_arr) for l in layers])

    sb_layers, hg_layers = range(0, depth, 2), range(1, depth, 2)
    big_w = [sb_w_qkv, sb_w_o, hg_w_in, hg_w_o, mlp_w1, mlp_w2]
    big_m = [m_sb_w_qkv, m_sb_w_o, m_hg_w_in, m_hg_w_o, m_mlp_w1, m_mlp_w2]
    big_v = [v_sb_w_qkv, v_sb_w_o, v_hg_w_in, v_hg_w_o, v_mlp_w1, v_mlp_w2]
    chip_sums = [chip_sum("in", sb_layers), chip_sum("out", sb_layers), chip_sum("in", hg_layers),
                 chip_sum("out", hg_layers), chip_sum("w1", range(depth)), chip_sum("w2", range(depth))]
    other_core = _swap_cores(chip_sums)
    big = [_adamw(w, m, v, [a, b]) for w, m, v, a, b in zip(big_w, big_m, big_v, chip_sums, other_core)]

    n_small = 2 * depth + n_hg + 3
    small_rows = -(-n_small // 8) * 8
    small = jnp.zeros((small_rows, d), F32)
    small = small.at[0:2 * depth].set(jnp.concatenate([r for pair in dgains for r in pair], axis=0))
    small = small.at[2 * depth:2 * depth + n_hg].set(dlogits)
    base = 2 * depth + n_hg
    small = small.at[base, 0:n_sb * SB_HEAD_DIM].set(jnp.concatenate(dqg))
    small = small.at[base + 1, 0:n_sb * SB_HEAD_DIM].set(jnp.concatenate(dkg))
    small = small.at[base + 2, 0:n_hg * HG_HEAD_DIM].set(jnp.concatenate(dhgain))
    small = _sum_devices(_gather_devices(small))
    my_cols = lambda a: lax.dynamic_slice_in_dim(a, chip * dq, dq, axis=1)
    g_ng = my_cols(small[0:2 * depth]).reshape(norm_gains.shape)
    g_lbl = my_cols(small[2 * depth:base])
    g_qg = small[base, 0:n_sb * SB_HEAD_DIM].reshape(sb_q_gain.shape)
    g_kg = small[base + 1, 0:n_sb * SB_HEAD_DIM].reshape(sb_k_gain.shape)
    g_hgn = small[base + 2, 0:n_hg * HG_HEAD_DIM].reshape(hg_norm_gain.shape)
    r_ng = _adamw(norm_gains, m_norm_gains, v_norm_gains, [g_ng])
    r_qg = _adamw(sb_q_gain, m_sb_q_gain, v_sb_q_gain, [g_qg])
    r_kg = _adamw(sb_k_gain, m_sb_k_gain, v_sb_k_gain, [g_kg])
    r_lbl = _adamw(hg_lb_logits, m_hg_lb_logits, v_hg_lb_logits, [g_lbl])
    r_hgn = _adamw(hg_norm_gain, m_hg_norm_gain, v_hg_norm_gain, [g_hgn])

    per_weight = [r_ng, big[0], r_qg, r_kg, big[1], big[2], r_lbl, r_hgn, big[3], big[4], big[5]]
    outs = [loss, grad_x]
    for field in range(4):
        outs += [r[field] for r in per_weight]
    return tuple(outs)
```

```python
import functools
import math

import numpy as np
import jax
import jax.numpy as jnp
from jax import lax
from jax.experimental import pallas as pl
from jax.experimental.pallas import tpu as pltpu

F32 = jnp.float32
BF16 = jnp.bfloat16
GRAD_SLOT_DTYPE = jnp.bfloat16

NORM_EPS = 1e-6
SB_HEAD_DIM = 64
HG_HEAD_DIM = 128
HG_CHUNK = 64
LANES = 128
VMEM_LIMIT_BYTES = 56 * 2 ** 20
N_CHIPS = 4
N_DEVICES = 8

ADAM_LR = 0.001
ADAM_B1 = 0.9
ADAM_B2 = 0.999
ADAM_EPS = 1e-08
ADAM_WD = 0.01
ADAM_STEP = 10

MESH = pl.DeviceIdType.MESH
HBM_SPEC = pl.BlockSpec(memory_space=pltpu.HBM)

NN = (((1,), (0,)), ((), ()))
NT = (((1,), (1,)), ((), ()))
TN = (((0,), (0,)), ((), ()))


def _params(sem=None):
    return pltpu.CompilerParams(dimension_semantics=sem, vmem_limit_bytes=VMEM_LIMIT_BYTES)


def _pick(dim, pref):
    for t in (1024, 768, 512, 384, 256, 128, 64, 32, 16, 8):
        if t <= pref and dim % t == 0:
            return t
    return dim


def _dot(a, b, dims=NN):
    return lax.dot_general(a, b, dims, preferred_element_type=F32)


def _sigmoid(x):
    e = jnp.exp(-jnp.abs(x))
    return jnp.where(x >= 0, 1.0, e) / (1.0 + e)


def _matmul(a, b, *, mode, grid, a_block, a_map, b_block, b_map, o_block, o_map, out_shape, out_dtype, name,
            a_fn=None, epi_fn=None, epi_args=()):
    nk = grid[2]
    dims = {"nn": NN, "nt": NT, "tn": TN}[mode]
    n_epi = len(epi_args)

    def body(a_ref, b_ref, *rest):
        epi_refs = rest[:n_epi]
        o_ref = rest[n_epi]
        kk = pl.program_id(2)

        def emit(r):
            if epi_fn is not None:
                r = epi_fn(r, *[e[...] for e in epi_refs])
            o_ref[...] = r.astype(o_ref.dtype)

        av = a_ref[...]
        if a_fn is not None:
            av = a_fn(av)
        part = _dot(av.astype(BF16), b_ref[...].astype(BF16), dims)
        if nk == 1:
            emit(part)
            return
        acc_ref = rest[n_epi + 1]

        @pl.when(kk == 0)
        def _():
            acc_ref[...] = part

        @pl.when(kk > 0)
        def _():
            acc_ref[...] += part

        @pl.when(kk == nk - 1)
        def _():
            emit(acc_ref[...])

    acc_shape = tuple(d for d in o_block if d is not None)
    in_specs = [pl.BlockSpec(a_block, a_map), pl.BlockSpec(b_block, b_map)]
    in_specs += [pl.BlockSpec(o_block, o_map) for _ in epi_args]
    return pl.pallas_call(
        body, grid=grid, in_specs=in_specs, out_specs=pl.BlockSpec(o_block, o_map),
        out_shape=jax.ShapeDtypeStruct(out_shape, out_dtype),
        scratch_shapes=[pltpu.VMEM(acc_shape, F32)] if nk > 1 else [],
        compiler_params=_params(("parallel", "parallel", "arbitrary")), name=name,
    )(a, b, *epi_args)


def _relu2(u):
    r = jnp.maximum(u, 0.0)
    return r * r


def _add(r, res):
    return r + res


def _mm_fwd_cols(a, wg, *, name):
    s, k = a.shape
    ncs = wg.shape[2]
    tm, tk, tn = _pick(s, 1024), _pick(k, 1024), _pick(ncs, 1024)
    npb = ncs // tn
    return _matmul(a, wg, mode="nn", grid=(s // tm, N_CHIPS * npb, k // tk),
                   a_block=(tm, tk), a_map=lambda i, j, kk: (i, kk),
                   b_block=(None, tk, tn), b_map=lambda i, j, kk: (j // npb, kk, j % npb),
                   o_block=(tm, tn), o_map=lambda i, j, kk: (i, j),
                   out_shape=(s, N_CHIPS * ncs), out_dtype=F32, name=name)


def _mm_fwd_rows(a, wg, *, residual, name, a_fn=None):
    s = a.shape[0]
    krs, n = wg.shape[1], wg.shape[2]
    tm, tk, tn = _pick(s, 1024), _pick(krs, 1024), _pick(n, 1024)
    kpb = krs // tk
    return _matmul(a, wg, mode="nn", grid=(s // tm, n // tn, N_CHIPS * kpb),
                   a_block=(tm, tk), a_map=lambda i, j, kk: (i, kk),
                   b_block=(None, tk, tn), b_map=lambda i, j, kk: (kk // kpb, kk % kpb, j),
                   o_block=(tm, tn), o_map=lambda i, j, kk: (i, j),
                   out_shape=(s, n), out_dtype=F32, name=name, a_fn=a_fn, epi_fn=_add, epi_args=(residual,))


def _mm_bwd_cols(dy, wg, *, name, out_dtype=F32):
    s = dy.shape[0]
    kw, ncs = wg.shape[1], wg.shape[2]
    tm, tn, tk = _pick(s, 1024), _pick(kw, 1024), _pick(ncs, 1024)
    kpb = ncs // tk
    return _matmul(dy, wg, mode="nt", grid=(s // tm, kw // tn, N_CHIPS * kpb),
                   a_block=(tm, tk), a_map=lambda i, j, kk: (i, kk),
                   b_block=(None, tn, tk), b_map=lambda i, j, kk: (kk // kpb, j, kk % kpb),
                   o_block=(tm, tn), o_map=lambda i, j, kk: (i, j),
                   out_shape=(s, kw), out_dtype=out_dtype, name=name)


def _mm_bwd_rows(dy, wg, *, name, out_dtype=F32, epi_fn=None, epi_args=()):
    s, n = dy.shape
    krs = wg.shape[1]
    tm, tn, tk = _pick(s, 1024), _pick(krs, 1024), _pick(n, 1024)
    npb = krs // tn
    return _matmul(dy, wg, mode="nt", grid=(s // tm, N_CHIPS * npb, n // tk),
                   a_block=(tm, tk), a_map=lambda i, j, kk: (i, kk),
                   b_block=(None, tn, tk), b_map=lambda i, j, kk: (j // npb, j % npb, kk),
                   o_block=(tm, tn), o_map=lambda i, j, kk: (i, j),
                   out_shape=(s, N_CHIPS * krs), out_dtype=out_dtype, name=name, epi_fn=epi_fn, epi_args=epi_args)


def _mm_dw_cols(xa, dy, *, name):
    s, kx = xa.shape
    ncs = dy.shape[1] // N_CHIPS
    tm, tn, tk = _pick(kx, 1024), _pick(ncs, 1024), _pick(s, 1024)
    npb = ncs // tn
    return _matmul(xa, dy, mode="tn", grid=(kx // tm, N_CHIPS * npb, s // tk),
                   a_block=(tk, tm), a_map=lambda i, j, kk: (kk, i),
                   b_block=(tk, tn), b_map=lambda i, j, kk: (kk, j),
                   o_block=(None, tm, tn), o_map=lambda i, j, kk: (j // npb, i, j % npb),
                   out_shape=(N_CHIPS, kx, ncs), out_dtype=GRAD_SLOT_DTYPE, name=name)


def _mm_dw_rows(xa, dy, *, name, a_fn=None):
    s, n = dy.shape
    krs = xa.shape[1] // N_CHIPS
    tm, tn, tk = _pick(krs, 1024), _pick(n, 1024), _pick(s, 1024)
    mpb = krs // tm
    return _matmul(xa, dy, mode="tn", grid=(N_CHIPS * mpb, n // tn, s // tk),
                   a_block=(tk, tm), a_map=lambda i, j, kk: (kk, i),
                   b_block=(tk, tn), b_map=lambda i, j, kk: (kk, j),
                   o_block=(None, tm, tn), o_map=lambda i, j, kk: (i // mpb, i % mpb, j),
                   out_shape=(N_CHIPS, krs, n), out_dtype=GRAD_SLOT_DTYPE, name=name, a_fn=a_fn)


def _rmsnorm_fwd(x, gain_row):
    s, d = x.shape
    ts = _pick(s, 512)

    def body(x_ref, g_ref, h_ref):
        xv = x_ref[...]
        r = lax.rsqrt(jnp.mean(xv * xv, axis=-1, keepdims=True) + NORM_EPS)
        h_ref[...] = (xv * r * g_ref[...]).astype(h_ref.dtype)

    return pl.pallas_call(
        body, grid=(s // ts,),
        in_specs=[pl.BlockSpec((ts, d), lambda i: (i, 0)), pl.BlockSpec((1, d), lambda i: (0, 0))],
        out_specs=pl.BlockSpec((ts, d), lambda i: (i, 0)),
        out_shape=jax.ShapeDtypeStruct((s, d), BF16),
        compiler_params=_params(("parallel",)), name="rmsnorm_fwd",
    )(x, gain_row)


def _rmsnorm_bwd(dh, x, gain_row, dx_res):
    s, d = x.shape
    ts = _pick(s, 512)

    def body(dh_ref, x_ref, g_ref, res_ref, dx_ref, dg_ref):
        i = pl.program_id(0)
        xv = x_ref[...]
        r = lax.rsqrt(jnp.mean(xv * xv, axis=-1, keepdims=True) + NORM_EPS)
        xhat = xv * r
        dh_v = dh_ref[...]
        dxhat = dh_v * g_ref[...]
        dx = r * (dxhat - xhat * jnp.mean(dxhat * xhat, axis=-1, keepdims=True))
        dx_ref[...] = res_ref[...] + dx
        part = jnp.sum(dh_v * xhat, axis=0, keepdims=True)

        @pl.when(i == 0)
        def _():
            dg_ref[...] = part

        @pl.when(i > 0)
        def _():
            dg_ref[...] += part

    return pl.pallas_call(
        body, grid=(s // ts,),
        in_specs=[pl.BlockSpec((ts, d), lambda i: (i, 0)), pl.BlockSpec((ts, d), lambda i: (i, 0)),
                  pl.BlockSpec((1, d), lambda i: (0, 0)), pl.BlockSpec((ts, d), lambda i: (i, 0))],
        out_specs=[pl.BlockSpec((ts, d), lambda i: (i, 0)), pl.BlockSpec((1, d), lambda i: (0, 0))],
        out_shape=[jax.ShapeDtypeStruct((s, d), F32), jax.ShapeDtypeStruct((1, d), F32)],
        compiler_params=_params(("arbitrary",)), name="rmsnorm_bwd",
    )(dh, x, gain_row, dx_res)


def _loss_head(y, target):
    s, d = y.shape
    ts = _pick(s, 512)

    def body(y_ref, t_ref, sq_ref, dy_ref):
        i = pl.program_id(0)
        err = y_ref[...] - t_ref[...]
        dy_ref[...] = err / d
        part = jnp.sum(err * err, axis=0, keepdims=True)

        @pl.when(i == 0)
        def _():
            sq_ref[...] = part

        @pl.when(i > 0)
        def _():
            sq_ref[...] += part

    return pl.pallas_call(
        body, grid=(s // ts,),
        in_specs=[pl.BlockSpec((ts, d), lambda i: (i, 0)), pl.BlockSpec((ts, d), lambda i: (i, 0))],
        out_specs=[pl.BlockSpec((1, d), lambda i: (0, 0)), pl.BlockSpec((ts, d), lambda i: (i, 0))],
        out_shape=[jax.ShapeDtypeStruct((1, d), F32), jax.ShapeDtypeStruct((s, d), F32)],
        compiler_params=_params(("arbitrary",)), name="loss_head",
    )(y, target)


def _pair_mean(val, low_half):
    s0 = jnp.sum(jnp.where(low_half, val, 0.0), axis=-1, keepdims=True)
    s1 = jnp.sum(jnp.where(low_half, 0.0, val), axis=-1, keepdims=True)
    return jnp.where(low_half, s0, s1) * (1.0 / SB_HEAD_DIM)


def _qk_norm_fwd(qkv, qgain_row, kgain_row):
    s, d3 = qkv.shape
    d = d3 // 3
    ts = _pick(s, 512)
    groups = d // LANES

    def body(q_ref, k_ref, v_ref, qg_ref, kg_ref, qn_ref, kn_ref, vb_ref):
        low_half = lax.broadcasted_iota(jnp.int32, (ts, LANES), 1) < SB_HEAD_DIM
        for src, gain, dst in ((q_ref, qg_ref, qn_ref), (k_ref, kg_ref, kn_ref)):
            for p in range(groups):
                cols = slice(p * LANES, (p + 1) * LANES)
                xp = src[:, cols]
                r = lax.rsqrt(_pair_mean(xp * xp, low_half) + NORM_EPS)
                dst[:, cols] = (xp * r * gain[:, cols]).astype(dst.dtype)
        vb_ref[...] = v_ref[...].astype(vb_ref.dtype)

    tok = lambda c: pl.BlockSpec((ts, d), lambda i: (i, c))
    row = pl.BlockSpec((1, d), lambda i: (0, 0))
    return pl.pallas_call(
        body, grid=(s // ts,),
        in_specs=[tok(0), tok(1), tok(2), row, row],
        out_specs=[tok(0), tok(0), tok(0)],
        out_shape=[jax.ShapeDtypeStruct((s, d), BF16)] * 3,
        compiler_params=_params(("parallel",)), name="qk_norm_fwd",
    )(qkv, qkv, qkv, qgain_row, kgain_row)


def _qk_norm_bwd(qkv, qgain_row, kgain_row, dqn, dkn, dv):
    s, d3 = qkv.shape
    d = d3 // 3
    ts = _pick(s, 512)
    groups = d // LANES

    def body(q_ref, k_ref, qg_ref, kg_ref, dqn_ref, dkn_ref, dv_ref, dqkv_ref, dqg_ref, dkg_ref):
        i = pl.program_id(0)
        low_half = lax.broadcasted_iota(jnp.int32, (ts, LANES), 1) < SB_HEAD_DIM
        for which, (src, gain, dsrc, dgain) in enumerate(((q_ref, qg_ref, dqn_ref, dqg_ref),
                                                          (k_ref, kg_ref, dkn_ref, dkg_ref))):
            for p in range(groups):
                cols = slice(p * LANES, (p + 1) * LANES)
                xp = src[:, cols]
                r = lax.rsqrt(_pair_mean(xp * xp, low_half) + NORM_EPS)
                xhat = xp * r
                dy = dsrc[:, cols]
                dxhat = dy * gain[:, cols]
                dx = r * (dxhat - xhat * _pair_mean(dxhat * xhat, low_half))
                dqkv_ref[:, which * d + p * LANES: which * d + (p + 1) * LANES] = dx.astype(dqkv_ref.dtype)
                part = jnp.sum(dy * xhat, axis=0, keepdims=True)

                @pl.when(i == 0)
                def _():
                    dgain[:, cols] = part

                @pl.when(i > 0)
                def _():
                    dgain[:, cols] += part
        dqkv_ref[:, 2 * d:] = dv_ref[...].astype(dqkv_ref.dtype)

    tok = lambda c: pl.BlockSpec((ts, d), lambda i: (i, c))
    row = pl.BlockSpec((1, d), lambda i: (0, 0))
    return pl.pallas_call(
        body, grid=(s // ts,),
        in_specs=[tok(0), tok(1), row, row, tok(0), tok(0), tok(0)],
        out_specs=[pl.BlockSpec((ts, d3), lambda i: (i, 0)), row, row],
        out_shape=[jax.ShapeDtypeStruct((s, d3), BF16), jax.ShapeDtypeStruct((1, d), F32),
                   jax.ShapeDtypeStruct((1, d), F32)],
        compiler_params=_params(("arbitrary",)), name="qk_norm_bwd",
    )(qkv, qkv, qgain_row, kgain_row, dqn, dkn, dv)


def _split2(x):
    hi = x.astype(BF16)
    lo = (x - hi.astype(F32)).astype(BF16)
    return hi, lo


SB_TK = 128


def _sb_consts(tk):
    j = np.arange(tk)
    ones = np.ones((tk, tk), np.float32)
    out = []
    for tri in ((j[:, None] >= j[None, :]), (j[:, None] <= j[None, :])):
        half = np.concatenate([tri.astype(np.float32), ones], axis=1)
        out.append(jnp.asarray(np.concatenate([half, half], axis=0), BF16))
    return out


def _head_stack(blk, low_half):
    f = blk.astype(F32)
    return jnp.concatenate([jnp.where(low_half, f, 0.0), jnp.where(low_half, 0.0, f)], axis=0).astype(BF16)


def _sb_tile_sums(z, valid, tri2):
    e = jnp.exp(-jnp.abs(z))
    lstay = jnp.minimum(-z, 0.0) - jnp.log(1.0 + e)
    if valid is not None:
        lstay = jnp.where(valid, lstay, 0.0)
    hi, lo = _split2(lstay)
    return e, _dot(jnp.concatenate([hi, lo], axis=1), tri2)


def _sb_weights(z, c2, valid, run):
    w = jnp.exp(z + c2[:, :SB_TK] + run)
    return w if valid is None else jnp.where(valid, w, 0.0)


EXP_IS_ZERO_BELOW = -110.0


def _max_row_norm(x):
    f = x.astype(F32)
    return jnp.sqrt(jnp.max(jnp.sum(f * f, axis=-1, keepdims=True)))


def _sb_score_bound(qs, kmax_ref):
    return _max_row_norm(qs) * jnp.max(kmax_ref[...]) * 1.01 + 1.0


def _sb_rest_is_zero(run_ref, bound):
    return jnp.max(jnp.maximum(run_ref[0], run_ref[1])) + bound < EXP_IS_ZERO_BELOW


def _sb_attn_fwd(qn, kn, vb, exchange=None):
    s, d = qn.shape
    tk = SB_TK
    tq = _pick(s, 256)
    nq, ndiag = s // tq, tq // tk
    assert tq % (2 * tk) == 0, "tiles below the diagonal are taken two at a time"
    npairs = d // LANES
    scale = 1.0 / math.sqrt(SB_HEAD_DIM)
    tri_ge2, _ = _sb_consts(tk)

    def body(q_ref, k_ref, v_ref, tri_ref, o_ref, acc_ref, run_ref, kmax_ref):
        qi = pl.program_id(1)

        @pl.when(qi == 0)
        def _():
            kmax_ref[...] = jnp.full(kmax_ref.shape, _max_row_norm(k_ref[...]), F32)

        low_half = lax.broadcasted_iota(jnp.int32, (tk, LANES), 1) < SB_HEAD_DIM
        row = lax.broadcasted_iota(jnp.int32, (tq, tk), 0)
        col = lax.broadcasted_iota(jnp.int32, (tq, tk), 1)
        qs = (q_ref[...].astype(F32) * scale).astype(BF16)
        bound = _sb_score_bound(qs, kmax_ref)
        acc_ref[...] = jnp.zeros_like(acc_ref)
        run_ref[...] = jnp.zeros_like(run_ref)
        n_full = qi * ndiag

        def sums(kb, dd):
            koff = pl.multiple_of(kb * tk, tk)
            kcat = _head_stack(k_ref[pl.ds(koff, tk), :], low_half)
            vcat = _head_stack(v_ref[pl.ds(koff, tk), :], low_half)
            z2 = _dot(qs, kcat, NT)
            valid = None if dd is None else row > col + dd * tk
            zs = [z2[:, h * tk:(h + 1) * tk] for h in range(2)]
            return zs, [_sb_tile_sums(z, valid, tri_ref[...])[1] for z in zs], valid, vcat

        def finish(zs, c2s, valid, vcat):
            ws = []
            for h in range(2):
                ws.append(_sb_weights(zs[h], c2s[h], valid, run_ref[h]).astype(BF16))
                run_ref[h] += c2s[h][:, tk:]
            acc_ref[...] += _dot(jnp.concatenate(ws, axis=1), vcat)

        for pre in [sums(n_full + dd, dd) for dd in reversed(range(ndiag))]:
            finish(*pre)

        def two_tiles(carry):
            it, _ = carry
            kb = n_full - 1 - 2 * it
            first, second = sums(kb, None), sums(kb - 1, None)
            finish(*first)
            finish(*second)
            return it + 1, _sb_rest_is_zero(run_ref, bound)

        lax.while_loop(lambda c: jnp.logical_and(c[0] < n_full // 2, jnp.logical_not(c[1])), two_tiles,
                       (jnp.int32(0), _sb_rest_is_zero(run_ref, bound)))
        o_ref[...] = acc_ref[...]

    blk = pl.BlockSpec((tq, LANES), lambda p, i: (i, p))
    full = pl.BlockSpec((s, LANES), lambda p, i: (0, p))
    (o,), moved = _call_with_exchange(
        body, exchange, grid=(npairs, nq),
        in_specs=[blk, full, full, pl.BlockSpec((2 * tk, 2 * tk), lambda p, i: (0, 0))],
        out_specs=[blk], out_shape=[jax.ShapeDtypeStruct((s, d), F32)],
        scratch_shapes=[pltpu.VMEM((tq, LANES), F32), pltpu.VMEM((2, tq, tk), F32), pltpu.VMEM((8, LANES), F32)],
        name="sb_attn_fwd", args=(qn, kn, vb, tri_ge2))
    return o, moved


def _sb_attn_bwd(qn, kn, vb, do, exchange=None):
    s, d = qn.shape
    tk = SB_TK
    tq = _pick(s, 256)
    nq, ndiag = s // tq, tq // tk
    assert tq % (2 * tk) == 0, "tiles below the diagonal are taken two at a time"
    npairs = d // LANES
    scale = 1.0 / math.sqrt(SB_HEAD_DIM)
    tri_ge2, tri_le2 = _sb_consts(tk)

    def body(q_ref, k_ref, v_ref, do_ref, tge_ref, tle_ref, dq_ref, dk_ref, dv_ref,
             g_cache, s_cache, run_ref, dq_acc, kmax_ref):
        qi = pl.program_id(1)

        @pl.when(qi == 0)
        def _():
            dk_ref[...] = jnp.zeros_like(dk_ref)
            dv_ref[...] = jnp.zeros_like(dv_ref)
            kmax_ref[...] = jnp.full(kmax_ref.shape, _max_row_norm(k_ref[...]), F32)

        low_half = lax.broadcasted_iota(jnp.int32, (tk, LANES), 1) < SB_HEAD_DIM
        row = lax.broadcasted_iota(jnp.int32, (tq, tk), 0)
        col = lax.broadcasted_iota(jnp.int32, (tq, tk), 1)
        qs = (q_ref[...].astype(F32) * scale).astype(BF16)
        bound = _sb_score_bound(qs, kmax_ref)
        dob = do_ref[...].astype(BF16)
        n_full = qi * ndiag

        def a_sums(kb, dd):
            koff = pl.multiple_of(kb * tk, tk)
            kcat = _head_stack(k_ref[pl.ds(koff, tk), :], low_half)
            vcat = _head_stack(v_ref[pl.ds(koff, tk), :], low_half)
            z2 = _dot(qs, kcat, NT)
            dw2 = _dot(dob, vcat, NT)
            valid = None if dd is None else row > col + dd * tk
            c2s = []
            for h in range(2):
                cols = slice(h * tk, (h + 1) * tk)
                z = z2[:, cols]
                e, c2 = _sb_tile_sums(z, valid, tge_ref[...])
                s_cache[kb, :, cols] = jnp.where(z >= 0, 1.0, e) / (1.0 + e)
                c2s.append(c2)
            return kb, koff, z2, dw2, c2s, valid

        def a_finish(kb, koff, z2, dw2, c2s, valid):
            ws = []
            for h in range(2):
                cols = slice(h * tk, (h + 1) * tk)
                w = _sb_weights(z2[:, cols], c2s[h], valid, run_ref[h])
                run_ref[h] += c2s[h][:, tk:]
                g_cache[kb, :, cols] = w * dw2[:, cols]
                ws.append(w.astype(BF16))
            dv2 = _dot(jnp.concatenate(ws, axis=1), dob, TN)
            dv_ref[pl.ds(koff, tk), :] += jnp.where(low_half, dv2[:tk], dv2[tk:])

        def b_sums(kb, dd):
            gs = [g_cache[kb, :, h * tk:(h + 1) * tk] for h in range(2)]
            p2s = [_dot(jnp.concatenate(_split2(g), axis=1), tle_ref[...]) for g in gs]
            return kb, gs, p2s, (None if dd is None else row > col + dd * tk)

        def b_finish(kb, gs, p2s, valid):
            koff = pl.multiple_of(kb * tk, tk)
            dzs = []
            for h in range(2):
                dz = gs[h] - s_cache[kb, :, h * tk:(h + 1) * tk] * (p2s[h][:, :tk] + run_ref[h])
                if valid is not None:
                    dz = jnp.where(valid, dz, 0.0)
                run_ref[h] += p2s[h][:, tk:]
                dzs.append(dz.astype(BF16))
            dzcat = jnp.concatenate(dzs, axis=1)
            dq_acc[...] += _dot(dzcat, _head_stack(k_ref[pl.ds(koff, tk), :], low_half))
            dk2 = _dot(dzcat, qs, TN)
            dk_ref[pl.ds(koff, tk), :] += jnp.where(low_half, dk2[:tk], dk2[tk:])

        run_ref[...] = jnp.zeros_like(run_ref)
        for pre in [a_sums(n_full + dd, dd) for dd in reversed(range(ndiag))]:
            a_finish(*pre)

        def two_a(carry):
            it, _ = carry
            kb = n_full - 1 - 2 * it
            first, second = a_sums(kb, None), a_sums(kb - 1, None)
            a_finish(*first)
            a_finish(*second)
            return it + 1, _sb_rest_is_zero(run_ref, bound)

        trips, _ = lax.while_loop(lambda c: jnp.logical_and(c[0] < n_full // 2, jnp.logical_not(c[1])), two_a,
                                  (jnp.int32(0), _sb_rest_is_zero(run_ref, bound)))

        run_ref[...] = jnp.zeros_like(run_ref)
        dq_acc[...] = jnp.zeros_like(dq_acc)
        kb_first = n_full - 2 * trips

        def two_b(it, carry):
            first, second = b_sums(kb_first + 2 * it, None), b_sums(kb_first + 2 * it + 1, None)
            b_finish(*first)
            b_finish(*second)
            return carry

        lax.fori_loop(0, trips, two_b, 0)
        for pre in [b_sums(n_full + dd, dd) for dd in range(ndiag)]:
            b_finish(*pre)
        dq_ref[...] = dq_acc[...] * scale

    blk = pl.BlockSpec((tq, LANES), lambda p, i: (i, p))
    full = pl.BlockSpec((s, LANES), lambda p, i: (0, p))
    tri = pl.BlockSpec((2 * tk, 2 * tk), lambda p, i: (0, 0))
    return _call_with_exchange(
        body, exchange, grid=(npairs, nq),
        in_specs=[blk, full, full, blk, tri, tri],
        out_specs=[blk, full, full],
        out_shape=[jax.ShapeDtypeStruct((s, d), F32)] * 3,
        scratch_shapes=[pltpu.VMEM((s // tk, tq, 2 * tk), F32), pltpu.VMEM((s // tk, tq, 2 * tk), F32),
                        pltpu.VMEM((2, tq, tk), F32), pltpu.VMEM((tq, LANES), F32), pltpu.VMEM((8, LANES), F32)],
        name="sb_attn_bwd", args=(qn, kn, vb, do, tri_ge2, tri_le2))


def _hg_consts(c):
    levels = []
    h = c // 2
    while h >= 1:
        levels.append(h)
        h //= 2
    t = np.arange(c)
    j = t[None, :]
    rows, masks = [], []
    for h in levels:
        blk = t // (2 * h)
        mid = blk * 2 * h + h - 1
        second = (t % (2 * h)) >= h
        rows.append(second[:, None] & (j > mid[:, None]) & (j <= t[:, None]))
        rows.append((~second)[:, None] & (j > t[:, None]) & (j <= mid[:, None]))
        masks.append((blk[:, None] == blk[None, :]) & second[:, None] & (~second)[None, :])
    rows.append(j <= t[:, None])
    rows.append(j > t[:, None])
    masks.append(t[:, None] == t[None, :])
    m_all = np.concatenate(rows, axis=0).astype(np.float32)
    mask_all = np.stack(masks, axis=0).astype(np.float32)
    suffix = (t[None, :] >= t[:, None]).astype(np.float32)
    return len(levels), jnp.asarray(m_all, BF16), jnp.asarray(mask_all, F32), jnp.asarray(suffix, BF16)


def _split3(x):
    hi = x.astype(BF16)
    r1 = x - hi.astype(F32)
    mid = r1.astype(BF16)
    lo = (r1 - mid.astype(F32)).astype(BF16)
    return jnp.concatenate([hi, mid, lo], axis=1)


def _join3(e):
    n = e.shape[1] // 3
    return e[:, :n] + e[:, n:2 * n] + e[:, 2 * n:]


def _hg_gates(qr, fr, lb):
    sq = _sigmoid(qr)
    sf = _sigmoid(fr)
    forget = lb + (1.0 - lb) * sf
    return qr * sq, sq, sf, forget, jnp.log(forget), 1.0 - forget


def _hg_scores(q, k, expo, masks, nlev, c):
    qb, kb = q.astype(BF16), k.astype(BF16)
    a = masks[nlev] * _dot(qb, kb, NT)
    scaled = []
    for li in range(nlev):
        fq = jnp.exp(expo[(2 * li) * c:(2 * li + 1) * c])
        fk = jnp.exp(expo[(2 * li + 1) * c:(2 * li + 2) * c])
        qs, ks = (q * fq).astype(BF16), (k * fk).astype(BF16)
        a = a + masks[li] * _dot(qs, ks, NT)
        scaled.append((qs, ks, fq, fk))
    return a, scaled, qb, kb


def _hg_heads_per_step(nh):
    return 2 if nh % 2 == 0 else 1


def _hg_fwd(proj, lb_row, gain_row, exchange=None):
    s, d4 = proj.shape
    d = d4 // 4
    nh = d // HG_HEAD_DIM
    c = min(HG_CHUNK, s)
    tb = _pick(s, 512)
    ncb = tb // c
    nlev, m_all, mask_all, _ = _hg_consts(c)
    nrow = m_all.shape[0]

    hp = _hg_heads_per_step(nh)
    wide = hp * HG_HEAD_DIM

    def body(q_ref, f_ref, i_ref, g_ref, lb_ref, gain_ref, mall_ref, mask_ref, y_ref, o_ref, st_out_ref, st_ref):
        b = pl.program_id(1)

        @pl.when(b == 0)
        def _():
            st_ref[...] = jnp.zeros_like(st_ref)

        gain = gain_ref[...]

        def chunk(ci, carry):
            rows = pl.ds(pl.multiple_of(ci * c, c), c)
            for hh in range(hp):
                cols = slice(hh * HG_HEAD_DIM, (hh + 1) * HG_HEAD_DIM)
                q, _, _, _, lf, k = _hg_gates(q_ref[rows, cols], f_ref[rows, cols], lb_ref[:, cols])
                v = i_ref[rows, cols].astype(BF16)
                expo = _join3(_dot(mall_ref[...], _split3(lf)))
                st = st_ref[hh]
                st_out_ref[ci, hh] = st
                a, _, _, _ = _hg_scores(q, k, expo, mask_ref[...], nlev, c)
                b_cum = expo[2 * nlev * c:(2 * nlev + 1) * c]
                e_tail = expo[(2 * nlev + 1) * c:(2 * nlev + 2) * c]
                q_in = (q * jnp.exp(b_cum)).astype(BF16)
                o = _dot(q_in, st.astype(BF16), NT) + _dot(a.astype(BF16), v)
                k_dec = (k * jnp.exp(e_tail)).astype(BF16)
                st_ref[hh] = st * jnp.exp(b_cum[c - 1:c, :]) + _dot(v, k_dec, TN)
                o_ref[rows, cols] = o
                r = lax.rsqrt(jnp.mean(o * o, axis=-1, keepdims=True) + NORM_EPS)
                y_ref[rows, cols] = (o * r * gain * _sigmoid(g_ref[rows, cols])).astype(y_ref.dtype)
            return carry

        lax.fori_loop(0, ncb, chunk, 0)

    part = lambda k: pl.BlockSpec((tb, wide), lambda h, b: (b, k * (nh // hp) + h))
    head_row = pl.BlockSpec((1, wide), lambda h, b: (0, h))
    tok = pl.BlockSpec((tb, wide), lambda h, b: (b, h))
    return _call_with_exchange(
        body, exchange, grid=(nh // hp, s // tb),
        in_specs=[part(0), part(1), part(2), part(3), head_row,
                  pl.BlockSpec((1, HG_HEAD_DIM), lambda h, b: (0, 0)),
                  pl.BlockSpec((nrow, c), lambda h, b: (0, 0)),
                  pl.BlockSpec((nlev + 1, c, c), lambda h, b: (0, 0, 0))],
        out_specs=[tok, tok, pl.BlockSpec((ncb, hp, HG_HEAD_DIM, HG_HEAD_DIM), lambda h, b: (b, h, 0, 0))],
        out_shape=[jax.ShapeDtypeStruct((s, d), BF16), jax.ShapeDtypeStruct((s, d), F32),
                   jax.ShapeDtypeStruct((s // c, nh, HG_HEAD_DIM, HG_HEAD_DIM), F32)],
        scratch_shapes=[pltpu.VMEM((hp, HG_HEAD_DIM, HG_HEAD_DIM), F32)],
        name="hg_fwd", args=(proj, proj, proj, proj, lb_row, gain_row, m_all, mask_all))


def _hg_bwd(proj, lb_row, gain_row, o_saved, states, dy, exchange=None):
    s, d4 = proj.shape
    d = d4 // 4
    nh = d // HG_HEAD_DIM
    c = min(HG_CHUNK, s)
    tb = _pick(s, 512)
    ncb = tb // c
    nb = s // tb
    nlev, m_all, mask_all, suffix = _hg_consts(c)
    nrow = m_all.shape[0]
    hp = _hg_heads_per_step(nh)
    wide = hp * HG_HEAD_DIM

    def body(q_ref, f_ref, i_ref, g_ref, lb_ref, gain_ref, o_ref, st_in_ref, dy_ref, mall_ref, mask_ref, suf_ref,
             dq_ref, df_ref, di_ref, dg_ref, dlb_ref, dgain_ref, dst_ref, run_ref):
        b = pl.program_id(1)

        @pl.when(b == 0)
        def _():
            dst_ref[...] = jnp.zeros_like(dst_ref)
            run_ref[...] = jnp.zeros_like(run_ref)
            dlb_ref[...] = jnp.zeros_like(dlb_ref)
            dgain_ref[...] = jnp.zeros_like(dgain_ref)

        gain = gain_ref[...]

        def head_chunk(ci, rows, hh, cols):
            lb = lb_ref[:, cols]
            qr, fr = q_ref[rows, cols], f_ref[rows, cols]
            q, sq, sf, forget, lf, k = _hg_gates(qr, fr, lb)
            v = i_ref[rows, cols].astype(BF16)
            expo = _join3(_dot(mall_ref[...], _split3(lf)))
            masks = mask_ref[...]
            o = o_ref[rows, cols]
            dyv = dy_ref[rows, cols]
            sg = _sigmoid(g_ref[rows, cols])
            r = lax.rsqrt(jnp.mean(o * o, axis=-1, keepdims=True) + NORM_EPS)
            ohat = o * r
            dyn = dyv * sg
            dg_ref[rows, cols] = (dyv * ohat * gain * sg * (1.0 - sg)).astype(dg_ref.dtype)
            dgain_ref[:, cols] += jnp.sum(dyn * ohat, axis=0, keepdims=True)
            dohat = dyn * gain
            do = (r * (dohat - ohat * jnp.mean(dohat * ohat, axis=-1, keepdims=True))).astype(BF16)
            dst = dst_ref[hh]
            dstb = dst.astype(BF16)
            a, scaled, qb, kb = _hg_scores(q, k, expo, masks, nlev, c)
            f_cum = jnp.exp(expo[2 * nlev * c:(2 * nlev + 1) * c])
            f_tail = jnp.exp(expo[(2 * nlev + 1) * c:(2 * nlev + 2) * c])
            q_in = (q * f_cum).astype(BF16)
            k_dec = (k * f_tail).astype(BF16)
            t_in = _join3(_dot(do, _split3(st_in_ref[ci, hh])))
            t_st = _join3(_dot(v, _split3(dst)))
            da = _dot(do, v, NT)
            dam = (masks[nlev] * da).astype(BF16)
            dq = t_in * f_cum + _dot(dam, kb)
            dk = t_st * f_tail + _dot(dam, qb, TN)
            db = q_in.astype(F32) * t_in - k_dec.astype(F32) * t_st
            for li in range(nlev):
                qs, ks, fq, fk = scaled[li]
                dam = (masks[li] * da).astype(BF16)
                t_q = _dot(dam, ks)
                t_k = _dot(dam, qs, TN)
                dq = dq + t_q * fq
                dk = dk + t_k * fk
                db = db + (qs.astype(F32) * t_q - ks.astype(F32) * t_k)
            dv = _dot(a.astype(BF16), do, TN) + _dot(k_dec, dstb, NT)
            dst_ref[hh] = dst * f_cum[c - 1:c, :] + _dot(do, q_in, TN)
            dlf = _join3(_dot(suf_ref[...], _split3(db))) + run_ref[hh]
            run_ref[hh] = dlf[0:1, :]
            dforget = dlf / forget - dk
            dlb_ref[:, cols] += jnp.sum(dforget * (1.0 - sf), axis=0, keepdims=True)
            df_ref[rows, cols] = (dforget * (1.0 - lb) * sf * (1.0 - sf)).astype(df_ref.dtype)
            dq_ref[rows, cols] = (dq * sq * (1.0 + qr * (1.0 - sq))).astype(dq_ref.dtype)
            di_ref[rows, cols] = dv.astype(di_ref.dtype)

        def chunk(it, carry):
            ci = ncb - 1 - it
            rows = pl.ds(pl.multiple_of(ci * c, c), c)
            for hh in range(hp):
                head_chunk(ci, rows, hh, slice(hh * HG_HEAD_DIM, (hh + 1) * HG_HEAD_DIM))
            return carry

        lax.fori_loop(0, ncb, chunk, 0)

    part = lambda k: pl.BlockSpec((tb, wide), lambda h, b: (nb - 1 - b, k * (nh // hp) + h))
    head_row = pl.BlockSpec((1, wide), lambda h, b: (0, h))
    tok = pl.BlockSpec((tb, wide), lambda h, b: (nb - 1 - b, h))
    const2 = lambda shape: pl.BlockSpec(shape, lambda h, b: (0, 0))
    return _call_with_exchange(
        body, exchange, grid=(nh // hp, nb),
        in_specs=[part(0), part(1), part(2), part(3), head_row, const2((1, HG_HEAD_DIM)), tok,
                  pl.BlockSpec((ncb, hp, HG_HEAD_DIM, HG_HEAD_DIM), lambda h, b: (nb - 1 - b, h, 0, 0)),
                  tok, const2((nrow, c)), pl.BlockSpec((nlev + 1, c, c), lambda h, b: (0, 0, 0)), const2((c, c))],
        out_specs=[tok, tok, tok, tok, head_row, head_row],
        out_shape=[jax.ShapeDtypeStruct((s, d), BF16)] * 4 + [jax.ShapeDtypeStruct((1, d), F32)] * 2,
        scratch_shapes=[pltpu.VMEM((hp, HG_HEAD_DIM, HG_HEAD_DIM), F32), pltpu.VMEM((hp, 1, HG_HEAD_DIM), F32)],
        name="hg_bwd", args=(proj, proj, proj, proj, lb_row, gain_row, o_saved, states, dy, m_all, mask_all, suffix))


def _lb_fwd(logits):
    n, d = logits.shape

    def body(l_ref, lb_ref, p_ref):
        rows = [l_ref[i:i + 1, :] for i in range(n)]
        m = functools.reduce(jnp.maximum, rows)
        es = [jnp.exp(r - m) for r in rows]
        tot = functools.reduce(lambda a, b: a + b, es)
        ps = [e / tot for e in es]
        run = jnp.zeros_like(ps[0])
        for i in range(n):
            run = run + ps[i]
            lb_ref[i:i + 1, :] = run - ps[0]
            p_ref[i:i + 1, :] = ps[i]

    return pl.pallas_call(
        body, out_shape=[jax.ShapeDtypeStruct((n, d), F32)] * 2, name="lb_fwd",
    )(logits)


def _lb_bwd(p, dlb):
    n, d = p.shape

    def body(p_ref, dlb_ref, dl_ref):
        ps = [p_ref[i:i + 1, :] for i in range(n)]
        ds = [dlb_ref[i:i + 1, :] for i in range(n)]
        total = functools.reduce(lambda a, b: a + b, ds)
        dps = []
        for i in range(n):
            dp = functools.reduce(lambda a, b: a + b, ds[i:])
            dps.append(dp - total if i == 0 else dp)
        inner = functools.reduce(lambda a, b: a + b, [pi * di for pi, di in zip(ps, dps)])
        for i in range(n):
            dl_ref[i:i + 1, :] = ps[i] * (dps[i] - inner)

    return pl.pallas_call(body, out_shape=jax.ShapeDtypeStruct((n, d), F32), name="lb_bwd")(p, dlb)


def _as2d(a):
    return a.reshape(-1, a.shape[-1])


def _adamw(w, m, v, grads):
    shape = w.shape
    w2, m2, v2 = _as2d(w), _as2d(m), _as2d(v)
    g2 = [_as2d(g) for g in grads]
    rows, cols = w2.shape
    tr = _pick(rows, 512)
    ng = len(g2)
    bc1 = 1.0 - ADAM_B1 ** ADAM_STEP
    bc2 = 1.0 - ADAM_B2 ** ADAM_STEP

    def body(w_ref, m_ref, v_ref, *rest):
        g = rest[0][...]
        for extra in rest[1:ng]:
            g = g + extra[...]
        g_out, d_out, m_out, v_out = rest[ng:]
        mn = ADAM_B1 * m_ref[...] + (1.0 - ADAM_B1) * g
        vn = ADAM_B2 * v_ref[...] + (1.0 - ADAM_B2) * (g * g)
        m_hat = mn / bc1
        v_hat = vn / bc2
        g_out[...] = g
        d_out[...] = -ADAM_LR * (m_hat / (jnp.sqrt(v_hat) + ADAM_EPS) + ADAM_WD * w_ref[...])
        m_out[...] = mn
        v_out[...] = vn

    spec = pl.BlockSpec((tr, cols), lambda i: (i, 0))
    outs = pl.pallas_call(
        body, grid=(rows // tr,), in_specs=[spec] * (3 + ng), out_specs=[spec] * 4,
        out_shape=[jax.ShapeDtypeStruct((rows, cols), F32)] * 4,
        compiler_params=_params(("parallel",)), name="adamw",
    )(w2, m2, v2, *g2)
    return tuple(o.reshape(shape) for o in outs)


def _sum_slots(parts, recv, chip):
    _, rows, cols = parts.shape
    tr = _pick(rows, 512)

    def body(chip_ref, own_ref, r0_ref, r1_ref, r2_ref, o_ref):
        f = lambda r: r[...].astype(F32)
        o_ref[...] = ((f(own_ref) + f(r0_ref)) + f(r1_ref)) + f(r2_ref)

    grid_spec = pltpu.PrefetchScalarGridSpec(
        num_scalar_prefetch=1, grid=(rows // tr,),
        in_specs=[pl.BlockSpec((None, tr, cols), lambda i, chip_ref: (chip_ref[0], i, 0))]
        + [pl.BlockSpec((None, tr, cols), functools.partial(lambda i, chip_ref, k: (k, i, 0), k=k)) for k in range(3)],
        out_specs=pl.BlockSpec((tr, cols), lambda i, chip_ref: (i, 0)))
    return pl.pallas_call(
        body, grid_spec=grid_spec, out_shape=jax.ShapeDtypeStruct((rows, cols), F32),
        compiler_params=_params(("parallel",)), name="sum_slots",
    )(chip, parts, recv, recv, recv)


def _sum_devices(gathered):
    n, rows, cols = gathered.shape

    def body(g_ref, o_ref):
        acc = g_ref[0]
        for i in range(1, n):
            acc = acc + g_ref[i]
        o_ref[...] = acc

    return pl.pallas_call(body, out_shape=jax.ShapeDtypeStruct((rows, cols), F32), name="sum_devices")(gathered)


def _coords():
    return lax.axis_index("x"), lax.axis_index("y"), lax.axis_index("c")


def _chip_peers(x, y, c):
    out = []
    for fx, fy in ((0, 1), (1, 0), (1, 1)):
        px = 1 - x if fx else x
        py = 1 - y if fy else y
        out.append(((px, py, c), 2 * px + py))
    return out


class _ChipExchange:
    def __init__(self, kind, arrays):
        self.kind, self.arrays, self.n = kind, list(arrays), len(arrays)
        lead = lambda a: (N_CHIPS,) + a.shape if kind == "gather" else (3,) + a.shape[1:]
        self.out_shape = [jax.ShapeDtypeStruct(lead(a), a.dtype) for a in self.arrays]
        self.scratch = [pltpu.SemaphoreType.DMA((3 * self.n,)), pltpu.SemaphoreType.DMA((3 * self.n,)),
                        pltpu.SemaphoreType.DMA((self.n,))]

    def copies(self, ins, outs, send_sems, recv_sems, local_sems):
        x, y, c = _coords()
        me = 2 * x + y
        starts, waits = [], []
        for t in range(self.n):
            if self.kind == "gather":
                own = pltpu.make_async_copy(ins[t], outs[t].at[me], local_sems.at[t])
                starts.append(own.start)
                waits.append(own.wait)
            for k, (peer, peer_chip) in enumerate(_chip_peers(x, y, c)):
                sems = dict(send_sem=send_sems.at[3 * t + k], recv_sem=recv_sems.at[3 * t + k],
                            device_id=peer, device_id_type=MESH)
                if self.kind == "gather":
                    send = pltpu.make_async_remote_copy(src_ref=ins[t], dst_ref=outs[t].at[me], **sems)
                    recv = pltpu.make_async_remote_copy(src_ref=ins[t], dst_ref=outs[t].at[peer_chip], **sems)
                else:
                    send = pltpu.make_async_remote_copy(src_ref=ins[t].at[peer_chip], dst_ref=outs[t].at[k], **sems)
                    recv = send
                starts.append(send.start)
                waits += [send.wait_send, recv.wait_recv]
        return starts, waits

    def run(self, name):
        n = self.n

        def body(*refs):
            starts, waits = self.copies(refs[:n], refs[n:2 * n], *refs[2 * n:])
            for f in starts + waits:
                f()

        return pl.pallas_call(body, in_specs=[HBM_SPEC] * n, out_specs=[HBM_SPEC] * n, out_shape=self.out_shape,
                              scratch_shapes=self.scratch, name=name)(*self.arrays)


def _call_with_exchange(body, exchange, *, grid, in_specs, out_specs, out_shape, scratch_shapes, name, args):
    if exchange is None:
        outs = pl.pallas_call(body, grid=grid, in_specs=in_specs, out_specs=out_specs, out_shape=out_shape,
                              scratch_shapes=scratch_shapes,
                              compiler_params=_params(("parallel",) + ("arbitrary",) * (len(grid) - 1)),
                              name=name)(*args)
        return outs, []
    n_in, n_out, n_scr, n = len(in_specs), len(out_specs), len(scratch_shapes), exchange.n

    def wrapped(*refs):
        ins, ex_in = refs[:n_in], refs[n_in:n_in + n]
        outs = refs[n_in + n:n_in + n + n_out]
        ex_out = refs[n_in + n + n_out:n_in + 2 * n + n_out]
        scr = refs[n_in + 2 * n + n_out:n_in + 2 * n + n_out + n_scr]
        sems = refs[n_in + 2 * n + n_out + n_scr:]
        ids = [pl.program_id(a) for a in range(len(grid))]
        first = functools.reduce(jnp.logical_and, [i == 0 for i in ids])
        last = functools.reduce(jnp.logical_and, [i == g - 1 for i, g in zip(ids, grid)])

        @pl.when(first)
        def _():
            for f in exchange.copies(ex_in, ex_out, *sems)[0]:
                f()

        body(*ins, *outs, *scr)

        @pl.when(last)
        def _():
            for f in exchange.copies(ex_in, ex_out, *sems)[1]:
                f()

    res = pl.pallas_call(
        wrapped, grid=grid, in_specs=list(in_specs) + [HBM_SPEC] * n, out_specs=list(out_specs) + [HBM_SPEC] * n,
        out_shape=list(out_shape) + exchange.out_shape, scratch_shapes=list(scratch_shapes) + exchange.scratch,
        compiler_params=_params(("arbitrary",) * len(grid)), name=name + "_" + exchange.kind,
    )(*args, *exchange.arrays)
    return res[:n_out], res[n_out:]


def _swap_cores(arrs):
    n = len(arrs)

    def body(*refs):
        ins, outs = refs[:n], refs[n:2 * n]
        send_sems, recv_sems = refs[2 * n:]
        x, y, c = _coords()
        copies = []
        for t in range(n):
            cp = pltpu.make_async_remote_copy(
                src_ref=ins[t], dst_ref=outs[t], send_sem=send_sems.at[t], recv_sem=recv_sems.at[t],
                device_id=(x, y, 1 - c), device_id_type=MESH)
            cp.start()
            copies.append(cp)
        for cp in copies:
            cp.wait()

    return pl.pallas_call(
        body, in_specs=[HBM_SPEC] * n, out_specs=[HBM_SPEC] * n,
        out_shape=[jax.ShapeDtypeStruct(a.shape, a.dtype) for a in arrs],
        scratch_shapes=[pltpu.SemaphoreType.DMA((n,)), pltpu.SemaphoreType.DMA((n,))],
        name="swap_cores",
    )(*arrs)


def _gather_devices(a):
    def body(in_ref, out_ref, send_sems, recv_sems, local_sem):
        x, y, c = _coords()
        me = 4 * x + 2 * y + c
        own = pltpu.make_async_copy(in_ref, out_ref.at[me], local_sem)
        own.start()
        waits = [own.wait]
        for k in range(1, N_DEVICES):
            px = 1 - x if k & 4 else x
            py = 1 - y if k & 2 else y
            pc = 1 - c if k & 1 else c
            peer = (px, py, pc)
            send = pltpu.make_async_remote_copy(
                src_ref=in_ref, dst_ref=out_ref.at[me], send_sem=send_sems.at[k - 1], recv_sem=recv_sems.at[k - 1],
                device_id=peer, device_id_type=MESH)
            send.start()
            recv = pltpu.make_async_remote_copy(
                src_ref=in_ref, dst_ref=out_ref.at[4 * px + 2 * py + pc], send_sem=send_sems.at[k - 1],
                recv_sem=recv_sems.at[k - 1], device_id=peer, device_id_type=MESH)
            waits += [send.wait_send, recv.wait_recv]
        for w in waits:
            w()

    return pl.pallas_call(
        body, in_specs=[HBM_SPEC], out_specs=HBM_SPEC,
        out_shape=jax.ShapeDtypeStruct((N_DEVICES,) + a.shape, a.dtype),
        scratch_shapes=[pltpu.SemaphoreType.DMA((N_DEVICES - 1,)), pltpu.SemaphoreType.DMA((N_DEVICES - 1,)),
                        pltpu.SemaphoreType.DMA],
        name="gather_devices",
    )(a)


def _mlp_grad_epilogue(r, u):
    return r * (2.0 * jnp.maximum(u, 0.0))


def kernel(x, norm_gains, sb_w_qkv, sb_q_gain, sb_k_gain, sb_w_o, hg_w_in, hg_lb_logits, hg_norm_gain, hg_w_o, mlp_w1, mlp_w2, loss_target, m_norm_gains, m_sb_w_qkv, m_sb_q_gain, m_sb_k_gain, m_sb_w_o, m_hg_w_in, m_hg_lb_logits, m_hg_norm_gain, m_hg_w_o, m_mlp_w1, m_mlp_w2, v_norm_gains, v_sb_w_qkv, v_sb_q_gain, v_sb_k_gain, v_sb_w_o, v_hg_w_in, v_hg_lb_logits, v_hg_norm_gain, v_hg_w_o, v_mlp_w1, v_mlp_w2):
    depth = norm_gains.shape[0]
    n_sb, n_hg = sb_w_qkv.shape[0], hg_w_in.shape[0]
    xs, tgt = x[0], loss_target[0]
    s, d = xs.shape
    dq = d // N_CHIPS
    cx, cy, cc = _coords()
    chip = 2 * cx + cy
    chip_arr = jnp.reshape(chip, (1,)).astype(jnp.int32)

    def mixer_weights(layer):
        j = layer // 2
        return (sb_w_qkv[j], sb_w_o[j]) if layer % 2 == 0 else (hg_w_in[j], hg_w_o[j])

    w_in_g, ng_g, lbl_g = _ChipExchange(
        "gather", [mixer_weights(0)[0].astype(BF16), norm_gains, hg_lb_logits]).run("gather_first")
    gains = jnp.transpose(ng_g, (1, 2, 0, 3)).reshape(depth, 2, d)
    logits = jnp.transpose(lbl_g, (1, 0, 2)).reshape(n_hg, d)
    lbs, lb_p = _lb_fwd(logits)
    qg_rows = [jnp.tile(sb_q_gain[j], d // SB_HEAD_DIM)[None] for j in range(n_sb)]
    kg_rows = [jnp.tile(sb_k_gain[j], d // SB_HEAD_DIM)[None] for j in range(n_sb)]

    saved, wts = [], []
    xc = xs
    for layer in range(depth):
        j = layer // 2
        ahead = [mixer_weights(layer)[1], mlp_w1[layer], mlp_w2[layer]]
        if layer + 1 < depth:
            ahead.append(mixer_weights(layer + 1)[0])
        gather = _ChipExchange("gather", [a.astype(BF16) for a in ahead])
        h1 = _rmsnorm_fwd(xc, gains[layer, 0][None])
        if layer % 2 == 0:
            qkv = _mm_fwd_cols(h1, w_in_g, name="sb_qkv")
            qn, kn, vb = _qk_norm_fwd(qkv, qg_rows[j], kg_rows[j])
            o, moved = _sb_attn_fwd(qn, kn, vb, gather)
            x_mid = _mm_fwd_rows(o, moved[0], residual=xc, name="sb_out")
            mix = (qkv, qn, kn, vb, o)
        else:
            proj = _mm_fwd_cols(h1, w_in_g, name="hg_in")
            (y, o, states), moved = _hg_fwd(proj, lbs[j][None], hg_norm_gain[j][None], gather)
            x_mid = _mm_fwd_rows(y, moved[0], residual=xc, name="hg_out")
            mix = (proj, y, o, states)
        w_out_g, w1_g, w2_g = moved[:3]
        h2 = _rmsnorm_fwd(x_mid, gains[layer, 1][None])
        u = _mm_fwd_cols(h2, w1_g, name="mlp_up")
        x_out = _mm_fwd_rows(u, w2_g, residual=x_mid, a_fn=_relu2, name="mlp_down")
        saved.append((xc, h1, mix, x_mid, h2, u))
        wts.append((w_in_g, w_out_g, w1_g, w2_g))
        w_in_g = moved[3] if layer + 1 < depth else None
        xc = x_out

    sq, dx = _loss_head(xc, tgt)
    loss = lax.psum(jnp.sum(sq) * (0.5 / d), ("x", "y", "c"))

    dgains = [[None, None] for _ in range(depth)]
    dqg, dkg = [None] * n_sb, [None] * n_sb
    dhgain, dlb = [None] * n_hg, [None] * n_hg
    grads, received, pending = {}, {}, []

    def ready(key, parts):
        grads[key] = parts
        pending.append(key)

    def scatter_of(keys):
        return _ChipExchange("scatter", [grads[k] for k in keys]) if keys else None

    def sent(keys, moved):
        for k, r in zip(keys, moved):
            received[k] = r
            pending.remove(k)

    for layer in reversed(range(depth)):
        j = layer // 2
        x_in, h1, mix, x_mid, h2, u = saved[layer]
        w_in_g, w_out_g, w1_g, w2_g = wts[layer]
        du = _mm_bwd_rows(dx, w2_g, name="mlp_down_dx", out_dtype=BF16, epi_fn=_mlp_grad_epilogue, epi_args=(u,))
        ready(("w2", layer), _mm_dw_rows(u, dx, a_fn=_relu2, name="mlp_down_dw"))
        dh2 = _mm_bwd_cols(du, w1_g, name="mlp_up_dx")
        ready(("w1", layer), _mm_dw_cols(h2, du, name="mlp_up_dw"))
        dx, dgains[layer][1] = _rmsnorm_bwd(dh2, x_mid, gains[layer, 1][None], dx)
        if layer % 2 == 0:
            qkv, qn, kn, vb, o = mix
            do = _mm_bwd_rows(dx, w_out_g, name="sb_out_dx")
            ready(("out", layer), _mm_dw_rows(o, dx, name="sb_out_dw"))
            keys = list(pending)
            (dqn, dkn, dv), moved = _sb_attn_bwd(qn, kn, vb, do, scatter_of(keys))
            sent(keys, moved)
            dqkv, dqg_lane, dkg_lane = _qk_norm_bwd(qkv, qg_rows[j], kg_rows[j], dqn, dkn, dv)
            dqg[j] = jnp.sum(dqg_lane.reshape(-1, SB_HEAD_DIM), axis=0)
            dkg[j] = jnp.sum(dkg_lane.reshape(-1, SB_HEAD_DIM), axis=0)
            dh1 = _mm_bwd_cols(dqkv, w_in_g, name="sb_qkv_dx")
            ready(("in", layer), _mm_dw_cols(h1, dqkv, name="sb_qkv_dw"))
        else:
            proj, y, o, states = mix
            dy = _mm_bwd_rows(dx, w_out_g, name="hg_out_dx")
            ready(("out", layer), _mm_dw_rows(y, dx, name="hg_out_dw"))
            keys = list(pending)
            (dq_raw, df_raw, di, dg, dlb_row, dgain_heads), moved = _hg_bwd(
                proj, lbs[j][None], hg_norm_gain[j][None], o, states, dy, scatter_of(keys))
            sent(keys, moved)
            dlb[j] = dlb_row
            dhgain[j] = jnp.sum(dgain_heads.reshape(-1, HG_HEAD_DIM), axis=0)
            dproj = jnp.concatenate([dq_raw, df_raw, di, dg], axis=1)
            dh1 = _mm_bwd_cols(dproj, w_in_g, name="hg_in_dx")
            ready(("in", layer), _mm_dw_cols(h1, dproj, name="hg_in_dw"))
        dx, dgains[layer][0] = _rmsnorm_bwd(dh1, x_in, gains[layer, 0][None], dx)
    keys = list(pending)
    sent(keys, scatter_of(keys).run("scatter_last"))
    grad_x = dx[None]
    dlogits = _lb_bwd(lb_p, jnp.concatenate(dlb, axis=0))

    def chip_sum(kind, layers):
        return jnp.stack([_sum_slots(grads[kind, l], received[kind, l], chip_arr) for l in layers])

    sb_layers, hg_layers = range(0, depth, 2), range(1, depth, 2)
    big_w = [sb_w_qkv, sb_w_o, hg_w_in, hg_w_o, mlp_w1, mlp_w2]
    big_m = [m_sb_w_qkv, m_sb_w_o, m_hg_w_in, m_hg_w_o, m_mlp_w1, m_mlp_w2]
    big_v = [v_sb_w_qkv, v_sb_w_o, v_hg_w_in, v_hg_w_o, v_mlp_w1, v_mlp_w2]
    chip_sums = [chip_sum("in", sb_layers), chip_sum("out", sb_layers), chip_sum("in", hg_layers),
                 chip_sum("out", hg_layers), chip_sum("w1", range(depth)), chip_sum("w2", range(depth))]
    other_core = _swap_cores(chip_sums)
    big = [_adamw(w, m, v, [a, b]) for w, m, v, a, b in zip(big_w, big_m, big_v, chip_sums, other_core)]

    n_small = 2 * depth + n_hg + 3
    small_rows = -(-n_small // 8) * 8
    small = jnp.zeros((small_rows, d), F32)
    small = small.at[0:2 * depth].set(jnp.concatenate([r for pair in dgains for r in pair], axis=0))
    small = small.at[2 * depth:2 * depth + n_hg].set(dlogits)
    base = 2 * depth + n_hg
    small = small.at[base, 0:n_sb * SB_HEAD_DIM].set(jnp.concatenate(dqg))
    small = small.at[base + 1, 0:n_sb * SB_HEAD_DIM].set(jnp.concatenate(dkg))
    small = small.at[base + 2, 0:n_hg * HG_HEAD_DIM].set(jnp.concatenate(dhgain))
    small = _sum_devices(_gather_devices(small))
    my_cols = lambda a: lax.dynamic_slice_in_dim(a, chip * dq, dq, axis=1)
    g_ng = my_cols(small[0:2 * depth]).reshape(norm_gains.shape)
    g_lbl = my_cols(small[2 * depth:base])
    g_qg = small[base, 0:n_sb * SB_HEAD_DIM].reshape(sb_q_gain.shape)
    g_kg = small[base + 1, 0:n_sb * SB_HEAD_DIM].reshape(sb_k_gain.shape)
    g_hgn = small[base + 2, 0:n_hg * HG_HEAD_DIM].reshape(hg_norm_gain.shape)
    r_ng = _adamw(norm_gains, m_norm_gains, v_norm_gains, [g_ng])
    r_qg = _adamw(sb_q_gain, m_sb_q_gain, v_sb_q_gain, [g_qg])
    r_kg = _adamw(sb_k_gain, m_sb_k_gain, v_sb_k_gain, [g_kg])
    r_lbl = _adamw(hg_lb_logits, m_hg_lb_logits, v_hg_lb_logits, [g_lbl])
    r_hgn = _adamw(hg_norm_gain, m_hg_norm_gain, v_hg_norm_gain, [g_hgn])

    per_weight = [r_ng, big[0], r_qg, r_kg, big[1], big[2], r_lbl, r_hgn, big[3], big[4], big[5]]
    outs = [loss, grad_x]
    for field in range(4):
        outs += [r[field] for r in per_weight]
    return tuple(outs)
```

```python
import functools
import math

import numpy as np
import jax
import jax.numpy as jnp
from jax import lax
from jax.experimental import pallas as pl
from jax.experimental.pallas import tpu as pltpu

F32 = jnp.float32
BF16 = jnp.bfloat16
GRAD_SLOT_DTYPE = jnp.bfloat16

NORM_EPS = 1e-6
SB_HEAD_DIM = 64
HG_HEAD_DIM = 128
HG_CHUNK = 128
LANES = 128
VMEM_LIMIT_BYTES = 56 * 2 ** 20
N_CHIPS = 4
N_DEVICES = 8

ADAM_LR = 0.001
ADAM_B1 = 0.9
ADAM_B2 = 0.999
ADAM_EPS = 1e-08
ADAM_WD = 0.01
ADAM_STEP = 10

MESH = pl.DeviceIdType.MESH
HBM_SPEC = pl.BlockSpec(memory_space=pltpu.HBM)

NN = (((1,), (0,)), ((), ()))
NT = (((1,), (1,)), ((), ()))
TN = (((0,), (0,)), ((), ()))


def _params(sem=None):
    return pltpu.CompilerParams(dimension_semantics=sem, vmem_limit_bytes=VMEM_LIMIT_BYTES)


def _pick(dim, pref):
    for t in (1024, 768, 512, 384, 256, 128, 64, 32, 16, 8):
        if t <= pref and dim % t == 0:
            return t
    return dim


def _dot(a, b, dims=NN):
    return lax.dot_general(a, b, dims, preferred_element_type=F32)


def _sigmoid(x):
    e = jnp.exp(-jnp.abs(x))
    return jnp.where(x >= 0, 1.0, e) / (1.0 + e)


def _matmul(a, b, *, mode, grid, a_block, a_map, b_block, b_map, o_block, o_map, out_shape, out_dtype, name,
            a_fn=None, epi_fn=None, epi_args=(), epi_row_args=(), col_sums=False):
    nk = grid[2]
    dims = {"nn": NN, "nt": NT, "tn": TN}[mode]
    n_epi = len(epi_args) + len(epi_row_args)
    n_out = 2 if col_sums else 1
    tn = o_block[-1]
    assert not col_sums or grid[1] == 1, "the column sums stay resident only with one tile along N"

    def body(a_ref, b_ref, *rest):
        epi_refs = rest[:n_epi]
        o_ref = rest[n_epi]
        kk = pl.program_id(2)

        def emit(r):
            if epi_fn is not None:
                r = epi_fn(r, *[e[...] for e in epi_refs])
            if col_sums:
                r, row = r
                sums_ref = rest[n_epi + 1]
                first = pl.program_id(0) == 0

                @pl.when(first)
                def _():
                    sums_ref[...] = row

                @pl.when(jnp.logical_not(first))
                def _():
                    sums_ref[...] += row
            o_ref[...] = r.astype(o_ref.dtype)

        av = a_ref[...]
        if a_fn is not None:
            av = a_fn(av)
        part = _dot(av.astype(BF16), b_ref[...].astype(BF16), dims)
        if nk == 1:
            emit(part)
            return
        acc_ref = rest[n_epi + n_out]

        @pl.when(kk == 0)
        def _():
            acc_ref[...] = part

        @pl.when(kk > 0)
        def _():
            acc_ref[...] += part

        @pl.when(kk == nk - 1)
        def _():
            emit(acc_ref[...])

    acc_shape = tuple(d for d in o_block if d is not None)
    row_spec = pl.BlockSpec((1, tn), lambda i, j, kk: (0, j))
    in_specs = [pl.BlockSpec(a_block, a_map), pl.BlockSpec(b_block, b_map)]
    in_specs += [pl.BlockSpec(o_block, o_map) for _ in epi_args] + [row_spec for _ in epi_row_args]
    out_specs, out_shapes = pl.BlockSpec(o_block, o_map), jax.ShapeDtypeStruct(out_shape, out_dtype)
    if col_sums:
        out_specs, out_shapes = [out_specs, row_spec], [out_shapes, jax.ShapeDtypeStruct((1, out_shape[-1]), F32)]
    return pl.pallas_call(
        body, grid=grid, in_specs=in_specs, out_specs=out_specs, out_shape=out_shapes,
        scratch_shapes=[pltpu.VMEM(acc_shape, F32)] if nk > 1 else [],
        compiler_params=_params(("arbitrary",) * 3 if col_sums else ("parallel", "parallel", "arbitrary")), name=name,
    )(a, b, *epi_args, *epi_row_args)


def _relu2(u):
    r = jnp.maximum(u, 0.0)
    return r * r


def _add(r, res):
    return r + res


def _mm_fwd_cols(a, wg, *, name):
    s, k = a.shape
    ncs = wg.shape[2]
    tm, tk, tn = _pick(s, 1024), _pick(k, 1024), _pick(ncs, 1024)
    npb = ncs // tn
    return _matmul(a, wg, mode="nn", grid=(s // tm, N_CHIPS * npb, k // tk),
                   a_block=(tm, tk), a_map=lambda i, j, kk: (i, kk),
                   b_block=(None, tk, tn), b_map=lambda i, j, kk: (j // npb, kk, j % npb),
                   o_block=(tm, tn), o_map=lambda i, j, kk: (i, j),
                   out_shape=(s, N_CHIPS * ncs), out_dtype=F32, name=name)


def _mm_fwd_rows(a, wg, *, residual, name, a_fn=None):
    s = a.shape[0]
    krs, n = wg.shape[1], wg.shape[2]
    tm, tk, tn = _pick(s, 1024), _pick(krs, 1024), _pick(n, 1024)
    kpb = krs // tk
    return _matmul(a, wg, mode="nn", grid=(s // tm, n // tn, N_CHIPS * kpb),
                   a_block=(tm, tk), a_map=lambda i, j, kk: (i, kk),
                   b_block=(None, tk, tn), b_map=lambda i, j, kk: (kk // kpb, kk % kpb, j),
                   o_block=(tm, tn), o_map=lambda i, j, kk: (i, j),
                   out_shape=(s, n), out_dtype=F32, name=name, a_fn=a_fn, epi_fn=_add, epi_args=(residual,))


def _rmsnorm_grad(dh, x, dx_res, gain):
    r = lax.rsqrt(jnp.mean(x * x, axis=-1, keepdims=True) + NORM_EPS)
    xhat = x * r
    dxhat = dh * gain
    dx = r * (dxhat - xhat * jnp.mean(dxhat * xhat, axis=-1, keepdims=True))
    return dx_res + dx, jnp.sum(dh * xhat, axis=0, keepdims=True)


def _mm_bwd_cols_norm(dy, wg, x, gain_row, dx_res, *, name):
    s = dy.shape[0]
    kw, ncs = wg.shape[1], wg.shape[2]
    tm, tk = _pick(s, 512), _pick(ncs, 1024)
    kpb = ncs // tk
    return _matmul(dy, wg, mode="nt", grid=(s // tm, 1, N_CHIPS * kpb),
                   a_block=(tm, tk), a_map=lambda i, j, kk: (i, kk),
                   b_block=(None, kw, tk), b_map=lambda i, j, kk: (kk // kpb, j, kk % kpb),
                   o_block=(tm, kw), o_map=lambda i, j, kk: (i, j),
                   out_shape=(s, kw), out_dtype=F32, name=name,
                   epi_fn=_rmsnorm_grad, epi_args=(x, dx_res), epi_row_args=(gain_row,), col_sums=True)


def _mm_bwd_rows(dy, wg, *, name, out_dtype=F32, epi_fn=None, epi_args=()):
    s, n = dy.shape
    krs = wg.shape[1]
    tm, tn, tk = _pick(s, 1024), _pick(krs, 1024), _pick(n, 1024)
    npb = krs // tn
    return _matmul(dy, wg, mode="nt", grid=(s // tm, N_CHIPS * npb, n // tk),
                   a_block=(tm, tk), a_map=lambda i, j, kk: (i, kk),
                   b_block=(None, tn, tk), b_map=lambda i, j, kk: (j // npb, j % npb, kk),
                   o_block=(tm, tn), o_map=lambda i, j, kk: (i, j),
                   out_shape=(s, N_CHIPS * krs), out_dtype=out_dtype, name=name, epi_fn=epi_fn, epi_args=epi_args)


def _mm_dw_cols(xa, dy, *, name):
    s, kx = xa.shape
    ncs = dy.shape[1] // N_CHIPS
    tm, tn, tk = _pick(kx, 1024), _pick(ncs, 1024), _pick(s, 1024)
    npb = ncs // tn
    return _matmul(xa, dy, mode="tn", grid=(kx // tm, N_CHIPS * npb, s // tk),
                   a_block=(tk, tm), a_map=lambda i, j, kk: (kk, i),
                   b_block=(tk, tn), b_map=lambda i, j, kk: (kk, j),
                   o_block=(None, tm, tn), o_map=lambda i, j, kk: (j // npb, i, j % npb),
                   out_shape=(N_CHIPS, kx, ncs), out_dtype=GRAD_SLOT_DTYPE, name=name)


def _mm_dw_rows(xa, dy, *, name, a_fn=None):
    s, n = dy.shape
    krs = xa.shape[1] // N_CHIPS
    tm, tn, tk = _pick(krs, 1024), _pick(n, 1024), _pick(s, 1024)
    mpb = krs // tm
    return _matmul(xa, dy, mode="tn", grid=(N_CHIPS * mpb, n // tn, s // tk),
                   a_block=(tk, tm), a_map=lambda i, j, kk: (kk, i),
                   b_block=(tk, tn), b_map=lambda i, j, kk: (kk, j),
                   o_block=(None, tm, tn), o_map=lambda i, j, kk: (i // mpb, i % mpb, j),
                   out_shape=(N_CHIPS, krs, n), out_dtype=GRAD_SLOT_DTYPE, name=name, a_fn=a_fn)


def _rmsnorm_fwd(x, gain_row):
    s, d = x.shape
    ts = _pick(s, 512)

    def body(x_ref, g_ref, h_ref):
        xv = x_ref[...]
        r = lax.rsqrt(jnp.mean(xv * xv, axis=-1, keepdims=True) + NORM_EPS)
        h_ref[...] = (xv * r * g_ref[...]).astype(h_ref.dtype)

    return pl.pallas_call(
        body, grid=(s // ts,),
        in_specs=[pl.BlockSpec((ts, d), lambda i: (i, 0)), pl.BlockSpec((1, d), lambda i: (0, 0))],
        out_specs=pl.BlockSpec((ts, d), lambda i: (i, 0)),
        out_shape=jax.ShapeDtypeStruct((s, d), BF16),
        compiler_params=_params(("parallel",)), name="rmsnorm_fwd",
    )(x, gain_row)


def _loss_head(y, target):
    s, d = y.shape
    ts = _pick(s, 512)

    def body(y_ref, t_ref, sq_ref, dy_ref):
        i = pl.program_id(0)
        err = y_ref[...] - t_ref[...]
        dy_ref[...] = err / d
        part = jnp.sum(err * err, axis=0, keepdims=True)

        @pl.when(i == 0)
        def _():
            sq_ref[...] = part

        @pl.when(i > 0)
        def _():
            sq_ref[...] += part

    return pl.pallas_call(
        body, grid=(s // ts,),
        in_specs=[pl.BlockSpec((ts, d), lambda i: (i, 0)), pl.BlockSpec((ts, d), lambda i: (i, 0))],
        out_specs=[pl.BlockSpec((1, d), lambda i: (0, 0)), pl.BlockSpec((ts, d), lambda i: (i, 0))],
        out_shape=[jax.ShapeDtypeStruct((1, d), F32), jax.ShapeDtypeStruct((s, d), F32)],
        compiler_params=_params(("arbitrary",)), name="loss_head",
    )(y, target)


def _pair_mean(val, low_half):
    s0 = jnp.sum(jnp.where(low_half, val, 0.0), axis=-1, keepdims=True)
    s1 = jnp.sum(jnp.where(low_half, 0.0, val), axis=-1, keepdims=True)
    return jnp.where(low_half, s0, s1) * (1.0 / SB_HEAD_DIM)


def _qk_norm_fwd(qkv, qgain_row, kgain_row):
    s, d3 = qkv.shape
    d = d3 // 3
    ts = _pick(s, 512)
    groups = d // LANES

    def body(q_ref, k_ref, v_ref, qg_ref, kg_ref, qn_ref, kn_ref, vb_ref):
        low_half = lax.broadcasted_iota(jnp.int32, (ts, LANES), 1) < SB_HEAD_DIM
        for src, gain, dst in ((q_ref, qg_ref, qn_ref), (k_ref, kg_ref, kn_ref)):
            for p in range(groups):
                cols = slice(p * LANES, (p + 1) * LANES)
                xp = src[:, cols]
                r = lax.rsqrt(_pair_mean(xp * xp, low_half) + NORM_EPS)
                dst[:, cols] = (xp * r * gain[:, cols]).astype(dst.dtype)
        vb_ref[...] = v_ref[...].astype(vb_ref.dtype)

    tok = lambda c: pl.BlockSpec((ts, d), lambda i: (i, c))
    row = pl.BlockSpec((1, d), lambda i: (0, 0))
    return pl.pallas_call(
        body, grid=(s // ts,),
        in_specs=[tok(0), tok(1), tok(2), row, row],
        out_specs=[tok(0), tok(0), tok(0)],
        out_shape=[jax.ShapeDtypeStruct((s, d), BF16)] * 3,
        compiler_params=_params(("parallel",)), name="qk_norm_fwd",
    )(qkv, qkv, qkv, qgain_row, kgain_row)


def _qk_norm_bwd(qkv, qgain_row, kgain_row, dqn, dkn, dv):
    s, d3 = qkv.shape
    d = d3 // 3
    ts = _pick(s, 512)
    groups = d // LANES

    def body(q_ref, k_ref, qg_ref, kg_ref, dqn_ref, dkn_ref, dv_ref, dqkv_ref, dqg_ref, dkg_ref):
        i = pl.program_id(0)
        low_half = lax.broadcasted_iota(jnp.int32, (ts, LANES), 1) < SB_HEAD_DIM
        for which, (src, gain, dsrc, dgain) in enumerate(((q_ref, qg_ref, dqn_ref, dqg_ref),
                                                          (k_ref, kg_ref, dkn_ref, dkg_ref))):
            for p in range(groups):
                cols = slice(p * LANES, (p + 1) * LANES)
                xp = src[:, cols]
                r = lax.rsqrt(_pair_mean(xp * xp, low_half) + NORM_EPS)
                xhat = xp * r
                dy = dsrc[:, cols]
                dxhat = dy * gain[:, cols]
                dx = r * (dxhat - xhat * _pair_mean(dxhat * xhat, low_half))
                dqkv_ref[:, which * d + p * LANES: which * d + (p + 1) * LANES] = dx.astype(dqkv_ref.dtype)
                part = jnp.sum(dy * xhat, axis=0, keepdims=True)

                @pl.when(i == 0)
                def _():
                    dgain[:, cols] = part

                @pl.when(i > 0)
                def _():
                    dgain[:, cols] += part
        dqkv_ref[:, 2 * d:] = dv_ref[...].astype(dqkv_ref.dtype)

    tok = lambda c: pl.BlockSpec((ts, d), lambda i: (i, c))
    row = pl.BlockSpec((1, d), lambda i: (0, 0))
    return pl.pallas_call(
        body, grid=(s // ts,),
        in_specs=[tok(0), tok(1), row, row, tok(0), tok(0), tok(0)],
        out_specs=[pl.BlockSpec((ts, d3), lambda i: (i, 0)), row, row],
        out_shape=[jax.ShapeDtypeStruct((s, d3), BF16), jax.ShapeDtypeStruct((1, d), F32),
                   jax.ShapeDtypeStruct((1, d), F32)],
        compiler_params=_params(("arbitrary",)), name="qk_norm_bwd",
    )(qkv, qkv, qgain_row, kgain_row, dqn, dkn, dv)


def _split2(x):
    hi = x.astype(BF16)
    lo = (x - hi.astype(F32)).astype(BF16)
    return hi, lo


SB_TK = 128


def _sb_consts(tk):
    j = np.arange(tk)
    ones = np.ones((tk, tk), np.float32)
    out = []
    for tri in ((j[:, None] >= j[None, :]), (j[:, None] <= j[None, :])):
        half = np.concatenate([tri.astype(np.float32), ones], axis=1)
        out.append(jnp.asarray(np.concatenate([half, half], axis=0), BF16))
    return out


def _head_stack(blk, low_half):
    f = blk.astype(F32)
    return jnp.concatenate([jnp.where(low_half, f, 0.0), jnp.where(low_half, 0.0, f)], axis=0).astype(BF16)


def _sb_tile_sums(z, valid, tri2):
    e = jnp.exp(-jnp.abs(z))
    lstay = jnp.minimum(-z, 0.0) - jnp.log(1.0 + e)
    if valid is not None:
        lstay = jnp.where(valid, lstay, 0.0)
    hi, lo = _split2(lstay)
    return e, _dot(jnp.concatenate([hi, lo], axis=1), tri2)


def _sb_weights(z, c2, valid, run):
    w = jnp.exp(z + c2[:, :SB_TK] + run)
    return w if valid is None else jnp.where(valid, w, 0.0)


EXP_IS_ZERO_BELOW = -110.0


def _max_row_norm(x):
    f = x.astype(F32)
    return jnp.sqrt(jnp.max(jnp.sum(f * f, axis=-1, keepdims=True)))


def _sb_score_bound(qs, kmax_ref):
    return _max_row_norm(qs) * jnp.max(kmax_ref[...]) * 1.01 + 1.0


def _sb_rest_is_zero(run_ref, bound):
    return jnp.max(jnp.maximum(run_ref[0], run_ref[1])) + bound < EXP_IS_ZERO_BELOW


def _sb_attn_fwd(qn, kn, vb, exchange=None):
    s, d = qn.shape
    tk = SB_TK
    tq = _pick(s, 256)
    nq, ndiag = s // tq, tq // tk
    assert tq % (2 * tk) == 0, "tiles below the diagonal are taken two at a time"
    npairs = d // LANES
    scale = 1.0 / math.sqrt(SB_HEAD_DIM)
    tri_ge2, _ = _sb_consts(tk)

    def body(q_ref, k_ref, v_ref, tri_ref, o_ref, acc_ref, run_ref, kmax_ref):
        qi = pl.program_id(1)

        @pl.when(qi == 0)
        def _():
            kmax_ref[...] = jnp.full(kmax_ref.shape, _max_row_norm(k_ref[...]), F32)

        low_half = lax.broadcasted_iota(jnp.int32, (tk, LANES), 1) < SB_HEAD_DIM
        row = lax.broadcasted_iota(jnp.int32, (tq, tk), 0)
        col = lax.broadcasted_iota(jnp.int32, (tq, tk), 1)
        qs = (q_ref[...].astype(F32) * scale).astype(BF16)
        bound = _sb_score_bound(qs, kmax_ref)
        acc_ref[...] = jnp.zeros_like(acc_ref)
        run_ref[...] = jnp.zeros_like(run_ref)
        n_full = qi * ndiag

        def sums(kb, dd):
            koff = pl.multiple_of(kb * tk, tk)
            kcat = _head_stack(k_ref[pl.ds(koff, tk), :], low_half)
            vcat = _head_stack(v_ref[pl.ds(koff, tk), :], low_half)
            z2 = _dot(qs, kcat, NT)
            valid = None if dd is None else row > col + dd * tk
            zs = [z2[:, h * tk:(h + 1) * tk] for h in range(2)]
            return zs, [_sb_tile_sums(z, valid, tri_ref[...])[1] for z in zs], valid, vcat

        def finish(zs, c2s, valid, vcat):
            ws = []
            for h in range(2):
                ws.append(_sb_weights(zs[h], c2s[h], valid, run_ref[h]).astype(BF16))
                run_ref[h] += c2s[h][:, tk:]
            acc_ref[...] += _dot(jnp.concatenate(ws, axis=1), vcat)

        for pre in [sums(n_full + dd, dd) for dd in reversed(range(ndiag))]:
            finish(*pre)

        def two_tiles(carry):
            it, _ = carry
            kb = n_full - 1 - 2 * it
            first, second = sums(kb, None), sums(kb - 1, None)
            finish(*first)
            finish(*second)
            return it + 1, _sb_rest_is_zero(run_ref, bound)

        lax.while_loop(lambda c: jnp.logical_and(c[0] < n_full // 2, jnp.logical_not(c[1])), two_tiles,
                       (jnp.int32(0), _sb_rest_is_zero(run_ref, bound)))
        o_ref[...] = acc_ref[...]

    blk = pl.BlockSpec((tq, LANES), lambda p, i: (i, p))
    full = pl.BlockSpec((s, LANES), lambda p, i: (0, p))
    (o,), moved = _call_with_exchange(
        body, exchange, grid=(npairs, nq),
        in_specs=[blk, full, full, pl.BlockSpec((2 * tk, 2 * tk), lambda p, i: (0, 0))],
        out_specs=[blk], out_shape=[jax.ShapeDtypeStruct((s, d), F32)],
        scratch_shapes=[pltpu.VMEM((tq, LANES), F32), pltpu.VMEM((2, tq, tk), F32), pltpu.VMEM((8, LANES), F32)],
        name="sb_attn_fwd", args=(qn, kn, vb, tri_ge2))
    return o, moved


def _sb_attn_bwd(qn, kn, vb, do, exchange=None):
    s, d = qn.shape
    tk = SB_TK
    tq = _pick(s, 256)
    nq, ndiag = s // tq, tq // tk
    assert tq % (2 * tk) == 0, "tiles below the diagonal are taken two at a time"
    npairs = d // LANES
    scale = 1.0 / math.sqrt(SB_HEAD_DIM)
    tri_ge2, tri_le2 = _sb_consts(tk)

    def body(q_ref, k_ref, v_ref, do_ref, tge_ref, tle_ref, dq_ref, dk_ref, dv_ref,
             g_cache, s_cache, run_ref, dq_acc, kmax_ref):
        qi = pl.program_id(1)

        @pl.when(qi == 0)
        def _():
            dk_ref[...] = jnp.zeros_like(dk_ref)
            dv_ref[...] = jnp.zeros_like(dv_ref)
            kmax_ref[...] = jnp.full(kmax_ref.shape, _max_row_norm(k_ref[...]), F32)

        low_half = lax.broadcasted_iota(jnp.int32, (tk, LANES), 1) < SB_HEAD_DIM
        row = lax.broadcasted_iota(jnp.int32, (tq, tk), 0)
        col = lax.broadcasted_iota(jnp.int32, (tq, tk), 1)
        qs = (q_ref[...].astype(F32) * scale).astype(BF16)
        bound = _sb_score_bound(qs, kmax_ref)
        dob = do_ref[...].astype(BF16)
        n_full = qi * ndiag

        def a_sums(kb, dd):
            koff = pl.multiple_of(kb * tk, tk)
            kcat = _head_stack(k_ref[pl.ds(koff, tk), :], low_half)
            vcat = _head_stack(v_ref[pl.ds(koff, tk), :], low_half)
            z2 = _dot(qs, kcat, NT)
            dw2 = _dot(dob, vcat, NT)
            valid = None if dd is None else row > col + dd * tk
            c2s = []
            for h in range(2):
                cols = slice(h * tk, (h + 1) * tk)
                z = z2[:, cols]
                e, c2 = _sb_tile_sums(z, valid, tge_ref[...])
                s_cache[kb, :, cols] = jnp.where(z >= 0, 1.0, e) / (1.0 + e)
                c2s.append(c2)
            return kb, koff, z2, dw2, c2s, valid

        def a_finish(kb, koff, z2, dw2, c2s, valid):
            ws = []
            for h in range(2):
                cols = slice(h * tk, (h + 1) * tk)
                w = _sb_weights(z2[:, cols], c2s[h], valid, run_ref[h])
                run_ref[h] += c2s[h][:, tk:]
                g_cache[kb, :, cols] = w * dw2[:, cols]
                ws.append(w.astype(BF16))
            dv2 = _dot(jnp.concatenate(ws, axis=1), dob, TN)
            dv_ref[pl.ds(koff, tk), :] += jnp.where(low_half, dv2[:tk], dv2[tk:])

        def b_sums(kb, dd):
            gs = [g_cache[kb, :, h * tk:(h + 1) * tk] for h in range(2)]
            p2s = [_dot(jnp.concatenate(_split2(g), axis=1), tle_ref[...]) for g in gs]
            return kb, gs, p2s, (None if dd is None else row > col + dd * tk)

        def b_finish(kb, gs, p2s, valid):
            koff = pl.multiple_of(kb * tk, tk)
            dzs = []
            for h in range(2):
                dz = gs[h] - s_cache[kb, :, h * tk:(h + 1) * tk] * (p2s[h][:, :tk] + run_ref[h])
                if valid is not None:
                    dz = jnp.where(valid, dz, 0.0)
                run_ref[h] += p2s[h][:, tk:]
                dzs.append(dz.astype(BF16))
            dzcat = jnp.concatenate(dzs, axis=1)
            dq_acc[...] += _dot(dzcat, _head_stack(k_ref[pl.ds(koff, tk), :], low_half))
            dk2 = _dot(dzcat, qs, TN)
            dk_ref[pl.ds(koff, tk), :] += jnp.where(low_half, dk2[:tk], dk2[tk:])

        run_ref[...] = jnp.zeros_like(run_ref)
        for pre in [a_sums(n_full + dd, dd) for dd in reversed(range(ndiag))]:
            a_finish(*pre)

        def two_a(carry):
            it, _ = carry
            kb = n_full - 1 - 2 * it
            first, second = a_sums(kb, None), a_sums(kb - 1, None)
            a_finish(*first)
            a_finish(*second)
            return it + 1, _sb_rest_is_zero(run_ref, bound)

        trips, _ = lax.while_loop(lambda c: jnp.logical_and(c[0] < n_full // 2, jnp.logical_not(c[1])), two_a,
                                  (jnp.int32(0), _sb_rest_is_zero(run_ref, bound)))

        run_ref[...] = jnp.zeros_like(run_ref)
        dq_acc[...] = jnp.zeros_like(dq_acc)
        kb_first = n_full - 2 * trips

        def two_b(it, carry):
            first, second = b_sums(kb_first + 2 * it, None), b_sums(kb_first + 2 * it + 1, None)
            b_finish(*first)
            b_finish(*second)
            return carry

        lax.fori_loop(0, trips, two_b, 0)
        for pre in [b_sums(n_full + dd, dd) for dd in range(ndiag)]:
            b_finish(*pre)
        dq_ref[...] = dq_acc[...] * scale

    blk = pl.BlockSpec((tq, LANES), lambda p, i: (i, p))
    full = pl.BlockSpec((s, LANES), lambda p, i: (0, p))
    tri = pl.BlockSpec((2 * tk, 2 * tk), lambda p, i: (0, 0))
    return _call_with_exchange(
        body, exchange, grid=(npairs, nq),
        in_specs=[blk, full, full, blk, tri, tri],
        out_specs=[blk, full, full],
        out_shape=[jax.ShapeDtypeStruct((s, d), F32)] * 3,
        scratch_shapes=[pltpu.VMEM((s // tk, tq, 2 * tk), F32), pltpu.VMEM((s // tk, tq, 2 * tk), F32),
                        pltpu.VMEM((2, tq, tk), F32), pltpu.VMEM((tq, LANES), F32), pltpu.VMEM((8, LANES), F32)],
        name="sb_attn_bwd", args=(qn, kn, vb, do, tri_ge2, tri_le2))


def _hg_consts(c):
    levels = []
    h = c // 2
    while h >= 1:
        levels.append(h)
        h //= 2
    t = np.arange(c)
    j = t[None, :]
    rows, masks = [], []
    for h in levels:
        blk = t // (2 * h)
        mid = blk * 2 * h + h - 1
        second = (t % (2 * h)) >= h
        rows.append(second[:, None] & (j > mid[:, None]) & (j <= t[:, None]))
        rows.append((~second)[:, None] & (j > t[:, None]) & (j <= mid[:, None]))
        masks.append((blk[:, None] == blk[None, :]) & second[:, None] & (~second)[None, :])
    rows.append(j <= t[:, None])
    rows.append(j > t[:, None])
    masks.append(t[:, None] == t[None, :])
    m_all = np.concatenate(rows, axis=0).astype(np.float32)
    mask_all = np.stack(masks, axis=0).astype(np.float32)
    suffix = (t[None, :] >= t[:, None]).astype(np.float32)
    return len(levels), jnp.asarray(m_all, BF16), jnp.asarray(mask_all, F32), jnp.asarray(suffix, BF16)


def _split3(x):
    hi = x.astype(BF16)
    r1 = x - hi.astype(F32)
    mid = r1.astype(BF16)
    lo = (r1 - mid.astype(F32)).astype(BF16)
    return jnp.concatenate([hi, mid, lo], axis=1)


def _join3(e):
    n = e.shape[1] // 3
    return e[:, :n] + e[:, n:2 * n] + e[:, 2 * n:]


def _hg_gates(qr, fr, lb):
    sq = _sigmoid(qr)
    sf = _sigmoid(fr)
    forget = lb + (1.0 - lb) * sf
    return qr * sq, sq, sf, forget, jnp.log(forget), 1.0 - forget


def _hg_scores(q, k, expo, masks, nlev, c):
    qb, kb = q.astype(BF16), k.astype(BF16)
    a = masks[nlev] * _dot(qb, kb, NT)
    scaled = []
    for li in range(nlev):
        fq = jnp.exp(expo[(2 * li) * c:(2 * li + 1) * c])
        fk = jnp.exp(expo[(2 * li + 1) * c:(2 * li + 2) * c])
        qs, ks = (q * fq).astype(BF16), (k * fk).astype(BF16)
        a = a + masks[li] * _dot(qs, ks, NT)
        scaled.append((qs, ks, fq, fk))
    return a, scaled, qb, kb


def _hg_heads_per_step(nh):
    return 2 if nh % 2 == 0 else 1


def _hg_fwd(proj, lb_row, gain_row, exchange=None):
    s, d4 = proj.shape
    d = d4 // 4
    nh = d // HG_HEAD_DIM
    c = min(HG_CHUNK, s)
    tb = _pick(s, 512)
    ncb = tb // c
    nlev, m_all, mask_all, _ = _hg_consts(c)
    nrow = m_all.shape[0]

    hp = _hg_heads_per_step(nh)
    wide = hp * HG_HEAD_DIM

    def body(q_ref, f_ref, i_ref, g_ref, lb_ref, gain_ref, mall_ref, mask_ref, y_ref, o_ref, st_out_ref, st_ref):
        b = pl.program_id(1)

        @pl.when(b == 0)
        def _():
            st_ref[...] = jnp.zeros_like(st_ref)

        gain = gain_ref[...]

        def chunk(ci, carry):
            rows = pl.ds(pl.multiple_of(ci * c, c), c)
            for hh in range(hp):
                cols = slice(hh * HG_HEAD_DIM, (hh + 1) * HG_HEAD_DIM)
                q, _, _, _, lf, k = _hg_gates(q_ref[rows, cols], f_ref[rows, cols], lb_ref[:, cols])
                v = i_ref[rows, cols].astype(BF16)
                expo = _join3(_dot(mall_ref[...], _split3(lf)))
                st = st_ref[hh]
                st_out_ref[ci, hh] = st
                a, _, _, _ = _hg_scores(q, k, expo, mask_ref[...], nlev, c)
                b_cum = expo[2 * nlev * c:(2 * nlev + 1) * c]
                e_tail = expo[(2 * nlev + 1) * c:(2 * nlev + 2) * c]
                q_in = (q * jnp.exp(b_cum)).astype(BF16)
                o = _dot(q_in, st.astype(BF16), NT) + _dot(a.astype(BF16), v)
                k_dec = (k * jnp.exp(e_tail)).astype(BF16)
                st_ref[hh] = st * jnp.exp(b_cum[c - 1:c, :]) + _dot(v, k_dec, TN)
                o_ref[rows, cols] = o
                r = lax.rsqrt(jnp.mean(o * o, axis=-1, keepdims=True) + NORM_EPS)
                y_ref[rows, cols] = (o * r * gain * _sigmoid(g_ref[rows, cols])).astype(y_ref.dtype)
            return carry

        lax.fori_loop(0, ncb, chunk, 0)

    part = lambda k: pl.BlockSpec((tb, wide), lambda h, b: (b, k * (nh // hp) + h))
    head_row = pl.BlockSpec((1, wide), lambda h, b: (0, h))
    tok = pl.BlockSpec((tb, wide), lambda h, b: (b, h))
    return _call_with_exchange(
        body, exchange, grid=(nh // hp, s // tb),
        in_specs=[part(0), part(1), part(2), part(3), head_row,
                  pl.BlockSpec((1, HG_HEAD_DIM), lambda h, b: (0, 0)),
                  pl.BlockSpec((nrow, c), lambda h, b: (0, 0)),
                  pl.BlockSpec((nlev + 1, c, c), lambda h, b: (0, 0, 0))],
        out_specs=[tok, tok, pl.BlockSpec((ncb, hp, HG_HEAD_DIM, HG_HEAD_DIM), lambda h, b: (b, h, 0, 0))],
        out_shape=[jax.ShapeDtypeStruct((s, d), BF16), jax.ShapeDtypeStruct((s, d), F32),
                   jax.ShapeDtypeStruct((s // c, nh, HG_HEAD_DIM, HG_HEAD_DIM), F32)],
        scratch_shapes=[pltpu.VMEM((hp, HG_HEAD_DIM, HG_HEAD_DIM), F32)],
        name="hg_fwd", args=(proj, proj, proj, proj, lb_row, gain_row, m_all, mask_all))


def _hg_bwd(proj, lb_row, gain_row, o_saved, states, dy, exchange=None):
    s, d4 = proj.shape
    d = d4 // 4
    nh = d // HG_HEAD_DIM
    c = min(HG_CHUNK, s)
    tb = _pick(s, 512)
    ncb = tb // c
    nb = s // tb
    nlev, m_all, mask_all, suffix = _hg_consts(c)
    nrow = m_all.shape[0]
    hp = _hg_heads_per_step(nh)
    wide = hp * HG_HEAD_DIM

    def body(q_ref, f_ref, i_ref, g_ref, lb_ref, gain_ref, o_ref, st_in_ref, dy_ref, mall_ref, mask_ref, suf_ref,
             dq_ref, df_ref, di_ref, dg_ref, dlb_ref, dgain_ref, dst_ref, run_ref):
        b = pl.program_id(1)

        @pl.when(b == 0)
        def _():
            dst_ref[...] = jnp.zeros_like(dst_ref)
            run_ref[...] = jnp.zeros_like(run_ref)
            dlb_ref[...] = jnp.zeros_like(dlb_ref)
            dgain_ref[...] = jnp.zeros_like(dgain_ref)

        gain = gain_ref[...]

        def head_chunk(ci, rows, hh, cols):
            lb = lb_ref[:, cols]
            qr, fr = q_ref[rows, cols], f_ref[rows, cols]
            q, sq, sf, forget, lf, k = _hg_gates(qr, fr, lb)
            v = i_ref[rows, cols].astype(BF16)
            expo = _join3(_dot(mall_ref[...], _split3(lf)))
            masks = mask_ref[...]
            o = o_ref[rows, cols]
            dyv = dy_ref[rows, cols]
            sg = _sigmoid(g_ref[rows, cols])
            r = lax.rsqrt(jnp.mean(o * o, axis=-1, keepdims=True) + NORM_EPS)
            ohat = o * r
            dyn = dyv * sg
            dg_ref[rows, cols] = (dyv * ohat * gain * sg * (1.0 - sg)).astype(dg_ref.dtype)
            dgain_ref[:, cols] += jnp.sum(dyn * ohat, axis=0, keepdims=True)
            dohat = dyn * gain
            do = (r * (dohat - ohat * jnp.mean(dohat * ohat, axis=-1, keepdims=True))).astype(BF16)
            dst = dst_ref[hh]
            dstb = dst.astype(BF16)
            a, scaled, qb, kb = _hg_scores(q, k, expo, masks, nlev, c)
            f_cum = jnp.exp(expo[2 * nlev * c:(2 * nlev + 1) * c])
            f_tail = jnp.exp(expo[(2 * nlev + 1) * c:(2 * nlev + 2) * c])
            q_in = (q * f_cum).astype(BF16)
            k_dec = (k * f_tail).astype(BF16)
            t_in = _join3(_dot(do, _split3(st_in_ref[ci, hh])))
            t_st = _join3(_dot(v, _split3(dst)))
            da = _dot(do, v, NT)
            dam = (masks[nlev] * da).astype(BF16)
            dq = t_in * f_cum + _dot(dam, kb)
            dk = t_st * f_tail + _dot(dam, qb, TN)
            db = q_in.astype(F32) * t_in - k_dec.astype(F32) * t_st
            for li in range(nlev):
                qs, ks, fq, fk = scaled[li]
                dam = (masks[li] * da).astype(BF16)
                t_q = _dot(dam, ks)
                t_k = _dot(dam, qs, TN)
                dq = dq + t_q * fq
                dk = dk + t_k * fk
                db = db + (qs.astype(F32) * t_q - ks.astype(F32) * t_k)
            dv = _dot(a.astype(BF16), do, TN) + _dot(k_dec, dstb, NT)
            dst_ref[hh] = dst * f_cum[c - 1:c, :] + _dot(do, q_in, TN)
            dlf = _join3(_dot(suf_ref[...], _split3(db))) + run_ref[hh]
            run_ref[hh] = dlf[0:1, :]
            dforget = dlf / forget - dk
            dlb_ref[:, cols] += jnp.sum(dforget * (1.0 - sf), axis=0, keepdims=True)
            df_ref[rows, cols] = (dforget * (1.0 - lb) * sf * (1.0 - sf)).astype(df_ref.dtype)
            dq_ref[rows, cols] = (dq * sq * (1.0 + qr * (1.0 - sq))).astype(dq_ref.dtype)
            di_ref[rows, cols] = dv.astype(di_ref.dtype)

        def chunk(it, carry):
            ci = ncb - 1 - it
            rows = pl.ds(pl.multiple_of(ci * c, c), c)
            for hh in range(hp):
                head_chunk(ci, rows, hh, slice(hh * HG_HEAD_DIM, (hh + 1) * HG_HEAD_DIM))
            return carry

        lax.fori_loop(0, ncb, chunk, 0)

    part = lambda k: pl.BlockSpec((tb, wide), lambda h, b: (nb - 1 - b, k * (nh // hp) + h))
    head_row = pl.BlockSpec((1, wide), lambda h, b: (0, h))
    tok = pl.BlockSpec((tb, wide), lambda h, b: (nb - 1 - b, h))
    const2 = lambda shape: pl.BlockSpec(shape, lambda h, b: (0, 0))
    return _call_with_exchange(
        body, exchange, grid=(nh // hp, nb),
        in_specs=[part(0), part(1), part(2), part(3), head_row, const2((1, HG_HEAD_DIM)), tok,
                  pl.BlockSpec((ncb, hp, HG_HEAD_DIM, HG_HEAD_DIM), lambda h, b: (nb - 1 - b, h, 0, 0)),
                  tok, const2((nrow, c)), pl.BlockSpec((nlev + 1, c, c), lambda h, b: (0, 0, 0)), const2((c, c))],
        out_specs=[tok, tok, tok, tok, head_row, head_row],
        out_shape=[jax.ShapeDtypeStruct((s, d), BF16)] * 4 + [jax.ShapeDtypeStruct((1, d), F32)] * 2,
        scratch_shapes=[pltpu.VMEM((hp, HG_HEAD_DIM, HG_HEAD_DIM), F32), pltpu.VMEM((hp, 1, HG_HEAD_DIM), F32)],
        name="hg_bwd", args=(proj, proj, proj, proj, lb_row, gain_row, o_saved, states, dy, m_all, mask_all, suffix))


def _lb_fwd(logits):
    n, d = logits.shape

    def body(l_ref, lb_ref, p_ref):
        rows = [l_ref[i:i + 1, :] for i in range(n)]
        m = functools.reduce(jnp.maximum, rows)
        es = [jnp.exp(r - m) for r in rows]
        tot = functools.reduce(lambda a, b: a + b, es)
        ps = [e / tot for e in es]
        run = jnp.zeros_like(ps[0])
        for i in range(n):
            run = run + ps[i]
            lb_ref[i:i + 1, :] = run - ps[0]
            p_ref[i:i + 1, :] = ps[i]

    return pl.pallas_call(
        body, out_shape=[jax.ShapeDtypeStruct((n, d), F32)] * 2, name="lb_fwd",
    )(logits)


def _lb_bwd(p, dlb):
    n, d = p.shape

    def body(p_ref, dlb_ref, dl_ref):
        ps = [p_ref[i:i + 1, :] for i in range(n)]
        ds = [dlb_ref[i:i + 1, :] for i in range(n)]
        total = functools.reduce(lambda a, b: a + b, ds)
        dps = []
        for i in range(n):
            dp = functools.reduce(lambda a, b: a + b, ds[i:])
            dps.append(dp - total if i == 0 else dp)
        inner = functools.reduce(lambda a, b: a + b, [pi * di for pi, di in zip(ps, dps)])
        for i in range(n):
            dl_ref[i:i + 1, :] = ps[i] * (dps[i] - inner)

    return pl.pallas_call(body, out_shape=jax.ShapeDtypeStruct((n, d), F32), name="lb_bwd")(p, dlb)


def _as2d(a):
    return a.reshape(-1, a.shape[-1])


def _adamw(w, m, v, grads):
    shape = w.shape
    w2, m2, v2 = _as2d(w), _as2d(m), _as2d(v)
    g2 = [_as2d(g) for g in grads]
    rows, cols = w2.shape
    tr = _pick(rows, 512)
    ng = len(g2)
    bc1 = 1.0 - ADAM_B1 ** ADAM_STEP
    bc2 = 1.0 - ADAM_B2 ** ADAM_STEP

    def body(w_ref, m_ref, v_ref, *rest):
        g = rest[0][...]
        for extra in rest[1:ng]:
            g = g + extra[...]
        g_out, d_out, m_out, v_out = rest[ng:]
        mn = ADAM_B1 * m_ref[...] + (1.0 - ADAM_B1) * g
        vn = ADAM_B2 * v_ref[...] + (1.0 - ADAM_B2) * (g * g)
        m_hat = mn / bc1
        v_hat = vn / bc2
        g_out[...] = g
        d_out[...] = -ADAM_LR * (m_hat / (jnp.sqrt(v_hat) + ADAM_EPS) + ADAM_WD * w_ref[...])
        m_out[...] = mn
        v_out[...] = vn

    spec = pl.BlockSpec((tr, cols), lambda i: (i, 0))
    outs = pl.pallas_call(
        body, grid=(rows // tr,), in_specs=[spec] * (3 + ng), out_specs=[spec] * 4,
        out_shape=[jax.ShapeDtypeStruct((rows, cols), F32)] * 4,
        compiler_params=_params(("parallel",)), name="adamw",
    )(w2, m2, v2, *g2)
    return tuple(o.reshape(shape) for o in outs)


def _sum_slots(parts, recv, chip):
    _, rows, cols = parts.shape
    tr = _pick(rows, 512)

    def body(chip_ref, own_ref, r0_ref, r1_ref, r2_ref, o_ref):
        f = lambda r: r[...].astype(F32)
        o_ref[...] = ((f(own_ref) + f(r0_ref)) + f(r1_ref)) + f(r2_ref)

    grid_spec = pltpu.PrefetchScalarGridSpec(
        num_scalar_prefetch=1, grid=(rows // tr,),
        in_specs=[pl.BlockSpec((None, tr, cols), lambda i, chip_ref: (chip_ref[0], i, 0))]
        + [pl.BlockSpec((None, tr, cols), functools.partial(lambda i, chip_ref, k: (k, i, 0), k=k)) for k in range(3)],
        out_specs=pl.BlockSpec((tr, cols), lambda i, chip_ref: (i, 0)))
    return pl.pallas_call(
        body, grid_spec=grid_spec, out_shape=jax.ShapeDtypeStruct((rows, cols), F32),
        compiler_params=_params(("parallel",)), name="sum_slots",
    )(chip, parts, recv, recv, recv)


def _pack_rows(pieces):
    cols = pieces[0].shape[1]
    used = sum(p.shape[0] for p in pieces)
    rows = -(-used // 8) * 8

    def body(*refs):
        out_ref = refs[-1]
        at = 0
        for ref in refs[:-1]:
            out_ref[at:at + ref.shape[0], :] = ref[...]
            at += ref.shape[0]
        if at < rows:
            out_ref[at:rows, :] = jnp.zeros((rows - at, cols), F32)

    return pl.pallas_call(body, out_shape=jax.ShapeDtypeStruct((rows, cols), F32), name="pack_rows")(*pieces)


def _sum_devices(gathered):
    n, rows, cols = gathered.shape

    def body(g_ref, o_ref):
        acc = g_ref[0]
        for i in range(1, n):
            acc = acc + g_ref[i]
        o_ref[...] = acc

    return pl.pallas_call(body, out_shape=jax.ShapeDtypeStruct((rows, cols), F32), name="sum_devices")(gathered)


def _coords():
    return lax.axis_index("x"), lax.axis_index("y"), lax.axis_index("c")


def _chip_peers(x, y, c):
    out = []
    for fx, fy in ((0, 1), (1, 0), (1, 1)):
        px = 1 - x if fx else x
        py = 1 - y if fy else y
        out.append(((px, py, c), 2 * px + py))
    return out


class _ChipExchange:
    def __init__(self, kind, arrays):
        self.kind, self.arrays, self.n = kind, list(arrays), len(arrays)
        lead = lambda a: (N_CHIPS,) + a.shape if kind == "gather" else (3,) + a.shape[1:]
        self.out_shape = [jax.ShapeDtypeStruct(lead(a), a.dtype) for a in self.arrays]
        self.scratch = [pltpu.SemaphoreType.DMA((3 * self.n,)), pltpu.SemaphoreType.DMA((3 * self.n,)),
                        pltpu.SemaphoreType.DMA((self.n,))]

    def copies(self, ins, outs, send_sems, recv_sems, local_sems):
        x, y, c = _coords()
        me = 2 * x + y
        starts, waits = [], []
        for t in range(self.n):
            if self.kind == "gather":
                own = pltpu.make_async_copy(ins[t], outs[t].at[me], local_sems.at[t])
                starts.append(own.start)
                waits.append(own.wait)
            for k, (peer, peer_chip) in enumerate(_chip_peers(x, y, c)):
                sems = dict(send_sem=send_sems.at[3 * t + k], recv_sem=recv_sems.at[3 * t + k],
                            device_id=peer, device_id_type=MESH)
                if self.kind == "gather":
                    send = pltpu.make_async_remote_copy(src_ref=ins[t], dst_ref=outs[t].at[me], **sems)
                    recv = pltpu.make_async_remote_copy(src_ref=ins[t], dst_ref=outs[t].at[peer_chip], **sems)
                else:
                    send = pltpu.make_async_remote_copy(src_ref=ins[t].at[peer_chip], dst_ref=outs[t].at[k], **sems)
                    recv = send
                starts.append(send.start)
                waits += [send.wait_send, recv.wait_recv]
        return starts, waits

    def run(self, name):
        n = self.n

        def body(*refs):
            starts, waits = self.copies(refs[:n], refs[n:2 * n], *refs[2 * n:])
            for f in starts + waits:
                f()

        return pl.pallas_call(body, in_specs=[HBM_SPEC] * n, out_specs=[HBM_SPEC] * n, out_shape=self.out_shape,
                              scratch_shapes=self.scratch, name=name)(*self.arrays)


def _call_with_exchange(body, exchange, *, grid, in_specs, out_specs, out_shape, scratch_shapes, name, args):
    if exchange is None:
        outs = pl.pallas_call(body, grid=grid, in_specs=in_specs, out_specs=out_specs, out_shape=out_shape,
                              scratch_shapes=scratch_shapes,
                              compiler_params=_params(("parallel",) + ("arbitrary",) * (len(grid) - 1)),
                              name=name)(*args)
        return outs, []
    n_in, n_out, n_scr, n = len(in_specs), len(out_specs), len(scratch_shapes), exchange.n

    def wrapped(*refs):
        ins, ex_in = refs[:n_in], refs[n_in:n_in + n]
        outs = refs[n_in + n:n_in + n + n_out]
        ex_out = refs[n_in + n + n_out:n_in + 2 * n + n_out]
        scr = refs[n_in + 2 * n + n_out:n_in + 2 * n + n_out + n_scr]
        sems = refs[n_in + 2 * n + n_out + n_scr:]
        ids = [pl.program_id(a) for a in range(len(grid))]
        first = functools.reduce(jnp.logical_and, [i == 0 for i in ids])
        last = functools.reduce(jnp.logical_and, [i == g - 1 for i, g in zip(ids, grid)])

        @pl.when(first)
        def _():
            for f in exchange.copies(ex_in, ex_out, *sems)[0]:
                f()

        body(*ins, *outs, *scr)

        @pl.when(last)
        def _():
            for f in exchange.copies(ex_in, ex_out, *sems)[1]:
                f()

    res = pl.pallas_call(
        wrapped, grid=grid, in_specs=list(in_specs) + [HBM_SPEC] * n, out_specs=list(out_specs) + [HBM_SPEC] * n,
        out_shape=list(out_shape) + exchange.out_shape, scratch_shapes=list(scratch_shapes) + exchange.scratch,
        compiler_params=_params(("arbitrary",) * len(grid)), name=name + "_" + exchange.kind,
    )(*args, *exchange.arrays)
    return res[:n_out], res[n_out:]


def _swap_cores(arrs):
    n = len(arrs)

    def body(*refs):
        ins, outs = refs[:n], refs[n:2 * n]
        send_sems, recv_sems = refs[2 * n:]
        x, y, c = _coords()
        copies = []
        for t in range(n):
            cp = pltpu.make_async_remote_copy(
                src_ref=ins[t], dst_ref=outs[t], send_sem=send_sems.at[t], recv_sem=recv_sems.at[t],
                device_id=(x, y, 1 - c), device_id_type=MESH)
            cp.start()
            copies.append(cp)
        for cp in copies:
            cp.wait()

    return pl.pallas_call(
        body, in_specs=[HBM_SPEC] * n, out_specs=[HBM_SPEC] * n,
        out_shape=[jax.ShapeDtypeStruct(a.shape, a.dtype) for a in arrs],
        scratch_shapes=[pltpu.SemaphoreType.DMA((n,)), pltpu.SemaphoreType.DMA((n,))],
        name="swap_cores",
    )(*arrs)


def _gather_devices(a):
    def body(in_ref, out_ref, send_sems, recv_sems, local_sem):
        x, y, c = _coords()
        me = 4 * x + 2 * y + c
        own = pltpu.make_async_copy(in_ref, out_ref.at[me], local_sem)
        own.start()
        waits = [own.wait]
        for k in range(1, N_DEVICES):
            px = 1 - x if k & 4 else x
            py = 1 - y if k & 2 else y
            pc = 1 - c if k & 1 else c
            peer = (px, py, pc)
            send = pltpu.make_async_remote_copy(
                src_ref=in_ref, dst_ref=out_ref.at[me], send_sem=send_sems.at[k - 1], recv_sem=recv_sems.at[k - 1],
                device_id=peer, device_id_type=MESH)
            send.start()
            recv = pltpu.make_async_remote_copy(
                src_ref=in_ref, dst_ref=out_ref.at[4 * px + 2 * py + pc], send_sem=send_sems.at[k - 1],
                recv_sem=recv_sems.at[k - 1], device_id=peer, device_id_type=MESH)
            waits += [send.wait_send, recv.wait_recv]
        for w in waits:
            w()

    return pl.pallas_call(
        body, in_specs=[HBM_SPEC], out_specs=HBM_SPEC,
        out_shape=jax.ShapeDtypeStruct((N_DEVICES,) + a.shape, a.dtype),
        scratch_shapes=[pltpu.SemaphoreType.DMA((N_DEVICES - 1,)), pltpu.SemaphoreType.DMA((N_DEVICES - 1,)),
                        pltpu.SemaphoreType.DMA],
        name="gather_devices",
    )(a)


def _mlp_grad_epilogue(r, u):
    return r * (2.0 * jnp.maximum(u, 0.0))


def kernel(x, norm_gains, sb_w_qkv, sb_q_gain, sb_k_gain, sb_w_o, hg_w_in, hg_lb_logits, hg_norm_gain, hg_w_o, mlp_w1, mlp_w2, loss_target, m_norm_gains, m_sb_w_qkv, m_sb_q_gain, m_sb_k_gain, m_sb_w_o, m_hg_w_in, m_hg_lb_logits, m_hg_norm_gain, m_hg_w_o, m_mlp_w1, m_mlp_w2, v_norm_gains, v_sb_w_qkv, v_sb_q_gain, v_sb_k_gain, v_sb_w_o, v_hg_w_in, v_hg_lb_logits, v_hg_norm_gain, v_hg_w_o, v_mlp_w1, v_mlp_w2):
    depth = norm_gains.shape[0]
    n_sb, n_hg = sb_w_qkv.shape[0], hg_w_in.shape[0]
    xs, tgt = x[0], loss_target[0]
    s, d = xs.shape
    dq = d // N_CHIPS
    cx, cy, cc = _coords()
    chip = 2 * cx + cy
    chip_arr = jnp.reshape(chip, (1,)).astype(jnp.int32)

    def mixer_weights(layer):
        j = layer // 2
        return (sb_w_qkv[j], sb_w_o[j]) if layer % 2 == 0 else (hg_w_in[j], hg_w_o[j])

    w_in_g, ng_g, lbl_g = _ChipExchange(
        "gather", [mixer_weights(0)[0].astype(BF16), norm_gains, hg_lb_logits]).run("gather_first")
    gains = jnp.transpose(ng_g, (1, 2, 0, 3)).reshape(depth, 2, d)
    logits = jnp.transpose(lbl_g, (1, 0, 2)).reshape(n_hg, d)
    lbs, lb_p = _lb_fwd(logits)
    qg_rows = [jnp.tile(sb_q_gain[j], d // SB_HEAD_DIM)[None] for j in range(n_sb)]
    kg_rows = [jnp.tile(sb_k_gain[j], d // SB_HEAD_DIM)[None] for j in range(n_sb)]

    saved, wts = [], []
    xc = xs
    for layer in range(depth):
        j = layer // 2
        ahead = [mixer_weights(layer)[1], mlp_w1[layer], mlp_w2[layer]]
        if layer + 1 < depth:
            ahead.append(mixer_weights(layer + 1)[0])
        gather = _ChipExchange("gather", [a.astype(BF16) for a in ahead])
        h1 = _rmsnorm_fwd(xc, gains[layer, 0][None])
        if layer % 2 == 0:
            qkv = _mm_fwd_cols(h1, w_in_g, name="sb_qkv")
            qn, kn, vb = _qk_norm_fwd(qkv, qg_rows[j], kg_rows[j])
            o, moved = _sb_attn_fwd(qn, kn, vb, gather)
            x_mid = _mm_fwd_rows(o, moved[0], residual=xc, name="sb_out")
            mix = (qkv, qn, kn, vb, o)
        else:
            proj = _mm_fwd_cols(h1, w_in_g, name="hg_in")
            (y, o, states), moved = _hg_fwd(proj, lbs[j][None], hg_norm_gain[j][None], gather)
            x_mid = _mm_fwd_rows(y, moved[0], residual=xc, name="hg_out")
            mix = (proj, y, o, states)
        w_out_g, w1_g, w2_g = moved[:3]
        h2 = _rmsnorm_fwd(x_mid, gains[layer, 1][None])
        u = _mm_fwd_cols(h2, w1_g, name="mlp_up")
        x_out = _mm_fwd_rows(u, w2_g, residual=x_mid, a_fn=_relu2, name="mlp_down")
        saved.append((xc, h1, mix, x_mid, h2, u))
        wts.append((w_in_g, w_out_g, w1_g, w2_g))
        w_in_g = moved[3] if layer + 1 < depth else None
        xc = x_out

    sq, dx = _loss_head(xc, tgt)
    loss = lax.psum(jnp.sum(sq) * (0.5 / d), ("x", "y", "c"))

    dgains = [[None, None] for _ in range(depth)]
    dqg, dkg = [None] * n_sb, [None] * n_sb
    dhgain, dlb = [None] * n_hg, [None] * n_hg
    grads, received, pending = {}, {}, []

    def ready(key, parts):
        grads[key] = parts
        pending.append(key)

    def scatter_of(keys):
        return _ChipExchange("scatter", [grads[k] for k in keys]) if keys else None

    def sent(keys, moved):
        for k, r in zip(keys, moved):
            received[k] = r
            pending.remove(k)

    for layer in reversed(range(depth)):
        j = layer // 2
        x_in, h1, mix, x_mid, h2, u = saved[layer]
        w_in_g, w_out_g, w1_g, w2_g = wts[layer]
        du = _mm_bwd_rows(dx, w2_g, name="mlp_down_dx", out_dtype=BF16, epi_fn=_mlp_grad_epilogue, epi_args=(u,))
        ready(("w2", layer), _mm_dw_rows(u, dx, a_fn=_relu2, name="mlp_down_dw"))
        ready(("w1", layer), _mm_dw_cols(h2, du, name="mlp_up_dw"))
        dx, dgains[layer][1] = _mm_bwd_cols_norm(du, w1_g, x_mid, gains[layer, 1][None], dx, name="mlp_up_dx")
        if layer % 2 == 0:
            qkv, qn, kn, vb, o = mix
            do = _mm_bwd_rows(dx, w_out_g, name="sb_out_dx")
            ready(("out", layer), _mm_dw_rows(o, dx, name="sb_out_dw"))
            keys = list(pending)
            (dqn, dkn, dv), moved = _sb_attn_bwd(qn, kn, vb, do, scatter_of(keys))
            sent(keys, moved)
            d_in, dqg[j], dkg[j] = _qk_norm_bwd(qkv, qg_rows[j], kg_rows[j], dqn, dkn, dv)
            ready(("in", layer), _mm_dw_cols(h1, d_in, name="sb_qkv_dw"))
            dx_name = "sb_qkv_dx"
        else:
            proj, y, o, states = mix
            dy = _mm_bwd_rows(dx, w_out_g, name="hg_out_dx")
            ready(("out", layer), _mm_dw_rows(y, dx, name="hg_out_dw"))
            keys = list(pending)
            (dq_raw, df_raw, di, dg, dlb_row, dgain_heads), moved = _hg_bwd(
                proj, lbs[j][None], hg_norm_gain[j][None], o, states, dy, scatter_of(keys))
            sent(keys, moved)
            dlb[j], dhgain[j] = dlb_row, dgain_heads
            d_in = jnp.concatenate([dq_raw, df_raw, di, dg], axis=1)
            ready(("in", layer), _mm_dw_cols(h1, d_in, name="hg_in_dw"))
            dx_name = "hg_in_dx"
        dx, dgains[layer][0] = _mm_bwd_cols_norm(d_in, w_in_g, x_in, gains[layer, 0][None], dx, name=dx_name)
    keys = list(pending)
    sent(keys, scatter_of(keys).run("scatter_last"))
    grad_x = dx[None]
    dlogits = _lb_bwd(lb_p, jnp.concatenate(dlb, axis=0))

    def chip_sum(kind, layers):
        return jnp.stack([_sum_slots(grads[kind, l], received[kind, l], chip_arr) for l in layers])

    sb_layers, hg_layers = range(0, depth, 2), range(1, depth, 2)
    big_w = [sb_w_qkv, sb_w_o, hg_w_in, hg_w_o, mlp_w1, mlp_w2]
    big_m = [m_sb_w_qkv, m_sb_w_o, m_hg_w_in, m_hg_w_o, m_mlp_w1, m_mlp_w2]
    big_v = [v_sb_w_qkv, v_sb_w_o, v_hg_w_in, v_hg_w_o, v_mlp_w1, v_mlp_w2]
    chip_sums = [chip_sum("in", sb_layers), chip_sum("out", sb_layers), chip_sum("in", hg_layers),
                 chip_sum("out", hg_layers), chip_sum("w1", range(depth)), chip_sum("w2", range(depth))]
    other_core = _swap_cores(chip_sums)
    big = [_adamw(w, m, v, [a, b]) for w, m, v, a, b in zip(big_w, big_m, big_v, chip_sums, other_core)]

    pieces = [r for pair in dgains for r in pair] + [dlogits] + dqg + dkg + dhgain
    small = _sum_devices(_gather_devices(_pack_rows(pieces)))
    my_cols = lambda a: lax.dynamic_slice_in_dim(a, chip * dq, dq, axis=1)
    fold = lambda rows, width: jnp.sum(rows.reshape(rows.shape[0], -1, width), axis=1)
    base = 2 * depth + n_hg
    g_ng = my_cols(small[0:2 * depth]).reshape(norm_gains.shape)
    g_lbl = my_cols(small[2 * depth:base])
    g_qg = fold(small[base:base + n_sb], SB_HEAD_DIM)
    g_kg = fold(small[base + n_sb:base + 2 * n_sb], SB_HEAD_DIM)
    g_hgn = fold(small[base + 2 * n_sb:base + 2 * n_sb + n_hg], HG_HEAD_DIM)
    r_ng = _adamw(norm_gains, m_norm_gains, v_norm_gains, [g_ng])
    r_qg = _adamw(sb_q_gain, m_sb_q_gain, v_sb_q_gain, [g_qg])
    r_kg = _adamw(sb_k_gain, m_sb_k_gain, v_sb_k_gain, [g_kg])
    r_lbl = _adamw(hg_lb_logits, m_hg_lb_logits, v_hg_lb_logits, [g_lbl])
    r_hgn = _adamw(hg_norm_gain, m_hg_norm_gain, v_hg_norm_gain, [g_hgn])

    per_weight = [r_ng, big[0], r_qg, r_kg, big[1], big[2], r_lbl, r_hgn, big[3], big[4], big[5]]
    outs = [loss, grad_x]
    for field in range(4):
        outs += [r[field] for r in per_weight]
    return tuple(outs)
```

```python
import functools
import math

import numpy as np
import jax
import jax.numpy as jnp
from jax import lax
from jax.experimental import pallas as pl
from jax.experimental.pallas import tpu as pltpu

F32 = jnp.float32
BF16 = jnp.bfloat16
GRAD_SLOT_DTYPE = jnp.bfloat16

NORM_EPS = 1e-6
SB_HEAD_DIM = 64
HG_HEAD_DIM = 128
HG_CHUNK = 128
LANES = 128
VMEM_LIMIT_BYTES = 56 * 2 ** 20
N_CHIPS = 4
N_DEVICES = 8

ADAM_LR = 0.001
ADAM_B1 = 0.9
ADAM_B2 = 0.999
ADAM_EPS = 1e-08
ADAM_WD = 0.01
ADAM_STEP = 10

MESH = pl.DeviceIdType.MESH
HBM_SPEC = pl.BlockSpec(memory_space=pltpu.HBM)

NN = (((1,), (0,)), ((), ()))
NT = (((1,), (1,)), ((), ()))
TN = (((0,), (0,)), ((), ()))


def _params(sem=None):
    return pltpu.CompilerParams(dimension_semantics=sem, vmem_limit_bytes=VMEM_LIMIT_BYTES)


def _pick(dim, pref):
    for t in (1024, 768, 512, 384, 256, 128, 64, 32, 16, 8):
        if t <= pref and dim % t == 0:
            return t
    return dim


def _dot(a, b, dims=NN):
    return lax.dot_general(a, b, dims, preferred_element_type=F32)


def _sigmoid(x):
    e = jnp.exp(-jnp.abs(x))
    return jnp.where(x >= 0, 1.0, e) / (1.0 + e)


def _matmul(a, b, *, mode, grid, a_block, a_map, b_block, b_map, o_block, o_map, out_shape, out_dtype, name,
            a_fn=None, epi_fn=None, epi_args=(), epi_row_args=(), col_sums=False, exchange=None):
    nk = grid[2]
    dims = {"nn": NN, "nt": NT, "tn": TN}[mode]
    n_epi = len(epi_args) + len(epi_row_args)
    n_out = 2 if col_sums else 1
    tn = o_block[-1]
    assert not col_sums or grid[1] == 1, "the column sums stay resident only with one tile along N"

    def body(a_ref, b_ref, *rest):
        epi_refs = rest[:n_epi]
        o_ref = rest[n_epi]
        kk = pl.program_id(2)

        def emit(r):
            if epi_fn is not None:
                r = epi_fn(r, *[e[...] for e in epi_refs])
            if col_sums:
                r, row = r
                sums_ref = rest[n_epi + 1]
                first = pl.program_id(0) == 0

                @pl.when(first)
                def _():
                    sums_ref[...] = row

                @pl.when(jnp.logical_not(first))
                def _():
                    sums_ref[...] += row
            o_ref[...] = r.astype(o_ref.dtype)

        av = a_ref[...]
        if a_fn is not None:
            av = a_fn(av)
        part = _dot(av.astype(BF16), b_ref[...].astype(BF16), dims)
        if nk == 1:
            emit(part)
            return
        acc_ref = rest[n_epi + n_out]

        @pl.when(kk == 0)
        def _():
            acc_ref[...] = part

        @pl.when(kk > 0)
        def _():
            acc_ref[...] += part

        @pl.when(kk == nk - 1)
        def _():
            emit(acc_ref[...])

    acc_shape = tuple(d for d in o_block if d is not None)
    row_spec = pl.BlockSpec((1, tn), lambda i, j, kk: (0, j))
    in_specs = [pl.BlockSpec(a_block, a_map), pl.BlockSpec(b_block, b_map)]
    in_specs += [pl.BlockSpec(o_block, o_map) for _ in epi_args] + [row_spec for _ in epi_row_args]
    out_specs, out_shapes = [pl.BlockSpec(o_block, o_map)], [jax.ShapeDtypeStruct(out_shape, out_dtype)]
    if col_sums:
        out_specs, out_shapes = out_specs + [row_spec], out_shapes + [jax.ShapeDtypeStruct((1, out_shape[-1]), F32)]
    outs, moved = _call_with_exchange(
        body, exchange, grid=grid, in_specs=in_specs, out_specs=out_specs, out_shape=out_shapes,
        scratch_shapes=[pltpu.VMEM(acc_shape, F32)] if nk > 1 else [], name=name,
        args=(a, b, *epi_args, *epi_row_args), sequential=col_sums)
    result = tuple(outs) if col_sums else outs[0]
    return result if exchange is None else (result, moved)


def _relu2(u):
    r = jnp.maximum(u, 0.0)
    return r * r


def _add(r, res):
    return r + res


def _mm_fwd_cols(a, wg, *, name):
    s, k = a.shape
    ncs = wg.shape[2]
    tm, tk, tn = _pick(s, 1024), _pick(k, 1024), _pick(ncs, 1024)
    npb = ncs // tn
    return _matmul(a, wg, mode="nn", grid=(s // tm, N_CHIPS * npb, k // tk),
                   a_block=(tm, tk), a_map=lambda i, j, kk: (i, kk),
                   b_block=(None, tk, tn), b_map=lambda i, j, kk: (j // npb, kk, j % npb),
                   o_block=(tm, tn), o_map=lambda i, j, kk: (i, j),
                   out_shape=(s, N_CHIPS * ncs), out_dtype=F32, name=name)


def _rows_joined(wg):
    assert wg.shape[1] % 16 == 0, "joining the leading axes must not cross a tile of 16 rows"
    return wg.reshape(wg.shape[0] * wg.shape[1], wg.shape[2])


def _mm_fwd_rows(a, wg, *, residual, name, a_fn=None):
    s = a.shape[0]
    w = _rows_joined(wg)
    k, n = w.shape
    tm, tk, tn = _pick(s, 1024), _pick(k, 1024), _pick(n, 1024)
    return _matmul(a, w, mode="nn", grid=(s // tm, n // tn, k // tk),
                   a_block=(tm, tk), a_map=lambda i, j, kk: (i, kk),
                   b_block=(tk, tn), b_map=lambda i, j, kk: (kk, j),
                   o_block=(tm, tn), o_map=lambda i, j, kk: (i, j),
                   out_shape=(s, n), out_dtype=F32, name=name, a_fn=a_fn, epi_fn=_add, epi_args=(residual,))


def _rmsnorm_grad(dh, x, dx_res, gain):
    r = lax.rsqrt(jnp.mean(x * x, axis=-1, keepdims=True) + NORM_EPS)
    xhat = x * r
    dxhat = dh * gain
    dx = r * (dxhat - xhat * jnp.mean(dxhat * xhat, axis=-1, keepdims=True))
    return dx_res + dx, jnp.sum(dh * xhat, axis=0, keepdims=True)


def _mm_bwd_cols_norm(dy, wg, x, gain_row, dx_res, *, name, exchange=None):
    s = dy.shape[0]
    kw, ncs = wg.shape[1], wg.shape[2]
    tm, tk = _pick(s, 512), _pick(ncs, 1024)
    kpb = ncs // tk
    return _matmul(dy, wg, mode="nt", grid=(s // tm, 1, N_CHIPS * kpb),
                   a_block=(tm, tk), a_map=lambda i, j, kk: (i, kk),
                   b_block=(None, kw, tk), b_map=lambda i, j, kk: (kk // kpb, j, kk % kpb),
                   o_block=(tm, kw), o_map=lambda i, j, kk: (i, j),
                   out_shape=(s, kw), out_dtype=F32, name=name, exchange=exchange,
                   epi_fn=_rmsnorm_grad, epi_args=(x, dx_res), epi_row_args=(gain_row,), col_sums=True)


def _mm_bwd_rows(dy, wg, *, name, out_dtype=F32, epi_fn=None, epi_args=()):
    s, n = dy.shape
    w = _rows_joined(wg)
    rows = w.shape[0]
    tm, tn, tk = _pick(s, 1024), _pick(rows, 1024), _pick(n, 1024)
    return _matmul(dy, w, mode="nt", grid=(s // tm, rows // tn, n // tk),
                   a_block=(tm, tk), a_map=lambda i, j, kk: (i, kk),
                   b_block=(tn, tk), b_map=lambda i, j, kk: (j, kk),
                   o_block=(tm, tn), o_map=lambda i, j, kk: (i, j),
                   out_shape=(s, rows), out_dtype=out_dtype, name=name, epi_fn=epi_fn, epi_args=epi_args)


def _mm_dw_cols(xa, dy, *, name):
    s, kx = xa.shape
    ncs = dy.shape[1] // N_CHIPS
    tm, tn, tk = _pick(kx, 1024), _pick(ncs, 1024), _pick(s, 1024)
    npb = ncs // tn
    return _matmul(xa, dy, mode="tn", grid=(kx // tm, N_CHIPS * npb, s // tk),
                   a_block=(tk, tm), a_map=lambda i, j, kk: (kk, i),
                   b_block=(tk, tn), b_map=lambda i, j, kk: (kk, j),
                   o_block=(None, tm, tn), o_map=lambda i, j, kk: (j // npb, i, j % npb),
                   out_shape=(N_CHIPS, kx, ncs), out_dtype=GRAD_SLOT_DTYPE, name=name)


def _mm_dw_rows(xa, dy, *, name, a_fn=None):
    s, n = dy.shape
    rows = xa.shape[1]
    assert (rows // N_CHIPS) % 16 == 0, "splitting the rows into slots must not cut a tile of 16 rows"
    tm, tn, tk = _pick(rows, 1024), _pick(n, 1024), _pick(s, 1024)
    dw = _matmul(xa, dy, mode="tn", grid=(rows // tm, n // tn, s // tk),
                 a_block=(tk, tm), a_map=lambda i, j, kk: (kk, i),
                 b_block=(tk, tn), b_map=lambda i, j, kk: (kk, j),
                 o_block=(tm, tn), o_map=lambda i, j, kk: (i, j),
                 out_shape=(rows, n), out_dtype=GRAD_SLOT_DTYPE, name=name, a_fn=a_fn)
    return dw.reshape(N_CHIPS, rows // N_CHIPS, n)


def _rmsnorm_fwd(x, gain_row):
    s, d = x.shape
    ts = _pick(s, 512)

    def body(x_ref, g_ref, h_ref):
        xv = x_ref[...]
        r = lax.rsqrt(jnp.mean(xv * xv, axis=-1, keepdims=True) + NORM_EPS)
        h_ref[...] = (xv * r * g_ref[...]).astype(h_ref.dtype)

    return pl.pallas_call(
        body, grid=(s // ts,),
        in_specs=[pl.BlockSpec((ts, d), lambda i: (i, 0)), pl.BlockSpec((1, d), lambda i: (0, 0))],
        out_specs=pl.BlockSpec((ts, d), lambda i: (i, 0)),
        out_shape=jax.ShapeDtypeStruct((s, d), BF16),
        compiler_params=_params(("parallel",)), name="rmsnorm_fwd",
    )(x, gain_row)


def _loss_head(y, target):
    s, d = y.shape
    ts = _pick(s, 512)

    def body(y_ref, t_ref, sq_ref, dy_ref):
        i = pl.program_id(0)
        err = y_ref[...] - t_ref[...]
        dy_ref[...] = err / d
        part = jnp.sum(err * err, axis=0, keepdims=True)

        @pl.when(i == 0)
        def _():
            sq_ref[...] = part

        @pl.when(i > 0)
        def _():
            sq_ref[...] += part

    return pl.pallas_call(
        body, grid=(s // ts,),
        in_specs=[pl.BlockSpec((ts, d), lambda i: (i, 0)), pl.BlockSpec((ts, d), lambda i: (i, 0))],
        out_specs=[pl.BlockSpec((1, d), lambda i: (0, 0)), pl.BlockSpec((ts, d), lambda i: (i, 0))],
        out_shape=[jax.ShapeDtypeStruct((1, d), F32), jax.ShapeDtypeStruct((s, d), F32)],
        compiler_params=_params(("arbitrary",)), name="loss_head",
    )(y, target)


def _pair_mean(val, low_half):
    s0 = jnp.sum(jnp.where(low_half, val, 0.0), axis=-1, keepdims=True)
    s1 = jnp.sum(jnp.where(low_half, 0.0, val), axis=-1, keepdims=True)
    return jnp.where(low_half, s0, s1) * (1.0 / SB_HEAD_DIM)


def _qk_norm_fwd(qkv, qgain_row, kgain_row):
    s, d3 = qkv.shape
    d = d3 // 3
    ts = _pick(s, 512)
    groups = d // LANES

    def body(q_ref, k_ref, v_ref, qg_ref, kg_ref, qn_ref, kn_ref, vb_ref):
        low_half = lax.broadcasted_iota(jnp.int32, (ts, LANES), 1) < SB_HEAD_DIM
        for src, gain, dst in ((q_ref, qg_ref, qn_ref), (k_ref, kg_ref, kn_ref)):
            for p in range(groups):
                cols = slice(p * LANES, (p + 1) * LANES)
                xp = src[:, cols]
                r = lax.rsqrt(_pair_mean(xp * xp, low_half) + NORM_EPS)
                dst[:, cols] = (xp * r * gain[:, cols]).astype(dst.dtype)
        vb_ref[...] = v_ref[...].astype(vb_ref.dtype)

    tok = lambda c: pl.BlockSpec((ts, d), lambda i: (i, c))
    row = pl.BlockSpec((1, d), lambda i: (0, 0))
    return pl.pallas_call(
        body, grid=(s // ts,),
        in_specs=[tok(0), tok(1), tok(2), row, row],
        out_specs=[tok(0), tok(0), tok(0)],
        out_shape=[jax.ShapeDtypeStruct((s, d), BF16)] * 3,
        compiler_params=_params(("parallel",)), name="qk_norm_fwd",
    )(qkv, qkv, qkv, qgain_row, kgain_row)


def _qk_norm_bwd(qkv, qgain_row, kgain_row, dqn, dkn, dv):
    s, d3 = qkv.shape
    d = d3 // 3
    ts = _pick(s, 512)
    groups = d // LANES

    def body(q_ref, k_ref, qg_ref, kg_ref, dqn_ref, dkn_ref, dv_ref, dqkv_ref, dqg_ref, dkg_ref):
        i = pl.program_id(0)
        low_half = lax.broadcasted_iota(jnp.int32, (ts, LANES), 1) < SB_HEAD_DIM
        for which, (src, gain, dsrc, dgain) in enumerate(((q_ref, qg_ref, dqn_ref, dqg_ref),
                                                          (k_ref, kg_ref, dkn_ref, dkg_ref))):
            for p in range(groups):
                cols = slice(p * LANES, (p + 1) * LANES)
                xp = src[:, cols]
                r = lax.rsqrt(_pair_mean(xp * xp, low_half) + NORM_EPS)
                xhat = xp * r
                dy = dsrc[:, cols]
                dxhat = dy * gain[:, cols]
                dx = r * (dxhat - xhat * _pair_mean(dxhat * xhat, low_half))
                dqkv_ref[:, which * d + p * LANES: which * d + (p + 1) * LANES] = dx.astype(dqkv_ref.dtype)
                part = jnp.sum(dy * xhat, axis=0, keepdims=True)

                @pl.when(i == 0)
                def _():
                    dgain[:, cols] = part

                @pl.when(i > 0)
                def _():
                    dgain[:, cols] += part
        dqkv_ref[:, 2 * d:] = dv_ref[...].astype(dqkv_ref.dtype)

    tok = lambda c: pl.BlockSpec((ts, d), lambda i: (i, c))
    row = pl.BlockSpec((1, d), lambda i: (0, 0))
    return pl.pallas_call(
        body, grid=(s // ts,),
        in_specs=[tok(0), tok(1), row, row, tok(0), tok(0), tok(0)],
        out_specs=[pl.BlockSpec((ts, d3), lambda i: (i, 0)), row, row],
        out_shape=[jax.ShapeDtypeStruct((s, d3), BF16), jax.ShapeDtypeStruct((1, d), F32),
                   jax.ShapeDtypeStruct((1, d), F32)],
        compiler_params=_params(("arbitrary",)), name="qk_norm_bwd",
    )(qkv, qkv, qgain_row, kgain_row, dqn, dkn, dv)


def _split2(x):
    hi = x.astype(BF16)
    lo = (x - hi.astype(F32)).astype(BF16)
    return hi, lo


SB_TK = 128


def _sb_consts(tk):
    j = np.arange(tk)
    ones = np.ones((tk, tk), np.float32)
    out = []
    for tri in ((j[:, None] >= j[None, :]), (j[:, None] <= j[None, :])):
        half = np.concatenate([tri.astype(np.float32), ones], axis=1)
        out.append(jnp.asarray(np.concatenate([half, half], axis=0), BF16))
    return out


def _head_stack(blk, low_half):
    f = blk.astype(F32)
    return jnp.concatenate([jnp.where(low_half, f, 0.0), jnp.where(low_half, 0.0, f)], axis=0).astype(BF16)


def _sb_tile_sums(z, valid, tri2):
    e = jnp.exp(-jnp.abs(z))
    lstay = jnp.minimum(-z, 0.0) - jnp.log(1.0 + e)
    if valid is not None:
        lstay = jnp.where(valid, lstay, 0.0)
    hi, lo = _split2(lstay)
    return e, _dot(jnp.concatenate([hi, lo], axis=1), tri2)


def _sb_weights(z, c2, valid, run):
    w = jnp.exp(z + c2[:, :SB_TK] + run)
    return w if valid is None else jnp.where(valid, w, 0.0)


EXP_IS_ZERO_BELOW = -110.0


def _max_row_norm(x):
    f = x.astype(F32)
    return jnp.sqrt(jnp.max(jnp.sum(f * f, axis=-1, keepdims=True)))


def _sb_score_bound(qs, kmax_ref):
    return _max_row_norm(qs) * jnp.max(kmax_ref[...]) * 1.01 + 1.0


def _sb_rest_is_zero(run_ref, bound):
    return jnp.max(jnp.maximum(run_ref[0], run_ref[1])) + bound < EXP_IS_ZERO_BELOW


def _sb_attn_fwd(qn, kn, vb, exchange=None):
    s, d = qn.shape
    tk = SB_TK
    tq = _pick(s, 256)
    nq, ndiag = s // tq, tq // tk
    assert tq % (2 * tk) == 0, "tiles below the diagonal are taken two at a time"
    npairs = d // LANES
    scale = 1.0 / math.sqrt(SB_HEAD_DIM)
    tri_ge2, _ = _sb_consts(tk)

    def body(q_ref, k_ref, v_ref, tri_ref, o_ref, acc_ref, run_ref, kmax_ref):
        qi = pl.program_id(1)

        @pl.when(qi == 0)
        def _():
            kmax_ref[...] = jnp.full(kmax_ref.shape, _max_row_norm(k_ref[...]), F32)

        low_half = lax.broadcasted_iota(jnp.int32, (tk, LANES), 1) < SB_HEAD_DIM
        row = lax.broadcasted_iota(jnp.int32, (tq, tk), 0)
        col = lax.broadcasted_iota(jnp.int32, (tq, tk), 1)
        qs = (q_ref[...].astype(F32) * scale).astype(BF16)
        bound = _sb_score_bound(qs, kmax_ref)
        acc_ref[...] = jnp.zeros_like(acc_ref)
        run_ref[...] = jnp.zeros_like(run_ref)
        n_full = qi * ndiag

        def sums(kb, dd):
            koff = pl.multiple_of(kb * tk, tk)
            kcat = _head_stack(k_ref[pl.ds(koff, tk), :], low_half)
            vcat = _head_stack(v_ref[pl.ds(koff, tk), :], low_half)
            z2 = _dot(qs, kcat, NT)
            valid = None if dd is None else row > col + dd * tk
            zs = [z2[:, h * tk:(h + 1) * tk] for h in range(2)]
            return zs, [_sb_tile_sums(z, valid, tri_ref[...])[1] for z in zs], valid, vcat

        def finish(zs, c2s, valid, vcat):
            ws = []
            for h in range(2):
                ws.append(_sb_weights(zs[h], c2s[h], valid, run_ref[h]).astype(BF16))
                run_ref[h] += c2s[h][:, tk:]
            acc_ref[...] += _dot(jnp.concatenate(ws, axis=1), vcat)

        for pre in [sums(n_full + dd, dd) for dd in reversed(range(ndiag))]:
            finish(*pre)

        def two_tiles(carry):
            it, _ = carry
            kb = n_full - 1 - 2 * it
            first, second = sums(kb, None), sums(kb - 1, None)
            finish(*first)
            finish(*second)
            return it + 1, _sb_rest_is_zero(run_ref, bound)

        lax.while_loop(lambda c: jnp.logical_and(c[0] < n_full // 2, jnp.logical_not(c[1])), two_tiles,
                       (jnp.int32(0), _sb_rest_is_zero(run_ref, bound)))
        o_ref[...] = acc_ref[...]

    blk = pl.BlockSpec((tq, LANES), lambda p, i: (i, p))
    full = pl.BlockSpec((s, LANES), lambda p, i: (0, p))
    (o,), moved = _call_with_exchange(
        body, exchange, grid=(npairs, nq),
        in_specs=[blk, full, full, pl.BlockSpec((2 * tk, 2 * tk), lambda p, i: (0, 0))],
        out_specs=[blk], out_shape=[jax.ShapeDtypeStruct((s, d), F32)],
        scratch_shapes=[pltpu.VMEM((tq, LANES), F32), pltpu.VMEM((2, tq, tk), F32), pltpu.VMEM((8, LANES), F32)],
        name="sb_attn_fwd", args=(qn, kn, vb, tri_ge2))
    return o, moved


def _sb_attn_bwd(qn, kn, vb, do, exchange=None):
    s, d = qn.shape
    tk = SB_TK
    tq = _pick(s, 256)
    nq, ndiag = s // tq, tq // tk
    assert tq % (2 * tk) == 0, "tiles below the diagonal are taken two at a time"
    npairs = d // LANES
    scale = 1.0 / math.sqrt(SB_HEAD_DIM)
    tri_ge2, tri_le2 = _sb_consts(tk)

    def body(q_ref, k_ref, v_ref, do_ref, tge_ref, tle_ref, dq_ref, dk_ref, dv_ref,
             g_cache, s_cache, run_ref, dq_acc, kmax_ref):
        qi = pl.program_id(1)

        @pl.when(qi == 0)
        def _():
            dk_ref[...] = jnp.zeros_like(dk_ref)
            dv_ref[...] = jnp.zeros_like(dv_ref)
            kmax_ref[...] = jnp.full(kmax_ref.shape, _max_row_norm(k_ref[...]), F32)

        low_half = lax.broadcasted_iota(jnp.int32, (tk, LANES), 1) < SB_HEAD_DIM
        row = lax.broadcasted_iota(jnp.int32, (tq, tk), 0)
        col = lax.broadcasted_iota(jnp.int32, (tq, tk), 1)
        qs = (q_ref[...].astype(F32) * scale).astype(BF16)
        bound = _sb_score_bound(qs, kmax_ref)
        dob = do_ref[...].astype(BF16)
        n_full = qi * ndiag

        def a_sums(kb, dd):
            koff = pl.multiple_of(kb * tk, tk)
            kcat = _head_stack(k_ref[pl.ds(koff, tk), :], low_half)
            vcat = _head_stack(v_ref[pl.ds(koff, tk), :], low_half)
            z2 = _dot(qs, kcat, NT)
            dw2 = _dot(dob, vcat, NT)
            valid = None if dd is None else row > col + dd * tk
            c2s = []
            for h in range(2):
                cols = slice(h * tk, (h + 1) * tk)
                z = z2[:, cols]
                e, c2 = _sb_tile_sums(z, valid, tge_ref[...])
                s_cache[kb, :, cols] = jnp.where(z >= 0, 1.0, e) / (1.0 + e)
                c2s.append(c2)
            return kb, koff, z2, dw2, c2s, valid

        def a_finish(kb, koff, z2, dw2, c2s, valid):
            ws = []
            for h in range(2):
                cols = slice(h * tk, (h + 1) * tk)
                w = _sb_weights(z2[:, cols], c2s[h], valid, run_ref[h])
                run_ref[h] += c2s[h][:, tk:]
                g_cache[kb, :, cols] = w * dw2[:, cols]
                ws.append(w.astype(BF16))
            dv2 = _dot(jnp.concatenate(ws, axis=1), dob, TN)
            dv_ref[pl.ds(koff, tk), :] += jnp.where(low_half, dv2[:tk], dv2[tk:])

        def b_sums(kb, dd):
            gs = [g_cache[kb, :, h * tk:(h + 1) * tk] for h in range(2)]
            p2s = [_dot(jnp.concatenate(_split2(g), axis=1), tle_ref[...]) for g in gs]
            return kb, gs, p2s, (None if dd is None else row > col + dd * tk)

        def b_finish(kb, gs, p2s, valid):
            koff = pl.multiple_of(kb * tk, tk)
            dzs = []
            for h in range(2):
                dz = gs[h] - s_cache[kb, :, h * tk:(h + 1) * tk] * (p2s[h][:, :tk] + run_ref[h])
                if valid is not None:
                    dz = jnp.where(valid, dz, 0.0)
                run_ref[h] += p2s[h][:, tk:]
                dzs.append(dz.astype(BF16))
            dzcat = jnp.concatenate(dzs, axis=1)
            dq_acc[...] += _dot(dzcat, _head_stack(k_ref[pl.ds(koff, tk), :], low_half))
            dk2 = _dot(dzcat, qs, TN)
            dk_ref[pl.ds(koff, tk), :] += jnp.where(low_half, dk2[:tk], dk2[tk:])

        run_ref[...] = jnp.zeros_like(run_ref)
        for pre in [a_sums(n_full + dd, dd) for dd in reversed(range(ndiag))]:
            a_finish(*pre)

        def two_a(carry):
            it, _ = carry
            kb = n_full - 1 - 2 * it
            first, second = a_sums(kb, None), a_sums(kb - 1, None)
            a_finish(*first)
            a_finish(*second)
            return it + 1, _sb_rest_is_zero(run_ref, bound)

        trips, _ = lax.while_loop(lambda c: jnp.logical_and(c[0] < n_full // 2, jnp.logical_not(c[1])), two_a,
                                  (jnp.int32(0), _sb_rest_is_zero(run_ref, bound)))

        run_ref[...] = jnp.zeros_like(run_ref)
        dq_acc[...] = jnp.zeros_like(dq_acc)
        kb_first = n_full - 2 * trips

        def two_b(it, carry):
            first, second = b_sums(kb_first + 2 * it, None), b_sums(kb_first + 2 * it + 1, None)
            b_finish(*first)
            b_finish(*second)
            return carry

        lax.fori_loop(0, trips, two_b, 0)
        for pre in [b_sums(n_full + dd, dd) for dd in range(ndiag)]:
            b_finish(*pre)
        dq_ref[...] = dq_acc[...] * scale

    blk = pl.BlockSpec((tq, LANES), lambda p, i: (i, p))
    full = pl.BlockSpec((s, LANES), lambda p, i: (0, p))
    tri = pl.BlockSpec((2 * tk, 2 * tk), lambda p, i: (0, 0))
    return _call_with_exchange(
        body, exchange, grid=(npairs, nq),
        in_specs=[blk, full, full, blk, tri, tri],
        out_specs=[blk, full, full],
        out_shape=[jax.ShapeDtypeStruct((s, d), F32)] * 3,
        scratch_shapes=[pltpu.VMEM((s // tk, tq, 2 * tk), F32), pltpu.VMEM((s // tk, tq, 2 * tk), F32),
                        pltpu.VMEM((2, tq, tk), F32), pltpu.VMEM((tq, LANES), F32), pltpu.VMEM((8, LANES), F32)],
        name="sb_attn_bwd", args=(qn, kn, vb, do, tri_ge2, tri_le2))


def _hg_consts(c):
    levels = []
    h = c // 2
    while h >= 1:
        levels.append(h)
        h //= 2
    t = np.arange(c)
    j = t[None, :]
    rows, masks = [], []
    for h in levels:
        blk = t // (2 * h)
        mid = blk * 2 * h + h - 1
        second = (t % (2 * h)) >= h
        rows.append(second[:, None] & (j > mid[:, None]) & (j <= t[:, None]))
        rows.append((~second)[:, None] & (j > t[:, None]) & (j <= mid[:, None]))
        masks.append((blk[:, None] == blk[None, :]) & second[:, None] & (~second)[None, :])
    rows.append(j <= t[:, None])
    rows.append(j > t[:, None])
    masks.append(t[:, None] == t[None, :])
    m_all = np.concatenate(rows, axis=0).astype(np.float32)
    mask_all = np.stack(masks, axis=0).astype(np.float32)
    suffix = (t[None, :] >= t[:, None]).astype(np.float32)
    return len(levels), jnp.asarray(m_all, BF16), jnp.asarray(mask_all, F32), jnp.asarray(suffix, BF16)


def _split3(x):
    hi = x.astype(BF16)
    r1 = x - hi.astype(F32)
    mid = r1.astype(BF16)
    lo = (r1 - mid.astype(F32)).astype(BF16)
    return jnp.concatenate([hi, mid, lo], axis=1)


def _join3(e):
    n = e.shape[1] // 3
    return e[:, :n] + e[:, n:2 * n] + e[:, 2 * n:]


def _hg_gates(qr, fr, lb):
    sq = _sigmoid(qr)
    sf = _sigmoid(fr)
    forget = lb + (1.0 - lb) * sf
    return qr * sq, sq, sf, forget, jnp.log(forget), 1.0 - forget


def _hg_scores(q, k, expo, masks, nlev, c):
    qb, kb = q.astype(BF16), k.astype(BF16)
    a = masks[nlev] * _dot(qb, kb, NT)
    scaled = []
    for li in range(nlev):
        fq = jnp.exp(expo[(2 * li) * c:(2 * li + 1) * c])
        fk = jnp.exp(expo[(2 * li + 1) * c:(2 * li + 2) * c])
        qs, ks = (q * fq).astype(BF16), (k * fk).astype(BF16)
        a = a + masks[li] * _dot(qs, ks, NT)
        scaled.append((qs, ks, fq, fk))
    return a, scaled, qb, kb


def _hg_heads_per_step(nh):
    return 2 if nh % 2 == 0 else 1


def _hg_fwd(proj, lb_row, gain_row, exchange=None):
    s, d4 = proj.shape
    d = d4 // 4
    nh = d // HG_HEAD_DIM
    c = min(HG_CHUNK, s)
    tb = _pick(s, 512)
    ncb = tb // c
    nlev, m_all, mask_all, _ = _hg_consts(c)
    nrow = m_all.shape[0]

    hp = _hg_heads_per_step(nh)
    wide = hp * HG_HEAD_DIM

    def body(q_ref, f_ref, i_ref, g_ref, lb_ref, gain_ref, mall_ref, mask_ref, y_ref, o_ref, st_out_ref, st_ref):
        b = pl.program_id(1)

        @pl.when(b == 0)
        def _():
            st_ref[...] = jnp.zeros_like(st_ref)

        gain = gain_ref[...]

        def chunk(ci, carry):
            rows = pl.ds(pl.multiple_of(ci * c, c), c)
            for hh in range(hp):
                cols = slice(hh * HG_HEAD_DIM, (hh + 1) * HG_HEAD_DIM)
                q, _, _, _, lf, k = _hg_gates(q_ref[rows, cols], f_ref[rows, cols], lb_ref[:, cols])
                v = i_ref[rows, cols].astype(BF16)
                expo = _join3(_dot(mall_ref[...], _split3(lf)))
                st = st_ref[hh]
                st_out_ref[ci, hh] = st
                a, _, _, _ = _hg_scores(q, k, expo, mask_ref[...], nlev, c)
                b_cum = expo[2 * nlev * c:(2 * nlev + 1) * c]
                e_tail = expo[(2 * nlev + 1) * c:(2 * nlev + 2) * c]
                q_in = (q * jnp.exp(b_cum)).astype(BF16)
                o = _dot(q_in, st.astype(BF16), NT) + _dot(a.astype(BF16), v)
                k_dec = (k * jnp.exp(e_tail)).astype(BF16)
                st_ref[hh] = st * jnp.exp(b_cum[c - 1:c, :]) + _dot(v, k_dec, TN)
                o_ref[rows, cols] = o
                r = lax.rsqrt(jnp.mean(o * o, axis=-1, keepdims=True) + NORM_EPS)
                y_ref[rows, cols] = (o * r * gain * _sigmoid(g_ref[rows, cols])).astype(y_ref.dtype)
            return carry

        lax.fori_loop(0, ncb, chunk, 0)

    part = lambda k: pl.BlockSpec((tb, wide), lambda h, b: (b, k * (nh // hp) + h))
    head_row = pl.BlockSpec((1, wide), lambda h, b: (0, h))
    tok = pl.BlockSpec((tb, wide), lambda h, b: (b, h))
    return _call_with_exchange(
        body, exchange, grid=(nh // hp, s // tb),
        in_specs=[part(0), part(1), part(2), part(3), head_row,
                  pl.BlockSpec((1, HG_HEAD_DIM), lambda h, b: (0, 0)),
                  pl.BlockSpec((nrow, c), lambda h, b: (0, 0)),
                  pl.BlockSpec((nlev + 1, c, c), lambda h, b: (0, 0, 0))],
        out_specs=[tok, tok, pl.BlockSpec((ncb, hp, HG_HEAD_DIM, HG_HEAD_DIM), lambda h, b: (b, h, 0, 0))],
        out_shape=[jax.ShapeDtypeStruct((s, d), BF16), jax.ShapeDtypeStruct((s, d), F32),
                   jax.ShapeDtypeStruct((s // c, nh, HG_HEAD_DIM, HG_HEAD_DIM), F32)],
        scratch_shapes=[pltpu.VMEM((hp, HG_HEAD_DIM, HG_HEAD_DIM), F32)],
        name="hg_fwd", args=(proj, proj, proj, proj, lb_row, gain_row, m_all, mask_all))


def _hg_bwd(proj, lb_row, gain_row, o_saved, states, dy, exchange=None):
    s, d4 = proj.shape
    d = d4 // 4
    nh = d // HG_HEAD_DIM
    c = min(HG_CHUNK, s)
    tb = _pick(s, 512)
    ncb = tb // c
    nb = s // tb
    nlev, m_all, mask_all, suffix = _hg_consts(c)
    nrow = m_all.shape[0]
    hp = _hg_heads_per_step(nh)
    wide = hp * HG_HEAD_DIM

    def body(q_ref, f_ref, i_ref, g_ref, lb_ref, gain_ref, o_ref, st_in_ref, dy_ref, mall_ref, mask_ref, suf_ref,
             dq_ref, df_ref, di_ref, dg_ref, dlb_ref, dgain_ref, dst_ref, run_ref):
        b = pl.program_id(1)

        @pl.when(b == 0)
        def _():
            dst_ref[...] = jnp.zeros_like(dst_ref)
            run_ref[...] = jnp.zeros_like(run_ref)
            dlb_ref[...] = jnp.zeros_like(dlb_ref)
            dgain_ref[...] = jnp.zeros_like(dgain_ref)

        gain = gain_ref[...]

        def head_chunk(ci, rows, hh, cols):
            lb = lb_ref[:, cols]
            qr, fr = q_ref[rows, cols], f_ref[rows, cols]
            q, sq, sf, forget, lf, k = _hg_gates(qr, fr, lb)
            v = i_ref[rows, cols].astype(BF16)
            expo = _join3(_dot(mall_ref[...], _split3(lf)))
            masks = mask_ref[...]
            o = o_ref[rows, cols]
            dyv = dy_ref[rows, cols]
            sg = _sigmoid(g_ref[rows, cols])
            r = lax.rsqrt(jnp.mean(o * o, axis=-1, keepdims=True) + NORM_EPS)
            ohat = o * r
            dyn = dyv * sg
            dg_ref[rows, cols] = (dyv * ohat * gain * sg * (1.0 - sg)).astype(dg_ref.dtype)
            dgain_ref[:, cols] += jnp.sum(dyn * ohat, axis=0, keepdims=True)
            dohat = dyn * gain
            do = (r * (dohat - ohat * jnp.mean(dohat * ohat, axis=-1, keepdims=True))).astype(BF16)
            dst = dst_ref[hh]
            dstb = dst.astype(BF16)
            a, scaled, qb, kb = _hg_scores(q, k, expo, masks, nlev, c)
            f_cum = jnp.exp(expo[2 * nlev * c:(2 * nlev + 1) * c])
            f_tail = jnp.exp(expo[(2 * nlev + 1) * c:(2 * nlev + 2) * c])
            q_in = (q * f_cum).astype(BF16)
            k_dec = (k * f_tail).astype(BF16)
            t_in = _join3(_dot(do, _split3(st_in_ref[ci, hh])))
            t_st = _join3(_dot(v, _split3(dst)))
            da = _dot(do, v, NT)
            dam = (masks[nlev] * da).astype(BF16)
            dq = t_in * f_cum + _dot(dam, kb)
            dk = t_st * f_tail + _dot(dam, qb, TN)
            db = q_in.astype(F32) * t_in - k_dec.astype(F32) * t_st
            for li in range(nlev):
                qs, ks, fq, fk = scaled[li]
                dam = (masks[li] * da).astype(BF16)
                t_q = _dot(dam, ks)
                t_k = _dot(dam, qs, TN)
                dq = dq + t_q * fq
                dk = dk + t_k * fk
                db = db + (qs.astype(F32) * t_q - ks.astype(F32) * t_k)
            dv = _dot(a.astype(BF16), do, TN) + _dot(k_dec, dstb, NT)
            dst_ref[hh] = dst * f_cum[c - 1:c, :] + _dot(do, q_in, TN)
            dlf = _join3(_dot(suf_ref[...], _split3(db))) + run_ref[hh]
            run_ref[hh] = dlf[0:1, :]
            dforget = dlf / forget - dk
            dlb_ref[:, cols] += jnp.sum(dforget * (1.0 - sf), axis=0, keepdims=True)
            df_ref[rows, cols] = (dforget * (1.0 - lb) * sf * (1.0 - sf)).astype(df_ref.dtype)
            dq_ref[rows, cols] = (dq * sq * (1.0 + qr * (1.0 - sq))).astype(dq_ref.dtype)
            di_ref[rows, cols] = dv.astype(di_ref.dtype)

        def chunk(it, carry):
            ci = ncb - 1 - it
            rows = pl.ds(pl.multiple_of(ci * c, c), c)
            for hh in range(hp):
                head_chunk(ci, rows, hh, slice(hh * HG_HEAD_DIM, (hh + 1) * HG_HEAD_DIM))
            return carry

        lax.fori_loop(0, ncb, chunk, 0)

    part = lambda k: pl.BlockSpec((tb, wide), lambda h, b: (nb - 1 - b, k * (nh // hp) + h))
    head_row = pl.BlockSpec((1, wide), lambda h, b: (0, h))
    tok = pl.BlockSpec((tb, wide), lambda h, b: (nb - 1 - b, h))
    const2 = lambda shape: pl.BlockSpec(shape, lambda h, b: (0, 0))
    return _call_with_exchange(
        body, exchange, grid=(nh // hp, nb),
        in_specs=[part(0), part(1), part(2), part(3), head_row, const2((1, HG_HEAD_DIM)), tok,
                  pl.BlockSpec((ncb, hp, HG_HEAD_DIM, HG_HEAD_DIM), lambda h, b: (nb - 1 - b, h, 0, 0)),
                  tok, const2((nrow, c)), pl.BlockSpec((nlev + 1, c, c), lambda h, b: (0, 0, 0)), const2((c, c))],
        out_specs=[tok, tok, tok, tok, head_row, head_row],
        out_shape=[jax.ShapeDtypeStruct((s, d), BF16)] * 4 + [jax.ShapeDtypeStruct((1, d), F32)] * 2,
        scratch_shapes=[pltpu.VMEM((hp, HG_HEAD_DIM, HG_HEAD_DIM), F32), pltpu.VMEM((hp, 1, HG_HEAD_DIM), F32)],
        name="hg_bwd", args=(proj, proj, proj, proj, lb_row, gain_row, o_saved, states, dy, m_all, mask_all, suffix))


def _lb_fwd(logits):
    n, d = logits.shape

    def body(l_ref, lb_ref, p_ref):
        rows = [l_ref[i:i + 1, :] for i in range(n)]
        m = functools.reduce(jnp.maximum, rows)
        es = [jnp.exp(r - m) for r in rows]
        tot = functools.reduce(lambda a, b: a + b, es)
        ps = [e / tot for e in es]
        run = jnp.zeros_like(ps[0])
        for i in range(n):
            run = run + ps[i]
            lb_ref[i:i + 1, :] = run - ps[0]
            p_ref[i:i + 1, :] = ps[i]

    return pl.pallas_call(
        body, out_shape=[jax.ShapeDtypeStruct((n, d), F32)] * 2, name="lb_fwd",
    )(logits)


def _lb_bwd(p, dlb):
    n, d = p.shape

    def body(p_ref, dlb_ref, dl_ref):
        ps = [p_ref[i:i + 1, :] for i in range(n)]
        ds = [dlb_ref[i:i + 1, :] for i in range(n)]
        total = functools.reduce(lambda a, b: a + b, ds)
        dps = []
        for i in range(n):
            dp = functools.reduce(lambda a, b: a + b, ds[i:])
            dps.append(dp - total if i == 0 else dp)
        inner = functools.reduce(lambda a, b: a + b, [pi * di for pi, di in zip(ps, dps)])
        for i in range(n):
            dl_ref[i:i + 1, :] = ps[i] * (dps[i] - inner)

    return pl.pallas_call(body, out_shape=jax.ShapeDtypeStruct((n, d), F32), name="lb_bwd")(p, dlb)


def _as2d(a):
    return a.reshape(-1, a.shape[-1])


def _adamw(w, m, v, grads):
    shape = w.shape
    w2, m2, v2 = _as2d(w), _as2d(m), _as2d(v)
    g2 = [_as2d(g) for g in grads]
    rows, cols = w2.shape
    tr = _pick(rows, 512)
    ng = len(g2)
    bc1 = 1.0 - ADAM_B1 ** ADAM_STEP
    bc2 = 1.0 - ADAM_B2 ** ADAM_STEP

    def body(w_ref, m_ref, v_ref, *rest):
        g = rest[0][...]
        for extra in rest[1:ng]:
            g = g + extra[...]
        g_out, d_out, m_out, v_out = rest[ng:]
        mn = ADAM_B1 * m_ref[...] + (1.0 - ADAM_B1) * g
        vn = ADAM_B2 * v_ref[...] + (1.0 - ADAM_B2) * (g * g)
        m_hat = mn / bc1
        v_hat = vn / bc2
        g_out[...] = g
        d_out[...] = -ADAM_LR * (m_hat / (jnp.sqrt(v_hat) + ADAM_EPS) + ADAM_WD * w_ref[...])
        m_out[...] = mn
        v_out[...] = vn

    spec = pl.BlockSpec((tr, cols), lambda i: (i, 0))
    outs = pl.pallas_call(
        body, grid=(rows // tr,), in_specs=[spec] * (3 + ng), out_specs=[spec] * 4,
        out_shape=[jax.ShapeDtypeStruct((rows, cols), F32)] * 4,
        compiler_params=_params(("parallel",)), name="adamw",
    )(w2, m2, v2, *g2)
    return tuple(o.reshape(shape) for o in outs)


def _sum_slots(parts, recv, chip, into, index):
    _, rows, cols = parts.shape
    tr = _pick(rows, 512)

    def body(chip_ref, own_ref, r0_ref, r1_ref, r2_ref, into_ref, o_ref):
        f = lambda r: r[...].astype(F32)
        o_ref[...] = ((f(own_ref) + f(r0_ref)) + f(r1_ref)) + f(r2_ref)

    grid_spec = pltpu.PrefetchScalarGridSpec(
        num_scalar_prefetch=1, grid=(rows // tr,),
        in_specs=[pl.BlockSpec((None, tr, cols), lambda i, chip_ref: (chip_ref[0], i, 0))]
        + [pl.BlockSpec((None, tr, cols), functools.partial(lambda i, chip_ref, k: (k, i, 0), k=k)) for k in range(3)]
        + [pl.BlockSpec(memory_space=pl.ANY)],
        out_specs=pl.BlockSpec((None, tr, cols), lambda i, chip_ref: (index, i, 0)))
    return pl.pallas_call(
        body, grid_spec=grid_spec, out_shape=jax.ShapeDtypeStruct(into.shape, F32),
        input_output_aliases={5: 0}, compiler_params=_params(("parallel",)), name="sum_slots",
    )(chip, parts, recv, recv, recv, into)


def _pack_rows(pieces):
    cols = pieces[0].shape[1]
    used = sum(p.shape[0] for p in pieces)
    rows = -(-used // 8) * 8

    def body(*refs):
        out_ref = refs[-1]
        at = 0
        for ref in refs[:-1]:
            out_ref[at:at + ref.shape[0], :] = ref[...]
            at += ref.shape[0]
        if at < rows:
            out_ref[at:rows, :] = jnp.zeros((rows - at, cols), F32)

    return pl.pallas_call(body, out_shape=jax.ShapeDtypeStruct((rows, cols), F32), name="pack_rows")(*pieces)


def _sum_devices(gathered):
    n, rows, cols = gathered.shape

    def body(g_ref, o_ref):
        acc = g_ref[0]
        for i in range(1, n):
            acc = acc + g_ref[i]
        o_ref[...] = acc

    return pl.pallas_call(body, out_shape=jax.ShapeDtypeStruct((rows, cols), F32), name="sum_devices")(gathered)


def _coords():
    return lax.axis_index("x"), lax.axis_index("y"), lax.axis_index("c")


def _chip_peers(x, y, c):
    out = []
    for fx, fy in ((0, 1), (1, 0), (1, 1)):
        px = 1 - x if fx else x
        py = 1 - y if fy else y
        out.append(((px, py, c), 2 * px + py))
    return out


class _ChipExchange:
    def __init__(self, kind, arrays):
        self.kind, self.arrays, self.n = kind, list(arrays), len(arrays)
        lead = lambda a: (N_CHIPS,) + a.shape if kind == "gather" else (3,) + a.shape[1:]
        self.out_shape = [jax.ShapeDtypeStruct(lead(a), a.dtype) for a in self.arrays]
        self.scratch = [pltpu.SemaphoreType.DMA((3 * self.n,)), pltpu.SemaphoreType.DMA((3 * self.n,)),
                        pltpu.SemaphoreType.DMA((self.n,))]

    def copies(self, ins, outs, send_sems, recv_sems, local_sems):
        x, y, c = _coords()
        me = 2 * x + y
        starts, waits = [], []
        for t in range(self.n):
            if self.kind == "gather":
                own = pltpu.make_async_copy(ins[t], outs[t].at[me], local_sems.at[t])
                starts.append(own.start)
                waits.append(own.wait)
            for k, (peer, peer_chip) in enumerate(_chip_peers(x, y, c)):
                sems = dict(send_sem=send_sems.at[3 * t + k], recv_sem=recv_sems.at[3 * t + k],
                            device_id=peer, device_id_type=MESH)
                if self.kind == "gather":
                    send = pltpu.make_async_remote_copy(src_ref=ins[t], dst_ref=outs[t].at[me], **sems)
                    recv = pltpu.make_async_remote_copy(src_ref=ins[t], dst_ref=outs[t].at[peer_chip], **sems)
                else:
                    send = pltpu.make_async_remote_copy(src_ref=ins[t].at[peer_chip], dst_ref=outs[t].at[k], **sems)
                    recv = send
                starts.append(send.start)
                waits += [send.wait_send, recv.wait_recv]
        return starts, waits

    def run(self, name):
        n = self.n

        def body(*refs):
            starts, waits = self.copies(refs[:n], refs[n:2 * n], *refs[2 * n:])
            for f in starts + waits:
                f()

        return pl.pallas_call(body, in_specs=[HBM_SPEC] * n, out_specs=[HBM_SPEC] * n, out_shape=self.out_shape,
                              scratch_shapes=self.scratch, name=name)(*self.arrays)


def _call_with_exchange(body, exchange, *, grid, in_specs, out_specs, out_shape, scratch_shapes, name, args,
                        sequential=False):
    if exchange is None:
        first_axis = "arbitrary" if sequential else "parallel"
        outs = pl.pallas_call(body, grid=grid, in_specs=in_specs, out_specs=out_specs, out_shape=out_shape,
                              scratch_shapes=scratch_shapes,
                              compiler_params=_params((first_axis,) + ("arbitrary",) * (len(grid) - 1)),
                              name=name)(*args)
        return outs, []
    n_in, n_out, n_scr, n = len(in_specs), len(out_specs), len(scratch_shapes), exchange.n

    def wrapped(*refs):
        ins, ex_in = refs[:n_in], refs[n_in:n_in + n]
        outs = refs[n_in + n:n_in + n + n_out]
        ex_out = refs[n_in + n + n_out:n_in + 2 * n + n_out]
        scr = refs[n_in + 2 * n + n_out:n_in + 2 * n + n_out + n_scr]
        sems = refs[n_in + 2 * n + n_out + n_scr:]
        ids = [pl.program_id(a) for a in range(len(grid))]
        first = functools.reduce(jnp.logical_and, [i == 0 for i in ids])
        last = functools.reduce(jnp.logical_and, [i == g - 1 for i, g in zip(ids, grid)])

        @pl.when(first)
        def _():
            for f in exchange.copies(ex_in, ex_out, *sems)[0]:
                f()

        body(*ins, *outs, *scr)

        @pl.when(last)
        def _():
            for f in exchange.copies(ex_in, ex_out, *sems)[1]:
                f()

    res = pl.pallas_call(
        wrapped, grid=grid, in_specs=list(in_specs) + [HBM_SPEC] * n, out_specs=list(out_specs) + [HBM_SPEC] * n,
        out_shape=list(out_shape) + exchange.out_shape, scratch_shapes=list(scratch_shapes) + exchange.scratch,
        compiler_params=_params(("arbitrary",) * len(grid)), name=name + "_" + exchange.kind,
    )(*args, *exchange.arrays)
    return res[:n_out], res[n_out:]


def _swap_cores(arrs):
    n = len(arrs)

    def body(*refs):
        ins, outs = refs[:n], refs[n:2 * n]
        send_sems, recv_sems = refs[2 * n:]
        x, y, c = _coords()
        copies = []
        for t in range(n):
            cp = pltpu.make_async_remote_copy(
                src_ref=ins[t], dst_ref=outs[t], send_sem=send_sems.at[t], recv_sem=recv_sems.at[t],
                device_id=(x, y, 1 - c), device_id_type=MESH)
            cp.start()
            copies.append(cp)
        for cp in copies:
            cp.wait()

    return pl.pallas_call(
        body, in_specs=[HBM_SPEC] * n, out_specs=[HBM_SPEC] * n,
        out_shape=[jax.ShapeDtypeStruct(a.shape, a.dtype) for a in arrs],
        scratch_shapes=[pltpu.SemaphoreType.DMA((n,)), pltpu.SemaphoreType.DMA((n,))],
        name="swap_cores",
    )(*arrs)


def _gather_devices(a):
    def body(in_ref, out_ref, send_sems, recv_sems, local_sem):
        x, y, c = _coords()
        me = 4 * x + 2 * y + c
        own = pltpu.make_async_copy(in_ref, out_ref.at[me], local_sem)
        own.start()
        waits = [own.wait]
        for k in range(1, N_DEVICES):
            px = 1 - x if k & 4 else x
            py = 1 - y if k & 2 else y
            pc = 1 - c if k & 1 else c
            peer = (px, py, pc)
            send = pltpu.make_async_remote_copy(
                src_ref=in_ref, dst_ref=out_ref.at[me], send_sem=send_sems.at[k - 1], recv_sem=recv_sems.at[k - 1],
                device_id=peer, device_id_type=MESH)
            send.start()
            recv = pltpu.make_async_remote_copy(
                src_ref=in_ref, dst_ref=out_ref.at[4 * px + 2 * py + pc], send_sem=send_sems.at[k - 1],
                recv_sem=recv_sems.at[k - 1], device_id=peer, device_id_type=MESH)
            waits += [send.wait_send, recv.wait_recv]
        for w in waits:
            w()

    return pl.pallas_call(
        body, in_specs=[HBM_SPEC], out_specs=HBM_SPEC,
        out_shape=jax.ShapeDtypeStruct((N_DEVICES,) + a.shape, a.dtype),
        scratch_shapes=[pltpu.SemaphoreType.DMA((N_DEVICES - 1,)), pltpu.SemaphoreType.DMA((N_DEVICES - 1,)),
                        pltpu.SemaphoreType.DMA],
        name="gather_devices",
    )(a)


def _mlp_grad_epilogue(r, u):
    return r * (2.0 * jnp.maximum(u, 0.0))


def kernel(x, norm_gains, sb_w_qkv, sb_q_gain, sb_k_gain, sb_w_o, hg_w_in, hg_lb_logits, hg_norm_gain, hg_w_o, mlp_w1, mlp_w2, loss_target, m_norm_gains, m_sb_w_qkv, m_sb_q_gain, m_sb_k_gain, m_sb_w_o, m_hg_w_in, m_hg_lb_logits, m_hg_norm_gain, m_hg_w_o, m_mlp_w1, m_mlp_w2, v_norm_gains, v_sb_w_qkv, v_sb_q_gain, v_sb_k_gain, v_sb_w_o, v_hg_w_in, v_hg_lb_logits, v_hg_norm_gain, v_hg_w_o, v_mlp_w1, v_mlp_w2):
    depth = norm_gains.shape[0]
    n_sb, n_hg = sb_w_qkv.shape[0], hg_w_in.shape[0]
    xs, tgt = x[0], loss_target[0]
    s, d = xs.shape
    dq = d // N_CHIPS
    cx, cy, cc = _coords()
    chip = 2 * cx + cy
    chip_arr = jnp.reshape(chip, (1,)).astype(jnp.int32)

    def mixer_weights(layer):
        j = layer // 2
        return (sb_w_qkv[j], sb_w_o[j]) if layer % 2 == 0 else (hg_w_in[j], hg_w_o[j])

    w_in_g, ng_g, lbl_g = _ChipExchange(
        "gather", [mixer_weights(0)[0].astype(BF16), norm_gains, hg_lb_logits]).run("gather_first")
    gains = jnp.transpose(ng_g, (1, 2, 0, 3)).reshape(depth, 2, d)
    logits = jnp.transpose(lbl_g, (1, 0, 2)).reshape(n_hg, d)
    lbs, lb_p = _lb_fwd(logits)
    qg_rows = [jnp.tile(sb_q_gain[j], d // SB_HEAD_DIM)[None] for j in range(n_sb)]
    kg_rows = [jnp.tile(sb_k_gain[j], d // SB_HEAD_DIM)[None] for j in range(n_sb)]

    saved, wts = [], []
    xc = xs
    for layer in range(depth):
        j = layer // 2
        ahead = [mixer_weights(layer)[1], mlp_w1[layer], mlp_w2[layer]]
        if layer + 1 < depth:
            ahead.append(mixer_weights(layer + 1)[0])
        gather = _ChipExchange("gather", [a.astype(BF16) for a in ahead])
        h1 = _rmsnorm_fwd(xc, gains[layer, 0][None])
        if layer % 2 == 0:
            qkv = _mm_fwd_cols(h1, w_in_g, name="sb_qkv")
            qn, kn, vb = _qk_norm_fwd(qkv, qg_rows[j], kg_rows[j])
            o, moved = _sb_attn_fwd(qn, kn, vb, gather)
            x_mid = _mm_fwd_rows(o, moved[0], residual=xc, name="sb_out")
            mix = (qkv, qn, kn, vb, o)
        else:
            proj = _mm_fwd_cols(h1, w_in_g, name="hg_in")
            (y, o, states), moved = _hg_fwd(proj, lbs[j][None], hg_norm_gain[j][None], gather)
            x_mid = _mm_fwd_rows(y, moved[0], residual=xc, name="hg_out")
            mix = (proj, y, o, states)
        w_out_g, w1_g, w2_g = moved[:3]
        h2 = _rmsnorm_fwd(x_mid, gains[layer, 1][None])
        u = _mm_fwd_cols(h2, w1_g, name="mlp_up")
        x_out = _mm_fwd_rows(u, w2_g, residual=x_mid, a_fn=_relu2, name="mlp_down")
        saved.append((xc, h1, mix, x_mid, h2, u))
        wts.append((w_in_g, w_out_g, w1_g, w2_g))
        w_in_g = moved[3] if layer + 1 < depth else None
        xc = x_out

    sq, dx = _loss_head(xc, tgt)
    loss = lax.psum(jnp.sum(sq) * (0.5 / d), ("x", "y", "c"))

    dgains = [[None, None] for _ in range(depth)]
    dqg, dkg = [None] * n_sb, [None] * n_sb
    dhgain, dlb = [None] * n_hg, [None] * n_hg
    grads, received, pending = {}, {}, []

    def ready(key, parts):
        grads[key] = parts
        pending.append(key)

    def scatter_of(keys):
        return _ChipExchange("scatter", [grads[k] for k in keys]) if keys else None

    def sent(keys, moved):
        for k, r in zip(keys, moved):
            received[k] = r
            pending.remove(k)

    for layer in reversed(range(depth)):
        j = layer // 2
        x_in, h1, mix, x_mid, h2, u = saved[layer]
        w_in_g, w_out_g, w1_g, w2_g = wts[layer]
        du = _mm_bwd_rows(dx, w2_g, name="mlp_down_dx", out_dtype=BF16, epi_fn=_mlp_grad_epilogue, epi_args=(u,))
        ready(("w2", layer), _mm_dw_rows(u, dx, a_fn=_relu2, name="mlp_down_dw"))
        ready(("w1", layer), _mm_dw_cols(h2, du, name="mlp_up_dw"))
        dx, dgains[layer][1] = _mm_bwd_cols_norm(du, w1_g, x_mid, gains[layer, 1][None], dx, name="mlp_up_dx")
        if layer % 2 == 0:
            qkv, qn, kn, vb, o = mix
            do = _mm_bwd_rows(dx, w_out_g, name="sb_out_dx")
            ready(("out", layer), _mm_dw_rows(o, dx, name="sb_out_dw"))
            keys = list(pending)
            (dqn, dkn, dv), moved = _sb_attn_bwd(qn, kn, vb, do, scatter_of(keys))
            sent(keys, moved)
            d_in, dqg[j], dkg[j] = _qk_norm_bwd(qkv, qg_rows[j], kg_rows[j], dqn, dkn, dv)
            ready(("in", layer), _mm_dw_cols(h1, d_in, name="sb_qkv_dw"))
            dx_name = "sb_qkv_dx"
        else:
            proj, y, o, states = mix
            dy = _mm_bwd_rows(dx, w_out_g, name="hg_out_dx")
            ready(("out", layer), _mm_dw_rows(y, dx, name="hg_out_dw"))
            keys = list(pending)
            (dq_raw, df_raw, di, dg, dlb_row, dgain_heads), moved = _hg_bwd(
                proj, lbs[j][None], hg_norm_gain[j][None], o, states, dy, scatter_of(keys))
            sent(keys, moved)
            dlb[j], dhgain[j] = dlb_row, dgain_heads
            d_in = jnp.concatenate([dq_raw, df_raw, di, dg], axis=1)
            ready(("in", layer), _mm_dw_cols(h1, d_in, name="hg_in_dw"))
            dx_name = "hg_in_dx"
        if layer > 0:
            dx, dgains[layer][0] = _mm_bwd_cols_norm(d_in, w_in_g, x_in, gains[layer, 0][None], dx, name=dx_name)
        else:
            keys = list(pending)
            (dx, dgains[layer][0]), moved = _mm_bwd_cols_norm(
                d_in, w_in_g, x_in, gains[layer, 0][None], dx, name=dx_name, exchange=scatter_of(keys))
            sent(keys, moved)
    grad_x = dx[None]
    dlogits = _lb_bwd(lb_p, jnp.concatenate(dlb, axis=0))

    def chip_sum(kind, layers):
        total = jnp.zeros((len(layers),) + grads[kind, layers[0]].shape[1:], F32)
        for index, l in enumerate(layers):
            total = _sum_slots(grads[kind, l], received[kind, l], chip_arr, total, index)
        return total

    sb_layers, hg_layers = range(0, depth, 2), range(1, depth, 2)
    big_w = [sb_w_qkv, sb_w_o, hg_w_in, hg_w_o, mlp_w1, mlp_w2]
    big_m = [m_sb_w_qkv, m_sb_w_o, m_hg_w_in, m_hg_w_o, m_mlp_w1, m_mlp_w2]
    big_v = [v_sb_w_qkv, v_sb_w_o, v_hg_w_in, v_hg_w_o, v_mlp_w1, v_mlp_w2]
    chip_sums = [chip_sum("in", sb_layers), chip_sum("out", sb_layers), chip_sum("in", hg_layers),
                 chip_sum("out", hg_layers), chip_sum("w1", range(depth)), chip_sum("w2", range(depth))]
    other_core = _swap_cores(chip_sums)
    big = [_adamw(w, m, v, [a, b]) for w, m, v, a, b in zip(big_w, big_m, big_v, chip_sums, other_core)]

    pieces = [r for pair in dgains for r in pair] + [dlogits] + dqg + dkg + dhgain
    small = _sum_devices(_gather_devices(_pack_rows(pieces)))
    my_cols = lambda a: lax.dynamic_slice_in_dim(a, chip * dq, dq, axis=1)
    fold = lambda rows, width: jnp.sum(rows.reshape(rows.shape[0], -1, width), axis=1)
    base = 2 * depth + n_hg
    g_ng = my_cols(small[0:2 * depth]).reshape(norm_gains.shape)
    g_lbl = my_cols(small[2 * depth:base])
    g_qg = fold(small[base:base + n_sb], SB_HEAD_DIM)
    g_kg = fold(small[base + n_sb:base + 2 * n_sb], SB_HEAD_DIM)
    g_hgn = fold(small[base + 2 * n_sb:base + 2 * n_sb + n_hg], HG_HEAD_DIM)
    r_ng = _adamw(norm_gains, m_norm_gains, v_norm_gains, [g_ng])
    r_qg = _adamw(sb_q_gain, m_sb_q_gain, v_sb_q_gain, [g_qg])
    r_kg = _adamw(sb_k_gain, m_sb_k_gain, v_sb_k_gain, [g_kg])
    r_lbl = _adamw(hg_lb_logits, m_hg_lb_logits, v_hg_lb_logits, [g_lbl])
    r_hgn = _adamw(hg_norm_gain, m_hg_norm_gain, v_hg_norm_gain, [g_hgn])

    per_weight = [r_ng, big[0], r_qg, r_kg, big[1], big[2], r_lbl, r_hgn, big[3], big[4], big[5]]
    outs = [loss, grad_x]
    for field in range(4):
        outs += [r[field] for r in per_weight]
    return tuple(outs)
```

```python
import functools
import math

import numpy as np
import jax
import jax.numpy as jnp
from jax import lax
from jax.experimental import pallas as pl
from jax.experimental.pallas import tpu as pltpu

F32 = jnp.float32
BF16 = jnp.bfloat16
GRAD_SLOT_DTYPE = jnp.bfloat16

NORM_EPS = 1e-6
SB_HEAD_DIM = 64
HG_HEAD_DIM = 128
HG_CHUNK = 128
LANES = 128
VMEM_LIMIT_BYTES = 56 * 2 ** 20
N_CHIPS = 4
N_DEVICES = 8

ADAM_LR = 0.001
ADAM_B1 = 0.9
ADAM_B2 = 0.999
ADAM_EPS = 1e-08
ADAM_WD = 0.01
ADAM_STEP = 10

MESH = pl.DeviceIdType.MESH
HBM_SPEC = pl.BlockSpec(memory_space=pltpu.HBM)

NN = (((1,), (0,)), ((), ()))
NT = (((1,), (1,)), ((), ()))
TN = (((0,), (0,)), ((), ()))


def _params(sem=None):
    return pltpu.CompilerParams(dimension_semantics=sem, vmem_limit_bytes=VMEM_LIMIT_BYTES)


def _pick(dim, pref):
    for t in (1024, 768, 512, 384, 256, 128, 64, 32, 16, 8):
        if t <= pref and dim % t == 0:
            return t
    return dim


def _dot(a, b, dims=NN):
    return lax.dot_general(a, b, dims, preferred_element_type=F32)


def _sigmoid(x):
    e = jnp.exp(-jnp.abs(x))
    return jnp.where(x >= 0, 1.0, e) / (1.0 + e)


def _matmul(a, b, *, mode, grid, a_block, a_map, b_block, b_map, o_block, o_map, out_shape, out_dtype, name,
            a_fn=None, epi_fn=None, epi_args=(), epi_row_args=(), col_sums=False, exchange=None):
    nk = grid[2]
    dims = {"nn": NN, "nt": NT, "tn": TN}[mode]
    n_epi = len(epi_args) + len(epi_row_args)
    n_out = 2 if col_sums else 1
    tn = o_block[-1]
    assert not col_sums or grid[1] == 1, "the column sums stay resident only with one tile along N"

    def body(a_ref, b_ref, *rest):
        epi_refs = rest[:n_epi]
        o_ref = rest[n_epi]
        kk = pl.program_id(2)

        def emit(r):
            if epi_fn is not None:
                r = epi_fn(r, *[e[...] for e in epi_refs])
            if col_sums:
                r, row = r
                sums_ref = rest[n_epi + 1]
                first = pl.program_id(0) == 0

                @pl.when(first)
                def _():
                    sums_ref[...] = row

                @pl.when(jnp.logical_not(first))
                def _():
                    sums_ref[...] += row
            o_ref[...] = r.astype(o_ref.dtype)

        av = a_ref[...]
        if a_fn is not None:
            av = a_fn(av)
        part = _dot(av.astype(BF16), b_ref[...].astype(BF16), dims)
        if nk == 1:
            emit(part)
            return
        acc_ref = rest[n_epi + n_out]

        @pl.when(kk == 0)
        def _():
            acc_ref[...] = part

        @pl.when(kk > 0)
        def _():
            acc_ref[...] += part

        @pl.when(kk == nk - 1)
        def _():
            emit(acc_ref[...])

    acc_shape = tuple(d for d in o_block if d is not None)
    row_spec = pl.BlockSpec((1, tn), lambda i, j, kk: (0, j))
    in_specs = [pl.BlockSpec(a_block, a_map), pl.BlockSpec(b_block, b_map)]
    in_specs += [pl.BlockSpec(o_block, o_map) for _ in epi_args] + [row_spec for _ in epi_row_args]
    out_specs, out_shapes = [pl.BlockSpec(o_block, o_map)], [jax.ShapeDtypeStruct(out_shape, out_dtype)]
    if col_sums:
        out_specs, out_shapes = out_specs + [row_spec], out_shapes + [jax.ShapeDtypeStruct((1, out_shape[-1]), F32)]
    outs, moved = _call_with_exchange(
        body, exchange, grid=grid, in_specs=in_specs, out_specs=out_specs, out_shape=out_shapes,
        scratch_shapes=[pltpu.VMEM(acc_shape, F32)] if nk > 1 else [], name=name,
        args=(a, b, *epi_args, *epi_row_args), sequential=col_sums)
    result = tuple(outs) if col_sums else outs[0]
    return result if exchange is None else (result, moved)


def _relu2(u):
    r = jnp.maximum(u, 0.0)
    return r * r


def _add(r, res):
    return r + res


def _mm_fwd_cols(a, wg, *, name):
    s, k = a.shape
    ncs = wg.shape[2]
    tm, tk, tn = _pick(s, 1024), _pick(k, 1024), _pick(ncs, 1024)
    npb = ncs // tn
    return _matmul(a, wg, mode="nn", grid=(s // tm, N_CHIPS * npb, k // tk),
                   a_block=(tm, tk), a_map=lambda i, j, kk: (i, kk),
                   b_block=(None, tk, tn), b_map=lambda i, j, kk: (j // npb, kk, j % npb),
                   o_block=(tm, tn), o_map=lambda i, j, kk: (i, j),
                   out_shape=(s, N_CHIPS * ncs), out_dtype=F32, name=name)


def _rows_joined(wg):
    assert wg.shape[1] % 16 == 0, "joining the leading axes must not cross a tile of 16 rows"
    return wg.reshape(wg.shape[0] * wg.shape[1], wg.shape[2])


def _mm_fwd_rows(a, wg, *, residual, name, a_fn=None):
    s = a.shape[0]
    w = _rows_joined(wg)
    k, n = w.shape
    tm, tk, tn = _pick(s, 1024), _pick(k, 1024), _pick(n, 1024)
    return _matmul(a, w, mode="nn", grid=(s // tm, n // tn, k // tk),
                   a_block=(tm, tk), a_map=lambda i, j, kk: (i, kk),
                   b_block=(tk, tn), b_map=lambda i, j, kk: (kk, j),
                   o_block=(tm, tn), o_map=lambda i, j, kk: (i, j),
                   out_shape=(s, n), out_dtype=F32, name=name, a_fn=a_fn, epi_fn=_add, epi_args=(residual,))


def _rmsnorm_grad(dh, x, dx_res, gain):
    r = lax.rsqrt(jnp.mean(x * x, axis=-1, keepdims=True) + NORM_EPS)
    xhat = x * r
    dxhat = dh * gain
    dx = r * (dxhat - xhat * jnp.mean(dxhat * xhat, axis=-1, keepdims=True))
    return dx_res + dx, jnp.sum(dh * xhat, axis=0, keepdims=True)


def _mm_bwd_cols_norm(dy, wg, x, gain_row, dx_res, *, name, exchange=None):
    by_slot = dy.ndim == 3
    s = dy.shape[1] if by_slot else dy.shape[0]
    kw, ncs = wg.shape[1], wg.shape[2]
    tm, tk = _pick(s, 512), _pick(ncs, 1024)
    kpb = ncs // tk
    a_block, a_map = ((None, tm, tk), lambda i, j, kk: (kk // kpb, i, kk % kpb)) if by_slot else (
        (tm, tk), lambda i, j, kk: (i, kk))
    return _matmul(dy, wg, mode="nt", grid=(s // tm, 1, N_CHIPS * kpb),
                   a_block=a_block, a_map=a_map,
                   b_block=(None, kw, tk), b_map=lambda i, j, kk: (kk // kpb, j, kk % kpb),
                   o_block=(tm, kw), o_map=lambda i, j, kk: (i, j),
                   out_shape=(s, kw), out_dtype=F32, name=name, exchange=exchange,
                   epi_fn=_rmsnorm_grad, epi_args=(x, dx_res), epi_row_args=(gain_row,), col_sums=True)


def _mm_bwd_rows(dy, wg, *, name, out_dtype=F32, epi_fn=None, epi_args=()):
    s, n = dy.shape
    w = _rows_joined(wg)
    rows = w.shape[0]
    tm, tn, tk = _pick(s, 1024), _pick(rows, 1024), _pick(n, 1024)
    return _matmul(dy, w, mode="nt", grid=(s // tm, rows // tn, n // tk),
                   a_block=(tm, tk), a_map=lambda i, j, kk: (i, kk),
                   b_block=(tn, tk), b_map=lambda i, j, kk: (j, kk),
                   o_block=(tm, tn), o_map=lambda i, j, kk: (i, j),
                   out_shape=(s, rows), out_dtype=out_dtype, name=name, epi_fn=epi_fn, epi_args=epi_args)


def _mm_dw_cols(xa, dy, *, name):
    s, kx = xa.shape
    by_slot = dy.ndim == 3
    ncs = dy.shape[2] if by_slot else dy.shape[1] // N_CHIPS
    tm, tn, tk = _pick(kx, 1024), _pick(ncs, 1024), _pick(s, 1024)
    npb = ncs // tn
    b_block, b_map = ((None, tk, tn), lambda i, j, kk: (j // npb, kk, j % npb)) if by_slot else (
        (tk, tn), lambda i, j, kk: (kk, j))
    return _matmul(xa, dy, mode="tn", grid=(kx // tm, N_CHIPS * npb, s // tk),
                   a_block=(tk, tm), a_map=lambda i, j, kk: (kk, i),
                   b_block=b_block, b_map=b_map,
                   o_block=(None, tm, tn), o_map=lambda i, j, kk: (j // npb, i, j % npb),
                   out_shape=(N_CHIPS, kx, ncs), out_dtype=GRAD_SLOT_DTYPE, name=name)


def _mm_dw_rows(xa, dy, *, name, a_fn=None):
    s, n = dy.shape
    rows = xa.shape[1]
    assert (rows // N_CHIPS) % 16 == 0, "splitting the rows into slots must not cut a tile of 16 rows"
    tm, tn, tk = _pick(rows, 1024), _pick(n, 1024), _pick(s, 1024)
    dw = _matmul(xa, dy, mode="tn", grid=(rows // tm, n // tn, s // tk),
                 a_block=(tk, tm), a_map=lambda i, j, kk: (kk, i),
                 b_block=(tk, tn), b_map=lambda i, j, kk: (kk, j),
                 o_block=(tm, tn), o_map=lambda i, j, kk: (i, j),
                 out_shape=(rows, n), out_dtype=GRAD_SLOT_DTYPE, name=name, a_fn=a_fn)
    return dw.reshape(N_CHIPS, rows // N_CHIPS, n)


def _rmsnorm_fwd(x, gain_row):
    s, d = x.shape
    ts = _pick(s, 512)

    def body(x_ref, g_ref, h_ref):
        xv = x_ref[...]
        r = lax.rsqrt(jnp.mean(xv * xv, axis=-1, keepdims=True) + NORM_EPS)
        h_ref[...] = (xv * r * g_ref[...]).astype(h_ref.dtype)

    return pl.pallas_call(
        body, grid=(s // ts,),
        in_specs=[pl.BlockSpec((ts, d), lambda i: (i, 0)), pl.BlockSpec((1, d), lambda i: (0, 0))],
        out_specs=pl.BlockSpec((ts, d), lambda i: (i, 0)),
        out_shape=jax.ShapeDtypeStruct((s, d), BF16),
        compiler_params=_params(("parallel",)), name="rmsnorm_fwd",
    )(x, gain_row)


def _loss_head(y, target):
    s, d = y.shape
    ts = _pick(s, 512)

    def body(y_ref, t_ref, sq_ref, dy_ref):
        i = pl.program_id(0)
        err = y_ref[...] - t_ref[...]
        dy_ref[...] = err / d
        part = jnp.sum(err * err, axis=0, keepdims=True)

        @pl.when(i == 0)
        def _():
            sq_ref[...] = part

        @pl.when(i > 0)
        def _():
            sq_ref[...] += part

    return pl.pallas_call(
        body, grid=(s // ts,),
        in_specs=[pl.BlockSpec((ts, d), lambda i: (i, 0)), pl.BlockSpec((ts, d), lambda i: (i, 0))],
        out_specs=[pl.BlockSpec((1, d), lambda i: (0, 0)), pl.BlockSpec((ts, d), lambda i: (i, 0))],
        out_shape=[jax.ShapeDtypeStruct((1, d), F32), jax.ShapeDtypeStruct((s, d), F32)],
        compiler_params=_params(("arbitrary",)), name="loss_head",
    )(y, target)


def _pair_mean(val, low_half):
    s0 = jnp.sum(jnp.where(low_half, val, 0.0), axis=-1, keepdims=True)
    s1 = jnp.sum(jnp.where(low_half, 0.0, val), axis=-1, keepdims=True)
    return jnp.where(low_half, s0, s1) * (1.0 / SB_HEAD_DIM)


def _qk_norm_fwd(qkv, qgain_row, kgain_row):
    s, d3 = qkv.shape
    d = d3 // 3
    ts = _pick(s, 512)
    groups = d // LANES

    def body(q_ref, k_ref, v_ref, qg_ref, kg_ref, qn_ref, kn_ref, vb_ref):
        low_half = lax.broadcasted_iota(jnp.int32, (ts, LANES), 1) < SB_HEAD_DIM
        for src, gain, dst in ((q_ref, qg_ref, qn_ref), (k_ref, kg_ref, kn_ref)):
            for p in range(groups):
                cols = slice(p * LANES, (p + 1) * LANES)
                xp = src[:, cols]
                r = lax.rsqrt(_pair_mean(xp * xp, low_half) + NORM_EPS)
                dst[:, cols] = (xp * r * gain[:, cols]).astype(dst.dtype)
        vb_ref[...] = v_ref[...].astype(vb_ref.dtype)

    tok = lambda c: pl.BlockSpec((ts, d), lambda i: (i, c))
    row = pl.BlockSpec((1, d), lambda i: (0, 0))
    return pl.pallas_call(
        body, grid=(s // ts,),
        in_specs=[tok(0), tok(1), tok(2), row, row],
        out_specs=[tok(0), tok(0), tok(0)],
        out_shape=[jax.ShapeDtypeStruct((s, d), BF16)] * 3,
        compiler_params=_params(("parallel",)), name="qk_norm_fwd",
    )(qkv, qkv, qkv, qgain_row, kgain_row)


def _qk_norm_bwd(qkv, qgain_row, kgain_row, dqn, dkn, dv):
    s, d3 = qkv.shape
    d = d3 // 3
    ts = _pick(s, 512)
    groups = d // LANES

    def body(q_ref, k_ref, qg_ref, kg_ref, dqn_ref, dkn_ref, dv_ref, dqkv_ref, dqg_ref, dkg_ref):
        i = pl.program_id(0)
        low_half = lax.broadcasted_iota(jnp.int32, (ts, LANES), 1) < SB_HEAD_DIM
        for which, (src, gain, dsrc, dgain) in enumerate(((q_ref, qg_ref, dqn_ref, dqg_ref),
                                                          (k_ref, kg_ref, dkn_ref, dkg_ref))):
            for p in range(groups):
                cols = slice(p * LANES, (p + 1) * LANES)
                xp = src[:, cols]
                r = lax.rsqrt(_pair_mean(xp * xp, low_half) + NORM_EPS)
                xhat = xp * r
                dy = dsrc[:, cols]
                dxhat = dy * gain[:, cols]
                dx = r * (dxhat - xhat * _pair_mean(dxhat * xhat, low_half))
                dqkv_ref[:, which * d + p * LANES: which * d + (p + 1) * LANES] = dx.astype(dqkv_ref.dtype)
                part = jnp.sum(dy * xhat, axis=0, keepdims=True)

                @pl.when(i == 0)
                def _():
                    dgain[:, cols] = part

                @pl.when(i > 0)
                def _():
                    dgain[:, cols] += part
        dqkv_ref[:, 2 * d:] = dv_ref[...].astype(dqkv_ref.dtype)

    tok = lambda c: pl.BlockSpec((ts, d), lambda i: (i, c))
    row = pl.BlockSpec((1, d), lambda i: (0, 0))
    return pl.pallas_call(
        body, grid=(s // ts,),
        in_specs=[tok(0), tok(1), row, row, tok(0), tok(0), tok(0)],
        out_specs=[pl.BlockSpec((ts, d3), lambda i: (i, 0)), row, row],
        out_shape=[jax.ShapeDtypeStruct((s, d3), BF16), jax.ShapeDtypeStruct((1, d), F32),
                   jax.ShapeDtypeStruct((1, d), F32)],
        compiler_params=_params(("arbitrary",)), name="qk_norm_bwd",
    )(qkv, qkv, qgain_row, kgain_row, dqn, dkn, dv)


def _split2(x):
    hi = x.astype(BF16)
    lo = (x - hi.astype(F32)).astype(BF16)
    return hi, lo


SB_TK = 128


def _sb_consts(tk):
    j = np.arange(tk)
    ones = np.ones((tk, tk), np.float32)
    out = []
    for tri in ((j[:, None] >= j[None, :]), (j[:, None] <= j[None, :])):
        half = np.concatenate([tri.astype(np.float32), ones], axis=1)
        out.append(jnp.asarray(np.concatenate([half, half], axis=0), BF16))
    return out


def _head_stack(blk, low_half):
    f = blk.astype(F32)
    return jnp.concatenate([jnp.where(low_half, f, 0.0), jnp.where(low_half, 0.0, f)], axis=0).astype(BF16)


def _sb_tile_sums(z, valid, tri2):
    e = jnp.exp(-jnp.abs(z))
    lstay = jnp.minimum(-z, 0.0) - jnp.log(1.0 + e)
    if valid is not None:
        lstay = jnp.where(valid, lstay, 0.0)
    hi, lo = _split2(lstay)
    return e, _dot(jnp.concatenate([hi, lo], axis=1), tri2)


def _sb_weights(z, c2, valid, run):
    w = jnp.exp(z + c2[:, :SB_TK] + run)
    return w if valid is None else jnp.where(valid, w, 0.0)


EXP_IS_ZERO_BELOW = -110.0


def _max_row_norm(x):
    f = x.astype(F32)
    return jnp.sqrt(jnp.max(jnp.sum(f * f, axis=-1, keepdims=True)))


def _sb_score_bound(qs, kmax_ref):
    return _max_row_norm(qs) * jnp.max(kmax_ref[...]) * 1.01 + 1.0


def _sb_rest_is_zero(run_ref, bound):
    return jnp.max(jnp.maximum(run_ref[0], run_ref[1])) + bound < EXP_IS_ZERO_BELOW


def _sb_attn_fwd(qn, kn, vb, exchange=None):
    s, d = qn.shape
    tk = SB_TK
    tq = _pick(s, 256)
    nq, ndiag = s // tq, tq // tk
    assert tq % (2 * tk) == 0, "tiles below the diagonal are taken two at a time"
    npairs = d // LANES
    scale = 1.0 / math.sqrt(SB_HEAD_DIM)
    tri_ge2, _ = _sb_consts(tk)

    def body(q_ref, k_ref, v_ref, tri_ref, o_ref, acc_ref, run_ref, kmax_ref):
        qi = pl.program_id(1)

        @pl.when(qi == 0)
        def _():
            kmax_ref[...] = jnp.full(kmax_ref.shape, _max_row_norm(k_ref[...]), F32)

        low_half = lax.broadcasted_iota(jnp.int32, (tk, LANES), 1) < SB_HEAD_DIM
        row = lax.broadcasted_iota(jnp.int32, (tq, tk), 0)
        col = lax.broadcasted_iota(jnp.int32, (tq, tk), 1)
        qs = (q_ref[...].astype(F32) * scale).astype(BF16)
        bound = _sb_score_bound(qs, kmax_ref)
        acc_ref[...] = jnp.zeros_like(acc_ref)
        run_ref[...] = jnp.zeros_like(run_ref)
        n_full = qi * ndiag

        def sums(kb, dd):
            koff = pl.multiple_of(kb * tk, tk)
            kcat = _head_stack(k_ref[pl.ds(koff, tk), :], low_half)
            vcat = _head_stack(v_ref[pl.ds(koff, tk), :], low_half)
            z2 = _dot(qs, kcat, NT)
            valid = None if dd is None else row > col + dd * tk
            zs = [z2[:, h * tk:(h + 1) * tk] for h in range(2)]
            return zs, [_sb_tile_sums(z, valid, tri_ref[...])[1] for z in zs], valid, vcat

        def finish(zs, c2s, valid, vcat):
            ws = []
            for h in range(2):
                ws.append(_sb_weights(zs[h], c2s[h], valid, run_ref[h]).astype(BF16))
                run_ref[h] += c2s[h][:, tk:]
            acc_ref[...] += _dot(jnp.concatenate(ws, axis=1), vcat)

        for pre in [sums(n_full + dd, dd) for dd in reversed(range(ndiag))]:
            finish(*pre)

        def two_tiles(carry):
            it, _ = carry
            kb = n_full - 1 - 2 * it
            first, second = sums(kb, None), sums(kb - 1, None)
            finish(*first)
            finish(*second)
            return it + 1, _sb_rest_is_zero(run_ref, bound)

        lax.while_loop(lambda c: jnp.logical_and(c[0] < n_full // 2, jnp.logical_not(c[1])), two_tiles,
                       (jnp.int32(0), _sb_rest_is_zero(run_ref, bound)))
        o_ref[...] = acc_ref[...]

    blk = pl.BlockSpec((tq, LANES), lambda p, i: (i, p))
    full = pl.BlockSpec((s, LANES), lambda p, i: (0, p))
    (o,), moved = _call_with_exchange(
        body, exchange, grid=(npairs, nq),
        in_specs=[blk, full, full, pl.BlockSpec((2 * tk, 2 * tk), lambda p, i: (0, 0))],
        out_specs=[blk], out_shape=[jax.ShapeDtypeStruct((s, d), F32)],
        scratch_shapes=[pltpu.VMEM((tq, LANES), F32), pltpu.VMEM((2, tq, tk), F32), pltpu.VMEM((8, LANES), F32)],
        name="sb_attn_fwd", args=(qn, kn, vb, tri_ge2))
    return o, moved


def _sb_attn_bwd(qn, kn, vb, do, exchange=None):
    s, d = qn.shape
    tk = SB_TK
    tq = _pick(s, 256)
    nq, ndiag = s // tq, tq // tk
    assert tq % (2 * tk) == 0, "tiles below the diagonal are taken two at a time"
    npairs = d // LANES
    scale = 1.0 / math.sqrt(SB_HEAD_DIM)
    tri_ge2, tri_le2 = _sb_consts(tk)

    def body(q_ref, k_ref, v_ref, do_ref, tge_ref, tle_ref, dq_ref, dk_ref, dv_ref,
             g_cache, s_cache, run_ref, dq_acc, kmax_ref):
        qi = pl.program_id(1)

        @pl.when(qi == 0)
        def _():
            dk_ref[...] = jnp.zeros_like(dk_ref)
            dv_ref[...] = jnp.zeros_like(dv_ref)
            kmax_ref[...] = jnp.full(kmax_ref.shape, _max_row_norm(k_ref[...]), F32)

        low_half = lax.broadcasted_iota(jnp.int32, (tk, LANES), 1) < SB_HEAD_DIM
        row = lax.broadcasted_iota(jnp.int32, (tq, tk), 0)
        col = lax.broadcasted_iota(jnp.int32, (tq, tk), 1)
        qs = (q_ref[...].astype(F32) * scale).astype(BF16)
        bound = _sb_score_bound(qs, kmax_ref)
        dob = do_ref[...].astype(BF16)
        n_full = qi * ndiag

        def a_sums(kb, dd):
            koff = pl.multiple_of(kb * tk, tk)
            kcat = _head_stack(k_ref[pl.ds(koff, tk), :], low_half)
            vcat = _head_stack(v_ref[pl.ds(koff, tk), :], low_half)
            z2 = _dot(qs, kcat, NT)
            dw2 = _dot(dob, vcat, NT)
            valid = None if dd is None else row > col + dd * tk
            c2s = []
            for h in range(2):
                cols = slice(h * tk, (h + 1) * tk)
                z = z2[:, cols]
                e, c2 = _sb_tile_sums(z, valid, tge_ref[...])
                s_cache[kb, :, cols] = jnp.where(z >= 0, 1.0, e) / (1.0 + e)
                c2s.append(c2)
            return kb, koff, z2, dw2, c2s, valid

        def a_finish(kb, koff, z2, dw2, c2s, valid):
            ws = []
            for h in range(2):
                cols = slice(h * tk, (h + 1) * tk)
                w = _sb_weights(z2[:, cols], c2s[h], valid, run_ref[h])
                run_ref[h] += c2s[h][:, tk:]
                g_cache[kb, :, cols] = w * dw2[:, cols]
                ws.append(w.astype(BF16))
            dv2 = _dot(jnp.concatenate(ws, axis=1), dob, TN)
            dv_ref[pl.ds(koff, tk), :] += jnp.where(low_half, dv2[:tk], dv2[tk:])

        def b_sums(kb, dd):
            gs = [g_cache[kb, :, h * tk:(h + 1) * tk] for h in range(2)]
            p2s = [_dot(jnp.concatenate(_split2(g), axis=1), tle_ref[...]) for g in gs]
            return kb, gs, p2s, (None if dd is None else row > col + dd * tk)

        def b_finish(kb, gs, p2s, valid):
            koff = pl.multiple_of(kb * tk, tk)
            dzs = []
            for h in range(2):
                dz = gs[h] - s_cache[kb, :, h * tk:(h + 1) * tk] * (p2s[h][:, :tk] + run_ref[h])
                if valid is not None:
                    dz = jnp.where(valid, dz, 0.0)
                run_ref[h] += p2s[h][:, tk:]
                dzs.append(dz.astype(BF16))
            dzcat = jnp.concatenate(dzs, axis=1)
            dq_acc[...] += _dot(dzcat, _head_stack(k_ref[pl.ds(koff, tk), :], low_half))
            dk2 = _dot(dzcat, qs, TN)
            dk_ref[pl.ds(koff, tk), :] += jnp.where(low_half, dk2[:tk], dk2[tk:])

        run_ref[...] = jnp.zeros_like(run_ref)
        for pre in [a_sums(n_full + dd, dd) for dd in reversed(range(ndiag))]:
            a_finish(*pre)

        def two_a(carry):
            it, _ = carry
            kb = n_full - 1 - 2 * it
            first, second = a_sums(kb, None), a_sums(kb - 1, None)
            a_finish(*first)
            a_finish(*second)
            return it + 1, _sb_rest_is_zero(run_ref, bound)

        trips, _ = lax.while_loop(lambda c: jnp.logical_and(c[0] < n_full // 2, jnp.logical_not(c[1])), two_a,
                                  (jnp.int32(0), _sb_rest_is_zero(run_ref, bound)))

        run_ref[...] = jnp.zeros_like(run_ref)
        dq_acc[...] = jnp.zeros_like(dq_acc)
        kb_first = n_full - 2 * trips

        def two_b(it, carry):
            first, second = b_sums(kb_first + 2 * it, None), b_sums(kb_first + 2 * it + 1, None)
            b_finish(*first)
            b_finish(*second)
            return carry

        lax.fori_loop(0, trips, two_b, 0)
        for pre in [b_sums(n_full + dd, dd) for dd in range(ndiag)]:
            b_finish(*pre)
        dq_ref[...] = dq_acc[...] * scale

    blk = pl.BlockSpec((tq, LANES), lambda p, i: (i, p))
    full = pl.BlockSpec((s, LANES), lambda p, i: (0, p))
    tri = pl.BlockSpec((2 * tk, 2 * tk), lambda p, i: (0, 0))
    return _call_with_exchange(
        body, exchange, grid=(npairs, nq),
        in_specs=[blk, full, full, blk, tri, tri],
        out_specs=[blk, full, full],
        out_shape=[jax.ShapeDtypeStruct((s, d), F32)] * 3,
        scratch_shapes=[pltpu.VMEM((s // tk, tq, 2 * tk), F32), pltpu.VMEM((s // tk, tq, 2 * tk), F32),
                        pltpu.VMEM((2, tq, tk), F32), pltpu.VMEM((tq, LANES), F32), pltpu.VMEM((8, LANES), F32)],
        name="sb_attn_bwd", args=(qn, kn, vb, do, tri_ge2, tri_le2))


def _hg_consts(c):
    levels = []
    h = c // 2
    while h >= 1:
        levels.append(h)
        h //= 2
    t = np.arange(c)
    j = t[None, :]
    rows, masks = [], []
    for h in levels:
        blk = t // (2 * h)
        mid = blk * 2 * h + h - 1
        second = (t % (2 * h)) >= h
        rows.append(second[:, None] & (j > mid[:, None]) & (j <= t[:, None]))
        rows.append((~second)[:, None] & (j > t[:, None]) & (j <= mid[:, None]))
        masks.append((blk[:, None] == blk[None, :]) & second[:, None] & (~second)[None, :])
    rows.append(j <= t[:, None])
    rows.append(j > t[:, None])
    masks.append(t[:, None] == t[None, :])
    m_all = np.concatenate(rows, axis=0).astype(np.float32)
    mask_all = np.stack(masks, axis=0).astype(np.float32)
    suffix = (t[None, :] >= t[:, None]).astype(np.float32)
    return len(levels), jnp.asarray(m_all, BF16), jnp.asarray(mask_all, F32), jnp.asarray(suffix, BF16)


def _split3(x):
    hi = x.astype(BF16)
    r1 = x - hi.astype(F32)
    mid = r1.astype(BF16)
    lo = (r1 - mid.astype(F32)).astype(BF16)
    return jnp.concatenate([hi, mid, lo], axis=1)


def _join3(e):
    n = e.shape[1] // 3
    return e[:, :n] + e[:, n:2 * n] + e[:, 2 * n:]


def _hg_gates(qr, fr, lb):
    sq = _sigmoid(qr)
    sf = _sigmoid(fr)
    forget = lb + (1.0 - lb) * sf
    return qr * sq, sq, sf, forget, jnp.log(forget), 1.0 - forget


def _hg_scores(q, k, expo, masks, nlev, c):
    qb, kb = q.astype(BF16), k.astype(BF16)
    a = masks[nlev] * _dot(qb, kb, NT)
    scaled = []
    for li in range(nlev):
        fq = jnp.exp(expo[(2 * li) * c:(2 * li + 1) * c])
        fk = jnp.exp(expo[(2 * li + 1) * c:(2 * li + 2) * c])
        qs, ks = (q * fq).astype(BF16), (k * fk).astype(BF16)
        a = a + masks[li] * _dot(qs, ks, NT)
        scaled.append((qs, ks, fq, fk))
    return a, scaled, qb, kb


def _hg_heads_per_step(nh):
    return 2 if nh % 2 == 0 else 1


def _hg_fwd(proj, lb_row, gain_row, exchange=None):
    s, d4 = proj.shape
    d = d4 // 4
    nh = d // HG_HEAD_DIM
    c = min(HG_CHUNK, s)
    tb = _pick(s, 512)
    ncb = tb // c
    nlev, m_all, mask_all, _ = _hg_consts(c)
    nrow = m_all.shape[0]

    hp = _hg_heads_per_step(nh)
    wide = hp * HG_HEAD_DIM

    def body(q_ref, f_ref, i_ref, g_ref, lb_ref, gain_ref, mall_ref, mask_ref, y_ref, o_ref, st_out_ref, st_ref):
        b = pl.program_id(1)

        @pl.when(b == 0)
        def _():
            st_ref[...] = jnp.zeros_like(st_ref)

        gain = gain_ref[...]

        def chunk(ci, carry):
            rows = pl.ds(pl.multiple_of(ci * c, c), c)
            for hh in range(hp):
                cols = slice(hh * HG_HEAD_DIM, (hh + 1) * HG_HEAD_DIM)
                q, _, _, _, lf, k = _hg_gates(q_ref[rows, cols], f_ref[rows, cols], lb_ref[:, cols])
                v = i_ref[rows, cols].astype(BF16)
                expo = _join3(_dot(mall_ref[...], _split3(lf)))
                st = st_ref[hh]
                st_out_ref[ci, hh] = st
                a, _, _, _ = _hg_scores(q, k, expo, mask_ref[...], nlev, c)
                b_cum = expo[2 * nlev * c:(2 * nlev + 1) * c]
                e_tail = expo[(2 * nlev + 1) * c:(2 * nlev + 2) * c]
                q_in = (q * jnp.exp(b_cum)).astype(BF16)
                o = _dot(q_in, st.astype(BF16), NT) + _dot(a.astype(BF16), v)
                k_dec = (k * jnp.exp(e_tail)).astype(BF16)
                st_ref[hh] = st * jnp.exp(b_cum[c - 1:c, :]) + _dot(v, k_dec, TN)
                o_ref[rows, cols] = o
                r = lax.rsqrt(jnp.mean(o * o, axis=-1, keepdims=True) + NORM_EPS)
                y_ref[rows, cols] = (o * r * gain * _sigmoid(g_ref[rows, cols])).astype(y_ref.dtype)
            return carry

        lax.fori_loop(0, ncb, chunk, 0)

    part = lambda k: pl.BlockSpec((tb, wide), lambda h, b: (b, k * (nh // hp) + h))
    head_row = pl.BlockSpec((1, wide), lambda h, b: (0, h))
    tok = pl.BlockSpec((tb, wide), lambda h, b: (b, h))
    return _call_with_exchange(
        body, exchange, grid=(nh // hp, s // tb),
        in_specs=[part(0), part(1), part(2), part(3), head_row,
                  pl.BlockSpec((1, HG_HEAD_DIM), lambda h, b: (0, 0)),
                  pl.BlockSpec((nrow, c), lambda h, b: (0, 0)),
                  pl.BlockSpec((nlev + 1, c, c), lambda h, b: (0, 0, 0))],
        out_specs=[tok, tok, pl.BlockSpec((ncb, hp, HG_HEAD_DIM, HG_HEAD_DIM), lambda h, b: (b, h, 0, 0))],
        out_shape=[jax.ShapeDtypeStruct((s, d), BF16), jax.ShapeDtypeStruct((s, d), F32),
                   jax.ShapeDtypeStruct((s // c, nh, HG_HEAD_DIM, HG_HEAD_DIM), F32)],
        scratch_shapes=[pltpu.VMEM((hp, HG_HEAD_DIM, HG_HEAD_DIM), F32)],
        name="hg_fwd", args=(proj, proj, proj, proj, lb_row, gain_row, m_all, mask_all))


def _hg_bwd(proj, lb_row, gain_row, o_saved, states, dy, exchange=None):
    s, d4 = proj.shape
    d = d4 // 4
    nh = d // HG_HEAD_DIM
    c = min(HG_CHUNK, s)
    tb = _pick(s, 512)
    ncb = tb // c
    nb = s // tb
    nlev, m_all, mask_all, suffix = _hg_consts(c)
    nrow = m_all.shape[0]
    hp = _hg_heads_per_step(nh)
    wide = hp * HG_HEAD_DIM

    def body(q_ref, f_ref, i_ref, g_ref, lb_ref, gain_ref, o_ref, st_in_ref, dy_ref, mall_ref, mask_ref, suf_ref,
             dproj_ref, dlb_ref, dgain_ref, dst_ref, run_ref):
        b = pl.program_id(1)

        @pl.when(b == 0)
        def _():
            dst_ref[...] = jnp.zeros_like(dst_ref)
            run_ref[...] = jnp.zeros_like(run_ref)
            dlb_ref[...] = jnp.zeros_like(dlb_ref)
            dgain_ref[...] = jnp.zeros_like(dgain_ref)

        gain = gain_ref[...]

        def head_chunk(ci, rows, hh, cols):
            lb = lb_ref[:, cols]
            qr, fr = q_ref[rows, cols], f_ref[rows, cols]
            q, sq, sf, forget, lf, k = _hg_gates(qr, fr, lb)
            v = i_ref[rows, cols].astype(BF16)
            expo = _join3(_dot(mall_ref[...], _split3(lf)))
            masks = mask_ref[...]
            o = o_ref[rows, cols]
            dyv = dy_ref[rows, cols]
            sg = _sigmoid(g_ref[rows, cols])
            r = lax.rsqrt(jnp.mean(o * o, axis=-1, keepdims=True) + NORM_EPS)
            ohat = o * r
            dyn = dyv * sg
            dproj_ref[3, rows, cols] = (dyv * ohat * gain * sg * (1.0 - sg)).astype(dproj_ref.dtype)
            dgain_ref[:, cols] += jnp.sum(dyn * ohat, axis=0, keepdims=True)
            dohat = dyn * gain
            do = (r * (dohat - ohat * jnp.mean(dohat * ohat, axis=-1, keepdims=True))).astype(BF16)
            dst = dst_ref[hh]
            dstb = dst.astype(BF16)
            a, scaled, qb, kb = _hg_scores(q, k, expo, masks, nlev, c)
            f_cum = jnp.exp(expo[2 * nlev * c:(2 * nlev + 1) * c])
            f_tail = jnp.exp(expo[(2 * nlev + 1) * c:(2 * nlev + 2) * c])
            q_in = (q * f_cum).astype(BF16)
            k_dec = (k * f_tail).astype(BF16)
            t_in = _join3(_dot(do, _split3(st_in_ref[ci, hh])))
            t_st = _join3(_dot(v, _split3(dst)))
            da = _dot(do, v, NT)
            dam = (masks[nlev] * da).astype(BF16)
            dq = t_in * f_cum + _dot(dam, kb)
            dk = t_st * f_tail + _dot(dam, qb, TN)
            db = q_in.astype(F32) * t_in - k_dec.astype(F32) * t_st
            for li in range(nlev):
                qs, ks, fq, fk = scaled[li]
                dam = (masks[li] * da).astype(BF16)
                t_q = _dot(dam, ks)
                t_k = _dot(dam, qs, TN)
                dq = dq + t_q * fq
                dk = dk + t_k * fk
                db = db + (qs.astype(F32) * t_q - ks.astype(F32) * t_k)
            dv = _dot(a.astype(BF16), do, TN) + _dot(k_dec, dstb, NT)
            dst_ref[hh] = dst * f_cum[c - 1:c, :] + _dot(do, q_in, TN)
            dlf = _join3(_dot(suf_ref[...], _split3(db))) + run_ref[hh]
            run_ref[hh] = dlf[0:1, :]
            dforget = dlf / forget - dk
            dlb_ref[:, cols] += jnp.sum(dforget * (1.0 - sf), axis=0, keepdims=True)
            dproj_ref[1, rows, cols] = (dforget * (1.0 - lb) * sf * (1.0 - sf)).astype(dproj_ref.dtype)
            dproj_ref[0, rows, cols] = (dq * sq * (1.0 + qr * (1.0 - sq))).astype(dproj_ref.dtype)
            dproj_ref[2, rows, cols] = dv.astype(dproj_ref.dtype)

        def chunk(it, carry):
            ci = ncb - 1 - it
            rows = pl.ds(pl.multiple_of(ci * c, c), c)
            for hh in range(hp):
                head_chunk(ci, rows, hh, slice(hh * HG_HEAD_DIM, (hh + 1) * HG_HEAD_DIM))
            return carry

        lax.fori_loop(0, ncb, chunk, 0)

    part = lambda k: pl.BlockSpec((tb, wide), lambda h, b: (nb - 1 - b, k * (nh // hp) + h))
    head_row = pl.BlockSpec((1, wide), lambda h, b: (0, h))
    tok = pl.BlockSpec((tb, wide), lambda h, b: (nb - 1 - b, h))
    const2 = lambda shape: pl.BlockSpec(shape, lambda h, b: (0, 0))
    return _call_with_exchange(
        body, exchange, grid=(nh // hp, nb),
        in_specs=[part(0), part(1), part(2), part(3), head_row, const2((1, HG_HEAD_DIM)), tok,
                  pl.BlockSpec((ncb, hp, HG_HEAD_DIM, HG_HEAD_DIM), lambda h, b: (nb - 1 - b, h, 0, 0)),
                  tok, const2((nrow, c)), pl.BlockSpec((nlev + 1, c, c), lambda h, b: (0, 0, 0)), const2((c, c))],
        out_specs=[pl.BlockSpec((4, tb, wide), lambda h, b: (0, nb - 1 - b, h)), head_row, head_row],
        out_shape=[jax.ShapeDtypeStruct((4, s, d), BF16)] + [jax.ShapeDtypeStruct((1, d), F32)] * 2,
        scratch_shapes=[pltpu.VMEM((hp, HG_HEAD_DIM, HG_HEAD_DIM), F32), pltpu.VMEM((hp, 1, HG_HEAD_DIM), F32)],
        name="hg_bwd", args=(proj, proj, proj, proj, lb_row, gain_row, o_saved, states, dy, m_all, mask_all, suffix))


def _lb_fwd(logits):
    n, d = logits.shape

    def body(l_ref, lb_ref, p_ref):
        rows = [l_ref[i:i + 1, :] for i in range(n)]
        m = functools.reduce(jnp.maximum, rows)
        es = [jnp.exp(r - m) for r in rows]
        tot = functools.reduce(lambda a, b: a + b, es)
        ps = [e / tot for e in es]
        run = jnp.zeros_like(ps[0])
        for i in range(n):
            run = run + ps[i]
            lb_ref[i:i + 1, :] = run - ps[0]
            p_ref[i:i + 1, :] = ps[i]

    return pl.pallas_call(
        body, out_shape=[jax.ShapeDtypeStruct((n, d), F32)] * 2, name="lb_fwd",
    )(logits)


def _lb_bwd(p, dlb):
    n, d = p.shape

    def body(p_ref, dlb_ref, dl_ref):
        ps = [p_ref[i:i + 1, :] for i in range(n)]
        ds = [dlb_ref[i:i + 1, :] for i in range(n)]
        total = functools.reduce(lambda a, b: a + b, ds)
        dps = []
        for i in range(n):
            dp = functools.reduce(lambda a, b: a + b, ds[i:])
            dps.append(dp - total if i == 0 else dp)
        inner = functools.reduce(lambda a, b: a + b, [pi * di for pi, di in zip(ps, dps)])
        for i in range(n):
            dl_ref[i:i + 1, :] = ps[i] * (dps[i] - inner)

    return pl.pallas_call(body, out_shape=jax.ShapeDtypeStruct((n, d), F32), name="lb_bwd")(p, dlb)


def _as2d(a):
    return a.reshape(-1, a.shape[-1])


def _adamw(w, m, v, grads, exchange=None):
    shape = w.shape
    w2, m2, v2 = _as2d(w), _as2d(m), _as2d(v)
    g2 = [_as2d(g) for g in grads]
    rows, cols = w2.shape
    tr = _pick(rows, 512)
    ng = len(g2)
    bc1 = 1.0 - ADAM_B1 ** ADAM_STEP
    bc2 = 1.0 - ADAM_B2 ** ADAM_STEP

    def body(w_ref, m_ref, v_ref, *rest):
        g = rest[0][...]
        for extra in rest[1:ng]:
            g = g + extra[...]
        g_out, d_out, m_out, v_out = rest[ng:]
        mn = ADAM_B1 * m_ref[...] + (1.0 - ADAM_B1) * g
        vn = ADAM_B2 * v_ref[...] + (1.0 - ADAM_B2) * (g * g)
        m_hat = mn / bc1
        v_hat = vn / bc2
        g_out[...] = g
        d_out[...] = -ADAM_LR * (m_hat / (jnp.sqrt(v_hat) + ADAM_EPS) + ADAM_WD * w_ref[...])
        m_out[...] = mn
        v_out[...] = vn

    spec = pl.BlockSpec((tr, cols), lambda i: (i, 0))
    outs, moved = _call_with_exchange(
        body, exchange, grid=(rows // tr,), in_specs=[spec] * (3 + ng), out_specs=[spec] * 4,
        out_shape=[jax.ShapeDtypeStruct((rows, cols), F32)] * 4, scratch_shapes=[], name="adamw",
        args=(w2, m2, v2, *g2))
    result = tuple(o.reshape(shape) for o in outs)
    return result if exchange is None else (result, moved)


def _sum_slots(parts, recv, chip, into, index):
    _, rows, cols = parts.shape
    tr = _pick(rows, 512)

    def body(chip_ref, own_ref, r0_ref, r1_ref, r2_ref, into_ref, o_ref):
        f = lambda r: r[...].astype(F32)
        o_ref[...] = ((f(own_ref) + f(r0_ref)) + f(r1_ref)) + f(r2_ref)

    grid_spec = pltpu.PrefetchScalarGridSpec(
        num_scalar_prefetch=1, grid=(rows // tr,),
        in_specs=[pl.BlockSpec((None, tr, cols), lambda i, chip_ref: (chip_ref[0], i, 0))]
        + [pl.BlockSpec((None, tr, cols), functools.partial(lambda i, chip_ref, k: (k, i, 0), k=k)) for k in range(3)]
        + [pl.BlockSpec(memory_space=pl.ANY)],
        out_specs=pl.BlockSpec((None, tr, cols), lambda i, chip_ref: (index, i, 0)))
    return pl.pallas_call(
        body, grid_spec=grid_spec, out_shape=jax.ShapeDtypeStruct(into.shape, F32),
        input_output_aliases={5: 0}, compiler_params=_params(("parallel",)), name="sum_slots",
    )(chip, parts, recv, recv, recv, into)


def _pack_rows(pieces):
    cols = pieces[0].shape[1]
    used = sum(p.shape[0] for p in pieces)
    rows = -(-used // 8) * 8

    def body(*refs):
        out_ref = refs[-1]
        at = 0
        for ref in refs[:-1]:
            out_ref[at:at + ref.shape[0], :] = ref[...]
            at += ref.shape[0]
        if at < rows:
            out_ref[at:rows, :] = jnp.zeros((rows - at, cols), F32)

    return pl.pallas_call(body, out_shape=jax.ShapeDtypeStruct((rows, cols), F32), name="pack_rows")(*pieces)


def _sum_devices(gathered):
    n, rows, cols = gathered.shape

    def body(g_ref, o_ref):
        acc = g_ref[0]
        for i in range(1, n):
            acc = acc + g_ref[i]
        o_ref[...] = acc

    return pl.pallas_call(body, out_shape=jax.ShapeDtypeStruct((rows, cols), F32), name="sum_devices")(gathered)


def _coords():
    return lax.axis_index("x"), lax.axis_index("y"), lax.axis_index("c")


def _chip_peers(x, y, c):
    out = []
    for fx, fy in ((0, 1), (1, 0), (1, 1)):
        px = 1 - x if fx else x
        py = 1 - y if fy else y
        out.append(((px, py, c), 2 * px + py))
    return out


class _ChipExchange:
    def __init__(self, kind, arrays):
        self.kind, self.kinds, self.arrays, self.n = kind, [kind] * len(arrays), list(arrays), len(arrays)
        self._shapes()

    def also(self, kind, arrays):
        self.kinds += [kind] * len(arrays)
        self.arrays += list(arrays)
        self.n = len(self.arrays)
        self._shapes()
        return self

    def _shapes(self):
        lead = {"gather": lambda a: (N_CHIPS,) + a.shape, "scatter": lambda a: (3,) + a.shape[1:],
                "swap": lambda a: a.shape}
        self.out_shape = [jax.ShapeDtypeStruct(lead[k](a), a.dtype) for k, a in zip(self.kinds, self.arrays)]
        self.scratch = [pltpu.SemaphoreType.DMA((3 * self.n,)), pltpu.SemaphoreType.DMA((3 * self.n,)),
                        pltpu.SemaphoreType.DMA((self.n,))]

    def copies(self, ins, outs, send_sems, recv_sems, local_sems):
        x, y, c = _coords()
        me = 2 * x + y
        starts, waits = [], []
        for t, kind in enumerate(self.kinds):
            if kind == "swap":
                cp = pltpu.make_async_remote_copy(
                    src_ref=ins[t], dst_ref=outs[t], send_sem=send_sems.at[3 * t], recv_sem=recv_sems.at[3 * t],
                    device_id=(x, y, 1 - c), device_id_type=MESH)
                starts.append(cp.start)
                waits += [cp.wait_send, cp.wait_recv]
                continue
            if kind == "gather":
                own = pltpu.make_async_copy(ins[t], outs[t].at[me], local_sems.at[t])
                starts.append(own.start)
                waits.append(own.wait)
            for k, (peer, peer_chip) in enumerate(_chip_peers(x, y, c)):
                sems = dict(send_sem=send_sems.at[3 * t + k], recv_sem=recv_sems.at[3 * t + k],
                            device_id=peer, device_id_type=MESH)
                if kind == "gather":
                    send = pltpu.make_async_remote_copy(src_ref=ins[t], dst_ref=outs[t].at[me], **sems)
                    recv = pltpu.make_async_remote_copy(src_ref=ins[t], dst_ref=outs[t].at[peer_chip], **sems)
                else:
                    send = pltpu.make_async_remote_copy(src_ref=ins[t].at[peer_chip], dst_ref=outs[t].at[k], **sems)
                    recv = send
                starts.append(send.start)
                waits += [send.wait_send, recv.wait_recv]
        return starts, waits

    def run(self, name):
        n = self.n

        def body(*refs):
            starts, waits = self.copies(refs[:n], refs[n:2 * n], *refs[2 * n:])
            for f in starts + waits:
                f()

        return pl.pallas_call(body, in_specs=[HBM_SPEC] * n, out_specs=[HBM_SPEC] * n, out_shape=self.out_shape,
                              scratch_shapes=self.scratch, name=name)(*self.arrays)


def _call_with_exchange(body, exchange, *, grid, in_specs, out_specs, out_shape, scratch_shapes, name, args,
                        sequential=False):
    if exchange is None:
        first_axis = "arbitrary" if sequential else "parallel"
        outs = pl.pallas_call(body, grid=grid, in_specs=in_specs, out_specs=out_specs, out_shape=out_shape,
                              scratch_shapes=scratch_shapes,
                              compiler_params=_params((first_axis,) + ("arbitrary",) * (len(grid) - 1)),
                              name=name)(*args)
        return outs, []
    n_in, n_out, n_scr, n = len(in_specs), len(out_specs), len(scratch_shapes), exchange.n

    def wrapped(*refs):
        ins, ex_in = refs[:n_in], refs[n_in:n_in + n]
        outs = refs[n_in + n:n_in + n + n_out]
        ex_out = refs[n_in + n + n_out:n_in + 2 * n + n_out]
        scr = refs[n_in + 2 * n + n_out:n_in + 2 * n + n_out + n_scr]
        sems = refs[n_in + 2 * n + n_out + n_scr:]
        ids = [pl.program_id(a) for a in range(len(grid))]
        first = functools.reduce(jnp.logical_and, [i == 0 for i in ids])
        last = functools.reduce(jnp.logical_and, [i == g - 1 for i, g in zip(ids, grid)])

        @pl.when(first)
        def _():
            for f in exchange.copies(ex_in, ex_out, *sems)[0]:
                f()

        body(*ins, *outs, *scr)

        @pl.when(last)
        def _():
            for f in exchange.copies(ex_in, ex_out, *sems)[1]:
                f()

    res = pl.pallas_call(
        wrapped, grid=grid, in_specs=list(in_specs) + [HBM_SPEC] * n, out_specs=list(out_specs) + [HBM_SPEC] * n,
        out_shape=list(out_shape) + exchange.out_shape, scratch_shapes=list(scratch_shapes) + exchange.scratch,
        compiler_params=_params(("arbitrary",) * len(grid)), name=name + "_" + exchange.kind,
    )(*args, *exchange.arrays)
    return res[:n_out], res[n_out:]


def _gather_devices(a):
    def body(in_ref, out_ref, send_sems, recv_sems, local_sem):
        x, y, c = _coords()
        me = 4 * x + 2 * y + c
        own = pltpu.make_async_copy(in_ref, out_ref.at[me], local_sem)
        own.start()
        waits = [own.wait]
        for k in range(1, N_DEVICES):
            px = 1 - x if k & 4 else x
            py = 1 - y if k & 2 else y
            pc = 1 - c if k & 1 else c
            peer = (px, py, pc)
            send = pltpu.make_async_remote_copy(
                src_ref=in_ref, dst_ref=out_ref.at[me], send_sem=send_sems.at[k - 1], recv_sem=recv_sems.at[k - 1],
                device_id=peer, device_id_type=MESH)
            send.start()
            recv = pltpu.make_async_remote_copy(
                src_ref=in_ref, dst_ref=out_ref.at[4 * px + 2 * py + pc], send_sem=send_sems.at[k - 1],
                recv_sem=recv_sems.at[k - 1], device_id=peer, device_id_type=MESH)
            waits += [send.wait_send, recv.wait_recv]
        for w in waits:
            w()

    return pl.pallas_call(
        body, in_specs=[HBM_SPEC], out_specs=HBM_SPEC,
        out_shape=jax.ShapeDtypeStruct((N_DEVICES,) + a.shape, a.dtype),
        scratch_shapes=[pltpu.SemaphoreType.DMA((N_DEVICES - 1,)), pltpu.SemaphoreType.DMA((N_DEVICES - 1,)),
                        pltpu.SemaphoreType.DMA],
        name="gather_devices",
    )(a)


def _mlp_grad_epilogue(r, u):
    return r * (2.0 * jnp.maximum(u, 0.0))


def kernel(x, norm_gains, sb_w_qkv, sb_q_gain, sb_k_gain, sb_w_o, hg_w_in, hg_lb_logits, hg_norm_gain, hg_w_o, mlp_w1, mlp_w2, loss_target, m_norm_gains, m_sb_w_qkv, m_sb_q_gain, m_sb_k_gain, m_sb_w_o, m_hg_w_in, m_hg_lb_logits, m_hg_norm_gain, m_hg_w_o, m_mlp_w1, m_mlp_w2, v_norm_gains, v_sb_w_qkv, v_sb_q_gain, v_sb_k_gain, v_sb_w_o, v_hg_w_in, v_hg_lb_logits, v_hg_norm_gain, v_hg_w_o, v_mlp_w1, v_mlp_w2):
    depth = norm_gains.shape[0]
    n_sb, n_hg = sb_w_qkv.shape[0], hg_w_in.shape[0]
    xs, tgt = x[0], loss_target[0]
    s, d = xs.shape
    dq = d // N_CHIPS
    cx, cy, cc = _coords()
    chip = 2 * cx + cy
    chip_arr = jnp.reshape(chip, (1,)).astype(jnp.int32)

    def mixer_weights(layer):
        j = layer // 2
        return (sb_w_qkv[j], sb_w_o[j]) if layer % 2 == 0 else (hg_w_in[j], hg_w_o[j])

    w_in_g, ng_g, lbl_g = _ChipExchange(
        "gather", [mixer_weights(0)[0].astype(BF16), norm_gains, hg_lb_logits]).run("gather_first")
    gains = jnp.transpose(ng_g, (1, 2, 0, 3)).reshape(depth, 2, d)
    logits = jnp.transpose(lbl_g, (1, 0, 2)).reshape(n_hg, d)
    lbs, lb_p = _lb_fwd(logits)
    qg_rows = [jnp.tile(sb_q_gain[j], d // SB_HEAD_DIM)[None] for j in range(n_sb)]
    kg_rows = [jnp.tile(sb_k_gain[j], d // SB_HEAD_DIM)[None] for j in range(n_sb)]

    saved, wts = [], []
    xc = xs
    for layer in range(depth):
        j = layer // 2
        ahead = [mixer_weights(layer)[1], mlp_w1[layer], mlp_w2[layer]]
        if layer + 1 < depth:
            ahead.append(mixer_weights(layer + 1)[0])
        gather = _ChipExchange("gather", [a.astype(BF16) for a in ahead])
        h1 = _rmsnorm_fwd(xc, gains[layer, 0][None])
        if layer % 2 == 0:
            qkv = _mm_fwd_cols(h1, w_in_g, name="sb_qkv")
            qn, kn, vb = _qk_norm_fwd(qkv, qg_rows[j], kg_rows[j])
            o, moved = _sb_attn_fwd(qn, kn, vb, gather)
            x_mid = _mm_fwd_rows(o, moved[0], residual=xc, name="sb_out")
            mix = (qkv, qn, kn, vb, o)
        else:
            proj = _mm_fwd_cols(h1, w_in_g, name="hg_in")
            (y, o, states), moved = _hg_fwd(proj, lbs[j][None], hg_norm_gain[j][None], gather)
            x_mid = _mm_fwd_rows(y, moved[0], residual=xc, name="hg_out")
            mix = (proj, y, o, states)
        w_out_g, w1_g, w2_g = moved[:3]
        h2 = _rmsnorm_fwd(x_mid, gains[layer, 1][None])
        u = _mm_fwd_cols(h2, w1_g, name="mlp_up")
        x_out = _mm_fwd_rows(u, w2_g, residual=x_mid, a_fn=_relu2, name="mlp_down")
        saved.append((xc, h1, mix, x_mid, h2, u))
        wts.append((w_in_g, w_out_g, w1_g, w2_g))
        w_in_g = moved[3] if layer + 1 < depth else None
        xc = x_out

    sq, dx = _loss_head(xc, tgt)
    loss = lax.psum(jnp.sum(sq) * (0.5 / d), ("x", "y", "c"))

    dgains = [[None, None] for _ in range(depth)]
    dqg, dkg = [None] * n_sb, [None] * n_sb
    dhgain, dlb = [None] * n_hg, [None] * n_hg
    grads, received, pending = {}, {}, []

    def ready(key, parts):
        grads[key] = parts
        pending.append(key)

    def scatter_of(keys):
        return _ChipExchange("scatter", [grads[k] for k in keys]) if keys else None

    def sent(keys, moved):
        for k, r in zip(keys, moved):
            received[k] = r
            pending.remove(k)

    def chip_sum(kind, layers):
        total = jnp.zeros((len(layers),) + grads[kind, layers[0]].shape[1:], F32)
        for index, l in enumerate(layers):
            total = _sum_slots(grads[kind, l], received[kind, l], chip_arr, total, index)
        return total

    sb_layers, hg_layers = range(0, depth, 2), range(1, depth, 2)
    tensors = [("in", sb_layers), ("out", sb_layers), ("in", hg_layers), ("out", hg_layers),
               ("w1", range(depth)), ("w2", range(depth))]

    for layer in reversed(range(depth)):
        j = layer // 2
        x_in, h1, mix, x_mid, h2, u = saved[layer]
        w_in_g, w_out_g, w1_g, w2_g = wts[layer]
        du = _mm_bwd_rows(dx, w2_g, name="mlp_down_dx", out_dtype=BF16, epi_fn=_mlp_grad_epilogue, epi_args=(u,))
        ready(("w2", layer), _mm_dw_rows(u, dx, a_fn=_relu2, name="mlp_down_dw"))
        ready(("w1", layer), _mm_dw_cols(h2, du, name="mlp_up_dw"))
        dx, dgains[layer][1] = _mm_bwd_cols_norm(du, w1_g, x_mid, gains[layer, 1][None], dx, name="mlp_up_dx")
        if layer % 2 == 0:
            qkv, qn, kn, vb, o = mix
            do = _mm_bwd_rows(dx, w_out_g, name="sb_out_dx")
            ready(("out", layer), _mm_dw_rows(o, dx, name="sb_out_dw"))
            keys = list(pending)
            (dqn, dkn, dv), moved = _sb_attn_bwd(qn, kn, vb, do, scatter_of(keys))
            sent(keys, moved)
            d_in, dqg[j], dkg[j] = _qk_norm_bwd(qkv, qg_rows[j], kg_rows[j], dqn, dkn, dv)
            ready(("in", layer), _mm_dw_cols(h1, d_in, name="sb_qkv_dw"))
            dx_name = "sb_qkv_dx"
        else:
            proj, y, o, states = mix
            dy = _mm_bwd_rows(dx, w_out_g, name="hg_out_dx")
            ready(("out", layer), _mm_dw_rows(y, dx, name="hg_out_dw"))
            keys = list(pending)
            (d_in, dlb[j], dhgain[j]), moved = _hg_bwd(
                proj, lbs[j][None], hg_norm_gain[j][None], o, states, dy, scatter_of(keys))
            sent(keys, moved)
            ready(("in", layer), _mm_dw_cols(h1, d_in, name="hg_in_dw"))
            dx_name = "hg_in_dx"
        if layer > 0:
            dx, dgains[layer][0] = _mm_bwd_cols_norm(d_in, w_in_g, x_in, gains[layer, 0][None], dx, name=dx_name)
        else:
            keys = list(pending)
            early_sums = [chip_sum(*t) for t in tensors[1:]]
            (dx, dgains[layer][0]), moved = _mm_bwd_cols_norm(
                d_in, w_in_g, x_in, gains[layer, 0][None], dx, name=dx_name,
                exchange=scatter_of(keys).also("swap", early_sums))
            sent(keys, moved[:len(keys)])
            early_other = moved[len(keys):]
    grad_x = dx[None]
    dlogits = _lb_bwd(lb_p, jnp.concatenate(dlb, axis=0))

    big_w = [sb_w_qkv, sb_w_o, hg_w_in, hg_w_o, mlp_w1, mlp_w2]
    big_m = [m_sb_w_qkv, m_sb_w_o, m_hg_w_in, m_hg_w_o, m_mlp_w1, m_mlp_w2]
    big_v = [v_sb_w_qkv, v_sb_w_o, v_hg_w_in, v_hg_w_o, v_mlp_w1, v_mlp_w2]
    late_sum = chip_sum(*tensors[0])
    big_last, late_other = _adamw(big_w[-1], big_m[-1], big_v[-1], [early_sums[-1], early_other[-1]],
                                  _ChipExchange("swap", [late_sum]))
    big = [_adamw(big_w[0], big_m[0], big_v[0], [late_sum, late_other[0]])]
    big += [_adamw(w, m, v, [a, b]) for w, m, v, a, b in
            zip(big_w[1:-1], big_m[1:-1], big_v[1:-1], early_sums[:-1], early_other[:-1])]
    big.append(big_last)

    pieces = [r for pair in dgains for r in pair] + [dlogits] + dqg + dkg + dhgain
    small = _sum_devices(_gather_devices(_pack_rows(pieces)))
    my_cols = lambda a: lax.dynamic_slice_in_dim(a, chip * dq, dq, axis=1)
    fold = lambda rows, width: jnp.sum(rows.reshape(rows.shape[0], -1, width), axis=1)
    base = 2 * depth + n_hg
    g_ng = my_cols(small[0:2 * depth]).reshape(norm_gains.shape)
    g_lbl = my_cols(small[2 * depth:base])
    g_qg = fold(small[base:base + n_sb], SB_HEAD_DIM)
    g_kg = fold(small[base + n_sb:base + 2 * n_sb], SB_HEAD_DIM)
    g_hgn = fold(small[base + 2 * n_sb:base + 2 * n_sb + n_hg], HG_HEAD_DIM)
    r_ng = _adamw(norm_gains, m_norm_gains, v_norm_gains, [g_ng])
    r_qg = _adamw(sb_q_gain, m_sb_q_gain, v_sb_q_gain, [g_qg])
    r_kg = _adamw(sb_k_gain, m_sb_k_gain, v_sb_k_gain, [g_kg])
    r_lbl = _adamw(hg_lb_logits, m_hg_lb_logits, v_hg_lb_logits, [g_lbl])
    r_hgn = _adamw(hg_norm_gain, m_hg_norm_gain, v_hg_norm_gain, [g_hgn])

    per_weight = [r_ng, big[0], r_qg, r_kg, big[1], big[2], r_lbl, r_hgn, big[3], big[4], big[5]]
    outs = [loss, grad_x]
    for field in range(4):
        outs += [r[field] for r in per_weight]
    return tuple(outs)
```

```python
import functools
import math

import numpy as np
import jax
import jax.numpy as jnp
from jax import lax
from jax.experimental import pallas as pl
from jax.experimental.pallas import tpu as pltpu

F32 = jnp.float32
BF16 = jnp.bfloat16
GRAD_SLOT_DTYPE = jnp.bfloat16

NORM_EPS = 1e-6
SB_HEAD_DIM = 64
HG_HEAD_DIM = 128
HG_CHUNK = 128
LANES = 128
VMEM_LIMIT_BYTES = 56 * 2 ** 20
N_CHIPS = 4
N_DEVICES = 8

ADAM_LR = 0.001
ADAM_B1 = 0.9
ADAM_B2 = 0.999
ADAM_EPS = 1e-08
ADAM_WD = 0.01
ADAM_STEP = 10

MESH = pl.DeviceIdType.MESH
HBM_SPEC = pl.BlockSpec(memory_space=pltpu.HBM)

NN = (((1,), (0,)), ((), ()))
NT = (((1,), (1,)), ((), ()))
TN = (((0,), (0,)), ((), ()))


def _params(sem=None):
    return pltpu.CompilerParams(dimension_semantics=sem, vmem_limit_bytes=VMEM_LIMIT_BYTES)


def _pick(dim, pref):
    for t in (1024, 768, 512, 384, 256, 128, 64, 32, 16, 8):
        if t <= pref and dim % t == 0:
            return t
    return dim


def _dot(a, b, dims=NN):
    return lax.dot_general(a, b, dims, preferred_element_type=F32)


def _sigmoid(x):
    e = jnp.exp(-jnp.abs(x))
    return jnp.where(x >= 0, 1.0, e) / (1.0 + e)


def _matmul(a, b, *, mode, grid, a_block, a_map, b_block, b_map, o_block, o_map, out_shape, out_dtype, name,
            a_fn=None, epi_fn=None, epi_args=(), epi_row_args=(), col_sums=False, exchange=None):
    nk = grid[2]
    dims = {"nn": NN, "nt": NT, "tn": TN}[mode]
    n_epi = len(epi_args) + len(epi_row_args)
    n_out = 2 if col_sums else 1
    tn = o_block[-1]
    assert not col_sums or grid[1] == 1, "the column sums stay resident only with one tile along N"

    def body(a_ref, b_ref, *rest):
        epi_refs = rest[:n_epi]
        o_ref = rest[n_epi]
        kk = pl.program_id(2)

        def emit(r):
            if epi_fn is not None:
                r = epi_fn(r, *[e[...] for e in epi_refs])
            if col_sums:
                r, row = r
                sums_ref = rest[n_epi + 1]
                first = pl.program_id(0) == 0

                @pl.when(first)
                def _():
                    sums_ref[...] = row

                @pl.when(jnp.logical_not(first))
                def _():
                    sums_ref[...] += row
            o_ref[...] = r.astype(o_ref.dtype)

        av = a_ref[...]
        if a_fn is not None:
            av = a_fn(av)
        part = _dot(av.astype(BF16), b_ref[...].astype(BF16), dims)
        if nk == 1:
            emit(part)
            return
        acc_ref = rest[n_epi + n_out]

        @pl.when(kk == 0)
        def _():
            acc_ref[...] = part

        @pl.when(kk > 0)
        def _():
            acc_ref[...] += part

        @pl.when(kk == nk - 1)
        def _():
            emit(acc_ref[...])

    acc_shape = tuple(d for d in o_block if d is not None)
    row_spec = pl.BlockSpec((1, tn), lambda i, j, kk: (0, j))
    in_specs = [pl.BlockSpec(a_block, a_map), pl.BlockSpec(b_block, b_map)]
    in_specs += [pl.BlockSpec(o_block, o_map) for _ in epi_args] + [row_spec for _ in epi_row_args]
    out_specs, out_shapes = [pl.BlockSpec(o_block, o_map)], [jax.ShapeDtypeStruct(out_shape, out_dtype)]
    if col_sums:
        out_specs, out_shapes = out_specs + [row_spec], out_shapes + [jax.ShapeDtypeStruct((1, out_shape[-1]), F32)]
    outs, moved = _call_with_exchange(
        body, exchange, grid=grid, in_specs=in_specs, out_specs=out_specs, out_shape=out_shapes,
        scratch_shapes=[pltpu.VMEM(acc_shape, F32)] if nk > 1 else [], name=name,
        args=(a, b, *epi_args, *epi_row_args), sequential=col_sums)
    result = tuple(outs) if col_sums else outs[0]
    return result if exchange is None else (result, moved)


def _relu2(u):
    r = jnp.maximum(u, 0.0)
    return r * r


def _add(r, res):
    return r + res


def _mm_fwd_cols(a, wg, *, name):
    s, k = a.shape
    ncs = wg.shape[2]
    tm, tk, tn = _pick(s, 1024), _pick(k, 1024), _pick(ncs, 1024)
    npb = ncs // tn
    return _matmul(a, wg, mode="nn", grid=(s // tm, N_CHIPS * npb, k // tk),
                   a_block=(tm, tk), a_map=lambda i, j, kk: (i, kk),
                   b_block=(None, tk, tn), b_map=lambda i, j, kk: (j // npb, kk, j % npb),
                   o_block=(tm, tn), o_map=lambda i, j, kk: (i, j),
                   out_shape=(s, N_CHIPS * ncs), out_dtype=F32, name=name)


def _rows_joined(wg):
    assert wg.shape[1] % 16 == 0, "joining the leading axes must not cross a tile of 16 rows"
    return wg.reshape(wg.shape[0] * wg.shape[1], wg.shape[2])


def _mm_fwd_rows(a, wg, *, residual, name, a_fn=None):
    s = a.shape[0]
    w = _rows_joined(wg)
    k, n = w.shape
    tm, tk, tn = _pick(s, 1024), _pick(k, 1024), _pick(n, 1024)
    return _matmul(a, w, mode="nn", grid=(s // tm, n // tn, k // tk),
                   a_block=(tm, tk), a_map=lambda i, j, kk: (i, kk),
                   b_block=(tk, tn), b_map=lambda i, j, kk: (kk, j),
                   o_block=(tm, tn), o_map=lambda i, j, kk: (i, j),
                   out_shape=(s, n), out_dtype=F32, name=name, a_fn=a_fn, epi_fn=_add, epi_args=(residual,))


def _rmsnorm_grad(dh, x, dx_res, gain):
    r = lax.rsqrt(jnp.mean(x * x, axis=-1, keepdims=True) + NORM_EPS)
    xhat = x * r
    dxhat = dh * gain
    dx = r * (dxhat - xhat * jnp.mean(dxhat * xhat, axis=-1, keepdims=True))
    return dx_res + dx, jnp.sum(dh * xhat, axis=0, keepdims=True)


def _mm_bwd_cols_norm(dy, wg, x, gain_row, dx_res, *, name, exchange=None):
    by_slot = dy.ndim == 3
    s = dy.shape[1] if by_slot else dy.shape[0]
    kw, ncs = wg.shape[1], wg.shape[2]
    tm, tk = _pick(s, 512), _pick(ncs, 1024)
    kpb = ncs // tk
    a_block, a_map = ((None, tm, tk), lambda i, j, kk: (kk // kpb, i, kk % kpb)) if by_slot else (
        (tm, tk), lambda i, j, kk: (i, kk))
    return _matmul(dy, wg, mode="nt", grid=(s // tm, 1, N_CHIPS * kpb),
                   a_block=a_block, a_map=a_map,
                   b_block=(None, kw, tk), b_map=lambda i, j, kk: (kk // kpb, j, kk % kpb),
                   o_block=(tm, kw), o_map=lambda i, j, kk: (i, j),
                   out_shape=(s, kw), out_dtype=F32, name=name, exchange=exchange,
                   epi_fn=_rmsnorm_grad, epi_args=(x, dx_res), epi_row_args=(gain_row,), col_sums=True)


def _mm_bwd_rows(dy, wg, *, name, out_dtype=F32, epi_fn=None, epi_args=()):
    s, n = dy.shape
    w = _rows_joined(wg)
    rows = w.shape[0]
    tm, tn, tk = _pick(s, 1024), _pick(rows, 1024), _pick(n, 1024)
    return _matmul(dy, w, mode="nt", grid=(s // tm, rows // tn, n // tk),
                   a_block=(tm, tk), a_map=lambda i, j, kk: (i, kk),
                   b_block=(tn, tk), b_map=lambda i, j, kk: (j, kk),
                   o_block=(tm, tn), o_map=lambda i, j, kk: (i, j),
                   out_shape=(s, rows), out_dtype=out_dtype, name=name, epi_fn=epi_fn, epi_args=epi_args)


def _mm_dw_cols(xa, dy, *, name):
    s, kx = xa.shape
    by_slot = dy.ndim == 3
    ncs = dy.shape[2] if by_slot else dy.shape[1] // N_CHIPS
    tm, tn, tk = _pick(kx, 1024), _pick(ncs, 1024), _pick(s, 1024)
    npb = ncs // tn
    b_block, b_map = ((None, tk, tn), lambda i, j, kk: (j // npb, kk, j % npb)) if by_slot else (
        (tk, tn), lambda i, j, kk: (kk, j))
    return _matmul(xa, dy, mode="tn", grid=(kx // tm, N_CHIPS * npb, s // tk),
                   a_block=(tk, tm), a_map=lambda i, j, kk: (kk, i),
                   b_block=b_block, b_map=b_map,
                   o_block=(None, tm, tn), o_map=lambda i, j, kk: (j // npb, i, j % npb),
                   out_shape=(N_CHIPS, kx, ncs), out_dtype=GRAD_SLOT_DTYPE, name=name)


def _mm_dw_rows(xa, dy, *, name, a_fn=None):
    s, n = dy.shape
    rows = xa.shape[1]
    assert (rows // N_CHIPS) % 16 == 0, "splitting the rows into slots must not cut a tile of 16 rows"
    tm, tn, tk = _pick(rows, 1024), _pick(n, 1024), _pick(s, 1024)
    dw = _matmul(xa, dy, mode="tn", grid=(rows // tm, n // tn, s // tk),
                 a_block=(tk, tm), a_map=lambda i, j, kk: (kk, i),
                 b_block=(tk, tn), b_map=lambda i, j, kk: (kk, j),
                 o_block=(tm, tn), o_map=lambda i, j, kk: (i, j),
                 out_shape=(rows, n), out_dtype=GRAD_SLOT_DTYPE, name=name, a_fn=a_fn)
    return dw.reshape(N_CHIPS, rows // N_CHIPS, n)


def _rmsnorm_fwd(x, gain_row):
    s, d = x.shape
    ts = _pick(s, 512)

    def body(x_ref, g_ref, h_ref):
        xv = x_ref[...]
        r = lax.rsqrt(jnp.mean(xv * xv, axis=-1, keepdims=True) + NORM_EPS)
        h_ref[...] = (xv * r * g_ref[...]).astype(h_ref.dtype)

    return pl.pallas_call(
        body, grid=(s // ts,),
        in_specs=[pl.BlockSpec((ts, d), lambda i: (i, 0)), pl.BlockSpec((1, d), lambda i: (0, 0))],
        out_specs=pl.BlockSpec((ts, d), lambda i: (i, 0)),
        out_shape=jax.ShapeDtypeStruct((s, d), BF16),
        compiler_params=_params(("parallel",)), name="rmsnorm_fwd",
    )(x, gain_row)


def _loss_head(y, target):
    s, d = y.shape
    ts = _pick(s, 512)

    def body(y_ref, t_ref, sq_ref, dy_ref):
        i = pl.program_id(0)
        err = y_ref[...] - t_ref[...]
        dy_ref[...] = err / d
        part = jnp.sum(err * err, axis=0, keepdims=True)

        @pl.when(i == 0)
        def _():
            sq_ref[...] = part

        @pl.when(i > 0)
        def _():
            sq_ref[...] += part

    return pl.pallas_call(
        body, grid=(s // ts,),
        in_specs=[pl.BlockSpec((ts, d), lambda i: (i, 0)), pl.BlockSpec((ts, d), lambda i: (i, 0))],
        out_specs=[pl.BlockSpec((1, d), lambda i: (0, 0)), pl.BlockSpec((ts, d), lambda i: (i, 0))],
        out_shape=[jax.ShapeDtypeStruct((1, d), F32), jax.ShapeDtypeStruct((s, d), F32)],
        compiler_params=_params(("arbitrary",)), name="loss_head",
    )(y, target)


def _pair_mean(val, low_half):
    s0 = jnp.sum(jnp.where(low_half, val, 0.0), axis=-1, keepdims=True)
    s1 = jnp.sum(jnp.where(low_half, 0.0, val), axis=-1, keepdims=True)
    return jnp.where(low_half, s0, s1) * (1.0 / SB_HEAD_DIM)


def _qk_norm_fwd(qkv, qgain_row, kgain_row):
    s, d3 = qkv.shape
    d = d3 // 3
    ts = _pick(s, 512)
    groups = d // LANES

    def body(q_ref, k_ref, v_ref, qg_ref, kg_ref, qn_ref, kn_ref, vb_ref):
        low_half = lax.broadcasted_iota(jnp.int32, (ts, LANES), 1) < SB_HEAD_DIM
        for src, gain, dst in ((q_ref, qg_ref, qn_ref), (k_ref, kg_ref, kn_ref)):
            for p in range(groups):
                cols = slice(p * LANES, (p + 1) * LANES)
                xp = src[:, cols]
                r = lax.rsqrt(_pair_mean(xp * xp, low_half) + NORM_EPS)
                dst[:, cols] = (xp * r * gain[:, cols]).astype(dst.dtype)
        vb_ref[...] = v_ref[...].astype(vb_ref.dtype)

    tok = lambda c: pl.BlockSpec((ts, d), lambda i: (i, c))
    row = pl.BlockSpec((1, d), lambda i: (0, 0))
    return pl.pallas_call(
        body, grid=(s // ts,),
        in_specs=[tok(0), tok(1), tok(2), row, row],
        out_specs=[tok(0), tok(0), tok(0)],
        out_shape=[jax.ShapeDtypeStruct((s, d), BF16)] * 3,
        compiler_params=_params(("parallel",)), name="qk_norm_fwd",
    )(qkv, qkv, qkv, qgain_row, kgain_row)


def _qk_norm_bwd(qkv, qgain_row, kgain_row, dqn, dkn, dv):
    s, d3 = qkv.shape
    d = d3 // 3
    ts = _pick(s, 512)
    groups = d // LANES

    def body(q_ref, k_ref, qg_ref, kg_ref, dqn_ref, dkn_ref, dv_ref, dqkv_ref, dqg_ref, dkg_ref):
        i = pl.program_id(0)
        low_half = lax.broadcasted_iota(jnp.int32, (ts, LANES), 1) < SB_HEAD_DIM
        for which, (src, gain, dsrc, dgain) in enumerate(((q_ref, qg_ref, dqn_ref, dqg_ref),
                                                          (k_ref, kg_ref, dkn_ref, dkg_ref))):
            for p in range(groups):
                cols = slice(p * LANES, (p + 1) * LANES)
                xp = src[:, cols]
                r = lax.rsqrt(_pair_mean(xp * xp, low_half) + NORM_EPS)
                xhat = xp * r
                dy = dsrc[:, cols]
                dxhat = dy * gain[:, cols]
                dx = r * (dxhat - xhat * _pair_mean(dxhat * xhat, low_half))
                dqkv_ref[:, which * d + p * LANES: which * d + (p + 1) * LANES] = dx.astype(dqkv_ref.dtype)
                part = jnp.sum(dy * xhat, axis=0, keepdims=True)

                @pl.when(i == 0)
                def _():
                    dgain[:, cols] = part

                @pl.when(i > 0)
                def _():
                    dgain[:, cols] += part
        dqkv_ref[:, 2 * d:] = dv_ref[...].astype(dqkv_ref.dtype)

    tok = lambda c: pl.BlockSpec((ts, d), lambda i: (i, c))
    row = pl.BlockSpec((1, d), lambda i: (0, 0))
    return pl.pallas_call(
        body, grid=(s // ts,),
        in_specs=[tok(0), tok(1), row, row, tok(0), tok(0), tok(0)],
        out_specs=[pl.BlockSpec((ts, d3), lambda i: (i, 0)), row, row],
        out_shape=[jax.ShapeDtypeStruct((s, d3), BF16), jax.ShapeDtypeStruct((1, d), F32),
                   jax.ShapeDtypeStruct((1, d), F32)],
        compiler_params=_params(("arbitrary",)), name="qk_norm_bwd",
    )(qkv, qkv, qgain_row, kgain_row, dqn, dkn, dv)


def _split2(x):
    hi = x.astype(BF16)
    lo = (x - hi.astype(F32)).astype(BF16)
    return hi, lo


SB_TK = 128


def _sb_consts(tk):
    j = np.arange(tk)
    ones = np.ones((tk, tk), np.float32)
    out = []
    for tri in ((j[:, None] >= j[None, :]), (j[:, None] <= j[None, :])):
        half = np.concatenate([tri.astype(np.float32), ones], axis=1)
        out.append(jnp.asarray(np.concatenate([half, half], axis=0), BF16))
    return out


def _head_stack(blk, low_half):
    f = blk.astype(F32)
    return jnp.concatenate([jnp.where(low_half, f, 0.0), jnp.where(low_half, 0.0, f)], axis=0).astype(BF16)


def _sb_tile_sums(z, valid, tri2):
    e = jnp.exp(-jnp.abs(z))
    lstay = jnp.minimum(-z, 0.0) - jnp.log(1.0 + e)
    if valid is not None:
        lstay = jnp.where(valid, lstay, 0.0)
    hi, lo = _split2(lstay)
    return e, _dot(jnp.concatenate([hi, lo], axis=1), tri2)


def _sb_weights(z, c2, valid, run):
    w = jnp.exp(z + c2[:, :SB_TK] + run)
    return w if valid is None else jnp.where(valid, w, 0.0)


EXP_IS_ZERO_BELOW = -110.0


def _max_row_norm(x):
    f = x.astype(F32)
    return jnp.sqrt(jnp.max(jnp.sum(f * f, axis=-1, keepdims=True)))


def _sb_score_bound(qs, kmax_ref):
    return _max_row_norm(qs) * jnp.max(kmax_ref[...]) * 1.01 + 1.0


def _sb_rest_is_zero(run_ref, bound):
    return jnp.max(jnp.maximum(run_ref[0], run_ref[1])) + bound < EXP_IS_ZERO_BELOW


def _sb_attn_fwd(qn, kn, vb, exchange=None):
    s, d = qn.shape
    tk = SB_TK
    tq = _pick(s, 256)
    nq, ndiag = s // tq, tq // tk
    assert tq % (2 * tk) == 0, "tiles below the diagonal are taken two at a time"
    npairs = d // LANES
    scale = 1.0 / math.sqrt(SB_HEAD_DIM)
    tri_ge2, _ = _sb_consts(tk)

    def body(q_ref, k_ref, v_ref, tri_ref, o_ref, acc_ref, run_ref, kmax_ref):
        qi = pl.program_id(1)

        @pl.when(qi == 0)
        def _():
            kmax_ref[...] = jnp.full(kmax_ref.shape, _max_row_norm(k_ref[...]), F32)

        low_half = lax.broadcasted_iota(jnp.int32, (tk, LANES), 1) < SB_HEAD_DIM
        row = lax.broadcasted_iota(jnp.int32, (tq, tk), 0)
        col = lax.broadcasted_iota(jnp.int32, (tq, tk), 1)
        qs = (q_ref[...].astype(F32) * scale).astype(BF16)
        bound = _sb_score_bound(qs, kmax_ref)
        acc_ref[...] = jnp.zeros_like(acc_ref)
        run_ref[...] = jnp.zeros_like(run_ref)
        n_full = qi * ndiag

        def sums(kb, dd):
            koff = pl.multiple_of(kb * tk, tk)
            kcat = _head_stack(k_ref[pl.ds(koff, tk), :], low_half)
            vcat = _head_stack(v_ref[pl.ds(koff, tk), :], low_half)
            z2 = _dot(qs, kcat, NT)
            valid = None if dd is None else row > col + dd * tk
            zs = [z2[:, h * tk:(h + 1) * tk] for h in range(2)]
            return zs, [_sb_tile_sums(z, valid, tri_ref[...])[1] for z in zs], valid, vcat

        def finish(zs, c2s, valid, vcat):
            ws = []
            for h in range(2):
                ws.append(_sb_weights(zs[h], c2s[h], valid, run_ref[h]).astype(BF16))
                run_ref[h] += c2s[h][:, tk:]
            acc_ref[...] += _dot(jnp.concatenate(ws, axis=1), vcat)

        def first_tiles(below):
            pres = [sums(n_full + dd, dd) for dd in reversed(range(ndiag))]
            pres += [sums(n_full - 1 - n, None) for n in range(below)]
            for pre in pres:
                finish(*pre)

        @pl.when(qi == 0)
        def _():
            first_tiles(0)

        @pl.when(qi > 0)
        def _():
            first_tiles(2)

        def two_tiles(carry):
            it, _ = carry
            kb = n_full - 1 - 2 * it
            first, second = sums(kb, None), sums(kb - 1, None)
            finish(*first)
            finish(*second)
            return it + 1, _sb_rest_is_zero(run_ref, bound)

        lax.while_loop(lambda c: jnp.logical_and(c[0] < n_full // 2, jnp.logical_not(c[1])), two_tiles,
                       (jnp.minimum(qi, 1), _sb_rest_is_zero(run_ref, bound)))
        o_ref[...] = acc_ref[...]

    blk = pl.BlockSpec((tq, LANES), lambda p, i: (i, p))
    full = pl.BlockSpec((s, LANES), lambda p, i: (0, p))
    (o,), moved = _call_with_exchange(
        body, exchange, grid=(npairs, nq),
        in_specs=[blk, full, full, pl.BlockSpec((2 * tk, 2 * tk), lambda p, i: (0, 0))],
        out_specs=[blk], out_shape=[jax.ShapeDtypeStruct((s, d), F32)],
        scratch_shapes=[pltpu.VMEM((tq, LANES), F32), pltpu.VMEM((2, tq, tk), F32), pltpu.VMEM((8, LANES), F32)],
        name="sb_attn_fwd", args=(qn, kn, vb, tri_ge2))
    return o, moved


def _sb_attn_bwd(qn, kn, vb, do, exchange=None):
    s, d = qn.shape
    tk = SB_TK
    tq = _pick(s, 256)
    nq, ndiag = s // tq, tq // tk
    assert tq % (2 * tk) == 0, "tiles below the diagonal are taken two at a time"
    npairs = d // LANES
    scale = 1.0 / math.sqrt(SB_HEAD_DIM)
    tri_ge2, tri_le2 = _sb_consts(tk)

    def body(q_ref, k_ref, v_ref, do_ref, tge_ref, tle_ref, dq_ref, dk_ref, dv_ref,
             g_cache, s_cache, run_ref, dq_acc, kmax_ref):
        qi = pl.program_id(1)

        @pl.when(qi == 0)
        def _():
            dk_ref[...] = jnp.zeros_like(dk_ref)
            dv_ref[...] = jnp.zeros_like(dv_ref)
            kmax_ref[...] = jnp.full(kmax_ref.shape, _max_row_norm(k_ref[...]), F32)

        low_half = lax.broadcasted_iota(jnp.int32, (tk, LANES), 1) < SB_HEAD_DIM
        row = lax.broadcasted_iota(jnp.int32, (tq, tk), 0)
        col = lax.broadcasted_iota(jnp.int32, (tq, tk), 1)
        qs = (q_ref[...].astype(F32) * scale).astype(BF16)
        bound = _sb_score_bound(qs, kmax_ref)
        dob = do_ref[...].astype(BF16)
        n_full = qi * ndiag

        def a_sums(kb, dd):
            koff = pl.multiple_of(kb * tk, tk)
            kcat = _head_stack(k_ref[pl.ds(koff, tk), :], low_half)
            vcat = _head_stack(v_ref[pl.ds(koff, tk), :], low_half)
            z2 = _dot(qs, kcat, NT)
            dw2 = _dot(dob, vcat, NT)
            valid = None if dd is None else row > col + dd * tk
            c2s = []
            for h in range(2):
                cols = slice(h * tk, (h + 1) * tk)
                z = z2[:, cols]
                e, c2 = _sb_tile_sums(z, valid, tge_ref[...])
                s_cache[kb, :, cols] = jnp.where(z >= 0, 1.0, e) / (1.0 + e)
                c2s.append(c2)
            return kb, koff, z2, dw2, c2s, valid

        def a_finish(kb, koff, z2, dw2, c2s, valid):
            ws = []
            for h in range(2):
                cols = slice(h * tk, (h + 1) * tk)
                w = _sb_weights(z2[:, cols], c2s[h], valid, run_ref[h])
                run_ref[h] += c2s[h][:, tk:]
                g_cache[kb, :, cols] = w * dw2[:, cols]
                ws.append(w.astype(BF16))
            dv2 = _dot(jnp.concatenate(ws, axis=1), dob, TN)
            dv_ref[pl.ds(koff, tk), :] += jnp.where(low_half, dv2[:tk], dv2[tk:])

        def b_sums(kb, dd):
            gs = [g_cache[kb, :, h * tk:(h + 1) * tk] for h in range(2)]
            p2s = [_dot(jnp.concatenate(_split2(g), axis=1), tle_ref[...]) for g in gs]
            return kb, gs, p2s, (None if dd is None else row > col + dd * tk)

        def b_finish(kb, gs, p2s, valid):
            koff = pl.multiple_of(kb * tk, tk)
            dzs = []
            for h in range(2):
                dz = gs[h] - s_cache[kb, :, h * tk:(h + 1) * tk] * (p2s[h][:, :tk] + run_ref[h])
                if valid is not None:
                    dz = jnp.where(valid, dz, 0.0)
                run_ref[h] += p2s[h][:, tk:]
                dzs.append(dz.astype(BF16))
            dzcat = jnp.concatenate(dzs, axis=1)
            dq_acc[...] += _dot(dzcat, _head_stack(k_ref[pl.ds(koff, tk), :], low_half))
            dk2 = _dot(dzcat, qs, TN)
            dk_ref[pl.ds(koff, tk), :] += jnp.where(low_half, dk2[:tk], dk2[tk:])

        run_ref[...] = jnp.zeros_like(run_ref)
        near = jnp.minimum(qi, 1)

        def a_first_tiles(below):
            pres = [a_sums(n_full + dd, dd) for dd in reversed(range(ndiag))]
            pres += [a_sums(n_full - 1 - n, None) for n in range(below)]
            for pre in pres:
                a_finish(*pre)

        @pl.when(qi == 0)
        def _():
            a_first_tiles(0)

        @pl.when(qi > 0)
        def _():
            a_first_tiles(2)

        def two_a(carry):
            it, _ = carry
            kb = n_full - 1 - 2 * it
            first, second = a_sums(kb, None), a_sums(kb - 1, None)
            a_finish(*first)
            a_finish(*second)
            return it + 1, _sb_rest_is_zero(run_ref, bound)

        trips, _ = lax.while_loop(lambda c: jnp.logical_and(c[0] < n_full // 2, jnp.logical_not(c[1])), two_a,
                                  (near, _sb_rest_is_zero(run_ref, bound)))

        run_ref[...] = jnp.zeros_like(run_ref)
        dq_acc[...] = jnp.zeros_like(dq_acc)
        kb_first = n_full - 2 * trips

        def two_b(it, carry):
            first, second = b_sums(kb_first + 2 * it, None), b_sums(kb_first + 2 * it + 1, None)
            b_finish(*first)
            b_finish(*second)
            return carry

        lax.fori_loop(0, trips - near, two_b, 0)

        def b_last_tiles(below):
            pres = [b_sums(n_full - below + n, None) for n in range(below)]
            pres += [b_sums(n_full + dd, dd) for dd in range(ndiag)]
            for pre in pres:
                b_finish(*pre)

        @pl.when(qi == 0)
        def _():
            b_last_tiles(0)

        @pl.when(qi > 0)
        def _():
            b_last_tiles(2)

        dq_ref[...] = dq_acc[...] * scale

    blk = pl.BlockSpec((tq, LANES), lambda p, i: (i, p))
    full = pl.BlockSpec((s, LANES), lambda p, i: (0, p))
    tri = pl.BlockSpec((2 * tk, 2 * tk), lambda p, i: (0, 0))
    return _call_with_exchange(
        body, exchange, grid=(npairs, nq),
        in_specs=[blk, full, full, blk, tri, tri],
        out_specs=[blk, full, full],
        out_shape=[jax.ShapeDtypeStruct((s, d), F32)] * 3,
        scratch_shapes=[pltpu.VMEM((s // tk, tq, 2 * tk), F32), pltpu.VMEM((s // tk, tq, 2 * tk), F32),
                        pltpu.VMEM((2, tq, tk), F32), pltpu.VMEM((tq, LANES), F32), pltpu.VMEM((8, LANES), F32)],
        name="sb_attn_bwd", args=(qn, kn, vb, do, tri_ge2, tri_le2))


def _hg_consts(c):
    levels = []
    h = c // 2
    while h >= 1:
        levels.append(h)
        h //= 2
    t = np.arange(c)
    j = t[None, :]
    rows, masks = [], []
    for h in levels:
        blk = t // (2 * h)
        mid = blk * 2 * h + h - 1
        second = (t % (2 * h)) >= h
        rows.append(second[:, None] & (j > mid[:, None]) & (j <= t[:, None]))
        rows.append((~second)[:, None] & (j > t[:, None]) & (j <= mid[:, None]))
        masks.append((blk[:, None] == blk[None, :]) & second[:, None] & (~second)[None, :])
    rows.append(j <= t[:, None])
    rows.append(j > t[:, None])
    masks.append(t[:, None] == t[None, :])
    m_all = np.concatenate(rows, axis=0).astype(np.float32)
    mask_all = np.stack(masks, axis=0).astype(np.float32)
    suffix = (t[None, :] >= t[:, None]).astype(np.float32)
    return len(levels), jnp.asarray(m_all, BF16), jnp.asarray(mask_all, F32), jnp.asarray(suffix, BF16)


def _split3(x):
    hi = x.astype(BF16)
    r1 = x - hi.astype(F32)
    mid = r1.astype(BF16)
    lo = (r1 - mid.astype(F32)).astype(BF16)
    return jnp.concatenate([hi, mid, lo], axis=1)


def _join3(e):
    n = e.shape[1] // 3
    return e[:, :n] + e[:, n:2 * n] + e[:, 2 * n:]


def _hg_gates(qr, fr, lb):
    sq = _sigmoid(qr)
    sf = _sigmoid(fr)
    forget = lb + (1.0 - lb) * sf
    return qr * sq, sq, sf, forget, jnp.log(forget), 1.0 - forget


def _hg_scores(q, k, expo, masks, nlev, c):
    qb, kb = q.astype(BF16), k.astype(BF16)
    a = masks[nlev] * _dot(qb, kb, NT)
    scaled = []
    for li in range(nlev):
        fq = jnp.exp(expo[(2 * li) * c:(2 * li + 1) * c])
        fk = jnp.exp(expo[(2 * li + 1) * c:(2 * li + 2) * c])
        qs, ks = (q * fq).astype(BF16), (k * fk).astype(BF16)
        a = a + masks[li] * _dot(qs, ks, NT)
        scaled.append((qs, ks, fq, fk))
    return a, scaled, qb, kb


def _hg_heads_per_step(nh):
    return 2 if nh % 2 == 0 else 1


def _hg_fwd(proj, lb_row, gain_row, exchange=None):
    s, d4 = proj.shape
    d = d4 // 4
    nh = d // HG_HEAD_DIM
    c = min(HG_CHUNK, s)
    tb = _pick(s, 512)
    ncb = tb // c
    nlev, m_all, mask_all, _ = _hg_consts(c)
    nrow = m_all.shape[0]

    hp = _hg_heads_per_step(nh)
    wide = hp * HG_HEAD_DIM

    def body(q_ref, f_ref, i_ref, g_ref, lb_ref, gain_ref, mall_ref, mask_ref, y_ref, o_ref, st_out_ref, st_ref):
        b = pl.program_id(1)

        @pl.when(b == 0)
        def _():
            st_ref[...] = jnp.zeros_like(st_ref)

        gain = gain_ref[...]

        def inside(ci, hh):
            rows = pl.ds(pl.multiple_of(ci * c, c), c)
            cols = slice(hh * HG_HEAD_DIM, (hh + 1) * HG_HEAD_DIM)
            q, _, _, _, lf, k = _hg_gates(q_ref[rows, cols], f_ref[rows, cols], lb_ref[:, cols])
            v = i_ref[rows, cols].astype(BF16)
            expo = _join3(_dot(mall_ref[...], _split3(lf)))
            a, _, _, _ = _hg_scores(q, k, expo, mask_ref[...], nlev, c)
            b_cum = expo[2 * nlev * c:(2 * nlev + 1) * c]
            e_tail = expo[(2 * nlev + 1) * c:(2 * nlev + 2) * c]
            q_in = (q * jnp.exp(b_cum)).astype(BF16)
            k_dec = (k * jnp.exp(e_tail)).astype(BF16)
            return ci, hh, rows, cols, q_in, _dot(a.astype(BF16), v), jnp.exp(b_cum[c - 1:c, :]), _dot(v, k_dec, TN)

        def across(ci, hh, rows, cols, q_in, o_intra, decay, kv):
            st = st_ref[hh]
            st_out_ref[ci, hh] = st
            o = _dot(q_in, st.astype(BF16), NT) + o_intra
            st_ref[hh] = st * decay + kv
            o_ref[rows, cols] = o
            r = lax.rsqrt(jnp.mean(o * o, axis=-1, keepdims=True) + NORM_EPS)
            y_ref[rows, cols] = (o * r * gain * _sigmoid(g_ref[rows, cols])).astype(y_ref.dtype)

        per_trip = 2 if ncb % 2 == 0 else 1

        def trip(it, carry):
            ready = [inside(per_trip * it + n, hh) for n in range(per_trip) for hh in range(hp)]
            for r in ready:
                across(*r)
            return carry

        lax.fori_loop(0, ncb // per_trip, trip, 0)

    part = lambda k: pl.BlockSpec((tb, wide), lambda h, b: (b, k * (nh // hp) + h))
    head_row = pl.BlockSpec((1, wide), lambda h, b: (0, h))
    tok = pl.BlockSpec((tb, wide), lambda h, b: (b, h))
    return _call_with_exchange(
        body, exchange, grid=(nh // hp, s // tb),
        in_specs=[part(0), part(1), part(2), part(3), head_row,
                  pl.BlockSpec((1, HG_HEAD_DIM), lambda h, b: (0, 0)),
                  pl.BlockSpec((nrow, c), lambda h, b: (0, 0)),
                  pl.BlockSpec((nlev + 1, c, c), lambda h, b: (0, 0, 0))],
        out_specs=[tok, tok, pl.BlockSpec((ncb, hp, HG_HEAD_DIM, HG_HEAD_DIM), lambda h, b: (b, h, 0, 0))],
        out_shape=[jax.ShapeDtypeStruct((s, d), BF16), jax.ShapeDtypeStruct((s, d), F32),
                   jax.ShapeDtypeStruct((s // c, nh, HG_HEAD_DIM, HG_HEAD_DIM), F32)],
        scratch_shapes=[pltpu.VMEM((hp, HG_HEAD_DIM, HG_HEAD_DIM), F32)],
        name="hg_fwd", args=(proj, proj, proj, proj, lb_row, gain_row, m_all, mask_all))


def _hg_bwd(proj, lb_row, gain_row, o_saved, states, dy, exchange=None):
    s, d4 = proj.shape
    d = d4 // 4
    nh = d // HG_HEAD_DIM
    c = min(HG_CHUNK, s)
    tb = _pick(s, 512)
    ncb = tb // c
    nb = s // tb
    nlev, m_all, mask_all, suffix = _hg_consts(c)
    nrow = m_all.shape[0]
    hp = _hg_heads_per_step(nh)
    wide = hp * HG_HEAD_DIM

    def body(q_ref, f_ref, i_ref, g_ref, lb_ref, gain_ref, o_ref, st_in_ref, dy_ref, mall_ref, mask_ref, suf_ref,
             dproj_ref, dlb_ref, dgain_ref, dst_ref, run_ref):
        b = pl.program_id(1)

        @pl.when(b == 0)
        def _():
            dst_ref[...] = jnp.zeros_like(dst_ref)
            run_ref[...] = jnp.zeros_like(run_ref)
            dlb_ref[...] = jnp.zeros_like(dlb_ref)
            dgain_ref[...] = jnp.zeros_like(dgain_ref)

        gain = gain_ref[...]

        def head_chunk(ci, rows, hh, cols):
            lb = lb_ref[:, cols]
            qr, fr = q_ref[rows, cols], f_ref[rows, cols]
            q, sq, sf, forget, lf, k = _hg_gates(qr, fr, lb)
            v = i_ref[rows, cols].astype(BF16)
            expo = _join3(_dot(mall_ref[...], _split3(lf)))
            masks = mask_ref[...]
            o = o_ref[rows, cols]
            dyv = dy_ref[rows, cols]
            sg = _sigmoid(g_ref[rows, cols])
            r = lax.rsqrt(jnp.mean(o * o, axis=-1, keepdims=True) + NORM_EPS)
            ohat = o * r
            dyn = dyv * sg
            dproj_ref[3, rows, cols] = (dyv * ohat * gain * sg * (1.0 - sg)).astype(dproj_ref.dtype)
            dgain_ref[:, cols] += jnp.sum(dyn * ohat, axis=0, keepdims=True)
            dohat = dyn * gain
            do = (r * (dohat - ohat * jnp.mean(dohat * ohat, axis=-1, keepdims=True))).astype(BF16)
            dst = dst_ref[hh]
            dstb = dst.astype(BF16)
            a, scaled, qb, kb = _hg_scores(q, k, expo, masks, nlev, c)
            f_cum = jnp.exp(expo[2 * nlev * c:(2 * nlev + 1) * c])
            f_tail = jnp.exp(expo[(2 * nlev + 1) * c:(2 * nlev + 2) * c])
            q_in = (q * f_cum).astype(BF16)
            k_dec = (k * f_tail).astype(BF16)
            t_in = _join3(_dot(do, _split3(st_in_ref[ci, hh])))
            t_st = _join3(_dot(v, _split3(dst)))
            da = _dot(do, v, NT)
            dam = (masks[nlev] * da).astype(BF16)
            dq = t_in * f_cum + _dot(dam, kb)
            dk = t_st * f_tail + _dot(dam, qb, TN)
            db = q_in.astype(F32) * t_in - k_dec.astype(F32) * t_st
            for li in range(nlev):
                qs, ks, fq, fk = scaled[li]
                dam = (masks[li] * da).astype(BF16)
                t_q = _dot(dam, ks)
                t_k = _dot(dam, qs, TN)
                dq = dq + t_q * fq
                dk = dk + t_k * fk
                db = db + (qs.astype(F32) * t_q - ks.astype(F32) * t_k)
            dv = _dot(a.astype(BF16), do, TN) + _dot(k_dec, dstb, NT)
            dst_ref[hh] = dst * f_cum[c - 1:c, :] + _dot(do, q_in, TN)
            dlf = _join3(_dot(suf_ref[...], _split3(db))) + run_ref[hh]
            run_ref[hh] = dlf[0:1, :]
            dforget = dlf / forget - dk
            dlb_ref[:, cols] += jnp.sum(dforget * (1.0 - sf), axis=0, keepdims=True)
            dproj_ref[1, rows, cols] = (dforget * (1.0 - lb) * sf * (1.0 - sf)).astype(dproj_ref.dtype)
            dproj_ref[0, rows, cols] = (dq * sq * (1.0 + qr * (1.0 - sq))).astype(dproj_ref.dtype)
            dproj_ref[2, rows, cols] = dv.astype(dproj_ref.dtype)

        def chunk(it, carry):
            ci = ncb - 1 - it
            rows = pl.ds(pl.multiple_of(ci * c, c), c)
            for hh in range(hp):
                head_chunk(ci, rows, hh, slice(hh * HG_HEAD_DIM, (hh + 1) * HG_HEAD_DIM))
            return carry

        lax.fori_loop(0, ncb, chunk, 0)

    part = lambda k: pl.BlockSpec((tb, wide), lambda h, b: (nb - 1 - b, k * (nh // hp) + h))
    head_row = pl.BlockSpec((1, wide), lambda h, b: (0, h))
    tok = pl.BlockSpec((tb, wide), lambda h, b: (nb - 1 - b, h))
    const2 = lambda shape: pl.BlockSpec(shape, lambda h, b: (0, 0))
    return _call_with_exchange(
        body, exchange, grid=(nh // hp, nb),
        in_specs=[part(0), part(1), part(2), part(3), head_row, const2((1, HG_HEAD_DIM)), tok,
                  pl.BlockSpec((ncb, hp, HG_HEAD_DIM, HG_HEAD_DIM), lambda h, b: (nb - 1 - b, h, 0, 0)),
                  tok, const2((nrow, c)), pl.BlockSpec((nlev + 1, c, c), lambda h, b: (0, 0, 0)), const2((c, c))],
        out_specs=[pl.BlockSpec((4, tb, wide), lambda h, b: (0, nb - 1 - b, h)), head_row, head_row],
        out_shape=[jax.ShapeDtypeStruct((4, s, d), BF16)] + [jax.ShapeDtypeStruct((1, d), F32)] * 2,
        scratch_shapes=[pltpu.VMEM((hp, HG_HEAD_DIM, HG_HEAD_DIM), F32), pltpu.VMEM((hp, 1, HG_HEAD_DIM), F32)],
        name="hg_bwd", args=(proj, proj, proj, proj, lb_row, gain_row, o_saved, states, dy, m_all, mask_all, suffix))


def _lb_fwd(logits):
    n, d = logits.shape

    def body(l_ref, lb_ref, p_ref):
        rows = [l_ref[i:i + 1, :] for i in range(n)]
        m = functools.reduce(jnp.maximum, rows)
        es = [jnp.exp(r - m) for r in rows]
        tot = functools.reduce(lambda a, b: a + b, es)
        ps = [e / tot for e in es]
        run = jnp.zeros_like(ps[0])
        for i in range(n):
            run = run + ps[i]
            lb_ref[i:i + 1, :] = run - ps[0]
            p_ref[i:i + 1, :] = ps[i]

    return pl.pallas_call(
        body, out_shape=[jax.ShapeDtypeStruct((n, d), F32)] * 2, name="lb_fwd",
    )(logits)


def _lb_bwd(p, dlb):
    n, d = p.shape

    def body(p_ref, dlb_ref, dl_ref):
        ps = [p_ref[i:i + 1, :] for i in range(n)]
        ds = [dlb_ref[i:i + 1, :] for i in range(n)]
        total = functools.reduce(lambda a, b: a + b, ds)
        dps = []
        for i in range(n):
            dp = functools.reduce(lambda a, b: a + b, ds[i:])
            dps.append(dp - total if i == 0 else dp)
        inner = functools.reduce(lambda a, b: a + b, [pi * di for pi, di in zip(ps, dps)])
        for i in range(n):
            dl_ref[i:i + 1, :] = ps[i] * (dps[i] - inner)

    return pl.pallas_call(body, out_shape=jax.ShapeDtypeStruct((n, d), F32), name="lb_bwd")(p, dlb)


def _as2d(a):
    return a.reshape(-1, a.shape[-1])


def _adamw(w, m, v, grads, exchange=None):
    shape = w.shape
    w2, m2, v2 = _as2d(w), _as2d(m), _as2d(v)
    g2 = [_as2d(g) for g in grads]
    rows, cols = w2.shape
    tr = _pick(rows, 512)
    ng = len(g2)
    bc1 = 1.0 - ADAM_B1 ** ADAM_STEP
    bc2 = 1.0 - ADAM_B2 ** ADAM_STEP

    def body(w_ref, m_ref, v_ref, *rest):
        g = rest[0][...]
        for extra in rest[1:ng]:
            g = g + extra[...]
        g_out, d_out, m_out, v_out = rest[ng:]
        mn = ADAM_B1 * m_ref[...] + (1.0 - ADAM_B1) * g
        vn = ADAM_B2 * v_ref[...] + (1.0 - ADAM_B2) * (g * g)
        m_hat = mn / bc1
        v_hat = vn / bc2
        g_out[...] = g
        d_out[...] = -ADAM_LR * (m_hat / (jnp.sqrt(v_hat) + ADAM_EPS) + ADAM_WD * w_ref[...])
        m_out[...] = mn
        v_out[...] = vn

    spec = pl.BlockSpec((tr, cols), lambda i: (i, 0))
    outs, moved = _call_with_exchange(
        body, exchange, grid=(rows // tr,), in_specs=[spec] * (3 + ng), out_specs=[spec] * 4,
        out_shape=[jax.ShapeDtypeStruct((rows, cols), F32)] * 4, scratch_shapes=[], name="adamw",
        args=(w2, m2, v2, *g2))
    result = tuple(o.reshape(shape) for o in outs)
    return result if exchange is None else (result, moved)


def _sum_slots(parts, recv, chip, into, index):
    _, rows, cols = parts.shape
    tr = _pick(rows, 512)

    def body(chip_ref, own_ref, r0_ref, r1_ref, r2_ref, into_ref, o_ref):
        f = lambda r: r[...].astype(F32)
        o_ref[...] = ((f(own_ref) + f(r0_ref)) + f(r1_ref)) + f(r2_ref)

    grid_spec = pltpu.PrefetchScalarGridSpec(
        num_scalar_prefetch=1, grid=(rows // tr,),
        in_specs=[pl.BlockSpec((None, tr, cols), lambda i, chip_ref: (chip_ref[0], i, 0))]
        + [pl.BlockSpec((None, tr, cols), functools.partial(lambda i, chip_ref, k: (k, i, 0), k=k)) for k in range(3)]
        + [pl.BlockSpec(memory_space=pl.ANY)],
        out_specs=pl.BlockSpec((None, tr, cols), lambda i, chip_ref: (index, i, 0)))
    return pl.pallas_call(
        body, grid_spec=grid_spec, out_shape=jax.ShapeDtypeStruct(into.shape, F32),
        input_output_aliases={5: 0}, compiler_params=_params(("parallel",)), name="sum_slots",
    )(chip, parts, recv, recv, recv, into)


def _pack_rows(pieces):
    cols = pieces[0].shape[1]
    used = sum(p.shape[0] for p in pieces)
    rows = -(-used // 8) * 8

    def body(*refs):
        out_ref = refs[-1]
        at = 0
        for ref in refs[:-1]:
            out_ref[at:at + ref.shape[0], :] = ref[...]
            at += ref.shape[0]
        if at < rows:
            out_ref[at:rows, :] = jnp.zeros((rows - at, cols), F32)

    return pl.pallas_call(body, out_shape=jax.ShapeDtypeStruct((rows, cols), F32), name="pack_rows")(*pieces)


def _sum_devices(gathered):
    n, rows, cols = gathered.shape

    def body(g_ref, o_ref):
        acc = g_ref[0]
        for i in range(1, n):
            acc = acc + g_ref[i]
        o_ref[...] = acc

    return pl.pallas_call(body, out_shape=jax.ShapeDtypeStruct((rows, cols), F32), name="sum_devices")(gathered)


def _coords():
    return lax.axis_index("x"), lax.axis_index("y"), lax.axis_index("c")


def _chip_peers(x, y, c):
    out = []
    for fx, fy in ((0, 1), (1, 0), (1, 1)):
        px = 1 - x if fx else x
        py = 1 - y if fy else y
        out.append(((px, py, c), 2 * px + py))
    return out


class _ChipExchange:
    def __init__(self, kind, arrays):
        self.kind, self.kinds, self.arrays, self.n = kind, [kind] * len(arrays), list(arrays), len(arrays)
        self._shapes()

    def also(self, kind, arrays):
        self.kinds += [kind] * len(arrays)
        self.arrays += list(arrays)
        self.n = len(self.arrays)
        self._shapes()
        return self

    def _shapes(self):
        lead = {"gather": lambda a: (N_CHIPS,) + a.shape, "scatter": lambda a: (3,) + a.shape[1:],
                "swap": lambda a: a.shape}
        self.out_shape = [jax.ShapeDtypeStruct(lead[k](a), a.dtype) for k, a in zip(self.kinds, self.arrays)]
        self.scratch = [pltpu.SemaphoreType.DMA((3 * self.n,)), pltpu.SemaphoreType.DMA((3 * self.n,)),
                        pltpu.SemaphoreType.DMA((self.n,))]

    def copies(self, ins, outs, send_sems, recv_sems, local_sems):
        x, y, c = _coords()
        me = 2 * x + y
        starts, waits = [], []
        for t, kind in enumerate(self.kinds):
            if kind == "swap":
                cp = pltpu.make_async_remote_copy(
                    src_ref=ins[t], dst_ref=outs[t], send_sem=send_sems.at[3 * t], recv_sem=recv_sems.at[3 * t],
                    device_id=(x, y, 1 - c), device_id_type=MESH)
                starts.append(cp.start)
                waits += [cp.wait_send, cp.wait_recv]
                continue
            if kind == "gather":
                own = pltpu.make_async_copy(ins[t], outs[t].at[me], local_sems.at[t])
                starts.append(own.start)
                waits.append(own.wait)
            for k, (peer, peer_chip) in enumerate(_chip_peers(x, y, c)):
                sems = dict(send_sem=send_sems.at[3 * t + k], recv_sem=recv_sems.at[3 * t + k],
                            device_id=peer, device_id_type=MESH)
                if kind == "gather":
                    send = pltpu.make_async_remote_copy(src_ref=ins[t], dst_ref=outs[t].at[me], **sems)
                    recv = pltpu.make_async_remote_copy(src_ref=ins[t], dst_ref=outs[t].at[peer_chip], **sems)
                else:
                    send = pltpu.make_async_remote_copy(src_ref=ins[t].at[peer_chip], dst_ref=outs[t].at[k], **sems)
                    recv = send
                starts.append(send.start)
                waits += [send.wait_send, recv.wait_recv]
        return starts, waits

    def run(self, name):
        n = self.n

        def body(*refs):
            starts, waits = self.copies(refs[:n], refs[n:2 * n], *refs[2 * n:])
            for f in starts + waits:
                f()

        return pl.pallas_call(body, in_specs=[HBM_SPEC] * n, out_specs=[HBM_SPEC] * n, out_shape=self.out_shape,
                              scratch_shapes=self.scratch, name=name)(*self.arrays)


def _call_with_exchange(body, exchange, *, grid, in_specs, out_specs, out_shape, scratch_shapes, name, args,
                        sequential=False):
    if exchange is None:
        first_axis = "arbitrary" if sequential else "parallel"
        outs = pl.pallas_call(body, grid=grid, in_specs=in_specs, out_specs=out_specs, out_shape=out_shape,
                              scratch_shapes=scratch_shapes,
                              compiler_params=_params((first_axis,) + ("arbitrary",) * (len(grid) - 1)),
                              name=name)(*args)
        return outs, []
    n_in, n_out, n_scr, n = len(in_specs), len(out_specs), len(scratch_shapes), exchange.n

    def wrapped(*refs):
        ins, ex_in = refs[:n_in], refs[n_in:n_in + n]
        outs = refs[n_in + n:n_in + n + n_out]
        ex_out = refs[n_in + n + n_out:n_in + 2 * n + n_out]
        scr = refs[n_in + 2 * n + n_out:n_in + 2 * n + n_out + n_scr]
        sems = refs[n_in + 2 * n + n_out + n_scr:]
        ids = [pl.program_id(a) for a in range(len(grid))]
        first = functools.reduce(jnp.logical_and, [i == 0 for i in ids])
        last = functools.reduce(jnp.logical_and, [i == g - 1 for i, g in zip(ids, grid)])

        @pl.when(first)
        def _():
            for f in exchange.copies(ex_in, ex_out, *sems)[0]:
                f()

        body(*ins, *outs, *scr)

        @pl.when(last)
        def _():
            for f in exchange.copies(ex_in, ex_out, *sems)[1]:
                f()

    res = pl.pallas_call(
        wrapped, grid=grid, in_specs=list(in_specs) + [HBM_SPEC] * n, out_specs=list(out_specs) + [HBM_SPEC] * n,
        out_shape=list(out_shape) + exchange.out_shape, scratch_shapes=list(scratch_shapes) + exchange.scratch,
        compiler_params=_params(("arbitrary",) * len(grid)), name=name + "_" + exchange.kind,
    )(*args, *exchange.arrays)
    return res[:n_out], res[n_out:]


def _gather_devices(a):
    def body(in_ref, out_ref, send_sems, recv_sems, local_sem):
        x, y, c = _coords()
        me = 4 * x + 2 * y + c
        own = pltpu.make_async_copy(in_ref, out_ref.at[me], local_sem)
        own.start()
        waits = [own.wait]
        for k in range(1, N_DEVICES):
            px = 1 - x if k & 4 else x
            py = 1 - y if k & 2 else y
            pc = 1 - c if k & 1 else c
            peer = (px, py, pc)
            send = pltpu.make_async_remote_copy(
                src_ref=in_ref, dst_ref=out_ref.at[me], send_sem=send_sems.at[k - 1], recv_sem=recv_sems.at[k - 1],
                device_id=peer, device_id_type=MESH)
            send.start()
            recv = pltpu.make_async_remote_copy(
                src_ref=in_ref, dst_ref=out_ref.at[4 * px + 2 * py + pc], send_sem=send_sems.at[k - 1],
                recv_sem=recv_sems.at[k - 1], device_id=peer, device_id_type=MESH)
            waits += [send.wait_send, recv.wait_recv]
        for w in waits:
            w()

    return pl.pallas_call(
        body, in_specs=[HBM_SPEC], out_specs=HBM_SPEC,
        out_shape=jax.ShapeDtypeStruct((N_DEVICES,) + a.shape, a.dtype),
        scratch_shapes=[pltpu.SemaphoreType.DMA((N_DEVICES - 1,)), pltpu.SemaphoreType.DMA((N_DEVICES - 1,)),
                        pltpu.SemaphoreType.DMA],
        name="gather_devices",
    )(a)


def _mlp_grad_epilogue(r, u):
    return r * (2.0 * jnp.maximum(u, 0.0))


def kernel(x, norm_gains, sb_w_qkv, sb_q_gain, sb_k_gain, sb_w_o, hg_w_in, hg_lb_logits, hg_norm_gain, hg_w_o, mlp_w1, mlp_w2, loss_target, m_norm_gains, m_sb_w_qkv, m_sb_q_gain, m_sb_k_gain, m_sb_w_o, m_hg_w_in, m_hg_lb_logits, m_hg_norm_gain, m_hg_w_o, m_mlp_w1, m_mlp_w2, v_norm_gains, v_sb_w_qkv, v_sb_q_gain, v_sb_k_gain, v_sb_w_o, v_hg_w_in, v_hg_lb_logits, v_hg_norm_gain, v_hg_w_o, v_mlp_w1, v_mlp_w2):
    depth = norm_gains.shape[0]
    n_sb, n_hg = sb_w_qkv.shape[0], hg_w_in.shape[0]
    xs, tgt = x[0], loss_target[0]
    s, d = xs.shape
    dq = d // N_CHIPS
    cx, cy, cc = _coords()
    chip = 2 * cx + cy
    chip_arr = jnp.reshape(chip, (1,)).astype(jnp.int32)

    def mixer_weights(layer):
        j = layer // 2
        return (sb_w_qkv[j], sb_w_o[j]) if layer % 2 == 0 else (hg_w_in[j], hg_w_o[j])

    w_in_g, ng_g, lbl_g = _ChipExchange(
        "gather", [mixer_weights(0)[0].astype(BF16), norm_gains, hg_lb_logits]).run("gather_first")
    gains = jnp.transpose(ng_g, (1, 2, 0, 3)).reshape(depth, 2, d)
    logits = jnp.transpose(lbl_g, (1, 0, 2)).reshape(n_hg, d)
    lbs, lb_p = _lb_fwd(logits)
    qg_rows = [jnp.tile(sb_q_gain[j], d // SB_HEAD_DIM)[None] for j in range(n_sb)]
    kg_rows = [jnp.tile(sb_k_gain[j], d // SB_HEAD_DIM)[None] for j in range(n_sb)]

    saved, wts = [], []
    xc = xs
    for layer in range(depth):
        j = layer // 2
        ahead = [mixer_weights(layer)[1], mlp_w1[layer], mlp_w2[layer]]
        if layer + 1 < depth:
            ahead.append(mixer_weights(layer + 1)[0])
        gather = _ChipExchange("gather", [a.astype(BF16) for a in ahead])
        h1 = _rmsnorm_fwd(xc, gains[layer, 0][None])
        if layer % 2 == 0:
            qkv = _mm_fwd_cols(h1, w_in_g, name="sb_qkv")
            qn, kn, vb = _qk_norm_fwd(qkv, qg_rows[j], kg_rows[j])
            o, moved = _sb_attn_fwd(qn, kn, vb, gather)
            x_mid = _mm_fwd_rows(o, moved[0], residual=xc, name="sb_out")
            mix = (qkv, qn, kn, vb, o)
        else:
            proj = _mm_fwd_cols(h1, w_in_g, name="hg_in")
            (y, o, states), moved = _hg_fwd(proj, lbs[j][None], hg_norm_gain[j][None], gather)
            x_mid = _mm_fwd_rows(y, moved[0], residual=xc, name="hg_out")
            mix = (proj, y, o, states)
        w_out_g, w1_g, w2_g = moved[:3]
        h2 = _rmsnorm_fwd(x_mid, gains[layer, 1][None])
        u = _mm_fwd_cols(h2, w1_g, name="mlp_up")
        x_out = _mm_fwd_rows(u, w2_g, residual=x_mid, a_fn=_relu2, name="mlp_down")
        saved.append((xc, h1, mix, x_mid, h2, u))
        wts.append((w_in_g, w_out_g, w1_g, w2_g))
        w_in_g = moved[3] if layer + 1 < depth else None
        xc = x_out

    sq, dx = _loss_head(xc, tgt)
    loss = lax.psum(jnp.sum(sq) * (0.5 / d), ("x", "y", "c"))

    dgains = [[None, None] for _ in range(depth)]
    dqg, dkg = [None] * n_sb, [None] * n_sb
    dhgain, dlb = [None] * n_hg, [None] * n_hg
    grads, received, pending = {}, {}, []

    def ready(key, parts):
        grads[key] = parts
        pending.append(key)

    def scatter_of(keys):
        return _ChipExchange("scatter", [grads[k] for k in keys]) if keys else None

    def sent(keys, moved):
        for k, r in zip(keys, moved):
            received[k] = r
            pending.remove(k)

    def chip_sum(kind, layers):
        total = jnp.zeros((len(layers),) + grads[kind, layers[0]].shape[1:], F32)
        for index, l in enumerate(layers):
            total = _sum_slots(grads[kind, l], received[kind, l], chip_arr, total, index)
        return total

    sb_layers, hg_layers = range(0, depth, 2), range(1, depth, 2)
    tensors = [("in", sb_layers), ("out", sb_layers), ("in", hg_layers), ("out", hg_layers),
               ("w1", range(depth)), ("w2", range(depth))]

    for layer in reversed(range(depth)):
        j = layer // 2
        x_in, h1, mix, x_mid, h2, u = saved[layer]
        w_in_g, w_out_g, w1_g, w2_g = wts[layer]
        du = _mm_bwd_rows(dx, w2_g, name="mlp_down_dx", out_dtype=BF16, epi_fn=_mlp_grad_epilogue, epi_args=(u,))
        ready(("w2", layer), _mm_dw_rows(u, dx, a_fn=_relu2, name="mlp_down_dw"))
        ready(("w1", layer), _mm_dw_cols(h2, du, name="mlp_up_dw"))
        dx, dgains[layer][1] = _mm_bwd_cols_norm(du, w1_g, x_mid, gains[layer, 1][None], dx, name="mlp_up_dx")
        if layer % 2 == 0:
            qkv, qn, kn, vb, o = mix
            do = _mm_bwd_rows(dx, w_out_g, name="sb_out_dx")
            ready(("out", layer), _mm_dw_rows(o, dx, name="sb_out_dw"))
            keys = list(pending)
            (dqn, dkn, dv), moved = _sb_attn_bwd(qn, kn, vb, do, scatter_of(keys))
            sent(keys, moved)
            d_in, dqg[j], dkg[j] = _qk_norm_bwd(qkv, qg_rows[j], kg_rows[j], dqn, dkn, dv)
            ready(("in", layer), _mm_dw_cols(h1, d_in, name="sb_qkv_dw"))
            dx_name = "sb_qkv_dx"
        else:
            proj, y, o, states = mix
            dy = _mm_bwd_rows(dx, w_out_g, name="hg_out_dx")
            ready(("out", layer), _mm_dw_rows(y, dx, name="hg_out_dw"))
            keys = list(pending)
            (d_in, dlb[j], dhgain[j]), moved = _hg_bwd(
                proj, lbs[j][None], hg_norm_gain[j][None], o, states, dy, scatter_of(keys))
            sent(keys, moved)
            ready(("in", layer), _mm_dw_cols(h1, d_in, name="hg_in_dw"))
            dx_name = "hg_in_dx"
        if layer > 0:
            dx, dgains[layer][0] = _mm_bwd_cols_norm(d_in, w_in_g, x_in, gains[layer, 0][None], dx, name=dx_name)
        else:
            keys = list(pending)
            early_sums = [chip_sum(*t) for t in tensors[1:]]
            (dx, dgains[layer][0]), moved = _mm_bwd_cols_norm(
                d_in, w_in_g, x_in, gains[layer, 0][None], dx, name=dx_name,
                exchange=scatter_of(keys).also("swap", early_sums))
            sent(keys, moved[:len(keys)])
            early_other = moved[len(keys):]
    grad_x = dx[None]
    dlogits = _lb_bwd(lb_p, jnp.concatenate(dlb, axis=0))

    big_w = [sb_w_qkv, sb_w_o, hg_w_in, hg_w_o, mlp_w1, mlp_w2]
    big_m = [m_sb_w_qkv, m_sb_w_o, m_hg_w_in, m_hg_w_o, m_mlp_w1, m_mlp_w2]
    big_v = [v_sb_w_qkv, v_sb_w_o, v_hg_w_in, v_hg_w_o, v_mlp_w1, v_mlp_w2]
    late_sum = chip_sum(*tensors[0])
    big_second, late_other = _adamw(big_w[1], big_m[1], big_v[1], [early_sums[0], early_other[0]],
                                    _ChipExchange("swap", [late_sum]))
    big = [_adamw(big_w[0], big_m[0], big_v[0], [late_sum, late_other[0]]), big_second]
    big += [_adamw(w, m, v, [a, b]) for w, m, v, a, b in
            zip(big_w[2:], big_m[2:], big_v[2:], early_sums[1:], early_other[1:])]

    pieces = [r for pair in dgains for r in pair] + [dlogits] + dqg + dkg + dhgain
    small = _sum_devices(_gather_devices(_pack_rows(pieces)))
    my_cols = lambda a: lax.dynamic_slice_in_dim(a, chip * dq, dq, axis=1)
    fold = lambda rows, width: jnp.sum(rows.reshape(rows.shape[0], -1, width), axis=1)
    base = 2 * depth + n_hg
    g_ng = my_cols(small[0:2 * depth]).reshape(norm_gains.shape)
    g_lbl = my_cols(small[2 * depth:base])
    g_qg = fold(small[base:base + n_sb], SB_HEAD_DIM)
    g_kg = fold(small[base + n_sb:base + 2 * n_sb], SB_HEAD_DIM)
    g_hgn = fold(small[base + 2 * n_sb:base + 2 * n_sb + n_hg], HG_HEAD_DIM)
    r_ng = _adamw(norm_gains, m_norm_gains, v_norm_gains, [g_ng])
    r_qg = _adamw(sb_q_gain, m_sb_q_gain, v_sb_q_gain, [g_qg])
    r_kg = _adamw(sb_k_gain, m_sb_k_gain, v_sb_k_gain, [g_kg])
    r_lbl = _adamw(hg_lb_logits, m_hg_lb_logits, v_hg_lb_logits, [g_lbl])
    r_hgn = _adamw(hg_norm_gain, m_hg_norm_gain, v_hg_norm_gain, [g_hgn])

    per_weight = [r_ng, big[0], r_qg, r_kg, big[1], big[2], r_lbl, r_hgn, big[3], big[4], big[5]]
    outs = [loss, grad_x]
    for field in range(4):
        outs += [r[field] for r in per_weight]
    return tuple(outs)
```

```python
import functools
import math

import numpy as np
import jax
import jax.numpy as jnp
from jax import lax
from jax.experimental import pallas as pl
from jax.experimental.pallas import tpu as pltpu

F32 = jnp.float32
BF16 = jnp.bfloat16
GRAD_SLOT_DTYPE = jnp.bfloat16

NORM_EPS = 1e-6
SB_HEAD_DIM = 64
HG_HEAD_DIM = 128
HG_CHUNK = 128
LANES = 128
VMEM_LIMIT_BYTES = 56 * 2 ** 20
N_CHIPS = 4
N_DEVICES = 8

ADAM_LR = 0.001
ADAM_B1 = 0.9
ADAM_B2 = 0.999
ADAM_EPS = 1e-08
ADAM_WD = 0.01
ADAM_STEP = 10

MESH = pl.DeviceIdType.MESH
HBM_SPEC = pl.BlockSpec(memory_space=pltpu.HBM)

NN = (((1,), (0,)), ((), ()))
NT = (((1,), (1,)), ((), ()))
TN = (((0,), (0,)), ((), ()))


def _params(sem=None):
    return pltpu.CompilerParams(dimension_semantics=sem, vmem_limit_bytes=VMEM_LIMIT_BYTES)


def _pick(dim, pref):
    for t in (1024, 768, 512, 384, 256, 128, 64, 32, 16, 8):
        if t <= pref and dim % t == 0:
            return t
    return dim


def _dot(a, b, dims=NN):
    return lax.dot_general(a, b, dims, preferred_element_type=F32)


def _sigmoid(x):
    e = jnp.exp(-jnp.abs(x))
    return jnp.where(x >= 0, 1.0, e) / (1.0 + e)


def _matmul(a, b, *, mode, grid, a_block, a_map, b_block, b_map, o_block, o_map, out_shape, out_dtype, name,
            a_fn=None, epi_fn=None, epi_args=(), epi_row_args=(), col_sums=False, exchange=None):
    nk = grid[2]
    dims = {"nn": NN, "nt": NT, "tn": TN}[mode]
    n_epi = len(epi_args) + len(epi_row_args)
    n_out = 2 if col_sums else 1
    tn = o_block[-1]
    assert not col_sums or grid[1] == 1, "the column sums stay resident only with one tile along N"

    def body(a_ref, b_ref, *rest):
        epi_refs = rest[:n_epi]
        o_ref = rest[n_epi]
        kk = pl.program_id(2)

        def emit(r):
            if epi_fn is not None:
                r = epi_fn(r, *[e[...] for e in epi_refs])
            if col_sums:
                r, row = r
                sums_ref = rest[n_epi + 1]
                first = pl.program_id(0) == 0

                @pl.when(first)
                def _():
                    sums_ref[...] = row

                @pl.when(jnp.logical_not(first))
                def _():
                    sums_ref[...] += row
            o_ref[...] = r.astype(o_ref.dtype)

        av = a_ref[...]
        if a_fn is not None:
            av = a_fn(av)
        part = _dot(av.astype(BF16), b_ref[...].astype(BF16), dims)
        if nk == 1:
            emit(part)
            return
        acc_ref = rest[n_epi + n_out]

        @pl.when(kk == 0)
        def _():
            acc_ref[...] = part

        @pl.when(kk > 0)
        def _():
            acc_ref[...] += part

        @pl.when(kk == nk - 1)
        def _():
            emit(acc_ref[...])

    acc_shape = tuple(d for d in o_block if d is not None)
    row_spec = pl.BlockSpec((1, tn), lambda i, j, kk: (0, j))
    in_specs = [pl.BlockSpec(a_block, a_map), pl.BlockSpec(b_block, b_map)]
    in_specs += [pl.BlockSpec(o_block, o_map) for _ in epi_args] + [row_spec for _ in epi_row_args]
    out_specs, out_shapes = [pl.BlockSpec(o_block, o_map)], [jax.ShapeDtypeStruct(out_shape, out_dtype)]
    if col_sums:
        out_specs, out_shapes = out_specs + [row_spec], out_shapes + [jax.ShapeDtypeStruct((1, out_shape[-1]), F32)]
    outs, moved = _call_with_exchange(
        body, exchange, grid=grid, in_specs=in_specs, out_specs=out_specs, out_shape=out_shapes,
        scratch_shapes=[pltpu.VMEM(acc_shape, F32)] if nk > 1 else [], name=name,
        args=(a, b, *epi_args, *epi_row_args), sequential=col_sums)
    result = tuple(outs) if col_sums else outs[0]
    return result if exchange is None else (result, moved)


def _relu2(u):
    r = jnp.maximum(u, 0.0)
    return r * r


def _add(r, res):
    return r + res


def _mm_fwd_cols(a, wg, *, name):
    s, k = a.shape
    ncs = wg.shape[2]
    tm, tk, tn = _pick(s, 1024), _pick(k, 1024), _pick(ncs, 1024)
    npb = ncs // tn
    return _matmul(a, wg, mode="nn", grid=(s // tm, N_CHIPS * npb, k // tk),
                   a_block=(tm, tk), a_map=lambda i, j, kk: (i, kk),
                   b_block=(None, tk, tn), b_map=lambda i, j, kk: (j // npb, kk, j % npb),
                   o_block=(tm, tn), o_map=lambda i, j, kk: (i, j),
                   out_shape=(s, N_CHIPS * ncs), out_dtype=F32, name=name)


def _rows_joined(wg):
    assert wg.shape[1] % 16 == 0, "joining the leading axes must not cross a tile of 16 rows"
    return wg.reshape(wg.shape[0] * wg.shape[1], wg.shape[2])


def _mm_fwd_rows(a, wg, *, residual, name, a_fn=None):
    s = a.shape[0]
    w = _rows_joined(wg)
    k, n = w.shape
    tm, tk, tn = _pick(s, 1024), _pick(k, 1024), _pick(n, 1024)
    return _matmul(a, w, mode="nn", grid=(s // tm, n // tn, k // tk),
                   a_block=(tm, tk), a_map=lambda i, j, kk: (i, kk),
                   b_block=(tk, tn), b_map=lambda i, j, kk: (kk, j),
                   o_block=(tm, tn), o_map=lambda i, j, kk: (i, j),
                   out_shape=(s, n), out_dtype=F32, name=name, a_fn=a_fn, epi_fn=_add, epi_args=(residual,))


def _rmsnorm_grad(dh, x, dx_res, gain):
    r = lax.rsqrt(jnp.mean(x * x, axis=-1, keepdims=True) + NORM_EPS)
    xhat = x * r
    dxhat = dh * gain
    dx = r * (dxhat - xhat * jnp.mean(dxhat * xhat, axis=-1, keepdims=True))
    return dx_res + dx, jnp.sum(dh * xhat, axis=0, keepdims=True)


def _mm_bwd_cols_norm(dy, wg, x, gain_row, dx_res, *, name, exchange=None):
    by_slot = dy.ndim == 3
    s = dy.shape[1] if by_slot else dy.shape[0]
    kw, ncs = wg.shape[1], wg.shape[2]
    tm, tk = _pick(s, 512), _pick(ncs, 1024)
    kpb = ncs // tk
    a_block, a_map = ((None, tm, tk), lambda i, j, kk: (kk // kpb, i, kk % kpb)) if by_slot else (
        (tm, tk), lambda i, j, kk: (i, kk))
    return _matmul(dy, wg, mode="nt", grid=(s // tm, 1, N_CHIPS * kpb),
                   a_block=a_block, a_map=a_map,
                   b_block=(None, kw, tk), b_map=lambda i, j, kk: (kk // kpb, j, kk % kpb),
                   o_block=(tm, kw), o_map=lambda i, j, kk: (i, j),
                   out_shape=(s, kw), out_dtype=F32, name=name, exchange=exchange,
                   epi_fn=_rmsnorm_grad, epi_args=(x, dx_res), epi_row_args=(gain_row,), col_sums=True)


def _mm_bwd_rows(dy, wg, *, name, out_dtype=F32, epi_fn=None, epi_args=()):
    s, n = dy.shape
    w = _rows_joined(wg)
    rows = w.shape[0]
    tm, tn, tk = _pick(s, 1024), _pick(rows, 1024), _pick(n, 1024)
    return _matmul(dy, w, mode="nt", grid=(s // tm, rows // tn, n // tk),
                   a_block=(tm, tk), a_map=lambda i, j, kk: (i, kk),
                   b_block=(tn, tk), b_map=lambda i, j, kk: (j, kk),
                   o_block=(tm, tn), o_map=lambda i, j, kk: (i, j),
                   out_shape=(s, rows), out_dtype=out_dtype, name=name, epi_fn=epi_fn, epi_args=epi_args)


def _mm_dw_cols(xa, dy, *, name):
    s, kx = xa.shape
    by_slot = dy.ndim == 3
    ncs = dy.shape[2] if by_slot else dy.shape[1] // N_CHIPS
    tm, tn, tk = _pick(kx, 1024), _pick(ncs, 1024), _pick(s, 1024)
    npb = ncs // tn
    b_block, b_map = ((None, tk, tn), lambda i, j, kk: (j // npb, kk, j % npb)) if by_slot else (
        (tk, tn), lambda i, j, kk: (kk, j))
    return _matmul(xa, dy, mode="tn", grid=(kx // tm, N_CHIPS * npb, s // tk),
                   a_block=(tk, tm), a_map=lambda i, j, kk: (kk, i),
                   b_block=b_block, b_map=b_map,
                   o_block=(None, tm, tn), o_map=lambda i, j, kk: (j // npb, i, j % npb),
                   out_shape=(N_CHIPS, kx, ncs), out_dtype=GRAD_SLOT_DTYPE, name=name)


def _mm_dw_rows(xa, dy, *, name, a_fn=None):
    s, n = dy.shape
    rows = xa.shape[1]
    assert (rows // N_CHIPS) % 16 == 0, "splitting the rows into slots must not cut a tile of 16 rows"
    tm, tn, tk = _pick(rows, 1024), _pick(n, 1024), _pick(s, 1024)
    dw = _matmul(xa, dy, mode="tn", grid=(rows // tm, n // tn, s // tk),
                 a_block=(tk, tm), a_map=lambda i, j, kk: (kk, i),
                 b_block=(tk, tn), b_map=lambda i, j, kk: (kk, j),
                 o_block=(tm, tn), o_map=lambda i, j, kk: (i, j),
                 out_shape=(rows, n), out_dtype=GRAD_SLOT_DTYPE, name=name, a_fn=a_fn)
    return dw.reshape(N_CHIPS, rows // N_CHIPS, n)


def _rmsnorm_fwd(x, gain_row):
    s, d = x.shape
    ts = _pick(s, 512)

    def body(x_ref, g_ref, h_ref):
        xv = x_ref[...]
        r = lax.rsqrt(jnp.mean(xv * xv, axis=-1, keepdims=True) + NORM_EPS)
        h_ref[...] = (xv * r * g_ref[...]).astype(h_ref.dtype)

    return pl.pallas_call(
        body, grid=(s // ts,),
        in_specs=[pl.BlockSpec((ts, d), lambda i: (i, 0)), pl.BlockSpec((1, d), lambda i: (0, 0))],
        out_specs=pl.BlockSpec((ts, d), lambda i: (i, 0)),
        out_shape=jax.ShapeDtypeStruct((s, d), BF16),
        compiler_params=_params(("parallel",)), name="rmsnorm_fwd",
    )(x, gain_row)


def _loss_head(y, target):
    s, d = y.shape
    ts = _pick(s, 512)

    def body(y_ref, t_ref, sq_ref, dy_ref):
        i = pl.program_id(0)
        err = y_ref[...] - t_ref[...]
        dy_ref[...] = err / d
        part = jnp.sum(err * err, axis=0, keepdims=True)

        @pl.when(i == 0)
        def _():
            sq_ref[...] = part

        @pl.when(i > 0)
        def _():
            sq_ref[...] += part

    return pl.pallas_call(
        body, grid=(s // ts,),
        in_specs=[pl.BlockSpec((ts, d), lambda i: (i, 0)), pl.BlockSpec((ts, d), lambda i: (i, 0))],
        out_specs=[pl.BlockSpec((1, d), lambda i: (0, 0)), pl.BlockSpec((ts, d), lambda i: (i, 0))],
        out_shape=[jax.ShapeDtypeStruct((1, d), F32), jax.ShapeDtypeStruct((s, d), F32)],
        compiler_params=_params(("arbitrary",)), name="loss_head",
    )(y, target)


def _pair_mean(val, low_half):
    s0 = jnp.sum(jnp.where(low_half, val, 0.0), axis=-1, keepdims=True)
    s1 = jnp.sum(jnp.where(low_half, 0.0, val), axis=-1, keepdims=True)
    return jnp.where(low_half, s0, s1) * (1.0 / SB_HEAD_DIM)


def _qk_norm_fwd(qkv, qgain_row, kgain_row):
    s, d3 = qkv.shape
    d = d3 // 3
    ts = _pick(s, 512)
    groups = d // LANES

    def body(q_ref, k_ref, v_ref, qg_ref, kg_ref, qn_ref, kn_ref, vb_ref):
        low_half = lax.broadcasted_iota(jnp.int32, (ts, LANES), 1) < SB_HEAD_DIM
        for src, gain, dst in ((q_ref, qg_ref, qn_ref), (k_ref, kg_ref, kn_ref)):
            for p in range(groups):
                cols = slice(p * LANES, (p + 1) * LANES)
                xp = src[:, cols]
                r = lax.rsqrt(_pair_mean(xp * xp, low_half) + NORM_EPS)
                dst[:, cols] = (xp * r * gain[:, cols]).astype(dst.dtype)
        vb_ref[...] = v_ref[...].astype(vb_ref.dtype)

    tok = lambda c: pl.BlockSpec((ts, d), lambda i: (i, c))
    row = pl.BlockSpec((1, d), lambda i: (0, 0))
    return pl.pallas_call(
        body, grid=(s // ts,),
        in_specs=[tok(0), tok(1), tok(2), row, row],
        out_specs=[tok(0), tok(0), tok(0)],
        out_shape=[jax.ShapeDtypeStruct((s, d), BF16)] * 3,
        compiler_params=_params(("parallel",)), name="qk_norm_fwd",
    )(qkv, qkv, qkv, qgain_row, kgain_row)


def _qk_norm_bwd(qkv, qgain_row, kgain_row, dqn, dkn, dv):
    s, d3 = qkv.shape
    d = d3 // 3
    ts = _pick(s, 512)
    groups = d // LANES

    def body(q_ref, k_ref, qg_ref, kg_ref, dqn_ref, dkn_ref, dv_ref, dqkv_ref, dqg_ref, dkg_ref):
        i = pl.program_id(0)
        low_half = lax.broadcasted_iota(jnp.int32, (ts, LANES), 1) < SB_HEAD_DIM
        for which, (src, gain, dsrc, dgain) in enumerate(((q_ref, qg_ref, dqn_ref, dqg_ref),
                                                          (k_ref, kg_ref, dkn_ref, dkg_ref))):
            for p in range(groups):
                cols = slice(p * LANES, (p + 1) * LANES)
                xp = src[:, cols]
                r = lax.rsqrt(_pair_mean(xp * xp, low_half) + NORM_EPS)
                xhat = xp * r
                dy = dsrc[:, cols]
                dxhat = dy * gain[:, cols]
                dx = r * (dxhat - xhat * _pair_mean(dxhat * xhat, low_half))
                dqkv_ref[:, which * d + p * LANES: which * d + (p + 1) * LANES] = dx.astype(dqkv_ref.dtype)
                part = jnp.sum(dy * xhat, axis=0, keepdims=True)

                @pl.when(i == 0)
                def _():
                    dgain[:, cols] = part

                @pl.when(i > 0)
                def _():
                    dgain[:, cols] += part
        dqkv_ref[:, 2 * d:] = dv_ref[...].astype(dqkv_ref.dtype)

    tok = lambda c: pl.BlockSpec((ts, d), lambda i: (i, c))
    row = pl.BlockSpec((1, d), lambda i: (0, 0))
    return pl.pallas_call(
        body, grid=(s // ts,),
        in_specs=[tok(0), tok(1), row, row, tok(0), tok(0), tok(0)],
        out_specs=[pl.BlockSpec((ts, d3), lambda i: (i, 0)), row, row],
        out_shape=[jax.ShapeDtypeStruct((s, d3), BF16), jax.ShapeDtypeStruct((1, d), F32),
                   jax.ShapeDtypeStruct((1, d), F32)],
        compiler_params=_params(("arbitrary",)), name="qk_norm_bwd",
    )(qkv, qkv, qgain_row, kgain_row, dqn, dkn, dv)


def _split2(x):
    hi = x.astype(BF16)
    lo = (x - hi.astype(F32)).astype(BF16)
    return hi, lo


SB_TK = 128


def _sb_consts(tk):
    j = np.arange(tk)
    ones = np.ones((tk, tk), np.float32)
    out = []
    for tri in ((j[:, None] >= j[None, :]), (j[:, None] <= j[None, :])):
        half = np.concatenate([tri.astype(np.float32), ones], axis=1)
        out.append(jnp.asarray(np.concatenate([half, half], axis=0), BF16))
    return out


def _head_stack(blk, low_half):
    f = blk.astype(F32)
    return jnp.concatenate([jnp.where(low_half, f, 0.0), jnp.where(low_half, 0.0, f)], axis=0).astype(BF16)


def _sb_tile_sums(z, valid, tri2):
    e = jnp.exp(-jnp.abs(z))
    lstay = jnp.minimum(-z, 0.0) - jnp.log(1.0 + e)
    if valid is not None:
        lstay = jnp.where(valid, lstay, 0.0)
    hi, lo = _split2(lstay)
    return e, _dot(jnp.concatenate([hi, lo], axis=1), tri2)


def _sb_weights(z, c2, valid, run):
    w = jnp.exp(z + c2[:, :SB_TK] + run)
    return w if valid is None else jnp.where(valid, w, 0.0)


EXP_IS_ZERO_BELOW = -110.0


def _max_row_norm(x):
    f = x.astype(F32)
    return jnp.sqrt(jnp.max(jnp.sum(f * f, axis=-1, keepdims=True)))


def _sb_score_bound(qs, kmax_ref):
    return _max_row_norm(qs) * jnp.max(kmax_ref[...]) * 1.01 + 1.0


def _sb_rest_is_zero(run_ref, bound):
    return jnp.max(jnp.maximum(run_ref[0], run_ref[1])) + bound < EXP_IS_ZERO_BELOW


def _sb_attn_fwd(qn, kn, vb, exchange=None):
    s, d = qn.shape
    tk = SB_TK
    tq = _pick(s, 256)
    nq, ndiag = s // tq, tq // tk
    assert tq % (2 * tk) == 0, "tiles below the diagonal are taken two at a time"
    npairs = d // LANES
    scale = 1.0 / math.sqrt(SB_HEAD_DIM)
    tri_ge2, _ = _sb_consts(tk)

    def body(q_ref, k_ref, v_ref, tri_ref, o_ref, acc_ref, run_ref, kmax_ref):
        qi = pl.program_id(1)

        @pl.when(qi == 0)
        def _():
            kmax_ref[...] = jnp.full(kmax_ref.shape, _max_row_norm(k_ref[...]), F32)

        low_half = lax.broadcasted_iota(jnp.int32, (tk, LANES), 1) < SB_HEAD_DIM
        row = lax.broadcasted_iota(jnp.int32, (tq, tk), 0)
        col = lax.broadcasted_iota(jnp.int32, (tq, tk), 1)
        qs = (q_ref[...].astype(F32) * scale).astype(BF16)
        bound = _sb_score_bound(qs, kmax_ref)
        acc_ref[...] = jnp.zeros_like(acc_ref)
        run_ref[...] = jnp.zeros_like(run_ref)
        n_full = qi * ndiag

        def sums(kb, dd):
            koff = pl.multiple_of(kb * tk, tk)
            kcat = _head_stack(k_ref[pl.ds(koff, tk), :], low_half)
            vcat = _head_stack(v_ref[pl.ds(koff, tk), :], low_half)
            z2 = _dot(qs, kcat, NT)
            valid = None if dd is None else row > col + dd * tk
            zs = [z2[:, h * tk:(h + 1) * tk] for h in range(2)]
            return zs, [_sb_tile_sums(z, valid, tri_ref[...])[1] for z in zs], valid, vcat

        def finish(zs, c2s, valid, vcat):
            ws = []
            for h in range(2):
                ws.append(_sb_weights(zs[h], c2s[h], valid, run_ref[h]).astype(BF16))
                run_ref[h] += c2s[h][:, tk:]
            acc_ref[...] += _dot(jnp.concatenate(ws, axis=1), vcat)

        def first_tiles(below):
            pres = [sums(n_full + dd, dd) for dd in reversed(range(ndiag))]
            pres += [sums(n_full - 1 - n, None) for n in range(below)]
            for pre in pres:
                finish(*pre)

        @pl.when(qi == 0)
        def _():
            first_tiles(0)

        @pl.when(qi > 0)
        def _():
            first_tiles(2)

        def two_tiles(carry):
            it, _ = carry
            kb = n_full - 1 - 2 * it
            first, second = sums(kb, None), sums(kb - 1, None)
            finish(*first)
            finish(*second)
            return it + 1, _sb_rest_is_zero(run_ref, bound)

        lax.while_loop(lambda c: jnp.logical_and(c[0] < n_full // 2, jnp.logical_not(c[1])), two_tiles,
                       (jnp.minimum(qi, 1), _sb_rest_is_zero(run_ref, bound)))
        o_ref[...] = acc_ref[...]

    blk = pl.BlockSpec((tq, LANES), lambda p, i: (i, p))
    full = pl.BlockSpec((s, LANES), lambda p, i: (0, p))
    (o,), moved = _call_with_exchange(
        body, exchange, grid=(npairs, nq),
        in_specs=[blk, full, full, pl.BlockSpec((2 * tk, 2 * tk), lambda p, i: (0, 0))],
        out_specs=[blk], out_shape=[jax.ShapeDtypeStruct((s, d), F32)],
        scratch_shapes=[pltpu.VMEM((tq, LANES), F32), pltpu.VMEM((2, tq, tk), F32), pltpu.VMEM((8, LANES), F32)],
        name="sb_attn_fwd", args=(qn, kn, vb, tri_ge2))
    return o, moved


def _sb_attn_bwd(qn, kn, vb, do, exchange=None):
    s, d = qn.shape
    tk = SB_TK
    tq = _pick(s, 256)
    nq, ndiag = s // tq, tq // tk
    assert tq % (2 * tk) == 0, "tiles below the diagonal are taken two at a time"
    npairs = d // LANES
    scale = 1.0 / math.sqrt(SB_HEAD_DIM)
    tri_ge2, tri_le2 = _sb_consts(tk)

    def body(q_ref, k_ref, v_ref, do_ref, tge_ref, tle_ref, dq_ref, dk_ref, dv_ref,
             g_cache, s_cache, run_ref, dq_acc, kmax_ref):
        qi = pl.program_id(1)

        @pl.when(qi == 0)
        def _():
            dk_ref[...] = jnp.zeros_like(dk_ref)
            dv_ref[...] = jnp.zeros_like(dv_ref)
            kmax_ref[...] = jnp.full(kmax_ref.shape, _max_row_norm(k_ref[...]), F32)

        low_half = lax.broadcasted_iota(jnp.int32, (tk, LANES), 1) < SB_HEAD_DIM
        row = lax.broadcasted_iota(jnp.int32, (tq, tk), 0)
        col = lax.broadcasted_iota(jnp.int32, (tq, tk), 1)
        qs = (q_ref[...].astype(F32) * scale).astype(BF16)
        bound = _sb_score_bound(qs, kmax_ref)
        dob = do_ref[...].astype(BF16)
        n_full = qi * ndiag

        def a_sums(kb, dd):
            koff = pl.multiple_of(kb * tk, tk)
            kcat = _head_stack(k_ref[pl.ds(koff, tk), :], low_half)
            vcat = _head_stack(v_ref[pl.ds(koff, tk), :], low_half)
            z2 = _dot(qs, kcat, NT)
            dw2 = _dot(dob, vcat, NT)
            valid = None if dd is None else row > col + dd * tk
            c2s = []
            for h in range(2):
                cols = slice(h * tk, (h + 1) * tk)
                z = z2[:, cols]
                e, c2 = _sb_tile_sums(z, valid, tge_ref[...])
                s_cache[kb, :, cols] = jnp.where(z >= 0, 1.0, e) / (1.0 + e)
                c2s.append(c2)
            return kb, koff, z2, dw2, c2s, valid

        def a_finish(kb, koff, z2, dw2, c2s, valid):
            ws = []
            for h in range(2):
                cols = slice(h * tk, (h + 1) * tk)
                w = _sb_weights(z2[:, cols], c2s[h], valid, run_ref[h])
                run_ref[h] += c2s[h][:, tk:]
                g_cache[kb, :, cols] = w * dw2[:, cols]
                ws.append(w.astype(BF16))
            dv2 = _dot(jnp.concatenate(ws, axis=1), dob, TN)
            dv_ref[pl.ds(koff, tk), :] += jnp.where(low_half, dv2[:tk], dv2[tk:])

        def b_sums(kb, dd):
            gs = [g_cache[kb, :, h * tk:(h + 1) * tk] for h in range(2)]
            p2s = [_dot(jnp.concatenate(_split2(g), axis=1), tle_ref[...]) for g in gs]
            return kb, gs, p2s, (None if dd is None else row > col + dd * tk)

        def b_finish(kb, gs, p2s, valid):
            koff = pl.multiple_of(kb * tk, tk)
            dzs = []
            for h in range(2):
                dz = gs[h] - s_cache[kb, :, h * tk:(h + 1) * tk] * (p2s[h][:, :tk] + run_ref[h])
                if valid is not None:
                    dz = jnp.where(valid, dz, 0.0)
                run_ref[h] += p2s[h][:, tk:]
                dzs.append(dz.astype(BF16))
            dzcat = jnp.concatenate(dzs, axis=1)
            dq_acc[...] += _dot(dzcat, _head_stack(k_ref[pl.ds(koff, tk), :], low_half))
            dk2 = _dot(dzcat, qs, TN)
            dk_ref[pl.ds(koff, tk), :] += jnp.where(low_half, dk2[:tk], dk2[tk:])

        run_ref[...] = jnp.zeros_like(run_ref)
        near = jnp.minimum(qi, 1)

        def a_first_tiles(below):
            pres = [a_sums(n_full + dd, dd) for dd in reversed(range(ndiag))]
            pres += [a_sums(n_full - 1 - n, None) for n in range(below)]
            for pre in pres:
                a_finish(*pre)

        @pl.when(qi == 0)
        def _():
            a_first_tiles(0)

        @pl.when(qi > 0)
        def _():
            a_first_tiles(2)

        def two_a(carry):
            it, _ = carry
            kb = n_full - 1 - 2 * it
            first, second = a_sums(kb, None), a_sums(kb - 1, None)
            a_finish(*first)
            a_finish(*second)
            return it + 1, _sb_rest_is_zero(run_ref, bound)

        trips, _ = lax.while_loop(lambda c: jnp.logical_and(c[0] < n_full // 2, jnp.logical_not(c[1])), two_a,
                                  (near, _sb_rest_is_zero(run_ref, bound)))

        run_ref[...] = jnp.zeros_like(run_ref)
        dq_acc[...] = jnp.zeros_like(dq_acc)
        kb_first = n_full - 2 * trips

        def two_b(it, carry):
            first, second = b_sums(kb_first + 2 * it, None), b_sums(kb_first + 2 * it + 1, None)
            b_finish(*first)
            b_finish(*second)
            return carry

        lax.fori_loop(0, trips - near, two_b, 0)

        def b_last_tiles(below):
            pres = [b_sums(n_full - below + n, None) for n in range(below)]
            pres += [b_sums(n_full + dd, dd) for dd in range(ndiag)]
            for pre in pres:
                b_finish(*pre)

        @pl.when(qi == 0)
        def _():
            b_last_tiles(0)

        @pl.when(qi > 0)
        def _():
            b_last_tiles(2)

        dq_ref[...] = dq_acc[...] * scale

    blk = pl.BlockSpec((tq, LANES), lambda p, i: (i, p))
    full = pl.BlockSpec((s, LANES), lambda p, i: (0, p))
    tri = pl.BlockSpec((2 * tk, 2 * tk), lambda p, i: (0, 0))
    return _call_with_exchange(
        body, exchange, grid=(npairs, nq),
        in_specs=[blk, full, full, blk, tri, tri],
        out_specs=[blk, full, full],
        out_shape=[jax.ShapeDtypeStruct((s, d), F32)] * 3,
        scratch_shapes=[pltpu.VMEM((s // tk, tq, 2 * tk), F32), pltpu.VMEM((s // tk, tq, 2 * tk), F32),
                        pltpu.VMEM((2, tq, tk), F32), pltpu.VMEM((tq, LANES), F32), pltpu.VMEM((8, LANES), F32)],
        name="sb_attn_bwd", args=(qn, kn, vb, do, tri_ge2, tri_le2))


def _hg_consts(c):
    levels = []
    h = c // 2
    while h >= 1:
        levels.append(h)
        h //= 2
    t = np.arange(c)
    j = t[None, :]
    rows, masks = [], []
    for h in levels:
        blk = t // (2 * h)
        mid = blk * 2 * h + h - 1
        second = (t % (2 * h)) >= h
        rows.append(second[:, None] & (j > mid[:, None]) & (j <= t[:, None]))
        rows.append((~second)[:, None] & (j > t[:, None]) & (j <= mid[:, None]))
        masks.append((blk[:, None] == blk[None, :]) & second[:, None] & (~second)[None, :])
    rows.append(j <= t[:, None])
    rows.append(j > t[:, None])
    masks.append(t[:, None] == t[None, :])
    m_all = np.concatenate(rows, axis=0).astype(np.float32)
    mask_all = np.stack(masks, axis=0).astype(np.float32)
    suffix = (t[None, :] >= t[:, None]).astype(np.float32)
    return len(levels), jnp.asarray(m_all, BF16), jnp.asarray(mask_all, F32), jnp.asarray(suffix, BF16)


def _split3(x):
    hi = x.astype(BF16)
    r1 = x - hi.astype(F32)
    mid = r1.astype(BF16)
    lo = (r1 - mid.astype(F32)).astype(BF16)
    return jnp.concatenate([hi, mid, lo], axis=1)


def _join3(e):
    n = e.shape[1] // 3
    return e[:, :n] + e[:, n:2 * n] + e[:, 2 * n:]


def _hg_gates(qr, fr, lb):
    sq = _sigmoid(qr)
    sf = _sigmoid(fr)
    forget = lb + (1.0 - lb) * sf
    return qr * sq, sq, sf, forget, jnp.log(forget), 1.0 - forget


def _hg_scores(q, k, expo, masks, nlev, c):
    qb, kb = q.astype(BF16), k.astype(BF16)
    a = masks[nlev] * _dot(qb, kb, NT)
    scaled = []
    for li in range(nlev):
        fq = jnp.exp(expo[(2 * li) * c:(2 * li + 1) * c])
        fk = jnp.exp(expo[(2 * li + 1) * c:(2 * li + 2) * c])
        qs, ks = (q * fq).astype(BF16), (k * fk).astype(BF16)
        a = a + masks[li] * _dot(qs, ks, NT)
        scaled.append((qs, ks, fq, fk))
    return a, scaled, qb, kb


def _hg_heads_per_step(nh):
    return 2 if nh % 2 == 0 else 1


def _hg_fwd(proj, lb_row, gain_row, exchange=None):
    s, d4 = proj.shape
    d = d4 // 4
    nh = d // HG_HEAD_DIM
    c = min(HG_CHUNK, s)
    tb = _pick(s, 512)
    ncb = tb // c
    nlev, m_all, mask_all, _ = _hg_consts(c)
    nrow = m_all.shape[0]

    hp = _hg_heads_per_step(nh)
    wide = hp * HG_HEAD_DIM

    def body(q_ref, f_ref, i_ref, g_ref, lb_ref, gain_ref, mall_ref, mask_ref, y_ref, o_ref, st_out_ref, st_ref):
        b = pl.program_id(1)

        @pl.when(b == 0)
        def _():
            st_ref[...] = jnp.zeros_like(st_ref)

        gain = gain_ref[...]

        def inside(ci, hh):
            rows = pl.ds(pl.multiple_of(ci * c, c), c)
            cols = slice(hh * HG_HEAD_DIM, (hh + 1) * HG_HEAD_DIM)
            q, _, _, _, lf, k = _hg_gates(q_ref[rows, cols], f_ref[rows, cols], lb_ref[:, cols])
            v = i_ref[rows, cols].astype(BF16)
            expo = _join3(_dot(mall_ref[...], _split3(lf)))
            a, _, _, _ = _hg_scores(q, k, expo, mask_ref[...], nlev, c)
            b_cum = expo[2 * nlev * c:(2 * nlev + 1) * c]
            e_tail = expo[(2 * nlev + 1) * c:(2 * nlev + 2) * c]
            q_in = (q * jnp.exp(b_cum)).astype(BF16)
            k_dec = (k * jnp.exp(e_tail)).astype(BF16)
            return ci, hh, rows, cols, q_in, _dot(a.astype(BF16), v), jnp.exp(b_cum[c - 1:c, :]), _dot(v, k_dec, TN)

        def across(ci, hh, rows, cols, q_in, o_intra, decay, kv):
            st = st_ref[hh]
            st_out_ref[ci, hh] = st
            o = _dot(q_in, st.astype(BF16), NT) + o_intra
            st_ref[hh] = st * decay + kv
            o_ref[rows, cols] = o
            r = lax.rsqrt(jnp.mean(o * o, axis=-1, keepdims=True) + NORM_EPS)
            y_ref[rows, cols] = (o * r * gain * _sigmoid(g_ref[rows, cols])).astype(y_ref.dtype)

        per_trip = 2 if ncb % 2 == 0 else 1

        def trip(it, carry):
            ready = [inside(per_trip * it + n, hh) for n in range(per_trip) for hh in range(hp)]
            for r in ready:
                across(*r)
            return carry

        lax.fori_loop(0, ncb // per_trip, trip, 0)

    part = lambda k: pl.BlockSpec((tb, wide), lambda h, b: (b, k * (nh // hp) + h))
    head_row = pl.BlockSpec((1, wide), lambda h, b: (0, h))
    tok = pl.BlockSpec((tb, wide), lambda h, b: (b, h))
    return _call_with_exchange(
        body, exchange, grid=(nh // hp, s // tb),
        in_specs=[part(0), part(1), part(2), part(3), head_row,
                  pl.BlockSpec((1, HG_HEAD_DIM), lambda h, b: (0, 0)),
                  pl.BlockSpec((nrow, c), lambda h, b: (0, 0)),
                  pl.BlockSpec((nlev + 1, c, c), lambda h, b: (0, 0, 0))],
        out_specs=[tok, tok, pl.BlockSpec((ncb, hp, HG_HEAD_DIM, HG_HEAD_DIM), lambda h, b: (b, h, 0, 0))],
        out_shape=[jax.ShapeDtypeStruct((s, d), BF16), jax.ShapeDtypeStruct((s, d), F32),
                   jax.ShapeDtypeStruct((s // c, nh, HG_HEAD_DIM, HG_HEAD_DIM), F32)],
        scratch_shapes=[pltpu.VMEM((hp, HG_HEAD_DIM, HG_HEAD_DIM), F32)],
        name="hg_fwd", args=(proj, proj, proj, proj, lb_row, gain_row, m_all, mask_all))


def _hg_bwd(proj, lb_row, gain_row, o_saved, states, dy, exchange=None):
    s, d4 = proj.shape
    d = d4 // 4
    nh = d // HG_HEAD_DIM
    c = min(HG_CHUNK, s)
    tb = _pick(s, 512)
    ncb = tb // c
    nb = s // tb
    nlev, m_all, mask_all, suffix = _hg_consts(c)
    nrow = m_all.shape[0]
    hp = _hg_heads_per_step(nh)
    wide = hp * HG_HEAD_DIM

    def body(q_ref, f_ref, i_ref, g_ref, lb_ref, gain_ref, o_ref, st_in_ref, dy_ref, mall_ref, mask_ref, suf_ref,
             dproj_ref, dlb_ref, dgain_ref, dst_ref, run_ref):
        b = pl.program_id(1)

        @pl.when(b == 0)
        def _():
            dst_ref[...] = jnp.zeros_like(dst_ref)
            run_ref[...] = jnp.zeros_like(run_ref)
            dlb_ref[...] = jnp.zeros_like(dlb_ref)
            dgain_ref[...] = jnp.zeros_like(dgain_ref)

        gain = gain_ref[...]

        def head_chunk(ci, rows, hh, cols):
            lb = lb_ref[:, cols]
            qr, fr = q_ref[rows, cols], f_ref[rows, cols]
            q, sq, sf, forget, lf, k = _hg_gates(qr, fr, lb)
            v = i_ref[rows, cols].astype(BF16)
            expo = _join3(_dot(mall_ref[...], _split3(lf)))
            masks = mask_ref[...]
            o = o_ref[rows, cols]
            dyv = dy_ref[rows, cols]
            sg = _sigmoid(g_ref[rows, cols])
            r = lax.rsqrt(jnp.mean(o * o, axis=-1, keepdims=True) + NORM_EPS)
            ohat = o * r
            dyn = dyv * sg
            dproj_ref[3, rows, cols] = (dyv * ohat * gain * sg * (1.0 - sg)).astype(dproj_ref.dtype)
            dgain_ref[:, cols] += jnp.sum(dyn * ohat, axis=0, keepdims=True)
            dohat = dyn * gain
            do = (r * (dohat - ohat * jnp.mean(dohat * ohat, axis=-1, keepdims=True))).astype(BF16)
            dst = dst_ref[hh]
            dstb = dst.astype(BF16)
            a, scaled, qb, kb = _hg_scores(q, k, expo, masks, nlev, c)
            f_cum = jnp.exp(expo[2 * nlev * c:(2 * nlev + 1) * c])
            f_tail = jnp.exp(expo[(2 * nlev + 1) * c:(2 * nlev + 2) * c])
            q_in = (q * f_cum).astype(BF16)
            k_dec = (k * f_tail).astype(BF16)
            t_in = _join3(_dot(do, _split3(st_in_ref[ci, hh])))
            t_st = _join3(_dot(v, _split3(dst)))
            da = _dot(do, v, NT)
            dam = (masks[nlev] * da).astype(BF16)
            dq = t_in * f_cum + _dot(dam, kb)
            dk = t_st * f_tail + _dot(dam, qb, TN)
            db = q_in.astype(F32) * t_in - k_dec.astype(F32) * t_st
            for li in range(nlev):
                qs, ks, fq, fk = scaled[li]
                dam = (masks[li] * da).astype(BF16)
                t_q = _dot(dam, ks)
                t_k = _dot(dam, qs, TN)
                dq = dq + t_q * fq
                dk = dk + t_k * fk
                db = db + (qs.astype(F32) * t_q - ks.astype(F32) * t_k)
            dv = _dot(a.astype(BF16), do, TN) + _dot(k_dec, dstb, NT)
            dst_ref[hh] = dst * f_cum[c - 1:c, :] + _dot(do, q_in, TN)
            dlf = _join3(_dot(suf_ref[...], _split3(db))) + run_ref[hh]
            run_ref[hh] = dlf[0:1, :]
            dforget = dlf / forget - dk
            dlb_ref[:, cols] += jnp.sum(dforget * (1.0 - sf), axis=0, keepdims=True)
            dproj_ref[1, rows, cols] = (dforget * (1.0 - lb) * sf * (1.0 - sf)).astype(dproj_ref.dtype)
            dproj_ref[0, rows, cols] = (dq * sq * (1.0 + qr * (1.0 - sq))).astype(dproj_ref.dtype)
            dproj_ref[2, rows, cols] = dv.astype(dproj_ref.dtype)

        def chunk(it, carry):
            ci = ncb - 1 - it
            rows = pl.ds(pl.multiple_of(ci * c, c), c)
            for hh in range(hp):
                head_chunk(ci, rows, hh, slice(hh * HG_HEAD_DIM, (hh + 1) * HG_HEAD_DIM))
            return carry

        lax.fori_loop(0, ncb, chunk, 0)

    part = lambda k: pl.BlockSpec((tb, wide), lambda h, b: (nb - 1 - b, k * (nh // hp) + h))
    head_row = pl.BlockSpec((1, wide), lambda h, b: (0, h))
    tok = pl.BlockSpec((tb, wide), lambda h, b: (nb - 1 - b, h))
    const2 = lambda shape: pl.BlockSpec(shape, lambda h, b: (0, 0))
    return _call_with_exchange(
        body, exchange, grid=(nh // hp, nb),
        in_specs=[part(0), part(1), part(2), part(3), head_row, const2((1, HG_HEAD_DIM)), tok,
                  pl.BlockSpec((ncb, hp, HG_HEAD_DIM, HG_HEAD_DIM), lambda h, b: (nb - 1 - b, h, 0, 0)),
                  tok, const2((nrow, c)), pl.BlockSpec((nlev + 1, c, c), lambda h, b: (0, 0, 0)), const2((c, c))],
        out_specs=[pl.BlockSpec((4, tb, wide), lambda h, b: (0, nb - 1 - b, h)), head_row, head_row],
        out_shape=[jax.ShapeDtypeStruct((4, s, d), BF16)] + [jax.ShapeDtypeStruct((1, d), F32)] * 2,
        scratch_shapes=[pltpu.VMEM((hp, HG_HEAD_DIM, HG_HEAD_DIM), F32), pltpu.VMEM((hp, 1, HG_HEAD_DIM), F32)],
        name="hg_bwd", args=(proj, proj, proj, proj, lb_row, gain_row, o_saved, states, dy, m_all, mask_all, suffix))


def _lb_fwd(logits):
    n, d = logits.shape

    def body(l_ref, lb_ref, p_ref):
        rows = [l_ref[i:i + 1, :] for i in range(n)]
        m = functools.reduce(jnp.maximum, rows)
        es = [jnp.exp(r - m) for r in rows]
        tot = functools.reduce(lambda a, b: a + b, es)
        ps = [e / tot for e in es]
        run = jnp.zeros_like(ps[0])
        for i in range(n):
            run = run + ps[i]
            lb_ref[i:i + 1, :] = run - ps[0]
            p_ref[i:i + 1, :] = ps[i]

    return pl.pallas_call(
        body, out_shape=[jax.ShapeDtypeStruct((n, d), F32)] * 2, name="lb_fwd",
    )(logits)


def _lb_bwd(p, dlb):
    n, d = p.shape

    def body(p_ref, dlb_ref, dl_ref):
        ps = [p_ref[i:i + 1, :] for i in range(n)]
        ds = [dlb_ref[i:i + 1, :] for i in range(n)]
        total = functools.reduce(lambda a, b: a + b, ds)
        dps = []
        for i in range(n):
            dp = functools.reduce(lambda a, b: a + b, ds[i:])
            dps.append(dp - total if i == 0 else dp)
        inner = functools.reduce(lambda a, b: a + b, [pi * di for pi, di in zip(ps, dps)])
        for i in range(n):
            dl_ref[i:i + 1, :] = ps[i] * (dps[i] - inner)

    return pl.pallas_call(body, out_shape=jax.ShapeDtypeStruct((n, d), F32), name="lb_bwd")(p, dlb)


def _as2d(a):
    return a.reshape(-1, a.shape[-1])


def _adamw(w, m, v, grads, exchange=None):
    shape = w.shape
    w2, m2, v2 = _as2d(w), _as2d(m), _as2d(v)
    g2 = [_as2d(g) for g in grads]
    rows, cols = w2.shape
    tr = _pick(rows, 512)
    ng = len(g2)
    bc1 = 1.0 - ADAM_B1 ** ADAM_STEP
    bc2 = 1.0 - ADAM_B2 ** ADAM_STEP

    def body(w_ref, m_ref, v_ref, *rest):
        g = rest[0][...]
        for extra in rest[1:ng]:
            g = g + extra[...]
        g_out, d_out, m_out, v_out = rest[ng:]
        mn = ADAM_B1 * m_ref[...] + (1.0 - ADAM_B1) * g
        vn = ADAM_B2 * v_ref[...] + (1.0 - ADAM_B2) * (g * g)
        m_hat = mn / bc1
        v_hat = vn / bc2
        g_out[...] = g
        d_out[...] = -ADAM_LR * (m_hat / (jnp.sqrt(v_hat) + ADAM_EPS) + ADAM_WD * w_ref[...])
        m_out[...] = mn
        v_out[...] = vn

    spec = pl.BlockSpec((tr, cols), lambda i: (i, 0))
    outs, moved = _call_with_exchange(
        body, exchange, grid=(rows // tr,), in_specs=[spec] * (3 + ng), out_specs=[spec] * 4,
        out_shape=[jax.ShapeDtypeStruct((rows, cols), F32)] * 4, scratch_shapes=[], name="adamw",
        args=(w2, m2, v2, *g2))
    result = tuple(o.reshape(shape) for o in outs)
    return result if exchange is None else (result, moved)


def _sum_slots(parts, recv, chip, into, index):
    _, rows, cols = parts.shape
    tr = _pick(rows, 512)

    def body(chip_ref, own_ref, r0_ref, r1_ref, r2_ref, into_ref, o_ref):
        f = lambda r: r[...].astype(F32)
        o_ref[...] = ((f(own_ref) + f(r0_ref)) + f(r1_ref)) + f(r2_ref)

    grid_spec = pltpu.PrefetchScalarGridSpec(
        num_scalar_prefetch=1, grid=(rows // tr,),
        in_specs=[pl.BlockSpec((None, tr, cols), lambda i, chip_ref: (chip_ref[0], i, 0))]
        + [pl.BlockSpec((None, tr, cols), functools.partial(lambda i, chip_ref, k: (k, i, 0), k=k)) for k in range(3)]
        + [pl.BlockSpec(memory_space=pl.ANY)],
        out_specs=pl.BlockSpec((None, tr, cols), lambda i, chip_ref: (index, i, 0)))
    return pl.pallas_call(
        body, grid_spec=grid_spec, out_shape=jax.ShapeDtypeStruct(into.shape, F32),
        input_output_aliases={5: 0}, compiler_params=_params(("parallel",)), name="sum_slots",
    )(chip, parts, recv, recv, recv, into)


def _pack_rows(pieces):
    cols = pieces[0].shape[1]
    used = sum(p.shape[0] for p in pieces)
    rows = -(-used // 8) * 8

    def body(*refs):
        out_ref = refs[-1]
        at = 0
        for ref in refs[:-1]:
            out_ref[at:at + ref.shape[0], :] = ref[...]
            at += ref.shape[0]
        if at < rows:
            out_ref[at:rows, :] = jnp.zeros((rows - at, cols), F32)

    return pl.pallas_call(body, out_shape=jax.ShapeDtypeStruct((rows, cols), F32), name="pack_rows")(*pieces)


def _sum_devices(gathered):
    n, rows, cols = gathered.shape

    def body(g_ref, o_ref):
        acc = g_ref[0]
        for i in range(1, n):
            acc = acc + g_ref[i]
        o_ref[...] = acc

    return pl.pallas_call(body, out_shape=jax.ShapeDtypeStruct((rows, cols), F32), name="sum_devices")(gathered)


def _coords():
    return lax.axis_index("x"), lax.axis_index("y"), lax.axis_index("c")


def _chip_peers(x, y, c):
    out = []
    for fx, fy in ((0, 1), (1, 0), (1, 1)):
        px = 1 - x if fx else x
        py = 1 - y if fy else y
        out.append(((px, py, c), 2 * px + py))
    return out


class _ChipExchange:
    def __init__(self, kind, arrays):
        self.kind, self.kinds, self.arrays, self.n = kind, [kind] * len(arrays), list(arrays), len(arrays)
        self._shapes()

    def also(self, kind, arrays):
        self.kinds += [kind] * len(arrays)
        self.arrays += list(arrays)
        self.n = len(self.arrays)
        self._shapes()
        return self

    def _shapes(self):
        lead = {"gather": lambda a: (N_CHIPS,) + a.shape, "scatter": lambda a: (3,) + a.shape[1:],
                "swap": lambda a: a.shape}
        self.out_shape = [jax.ShapeDtypeStruct(lead[k](a), a.dtype) for k, a in zip(self.kinds, self.arrays)]
        for k, a in zip(self.kinds, self.arrays):
            assert k != "gather" or a.shape[0] % 2 == 0, "a gathered array is cut in two along its leading axis"
        self.scratch = [pltpu.SemaphoreType.DMA((6 * self.n,)), pltpu.SemaphoreType.DMA((6 * self.n,)),
                        pltpu.SemaphoreType.DMA((self.n,))]

    def copies(self, ins, outs, send_sems, recv_sems, local_sems):
        x, y, c = _coords()
        me = 2 * x + y
        sibling = (x, y, 1 - c)
        starts, waits, last = [], [], []
        for t, kind in enumerate(self.kinds):
            if kind == "swap":
                cp = pltpu.make_async_remote_copy(
                    src_ref=ins[t], dst_ref=outs[t], send_sem=send_sems.at[6 * t], recv_sem=recv_sems.at[6 * t],
                    device_id=sibling, device_id_type=MESH)
                starts.append(cp.start)
                waits += [cp.wait_send, cp.wait_recv]
                continue
            if kind == "gather":
                own = pltpu.make_async_copy(ins[t], outs[t].at[me], local_sems.at[t])
                starts.append(own.start)
                waits.append(own.wait)
                half_rows = ins[t].shape[0] // 2
                mine = pl.ds(c * half_rows, half_rows)
                theirs = pl.ds((1 - c) * half_rows, half_rows)
            for k, (peer, peer_chip) in enumerate(_chip_peers(x, y, c)):
                sems = dict(send_sem=send_sems.at[6 * t + k], recv_sem=recv_sems.at[6 * t + k],
                            device_id=peer, device_id_type=MESH)
                if kind == "scatter":
                    send = pltpu.make_async_remote_copy(src_ref=ins[t].at[peer_chip], dst_ref=outs[t].at[k], **sems)
                    starts.append(send.start)
                    waits += [send.wait_send, send.wait_recv]
                    continue
                send = pltpu.make_async_remote_copy(
                    src_ref=ins[t].at[mine], dst_ref=outs[t].at[me].at[mine], **sems)
                landed = outs[t].at[peer_chip].at[mine]
                recv = pltpu.make_async_remote_copy(src_ref=ins[t].at[mine], dst_ref=landed, **sems)
                pass_on = pltpu.make_async_remote_copy(
                    src_ref=landed, dst_ref=landed, send_sem=send_sems.at[6 * t + 3 + k],
                    recv_sem=recv_sems.at[6 * t + 3 + k], device_id=sibling, device_id_type=MESH)
                handed = pltpu.make_async_remote_copy(
                    src_ref=landed, dst_ref=outs[t].at[peer_chip].at[theirs], send_sem=send_sems.at[6 * t + 3 + k],
                    recv_sem=recv_sems.at[6 * t + 3 + k], device_id=sibling, device_id_type=MESH)
                starts.append(send.start)
                waits += [recv.wait_recv, pass_on.start]
                last += [send.wait_send, pass_on.wait_send, handed.wait_recv]
        return starts, waits + last

    def run(self, name):
        n = self.n

        def body(*refs):
            starts, waits = self.copies(refs[:n], refs[n:2 * n], *refs[2 * n:])
            for f in starts + waits:
                f()

        return pl.pallas_call(body, in_specs=[HBM_SPEC] * n, out_specs=[HBM_SPEC] * n, out_shape=self.out_shape,
                              scratch_shapes=self.scratch, name=name)(*self.arrays)


def _call_with_exchange(body, exchange, *, grid, in_specs, out_specs, out_shape, scratch_shapes, name, args,
                        sequential=False):
    if exchange is None:
        first_axis = "arbitrary" if sequential else "parallel"
        outs = pl.pallas_call(body, grid=grid, in_specs=in_specs, out_specs=out_specs, out_shape=out_shape,
                              scratch_shapes=scratch_shapes,
                              compiler_params=_params((first_axis,) + ("arbitrary",) * (len(grid) - 1)),
                              name=name)(*args)
        return outs, []
    n_in, n_out, n_scr, n = len(in_specs), len(out_specs), len(scratch_shapes), exchange.n

    def wrapped(*refs):
        ins, ex_in = refs[:n_in], refs[n_in:n_in + n]
        outs = refs[n_in + n:n_in + n + n_out]
        ex_out = refs[n_in + n + n_out:n_in + 2 * n + n_out]
        scr = refs[n_in + 2 * n + n_out:n_in + 2 * n + n_out + n_scr]
        sems = refs[n_in + 2 * n + n_out + n_scr:]
        ids = [pl.program_id(a) for a in range(len(grid))]
        first = functools.reduce(jnp.logical_and, [i == 0 for i in ids])
        last = functools.reduce(jnp.logical_and, [i == g - 1 for i, g in zip(ids, grid)])

        @pl.when(first)
        def _():
            for f in exchange.copies(ex_in, ex_out, *sems)[0]:
                f()

        body(*ins, *outs, *scr)

        @pl.when(last)
        def _():
            for f in exchange.copies(ex_in, ex_out, *sems)[1]:
                f()

    res = pl.pallas_call(
        wrapped, grid=grid, in_specs=list(in_specs) + [HBM_SPEC] * n, out_specs=list(out_specs) + [HBM_SPEC] * n,
        out_shape=list(out_shape) + exchange.out_shape, scratch_shapes=list(scratch_shapes) + exchange.scratch,
        compiler_params=_params(("arbitrary",) * len(grid)), name=name + "_" + exchange.kind,
    )(*args, *exchange.arrays)
    return res[:n_out], res[n_out:]


def _gather_devices(a):
    def body(in_ref, out_ref, send_sems, recv_sems, local_sem):
        x, y, c = _coords()
        me = 4 * x + 2 * y + c
        own = pltpu.make_async_copy(in_ref, out_ref.at[me], local_sem)
        own.start()
        waits = [own.wait]
        for k in range(1, N_DEVICES):
            px = 1 - x if k & 4 else x
            py = 1 - y if k & 2 else y
            pc = 1 - c if k & 1 else c
            peer = (px, py, pc)
            send = pltpu.make_async_remote_copy(
                src_ref=in_ref, dst_ref=out_ref.at[me], send_sem=send_sems.at[k - 1], recv_sem=recv_sems.at[k - 1],
                device_id=peer, device_id_type=MESH)
            send.start()
            recv = pltpu.make_async_remote_copy(
                src_ref=in_ref, dst_ref=out_ref.at[4 * px + 2 * py + pc], send_sem=send_sems.at[k - 1],
                recv_sem=recv_sems.at[k - 1], device_id=peer, device_id_type=MESH)
            waits += [send.wait_send, recv.wait_recv]
        for w in waits:
            w()

    return pl.pallas_call(
        body, in_specs=[HBM_SPEC], out_specs=HBM_SPEC,
        out_shape=jax.ShapeDtypeStruct((N_DEVICES,) + a.shape, a.dtype),
        scratch_shapes=[pltpu.SemaphoreType.DMA((N_DEVICES - 1,)), pltpu.SemaphoreType.DMA((N_DEVICES - 1,)),
                        pltpu.SemaphoreType.DMA],
        name="gather_devices",
    )(a)


def _mlp_grad_epilogue(r, u):
    return r * (2.0 * jnp.maximum(u, 0.0))


def kernel(x, norm_gains, sb_w_qkv, sb_q_gain, sb_k_gain, sb_w_o, hg_w_in, hg_lb_logits, hg_norm_gain, hg_w_o, mlp_w1, mlp_w2, loss_target, m_norm_gains, m_sb_w_qkv, m_sb_q_gain, m_sb_k_gain, m_sb_w_o, m_hg_w_in, m_hg_lb_logits, m_hg_norm_gain, m_hg_w_o, m_mlp_w1, m_mlp_w2, v_norm_gains, v_sb_w_qkv, v_sb_q_gain, v_sb_k_gain, v_sb_w_o, v_hg_w_in, v_hg_lb_logits, v_hg_norm_gain, v_hg_w_o, v_mlp_w1, v_mlp_w2):
    depth = norm_gains.shape[0]
    n_sb, n_hg = sb_w_qkv.shape[0], hg_w_in.shape[0]
    xs, tgt = x[0], loss_target[0]
    s, d = xs.shape
    dq = d // N_CHIPS
    cx, cy, cc = _coords()
    chip = 2 * cx + cy
    chip_arr = jnp.reshape(chip, (1,)).astype(jnp.int32)

    def mixer_weights(layer):
        j = layer // 2
        return (sb_w_qkv[j], sb_w_o[j]) if layer % 2 == 0 else (hg_w_in[j], hg_w_o[j])

    w_in_g, ng_g, lbl_g = _ChipExchange(
        "gather", [mixer_weights(0)[0].astype(BF16), norm_gains, hg_lb_logits]).run("gather_first")
    gains = jnp.transpose(ng_g, (1, 2, 0, 3)).reshape(depth, 2, d)
    logits = jnp.transpose(lbl_g, (1, 0, 2)).reshape(n_hg, d)
    lbs, lb_p = _lb_fwd(logits)
    qg_rows = [jnp.tile(sb_q_gain[j], d // SB_HEAD_DIM)[None] for j in range(n_sb)]
    kg_rows = [jnp.tile(sb_k_gain[j], d // SB_HEAD_DIM)[None] for j in range(n_sb)]

    saved, wts = [], []
    xc = xs
    for layer in range(depth):
        j = layer // 2
        ahead = [mixer_weights(layer)[1], mlp_w1[layer], mlp_w2[layer]]
        if layer + 1 < depth:
            ahead.append(mixer_weights(layer + 1)[0])
        gather = _ChipExchange("gather", [a.astype(BF16) for a in ahead])
        h1 = _rmsnorm_fwd(xc, gains[layer, 0][None])
        if layer % 2 == 0:
            qkv = _mm_fwd_cols(h1, w_in_g, name="sb_qkv")
            qn, kn, vb = _qk_norm_fwd(qkv, qg_rows[j], kg_rows[j])
            o, moved = _sb_attn_fwd(qn, kn, vb, gather)
            x_mid = _mm_fwd_rows(o, moved[0], residual=xc, name="sb_out")
            mix = (qkv, qn, kn, vb, o)
        else:
            proj = _mm_fwd_cols(h1, w_in_g, name="hg_in")
            (y, o, states), moved = _hg_fwd(proj, lbs[j][None], hg_norm_gain[j][None], gather)
            x_mid = _mm_fwd_rows(y, moved[0], residual=xc, name="hg_out")
            mix = (proj, y, o, states)
        w_out_g, w1_g, w2_g = moved[:3]
        h2 = _rmsnorm_fwd(x_mid, gains[layer, 1][None])
        u = _mm_fwd_cols(h2, w1_g, name="mlp_up")
        x_out = _mm_fwd_rows(u, w2_g, residual=x_mid, a_fn=_relu2, name="mlp_down")
        saved.append((xc, h1, mix, x_mid, h2, u))
        wts.append((w_in_g, w_out_g, w1_g, w2_g))
        w_in_g = moved[3] if layer + 1 < depth else None
        xc = x_out

    sq, dx = _loss_head(xc, tgt)
    loss = lax.psum(jnp.sum(sq) * (0.5 / d), ("x", "y", "c"))

    dgains = [[None, None] for _ in range(depth)]
    dqg, dkg = [None] * n_sb, [None] * n_sb
    dhgain, dlb = [None] * n_hg, [None] * n_hg
    grads, received, pending = {}, {}, []

    def ready(key, parts):
        grads[key] = parts
        pending.append(key)

    def scatter_of(keys):
        return _ChipExchange("scatter", [grads[k] for k in keys]) if keys else None

    def sent(keys, moved):
        for k, r in zip(keys, moved):
            received[k] = r
            pending.remove(k)

    def chip_sum(kind, layers):
        total = jnp.zeros((len(layers),) + grads[kind, layers[0]].shape[1:], F32)
        for index, l in enumerate(layers):
            total = _sum_slots(grads[kind, l], received[kind, l], chip_arr, total, index)
        return total

    sb_layers, hg_layers = range(0, depth, 2), range(1, depth, 2)
    tensors = [("in", sb_layers), ("out", sb_layers), ("in", hg_layers), ("out", hg_layers),
               ("w1", range(depth)), ("w2", range(depth))]

    for layer in reversed(range(depth)):
        j = layer // 2
        x_in, h1, mix, x_mid, h2, u = saved[layer]
        w_in_g, w_out_g, w1_g, w2_g = wts[layer]
        du = _mm_bwd_rows(dx, w2_g, name="mlp_down_dx", out_dtype=BF16, epi_fn=_mlp_grad_epilogue, epi_args=(u,))
        ready(("w2", layer), _mm_dw_rows(u, dx, a_fn=_relu2, name="mlp_down_dw"))
        ready(("w1", layer), _mm_dw_cols(h2, du, name="mlp_up_dw"))
        dx, dgains[layer][1] = _mm_bwd_cols_norm(du, w1_g, x_mid, gains[layer, 1][None], dx, name="mlp_up_dx")
        if layer % 2 == 0:
            qkv, qn, kn, vb, o = mix
            do = _mm_bwd_rows(dx, w_out_g, name="sb_out_dx")
            ready(("out", layer), _mm_dw_rows(o, dx, name="sb_out_dw"))
            keys = list(pending)
            (dqn, dkn, dv), moved = _sb_attn_bwd(qn, kn, vb, do, scatter_of(keys))
            sent(keys, moved)
            d_in, dqg[j], dkg[j] = _qk_norm_bwd(qkv, qg_rows[j], kg_rows[j], dqn, dkn, dv)
            ready(("in", layer), _mm_dw_cols(h1, d_in, name="sb_qkv_dw"))
            dx_name = "sb_qkv_dx"
        else:
            proj, y, o, states = mix
            dy = _mm_bwd_rows(dx, w_out_g, name="hg_out_dx")
            ready(("out", layer), _mm_dw_rows(y, dx, name="hg_out_dw"))
            keys = list(pending)
            (d_in, dlb[j], dhgain[j]), moved = _hg_bwd(
                proj, lbs[j][None], hg_norm_gain[j][None], o, states, dy, scatter_of(keys))
            sent(keys, moved)
            ready(("in", layer), _mm_dw_cols(h1, d_in, name="hg_in_dw"))
            dx_name = "hg_in_dx"
        if layer > 0:
            dx, dgains[layer][0] = _mm_bwd_cols_norm(d_in, w_in_g, x_in, gains[layer, 0][None], dx, name=dx_name)
        else:
            keys = list(pending)
            early_sums = [chip_sum(*t) for t in tensors[1:]]
            (dx, dgains[layer][0]), moved = _mm_bwd_cols_norm(
                d_in, w_in_g, x_in, gains[layer, 0][None], dx, name=dx_name,
                exchange=scatter_of(keys).also("swap", early_sums))
            sent(keys, moved[:len(keys)])
            early_other = moved[len(keys):]
    grad_x = dx[None]
    dlogits = _lb_bwd(lb_p, jnp.concatenate(dlb, axis=0))

    big_w = [sb_w_qkv, sb_w_o, hg_w_in, hg_w_o, mlp_w1, mlp_w2]
    big_m = [m_sb_w_qkv, m_sb_w_o, m_hg_w_in, m_hg_w_o, m_mlp_w1, m_mlp_w2]
    big_v = [v_sb_w_qkv, v_sb_w_o, v_hg_w_in, v_hg_w_o, v_mlp_w1, v_mlp_w2]
    late_sum = chip_sum(*tensors[0])
    big_second, late_other = _adamw(big_w[1], big_m[1], big_v[1], [early_sums[0], early_other[0]],
                                    _ChipExchange("swap", [late_sum]))
    big = [_adamw(big_w[0], big_m[0], big_v[0], [late_sum, late_other[0]]), big_second]
    big += [_adamw(w, m, v, [a, b]) for w, m, v, a, b in
            zip(big_w[2:], big_m[2:], big_v[2:], early_sums[1:], early_other[1:])]

    pieces = [r for pair in dgains for r in pair] + [dlogits] + dqg + dkg + dhgain
    small = _sum_devices(_gather_devices(_pack_rows(pieces)))
    my_cols = lambda a: lax.dynamic_slice_in_dim(a, chip * dq, dq, axis=1)
    fold = lambda rows, width: jnp.sum(rows.reshape(rows.shape[0], -1, width), axis=1)
    base = 2 * depth + n_hg
    g_ng = my_cols(small[0:2 * depth]).reshape(norm_gains.shape)
    g_lbl = my_cols(small[2 * depth:base])
    g_qg = fold(small[base:base + n_sb], SB_HEAD_DIM)
    g_kg = fold(small[base + n_sb:base + 2 * n_sb], SB_HEAD_DIM)
    g_hgn = fold(small[base + 2 * n_sb:base + 2 * n_sb + n_hg], HG_HEAD_DIM)
    r_ng = _adamw(norm_gains, m_norm_gains, v_norm_gains, [g_ng])
    r_qg = _adamw(sb_q_gain, m_sb_q_gain, v_sb_q_gain, [g_qg])
    r_kg = _adamw(sb_k_gain, m_sb_k_gain, v_sb_k_gain, [g_kg])
    r_lbl = _adamw(hg_lb_logits, m_hg_lb_logits, v_hg_lb_logits, [g_lbl])
    r_hgn = _adamw(hg_norm_gain, m_hg_norm_gain, v_hg_norm_gain, [g_hgn])

    per_weight = [r_ng, big[0], r_qg, r_kg, big[1], big[2], r_lbl, r_hgn, big[3], big[4], big[5]]
    outs = [loss, grad_x]
    for field in range(4):
        outs += [r[field] for r in per_weight]
    return tuple(outs)
```

```python
import functools
import math

import numpy as np
import jax
import jax.numpy as jnp
from jax import lax
from jax.experimental import pallas as pl
from jax.experimental.pallas import tpu as pltpu

F32 = jnp.float32
BF16 = jnp.bfloat16
GRAD_SLOT_DTYPE = jnp.bfloat16

NORM_EPS = 1e-6
SB_HEAD_DIM = 64
HG_HEAD_DIM = 128
HG_CHUNK = 128
LANES = 128
VMEM_LIMIT_BYTES = 56 * 2 ** 20
N_CHIPS = 4
N_DEVICES = 8

ADAM_LR = 0.001
ADAM_B1 = 0.9
ADAM_B2 = 0.999
ADAM_EPS = 1e-08
ADAM_WD = 0.01
ADAM_STEP = 10

MESH = pl.DeviceIdType.MESH
HBM_SPEC = pl.BlockSpec(memory_space=pltpu.HBM)

NN = (((1,), (0,)), ((), ()))
NT = (((1,), (1,)), ((), ()))
TN = (((0,), (0,)), ((), ()))


def _params(sem=None):
    return pltpu.CompilerParams(dimension_semantics=sem, vmem_limit_bytes=VMEM_LIMIT_BYTES)


def _pick(dim, pref):
    for t in (1024, 768, 512, 384, 256, 128, 64, 32, 16, 8):
        if t <= pref and dim % t == 0:
            return t
    return dim


def _dot(a, b, dims=NN):
    return lax.dot_general(a, b, dims, preferred_element_type=F32)


def _sigmoid(x):
    e = jnp.exp(-jnp.abs(x))
    return jnp.where(x >= 0, 1.0, e) / (1.0 + e)


def _matmul(a, b, *, mode, grid, a_block, a_map, b_block, b_map, o_block, o_map, out_shape, out_dtype, name,
            a_fn=None, epi_fn=None, epi_args=(), epi_row_args=(), col_sums=False, exchange=None):
    nk = grid[2]
    dims = {"nn": NN, "nt": NT, "tn": TN}[mode]
    n_epi = len(epi_args) + len(epi_row_args)
    n_out = 2 if col_sums else 1
    tn = o_block[-1]
    assert not col_sums or grid[1] == 1, "the column sums stay resident only with one tile along N"

    def body(a_ref, b_ref, *rest):
        epi_refs = rest[:n_epi]
        o_ref = rest[n_epi]
        kk = pl.program_id(2)

        def emit(r):
            if epi_fn is not None:
                r = epi_fn(r, *[e[...] for e in epi_refs])
            if col_sums:
                r, row = r
                sums_ref = rest[n_epi + 1]
                first = pl.program_id(0) == 0

                @pl.when(first)
                def _():
                    sums_ref[...] = row

                @pl.when(jnp.logical_not(first))
                def _():
                    sums_ref[...] += row
            o_ref[...] = r.astype(o_ref.dtype)

        av = a_ref[...]
        if a_fn is not None:
            av = a_fn(av)
        part = _dot(av.astype(BF16), b_ref[...].astype(BF16), dims)
        if nk == 1:
            emit(part)
            return
        acc_ref = rest[n_epi + n_out]

        @pl.when(kk == 0)
        def _():
            acc_ref[...] = part

        @pl.when(kk > 0)
        def _():
            acc_ref[...] += part

        @pl.when(kk == nk - 1)
        def _():
            emit(acc_ref[...])

    acc_shape = tuple(d for d in o_block if d is not None)
    row_spec = pl.BlockSpec((1, tn), lambda i, j, kk: (0, j))
    in_specs = [pl.BlockSpec(a_block, a_map), pl.BlockSpec(b_block, b_map)]
    in_specs += [pl.BlockSpec(o_block, o_map) for _ in epi_args] + [row_spec for _ in epi_row_args]
    out_specs, out_shapes = [pl.BlockSpec(o_block, o_map)], [jax.ShapeDtypeStruct(out_shape, out_dtype)]
    if col_sums:
        out_specs, out_shapes = out_specs + [row_spec], out_shapes + [jax.ShapeDtypeStruct((1, out_shape[-1]), F32)]
    outs, moved = _call_with_exchange(
        body, exchange, grid=grid, in_specs=in_specs, out_specs=out_specs, out_shape=out_shapes,
        scratch_shapes=[pltpu.VMEM(acc_shape, F32)] if nk > 1 else [], name=name,
        args=(a, b, *epi_args, *epi_row_args), sequential=col_sums)
    result = tuple(outs) if col_sums else outs[0]
    return result if exchange is None else (result, moved)


def _relu2(u):
    r = jnp.maximum(u, 0.0)
    return r * r


def _add(r, res):
    return r + res


def _mm_fwd_cols(a, wg, *, name):
    s, k = a.shape
    ncs = wg.shape[2]
    tm, tk, tn = _pick(s, 1024), _pick(k, 1024), _pick(ncs, 1024)
    npb = ncs // tn
    return _matmul(a, wg, mode="nn", grid=(s // tm, N_CHIPS * npb, k // tk),
                   a_block=(tm, tk), a_map=lambda i, j, kk: (i, kk),
                   b_block=(None, tk, tn), b_map=lambda i, j, kk: (j // npb, kk, j % npb),
                   o_block=(tm, tn), o_map=lambda i, j, kk: (i, j),
                   out_shape=(s, N_CHIPS * ncs), out_dtype=F32, name=name)


def _rows_joined(wg):
    assert wg.shape[1] % 16 == 0, "joining the leading axes must not cross a tile of 16 rows"
    return wg.reshape(wg.shape[0] * wg.shape[1], wg.shape[2])


def _mm_fwd_rows(a, wg, *, residual, name, a_fn=None):
    s = a.shape[0]
    w = _rows_joined(wg)
    k, n = w.shape
    tm, tk, tn = _pick(s, 1024), _pick(k, 1024), _pick(n, 1024)
    return _matmul(a, w, mode="nn", grid=(s // tm, n // tn, k // tk),
                   a_block=(tm, tk), a_map=lambda i, j, kk: (i, kk),
                   b_block=(tk, tn), b_map=lambda i, j, kk: (kk, j),
                   o_block=(tm, tn), o_map=lambda i, j, kk: (i, j),
                   out_shape=(s, n), out_dtype=F32, name=name, a_fn=a_fn, epi_fn=_add, epi_args=(residual,))


def _rmsnorm_grad(dh, x, dx_res, gain):
    r = lax.rsqrt(jnp.mean(x * x, axis=-1, keepdims=True) + NORM_EPS)
    xhat = x * r
    dxhat = dh * gain
    dx = r * (dxhat - xhat * jnp.mean(dxhat * xhat, axis=-1, keepdims=True))
    return dx_res + dx, jnp.sum(dh * xhat, axis=0, keepdims=True)


def _mm_bwd_cols_norm(dy, wg, x, gain_row, dx_res, *, name, exchange=None):
    by_slot = dy.ndim == 3
    s = dy.shape[1] if by_slot else dy.shape[0]
    kw, ncs = wg.shape[1], wg.shape[2]
    tm, tk = _pick(s, 512), _pick(ncs, 1024)
    kpb = ncs // tk
    a_block, a_map = ((None, tm, tk), lambda i, j, kk: (kk // kpb, i, kk % kpb)) if by_slot else (
        (tm, tk), lambda i, j, kk: (i, kk))
    return _matmul(dy, wg, mode="nt", grid=(s // tm, 1, N_CHIPS * kpb),
                   a_block=a_block, a_map=a_map,
                   b_block=(None, kw, tk), b_map=lambda i, j, kk: (kk // kpb, j, kk % kpb),
                   o_block=(tm, kw), o_map=lambda i, j, kk: (i, j),
                   out_shape=(s, kw), out_dtype=F32, name=name, exchange=exchange,
                   epi_fn=_rmsnorm_grad, epi_args=(x, dx_res), epi_row_args=(gain_row,), col_sums=True)


def _mm_bwd_rows(dy, wg, *, name, out_dtype=F32, epi_fn=None, epi_args=()):
    s, n = dy.shape
    w = _rows_joined(wg)
    rows = w.shape[0]
    tm, tn, tk = _pick(s, 1024), _pick(rows, 1024), _pick(n, 1024)
    return _matmul(dy, w, mode="nt", grid=(s // tm, rows // tn, n // tk),
                   a_block=(tm, tk), a_map=lambda i, j, kk: (i, kk),
                   b_block=(tn, tk), b_map=lambda i, j, kk: (j, kk),
                   o_block=(tm, tn), o_map=lambda i, j, kk: (i, j),
                   out_shape=(s, rows), out_dtype=out_dtype, name=name, epi_fn=epi_fn, epi_args=epi_args)


def _mm_dw_cols(xa, dy, *, name):
    s, kx = xa.shape
    by_slot = dy.ndim == 3
    ncs = dy.shape[2] if by_slot else dy.shape[1] // N_CHIPS
    tm, tn, tk = _pick(kx, 1024), _pick(ncs, 1024), _pick(s, 1024)
    npb = ncs // tn
    b_block, b_map = ((None, tk, tn), lambda i, j, kk: (j // npb, kk, j % npb)) if by_slot else (
        (tk, tn), lambda i, j, kk: (kk, j))
    return _matmul(xa, dy, mode="tn", grid=(kx // tm, N_CHIPS * npb, s // tk),
                   a_block=(tk, tm), a_map=lambda i, j, kk: (kk, i),
                   b_block=b_block, b_map=b_map,
                   o_block=(None, tm, tn), o_map=lambda i, j, kk: (j // npb, i, j % npb),
                   out_shape=(N_CHIPS, kx, ncs), out_dtype=GRAD_SLOT_DTYPE, name=name)


def _mm_dw_rows(xa, dy, *, name, a_fn=None):
    s, n = dy.shape
    rows = xa.shape[1]
    assert (rows // N_CHIPS) % 16 == 0, "splitting the rows into slots must not cut a tile of 16 rows"
    tm, tn, tk = _pick(rows, 1024), _pick(n, 1024), _pick(s, 1024)
    dw = _matmul(xa, dy, mode="tn", grid=(rows // tm, n // tn, s // tk),
                 a_block=(tk, tm), a_map=lambda i, j, kk: (kk, i),
                 b_block=(tk, tn), b_map=lambda i, j, kk: (kk, j),
                 o_block=(tm, tn), o_map=lambda i, j, kk: (i, j),
                 out_shape=(rows, n), out_dtype=GRAD_SLOT_DTYPE, name=name, a_fn=a_fn)
    return dw.reshape(N_CHIPS, rows // N_CHIPS, n)


def _rmsnorm_fwd(x, gain_row):
    s, d = x.shape
    ts = _pick(s, 512)

    def body(x_ref, g_ref, h_ref):
        xv = x_ref[...]
        r = lax.rsqrt(jnp.mean(xv * xv, axis=-1, keepdims=True) + NORM_EPS)
        h_ref[...] = (xv * r * g_ref[...]).astype(h_ref.dtype)

    return pl.pallas_call(
        body, grid=(s // ts,),
        in_specs=[pl.BlockSpec((ts, d), lambda i: (i, 0)), pl.BlockSpec((1, d), lambda i: (0, 0))],
        out_specs=pl.BlockSpec((ts, d), lambda i: (i, 0)),
        out_shape=jax.ShapeDtypeStruct((s, d), BF16),
        compiler_params=_params(("parallel",)), name="rmsnorm_fwd",
    )(x, gain_row)


def _loss_head(y, target):
    s, d = y.shape
    ts = _pick(s, 512)

    def body(y_ref, t_ref, sq_ref, dy_ref):
        i = pl.program_id(0)
        err = y_ref[...] - t_ref[...]
        dy_ref[...] = err / d
        part = jnp.sum(err * err, axis=0, keepdims=True)

        @pl.when(i == 0)
        def _():
            sq_ref[...] = part

        @pl.when(i > 0)
        def _():
            sq_ref[...] += part

    return pl.pallas_call(
        body, grid=(s // ts,),
        in_specs=[pl.BlockSpec((ts, d), lambda i: (i, 0)), pl.BlockSpec((ts, d), lambda i: (i, 0))],
        out_specs=[pl.BlockSpec((1, d), lambda i: (0, 0)), pl.BlockSpec((ts, d), lambda i: (i, 0))],
        out_shape=[jax.ShapeDtypeStruct((1, d), F32), jax.ShapeDtypeStruct((s, d), F32)],
        compiler_params=_params(("arbitrary",)), name="loss_head",
    )(y, target)


def _pair_ones():
    lane = np.arange(LANES)
    same_half = (lane[:, None] // SB_HEAD_DIM == lane[None, :] // SB_HEAD_DIM).astype(np.float32)
    return jnp.asarray(np.concatenate([same_half, same_half], axis=0), BF16)


def _pair_mean(val, pair_ones):
    return _dot(jnp.concatenate(_split2(val), axis=1), pair_ones) * (1.0 / SB_HEAD_DIM)


def _qk_norm_fwd(qkv, qgain_row, kgain_row):
    s, d3 = qkv.shape
    d = d3 // 3
    ts = _pick(s, 512)
    groups = d // LANES

    def body(q_ref, k_ref, v_ref, qg_ref, kg_ref, ones_ref, qn_ref, kn_ref, vb_ref):
        for src, gain, dst in ((q_ref, qg_ref, qn_ref), (k_ref, kg_ref, kn_ref)):
            for p in range(groups):
                cols = slice(p * LANES, (p + 1) * LANES)
                xp = src[:, cols]
                r = lax.rsqrt(_pair_mean(xp * xp, ones_ref[...]) + NORM_EPS)
                dst[:, cols] = (xp * r * gain[:, cols]).astype(dst.dtype)
        vb_ref[...] = v_ref[...].astype(vb_ref.dtype)

    tok = lambda c: pl.BlockSpec((ts, d), lambda i: (i, c))
    row = pl.BlockSpec((1, d), lambda i: (0, 0))
    return pl.pallas_call(
        body, grid=(s // ts,),
        in_specs=[tok(0), tok(1), tok(2), row, row, pl.BlockSpec((2 * LANES, LANES), lambda i: (0, 0))],
        out_specs=[tok(0), tok(0), tok(0)],
        out_shape=[jax.ShapeDtypeStruct((s, d), BF16)] * 3,
        compiler_params=_params(("parallel",)), name="qk_norm_fwd",
    )(qkv, qkv, qkv, qgain_row, kgain_row, _pair_ones())


def _qk_norm_bwd(qkv, qgain_row, kgain_row, dqn, dkn, dv):
    s, d3 = qkv.shape
    d = d3 // 3
    ts = _pick(s, 512)
    groups = d // LANES

    def body(q_ref, k_ref, qg_ref, kg_ref, dqn_ref, dkn_ref, dv_ref, ones_ref, dqkv_ref, dqg_ref, dkg_ref):
        i = pl.program_id(0)
        for which, (src, gain, dsrc, dgain) in enumerate(((q_ref, qg_ref, dqn_ref, dqg_ref),
                                                          (k_ref, kg_ref, dkn_ref, dkg_ref))):
            for p in range(groups):
                cols = slice(p * LANES, (p + 1) * LANES)
                xp = src[:, cols]
                r = lax.rsqrt(_pair_mean(xp * xp, ones_ref[...]) + NORM_EPS)
                xhat = xp * r
                dy = dsrc[:, cols]
                dxhat = dy * gain[:, cols]
                dx = r * (dxhat - xhat * _pair_mean(dxhat * xhat, ones_ref[...]))
                dqkv_ref[:, which * d + p * LANES: which * d + (p + 1) * LANES] = dx.astype(dqkv_ref.dtype)
                part = jnp.sum(dy * xhat, axis=0, keepdims=True)

                @pl.when(i == 0)
                def _():
                    dgain[:, cols] = part

                @pl.when(i > 0)
                def _():
                    dgain[:, cols] += part
        dqkv_ref[:, 2 * d:] = dv_ref[...].astype(dqkv_ref.dtype)

    tok = lambda c: pl.BlockSpec((ts, d), lambda i: (i, c))
    row = pl.BlockSpec((1, d), lambda i: (0, 0))
    return pl.pallas_call(
        body, grid=(s // ts,),
        in_specs=[tok(0), tok(1), row, row, tok(0), tok(0), tok(0),
                  pl.BlockSpec((2 * LANES, LANES), lambda i: (0, 0))],
        out_specs=[pl.BlockSpec((ts, d3), lambda i: (i, 0)), row, row],
        out_shape=[jax.ShapeDtypeStruct((s, d3), BF16), jax.ShapeDtypeStruct((1, d), F32),
                   jax.ShapeDtypeStruct((1, d), F32)],
        compiler_params=_params(("arbitrary",)), name="qk_norm_bwd",
    )(qkv, qkv, qgain_row, kgain_row, dqn, dkn, dv, _pair_ones())


def _split2(x):
    hi = x.astype(BF16)
    lo = (x - hi.astype(F32)).astype(BF16)
    return hi, lo


SB_TK = 128


def _sb_consts(tk):
    j = np.arange(tk)
    ones = np.ones((tk, tk), np.float32)
    out = []
    for tri in ((j[:, None] >= j[None, :]), (j[:, None] <= j[None, :])):
        half = np.concatenate([tri.astype(np.float32), ones], axis=1)
        out.append(jnp.asarray(np.concatenate([half, half], axis=0), BF16))
    return out


def _head_stack(blk, low_half):
    f = blk.astype(F32)
    return jnp.concatenate([jnp.where(low_half, f, 0.0), jnp.where(low_half, 0.0, f)], axis=0).astype(BF16)


def _sb_tile_sums(z, valid, tri2):
    e = jnp.exp(-jnp.abs(z))
    lstay = jnp.minimum(-z, 0.0) - jnp.log(1.0 + e)
    if valid is not None:
        lstay = jnp.where(valid, lstay, 0.0)
    hi, lo = _split2(lstay)
    return e, _dot(jnp.concatenate([hi, lo], axis=1), tri2)


def _sb_weights(z, c2, valid, run):
    w = jnp.exp(z + c2[:, :SB_TK] + run)
    return w if valid is None else jnp.where(valid, w, 0.0)


EXP_IS_ZERO_BELOW = -110.0


def _max_row_norm(x):
    f = x.astype(F32)
    return jnp.sqrt(jnp.max(jnp.sum(f * f, axis=-1, keepdims=True)))


def _sb_score_bound(qs, kmax_ref):
    return _max_row_norm(qs) * jnp.max(kmax_ref[...]) * 1.01 + 1.0


def _sb_rest_is_zero(run_ref, bound):
    return jnp.max(jnp.maximum(run_ref[0], run_ref[1])) + bound < EXP_IS_ZERO_BELOW


def _sb_attn_fwd(qn, kn, vb, exchange=None):
    s, d = qn.shape
    tk = SB_TK
    tq = _pick(s, 256)
    nq, ndiag = s // tq, tq // tk
    assert tq % (2 * tk) == 0, "tiles below the diagonal are taken two at a time"
    npairs = d // LANES
    scale = 1.0 / math.sqrt(SB_HEAD_DIM)
    tri_ge2, _ = _sb_consts(tk)

    def body(q_ref, k_ref, v_ref, tri_ref, o_ref, acc_ref, run_ref, kmax_ref):
        qi = pl.program_id(1)

        @pl.when(qi == 0)
        def _():
            kmax_ref[...] = jnp.full(kmax_ref.shape, _max_row_norm(k_ref[...]), F32)

        low_half = lax.broadcasted_iota(jnp.int32, (tk, LANES), 1) < SB_HEAD_DIM
        row = lax.broadcasted_iota(jnp.int32, (tq, tk), 0)
        col = lax.broadcasted_iota(jnp.int32, (tq, tk), 1)
        qs = (q_ref[...].astype(F32) * scale).astype(BF16)
        bound = _sb_score_bound(qs, kmax_ref)
        acc_ref[...] = jnp.zeros_like(acc_ref)
        run_ref[...] = jnp.zeros_like(run_ref)
        n_full = qi * ndiag

        def sums(kb, dd):
            koff = pl.multiple_of(kb * tk, tk)
            kcat = _head_stack(k_ref[pl.ds(koff, tk), :], low_half)
            vcat = _head_stack(v_ref[pl.ds(koff, tk), :], low_half)
            z2 = _dot(qs, kcat, NT)
            valid = None if dd is None else row > col + dd * tk
            zs = [z2[:, h * tk:(h + 1) * tk] for h in range(2)]
            return zs, [_sb_tile_sums(z, valid, tri_ref[...])[1] for z in zs], valid, vcat

        def finish(zs, c2s, valid, vcat):
            ws = []
            for h in range(2):
                ws.append(_sb_weights(zs[h], c2s[h], valid, run_ref[h]).astype(BF16))
                run_ref[h] += c2s[h][:, tk:]
            acc_ref[...] += _dot(jnp.concatenate(ws, axis=1), vcat)

        def first_tiles(below):
            pres = [sums(n_full + dd, dd) for dd in reversed(range(ndiag))]
            pres += [sums(n_full - 1 - n, None) for n in range(below)]
            for pre in pres:
                finish(*pre)

        @pl.when(qi == 0)
        def _():
            first_tiles(0)

        @pl.when(qi > 0)
        def _():
            first_tiles(2)

        def two_tiles(carry):
            it, _ = carry
            kb = n_full - 1 - 2 * it
            first, second = sums(kb, None), sums(kb - 1, None)
            finish(*first)
            finish(*second)
            return it + 1, _sb_rest_is_zero(run_ref, bound)

        lax.while_loop(lambda c: jnp.logical_and(c[0] < n_full // 2, jnp.logical_not(c[1])), two_tiles,
                       (jnp.minimum(qi, 1), _sb_rest_is_zero(run_ref, bound)))
        o_ref[...] = acc_ref[...]

    blk = pl.BlockSpec((tq, LANES), lambda p, i: (i, p))
    full = pl.BlockSpec((s, LANES), lambda p, i: (0, p))
    (o,), moved = _call_with_exchange(
        body, exchange, grid=(npairs, nq),
        in_specs=[blk, full, full, pl.BlockSpec((2 * tk, 2 * tk), lambda p, i: (0, 0))],
        out_specs=[blk], out_shape=[jax.ShapeDtypeStruct((s, d), F32)],
        scratch_shapes=[pltpu.VMEM((tq, LANES), F32), pltpu.VMEM((2, tq, tk), F32), pltpu.VMEM((8, LANES), F32)],
        name="sb_attn_fwd", args=(qn, kn, vb, tri_ge2))
    return o, moved


def _sb_attn_bwd(qn, kn, vb, do, exchange=None):
    s, d = qn.shape
    tk = SB_TK
    tq = _pick(s, 256)
    nq, ndiag = s // tq, tq // tk
    assert tq % (2 * tk) == 0, "tiles below the diagonal are taken two at a time"
    npairs = d // LANES
    scale = 1.0 / math.sqrt(SB_HEAD_DIM)
    tri_ge2, tri_le2 = _sb_consts(tk)

    def body(q_ref, k_ref, v_ref, do_ref, tge_ref, tle_ref, dq_ref, dk_ref, dv_ref,
             g_cache, s_cache, run_ref, dq_acc, kmax_ref):
        qi = pl.program_id(1)

        @pl.when(qi == 0)
        def _():
            dk_ref[...] = jnp.zeros_like(dk_ref)
            dv_ref[...] = jnp.zeros_like(dv_ref)
            kmax_ref[...] = jnp.full(kmax_ref.shape, _max_row_norm(k_ref[...]), F32)

        low_half = lax.broadcasted_iota(jnp.int32, (tk, LANES), 1) < SB_HEAD_DIM
        row = lax.broadcasted_iota(jnp.int32, (tq, tk), 0)
        col = lax.broadcasted_iota(jnp.int32, (tq, tk), 1)
        qs = (q_ref[...].astype(F32) * scale).astype(BF16)
        bound = _sb_score_bound(qs, kmax_ref)
        dob = do_ref[...].astype(BF16)
        n_full = qi * ndiag

        def a_sums(kb, dd):
            koff = pl.multiple_of(kb * tk, tk)
            kcat = _head_stack(k_ref[pl.ds(koff, tk), :], low_half)
            vcat = _head_stack(v_ref[pl.ds(koff, tk), :], low_half)
            z2 = _dot(qs, kcat, NT)
            dw2 = _dot(dob, vcat, NT)
            valid = None if dd is None else row > col + dd * tk
            c2s = []
            for h in range(2):
                cols = slice(h * tk, (h + 1) * tk)
                z = z2[:, cols]
                e, c2 = _sb_tile_sums(z, valid, tge_ref[...])
                s_cache[kb, :, cols] = jnp.where(z >= 0, 1.0, e) / (1.0 + e)
                c2s.append(c2)
            return kb, koff, z2, dw2, c2s, valid

        def a_finish(kb, koff, z2, dw2, c2s, valid):
            ws = []
            for h in range(2):
                cols = slice(h * tk, (h + 1) * tk)
                w = _sb_weights(z2[:, cols], c2s[h], valid, run_ref[h])
                run_ref[h] += c2s[h][:, tk:]
                g_cache[kb, :, cols] = w * dw2[:, cols]
                ws.append(w.astype(BF16))
            dv2 = _dot(jnp.concatenate(ws, axis=1), dob, TN)
            dv_ref[pl.ds(koff, tk), :] += jnp.where(low_half, dv2[:tk], dv2[tk:])

        def b_sums(kb, dd):
            gs = [g_cache[kb, :, h * tk:(h + 1) * tk] for h in range(2)]
            p2s = [_dot(jnp.concatenate(_split2(g), axis=1), tle_ref[...]) for g in gs]
            return kb, gs, p2s, (None if dd is None else row > col + dd * tk)

        def b_finish(kb, gs, p2s, valid):
            koff = pl.multiple_of(kb * tk, tk)
            dzs = []
            for h in range(2):
                dz = gs[h] - s_cache[kb, :, h * tk:(h + 1) * tk] * (p2s[h][:, :tk] + run_ref[h])
                if valid is not None:
                    dz = jnp.where(valid, dz, 0.0)
                run_ref[h] += p2s[h][:, tk:]
                dzs.append(dz.astype(BF16))
            dzcat = jnp.concatenate(dzs, axis=1)
            dq_acc[...] += _dot(dzcat, _head_stack(k_ref[pl.ds(koff, tk), :], low_half))
            dk2 = _dot(dzcat, qs, TN)
            dk_ref[pl.ds(koff, tk), :] += jnp.where(low_half, dk2[:tk], dk2[tk:])

        run_ref[...] = jnp.zeros_like(run_ref)
        near = jnp.minimum(qi, 1)

        def a_first_tiles(below):
            pres = [a_sums(n_full + dd, dd) for dd in reversed(range(ndiag))]
            pres += [a_sums(n_full - 1 - n, None) for n in range(below)]
            for pre in pres:
                a_finish(*pre)

        @pl.when(qi == 0)
        def _():
            a_first_tiles(0)

        @pl.when(qi > 0)
        def _():
            a_first_tiles(2)

        def two_a(carry):
            it, _ = carry
            kb = n_full - 1 - 2 * it
            first, second = a_sums(kb, None), a_sums(kb - 1, None)
            a_finish(*first)
            a_finish(*second)
            return it + 1, _sb_rest_is_zero(run_ref, bound)

        trips, _ = lax.while_loop(lambda c: jnp.logical_and(c[0] < n_full // 2, jnp.logical_not(c[1])), two_a,
                                  (near, _sb_rest_is_zero(run_ref, bound)))

        run_ref[...] = jnp.zeros_like(run_ref)
        dq_acc[...] = jnp.zeros_like(dq_acc)
        kb_first = n_full - 2 * trips

        def two_b(it, carry):
            first, second = b_sums(kb_first + 2 * it, None), b_sums(kb_first + 2 * it + 1, None)
            b_finish(*first)
            b_finish(*second)
            return carry

        lax.fori_loop(0, trips - near, two_b, 0)

        def b_last_tiles(below):
            pres = [b_sums(n_full - below + n, None) for n in range(below)]
            pres += [b_sums(n_full + dd, dd) for dd in range(ndiag)]
            for pre in pres:
                b_finish(*pre)

        @pl.when(qi == 0)
        def _():
            b_last_tiles(0)

        @pl.when(qi > 0)
        def _():
            b_last_tiles(2)

        dq_ref[...] = dq_acc[...] * scale

    blk = pl.BlockSpec((tq, LANES), lambda p, i: (i, p))
    full = pl.BlockSpec((s, LANES), lambda p, i: (0, p))
    tri = pl.BlockSpec((2 * tk, 2 * tk), lambda p, i: (0, 0))
    return _call_with_exchange(
        body, exchange, grid=(npairs, nq),
        in_specs=[blk, full, full, blk, tri, tri],
        out_specs=[blk, full, full],
        out_shape=[jax.ShapeDtypeStruct((s, d), F32)] * 3,
        scratch_shapes=[pltpu.VMEM((s // tk, tq, 2 * tk), F32), pltpu.VMEM((s // tk, tq, 2 * tk), F32),
                        pltpu.VMEM((2, tq, tk), F32), pltpu.VMEM((tq, LANES), F32), pltpu.VMEM((8, LANES), F32)],
        name="sb_attn_bwd", args=(qn, kn, vb, do, tri_ge2, tri_le2))


def _hg_consts(c):
    levels = []
    h = c // 2
    while h >= 1:
        levels.append(h)
        h //= 2
    t = np.arange(c)
    j = t[None, :]
    rows, masks = [], []
    for h in levels:
        blk = t // (2 * h)
        mid = blk * 2 * h + h - 1
        second = (t % (2 * h)) >= h
        rows.append(second[:, None] & (j > mid[:, None]) & (j <= t[:, None]))
        rows.append((~second)[:, None] & (j > t[:, None]) & (j <= mid[:, None]))
        masks.append((blk[:, None] == blk[None, :]) & second[:, None] & (~second)[None, :])
    rows.append(j <= t[:, None])
    rows.append(j > t[:, None])
    masks.append(t[:, None] == t[None, :])
    m_all = np.concatenate(rows, axis=0).astype(np.float32)
    mask_all = np.stack(masks, axis=0).astype(np.float32)
    suffix = (t[None, :] >= t[:, None]).astype(np.float32)
    return len(levels), jnp.asarray(m_all, BF16), jnp.asarray(mask_all, F32), jnp.asarray(suffix, BF16)


def _split3(x):
    hi = x.astype(BF16)
    r1 = x - hi.astype(F32)
    mid = r1.astype(BF16)
    lo = (r1 - mid.astype(F32)).astype(BF16)
    return jnp.concatenate([hi, mid, lo], axis=1)


def _join3(e):
    n = e.shape[1] // 3
    return e[:, :n] + e[:, n:2 * n] + e[:, 2 * n:]


def _hg_gates(qr, fr, lb):
    sq = _sigmoid(qr)
    sf = _sigmoid(fr)
    forget = lb + (1.0 - lb) * sf
    return qr * sq, sq, sf, forget, jnp.log(forget), 1.0 - forget


def _hg_scores(q, k, expo, masks, nlev, c):
    qb, kb = q.astype(BF16), k.astype(BF16)
    a = masks[nlev] * _dot(qb, kb, NT)
    scaled = []
    for li in range(nlev):
        fq = jnp.exp(expo[(2 * li) * c:(2 * li + 1) * c])
        fk = jnp.exp(expo[(2 * li + 1) * c:(2 * li + 2) * c])
        qs, ks = (q * fq).astype(BF16), (k * fk).astype(BF16)
        a = a + masks[li] * _dot(qs, ks, NT)
        scaled.append((qs, ks, fq, fk))
    return a, scaled, qb, kb


def _hg_heads_per_step(nh):
    return 2 if nh % 2 == 0 else 1


def _hg_fwd(proj, lb_row, gain_row, exchange=None):
    s, d4 = proj.shape
    d = d4 // 4
    nh = d // HG_HEAD_DIM
    c = min(HG_CHUNK, s)
    tb = _pick(s, 512)
    ncb = tb // c
    nlev, m_all, mask_all, _ = _hg_consts(c)
    nrow = m_all.shape[0]

    hp = _hg_heads_per_step(nh)
    wide = hp * HG_HEAD_DIM

    def body(q_ref, f_ref, i_ref, g_ref, lb_ref, gain_ref, mall_ref, mask_ref, y_ref, o_ref, st_out_ref, st_ref):
        b = pl.program_id(1)

        @pl.when(b == 0)
        def _():
            st_ref[...] = jnp.zeros_like(st_ref)

        gain = gain_ref[...]

        def inside(ci, hh):
            rows = pl.ds(pl.multiple_of(ci * c, c), c)
            cols = slice(hh * HG_HEAD_DIM, (hh + 1) * HG_HEAD_DIM)
            q, _, _, _, lf, k = _hg_gates(q_ref[rows, cols], f_ref[rows, cols], lb_ref[:, cols])
            v = i_ref[rows, cols].astype(BF16)
            expo = _join3(_dot(mall_ref[...], _split3(lf)))
            a, _, _, _ = _hg_scores(q, k, expo, mask_ref[...], nlev, c)
            b_cum = expo[2 * nlev * c:(2 * nlev + 1) * c]
            e_tail = expo[(2 * nlev + 1) * c:(2 * nlev + 2) * c]
            q_in = (q * jnp.exp(b_cum)).astype(BF16)
            k_dec = (k * jnp.exp(e_tail)).astype(BF16)
            return ci, hh, rows, cols, q_in, _dot(a.astype(BF16), v), jnp.exp(b_cum[c - 1:c, :]), _dot(v, k_dec, TN)

        def across(ci, hh, rows, cols, q_in, o_intra, decay, kv):
            st = st_ref[hh]
            st_out_ref[ci, hh] = st
            o = _dot(q_in, st.astype(BF16), NT) + o_intra
            st_ref[hh] = st * decay + kv
            o_ref[rows, cols] = o
            r = lax.rsqrt(jnp.mean(o * o, axis=-1, keepdims=True) + NORM_EPS)
            y_ref[rows, cols] = (o * r * gain * _sigmoid(g_ref[rows, cols])).astype(y_ref.dtype)

        per_trip = 2 if ncb % 2 == 0 else 1

        def trip(it, carry):
            ready = [inside(per_trip * it + n, hh) for n in range(per_trip) for hh in range(hp)]
            for r in ready:
                across(*r)
            return carry

        lax.fori_loop(0, ncb // per_trip, trip, 0)

    part = lambda k: pl.BlockSpec((tb, wide), lambda h, b: (b, k * (nh // hp) + h))
    head_row = pl.BlockSpec((1, wide), lambda h, b: (0, h))
    tok = pl.BlockSpec((tb, wide), lambda h, b: (b, h))
    return _call_with_exchange(
        body, exchange, grid=(nh // hp, s // tb),
        in_specs=[part(0), part(1), part(2), part(3), head_row,
                  pl.BlockSpec((1, HG_HEAD_DIM), lambda h, b: (0, 0)),
                  pl.BlockSpec((nrow, c), lambda h, b: (0, 0)),
                  pl.BlockSpec((nlev + 1, c, c), lambda h, b: (0, 0, 0))],
        out_specs=[tok, tok, pl.BlockSpec((ncb, hp, HG_HEAD_DIM, HG_HEAD_DIM), lambda h, b: (b, h, 0, 0))],
        out_shape=[jax.ShapeDtypeStruct((s, d), BF16), jax.ShapeDtypeStruct((s, d), F32),
                   jax.ShapeDtypeStruct((s // c, nh, HG_HEAD_DIM, HG_HEAD_DIM), F32)],
        scratch_shapes=[pltpu.VMEM((hp, HG_HEAD_DIM, HG_HEAD_DIM), F32)],
        name="hg_fwd", args=(proj, proj, proj, proj, lb_row, gain_row, m_all, mask_all))


def _hg_bwd(proj, lb_row, gain_row, o_saved, states, dy, exchange=None):
    s, d4 = proj.shape
    d = d4 // 4
    nh = d // HG_HEAD_DIM
    c = min(HG_CHUNK, s)
    tb = _pick(s, 512)
    ncb = tb // c
    nb = s // tb
    nlev, m_all, mask_all, suffix = _hg_consts(c)
    nrow = m_all.shape[0]
    hp = _hg_heads_per_step(nh)
    wide = hp * HG_HEAD_DIM

    def body(q_ref, f_ref, i_ref, g_ref, lb_ref, gain_ref, o_ref, st_in_ref, dy_ref, mall_ref, mask_ref, suf_ref,
             dproj_ref, dlb_ref, dgain_ref, dst_ref, run_ref):
        b = pl.program_id(1)

        @pl.when(b == 0)
        def _():
            dst_ref[...] = jnp.zeros_like(dst_ref)
            run_ref[...] = jnp.zeros_like(run_ref)
            dlb_ref[...] = jnp.zeros_like(dlb_ref)
            dgain_ref[...] = jnp.zeros_like(dgain_ref)

        gain = gain_ref[...]

        def head_chunk(ci, rows, hh, cols):
            lb = lb_ref[:, cols]
            qr, fr = q_ref[rows, cols], f_ref[rows, cols]
            q, sq, sf, forget, lf, k = _hg_gates(qr, fr, lb)
            v = i_ref[rows, cols].astype(BF16)
            expo = _join3(_dot(mall_ref[...], _split3(lf)))
            masks = mask_ref[...]
            o = o_ref[rows, cols]
            dyv = dy_ref[rows, cols]
            sg = _sigmoid(g_ref[rows, cols])
            r = lax.rsqrt(jnp.mean(o * o, axis=-1, keepdims=True) + NORM_EPS)
            ohat = o * r
            dyn = dyv * sg
            dproj_ref[3, rows, cols] = (dyv * ohat * gain * sg * (1.0 - sg)).astype(dproj_ref.dtype)
            dgain_ref[:, cols] += jnp.sum(dyn * ohat, axis=0, keepdims=True)
            dohat = dyn * gain
            do = (r * (dohat - ohat * jnp.mean(dohat * ohat, axis=-1, keepdims=True))).astype(BF16)
            dst = dst_ref[hh]
            dstb = dst.astype(BF16)
            qb, kb = q.astype(BF16), k.astype(BF16)
            f_cum = jnp.exp(expo[2 * nlev * c:(2 * nlev + 1) * c])
            f_tail = jnp.exp(expo[(2 * nlev + 1) * c:(2 * nlev + 2) * c])
            q_in = (q * f_cum).astype(BF16)
            k_dec = (k * f_tail).astype(BF16)
            t_in = _join3(_dot(do, _split3(st_in_ref[ci, hh])))
            t_st = _join3(_dot(v, _split3(dst)))
            da = _dot(do, v, NT)
            dam = (masks[nlev] * da).astype(BF16)
            a = masks[nlev] * _dot(qb, kb, NT)
            dq = t_in * f_cum + _dot(dam, kb)
            dk = t_st * f_tail + _dot(dam, qb, TN)
            db = q_in.astype(F32) * t_in - k_dec.astype(F32) * t_st
            for li in range(nlev):
                fq = jnp.exp(expo[(2 * li) * c:(2 * li + 1) * c])
                fk = jnp.exp(expo[(2 * li + 1) * c:(2 * li + 2) * c])
                qs, ks = (q * fq).astype(BF16), (k * fk).astype(BF16)
                a = a + masks[li] * _dot(qs, ks, NT)
                dam = (masks[li] * da).astype(BF16)
                t_q = _dot(dam, ks)
                t_k = _dot(dam, qs, TN)
                dq = dq + t_q * fq
                dk = dk + t_k * fk
                db = db + (qs.astype(F32) * t_q - ks.astype(F32) * t_k)
            dv = _dot(a.astype(BF16), do, TN) + _dot(k_dec, dstb, NT)
            dst_ref[hh] = dst * f_cum[c - 1:c, :] + _dot(do, q_in, TN)
            dlf = _join3(_dot(suf_ref[...], _split3(db))) + run_ref[hh]
            run_ref[hh] = dlf[0:1, :]
            dforget = dlf / forget - dk
            dlb_ref[:, cols] += jnp.sum(dforget * (1.0 - sf), axis=0, keepdims=True)
            dproj_ref[1, rows, cols] = (dforget * (1.0 - lb) * sf * (1.0 - sf)).astype(dproj_ref.dtype)
            dproj_ref[0, rows, cols] = (dq * sq * (1.0 + qr * (1.0 - sq))).astype(dproj_ref.dtype)
            dproj_ref[2, rows, cols] = dv.astype(dproj_ref.dtype)

        def chunk(it, carry):
            ci = ncb - 1 - it
            rows = pl.ds(pl.multiple_of(ci * c, c), c)
            for hh in range(hp):
                head_chunk(ci, rows, hh, slice(hh * HG_HEAD_DIM, (hh + 1) * HG_HEAD_DIM))
            return carry

        lax.fori_loop(0, ncb, chunk, 0)

    part = lambda k: pl.BlockSpec((tb, wide), lambda h, b: (nb - 1 - b, k * (nh // hp) + h))
    head_row = pl.BlockSpec((1, wide), lambda h, b: (0, h))
    tok = pl.BlockSpec((tb, wide), lambda h, b: (nb - 1 - b, h))
    const2 = lambda shape: pl.BlockSpec(shape, lambda h, b: (0, 0))
    return _call_with_exchange(
        body, exchange, grid=(nh // hp, nb),
        in_specs=[part(0), part(1), part(2), part(3), head_row, const2((1, HG_HEAD_DIM)), tok,
                  pl.BlockSpec((ncb, hp, HG_HEAD_DIM, HG_HEAD_DIM), lambda h, b: (nb - 1 - b, h, 0, 0)),
                  tok, const2((nrow, c)), pl.BlockSpec((nlev + 1, c, c), lambda h, b: (0, 0, 0)), const2((c, c))],
        out_specs=[pl.BlockSpec((4, tb, wide), lambda h, b: (0, nb - 1 - b, h)), head_row, head_row],
        out_shape=[jax.ShapeDtypeStruct((4, s, d), BF16)] + [jax.ShapeDtypeStruct((1, d), F32)] * 2,
        scratch_shapes=[pltpu.VMEM((hp, HG_HEAD_DIM, HG_HEAD_DIM), F32), pltpu.VMEM((hp, 1, HG_HEAD_DIM), F32)],
        name="hg_bwd", args=(proj, proj, proj, proj, lb_row, gain_row, o_saved, states, dy, m_all, mask_all, suffix))


def _lb_fwd(logits):
    n, d = logits.shape

    def body(l_ref, lb_ref, p_ref):
        rows = [l_ref[i:i + 1, :] for i in range(n)]
        m = functools.reduce(jnp.maximum, rows)
        es = [jnp.exp(r - m) for r in rows]
        tot = functools.reduce(lambda a, b: a + b, es)
        ps = [e / tot for e in es]
        run = jnp.zeros_like(ps[0])
        for i in range(n):
            run = run + ps[i]
            lb_ref[i:i + 1, :] = run - ps[0]
            p_ref[i:i + 1, :] = ps[i]

    return pl.pallas_call(
        body, out_shape=[jax.ShapeDtypeStruct((n, d), F32)] * 2, name="lb_fwd",
    )(logits)


def _lb_bwd(p, dlb):
    n, d = p.shape

    def body(p_ref, dlb_ref, dl_ref):
        ps = [p_ref[i:i + 1, :] for i in range(n)]
        ds = [dlb_ref[i:i + 1, :] for i in range(n)]
        total = functools.reduce(lambda a, b: a + b, ds)
        dps = []
        for i in range(n):
            dp = functools.reduce(lambda a, b: a + b, ds[i:])
            dps.append(dp - total if i == 0 else dp)
        inner = functools.reduce(lambda a, b: a + b, [pi * di for pi, di in zip(ps, dps)])
        for i in range(n):
            dl_ref[i:i + 1, :] = ps[i] * (dps[i] - inner)

    return pl.pallas_call(body, out_shape=jax.ShapeDtypeStruct((n, d), F32), name="lb_bwd")(p, dlb)


def _as2d(a):
    return a.reshape(-1, a.shape[-1])


def _adamw(w, m, v, grads, exchange=None):
    shape = w.shape
    w2, m2, v2 = _as2d(w), _as2d(m), _as2d(v)
    g2 = [_as2d(g) for g in grads]
    rows, cols = w2.shape
    tr = _pick(rows, 512)
    ng = len(g2)
    bc1 = 1.0 - ADAM_B1 ** ADAM_STEP
    bc2 = 1.0 - ADAM_B2 ** ADAM_STEP

    def body(w_ref, m_ref, v_ref, *rest):
        g = rest[0][...]
        for extra in rest[1:ng]:
            g = g + extra[...]
        g_out, d_out, m_out, v_out = rest[ng:]
        mn = ADAM_B1 * m_ref[...] + (1.0 - ADAM_B1) * g
        vn = ADAM_B2 * v_ref[...] + (1.0 - ADAM_B2) * (g * g)
        m_hat = mn / bc1
        v_hat = vn / bc2
        g_out[...] = g
        d_out[...] = -ADAM_LR * (m_hat / (jnp.sqrt(v_hat) + ADAM_EPS) + ADAM_WD * w_ref[...])
        m_out[...] = mn
        v_out[...] = vn

    spec = pl.BlockSpec((tr, cols), lambda i: (i, 0))
    outs, moved = _call_with_exchange(
        body, exchange, grid=(rows // tr,), in_specs=[spec] * (3 + ng), out_specs=[spec] * 4,
        out_shape=[jax.ShapeDtypeStruct((rows, cols), F32)] * 4, scratch_shapes=[], name="adamw",
        args=(w2, m2, v2, *g2))
    result = tuple(o.reshape(shape) for o in outs)
    return result if exchange is None else (result, moved)


def _sum_slots(parts, recv, chip, into, index):
    _, rows, cols = parts.shape
    tr = _pick(rows, 512)

    def body(chip_ref, own_ref, r0_ref, r1_ref, r2_ref, into_ref, o_ref):
        f = lambda r: r[...].astype(F32)
        o_ref[...] = ((f(own_ref) + f(r0_ref)) + f(r1_ref)) + f(r2_ref)

    grid_spec = pltpu.PrefetchScalarGridSpec(
        num_scalar_prefetch=1, grid=(rows // tr,),
        in_specs=[pl.BlockSpec((None, tr, cols), lambda i, chip_ref: (chip_ref[0], i, 0))]
        + [pl.BlockSpec((None, tr, cols), functools.partial(lambda i, chip_ref, k: (k, i, 0), k=k)) for k in range(3)]
        + [pl.BlockSpec(memory_space=pl.ANY)],
        out_specs=pl.BlockSpec((None, tr, cols), lambda i, chip_ref: (index, i, 0)))
    return pl.pallas_call(
        body, grid_spec=grid_spec, out_shape=jax.ShapeDtypeStruct(into.shape, F32),
        input_output_aliases={5: 0}, compiler_params=_params(("parallel",)), name="sum_slots",
    )(chip, parts, recv, recv, recv, into)


def _pack_rows(pieces):
    cols = pieces[0].shape[1]
    used = sum(p.shape[0] for p in pieces)
    rows = -(-used // 8) * 8

    def body(*refs):
        out_ref = refs[-1]
        at = 0
        for ref in refs[:-1]:
            out_ref[at:at + ref.shape[0], :] = ref[...]
            at += ref.shape[0]
        if at < rows:
            out_ref[at:rows, :] = jnp.zeros((rows - at, cols), F32)

    return pl.pallas_call(body, out_shape=jax.ShapeDtypeStruct((rows, cols), F32), name="pack_rows")(*pieces)


def _sum_devices(gathered):
    n, rows, cols = gathered.shape

    def body(g_ref, o_ref):
        acc = g_ref[0]
        for i in range(1, n):
            acc = acc + g_ref[i]
        o_ref[...] = acc

    return pl.pallas_call(body, out_shape=jax.ShapeDtypeStruct((rows, cols), F32), name="sum_devices")(gathered)


def _coords():
    return lax.axis_index("x"), lax.axis_index("y"), lax.axis_index("c")


def _chip_peers(x, y, c):
    out = []
    for fx, fy in ((0, 1), (1, 0), (1, 1)):
        px = 1 - x if fx else x
        py = 1 - y if fy else y
        out.append(((px, py, c), 2 * px + py))
    return out


class _ChipExchange:
    def __init__(self, kind, arrays):
        self.kind, self.kinds, self.arrays, self.n = kind, [kind] * len(arrays), list(arrays), len(arrays)
        self._shapes()

    def also(self, kind, arrays):
        self.kinds += [kind] * len(arrays)
        self.arrays += list(arrays)
        self.n = len(self.arrays)
        self._shapes()
        return self

    def _shapes(self):
        lead = {"gather": lambda a: (N_CHIPS,) + a.shape, "scatter": lambda a: (3,) + a.shape[1:],
                "swap": lambda a: a.shape}
        self.out_shape = [jax.ShapeDtypeStruct(lead[k](a), a.dtype) for k, a in zip(self.kinds, self.arrays)]
        for k, a in zip(self.kinds, self.arrays):
            assert k != "gather" or a.shape[0] % 2 == 0, "a gathered array is cut in two along its leading axis"
        self.scratch = [pltpu.SemaphoreType.DMA((6 * self.n,)), pltpu.SemaphoreType.DMA((6 * self.n,)),
                        pltpu.SemaphoreType.DMA((self.n,))]

    def copies(self, ins, outs, send_sems, recv_sems, local_sems):
        x, y, c = _coords()
        me = 2 * x + y
        sibling = (x, y, 1 - c)
        starts, waits, last = [], [], []
        for t, kind in enumerate(self.kinds):
            if kind == "swap":
                cp = pltpu.make_async_remote_copy(
                    src_ref=ins[t], dst_ref=outs[t], send_sem=send_sems.at[6 * t], recv_sem=recv_sems.at[6 * t],
                    device_id=sibling, device_id_type=MESH)
                starts.append(cp.start)
                waits += [cp.wait_send, cp.wait_recv]
                continue
            if kind == "gather":
                own = pltpu.make_async_copy(ins[t], outs[t].at[me], local_sems.at[t])
                starts.append(own.start)
                waits.append(own.wait)
                half_rows = ins[t].shape[0] // 2
                mine = pl.ds(c * half_rows, half_rows)
                theirs = pl.ds((1 - c) * half_rows, half_rows)
            for k, (peer, peer_chip) in enumerate(_chip_peers(x, y, c)):
                sems = dict(send_sem=send_sems.at[6 * t + k], recv_sem=recv_sems.at[6 * t + k],
                            device_id=peer, device_id_type=MESH)
                if kind == "scatter":
                    send = pltpu.make_async_remote_copy(src_ref=ins[t].at[peer_chip], dst_ref=outs[t].at[k], **sems)
                    starts.append(send.start)
                    waits += [send.wait_send, send.wait_recv]
                    continue
                send = pltpu.make_async_remote_copy(
                    src_ref=ins[t].at[mine], dst_ref=outs[t].at[me].at[mine], **sems)
                landed = outs[t].at[peer_chip].at[mine]
                recv = pltpu.make_async_remote_copy(src_ref=ins[t].at[mine], dst_ref=landed, **sems)
                pass_on = pltpu.make_async_remote_copy(
                    src_ref=landed, dst_ref=landed, send_sem=send_sems.at[6 * t + 3 + k],
                    recv_sem=recv_sems.at[6 * t + 3 + k], device_id=sibling, device_id_type=MESH)
                handed = pltpu.make_async_remote_copy(
                    src_ref=landed, dst_ref=outs[t].at[peer_chip].at[theirs], send_sem=send_sems.at[6 * t + 3 + k],
                    recv_sem=recv_sems.at[6 * t + 3 + k], device_id=sibling, device_id_type=MESH)
                starts.append(send.start)
                waits += [recv.wait_recv, pass_on.start]
                last += [send.wait_send, pass_on.wait_send, handed.wait_recv]
        return starts, waits + last

    def run(self, name):
        n = self.n

        def body(*refs):
            starts, waits = self.copies(refs[:n], refs[n:2 * n], *refs[2 * n:])
            for f in starts + waits:
                f()

        return pl.pallas_call(body, in_specs=[HBM_SPEC] * n, out_specs=[HBM_SPEC] * n, out_shape=self.out_shape,
                              scratch_shapes=self.scratch, name=name)(*self.arrays)


def _call_with_exchange(body, exchange, *, grid, in_specs, out_specs, out_shape, scratch_shapes, name, args,
                        sequential=False):
    if exchange is None:
        first_axis = "arbitrary" if sequential else "parallel"
        outs = pl.pallas_call(body, grid=grid, in_specs=in_specs, out_specs=out_specs, out_shape=out_shape,
                              scratch_shapes=scratch_shapes,
                              compiler_params=_params((first_axis,) + ("arbitrary",) * (len(grid) - 1)),
                              name=name)(*args)
        return outs, []
    n_in, n_out, n_scr, n = len(in_specs), len(out_specs), len(scratch_shapes), exchange.n

    def wrapped(*refs):
        ins, ex_in = refs[:n_in], refs[n_in:n_in + n]
        outs = refs[n_in + n:n_in + n + n_out]
        ex_out = refs[n_in + n + n_out:n_in + 2 * n + n_out]
        scr = refs[n_in + 2 * n + n_out:n_in + 2 * n + n_out + n_scr]
        sems = refs[n_in + 2 * n + n_out + n_scr:]
        ids = [pl.program_id(a) for a in range(len(grid))]
        first = functools.reduce(jnp.logical_and, [i == 0 for i in ids])
        last = functools.reduce(jnp.logical_and, [i == g - 1 for i, g in zip(ids, grid)])

        @pl.when(first)
        def _():
            for f in exchange.copies(ex_in, ex_out, *sems)[0]:
                f()

        body(*ins, *outs, *scr)

        @pl.when(last)
        def _():
            for f in exchange.copies(ex_in, ex_out, *sems)[1]:
                f()

    res = pl.pallas_call(
        wrapped, grid=grid, in_specs=list(in_specs) + [HBM_SPEC] * n, out_specs=list(out_specs) + [HBM_SPEC] * n,
        out_shape=list(out_shape) + exchange.out_shape, scratch_shapes=list(scratch_shapes) + exchange.scratch,
        compiler_params=_params(("arbitrary",) * len(grid)), name=name + "_" + exchange.kind,
    )(*args, *exchange.arrays)
    return res[:n_out], res[n_out:]


def _gather_devices(a):
    def body(in_ref, out_ref, send_sems, recv_sems, local_sem):
        x, y, c = _coords()
        me = 4 * x + 2 * y + c
        own = pltpu.make_async_copy(in_ref, out_ref.at[me], local_sem)
        own.start()
        waits = [own.wait]
        for k in range(1, N_DEVICES):
            px = 1 - x if k & 4 else x
            py = 1 - y if k & 2 else y
            pc = 1 - c if k & 1 else c
            peer = (px, py, pc)
            send = pltpu.make_async_remote_copy(
                src_ref=in_ref, dst_ref=out_ref.at[me], send_sem=send_sems.at[k - 1], recv_sem=recv_sems.at[k - 1],
                device_id=peer, device_id_type=MESH)
            send.start()
            recv = pltpu.make_async_remote_copy(
                src_ref=in_ref, dst_ref=out_ref.at[4 * px + 2 * py + pc], send_sem=send_sems.at[k - 1],
                recv_sem=recv_sems.at[k - 1], device_id=peer, device_id_type=MESH)
            waits += [send.wait_send, recv.wait_recv]
        for w in waits:
            w()

    return pl.pallas_call(
        body, in_specs=[HBM_SPEC], out_specs=HBM_SPEC,
        out_shape=jax.ShapeDtypeStruct((N_DEVICES,) + a.shape, a.dtype),
        scratch_shapes=[pltpu.SemaphoreType.DMA((N_DEVICES - 1,)), pltpu.SemaphoreType.DMA((N_DEVICES - 1,)),
                        pltpu.SemaphoreType.DMA],
        name="gather_devices",
    )(a)


def _mlp_grad_epilogue(r, u):
    return r * (2.0 * jnp.maximum(u, 0.0))


def kernel(x, norm_gains, sb_w_qkv, sb_q_gain, sb_k_gain, sb_w_o, hg_w_in, hg_lb_logits, hg_norm_gain, hg_w_o, mlp_w1, mlp_w2, loss_target, m_norm_gains, m_sb_w_qkv, m_sb_q_gain, m_sb_k_gain, m_sb_w_o, m_hg_w_in, m_hg_lb_logits, m_hg_norm_gain, m_hg_w_o, m_mlp_w1, m_mlp_w2, v_norm_gains, v_sb_w_qkv, v_sb_q_gain, v_sb_k_gain, v_sb_w_o, v_hg_w_in, v_hg_lb_logits, v_hg_norm_gain, v_hg_w_o, v_mlp_w1, v_mlp_w2):
    depth = norm_gains.shape[0]
    n_sb, n_hg = sb_w_qkv.shape[0], hg_w_in.shape[0]
    xs, tgt = x[0], loss_target[0]
    s, d = xs.shape
    dq = d // N_CHIPS
    cx, cy, cc = _coords()
    chip = 2 * cx + cy
    chip_arr = jnp.reshape(chip, (1,)).astype(jnp.int32)

    def mixer_weights(layer):
        j = layer // 2
        return (sb_w_qkv[j], sb_w_o[j]) if layer % 2 == 0 else (hg_w_in[j], hg_w_o[j])

    w_in_g, ng_g, lbl_g = _ChipExchange(
        "gather", [mixer_weights(0)[0].astype(BF16), norm_gains, hg_lb_logits]).run("gather_first")
    gains = jnp.transpose(ng_g, (1, 2, 0, 3)).reshape(depth, 2, d)
    logits = jnp.transpose(lbl_g, (1, 0, 2)).reshape(n_hg, d)
    lbs, lb_p = _lb_fwd(logits)
    qg_rows = [jnp.tile(sb_q_gain[j], d // SB_HEAD_DIM)[None] for j in range(n_sb)]
    kg_rows = [jnp.tile(sb_k_gain[j], d // SB_HEAD_DIM)[None] for j in range(n_sb)]

    saved, wts = [], []
    xc = xs
    for layer in range(depth):
        j = layer // 2
        ahead = [mixer_weights(layer)[1], mlp_w1[layer], mlp_w2[layer]]
        if layer + 1 < depth:
            ahead.append(mixer_weights(layer + 1)[0])
        gather = _ChipExchange("gather", [a.astype(BF16) for a in ahead])
        h1 = _rmsnorm_fwd(xc, gains[layer, 0][None])
        if layer % 2 == 0:
            qkv = _mm_fwd_cols(h1, w_in_g, name="sb_qkv")
            qn, kn, vb = _qk_norm_fwd(qkv, qg_rows[j], kg_rows[j])
            o, moved = _sb_attn_fwd(qn, kn, vb, gather)
            x_mid = _mm_fwd_rows(o, moved[0], residual=xc, name="sb_out")
            mix = (qkv, qn, kn, vb, o)
        else:
            proj = _mm_fwd_cols(h1, w_in_g, name="hg_in")
            (y, o, states), moved = _hg_fwd(proj, lbs[j][None], hg_norm_gain[j][None], gather)
            x_mid = _mm_fwd_rows(y, moved[0], residual=xc, name="hg_out")
            mix = (proj, y, o, states)
        w_out_g, w1_g, w2_g = moved[:3]
        h2 = _rmsnorm_fwd(x_mid, gains[layer, 1][None])
        u = _mm_fwd_cols(h2, w1_g, name="mlp_up")
        x_out = _mm_fwd_rows(u, w2_g, residual=x_mid, a_fn=_relu2, name="mlp_down")
        saved.append((xc, h1, mix, x_mid, h2, u))
        wts.append((w_in_g, w_out_g, w1_g, w2_g))
        w_in_g = moved[3] if layer + 1 < depth else None
        xc = x_out

    sq, dx = _loss_head(xc, tgt)
    loss = lax.psum(jnp.sum(sq) * (0.5 / d), ("x", "y", "c"))

    dgains = [[None, None] for _ in range(depth)]
    dqg, dkg = [None] * n_sb, [None] * n_sb
    dhgain, dlb = [None] * n_hg, [None] * n_hg
    grads, received, pending = {}, {}, []

    def ready(key, parts):
        grads[key] = parts
        pending.append(key)

    def scatter_of(keys):
        return _ChipExchange("scatter", [grads[k] for k in keys]) if keys else None

    def sent(keys, moved):
        for k, r in zip(keys, moved):
            received[k] = r
            pending.remove(k)

    def chip_sum(kind, layers):
        total = jnp.zeros((len(layers),) + grads[kind, layers[0]].shape[1:], F32)
        for index, l in enumerate(layers):
            total = _sum_slots(grads[kind, l], received[kind, l], chip_arr, total, index)
        return total

    sb_layers, hg_layers = range(0, depth, 2), range(1, depth, 2)
    tensors = [("in", sb_layers), ("out", sb_layers), ("in", hg_layers), ("out", hg_layers),
               ("w1", range(depth)), ("w2", range(depth))]

    for layer in reversed(range(depth)):
        j = layer // 2
        x_in, h1, mix, x_mid, h2, u = saved[layer]
        w_in_g, w_out_g, w1_g, w2_g = wts[layer]
        du = _mm_bwd_rows(dx, w2_g, name="mlp_down_dx", out_dtype=BF16, epi_fn=_mlp_grad_epilogue, epi_args=(u,))
        ready(("w2", layer), _mm_dw_rows(u, dx, a_fn=_relu2, name="mlp_down_dw"))
        ready(("w1", layer), _mm_dw_cols(h2, du, name="mlp_up_dw"))
        dx, dgains[layer][1] = _mm_bwd_cols_norm(du, w1_g, x_mid, gains[layer, 1][None], dx, name="mlp_up_dx")
        if layer % 2 == 0:
            qkv, qn, kn, vb, o = mix
            do = _mm_bwd_rows(dx, w_out_g, name="sb_out_dx")
            ready(("out", layer), _mm_dw_rows(o, dx, name="sb_out_dw"))
            keys = list(pending)
            (dqn, dkn, dv), moved = _sb_attn_bwd(qn, kn, vb, do, scatter_of(keys))
            sent(keys, moved)
            d_in, dqg[j], dkg[j] = _qk_norm_bwd(qkv, qg_rows[j], kg_rows[j], dqn, dkn, dv)
            ready(("in", layer), _mm_dw_cols(h1, d_in, name="sb_qkv_dw"))
            dx_name = "sb_qkv_dx"
        else:
            proj, y, o, states = mix
            dy = _mm_bwd_rows(dx, w_out_g, name="hg_out_dx")
            ready(("out", layer), _mm_dw_rows(y, dx, name="hg_out_dw"))
            keys = list(pending)
            (d_in, dlb[j], dhgain[j]), moved = _hg_bwd(
                proj, lbs[j][None], hg_norm_gain[j][None], o, states, dy, scatter_of(keys))
            sent(keys, moved)
            ready(("in", layer), _mm_dw_cols(h1, d_in, name="hg_in_dw"))
            dx_name = "hg_in_dx"
        if layer > 0:
            dx, dgains[layer][0] = _mm_bwd_cols_norm(d_in, w_in_g, x_in, gains[layer, 0][None], dx, name=dx_name)
        else:
            keys = list(pending)
            early_sums = [chip_sum(*t) for t in tensors[1:]]
            (dx, dgains[layer][0]), moved = _mm_bwd_cols_norm(
                d_in, w_in_g, x_in, gains[layer, 0][None], dx, name=dx_name,
                exchange=scatter_of(keys).also("swap", early_sums))
            sent(keys, moved[:len(keys)])
            early_other = moved[len(keys):]
    grad_x = dx[None]
    dlogits = _lb_bwd(lb_p, jnp.concatenate(dlb, axis=0))

    big_w = [sb_w_qkv, sb_w_o, hg_w_in, hg_w_o, mlp_w1, mlp_w2]
    big_m = [m_sb_w_qkv, m_sb_w_o, m_hg_w_in, m_hg_w_o, m_mlp_w1, m_mlp_w2]
    big_v = [v_sb_w_qkv, v_sb_w_o, v_hg_w_in, v_hg_w_o, v_mlp_w1, v_mlp_w2]
    late_sum = chip_sum(*tensors[0])
    big_second, late_other = _adamw(big_w[1], big_m[1], big_v[1], [early_sums[0], early_other[0]],
                                    _ChipExchange("swap", [late_sum]))
    big = [_adamw(big_w[0], big_m[0], big_v[0], [late_sum, late_other[0]]), big_second]
    big += [_adamw(w, m, v, [a, b]) for w, m, v, a, b in
            zip(big_w[2:], big_m[2:], big_v[2:], early_sums[1:], early_other[1:])]

    pieces = [r for pair in dgains for r in pair] + [dlogits] + dqg + dkg + dhgain
    small = _sum_devices(_gather_devices(_pack_rows(pieces)))
    my_cols = lambda a: lax.dynamic_slice_in_dim(a, chip * dq, dq, axis=1)
    fold = lambda rows, width: jnp.sum(rows.reshape(rows.shape[0], -1, width), axis=1)
    base = 2 * depth + n_hg
    g_ng = my_cols(small[0:2 * depth]).reshape(norm_gains.shape)
    g_lbl = my_cols(small[2 * depth:base])
    g_qg = fold(small[base:base + n_sb], SB_HEAD_DIM)
    g_kg = fold(small[base + n_sb:base + 2 * n_sb], SB_HEAD_DIM)
    g_hgn = fold(small[base + 2 * n_sb:base + 2 * n_sb + n_hg], HG_HEAD_DIM)
    r_ng = _adamw(norm_gains, m_norm_gains, v_norm_gains, [g_ng])
    r_qg = _adamw(sb_q_gain, m_sb_q_gain, v_sb_q_gain, [g_qg])
    r_kg = _adamw(sb_k_gain, m_sb_k_gain, v_sb_k_gain, [g_kg])
    r_lbl = _adamw(hg_lb_logits, m_hg_lb_logits, v_hg_lb_logits, [g_lbl])
    r_hgn = _adamw(hg_norm_gain, m_hg_norm_gain, v_hg_norm_gain, [g_hgn])

    per_weight = [r_ng, big[0], r_qg, r_kg, big[1], big[2], r_lbl, r_hgn, big[3], big[4], big[5]]
    outs = [loss, grad_x]
    for field in range(4):
        outs += [r[field] for r in per_weight]
    return tuple(outs)
```

```python
import functools
import math

import numpy as np
import jax
import jax.numpy as jnp
from jax import lax
from jax.experimental import pallas as pl
from jax.experimental.pallas import tpu as pltpu

F32 = jnp.float32
BF16 = jnp.bfloat16
GRAD_SLOT_DTYPE = jnp.bfloat16

NORM_EPS = 1e-6
SB_HEAD_DIM = 64
HG_HEAD_DIM = 128
HG_CHUNK = 128
LANES = 128
VMEM_LIMIT_BYTES = 56 * 2 ** 20
N_CHIPS = 4
N_DEVICES = 8

ADAM_LR = 0.001
ADAM_B1 = 0.9
ADAM_B2 = 0.999
ADAM_EPS = 1e-08
ADAM_WD = 0.01
ADAM_STEP = 10

MESH = pl.DeviceIdType.MESH
HBM_SPEC = pl.BlockSpec(memory_space=pltpu.HBM)

NN = (((1,), (0,)), ((), ()))
NT = (((1,), (1,)), ((), ()))
TN = (((0,), (0,)), ((), ()))


def _params(sem=None):
    return pltpu.CompilerParams(dimension_semantics=sem, vmem_limit_bytes=VMEM_LIMIT_BYTES)


def _pick(dim, pref):
    for t in (1024, 768, 512, 384, 256, 128, 64, 32, 16, 8):
        if t <= pref and dim % t == 0:
            return t
    return dim


def _dot(a, b, dims=NN):
    return lax.dot_general(a, b, dims, preferred_element_type=F32)


def _sigmoid(x):
    e = jnp.exp(-jnp.abs(x))
    return jnp.where(x >= 0, 1.0, e) / (1.0 + e)


def _matmul(a, b, *, mode, grid, a_block, a_map, b_block, b_map, o_block, o_map, out_shape, out_dtype, name,
            a_fn=None, epi_fn=None, epi_args=(), epi_row_args=(), col_sums=False, exchange=None):
    nk = grid[2]
    dims = {"nn": NN, "nt": NT, "tn": TN}[mode]
    n_epi = len(epi_args) + len(epi_row_args)
    n_out = 2 if col_sums else 1
    tn = o_block[-1]
    assert not col_sums or grid[1] == 1, "the column sums stay resident only with one tile along N"

    def body(a_ref, b_ref, *rest):
        epi_refs = rest[:n_epi]
        o_ref = rest[n_epi]
        kk = pl.program_id(2)

        def emit(r):
            if epi_fn is not None:
                r = epi_fn(r, *[e[...] for e in epi_refs])
            if col_sums:
                r, row = r
                sums_ref = rest[n_epi + 1]
                first = pl.program_id(0) == 0

                @pl.when(first)
                def _():
                    sums_ref[...] = row

                @pl.when(jnp.logical_not(first))
                def _():
                    sums_ref[...] += row
            o_ref[...] = r.astype(o_ref.dtype)

        av = a_ref[...]
        if a_fn is not None:
            av = a_fn(av)
        part = _dot(av.astype(BF16), b_ref[...].astype(BF16), dims)
        if nk == 1:
            emit(part)
            return
        acc_ref = rest[n_epi + n_out]

        @pl.when(kk == 0)
        def _():
            acc_ref[...] = part

        @pl.when(kk > 0)
        def _():
            acc_ref[...] += part

        @pl.when(kk == nk - 1)
        def _():
            emit(acc_ref[...])

    acc_shape = tuple(d for d in o_block if d is not None)
    row_spec = pl.BlockSpec((1, tn), lambda i, j, kk: (0, j))
    in_specs = [pl.BlockSpec(a_block, a_map), pl.BlockSpec(b_block, b_map)]
    in_specs += [pl.BlockSpec(o_block, o_map) for _ in epi_args] + [row_spec for _ in epi_row_args]
    out_specs, out_shapes = [pl.BlockSpec(o_block, o_map)], [jax.ShapeDtypeStruct(out_shape, out_dtype)]
    if col_sums:
        out_specs, out_shapes = out_specs + [row_spec], out_shapes + [jax.ShapeDtypeStruct((1, out_shape[-1]), F32)]
    outs, moved = _call_with_exchange(
        body, exchange, grid=grid, in_specs=in_specs, out_specs=out_specs, out_shape=out_shapes,
        scratch_shapes=[pltpu.VMEM(acc_shape, F32)] if nk > 1 else [], name=name,
        args=(a, b, *epi_args, *epi_row_args), sequential=col_sums)
    result = tuple(outs) if col_sums else outs[0]
    return result if exchange is None else (result, moved)


def _relu2(u):
    r = jnp.maximum(u, 0.0)
    return r * r


def _add(r, res):
    return r + res


def _mm_fwd_cols(a, wg, *, name):
    s, k = a.shape
    ncs = wg.shape[2]
    tm, tk, tn = _pick(s, 1024), _pick(k, 1024), _pick(ncs, 1024)
    npb = ncs // tn
    return _matmul(a, wg, mode="nn", grid=(s // tm, N_CHIPS * npb, k // tk),
                   a_block=(tm, tk), a_map=lambda i, j, kk: (i, kk),
                   b_block=(None, tk, tn), b_map=lambda i, j, kk: (j // npb, kk, j % npb),
                   o_block=(tm, tn), o_map=lambda i, j, kk: (i, j),
                   out_shape=(s, N_CHIPS * ncs), out_dtype=F32, name=name)


def _rows_joined(wg):
    assert wg.shape[1] % 16 == 0, "joining the leading axes must not cross a tile of 16 rows"
    return wg.reshape(wg.shape[0] * wg.shape[1], wg.shape[2])


def _mm_fwd_rows(a, wg, *, residual, name, a_fn=None):
    s = a.shape[0]
    w = _rows_joined(wg)
    k, n = w.shape
    tm, tk, tn = _pick(s, 1024), _pick(k, 1024), _pick(n, 1024)
    return _matmul(a, w, mode="nn", grid=(s // tm, n // tn, k // tk),
                   a_block=(tm, tk), a_map=lambda i, j, kk: (i, kk),
                   b_block=(tk, tn), b_map=lambda i, j, kk: (kk, j),
                   o_block=(tm, tn), o_map=lambda i, j, kk: (i, j),
                   out_shape=(s, n), out_dtype=F32, name=name, a_fn=a_fn, epi_fn=_add, epi_args=(residual,))


def _rmsnorm_grad(dh, x, dx_res, gain):
    r = lax.rsqrt(jnp.mean(x * x, axis=-1, keepdims=True) + NORM_EPS)
    xhat = x * r
    dxhat = dh * gain
    dx = r * (dxhat - xhat * jnp.mean(dxhat * xhat, axis=-1, keepdims=True))
    return dx_res + dx, jnp.sum(dh * xhat, axis=0, keepdims=True)


def _mm_bwd_cols_norm(dy, wg, x, gain_row, dx_res, *, name, exchange=None):
    by_slot = dy.ndim == 3
    s = dy.shape[1] if by_slot else dy.shape[0]
    kw, ncs = wg.shape[1], wg.shape[2]
    tm, tk = _pick(s, 512), _pick(ncs, 1024)
    kpb = ncs // tk
    a_block, a_map = ((None, tm, tk), lambda i, j, kk: (kk // kpb, i, kk % kpb)) if by_slot else (
        (tm, tk), lambda i, j, kk: (i, kk))
    return _matmul(dy, wg, mode="nt", grid=(s // tm, 1, N_CHIPS * kpb),
                   a_block=a_block, a_map=a_map,
                   b_block=(None, kw, tk), b_map=lambda i, j, kk: (kk // kpb, j, kk % kpb),
                   o_block=(tm, kw), o_map=lambda i, j, kk: (i, j),
                   out_shape=(s, kw), out_dtype=F32, name=name, exchange=exchange,
                   epi_fn=_rmsnorm_grad, epi_args=(x, dx_res), epi_row_args=(gain_row,), col_sums=True)


def _mm_bwd_rows(dy, wg, *, name, out_dtype=F32, epi_fn=None, epi_args=()):
    s, n = dy.shape
    w = _rows_joined(wg)
    rows = w.shape[0]
    tm, tn, tk = _pick(s, 1024), _pick(rows, 1024), _pick(n, 1024)
    return _matmul(dy, w, mode="nt", grid=(s // tm, rows // tn, n // tk),
                   a_block=(tm, tk), a_map=lambda i, j, kk: (i, kk),
                   b_block=(tn, tk), b_map=lambda i, j, kk: (j, kk),
                   o_block=(tm, tn), o_map=lambda i, j, kk: (i, j),
                   out_shape=(s, rows), out_dtype=out_dtype, name=name, epi_fn=epi_fn, epi_args=epi_args)


def _mm_dw_cols(xa, dy, *, name):
    s, kx = xa.shape
    by_slot = dy.ndim == 3
    ncs = dy.shape[2] if by_slot else dy.shape[1] // N_CHIPS
    tm, tn, tk = _pick(kx, 1024), _pick(ncs, 1024), _pick(s, 1024)
    npb = ncs // tn
    b_block, b_map = ((None, tk, tn), lambda i, j, kk: (j // npb, kk, j % npb)) if by_slot else (
        (tk, tn), lambda i, j, kk: (kk, j))
    return _matmul(xa, dy, mode="tn", grid=(kx // tm, N_CHIPS * npb, s // tk),
                   a_block=(tk, tm), a_map=lambda i, j, kk: (kk, i),
                   b_block=b_block, b_map=b_map,
                   o_block=(None, tm, tn), o_map=lambda i, j, kk: (j // npb, i, j % npb),
                   out_shape=(N_CHIPS, kx, ncs), out_dtype=GRAD_SLOT_DTYPE, name=name)


def _mm_dw_rows(xa, dy, *, name, a_fn=None):
    s, n = dy.shape
    rows = xa.shape[1]
    assert (rows // N_CHIPS) % 16 == 0, "splitting the rows into slots must not cut a tile of 16 rows"
    tm, tn, tk = _pick(rows, 1024), _pick(n, 1024), _pick(s, 1024)
    dw = _matmul(xa, dy, mode="tn", grid=(rows // tm, n // tn, s // tk),
                 a_block=(tk, tm), a_map=lambda i, j, kk: (kk, i),
                 b_block=(tk, tn), b_map=lambda i, j, kk: (kk, j),
                 o_block=(tm, tn), o_map=lambda i, j, kk: (i, j),
                 out_shape=(rows, n), out_dtype=GRAD_SLOT_DTYPE, name=name, a_fn=a_fn)
    return dw.reshape(N_CHIPS, rows // N_CHIPS, n)


def _rmsnorm_fwd(x, gain_row):
    s, d = x.shape
    ts = _pick(s, 1024)

    def body(x_ref, g_ref, h_ref):
        xv = x_ref[...]
        r = lax.rsqrt(jnp.mean(xv * xv, axis=-1, keepdims=True) + NORM_EPS)
        h_ref[...] = (xv * r * g_ref[...]).astype(h_ref.dtype)

    return pl.pallas_call(
        body, grid=(s // ts,),
        in_specs=[pl.BlockSpec((ts, d), lambda i: (i, 0)), pl.BlockSpec((1, d), lambda i: (0, 0))],
        out_specs=pl.BlockSpec((ts, d), lambda i: (i, 0)),
        out_shape=jax.ShapeDtypeStruct((s, d), BF16),
        compiler_params=_params(("parallel",)), name="rmsnorm_fwd",
    )(x, gain_row)


def _loss_head(y, target):
    s, d = y.shape
    ts = _pick(s, 512)

    def body(y_ref, t_ref, sq_ref, dy_ref):
        i = pl.program_id(0)
        err = y_ref[...] - t_ref[...]
        dy_ref[...] = err / d
        part = jnp.sum(err * err, axis=0, keepdims=True)

        @pl.when(i == 0)
        def _():
            sq_ref[...] = part

        @pl.when(i > 0)
        def _():
            sq_ref[...] += part

    return pl.pallas_call(
        body, grid=(s // ts,),
        in_specs=[pl.BlockSpec((ts, d), lambda i: (i, 0)), pl.BlockSpec((ts, d), lambda i: (i, 0))],
        out_specs=[pl.BlockSpec((1, d), lambda i: (0, 0)), pl.BlockSpec((ts, d), lambda i: (i, 0))],
        out_shape=[jax.ShapeDtypeStruct((1, d), F32), jax.ShapeDtypeStruct((s, d), F32)],
        compiler_params=_params(("arbitrary",)), name="loss_head",
    )(y, target)


def _pair_ones():
    lane = np.arange(LANES)
    same_half = (lane[:, None] // SB_HEAD_DIM == lane[None, :] // SB_HEAD_DIM).astype(np.float32)
    return jnp.asarray(np.concatenate([same_half, same_half], axis=0), BF16)


def _pair_mean(val, pair_ones):
    return _dot(jnp.concatenate(_split2(val), axis=1), pair_ones) * (1.0 / SB_HEAD_DIM)


def _pair_mean_lanes(val, low_half):
    s0 = jnp.sum(jnp.where(low_half, val, 0.0), axis=-1, keepdims=True)
    s1 = jnp.sum(jnp.where(low_half, 0.0, val), axis=-1, keepdims=True)
    return jnp.where(low_half, s0, s1) * (1.0 / SB_HEAD_DIM)


def _qk_norm_fwd(qkv, qgain_row, kgain_row):
    s, d3 = qkv.shape
    d = d3 // 3
    ts = _pick(s, 512)
    groups = d // LANES

    def body(q_ref, k_ref, v_ref, qg_ref, kg_ref, ones_ref, qn_ref, kn_ref, vb_ref):
        for src, gain, dst in ((q_ref, qg_ref, qn_ref), (k_ref, kg_ref, kn_ref)):
            for p in range(groups):
                cols = slice(p * LANES, (p + 1) * LANES)
                xp = src[:, cols]
                r = lax.rsqrt(_pair_mean(xp * xp, ones_ref[...]) + NORM_EPS)
                dst[:, cols] = (xp * r * gain[:, cols]).astype(dst.dtype)
        vb_ref[...] = v_ref[...].astype(vb_ref.dtype)

    tok = lambda c: pl.BlockSpec((ts, d), lambda i: (i, c))
    row = pl.BlockSpec((1, d), lambda i: (0, 0))
    return pl.pallas_call(
        body, grid=(s // ts,),
        in_specs=[tok(0), tok(1), tok(2), row, row, pl.BlockSpec((2 * LANES, LANES), lambda i: (0, 0))],
        out_specs=[tok(0), tok(0), tok(0)],
        out_shape=[jax.ShapeDtypeStruct((s, d), BF16)] * 3,
        compiler_params=_params(("parallel",)), name="qk_norm_fwd",
    )(qkv, qkv, qkv, qgain_row, kgain_row, _pair_ones())


def _qk_norm_bwd(qkv, qgain_row, kgain_row, dqn, dkn, dv):
    s, d3 = qkv.shape
    d = d3 // 3
    ts = _pick(s, 512)
    groups = d // LANES

    def body(q_ref, k_ref, qg_ref, kg_ref, dqn_ref, dkn_ref, dv_ref, dqkv_ref, dqg_ref, dkg_ref):
        i = pl.program_id(0)
        low_half = lax.broadcasted_iota(jnp.int32, (ts, LANES), 1) < SB_HEAD_DIM
        for which, (src, gain, dsrc, dgain) in enumerate(((q_ref, qg_ref, dqn_ref, dqg_ref),
                                                          (k_ref, kg_ref, dkn_ref, dkg_ref))):
            for p in range(groups):
                cols = slice(p * LANES, (p + 1) * LANES)
                xp = src[:, cols]
                r = lax.rsqrt(_pair_mean_lanes(xp * xp, low_half) + NORM_EPS)
                xhat = xp * r
                dy = dsrc[:, cols]
                dxhat = dy * gain[:, cols]
                dx = r * (dxhat - xhat * _pair_mean_lanes(dxhat * xhat, low_half))
                dqkv_ref[:, which * d + p * LANES: which * d + (p + 1) * LANES] = dx.astype(dqkv_ref.dtype)
                part = jnp.sum(dy * xhat, axis=0, keepdims=True)

                @pl.when(i == 0)
                def _():
                    dgain[:, cols] = part

                @pl.when(i > 0)
                def _():
                    dgain[:, cols] += part
        dqkv_ref[:, 2 * d:] = dv_ref[...].astype(dqkv_ref.dtype)

    tok = lambda c: pl.BlockSpec((ts, d), lambda i: (i, c))
    row = pl.BlockSpec((1, d), lambda i: (0, 0))
    return pl.pallas_call(
        body, grid=(s // ts,),
        in_specs=[tok(0), tok(1), row, row, tok(0), tok(0), tok(0)],
        out_specs=[pl.BlockSpec((ts, d3), lambda i: (i, 0)), row, row],
        out_shape=[jax.ShapeDtypeStruct((s, d3), BF16), jax.ShapeDtypeStruct((1, d), F32),
                   jax.ShapeDtypeStruct((1, d), F32)],
        compiler_params=_params(("arbitrary",)), name="qk_norm_bwd",
    )(qkv, qkv, qgain_row, kgain_row, dqn, dkn, dv)


def _split2(x):
    hi = x.astype(BF16)
    lo = (x - hi.astype(F32)).astype(BF16)
    return hi, lo


SB_TK = 128


def _sb_consts(tk):
    j = np.arange(tk)
    ones = np.ones((tk, tk), np.float32)
    out = []
    for tri in ((j[:, None] >= j[None, :]), (j[:, None] <= j[None, :])):
        half = np.concatenate([tri.astype(np.float32), ones], axis=1)
        out.append(jnp.asarray(np.concatenate([half, half], axis=0), BF16))
    return out


def _head_stack(blk, low_half):
    f = blk.astype(F32)
    return jnp.concatenate([jnp.where(low_half, f, 0.0), jnp.where(low_half, 0.0, f)], axis=0).astype(BF16)


def _sb_tile_sums(z, valid, tri2):
    e = jnp.exp(-jnp.abs(z))
    lstay = jnp.minimum(-z, 0.0) - jnp.log(1.0 + e)
    if valid is not None:
        lstay = jnp.where(valid, lstay, 0.0)
    hi, lo = _split2(lstay)
    return e, _dot(jnp.concatenate([hi, lo], axis=1), tri2)


def _sb_weights(z, c2, valid, run):
    w = jnp.exp(z + c2[:, :SB_TK] + run)
    return w if valid is None else jnp.where(valid, w, 0.0)


EXP_IS_ZERO_BELOW = -110.0


def _max_row_norm(x):
    f = x.astype(F32)
    return jnp.sqrt(jnp.max(jnp.sum(f * f, axis=-1, keepdims=True)))


def _sb_score_bound(qs, kmax_ref):
    return _max_row_norm(qs) * jnp.max(kmax_ref[...]) * 1.01 + 1.0


def _sb_rest_is_zero(run_ref, bound):
    return jnp.max(jnp.maximum(run_ref[0], run_ref[1])) + bound < EXP_IS_ZERO_BELOW


def _sb_attn_fwd(qn, kn, vb, exchange=None):
    s, d = qn.shape
    tk = SB_TK
    tq = _pick(s, 256)
    nq, ndiag = s // tq, tq // tk
    assert tq % (2 * tk) == 0, "tiles below the diagonal are taken two at a time"
    npairs = d // LANES
    scale = 1.0 / math.sqrt(SB_HEAD_DIM)
    tri_ge2, _ = _sb_consts(tk)

    def body(q_ref, k_ref, v_ref, tri_ref, o_ref, acc_ref, run_ref, kmax_ref):
        qi = pl.program_id(1)

        @pl.when(qi == 0)
        def _():
            kmax_ref[...] = jnp.full(kmax_ref.shape, _max_row_norm(k_ref[...]), F32)

        low_half = lax.broadcasted_iota(jnp.int32, (tk, LANES), 1) < SB_HEAD_DIM
        row = lax.broadcasted_iota(jnp.int32, (tq, tk), 0)
        col = lax.broadcasted_iota(jnp.int32, (tq, tk), 1)
        qs = (q_ref[...].astype(F32) * scale).astype(BF16)
        bound = _sb_score_bound(qs, kmax_ref)
        acc_ref[...] = jnp.zeros_like(acc_ref)
        run_ref[...] = jnp.zeros_like(run_ref)
        n_full = qi * ndiag

        def sums(kb, dd):
            koff = pl.multiple_of(kb * tk, tk)
            kcat = _head_stack(k_ref[pl.ds(koff, tk), :], low_half)
            vcat = _head_stack(v_ref[pl.ds(koff, tk), :], low_half)
            z2 = _dot(qs, kcat, NT)
            valid = None if dd is None else row > col + dd * tk
            zs = [z2[:, h * tk:(h + 1) * tk] for h in range(2)]
            return zs, [_sb_tile_sums(z, valid, tri_ref[...])[1] for z in zs], valid, vcat

        def finish(zs, c2s, valid, vcat):
            ws = []
            for h in range(2):
                ws.append(_sb_weights(zs[h], c2s[h], valid, run_ref[h]).astype(BF16))
                run_ref[h] += c2s[h][:, tk:]
            acc_ref[...] += _dot(jnp.concatenate(ws, axis=1), vcat)

        def first_tiles(below):
            pres = [sums(n_full + dd, dd) for dd in reversed(range(ndiag))]
            pres += [sums(n_full - 1 - n, None) for n in range(below)]
            for pre in pres:
                finish(*pre)

        @pl.when(qi == 0)
        def _():
            first_tiles(0)

        @pl.when(qi > 0)
        def _():
            first_tiles(2)

        def two_tiles(carry):
            it, _ = carry
            kb = n_full - 1 - 2 * it
            first, second = sums(kb, None), sums(kb - 1, None)
            finish(*first)
            finish(*second)
            return it + 1, _sb_rest_is_zero(run_ref, bound)

        lax.while_loop(lambda c: jnp.logical_and(c[0] < n_full // 2, jnp.logical_not(c[1])), two_tiles,
                       (jnp.minimum(qi, 1), _sb_rest_is_zero(run_ref, bound)))
        o_ref[...] = acc_ref[...]

    blk = pl.BlockSpec((tq, LANES), lambda p, i: (i, p))
    full = pl.BlockSpec((s, LANES), lambda p, i: (0, p))
    (o,), moved = _call_with_exchange(
        body, exchange, grid=(npairs, nq),
        in_specs=[blk, full, full, pl.BlockSpec((2 * tk, 2 * tk), lambda p, i: (0, 0))],
        out_specs=[blk], out_shape=[jax.ShapeDtypeStruct((s, d), F32)],
        scratch_shapes=[pltpu.VMEM((tq, LANES), F32), pltpu.VMEM((2, tq, tk), F32), pltpu.VMEM((8, LANES), F32)],
        name="sb_attn_fwd", args=(qn, kn, vb, tri_ge2))
    return o, moved


def _sb_attn_bwd(qn, kn, vb, do, exchange=None):
    s, d = qn.shape
    tk = SB_TK
    tq = _pick(s, 256)
    nq, ndiag = s // tq, tq // tk
    assert tq % (2 * tk) == 0, "tiles below the diagonal are taken two at a time"
    npairs = d // LANES
    scale = 1.0 / math.sqrt(SB_HEAD_DIM)
    tri_ge2, tri_le2 = _sb_consts(tk)

    def body(q_ref, k_ref, v_ref, do_ref, tge_ref, tle_ref, dq_ref, dk_ref, dv_ref,
             g_cache, s_cache, run_ref, dq_acc, kmax_ref):
        qi = pl.program_id(1)

        @pl.when(qi == 0)
        def _():
            dk_ref[...] = jnp.zeros_like(dk_ref)
            dv_ref[...] = jnp.zeros_like(dv_ref)
            kmax_ref[...] = jnp.full(kmax_ref.shape, _max_row_norm(k_ref[...]), F32)

        low_half = lax.broadcasted_iota(jnp.int32, (tk, LANES), 1) < SB_HEAD_DIM
        row = lax.broadcasted_iota(jnp.int32, (tq, tk), 0)
        col = lax.broadcasted_iota(jnp.int32, (tq, tk), 1)
        qs = (q_ref[...].astype(F32) * scale).astype(BF16)
        bound = _sb_score_bound(qs, kmax_ref)
        dob = do_ref[...].astype(BF16)
        n_full = qi * ndiag

        def a_sums(kb, dd):
            koff = pl.multiple_of(kb * tk, tk)
            kcat = _head_stack(k_ref[pl.ds(koff, tk), :], low_half)
            vcat = _head_stack(v_ref[pl.ds(koff, tk), :], low_half)
            z2 = _dot(qs, kcat, NT)
            dw2 = _dot(dob, vcat, NT)
            valid = None if dd is None else row > col + dd * tk
            c2s = []
            for h in range(2):
                cols = slice(h * tk, (h + 1) * tk)
                z = z2[:, cols]
                e, c2 = _sb_tile_sums(z, valid, tge_ref[...])
                s_cache[kb, :, cols] = jnp.where(z >= 0, 1.0, e) / (1.0 + e)
                c2s.append(c2)
            return kb, koff, z2, dw2, c2s, valid

        def a_finish(kb, koff, z2, dw2, c2s, valid):
            ws = []
            for h in range(2):
                cols = slice(h * tk, (h + 1) * tk)
                w = _sb_weights(z2[:, cols], c2s[h], valid, run_ref[h])
                run_ref[h] += c2s[h][:, tk:]
                g_cache[kb, :, cols] = w * dw2[:, cols]
                ws.append(w.astype(BF16))
            dv2 = _dot(jnp.concatenate(ws, axis=1), dob, TN)
            dv_ref[pl.ds(koff, tk), :] += jnp.where(low_half, dv2[:tk], dv2[tk:])

        def b_sums(kb, dd):
            gs = [g_cache[kb, :, h * tk:(h + 1) * tk] for h in range(2)]
            p2s = [_dot(jnp.concatenate(_split2(g), axis=1), tle_ref[...]) for g in gs]
            return kb, gs, p2s, (None if dd is None else row > col + dd * tk)

        def b_finish(kb, gs, p2s, valid):
            koff = pl.multiple_of(kb * tk, tk)
            dzs = []
            for h in range(2):
                dz = gs[h] - s_cache[kb, :, h * tk:(h + 1) * tk] * (p2s[h][:, :tk] + run_ref[h])
                if valid is not None:
                    dz = jnp.where(valid, dz, 0.0)
                run_ref[h] += p2s[h][:, tk:]
                dzs.append(dz.astype(BF16))
            dzcat = jnp.concatenate(dzs, axis=1)
            dq_acc[...] += _dot(dzcat, _head_stack(k_ref[pl.ds(koff, tk), :], low_half))
            dk2 = _dot(dzcat, qs, TN)
            dk_ref[pl.ds(koff, tk), :] += jnp.where(low_half, dk2[:tk], dk2[tk:])

        run_ref[...] = jnp.zeros_like(run_ref)
        near = jnp.minimum(qi, 1)

        def a_first_tiles(below):
            pres = [a_sums(n_full + dd, dd) for dd in reversed(range(ndiag))]
            pres += [a_sums(n_full - 1 - n, None) for n in range(below)]
            for pre in pres:
                a_finish(*pre)

        @pl.when(qi == 0)
        def _():
            a_first_tiles(0)

        @pl.when(qi > 0)
        def _():
            a_first_tiles(2)

        def two_a(carry):
            it, _ = carry
            kb = n_full - 1 - 2 * it
            first, second = a_sums(kb, None), a_sums(kb - 1, None)
            a_finish(*first)
            a_finish(*second)
            return it + 1, _sb_rest_is_zero(run_ref, bound)

        trips, _ = lax.while_loop(lambda c: jnp.logical_and(c[0] < n_full // 2, jnp.logical_not(c[1])), two_a,
                                  (near, _sb_rest_is_zero(run_ref, bound)))

        run_ref[...] = jnp.zeros_like(run_ref)
        dq_acc[...] = jnp.zeros_like(dq_acc)
        kb_first = n_full - 2 * trips

        def two_b(it, carry):
            first, second = b_sums(kb_first + 2 * it, None), b_sums(kb_first + 2 * it + 1, None)
            b_finish(*first)
            b_finish(*second)
            return carry

        lax.fori_loop(0, trips - near, two_b, 0)

        def b_last_tiles(below):
            pres = [b_sums(n_full - below + n, None) for n in range(below)]
            pres += [b_sums(n_full + dd, dd) for dd in range(ndiag)]
            for pre in pres:
                b_finish(*pre)

        @pl.when(qi == 0)
        def _():
            b_last_tiles(0)

        @pl.when(qi > 0)
        def _():
            b_last_tiles(2)

        dq_ref[...] = dq_acc[...] * scale

    blk = pl.BlockSpec((tq, LANES), lambda p, i: (i, p))
    full = pl.BlockSpec((s, LANES), lambda p, i: (0, p))
    tri = pl.BlockSpec((2 * tk, 2 * tk), lambda p, i: (0, 0))
    return _call_with_exchange(
        body, exchange, grid=(npairs, nq),
        in_specs=[blk, full, full, blk, tri, tri],
        out_specs=[blk, full, full],
        out_shape=[jax.ShapeDtypeStruct((s, d), F32)] * 3,
        scratch_shapes=[pltpu.VMEM((s // tk, tq, 2 * tk), F32), pltpu.VMEM((s // tk, tq, 2 * tk), F32),
                        pltpu.VMEM((2, tq, tk), F32), pltpu.VMEM((tq, LANES), F32), pltpu.VMEM((8, LANES), F32)],
        name="sb_attn_bwd", args=(qn, kn, vb, do, tri_ge2, tri_le2))


def _hg_consts(c):
    levels = []
    h = c // 2
    while h >= 1:
        levels.append(h)
        h //= 2
    t = np.arange(c)
    j = t[None, :]
    rows, masks = [], []
    for h in levels:
        blk = t // (2 * h)
        mid = blk * 2 * h + h - 1
        second = (t % (2 * h)) >= h
        rows.append(second[:, None] & (j > mid[:, None]) & (j <= t[:, None]))
        rows.append((~second)[:, None] & (j > t[:, None]) & (j <= mid[:, None]))
        masks.append((blk[:, None] == blk[None, :]) & second[:, None] & (~second)[None, :])
    rows.append(j <= t[:, None])
    rows.append(j > t[:, None])
    masks.append(t[:, None] == t[None, :])
    m_all = np.concatenate(rows, axis=0).astype(np.float32)
    mask_all = np.stack(masks, axis=0).astype(np.float32)
    suffix = (t[None, :] >= t[:, None]).astype(np.float32)
    return len(levels), jnp.asarray(m_all, BF16), jnp.asarray(mask_all, F32), jnp.asarray(suffix, BF16)


def _split3(x):
    hi = x.astype(BF16)
    r1 = x - hi.astype(F32)
    mid = r1.astype(BF16)
    lo = (r1 - mid.astype(F32)).astype(BF16)
    return jnp.concatenate([hi, mid, lo], axis=1)


def _join3(e):
    n = e.shape[1] // 3
    return e[:, :n] + e[:, n:2 * n] + e[:, 2 * n:]


def _hg_gates(qr, fr, lb):
    sq = _sigmoid(qr)
    sf = _sigmoid(fr)
    forget = lb + (1.0 - lb) * sf
    return qr * sq, sq, sf, forget, jnp.log(forget), 1.0 - forget


def _hg_scores(q, k, expo, masks, nlev, c):
    qb, kb = q.astype(BF16), k.astype(BF16)
    a = masks[nlev] * _dot(qb, kb, NT)
    scaled = []
    for li in range(nlev):
        fq = jnp.exp(expo[(2 * li) * c:(2 * li + 1) * c])
        fk = jnp.exp(expo[(2 * li + 1) * c:(2 * li + 2) * c])
        qs, ks = (q * fq).astype(BF16), (k * fk).astype(BF16)
        a = a + masks[li] * _dot(qs, ks, NT)
        scaled.append((qs, ks, fq, fk))
    return a, scaled, qb, kb


def _hg_heads_per_step(nh):
    return 2 if nh % 2 == 0 else 1


def _hg_fwd(proj, lb_row, gain_row, exchange=None):
    s, d4 = proj.shape
    d = d4 // 4
    nh = d // HG_HEAD_DIM
    c = min(HG_CHUNK, s)
    tb = _pick(s, 512)
    ncb = tb // c
    nlev, m_all, mask_all, _ = _hg_consts(c)
    nrow = m_all.shape[0]

    hp = _hg_heads_per_step(nh)
    wide = hp * HG_HEAD_DIM

    def body(q_ref, f_ref, i_ref, g_ref, lb_ref, gain_ref, mall_ref, mask_ref, y_ref, o_ref, st_out_ref, st_ref):
        b = pl.program_id(1)

        @pl.when(b == 0)
        def _():
            st_ref[...] = jnp.zeros_like(st_ref)

        gain = gain_ref[...]

        def inside(ci, hh):
            rows = pl.ds(pl.multiple_of(ci * c, c), c)
            cols = slice(hh * HG_HEAD_DIM, (hh + 1) * HG_HEAD_DIM)
            q, _, _, _, lf, k = _hg_gates(q_ref[rows, cols], f_ref[rows, cols], lb_ref[:, cols])
            v = i_ref[rows, cols].astype(BF16)
            expo = _join3(_dot(mall_ref[...], _split3(lf)))
            a, _, _, _ = _hg_scores(q, k, expo, mask_ref[...], nlev, c)
            b_cum = expo[2 * nlev * c:(2 * nlev + 1) * c]
            e_tail = expo[(2 * nlev + 1) * c:(2 * nlev + 2) * c]
            q_in = (q * jnp.exp(b_cum)).astype(BF16)
            k_dec = (k * jnp.exp(e_tail)).astype(BF16)
            return ci, hh, rows, cols, q_in, _dot(a.astype(BF16), v), jnp.exp(b_cum[c - 1:c, :]), _dot(v, k_dec, TN)

        def across(ci, hh, rows, cols, q_in, o_intra, decay, kv):
            st = st_ref[hh]
            st_out_ref[ci, hh] = st
            o = _dot(q_in, st.astype(BF16), NT) + o_intra
            st_ref[hh] = st * decay + kv
            o_ref[rows, cols] = o
            r = lax.rsqrt(jnp.mean(o * o, axis=-1, keepdims=True) + NORM_EPS)
            y_ref[rows, cols] = (o * r * gain * _sigmoid(g_ref[rows, cols])).astype(y_ref.dtype)

        per_trip = 2 if ncb % 2 == 0 else 1

        def trip(it, carry):
            ready = [inside(per_trip * it + n, hh) for n in range(per_trip) for hh in range(hp)]
            for r in ready:
                across(*r)
            return carry

        lax.fori_loop(0, ncb // per_trip, trip, 0)

    part = lambda k: pl.BlockSpec((tb, wide), lambda h, b: (b, k * (nh // hp) + h))
    head_row = pl.BlockSpec((1, wide), lambda h, b: (0, h))
    tok = pl.BlockSpec((tb, wide), lambda h, b: (b, h))
    return _call_with_exchange(
        body, exchange, grid=(nh // hp, s // tb),
        in_specs=[part(0), part(1), part(2), part(3), head_row,
                  pl.BlockSpec((1, HG_HEAD_DIM), lambda h, b: (0, 0)),
                  pl.BlockSpec((nrow, c), lambda h, b: (0, 0)),
                  pl.BlockSpec((nlev + 1, c, c), lambda h, b: (0, 0, 0))],
        out_specs=[tok, tok, pl.BlockSpec((ncb, hp, HG_HEAD_DIM, HG_HEAD_DIM), lambda h, b: (b, h, 0, 0))],
        out_shape=[jax.ShapeDtypeStruct((s, d), BF16), jax.ShapeDtypeStruct((s, d), F32),
                   jax.ShapeDtypeStruct((s // c, nh, HG_HEAD_DIM, HG_HEAD_DIM), F32)],
        scratch_shapes=[pltpu.VMEM((hp, HG_HEAD_DIM, HG_HEAD_DIM), F32)],
        name="hg_fwd", args=(proj, proj, proj, proj, lb_row, gain_row, m_all, mask_all))


def _hg_bwd(proj, lb_row, gain_row, o_saved, states, dy, exchange=None):
    s, d4 = proj.shape
    d = d4 // 4
    nh = d // HG_HEAD_DIM
    c = min(HG_CHUNK, s)
    tb = _pick(s, 512)
    ncb = tb // c
    nb = s // tb
    nlev, m_all, mask_all, suffix = _hg_consts(c)
    nrow = m_all.shape[0]
    hp = _hg_heads_per_step(nh)
    wide = hp * HG_HEAD_DIM

    def body(q_ref, f_ref, i_ref, g_ref, lb_ref, gain_ref, o_ref, st_in_ref, dy_ref, mall_ref, mask_ref, suf_ref,
             dproj_ref, dlb_ref, dgain_ref, dst_ref, run_ref):
        b = pl.program_id(1)

        @pl.when(b == 0)
        def _():
            dst_ref[...] = jnp.zeros_like(dst_ref)
            run_ref[...] = jnp.zeros_like(run_ref)
            dlb_ref[...] = jnp.zeros_like(dlb_ref)
            dgain_ref[...] = jnp.zeros_like(dgain_ref)

        gain = gain_ref[...]

        def head_chunk(ci, rows, hh, cols):
            lb = lb_ref[:, cols]
            qr, fr = q_ref[rows, cols], f_ref[rows, cols]
            q, sq, sf, forget, lf, k = _hg_gates(qr, fr, lb)
            v = i_ref[rows, cols].astype(BF16)
            expo = _join3(_dot(mall_ref[...], _split3(lf)))
            masks = mask_ref[...]
            o = o_ref[rows, cols]
            dyv = dy_ref[rows, cols]
            sg = _sigmoid(g_ref[rows, cols])
            r = lax.rsqrt(jnp.mean(o * o, axis=-1, keepdims=True) + NORM_EPS)
            ohat = o * r
            dyn = dyv * sg
            dproj_ref[3, rows, cols] = (dyv * ohat * gain * sg * (1.0 - sg)).astype(dproj_ref.dtype)
            dgain_ref[:, cols] += jnp.sum(dyn * ohat, axis=0, keepdims=True)
            dohat = dyn * gain
            do = (r * (dohat - ohat * jnp.mean(dohat * ohat, axis=-1, keepdims=True))).astype(BF16)
            dst = dst_ref[hh]
            dstb = dst.astype(BF16)
            qb, kb = q.astype(BF16), k.astype(BF16)
            f_cum = jnp.exp(expo[2 * nlev * c:(2 * nlev + 1) * c])
            f_tail = jnp.exp(expo[(2 * nlev + 1) * c:(2 * nlev + 2) * c])
            q_in = (q * f_cum).astype(BF16)
            k_dec = (k * f_tail).astype(BF16)
            t_in = _join3(_dot(do, _split3(st_in_ref[ci, hh])))
            t_st = _join3(_dot(v, _split3(dst)))
            da = _dot(do, v, NT)
            dam = (masks[nlev] * da).astype(BF16)
            a = masks[nlev] * _dot(qb, kb, NT)
            dq = t_in * f_cum + _dot(dam, kb)
            dk = t_st * f_tail + _dot(dam, qb, TN)
            db = q_in.astype(F32) * t_in - k_dec.astype(F32) * t_st
            for li in range(nlev):
                fq = jnp.exp(expo[(2 * li) * c:(2 * li + 1) * c])
                fk = jnp.exp(expo[(2 * li + 1) * c:(2 * li + 2) * c])
                qs, ks = (q * fq).astype(BF16), (k * fk).astype(BF16)
                a = a + masks[li] * _dot(qs, ks, NT)
                dam = (masks[li] * da).astype(BF16)
                t_q = _dot(dam, ks)
                t_k = _dot(dam, qs, TN)
                dq = dq + t_q * fq
                dk = dk + t_k * fk
                db = db + (qs.astype(F32) * t_q - ks.astype(F32) * t_k)
            dv = _dot(a.astype(BF16), do, TN) + _dot(k_dec, dstb, NT)
            dst_ref[hh] = dst * f_cum[c - 1:c, :] + _dot(do, q_in, TN)
            dlf = _join3(_dot(suf_ref[...], _split3(db))) + run_ref[hh]
            run_ref[hh] = dlf[0:1, :]
            dforget = dlf / forget - dk
            dlb_ref[:, cols] += jnp.sum(dforget * (1.0 - sf), axis=0, keepdims=True)
            dproj_ref[1, rows, cols] = (dforget * (1.0 - lb) * sf * (1.0 - sf)).astype(dproj_ref.dtype)
            dproj_ref[0, rows, cols] = (dq * sq * (1.0 + qr * (1.0 - sq))).astype(dproj_ref.dtype)
            dproj_ref[2, rows, cols] = dv.astype(dproj_ref.dtype)

        def chunk(it, carry):
            ci = ncb - 1 - it
            rows = pl.ds(pl.multiple_of(ci * c, c), c)
            for hh in range(hp):
                head_chunk(ci, rows, hh, slice(hh * HG_HEAD_DIM, (hh + 1) * HG_HEAD_DIM))
            return carry

        lax.fori_loop(0, ncb, chunk, 0)

    part = lambda k: pl.BlockSpec((tb, wide), lambda h, b: (nb - 1 - b, k * (nh // hp) + h))
    head_row = pl.BlockSpec((1, wide), lambda h, b: (0, h))
    tok = pl.BlockSpec((tb, wide), lambda h, b: (nb - 1 - b, h))
    const2 = lambda shape: pl.BlockSpec(shape, lambda h, b: (0, 0))
    return _call_with_exchange(
        body, exchange, grid=(nh // hp, nb),
        in_specs=[part(0), part(1), part(2), part(3), head_row, const2((1, HG_HEAD_DIM)), tok,
                  pl.BlockSpec((ncb, hp, HG_HEAD_DIM, HG_HEAD_DIM), lambda h, b: (nb - 1 - b, h, 0, 0)),
                  tok, const2((nrow, c)), pl.BlockSpec((nlev + 1, c, c), lambda h, b: (0, 0, 0)), const2((c, c))],
        out_specs=[pl.BlockSpec((4, tb, wide), lambda h, b: (0, nb - 1 - b, h)), head_row, head_row],
        out_shape=[jax.ShapeDtypeStruct((4, s, d), BF16)] + [jax.ShapeDtypeStruct((1, d), F32)] * 2,
        scratch_shapes=[pltpu.VMEM((hp, HG_HEAD_DIM, HG_HEAD_DIM), F32), pltpu.VMEM((hp, 1, HG_HEAD_DIM), F32)],
        name="hg_bwd", args=(proj, proj, proj, proj, lb_row, gain_row, o_saved, states, dy, m_all, mask_all, suffix))


def _lb_fwd(logits):
    n, d = logits.shape

    def body(l_ref, lb_ref, p_ref):
        rows = [l_ref[i:i + 1, :] for i in range(n)]
        m = functools.reduce(jnp.maximum, rows)
        es = [jnp.exp(r - m) for r in rows]
        tot = functools.reduce(lambda a, b: a + b, es)
        ps = [e / tot for e in es]
        run = jnp.zeros_like(ps[0])
        for i in range(n):
            run = run + ps[i]
            lb_ref[i:i + 1, :] = run - ps[0]
            p_ref[i:i + 1, :] = ps[i]

    return pl.pallas_call(
        body, out_shape=[jax.ShapeDtypeStruct((n, d), F32)] * 2, name="lb_fwd",
    )(logits)


def _lb_bwd(p, dlb):
    n, d = p.shape

    def body(p_ref, dlb_ref, dl_ref):
        ps = [p_ref[i:i + 1, :] for i in range(n)]
        ds = [dlb_ref[i:i + 1, :] for i in range(n)]
        total = functools.reduce(lambda a, b: a + b, ds)
        dps = []
        for i in range(n):
            dp = functools.reduce(lambda a, b: a + b, ds[i:])
            dps.append(dp - total if i == 0 else dp)
        inner = functools.reduce(lambda a, b: a + b, [pi * di for pi, di in zip(ps, dps)])
        for i in range(n):
            dl_ref[i:i + 1, :] = ps[i] * (dps[i] - inner)

    return pl.pallas_call(body, out_shape=jax.ShapeDtypeStruct((n, d), F32), name="lb_bwd")(p, dlb)


def _as2d(a):
    return a.reshape(-1, a.shape[-1])


def _adamw(w, m, v, grads, exchange=None):
    shape = w.shape
    w2, m2, v2 = _as2d(w), _as2d(m), _as2d(v)
    g2 = [_as2d(g) for g in grads]
    rows, cols = w2.shape
    tr = _pick(rows, 512)
    ng = len(g2)
    bc1 = 1.0 - ADAM_B1 ** ADAM_STEP
    bc2 = 1.0 - ADAM_B2 ** ADAM_STEP

    def body(w_ref, m_ref, v_ref, *rest):
        g = rest[0][...]
        for extra in rest[1:ng]:
            g = g + extra[...]
        g_out, d_out, m_out, v_out = rest[ng:]
        mn = ADAM_B1 * m_ref[...] + (1.0 - ADAM_B1) * g
        vn = ADAM_B2 * v_ref[...] + (1.0 - ADAM_B2) * (g * g)
        m_hat = mn / bc1
        v_hat = vn / bc2
        g_out[...] = g
        d_out[...] = -ADAM_LR * (m_hat / (jnp.sqrt(v_hat) + ADAM_EPS) + ADAM_WD * w_ref[...])
        m_out[...] = mn
        v_out[...] = vn

    spec = pl.BlockSpec((tr, cols), lambda i: (i, 0))
    outs, moved = _call_with_exchange(
        body, exchange, grid=(rows // tr,), in_specs=[spec] * (3 + ng), out_specs=[spec] * 4,
        out_shape=[jax.ShapeDtypeStruct((rows, cols), F32)] * 4, scratch_shapes=[], name="adamw",
        args=(w2, m2, v2, *g2))
    result = tuple(o.reshape(shape) for o in outs)
    return result if exchange is None else (result, moved)


def _sum_slots(parts, recv, chip, into, index):
    _, rows, cols = parts.shape
    tr = _pick(rows, 512)

    def body(chip_ref, own_ref, r0_ref, r1_ref, r2_ref, into_ref, o_ref):
        f = lambda r: r[...].astype(F32)
        o_ref[...] = ((f(own_ref) + f(r0_ref)) + f(r1_ref)) + f(r2_ref)

    grid_spec = pltpu.PrefetchScalarGridSpec(
        num_scalar_prefetch=1, grid=(rows // tr,),
        in_specs=[pl.BlockSpec((None, tr, cols), lambda i, chip_ref: (chip_ref[0], i, 0))]
        + [pl.BlockSpec((None, tr, cols), functools.partial(lambda i, chip_ref, k: (k, i, 0), k=k)) for k in range(3)]
        + [pl.BlockSpec(memory_space=pl.ANY)],
        out_specs=pl.BlockSpec((None, tr, cols), lambda i, chip_ref: (index, i, 0)))
    return pl.pallas_call(
        body, grid_spec=grid_spec, out_shape=jax.ShapeDtypeStruct(into.shape, F32),
        input_output_aliases={5: 0}, compiler_params=_params(("parallel",)), name="sum_slots",
    )(chip, parts, recv, recv, recv, into)


def _pack_rows(pieces):
    cols = pieces[0].shape[1]
    used = sum(p.shape[0] for p in pieces)
    rows = -(-used // 8) * 8

    def body(*refs):
        out_ref = refs[-1]
        at = 0
        for ref in refs[:-1]:
            out_ref[at:at + ref.shape[0], :] = ref[...]
            at += ref.shape[0]
        if at < rows:
            out_ref[at:rows, :] = jnp.zeros((rows - at, cols), F32)

    return pl.pallas_call(body, out_shape=jax.ShapeDtypeStruct((rows, cols), F32), name="pack_rows")(*pieces)


def _sum_devices(gathered):
    n, rows, cols = gathered.shape

    def body(g_ref, o_ref):
        acc = g_ref[0]
        for i in range(1, n):
            acc = acc + g_ref[i]
        o_ref[...] = acc

    return pl.pallas_call(body, out_shape=jax.ShapeDtypeStruct((rows, cols), F32), name="sum_devices")(gathered)


def _coords():
    return lax.axis_index("x"), lax.axis_index("y"), lax.axis_index("c")


def _chip_peers(x, y, c):
    out = []
    for fx, fy in ((0, 1), (1, 0), (1, 1)):
        px = 1 - x if fx else x
        py = 1 - y if fy else y
        out.append(((px, py, c), 2 * px + py))
    return out


class _ChipExchange:
    def __init__(self, kind, arrays):
        self.kind, self.kinds, self.arrays, self.n = kind, [kind] * len(arrays), list(arrays), len(arrays)
        self._shapes()

    def also(self, kind, arrays):
        self.kinds += [kind] * len(arrays)
        self.arrays += list(arrays)
        self.n = len(self.arrays)
        self._shapes()
        return self

    def _shapes(self):
        lead = {"gather": lambda a: (N_CHIPS,) + a.shape, "scatter": lambda a: (3,) + a.shape[1:],
                "swap": lambda a: a.shape}
        self.out_shape = [jax.ShapeDtypeStruct(lead[k](a), a.dtype) for k, a in zip(self.kinds, self.arrays)]
        for k, a in zip(self.kinds, self.arrays):
            assert k != "gather" or a.shape[0] % 2 == 0, "a gathered array is cut in two along its leading axis"
        self.scratch = [pltpu.SemaphoreType.DMA((6 * self.n,)), pltpu.SemaphoreType.DMA((6 * self.n,)),
                        pltpu.SemaphoreType.DMA((self.n,))]

    def copies(self, ins, outs, send_sems, recv_sems, local_sems):
        x, y, c = _coords()
        me = 2 * x + y
        sibling = (x, y, 1 - c)
        starts, waits, last = [], [], []
        for t, kind in enumerate(self.kinds):
            if kind == "swap":
                cp = pltpu.make_async_remote_copy(
                    src_ref=ins[t], dst_ref=outs[t], send_sem=send_sems.at[6 * t], recv_sem=recv_sems.at[6 * t],
                    device_id=sibling, device_id_type=MESH)
                starts.append(cp.start)
                waits += [cp.wait_send, cp.wait_recv]
                continue
            if kind == "gather":
                own = pltpu.make_async_copy(ins[t], outs[t].at[me], local_sems.at[t])
                starts.append(own.start)
                waits.append(own.wait)
                half_rows = ins[t].shape[0] // 2
                mine = pl.ds(c * half_rows, half_rows)
                theirs = pl.ds((1 - c) * half_rows, half_rows)
            for k, (peer, peer_chip) in enumerate(_chip_peers(x, y, c)):
                sems = dict(send_sem=send_sems.at[6 * t + k], recv_sem=recv_sems.at[6 * t + k],
                            device_id=peer, device_id_type=MESH)
                if kind == "scatter":
                    send = pltpu.make_async_remote_copy(src_ref=ins[t].at[peer_chip], dst_ref=outs[t].at[k], **sems)
                    starts.append(send.start)
                    waits += [send.wait_send, send.wait_recv]
                    continue
                send = pltpu.make_async_remote_copy(
                    src_ref=ins[t].at[mine], dst_ref=outs[t].at[me].at[mine], **sems)
                landed = outs[t].at[peer_chip].at[mine]
                recv = pltpu.make_async_remote_copy(src_ref=ins[t].at[mine], dst_ref=landed, **sems)
                pass_on = pltpu.make_async_remote_copy(
                    src_ref=landed, dst_ref=landed, send_sem=send_sems.at[6 * t + 3 + k],
                    recv_sem=recv_sems.at[6 * t + 3 + k], device_id=sibling, device_id_type=MESH)
                handed = pltpu.make_async_remote_copy(
                    src_ref=landed, dst_ref=outs[t].at[peer_chip].at[theirs], send_sem=send_sems.at[6 * t + 3 + k],
                    recv_sem=recv_sems.at[6 * t + 3 + k], device_id=sibling, device_id_type=MESH)
                starts.append(send.start)
                waits += [recv.wait_recv, pass_on.start]
                last += [send.wait_send, pass_on.wait_send, handed.wait_recv]
        return starts, waits + last

    def run(self, name):
        n = self.n

        def body(*refs):
            starts, waits = self.copies(refs[:n], refs[n:2 * n], *refs[2 * n:])
            for f in starts + waits:
                f()

        return pl.pallas_call(body, in_specs=[HBM_SPEC] * n, out_specs=[HBM_SPEC] * n, out_shape=self.out_shape,
                              scratch_shapes=self.scratch, name=name)(*self.arrays)


def _call_with_exchange(body, exchange, *, grid, in_specs, out_specs, out_shape, scratch_shapes, name, args,
                        sequential=False):
    if exchange is None:
        first_axis = "arbitrary" if sequential else "parallel"
        outs = pl.pallas_call(body, grid=grid, in_specs=in_specs, out_specs=out_specs, out_shape=out_shape,
                              scratch_shapes=scratch_shapes,
                              compiler_params=_params((first_axis,) + ("arbitrary",) * (len(grid) - 1)),
                              name=name)(*args)
        return outs, []
    n_in, n_out, n_scr, n = len(in_specs), len(out_specs), len(scratch_shapes), exchange.n

    def wrapped(*refs):
        ins, ex_in = refs[:n_in], refs[n_in:n_in + n]
        outs = refs[n_in + n:n_in + n + n_out]
        ex_out = refs[n_in + n + n_out:n_in + 2 * n + n_out]
        scr = refs[n_in + 2 * n + n_out:n_in + 2 * n + n_out + n_scr]
        sems = refs[n_in + 2 * n + n_out + n_scr:]
        ids = [pl.program_id(a) for a in range(len(grid))]
        first = functools.reduce(jnp.logical_and, [i == 0 for i in ids])
        last = functools.reduce(jnp.logical_and, [i == g - 1 for i, g in zip(ids, grid)])

        @pl.when(first)
        def _():
            for f in exchange.copies(ex_in, ex_out, *sems)[0]:
                f()

        body(*ins, *outs, *scr)

        @pl.when(last)
        def _():
            for f in exchange.copies(ex_in, ex_out, *sems)[1]:
                f()

    res = pl.pallas_call(
        wrapped, grid=grid, in_specs=list(in_specs) + [HBM_SPEC] * n, out_specs=list(out_specs) + [HBM_SPEC] * n,
        out_shape=list(out_shape) + exchange.out_shape, scratch_shapes=list(scratch_shapes) + exchange.scratch,
        compiler_params=_params(("arbitrary",) * len(grid)), name=name + "_" + exchange.kind,
    )(*args, *exchange.arrays)
    return res[:n_out], res[n_out:]


def _gather_devices(a):
    def body(in_ref, out_ref, send_sems, recv_sems, local_sem):
        x, y, c = _coords()
        me = 4 * x + 2 * y + c
        own = pltpu.make_async_copy(in_ref, out_ref.at[me], local_sem)
        own.start()
        waits = [own.wait]
        for k in range(1, N_DEVICES):
            px = 1 - x if k & 4 else x
            py = 1 - y if k & 2 else y
            pc = 1 - c if k & 1 else c
            peer = (px, py, pc)
            send = pltpu.make_async_remote_copy(
                src_ref=in_ref, dst_ref=out_ref.at[me], send_sem=send_sems.at[k - 1], recv_sem=recv_sems.at[k - 1],
                device_id=peer, device_id_type=MESH)
            send.start()
            recv = pltpu.make_async_remote_copy(
                src_ref=in_ref, dst_ref=out_ref.at[4 * px + 2 * py + pc], send_sem=send_sems.at[k - 1],
                recv_sem=recv_sems.at[k - 1], device_id=peer, device_id_type=MESH)
            waits += [send.wait_send, recv.wait_recv]
        for w in waits:
            w()

    return pl.pallas_call(
        body, in_specs=[HBM_SPEC], out_specs=HBM_SPEC,
        out_shape=jax.ShapeDtypeStruct((N_DEVICES,) + a.shape, a.dtype),
        scratch_shapes=[pltpu.SemaphoreType.DMA((N_DEVICES - 1,)), pltpu.SemaphoreType.DMA((N_DEVICES - 1,)),
                        pltpu.SemaphoreType.DMA],
        name="gather_devices",
    )(a)


def _mlp_grad_epilogue(r, u):
    return r * (2.0 * jnp.maximum(u, 0.0))


def kernel(x, norm_gains, sb_w_qkv, sb_q_gain, sb_k_gain, sb_w_o, hg_w_in, hg_lb_logits, hg_norm_gain, hg_w_o, mlp_w1, mlp_w2, loss_target, m_norm_gains, m_sb_w_qkv, m_sb_q_gain, m_sb_k_gain, m_sb_w_o, m_hg_w_in, m_hg_lb_logits, m_hg_norm_gain, m_hg_w_o, m_mlp_w1, m_mlp_w2, v_norm_gains, v_sb_w_qkv, v_sb_q_gain, v_sb_k_gain, v_sb_w_o, v_hg_w_in, v_hg_lb_logits, v_hg_norm_gain, v_hg_w_o, v_mlp_w1, v_mlp_w2):
    depth = norm_gains.shape[0]
    n_sb, n_hg = sb_w_qkv.shape[0], hg_w_in.shape[0]
    xs, tgt = x[0], loss_target[0]
    s, d = xs.shape
    dq = d // N_CHIPS
    cx, cy, cc = _coords()
    chip = 2 * cx + cy
    chip_arr = jnp.reshape(chip, (1,)).astype(jnp.int32)

    def mixer_weights(layer):
        j = layer // 2
        return (sb_w_qkv[j], sb_w_o[j]) if layer % 2 == 0 else (hg_w_in[j], hg_w_o[j])

    w_in_g, ng_g, lbl_g = _ChipExchange(
        "gather", [mixer_weights(0)[0].astype(BF16), norm_gains, hg_lb_logits]).run("gather_first")
    gains = jnp.transpose(ng_g, (1, 2, 0, 3)).reshape(depth, 2, d)
    logits = jnp.transpose(lbl_g, (1, 0, 2)).reshape(n_hg, d)
    lbs, lb_p = _lb_fwd(logits)
    qg_rows = [jnp.tile(sb_q_gain[j], d // SB_HEAD_DIM)[None] for j in range(n_sb)]
    kg_rows = [jnp.tile(sb_k_gain[j], d // SB_HEAD_DIM)[None] for j in range(n_sb)]

    saved, wts = [], []
    xc = xs
    for layer in range(depth):
        j = layer // 2
        ahead = [mixer_weights(layer)[1], mlp_w1[layer], mlp_w2[layer]]
        if layer + 1 < depth:
            ahead.append(mixer_weights(layer + 1)[0])
        gather = _ChipExchange("gather", [a.astype(BF16) for a in ahead])
        h1 = _rmsnorm_fwd(xc, gains[layer, 0][None])
        if layer % 2 == 0:
            qkv = _mm_fwd_cols(h1, w_in_g, name="sb_qkv")
            qn, kn, vb = _qk_norm_fwd(qkv, qg_rows[j], kg_rows[j])
            o, moved = _sb_attn_fwd(qn, kn, vb, gather)
            x_mid = _mm_fwd_rows(o, moved[0], residual=xc, name="sb_out")
            mix = (qkv, qn, kn, vb, o)
        else:
            proj = _mm_fwd_cols(h1, w_in_g, name="hg_in")
            (y, o, states), moved = _hg_fwd(proj, lbs[j][None], hg_norm_gain[j][None], gather)
            x_mid = _mm_fwd_rows(y, moved[0], residual=xc, name="hg_out")
            mix = (proj, y, o, states)
        w_out_g, w1_g, w2_g = moved[:3]
        h2 = _rmsnorm_fwd(x_mid, gains[layer, 1][None])
        u = _mm_fwd_cols(h2, w1_g, name="mlp_up")
        x_out = _mm_fwd_rows(u, w2_g, residual=x_mid, a_fn=_relu2, name="mlp_down")
        saved.append((xc, h1, mix, x_mid, h2, u))
        wts.append((w_in_g, w_out_g, w1_g, w2_g))
        w_in_g = moved[3] if layer + 1 < depth else None
        xc = x_out

    sq, dx = _loss_head(xc, tgt)
    loss = lax.psum(jnp.sum(sq) * (0.5 / d), ("x", "y", "c"))

    dgains = [[None, None] for _ in range(depth)]
    dqg, dkg = [None] * n_sb, [None] * n_sb
    dhgain, dlb = [None] * n_hg, [None] * n_hg
    grads, received, pending = {}, {}, []

    def ready(key, parts):
        grads[key] = parts
        pending.append(key)

    def scatter_of(keys):
        return _ChipExchange("scatter", [grads[k] for k in keys]) if keys else None

    def sent(keys, moved):
        for k, r in zip(keys, moved):
            received[k] = r
            pending.remove(k)

    def chip_sum(kind, layers):
        total = jnp.zeros((len(layers),) + grads[kind, layers[0]].shape[1:], F32)
        for index, l in enumerate(layers):
            total = _sum_slots(grads[kind, l], received[kind, l], chip_arr, total, index)
        return total

    sb_layers, hg_layers = range(0, depth, 2), range(1, depth, 2)
    tensors = [("in", sb_layers), ("out", sb_layers), ("in", hg_layers), ("out", hg_layers),
               ("w1", range(depth)), ("w2", range(depth))]

    for layer in reversed(range(depth)):
        j = layer // 2
        x_in, h1, mix, x_mid, h2, u = saved[layer]
        w_in_g, w_out_g, w1_g, w2_g = wts[layer]
        du = _mm_bwd_rows(dx, w2_g, name="mlp_down_dx", out_dtype=BF16, epi_fn=_mlp_grad_epilogue, epi_args=(u,))
        ready(("w2", layer), _mm_dw_rows(u, dx, a_fn=_relu2, name="mlp_down_dw"))
        ready(("w1", layer), _mm_dw_cols(h2, du, name="mlp_up_dw"))
        dx, dgains[layer][1] = _mm_bwd_cols_norm(du, w1_g, x_mid, gains[layer, 1][None], dx, name="mlp_up_dx")
        if layer % 2 == 0:
            qkv, qn, kn, vb, o = mix
            do = _mm_bwd_rows(dx, w_out_g, name="sb_out_dx")
            ready(("out", layer), _mm_dw_rows(o, dx, name="sb_out_dw"))
            keys = list(pending)
            (dqn, dkn, dv), moved = _sb_attn_bwd(qn, kn, vb, do, scatter_of(keys))
            sent(keys, moved)
            d_in, dqg[j], dkg[j] = _qk_norm_bwd(qkv, qg_rows[j], kg_rows[j], dqn, dkn, dv)
            ready(("in", layer), _mm_dw_cols(h1, d_in, name="sb_qkv_dw"))
            dx_name = "sb_qkv_dx"
        else:
            proj, y, o, states = mix
            dy = _mm_bwd_rows(dx, w_out_g, name="hg_out_dx")
            ready(("out", layer), _mm_dw_rows(y, dx, name="hg_out_dw"))
            keys = list(pending)
            (d_in, dlb[j], dhgain[j]), moved = _hg_bwd(
                proj, lbs[j][None], hg_norm_gain[j][None], o, states, dy, scatter_of(keys))
            sent(keys, moved)
            ready(("in", layer), _mm_dw_cols(h1, d_in, name="hg_in_dw"))
            dx_name = "hg_in_dx"
        if layer > 0:
            dx, dgains[layer][0] = _mm_bwd_cols_norm(d_in, w_in_g, x_in, gains[layer, 0][None], dx, name=dx_name)
        else:
            keys = list(pending)
            early_sums = [chip_sum(*t) for t in tensors[1:]]
            (dx, dgains[layer][0]), moved = _mm_bwd_cols_norm(
                d_in, w_in_g, x_in, gains[layer, 0][None], dx, name=dx_name,
                exchange=scatter_of(keys).also("swap", early_sums))
            sent(keys, moved[:len(keys)])
            early_other = moved[len(keys):]
    grad_x = dx[None]
    dlogits = _lb_bwd(lb_p, jnp.concatenate(dlb, axis=0))

    big_w = [sb_w_qkv, sb_w_o, hg_w_in, hg_w_o, mlp_w1, mlp_w2]
    big_m = [m_sb_w_qkv, m_sb_w_o, m_hg_w_in, m_hg_w_o, m_mlp_w1, m_mlp_w2]
    big_v = [v_sb_w_qkv, v_sb_w_o, v_hg_w_in, v_hg_w_o, v_mlp_w1, v_mlp_w2]
    late_sum = chip_sum(*tensors[0])
    big_second, late_other = _adamw(big_w[1], big_m[1], big_v[1], [early_sums[0], early_other[0]],
                                    _ChipExchange("swap", [late_sum]))
    big = [_adamw(big_w[0], big_m[0], big_v[0], [late_sum, late_other[0]]), big_second]
    big += [_adamw(w, m, v, [a, b]) for w, m, v, a, b in
            zip(big_w[2:], big_m[2:], big_v[2:], early_sums[1:], early_other[1:])]

    pieces = [r for pair in dgains for r in pair] + [dlogits] + dqg + dkg + dhgain
    small = _sum_devices(_gather_devices(_pack_rows(pieces)))
    my_cols = lambda a: lax.dynamic_slice_in_dim(a, chip * dq, dq, axis=1)
    fold = lambda rows, width: jnp.sum(rows.reshape(rows.shape[0], -1, width), axis=1)
    base = 2 * depth + n_hg
    g_ng = my_cols(small[0:2 * depth]).reshape(norm_gains.shape)
    g_lbl = my_cols(small[2 * depth:base])
    g_qg = fold(small[base:base + n_sb], SB_HEAD_DIM)
    g_kg = fold(small[base + n_sb:base + 2 * n_sb], SB_HEAD_DIM)
    g_hgn = fold(small[base + 2 * n_sb:base + 2 * n_sb + n_hg], HG_HEAD_DIM)
    r_ng = _adamw(norm_gains, m_norm_gains, v_norm_gains, [g_ng])
    r_qg = _adamw(sb_q_gain, m_sb_q_gain, v_sb_q_gain, [g_qg])
    r_kg = _adamw(sb_k_gain, m_sb_k_gain, v_sb_k_gain, [g_kg])
    r_lbl = _adamw(hg_lb_logits, m_hg_lb_logits, v_hg_lb_logits, [g_lbl])
    r_hgn = _adamw(hg_norm_gain, m_hg_norm_gain, v_hg_norm_gain, [g_hgn])

    per_weight = [r_ng, big[0], r_qg, r_kg, big[1], big[2], r_lbl, r_hgn, big[3], big[4], big[5]]
    outs = [loss, grad_x]
    for field in range(4):
        outs += [r[field] for r in per_weight]
    return tuple(outs)
```

```python
import functools
import math

import numpy as np
import jax
import jax.numpy as jnp
from jax import lax
from jax.experimental import pallas as pl
from jax.experimental.pallas import tpu as pltpu

F32 = jnp.float32
BF16 = jnp.bfloat16
GRAD_SLOT_DTYPE = jnp.bfloat16

NORM_EPS = 1e-6
SB_HEAD_DIM = 64
HG_HEAD_DIM = 128
HG_CHUNK = 128
LANES = 128
VMEM_LIMIT_BYTES = 56 * 2 ** 20
N_CHIPS = 4
N_DEVICES = 8

ADAM_LR = 0.001
ADAM_B1 = 0.9
ADAM_B2 = 0.999
ADAM_EPS = 1e-08
ADAM_WD = 0.01
ADAM_STEP = 10

MESH = pl.DeviceIdType.MESH
HBM_SPEC = pl.BlockSpec(memory_space=pltpu.HBM)

NN = (((1,), (0,)), ((), ()))
NT = (((1,), (1,)), ((), ()))
TN = (((0,), (0,)), ((), ()))


def _params(sem=None):
    return pltpu.CompilerParams(dimension_semantics=sem, vmem_limit_bytes=VMEM_LIMIT_BYTES)


def _pick(dim, pref):
    for t in (1024, 768, 512, 384, 256, 128, 64, 32, 16, 8):
        if t <= pref and dim % t == 0:
            return t
    return dim


def _dot(a, b, dims=NN):
    return lax.dot_general(a, b, dims, preferred_element_type=F32)


def _sigmoid(x):
    e = jnp.exp(-jnp.abs(x))
    return jnp.where(x >= 0, 1.0, e) / (1.0 + e)


def _matmul(a, b, *, mode, grid, a_block, a_map, b_block, b_map, o_block, o_map, out_shape, out_dtype, name,
            a_fn=None, epi_fn=None, epi_args=(), epi_row_args=(), col_sums=False, exchange=None):
    nk = grid[2]
    dims = {"nn": NN, "nt": NT, "tn": TN}[mode]
    n_epi = len(epi_args) + len(epi_row_args)
    n_out = 2 if col_sums else 1
    tn = o_block[-1]
    assert not col_sums or grid[1] == 1, "the column sums stay resident only with one tile along N"

    def body(a_ref, b_ref, *rest):
        epi_refs = rest[:n_epi]
        o_ref = rest[n_epi]
        kk = pl.program_id(2)

        def emit(r):
            if epi_fn is not None:
                r = epi_fn(r, *[e[...] for e in epi_refs])
            if col_sums:
                r, row = r
                sums_ref = rest[n_epi + 1]
                first = pl.program_id(0) == 0

                @pl.when(first)
                def _():
                    sums_ref[...] = row

                @pl.when(jnp.logical_not(first))
                def _():
                    sums_ref[...] += row
            o_ref[...] = r.astype(o_ref.dtype)

        av = a_ref[...]
        if a_fn is not None:
            av = a_fn(av)
        part = _dot(av.astype(BF16), b_ref[...].astype(BF16), dims)
        if nk == 1:
            emit(part)
            return
        acc_ref = rest[n_epi + n_out]

        @pl.when(kk == 0)
        def _():
            acc_ref[...] = part

        @pl.when(kk > 0)
        def _():
            acc_ref[...] += part

        @pl.when(kk == nk - 1)
        def _():
            emit(acc_ref[...])

    acc_shape = tuple(d for d in o_block if d is not None)
    row_spec = pl.BlockSpec((1, tn), lambda i, j, kk: (0, j))
    in_specs = [pl.BlockSpec(a_block, a_map), pl.BlockSpec(b_block, b_map)]
    in_specs += [pl.BlockSpec(o_block, o_map) for _ in epi_args] + [row_spec for _ in epi_row_args]
    out_specs, out_shapes = [pl.BlockSpec(o_block, o_map)], [jax.ShapeDtypeStruct(out_shape, out_dtype)]
    if col_sums:
        out_specs, out_shapes = out_specs + [row_spec], out_shapes + [jax.ShapeDtypeStruct((1, out_shape[-1]), F32)]
    outs, moved = _call_with_exchange(
        body, exchange, grid=grid, in_specs=in_specs, out_specs=out_specs, out_shape=out_shapes,
        scratch_shapes=[pltpu.VMEM(acc_shape, F32)] if nk > 1 else [], name=name,
        args=(a, b, *epi_args, *epi_row_args), sequential=col_sums)
    result = tuple(outs) if col_sums else outs[0]
    return result if exchange is None else (result, moved)


def _relu2(u):
    r = jnp.maximum(u, 0.0)
    return r * r


def _add(r, res):
    return r + res


def _mm_fwd_cols(a, wg, *, name):
    s, k = a.shape
    ncs = wg.shape[2]
    tm, tk, tn = _pick(s, 1024), _pick(k, 1024), _pick(ncs, 1024)
    npb = ncs // tn
    return _matmul(a, wg, mode="nn", grid=(s // tm, N_CHIPS * npb, k // tk),
                   a_block=(tm, tk), a_map=lambda i, j, kk: (i, kk),
                   b_block=(None, tk, tn), b_map=lambda i, j, kk: (j // npb, kk, j % npb),
                   o_block=(tm, tn), o_map=lambda i, j, kk: (i, j),
                   out_shape=(s, N_CHIPS * ncs), out_dtype=F32, name=name)


def _rows_joined(wg):
    assert wg.shape[1] % 16 == 0, "joining the leading axes must not cross a tile of 16 rows"
    return wg.reshape(wg.shape[0] * wg.shape[1], wg.shape[2])


def _mm_fwd_rows(a, wg, *, residual, name, a_fn=None):
    s = a.shape[0]
    w = _rows_joined(wg)
    k, n = w.shape
    tm, tk, tn = _pick(s, 1024), _pick(k, 1024), _pick(n, 1024)
    return _matmul(a, w, mode="nn", grid=(s // tm, n // tn, k // tk),
                   a_block=(tm, tk), a_map=lambda i, j, kk: (i, kk),
                   b_block=(tk, tn), b_map=lambda i, j, kk: (kk, j),
                   o_block=(tm, tn), o_map=lambda i, j, kk: (i, j),
                   out_shape=(s, n), out_dtype=F32, name=name, a_fn=a_fn, epi_fn=_add, epi_args=(residual,))


def _rmsnorm_grad(dh, x, dx_res, gain):
    r = lax.rsqrt(jnp.mean(x * x, axis=-1, keepdims=True) + NORM_EPS)
    xhat = x * r
    dxhat = dh * gain
    dx = r * (dxhat - xhat * jnp.mean(dxhat * xhat, axis=-1, keepdims=True))
    return dx_res + dx, jnp.sum(dh * xhat, axis=0, keepdims=True)


def _mm_bwd_cols_norm(dy, wg, x, gain_row, dx_res, *, name, exchange=None):
    by_slot = dy.ndim == 3
    s = dy.shape[1] if by_slot else dy.shape[0]
    kw, ncs = wg.shape[1], wg.shape[2]
    tm = _pick(s, 512)
    halves = 2 if tm % 512 == 0 else 1
    sub = tm // halves

    def body(dy_ref, w_ref, x_ref, res_ref, g_ref, dx_ref, dg_ref):
        total = None
        for r in range(halves):
            rows = slice(r * sub, (r + 1) * sub)
            acc = None
            for q in range(N_CHIPS):
                a = dy_ref[q, rows, :] if by_slot else dy_ref[rows, q * ncs:(q + 1) * ncs]
                part = _dot(a.astype(BF16), w_ref[q], NT)
                acc = part if acc is None else acc + part
            dx, row = _rmsnorm_grad(acc, x_ref[rows, :], res_ref[rows, :], g_ref[...])
            dx_ref[rows, :] = dx
            total = row if total is None else total + row
        first = pl.program_id(0) == 0

        @pl.when(first)
        def _():
            dg_ref[...] = total

        @pl.when(jnp.logical_not(first))
        def _():
            dg_ref[...] += total

    dy_spec = pl.BlockSpec((N_CHIPS, tm, ncs), lambda i: (0, i, 0)) if by_slot else pl.BlockSpec(
        (tm, N_CHIPS * ncs), lambda i: (i, 0))
    tok = pl.BlockSpec((tm, kw), lambda i: (i, 0))
    row_spec = pl.BlockSpec((1, kw), lambda i: (0, 0))
    outs, moved = _call_with_exchange(
        body, exchange, grid=(s // tm,),
        in_specs=[dy_spec, pl.BlockSpec((N_CHIPS, kw, ncs), lambda i: (0, 0, 0)), tok, tok, row_spec],
        out_specs=[tok, row_spec],
        out_shape=[jax.ShapeDtypeStruct((s, kw), F32), jax.ShapeDtypeStruct((1, kw), F32)],
        scratch_shapes=[], name=name, args=(dy, wg, x, dx_res, gain_row), sequential=True)
    return tuple(outs) if exchange is None else (tuple(outs), moved)


def _mm_bwd_rows(dy, wg, *, name, out_dtype=F32, epi_fn=None, epi_args=()):
    s, n = dy.shape
    w = _rows_joined(wg)
    rows = w.shape[0]
    tm, tn, tk = _pick(s, 1024), _pick(rows, 1024), _pick(n, 1024)
    return _matmul(dy, w, mode="nt", grid=(s // tm, rows // tn, n // tk),
                   a_block=(tm, tk), a_map=lambda i, j, kk: (i, kk),
                   b_block=(tn, tk), b_map=lambda i, j, kk: (j, kk),
                   o_block=(tm, tn), o_map=lambda i, j, kk: (i, j),
                   out_shape=(s, rows), out_dtype=out_dtype, name=name, epi_fn=epi_fn, epi_args=epi_args)


def _mm_dw_cols(xa, dy, *, name):
    s, kx = xa.shape
    by_slot = dy.ndim == 3
    ncs = dy.shape[2] if by_slot else dy.shape[1] // N_CHIPS
    tm, tn, tk = _pick(kx, 1024), _pick(ncs, 1024), _pick(s, 1024)
    npb = ncs // tn
    b_block, b_map = ((None, tk, tn), lambda i, j, kk: (j // npb, kk, j % npb)) if by_slot else (
        (tk, tn), lambda i, j, kk: (kk, j))
    return _matmul(xa, dy, mode="tn", grid=(kx // tm, N_CHIPS * npb, s // tk),
                   a_block=(tk, tm), a_map=lambda i, j, kk: (kk, i),
                   b_block=b_block, b_map=b_map,
                   o_block=(None, tm, tn), o_map=lambda i, j, kk: (j // npb, i, j % npb),
                   out_shape=(N_CHIPS, kx, ncs), out_dtype=GRAD_SLOT_DTYPE, name=name)


def _mm_dw_rows(xa, dy, *, name, a_fn=None):
    s, n = dy.shape
    rows = xa.shape[1]
    assert (rows // N_CHIPS) % 16 == 0, "splitting the rows into slots must not cut a tile of 16 rows"
    tm, tn, tk = _pick(rows, 1024), _pick(n, 1024), _pick(s, 1024)
    dw = _matmul(xa, dy, mode="tn", grid=(rows // tm, n // tn, s // tk),
                 a_block=(tk, tm), a_map=lambda i, j, kk: (kk, i),
                 b_block=(tk, tn), b_map=lambda i, j, kk: (kk, j),
                 o_block=(tm, tn), o_map=lambda i, j, kk: (i, j),
                 out_shape=(rows, n), out_dtype=GRAD_SLOT_DTYPE, name=name, a_fn=a_fn)
    return dw.reshape(N_CHIPS, rows // N_CHIPS, n)


def _rmsnorm_fwd(x, gain_row):
    s, d = x.shape
    ts = _pick(s, 1024)

    def body(x_ref, g_ref, h_ref):
        xv = x_ref[...]
        r = lax.rsqrt(jnp.mean(xv * xv, axis=-1, keepdims=True) + NORM_EPS)
        h_ref[...] = (xv * r * g_ref[...]).astype(h_ref.dtype)

    return pl.pallas_call(
        body, grid=(s // ts,),
        in_specs=[pl.BlockSpec((ts, d), lambda i: (i, 0)), pl.BlockSpec((1, d), lambda i: (0, 0))],
        out_specs=pl.BlockSpec((ts, d), lambda i: (i, 0)),
        out_shape=jax.ShapeDtypeStruct((s, d), BF16),
        compiler_params=_params(("parallel",)), name="rmsnorm_fwd",
    )(x, gain_row)


def _loss_head(y, target):
    s, d = y.shape
    ts = _pick(s, 512)

    def body(y_ref, t_ref, sq_ref, dy_ref):
        i = pl.program_id(0)
        err = y_ref[...] - t_ref[...]
        dy_ref[...] = err / d
        part = jnp.sum(err * err, axis=0, keepdims=True)

        @pl.when(i == 0)
        def _():
            sq_ref[...] = part

        @pl.when(i > 0)
        def _():
            sq_ref[...] += part

    return pl.pallas_call(
        body, grid=(s // ts,),
        in_specs=[pl.BlockSpec((ts, d), lambda i: (i, 0)), pl.BlockSpec((ts, d), lambda i: (i, 0))],
        out_specs=[pl.BlockSpec((1, d), lambda i: (0, 0)), pl.BlockSpec((ts, d), lambda i: (i, 0))],
        out_shape=[jax.ShapeDtypeStruct((1, d), F32), jax.ShapeDtypeStruct((s, d), F32)],
        compiler_params=_params(("arbitrary",)), name="loss_head",
    )(y, target)


def _pair_ones():
    lane = np.arange(LANES)
    same_half = (lane[:, None] // SB_HEAD_DIM == lane[None, :] // SB_HEAD_DIM).astype(np.float32)
    return jnp.asarray(np.concatenate([same_half, same_half], axis=0), BF16)


def _pair_mean(val, pair_ones):
    return _dot(jnp.concatenate(_split2(val), axis=1), pair_ones) * (1.0 / SB_HEAD_DIM)


def _pair_mean_lanes(val, low_half):
    s0 = jnp.sum(jnp.where(low_half, val, 0.0), axis=-1, keepdims=True)
    s1 = jnp.sum(jnp.where(low_half, 0.0, val), axis=-1, keepdims=True)
    return jnp.where(low_half, s0, s1) * (1.0 / SB_HEAD_DIM)


def _qk_norm_fwd(qkv, qgain_row, kgain_row):
    s, d3 = qkv.shape
    d = d3 // 3
    ts = _pick(s, 512)
    groups = d // LANES

    def body(q_ref, k_ref, v_ref, qg_ref, kg_ref, ones_ref, qn_ref, kn_ref, vb_ref):
        for src, gain, dst in ((q_ref, qg_ref, qn_ref), (k_ref, kg_ref, kn_ref)):
            for p in range(groups):
                cols = slice(p * LANES, (p + 1) * LANES)
                xp = src[:, cols]
                r = lax.rsqrt(_pair_mean(xp * xp, ones_ref[...]) + NORM_EPS)
                dst[:, cols] = (xp * r * gain[:, cols]).astype(dst.dtype)
        vb_ref[...] = v_ref[...].astype(vb_ref.dtype)

    tok = lambda c: pl.BlockSpec((ts, d), lambda i: (i, c))
    row = pl.BlockSpec((1, d), lambda i: (0, 0))
    return pl.pallas_call(
        body, grid=(s // ts,),
        in_specs=[tok(0), tok(1), tok(2), row, row, pl.BlockSpec((2 * LANES, LANES), lambda i: (0, 0))],
        out_specs=[tok(0), tok(0), tok(0)],
        out_shape=[jax.ShapeDtypeStruct((s, d), BF16)] * 3,
        compiler_params=_params(("parallel",)), name="qk_norm_fwd",
    )(qkv, qkv, qkv, qgain_row, kgain_row, _pair_ones())


def _qk_norm_bwd(qkv, qgain_row, kgain_row, dqn, dkn, dv):
    s, d3 = qkv.shape
    d = d3 // 3
    ts = _pick(s, 512)
    groups = d // LANES

    def body(q_ref, k_ref, qg_ref, kg_ref, dqn_ref, dkn_ref, dv_ref, dqkv_ref, dqg_ref, dkg_ref):
        i = pl.program_id(0)
        low_half = lax.broadcasted_iota(jnp.int32, (ts, LANES), 1) < SB_HEAD_DIM
        for which, (src, gain, dsrc, dgain) in enumerate(((q_ref, qg_ref, dqn_ref, dqg_ref),
                                                          (k_ref, kg_ref, dkn_ref, dkg_ref))):
            for p in range(groups):
                cols = slice(p * LANES, (p + 1) * LANES)
                xp = src[:, cols]
                r = lax.rsqrt(_pair_mean_lanes(xp * xp, low_half) + NORM_EPS)
                xhat = xp * r
                dy = dsrc[:, cols]
                dxhat = dy * gain[:, cols]
                dx = r * (dxhat - xhat * _pair_mean_lanes(dxhat * xhat, low_half))
                dqkv_ref[:, which * d + p * LANES: which * d + (p + 1) * LANES] = dx.astype(dqkv_ref.dtype)
                part = jnp.sum(dy * xhat, axis=0, keepdims=True)

                @pl.when(i == 0)
                def _():
                    dgain[:, cols] = part

                @pl.when(i > 0)
                def _():
                    dgain[:, cols] += part
        dqkv_ref[:, 2 * d:] = dv_ref[...].astype(dqkv_ref.dtype)

    tok = lambda c: pl.BlockSpec((ts, d), lambda i: (i, c))
    row = pl.BlockSpec((1, d), lambda i: (0, 0))
    return pl.pallas_call(
        body, grid=(s // ts,),
        in_specs=[tok(0), tok(1), row, row, tok(0), tok(0), tok(0)],
        out_specs=[pl.BlockSpec((ts, d3), lambda i: (i, 0)), row, row],
        out_shape=[jax.ShapeDtypeStruct((s, d3), BF16), jax.ShapeDtypeStruct((1, d), F32),
                   jax.ShapeDtypeStruct((1, d), F32)],
        compiler_params=_params(("arbitrary",)), name="qk_norm_bwd",
    )(qkv, qkv, qgain_row, kgain_row, dqn, dkn, dv)


def _split2(x):
    hi = x.astype(BF16)
    lo = (x - hi.astype(F32)).astype(BF16)
    return hi, lo


SB_TK = 128


def _sb_consts(tk):
    j = np.arange(tk)
    ones = np.ones((tk, tk), np.float32)
    out = []
    for tri in ((j[:, None] >= j[None, :]), (j[:, None] <= j[None, :])):
        half = np.concatenate([tri.astype(np.float32), ones], axis=1)
        out.append(jnp.asarray(np.concatenate([half, half], axis=0), BF16))
    return out


def _head_stack(blk, low_half):
    f = blk.astype(F32)
    return jnp.concatenate([jnp.where(low_half, f, 0.0), jnp.where(low_half, 0.0, f)], axis=0).astype(BF16)


def _sb_tile_sums(z, valid, tri2):
    e = jnp.exp(-jnp.abs(z))
    lstay = jnp.minimum(-z, 0.0) - jnp.log(1.0 + e)
    if valid is not None:
        lstay = jnp.where(valid, lstay, 0.0)
    hi, lo = _split2(lstay)
    return e, _dot(jnp.concatenate([hi, lo], axis=1), tri2)


def _sb_weights(z, c2, valid, run):
    w = jnp.exp(z + c2[:, :SB_TK] + run)
    return w if valid is None else jnp.where(valid, w, 0.0)


EXP_IS_ZERO_BELOW = -110.0


def _max_row_norm(x):
    f = x.astype(F32)
    return jnp.sqrt(jnp.max(jnp.sum(f * f, axis=-1, keepdims=True)))


def _sb_score_bound(qs, kmax_ref):
    return _max_row_norm(qs) * jnp.max(kmax_ref[...]) * 1.01 + 1.0


def _sb_rest_is_zero(run_ref, bound):
    return jnp.max(jnp.maximum(run_ref[0], run_ref[1])) + bound < EXP_IS_ZERO_BELOW


def _sb_attn_fwd(qn, kn, vb, exchange=None):
    s, d = qn.shape
    tk = SB_TK
    tq = _pick(s, 256)
    nq, ndiag = s // tq, tq // tk
    assert tq % (2 * tk) == 0, "tiles below the diagonal are taken two at a time"
    npairs = d // LANES
    scale = 1.0 / math.sqrt(SB_HEAD_DIM)
    tri_ge2, _ = _sb_consts(tk)

    def body(q_ref, k_ref, v_ref, tri_ref, o_ref, acc_ref, run_ref, kmax_ref):
        qi = pl.program_id(1)

        @pl.when(qi == 0)
        def _():
            kmax_ref[...] = jnp.full(kmax_ref.shape, _max_row_norm(k_ref[...]), F32)

        low_half = lax.broadcasted_iota(jnp.int32, (tk, LANES), 1) < SB_HEAD_DIM
        row = lax.broadcasted_iota(jnp.int32, (tq, tk), 0)
        col = lax.broadcasted_iota(jnp.int32, (tq, tk), 1)
        qs = (q_ref[...].astype(F32) * scale).astype(BF16)
        bound = _sb_score_bound(qs, kmax_ref)
        acc_ref[...] = jnp.zeros_like(acc_ref)
        run_ref[...] = jnp.zeros_like(run_ref)
        n_full = qi * ndiag

        def sums(kb, dd):
            koff = pl.multiple_of(kb * tk, tk)
            kcat = _head_stack(k_ref[pl.ds(koff, tk), :], low_half)
            vcat = _head_stack(v_ref[pl.ds(koff, tk), :], low_half)
            z2 = _dot(qs, kcat, NT)
            valid = None if dd is None else row > col + dd * tk
            zs = [z2[:, h * tk:(h + 1) * tk] for h in range(2)]
            return zs, [_sb_tile_sums(z, valid, tri_ref[...])[1] for z in zs], valid, vcat

        def finish(zs, c2s, valid, vcat):
            ws = []
            for h in range(2):
                ws.append(_sb_weights(zs[h], c2s[h], valid, run_ref[h]).astype(BF16))
                run_ref[h] += c2s[h][:, tk:]
            acc_ref[...] += _dot(jnp.concatenate(ws, axis=1), vcat)

        def first_tiles(below):
            pres = [sums(n_full + dd, dd) for dd in reversed(range(ndiag))]
            pres += [sums(n_full - 1 - n, None) for n in range(below)]
            for pre in pres:
                finish(*pre)

        @pl.when(qi == 0)
        def _():
            first_tiles(0)

        @pl.when(qi > 0)
        def _():
            first_tiles(2)

        def two_tiles(carry):
            it, _ = carry
            kb = n_full - 1 - 2 * it
            first, second = sums(kb, None), sums(kb - 1, None)
            finish(*first)
            finish(*second)
            return it + 1, _sb_rest_is_zero(run_ref, bound)

        lax.while_loop(lambda c: jnp.logical_and(c[0] < n_full // 2, jnp.logical_not(c[1])), two_tiles,
                       (jnp.minimum(qi, 1), _sb_rest_is_zero(run_ref, bound)))
        o_ref[...] = acc_ref[...]

    blk = pl.BlockSpec((tq, LANES), lambda p, i: (i, p))
    full = pl.BlockSpec((s, LANES), lambda p, i: (0, p))
    (o,), moved = _call_with_exchange(
        body, exchange, grid=(npairs, nq),
        in_specs=[blk, full, full, pl.BlockSpec((2 * tk, 2 * tk), lambda p, i: (0, 0))],
        out_specs=[blk], out_shape=[jax.ShapeDtypeStruct((s, d), F32)],
        scratch_shapes=[pltpu.VMEM((tq, LANES), F32), pltpu.VMEM((2, tq, tk), F32), pltpu.VMEM((8, LANES), F32)],
        name="sb_attn_fwd", args=(qn, kn, vb, tri_ge2))
    return o, moved


def _sb_attn_bwd(qn, kn, vb, do, exchange=None):
    s, d = qn.shape
    tk = SB_TK
    tq = _pick(s, 256)
    nq, ndiag = s // tq, tq // tk
    assert tq % (2 * tk) == 0, "tiles below the diagonal are taken two at a time"
    npairs = d // LANES
    scale = 1.0 / math.sqrt(SB_HEAD_DIM)
    tri_ge2, tri_le2 = _sb_consts(tk)

    def body(q_ref, k_ref, v_ref, do_ref, tge_ref, tle_ref, dq_ref, dk_ref, dv_ref,
             g_cache, s_cache, run_ref, dq_acc, kmax_ref):
        qi = pl.program_id(1)

        @pl.when(qi == 0)
        def _():
            dk_ref[...] = jnp.zeros_like(dk_ref)
            dv_ref[...] = jnp.zeros_like(dv_ref)
            kmax_ref[...] = jnp.full(kmax_ref.shape, _max_row_norm(k_ref[...]), F32)

        low_half = lax.broadcasted_iota(jnp.int32, (tk, LANES), 1) < SB_HEAD_DIM
        row = lax.broadcasted_iota(jnp.int32, (tq, tk), 0)
        col = lax.broadcasted_iota(jnp.int32, (tq, tk), 1)
        qs = (q_ref[...].astype(F32) * scale).astype(BF16)
        bound = _sb_score_bound(qs, kmax_ref)
        dob = do_ref[...].astype(BF16)
        n_full = qi * ndiag

        def a_sums(kb, dd):
            koff = pl.multiple_of(kb * tk, tk)
            kcat = _head_stack(k_ref[pl.ds(koff, tk), :], low_half)
            vcat = _head_stack(v_ref[pl.ds(koff, tk), :], low_half)
            z2 = _dot(qs, kcat, NT)
            dw2 = _dot(dob, vcat, NT)
            valid = None if dd is None else row > col + dd * tk
            c2s = []
            for h in range(2):
                cols = slice(h * tk, (h + 1) * tk)
                z = z2[:, cols]
                e, c2 = _sb_tile_sums(z, valid, tge_ref[...])
                s_cache[kb, :, cols] = jnp.where(z >= 0, 1.0, e) / (1.0 + e)
                c2s.append(c2)
            return kb, koff, z2, dw2, c2s, valid

        def a_finish(kb, koff, z2, dw2, c2s, valid):
            ws = []
            for h in range(2):
                cols = slice(h * tk, (h + 1) * tk)
                w = _sb_weights(z2[:, cols], c2s[h], valid, run_ref[h])
                run_ref[h] += c2s[h][:, tk:]
                g_cache[kb, :, cols] = w * dw2[:, cols]
                ws.append(w.astype(BF16))
            dv2 = _dot(jnp.concatenate(ws, axis=1), dob, TN)
            dv_ref[pl.ds(koff, tk), :] += jnp.where(low_half, dv2[:tk], dv2[tk:])

        def b_sums(kb, dd):
            gs = [g_cache[kb, :, h * tk:(h + 1) * tk] for h in range(2)]
            p2s = [_dot(jnp.concatenate(_split2(g), axis=1), tle_ref[...]) for g in gs]
            return kb, gs, p2s, (None if dd is None else row > col + dd * tk)

        def b_finish(kb, gs, p2s, valid):
            koff = pl.multiple_of(kb * tk, tk)
            dzs = []
            for h in range(2):
                dz = gs[h] - s_cache[kb, :, h * tk:(h + 1) * tk] * (p2s[h][:, :tk] + run_ref[h])
                if valid is not None:
                    dz = jnp.where(valid, dz, 0.0)
                run_ref[h] += p2s[h][:, tk:]
                dzs.append(dz.astype(BF16))
            dzcat = jnp.concatenate(dzs, axis=1)
            dq_acc[...] += _dot(dzcat, _head_stack(k_ref[pl.ds(koff, tk), :], low_half))
            dk2 = _dot(dzcat, qs, TN)
            dk_ref[pl.ds(koff, tk), :] += jnp.where(low_half, dk2[:tk], dk2[tk:])

        run_ref[...] = jnp.zeros_like(run_ref)
        near = jnp.minimum(qi, 1)

        def a_first_tiles(below):
            pres = [a_sums(n_full + dd, dd) for dd in reversed(range(ndiag))]
            pres += [a_sums(n_full - 1 - n, None) for n in range(below)]
            for pre in pres:
                a_finish(*pre)

        @pl.when(qi == 0)
        def _():
            a_first_tiles(0)

        @pl.when(qi > 0)
        def _():
            a_first_tiles(2)

        def two_a(carry):
            it, _ = carry
            kb = n_full - 1 - 2 * it
            first, second = a_sums(kb, None), a_sums(kb - 1, None)
            a_finish(*first)
            a_finish(*second)
            return it + 1, _sb_rest_is_zero(run_ref, bound)

        trips, _ = lax.while_loop(lambda c: jnp.logical_and(c[0] < n_full // 2, jnp.logical_not(c[1])), two_a,
                                  (near, _sb_rest_is_zero(run_ref, bound)))

        run_ref[...] = jnp.zeros_like(run_ref)
        dq_acc[...] = jnp.zeros_like(dq_acc)
        kb_first = n_full - 2 * trips

        def two_b(it, carry):
            first, second = b_sums(kb_first + 2 * it, None), b_sums(kb_first + 2 * it + 1, None)
            b_finish(*first)
            b_finish(*second)
            return carry

        lax.fori_loop(0, trips - near, two_b, 0)

        def b_last_tiles(below):
            pres = [b_sums(n_full - below + n, None) for n in range(below)]
            pres += [b_sums(n_full + dd, dd) for dd in range(ndiag)]
            for pre in pres:
                b_finish(*pre)

        @pl.when(qi == 0)
        def _():
            b_last_tiles(0)

        @pl.when(qi > 0)
        def _():
            b_last_tiles(2)

        dq_ref[...] = dq_acc[...] * scale

    blk = pl.BlockSpec((tq, LANES), lambda p, i: (i, p))
    full = pl.BlockSpec((s, LANES), lambda p, i: (0, p))
    tri = pl.BlockSpec((2 * tk, 2 * tk), lambda p, i: (0, 0))
    return _call_with_exchange(
        body, exchange, grid=(npairs, nq),
        in_specs=[blk, full, full, blk, tri, tri],
        out_specs=[blk, full, full],
        out_shape=[jax.ShapeDtypeStruct((s, d), F32)] * 3,
        scratch_shapes=[pltpu.VMEM((s // tk, tq, 2 * tk), F32), pltpu.VMEM((s // tk, tq, 2 * tk), F32),
                        pltpu.VMEM((2, tq, tk), F32), pltpu.VMEM((tq, LANES), F32), pltpu.VMEM((8, LANES), F32)],
        name="sb_attn_bwd", args=(qn, kn, vb, do, tri_ge2, tri_le2))


def _hg_consts(c):
    levels = []
    h = c // 2
    while h >= 1:
        levels.append(h)
        h //= 2
    t = np.arange(c)
    j = t[None, :]
    rows, masks = [], []
    for h in levels:
        blk = t // (2 * h)
        mid = blk * 2 * h + h - 1
        second = (t % (2 * h)) >= h
        rows.append(second[:, None] & (j > mid[:, None]) & (j <= t[:, None]))
        rows.append((~second)[:, None] & (j > t[:, None]) & (j <= mid[:, None]))
        masks.append((blk[:, None] == blk[None, :]) & second[:, None] & (~second)[None, :])
    rows.append(j <= t[:, None])
    rows.append(j > t[:, None])
    masks.append(t[:, None] == t[None, :])
    m_all = np.concatenate(rows, axis=0).astype(np.float32)
    mask_all = np.stack(masks, axis=0).astype(np.float32)
    suffix = (t[None, :] >= t[:, None]).astype(np.float32)
    return len(levels), jnp.asarray(m_all, BF16), jnp.asarray(mask_all, F32), jnp.asarray(suffix, BF16)


def _split3(x):
    hi = x.astype(BF16)
    r1 = x - hi.astype(F32)
    mid = r1.astype(BF16)
    lo = (r1 - mid.astype(F32)).astype(BF16)
    return jnp.concatenate([hi, mid, lo], axis=1)


def _join3(e):
    n = e.shape[1] // 3
    return e[:, :n] + e[:, n:2 * n] + e[:, 2 * n:]


def _hg_gates(qr, fr, lb):
    sq = _sigmoid(qr)
    sf = _sigmoid(fr)
    forget = lb + (1.0 - lb) * sf
    return qr * sq, sq, sf, forget, jnp.log(forget), 1.0 - forget


def _hg_scores(q, k, expo, masks, nlev, c):
    qb, kb = q.astype(BF16), k.astype(BF16)
    a = masks[nlev] * _dot(qb, kb, NT)
    scaled = []
    for li in range(nlev):
        fq = jnp.exp(expo[(2 * li) * c:(2 * li + 1) * c])
        fk = jnp.exp(expo[(2 * li + 1) * c:(2 * li + 2) * c])
        qs, ks = (q * fq).astype(BF16), (k * fk).astype(BF16)
        a = a + masks[li] * _dot(qs, ks, NT)
        scaled.append((qs, ks, fq, fk))
    return a, scaled, qb, kb


def _hg_heads_per_step(nh):
    return 2 if nh % 2 == 0 else 1


def _hg_fwd(proj, lb_row, gain_row, exchange=None):
    s, d4 = proj.shape
    d = d4 // 4
    nh = d // HG_HEAD_DIM
    c = min(HG_CHUNK, s)
    tb = _pick(s, 512)
    ncb = tb // c
    nlev, m_all, mask_all, _ = _hg_consts(c)
    nrow = m_all.shape[0]

    hp = _hg_heads_per_step(nh)
    wide = hp * HG_HEAD_DIM

    def body(q_ref, f_ref, i_ref, g_ref, lb_ref, gain_ref, mall_ref, mask_ref, y_ref, o_ref, st_out_ref, st_ref):
        b = pl.program_id(1)

        @pl.when(b == 0)
        def _():
            st_ref[...] = jnp.zeros_like(st_ref)

        gain = gain_ref[...]

        def inside(ci, hh):
            rows = pl.ds(pl.multiple_of(ci * c, c), c)
            cols = slice(hh * HG_HEAD_DIM, (hh + 1) * HG_HEAD_DIM)
            q, _, _, _, lf, k = _hg_gates(q_ref[rows, cols], f_ref[rows, cols], lb_ref[:, cols])
            v = i_ref[rows, cols].astype(BF16)
            expo = _join3(_dot(mall_ref[...], _split3(lf)))
            a, _, _, _ = _hg_scores(q, k, expo, mask_ref[...], nlev, c)
            b_cum = expo[2 * nlev * c:(2 * nlev + 1) * c]
            e_tail = expo[(2 * nlev + 1) * c:(2 * nlev + 2) * c]
            q_in = (q * jnp.exp(b_cum)).astype(BF16)
            k_dec = (k * jnp.exp(e_tail)).astype(BF16)
            return ci, hh, rows, cols, q_in, _dot(a.astype(BF16), v), jnp.exp(b_cum[c - 1:c, :]), _dot(v, k_dec, TN)

        def across(ci, hh, rows, cols, q_in, o_intra, decay, kv):
            st = st_ref[hh]
            st_out_ref[ci, hh] = st
            o = _dot(q_in, st.astype(BF16), NT) + o_intra
            st_ref[hh] = st * decay + kv
            o_ref[rows, cols] = o
            r = lax.rsqrt(jnp.mean(o * o, axis=-1, keepdims=True) + NORM_EPS)
            y_ref[rows, cols] = (o * r * gain * _sigmoid(g_ref[rows, cols])).astype(y_ref.dtype)

        per_trip = 2 if ncb % 2 == 0 else 1

        def trip(it, carry):
            ready = [inside(per_trip * it + n, hh) for n in range(per_trip) for hh in range(hp)]
            for r in ready:
                across(*r)
            return carry

        lax.fori_loop(0, ncb // per_trip, trip, 0)

    part = lambda k: pl.BlockSpec((tb, wide), lambda h, b: (b, k * (nh // hp) + h))
    head_row = pl.BlockSpec((1, wide), lambda h, b: (0, h))
    tok = pl.BlockSpec((tb, wide), lambda h, b: (b, h))
    return _call_with_exchange(
        body, exchange, grid=(nh // hp, s // tb),
        in_specs=[part(0), part(1), part(2), part(3), head_row,
                  pl.BlockSpec((1, HG_HEAD_DIM), lambda h, b: (0, 0)),
                  pl.BlockSpec((nrow, c), lambda h, b: (0, 0)),
                  pl.BlockSpec((nlev + 1, c, c), lambda h, b: (0, 0, 0))],
        out_specs=[tok, tok, pl.BlockSpec((ncb, hp, HG_HEAD_DIM, HG_HEAD_DIM), lambda h, b: (b, h, 0, 0))],
        out_shape=[jax.ShapeDtypeStruct((s, d), BF16), jax.ShapeDtypeStruct((s, d), F32),
                   jax.ShapeDtypeStruct((s // c, nh, HG_HEAD_DIM, HG_HEAD_DIM), F32)],
        scratch_shapes=[pltpu.VMEM((hp, HG_HEAD_DIM, HG_HEAD_DIM), F32)],
        name="hg_fwd", args=(proj, proj, proj, proj, lb_row, gain_row, m_all, mask_all))


def _hg_bwd(proj, lb_row, gain_row, o_saved, states, dy, exchange=None):
    s, d4 = proj.shape
    d = d4 // 4
    nh = d // HG_HEAD_DIM
    c = min(HG_CHUNK, s)
    tb = _pick(s, 512)
    ncb = tb // c
    nb = s // tb
    nlev, m_all, mask_all, suffix = _hg_consts(c)
    nrow = m_all.shape[0]
    hp = _hg_heads_per_step(nh)
    wide = hp * HG_HEAD_DIM

    def body(q_ref, f_ref, i_ref, g_ref, lb_ref, gain_ref, o_ref, st_in_ref, dy_ref, mall_ref, mask_ref, suf_ref,
             dproj_ref, dlb_ref, dgain_ref, dst_ref, run_ref):
        b = pl.program_id(1)

        @pl.when(b == 0)
        def _():
            dst_ref[...] = jnp.zeros_like(dst_ref)
            run_ref[...] = jnp.zeros_like(run_ref)
            dlb_ref[...] = jnp.zeros_like(dlb_ref)
            dgain_ref[...] = jnp.zeros_like(dgain_ref)

        gain = gain_ref[...]

        def head_chunk(ci, rows, hh, cols):
            lb = lb_ref[:, cols]
            qr, fr = q_ref[rows, cols], f_ref[rows, cols]
            q, sq, sf, forget, lf, k = _hg_gates(qr, fr, lb)
            v = i_ref[rows, cols].astype(BF16)
            expo = _join3(_dot(mall_ref[...], _split3(lf)))
            masks = mask_ref[...]
            o = o_ref[rows, cols]
            dyv = dy_ref[rows, cols]
            sg = _sigmoid(g_ref[rows, cols])
            r = lax.rsqrt(jnp.mean(o * o, axis=-1, keepdims=True) + NORM_EPS)
            ohat = o * r
            dyn = dyv * sg
            dproj_ref[3, rows, cols] = (dyv * ohat * gain * sg * (1.0 - sg)).astype(dproj_ref.dtype)
            dgain_ref[:, cols] += jnp.sum(dyn * ohat, axis=0, keepdims=True)
            dohat = dyn * gain
            do = (r * (dohat - ohat * jnp.mean(dohat * ohat, axis=-1, keepdims=True))).astype(BF16)
            dst = dst_ref[hh]
            dstb = dst.astype(BF16)
            qb, kb = q.astype(BF16), k.astype(BF16)
            f_cum = jnp.exp(expo[2 * nlev * c:(2 * nlev + 1) * c])
            f_tail = jnp.exp(expo[(2 * nlev + 1) * c:(2 * nlev + 2) * c])
            q_in = (q * f_cum).astype(BF16)
            k_dec = (k * f_tail).astype(BF16)
            t_in = _join3(_dot(do, _split3(st_in_ref[ci, hh])))
            t_st = _join3(_dot(v, _split3(dst)))
            da = _dot(do, v, NT)
            dam = (masks[nlev] * da).astype(BF16)
            a = masks[nlev] * _dot(qb, kb, NT)
            dq = t_in * f_cum + _dot(dam, kb)
            dk = t_st * f_tail + _dot(dam, qb, TN)
            db = q_in.astype(F32) * t_in - k_dec.astype(F32) * t_st
            for li in range(nlev):
                fq = jnp.exp(expo[(2 * li) * c:(2 * li + 1) * c])
                fk = jnp.exp(expo[(2 * li + 1) * c:(2 * li + 2) * c])
                qs, ks = (q * fq).astype(BF16), (k * fk).astype(BF16)
                a = a + masks[li] * _dot(qs, ks, NT)
                dam = (masks[li] * da).astype(BF16)
                t_q = _dot(dam, ks)
                t_k = _dot(dam, qs, TN)
                dq = dq + t_q * fq
                dk = dk + t_k * fk
                db = db + (qs.astype(F32) * t_q - ks.astype(F32) * t_k)
            dv = _dot(a.astype(BF16), do, TN) + _dot(k_dec, dstb, NT)
            dst_ref[hh] = dst * f_cum[c - 1:c, :] + _dot(do, q_in, TN)
            dlf = _join3(_dot(suf_ref[...], _split3(db))) + run_ref[hh]
            run_ref[hh] = dlf[0:1, :]
            dforget = dlf / forget - dk
            dlb_ref[:, cols] += jnp.sum(dforget * (1.0 - sf), axis=0, keepdims=True)
            dproj_ref[1, rows, cols] = (dforget * (1.0 - lb) * sf * (1.0 - sf)).astype(dproj_ref.dtype)
            dproj_ref[0, rows, cols] = (dq * sq * (1.0 + qr * (1.0 - sq))).astype(dproj_ref.dtype)
            dproj_ref[2, rows, cols] = dv.astype(dproj_ref.dtype)

        def chunk(it, carry):
            ci = ncb - 1 - it
            rows = pl.ds(pl.multiple_of(ci * c, c), c)
            for hh in range(hp):
                head_chunk(ci, rows, hh, slice(hh * HG_HEAD_DIM, (hh + 1) * HG_HEAD_DIM))
            return carry

        lax.fori_loop(0, ncb, chunk, 0)

    part = lambda k: pl.BlockSpec((tb, wide), lambda h, b: (nb - 1 - b, k * (nh // hp) + h))
    head_row = pl.BlockSpec((1, wide), lambda h, b: (0, h))
    tok = pl.BlockSpec((tb, wide), lambda h, b: (nb - 1 - b, h))
    const2 = lambda shape: pl.BlockSpec(shape, lambda h, b: (0, 0))
    return _call_with_exchange(
        body, exchange, grid=(nh // hp, nb),
        in_specs=[part(0), part(1), part(2), part(3), head_row, const2((1, HG_HEAD_DIM)), tok,
                  pl.BlockSpec((ncb, hp, HG_HEAD_DIM, HG_HEAD_DIM), lambda h, b: (nb - 1 - b, h, 0, 0)),
                  tok, const2((nrow, c)), pl.BlockSpec((nlev + 1, c, c), lambda h, b: (0, 0, 0)), const2((c, c))],
        out_specs=[pl.BlockSpec((4, tb, wide), lambda h, b: (0, nb - 1 - b, h)), head_row, head_row],
        out_shape=[jax.ShapeDtypeStruct((4, s, d), BF16)] + [jax.ShapeDtypeStruct((1, d), F32)] * 2,
        scratch_shapes=[pltpu.VMEM((hp, HG_HEAD_DIM, HG_HEAD_DIM), F32), pltpu.VMEM((hp, 1, HG_HEAD_DIM), F32)],
        name="hg_bwd", args=(proj, proj, proj, proj, lb_row, gain_row, o_saved, states, dy, m_all, mask_all, suffix))


def _lb_fwd(logits):
    n, d = logits.shape

    def body(l_ref, lb_ref, p_ref):
        rows = [l_ref[i:i + 1, :] for i in range(n)]
        m = functools.reduce(jnp.maximum, rows)
        es = [jnp.exp(r - m) for r in rows]
        tot = functools.reduce(lambda a, b: a + b, es)
        ps = [e / tot for e in es]
        run = jnp.zeros_like(ps[0])
        for i in range(n):
            run = run + ps[i]
            lb_ref[i:i + 1, :] = run - ps[0]
            p_ref[i:i + 1, :] = ps[i]

    return pl.pallas_call(
        body, out_shape=[jax.ShapeDtypeStruct((n, d), F32)] * 2, name="lb_fwd",
    )(logits)


def _lb_bwd(p, dlb):
    n, d = p.shape

    def body(p_ref, dlb_ref, dl_ref):
        ps = [p_ref[i:i + 1, :] for i in range(n)]
        ds = [dlb_ref[i:i + 1, :] for i in range(n)]
        total = functools.reduce(lambda a, b: a + b, ds)
        dps = []
        for i in range(n):
            dp = functools.reduce(lambda a, b: a + b, ds[i:])
            dps.append(dp - total if i == 0 else dp)
        inner = functools.reduce(lambda a, b: a + b, [pi * di for pi, di in zip(ps, dps)])
        for i in range(n):
            dl_ref[i:i + 1, :] = ps[i] * (dps[i] - inner)

    return pl.pallas_call(body, out_shape=jax.ShapeDtypeStruct((n, d), F32), name="lb_bwd")(p, dlb)


def _as2d(a):
    return a.reshape(-1, a.shape[-1])


def _adamw(w, m, v, grads, exchange=None):
    shape = w.shape
    w2, m2, v2 = _as2d(w), _as2d(m), _as2d(v)
    g2 = [_as2d(g) for g in grads]
    rows, cols = w2.shape
    tr = _pick(rows, 512)
    ng = len(g2)
    bc1 = 1.0 - ADAM_B1 ** ADAM_STEP
    bc2 = 1.0 - ADAM_B2 ** ADAM_STEP

    def body(w_ref, m_ref, v_ref, *rest):
        g = rest[0][...]
        for extra in rest[1:ng]:
            g = g + extra[...]
        g_out, d_out, m_out, v_out = rest[ng:]
        mn = ADAM_B1 * m_ref[...] + (1.0 - ADAM_B1) * g
        vn = ADAM_B2 * v_ref[...] + (1.0 - ADAM_B2) * (g * g)
        m_hat = mn / bc1
        v_hat = vn / bc2
        g_out[...] = g
        d_out[...] = -ADAM_LR * (m_hat / (jnp.sqrt(v_hat) + ADAM_EPS) + ADAM_WD * w_ref[...])
        m_out[...] = mn
        v_out[...] = vn

    spec = pl.BlockSpec((tr, cols), lambda i: (i, 0))
    outs, moved = _call_with_exchange(
        body, exchange, grid=(rows // tr,), in_specs=[spec] * (3 + ng), out_specs=[spec] * 4,
        out_shape=[jax.ShapeDtypeStruct((rows, cols), F32)] * 4, scratch_shapes=[], name="adamw",
        args=(w2, m2, v2, *g2))
    result = tuple(o.reshape(shape) for o in outs)
    return result if exchange is None else (result, moved)


def _sum_slots(parts, recv, chip, into, index):
    _, rows, cols = parts.shape
    tr = _pick(rows, 512)

    def body(chip_ref, own_ref, r0_ref, r1_ref, r2_ref, into_ref, o_ref):
        f = lambda r: r[...].astype(F32)
        o_ref[...] = ((f(own_ref) + f(r0_ref)) + f(r1_ref)) + f(r2_ref)

    grid_spec = pltpu.PrefetchScalarGridSpec(
        num_scalar_prefetch=1, grid=(rows // tr,),
        in_specs=[pl.BlockSpec((None, tr, cols), lambda i, chip_ref: (chip_ref[0], i, 0))]
        + [pl.BlockSpec((None, tr, cols), functools.partial(lambda i, chip_ref, k: (k, i, 0), k=k)) for k in range(3)]
        + [pl.BlockSpec(memory_space=pl.ANY)],
        out_specs=pl.BlockSpec((None, tr, cols), lambda i, chip_ref: (index, i, 0)))
    return pl.pallas_call(
        body, grid_spec=grid_spec, out_shape=jax.ShapeDtypeStruct(into.shape, F32),
        input_output_aliases={5: 0}, compiler_params=_params(("parallel",)), name="sum_slots",
    )(chip, parts, recv, recv, recv, into)


def _pack_rows(pieces):
    cols = pieces[0].shape[1]
    used = sum(p.shape[0] for p in pieces)
    rows = -(-used // 8) * 8

    def body(*refs):
        out_ref = refs[-1]
        at = 0
        for ref in refs[:-1]:
            out_ref[at:at + ref.shape[0], :] = ref[...]
            at += ref.shape[0]
        if at < rows:
            out_ref[at:rows, :] = jnp.zeros((rows - at, cols), F32)

    return pl.pallas_call(body, out_shape=jax.ShapeDtypeStruct((rows, cols), F32), name="pack_rows")(*pieces)


def _sum_devices(gathered):
    n, rows, cols = gathered.shape

    def body(g_ref, o_ref):
        acc = g_ref[0]
        for i in range(1, n):
            acc = acc + g_ref[i]
        o_ref[...] = acc

    return pl.pallas_call(body, out_shape=jax.ShapeDtypeStruct((rows, cols), F32), name="sum_devices")(gathered)


def _coords():
    return lax.axis_index("x"), lax.axis_index("y"), lax.axis_index("c")


def _chip_peers(x, y, c):
    out = []
    for fx, fy in ((0, 1), (1, 0), (1, 1)):
        px = 1 - x if fx else x
        py = 1 - y if fy else y
        out.append(((px, py, c), 2 * px + py))
    return out


class _ChipExchange:
    def __init__(self, kind, arrays):
        self.kind, self.kinds, self.arrays, self.n = kind, [kind] * len(arrays), list(arrays), len(arrays)
        self._shapes()

    def also(self, kind, arrays):
        self.kinds += [kind] * len(arrays)
        self.arrays += list(arrays)
        self.n = len(self.arrays)
        self._shapes()
        return self

    def _shapes(self):
        lead = {"gather": lambda a: (N_CHIPS,) + a.shape, "scatter": lambda a: (3,) + a.shape[1:],
                "swap": lambda a: a.shape}
        self.out_shape = [jax.ShapeDtypeStruct(lead[k](a), a.dtype) for k, a in zip(self.kinds, self.arrays)]
        for k, a in zip(self.kinds, self.arrays):
            assert k != "gather" or a.shape[0] % 2 == 0, "a gathered array is cut in two along its leading axis"
        self.scratch = [pltpu.SemaphoreType.DMA((6 * self.n,)), pltpu.SemaphoreType.DMA((6 * self.n,)),
                        pltpu.SemaphoreType.DMA((self.n,))]

    def copies(self, ins, outs, send_sems, recv_sems, local_sems):
        x, y, c = _coords()
        me = 2 * x + y
        sibling = (x, y, 1 - c)
        starts, waits, last = [], [], []
        for t, kind in enumerate(self.kinds):
            if kind == "swap":
                cp = pltpu.make_async_remote_copy(
                    src_ref=ins[t], dst_ref=outs[t], send_sem=send_sems.at[6 * t], recv_sem=recv_sems.at[6 * t],
                    device_id=sibling, device_id_type=MESH)
                starts.append(cp.start)
                waits += [cp.wait_send, cp.wait_recv]
                continue
            if kind == "gather":
                own = pltpu.make_async_copy(ins[t], outs[t].at[me], local_sems.at[t])
                starts.append(own.start)
                waits.append(own.wait)
                half_rows = ins[t].shape[0] // 2
                mine = pl.ds(c * half_rows, half_rows)
                theirs = pl.ds((1 - c) * half_rows, half_rows)
            for k, (peer, peer_chip) in enumerate(_chip_peers(x, y, c)):
                sems = dict(send_sem=send_sems.at[6 * t + k], recv_sem=recv_sems.at[6 * t + k],
                            device_id=peer, device_id_type=MESH)
                if kind == "scatter":
                    send = pltpu.make_async_remote_copy(src_ref=ins[t].at[peer_chip], dst_ref=outs[t].at[k], **sems)
                    starts.append(send.start)
                    waits += [send.wait_send, send.wait_recv]
                    continue
                send = pltpu.make_async_remote_copy(
                    src_ref=ins[t].at[mine], dst_ref=outs[t].at[me].at[mine], **sems)
                landed = outs[t].at[peer_chip].at[mine]
                recv = pltpu.make_async_remote_copy(src_ref=ins[t].at[mine], dst_ref=landed, **sems)
                pass_on = pltpu.make_async_remote_copy(
                    src_ref=landed, dst_ref=landed, send_sem=send_sems.at[6 * t + 3 + k],
                    recv_sem=recv_sems.at[6 * t + 3 + k], device_id=sibling, device_id_type=MESH)
                handed = pltpu.make_async_remote_copy(
                    src_ref=landed, dst_ref=outs[t].at[peer_chip].at[theirs], send_sem=send_sems.at[6 * t + 3 + k],
                    recv_sem=recv_sems.at[6 * t + 3 + k], device_id=sibling, device_id_type=MESH)
                starts.append(send.start)
                waits += [recv.wait_recv, pass_on.start]
                last += [send.wait_send, pass_on.wait_send, handed.wait_recv]
        return starts, waits + last

    def run(self, name):
        n = self.n

        def body(*refs):
            starts, waits = self.copies(refs[:n], refs[n:2 * n], *refs[2 * n:])
            for f in starts + waits:
                f()

        return pl.pallas_call(body, in_specs=[HBM_SPEC] * n, out_specs=[HBM_SPEC] * n, out_shape=self.out_shape,
                              scratch_shapes=self.scratch, name=name)(*self.arrays)


def _call_with_exchange(body, exchange, *, grid, in_specs, out_specs, out_shape, scratch_shapes, name, args,
                        sequential=False):
    if exchange is None:
        first_axis = "arbitrary" if sequential else "parallel"
        outs = pl.pallas_call(body, grid=grid, in_specs=in_specs, out_specs=out_specs, out_shape=out_shape,
                              scratch_shapes=scratch_shapes,
                              compiler_params=_params((first_axis,) + ("arbitrary",) * (len(grid) - 1)),
                              name=name)(*args)
        return outs, []
    n_in, n_out, n_scr, n = len(in_specs), len(out_specs), len(scratch_shapes), exchange.n

    def wrapped(*refs):
        ins, ex_in = refs[:n_in], refs[n_in:n_in + n]
        outs = refs[n_in + n:n_in + n + n_out]
        ex_out = refs[n_in + n + n_out:n_in + 2 * n + n_out]
        scr = refs[n_in + 2 * n + n_out:n_in + 2 * n + n_out + n_scr]
        sems = refs[n_in + 2 * n + n_out + n_scr:]
        ids = [pl.program_id(a) for a in range(len(grid))]
        first = functools.reduce(jnp.logical_and, [i == 0 for i in ids])
        last = functools.reduce(jnp.logical_and, [i == g - 1 for i, g in zip(ids, grid)])

        @pl.when(first)
        def _():
            for f in exchange.copies(ex_in, ex_out, *sems)[0]:
                f()

        body(*ins, *outs, *scr)

        @pl.when(last)
        def _():
            for f in exchange.copies(ex_in, ex_out, *sems)[1]:
                f()

    res = pl.pallas_call(
        wrapped, grid=grid, in_specs=list(in_specs) + [HBM_SPEC] * n, out_specs=list(out_specs) + [HBM_SPEC] * n,
        out_shape=list(out_shape) + exchange.out_shape, scratch_shapes=list(scratch_shapes) + exchange.scratch,
        compiler_params=_params(("arbitrary",) * len(grid)), name=name + "_" + exchange.kind,
    )(*args, *exchange.arrays)
    return res[:n_out], res[n_out:]


def _gather_devices(a):
    def body(in_ref, out_ref, send_sems, recv_sems, local_sem):
        x, y, c = _coords()
        me = 4 * x + 2 * y + c
        own = pltpu.make_async_copy(in_ref, out_ref.at[me], local_sem)
        own.start()
        waits = [own.wait]
        for k in range(1, N_DEVICES):
            px = 1 - x if k & 4 else x
            py = 1 - y if k & 2 else y
            pc = 1 - c if k & 1 else c
            peer = (px, py, pc)
            send = pltpu.make_async_remote_copy(
                src_ref=in_ref, dst_ref=out_ref.at[me], send_sem=send_sems.at[k - 1], recv_sem=recv_sems.at[k - 1],
                device_id=peer, device_id_type=MESH)
            send.start()
            recv = pltpu.make_async_remote_copy(
                src_ref=in_ref, dst_ref=out_ref.at[4 * px + 2 * py + pc], send_sem=send_sems.at[k - 1],
                recv_sem=recv_sems.at[k - 1], device_id=peer, device_id_type=MESH)
            waits += [send.wait_send, recv.wait_recv]
        for w in waits:
            w()

    return pl.pallas_call(
        body, in_specs=[HBM_SPEC], out_specs=HBM_SPEC,
        out_shape=jax.ShapeDtypeStruct((N_DEVICES,) + a.shape, a.dtype),
        scratch_shapes=[pltpu.SemaphoreType.DMA((N_DEVICES - 1,)), pltpu.SemaphoreType.DMA((N_DEVICES - 1,)),
                        pltpu.SemaphoreType.DMA],
        name="gather_devices",
    )(a)


def _mlp_grad_epilogue(r, u):
    return r * (2.0 * jnp.maximum(u, 0.0))


def kernel(x, norm_gains, sb_w_qkv, sb_q_gain, sb_k_gain, sb_w_o, hg_w_in, hg_lb_logits, hg_norm_gain, hg_w_o, mlp_w1, mlp_w2, loss_target, m_norm_gains, m_sb_w_qkv, m_sb_q_gain, m_sb_k_gain, m_sb_w_o, m_hg_w_in, m_hg_lb_logits, m_hg_norm_gain, m_hg_w_o, m_mlp_w1, m_mlp_w2, v_norm_gains, v_sb_w_qkv, v_sb_q_gain, v_sb_k_gain, v_sb_w_o, v_hg_w_in, v_hg_lb_logits, v_hg_norm_gain, v_hg_w_o, v_mlp_w1, v_mlp_w2):
    depth = norm_gains.shape[0]
    n_sb, n_hg = sb_w_qkv.shape[0], hg_w_in.shape[0]
    xs, tgt = x[0], loss_target[0]
    s, d = xs.shape
    dq = d // N_CHIPS
    cx, cy, cc = _coords()
    chip = 2 * cx + cy
    chip_arr = jnp.reshape(chip, (1,)).astype(jnp.int32)

    def mixer_weights(layer):
        j = layer // 2
        return (sb_w_qkv[j], sb_w_o[j]) if layer % 2 == 0 else (hg_w_in[j], hg_w_o[j])

    w_in_g, ng_g, lbl_g = _ChipExchange(
        "gather", [mixer_weights(0)[0].astype(BF16), norm_gains, hg_lb_logits]).run("gather_first")
    gains = jnp.transpose(ng_g, (1, 2, 0, 3)).reshape(depth, 2, d)
    logits = jnp.transpose(lbl_g, (1, 0, 2)).reshape(n_hg, d)
    lbs, lb_p = _lb_fwd(logits)
    qg_rows = [jnp.tile(sb_q_gain[j], d // SB_HEAD_DIM)[None] for j in range(n_sb)]
    kg_rows = [jnp.tile(sb_k_gain[j], d // SB_HEAD_DIM)[None] for j in range(n_sb)]

    saved, wts = [], []
    xc = xs
    for layer in range(depth):
        j = layer // 2
        ahead = [mixer_weights(layer)[1], mlp_w1[layer], mlp_w2[layer]]
        if layer + 1 < depth:
            ahead.append(mixer_weights(layer + 1)[0])
        gather = _ChipExchange("gather", [a.astype(BF16) for a in ahead])
        h1 = _rmsnorm_fwd(xc, gains[layer, 0][None])
        if layer % 2 == 0:
            qkv = _mm_fwd_cols(h1, w_in_g, name="sb_qkv")
            qn, kn, vb = _qk_norm_fwd(qkv, qg_rows[j], kg_rows[j])
            o, moved = _sb_attn_fwd(qn, kn, vb, gather)
            x_mid = _mm_fwd_rows(o, moved[0], residual=xc, name="sb_out")
            mix = (qkv, qn, kn, vb, o)
        else:
            proj = _mm_fwd_cols(h1, w_in_g, name="hg_in")
            (y, o, states), moved = _hg_fwd(proj, lbs[j][None], hg_norm_gain[j][None], gather)
            x_mid = _mm_fwd_rows(y, moved[0], residual=xc, name="hg_out")
            mix = (proj, y, o, states)
        w_out_g, w1_g, w2_g = moved[:3]
        h2 = _rmsnorm_fwd(x_mid, gains[layer, 1][None])
        u = _mm_fwd_cols(h2, w1_g, name="mlp_up")
        x_out = _mm_fwd_rows(u, w2_g, residual=x_mid, a_fn=_relu2, name="mlp_down")
        saved.append((xc, h1, mix, x_mid, h2, u))
        wts.append((w_in_g, w_out_g, w1_g, w2_g))
        w_in_g = moved[3] if layer + 1 < depth else None
        xc = x_out

    sq, dx = _loss_head(xc, tgt)
    loss = lax.psum(jnp.sum(sq) * (0.5 / d), ("x", "y", "c"))

    dgains = [[None, None] for _ in range(depth)]
    dqg, dkg = [None] * n_sb, [None] * n_sb
    dhgain, dlb = [None] * n_hg, [None] * n_hg
    grads, received, pending = {}, {}, []

    def ready(key, parts):
        grads[key] = parts
        pending.append(key)

    def scatter_of(keys):
        return _ChipExchange("scatter", [grads[k] for k in keys]) if keys else None

    def sent(keys, moved):
        for k, r in zip(keys, moved):
            received[k] = r
            pending.remove(k)

    def chip_sum(kind, layers):
        total = jnp.zeros((len(layers),) + grads[kind, layers[0]].shape[1:], F32)
        for index, l in enumerate(layers):
            total = _sum_slots(grads[kind, l], received[kind, l], chip_arr, total, index)
        return total

    sb_layers, hg_layers = range(0, depth, 2), range(1, depth, 2)
    tensors = [("in", sb_layers), ("out", sb_layers), ("in", hg_layers), ("out", hg_layers),
               ("w1", range(depth)), ("w2", range(depth))]

    for layer in reversed(range(depth)):
        j = layer // 2
        x_in, h1, mix, x_mid, h2, u = saved[layer]
        w_in_g, w_out_g, w1_g, w2_g = wts[layer]
        du = _mm_bwd_rows(dx, w2_g, name="mlp_down_dx", out_dtype=BF16, epi_fn=_mlp_grad_epilogue, epi_args=(u,))
        ready(("w2", layer), _mm_dw_rows(u, dx, a_fn=_relu2, name="mlp_down_dw"))
        ready(("w1", layer), _mm_dw_cols(h2, du, name="mlp_up_dw"))
        dx, dgains[layer][1] = _mm_bwd_cols_norm(du, w1_g, x_mid, gains[layer, 1][None], dx, name="mlp_up_dx")
        if layer % 2 == 0:
            qkv, qn, kn, vb, o = mix
            do = _mm_bwd_rows(dx, w_out_g, name="sb_out_dx")
            ready(("out", layer), _mm_dw_rows(o, dx, name="sb_out_dw"))
            keys = list(pending)
            (dqn, dkn, dv), moved = _sb_attn_bwd(qn, kn, vb, do, scatter_of(keys))
            sent(keys, moved)
            d_in, dqg[j], dkg[j] = _qk_norm_bwd(qkv, qg_rows[j], kg_rows[j], dqn, dkn, dv)
            ready(("in", layer), _mm_dw_cols(h1, d_in, name="sb_qkv_dw"))
            dx_name = "sb_qkv_dx"
        else:
            proj, y, o, states = mix
            dy = _mm_bwd_rows(dx, w_out_g, name="hg_out_dx")
            ready(("out", layer), _mm_dw_rows(y, dx, name="hg_out_dw"))
            keys = list(pending)
            (d_in, dlb[j], dhgain[j]), moved = _hg_bwd(
                proj, lbs[j][None], hg_norm_gain[j][None], o, states, dy, scatter_of(keys))
            sent(keys, moved)
            ready(("in", layer), _mm_dw_cols(h1, d_in, name="hg_in_dw"))
            dx_name = "hg_in_dx"
        if layer > 0:
            dx, dgains[layer][0] = _mm_bwd_cols_norm(d_in, w_in_g, x_in, gains[layer, 0][None], dx, name=dx_name)
        else:
            keys = list(pending)
            early_sums = [chip_sum(*t) for t in tensors[1:]]
            (dx, dgains[layer][0]), moved = _mm_bwd_cols_norm(
                d_in, w_in_g, x_in, gains[layer, 0][None], dx, name=dx_name,
                exchange=scatter_of(keys).also("swap", early_sums))
            sent(keys, moved[:len(keys)])
            early_other = moved[len(keys):]
    grad_x = dx[None]
    dlogits = _lb_bwd(lb_p, jnp.concatenate(dlb, axis=0))

    big_w = [sb_w_qkv, sb_w_o, hg_w_in, hg_w_o, mlp_w1, mlp_w2]
    big_m = [m_sb_w_qkv, m_sb_w_o, m_hg_w_in, m_hg_w_o, m_mlp_w1, m_mlp_w2]
    big_v = [v_sb_w_qkv, v_sb_w_o, v_hg_w_in, v_hg_w_o, v_mlp_w1, v_mlp_w2]
    late_sum = chip_sum(*tensors[0])
    big_second, late_other = _adamw(big_w[1], big_m[1], big_v[1], [early_sums[0], early_other[0]],
                                    _ChipExchange("swap", [late_sum]))
    big = [_adamw(big_w[0], big_m[0], big_v[0], [late_sum, late_other[0]]), big_second]
    big += [_adamw(w, m, v, [a, b]) for w, m, v, a, b in
            zip(big_w[2:], big_m[2:], big_v[2:], early_sums[1:], early_other[1:])]

    pieces = [r for pair in dgains for r in pair] + [dlogits] + dqg + dkg + dhgain
    small = _sum_devices(_gather_devices(_pack_rows(pieces)))
    my_cols = lambda a: lax.dynamic_slice_in_dim(a, chip * dq, dq, axis=1)
    fold = lambda rows, width: jnp.sum(rows.reshape(rows.shape[0], -1, width), axis=1)
    base = 2 * depth + n_hg
    g_ng = my_cols(small[0:2 * depth]).reshape(norm_gains.shape)
    g_lbl = my_cols(small[2 * depth:base])
    g_qg = fold(small[base:base + n_sb], SB_HEAD_DIM)
    g_kg = fold(small[base + n_sb:base + 2 * n_sb], SB_HEAD_DIM)
    g_hgn = fold(small[base + 2 * n_sb:base + 2 * n_sb + n_hg], HG_HEAD_DIM)
    r_ng = _adamw(norm_gains, m_norm_gains, v_norm_gains, [g_ng])
    r_qg = _adamw(sb_q_gain, m_sb_q_gain, v_sb_q_gain, [g_qg])
    r_kg = _adamw(sb_k_gain, m_sb_k_gain, v_sb_k_gain, [g_kg])
    r_lbl = _adamw(hg_lb_logits, m_hg_lb_logits, v_hg_lb_logits, [g_lbl])
    r_hgn = _adamw(hg_norm_gain, m_hg_norm_gain, v_hg_norm_gain, [g_hgn])

    per_weight = [r_ng, big[0], r_qg, r_kg, big[1], big[2], r_lbl, r_hgn, big[3], big[4], big[5]]
    outs = [loss, grad_x]
    for field in range(4):
        outs += [r[field] for r in per_weight]
    return tuple(outs)
```

```python
import functools
import math

import numpy as np
import jax
import jax.numpy as jnp
from jax import lax
from jax.experimental import pallas as pl
from jax.experimental.pallas import tpu as pltpu

F32 = jnp.float32
BF16 = jnp.bfloat16
GRAD_SLOT_DTYPE = jnp.bfloat16

NORM_EPS = 1e-6
SB_HEAD_DIM = 64
HG_HEAD_DIM = 128
HG_CHUNK = 128
LANES = 128
VMEM_LIMIT_BYTES = 56 * 2 ** 20
N_CHIPS = 4
N_DEVICES = 8

ADAM_LR = 0.001
ADAM_B1 = 0.9
ADAM_B2 = 0.999
ADAM_EPS = 1e-08
ADAM_WD = 0.01
ADAM_STEP = 10

MESH = pl.DeviceIdType.MESH
HBM_SPEC = pl.BlockSpec(memory_space=pltpu.HBM)

NN = (((1,), (0,)), ((), ()))
NT = (((1,), (1,)), ((), ()))
TN = (((0,), (0,)), ((), ()))


def _params(sem=None):
    return pltpu.CompilerParams(dimension_semantics=sem, vmem_limit_bytes=VMEM_LIMIT_BYTES)


def _pick(dim, pref):
    for t in (1024, 768, 512, 384, 256, 128, 64, 32, 16, 8):
        if t <= pref and dim % t == 0:
            return t
    return dim


def _dot(a, b, dims=NN):
    return lax.dot_general(a, b, dims, preferred_element_type=F32)


def _sigmoid(x):
    e = jnp.exp(-jnp.abs(x))
    return jnp.where(x >= 0, 1.0, e) / (1.0 + e)


def _matmul(a, b, *, mode, grid, a_block, a_map, b_block, b_map, o_block, o_map, out_shape, out_dtype, name,
            a_fn=None, epi_fn=None, epi_args=(), epi_row_args=(), col_sums=False, exchange=None):
    nk = grid[2]
    dims = {"nn": NN, "nt": NT, "tn": TN}[mode]
    n_epi = len(epi_args) + len(epi_row_args)
    n_out = 2 if col_sums else 1
    tn = o_block[-1]
    assert not col_sums or grid[1] == 1, "the column sums stay resident only with one tile along N"
    split_rows = (nk == 1 and epi_fn is not None and not col_sums and mode != "tn" and len(a_block) == 2
                  and len(o_block) == 2 and a_block[0] % 512 == 0)

    def body(a_ref, b_ref, *rest):
        epi_refs = rest[:n_epi]
        o_ref = rest[n_epi]
        kk = pl.program_id(2)

        def emit(r):
            if epi_fn is not None:
                r = epi_fn(r, *[e[...] for e in epi_refs])
            if col_sums:
                r, row = r
                sums_ref = rest[n_epi + 1]
                first = pl.program_id(0) == 0

                @pl.when(first)
                def _():
                    sums_ref[...] = row

                @pl.when(jnp.logical_not(first))
                def _():
                    sums_ref[...] += row
            o_ref[...] = r.astype(o_ref.dtype)

        if split_rows:
            sub = a_block[0] // 2
            for r in range(2):
                rows = slice(r * sub, (r + 1) * sub)
                av = a_ref[rows, :]
                if a_fn is not None:
                    av = a_fn(av)
                part = _dot(av.astype(BF16), b_ref[...].astype(BF16), dims)
                tiles = [e[rows, :] for e in epi_refs[:len(epi_args)]]
                o_ref[rows, :] = epi_fn(part, *tiles, *[e[...] for e in epi_refs[len(epi_args):]]).astype(o_ref.dtype)
            return
        av = a_ref[...]
        if a_fn is not None:
            av = a_fn(av)
        part = _dot(av.astype(BF16), b_ref[...].astype(BF16), dims)
        if nk == 1:
            emit(part)
            return
        acc_ref = rest[n_epi + n_out]

        @pl.when(kk == 0)
        def _():
            acc_ref[...] = part

        @pl.when(kk > 0)
        def _():
            acc_ref[...] += part

        @pl.when(kk == nk - 1)
        def _():
            emit(acc_ref[...])

    acc_shape = tuple(d for d in o_block if d is not None)
    row_spec = pl.BlockSpec((1, tn), lambda i, j, kk: (0, j))
    in_specs = [pl.BlockSpec(a_block, a_map), pl.BlockSpec(b_block, b_map)]
    in_specs += [pl.BlockSpec(o_block, o_map) for _ in epi_args] + [row_spec for _ in epi_row_args]
    out_specs, out_shapes = [pl.BlockSpec(o_block, o_map)], [jax.ShapeDtypeStruct(out_shape, out_dtype)]
    if col_sums:
        out_specs, out_shapes = out_specs + [row_spec], out_shapes + [jax.ShapeDtypeStruct((1, out_shape[-1]), F32)]
    outs, moved = _call_with_exchange(
        body, exchange, grid=grid, in_specs=in_specs, out_specs=out_specs, out_shape=out_shapes,
        scratch_shapes=[pltpu.VMEM(acc_shape, F32)] if nk > 1 else [], name=name,
        args=(a, b, *epi_args, *epi_row_args), sequential=col_sums)
    result = tuple(outs) if col_sums else outs[0]
    return result if exchange is None else (result, moved)


def _relu2(u):
    r = jnp.maximum(u, 0.0)
    return r * r


def _add(r, res):
    return r + res


def _mm_fwd_cols(a, wg, *, name):
    s, k = a.shape
    ncs = wg.shape[2]
    tm, tk, tn = _pick(s, 1024), _pick(k, 1024), _pick(ncs, 1024)
    npb = ncs // tn
    return _matmul(a, wg, mode="nn", grid=(s // tm, N_CHIPS * npb, k // tk),
                   a_block=(tm, tk), a_map=lambda i, j, kk: (i, kk),
                   b_block=(None, tk, tn), b_map=lambda i, j, kk: (j // npb, kk, j % npb),
                   o_block=(tm, tn), o_map=lambda i, j, kk: (i, j),
                   out_shape=(s, N_CHIPS * ncs), out_dtype=F32, name=name)


def _rows_joined(wg):
    assert wg.shape[1] % 16 == 0, "joining the leading axes must not cross a tile of 16 rows"
    return wg.reshape(wg.shape[0] * wg.shape[1], wg.shape[2])


def _mm_fwd_rows(a, wg, *, residual, name, a_fn=None):
    s = a.shape[0]
    w = _rows_joined(wg)
    k, n = w.shape
    tm, tk, tn = _pick(s, 1024), _pick(k, 1024), _pick(n, 1024)
    return _matmul(a, w, mode="nn", grid=(s // tm, n // tn, k // tk),
                   a_block=(tm, tk), a_map=lambda i, j, kk: (i, kk),
                   b_block=(tk, tn), b_map=lambda i, j, kk: (kk, j),
                   o_block=(tm, tn), o_map=lambda i, j, kk: (i, j),
                   out_shape=(s, n), out_dtype=F32, name=name, a_fn=a_fn, epi_fn=_add, epi_args=(residual,))


def _rmsnorm_grad(dh, x, dx_res, gain):
    r = lax.rsqrt(jnp.mean(x * x, axis=-1, keepdims=True) + NORM_EPS)
    xhat = x * r
    dxhat = dh * gain
    dx = r * (dxhat - xhat * jnp.mean(dxhat * xhat, axis=-1, keepdims=True))
    return dx_res + dx, jnp.sum(dh * xhat, axis=0, keepdims=True)


def _mm_bwd_cols_norm(dy, wg, x, gain_row, dx_res, *, name, exchange=None):
    by_slot = dy.ndim == 3
    s = dy.shape[1] if by_slot else dy.shape[0]
    kw, ncs = wg.shape[1], wg.shape[2]
    tm = _pick(s, 512)
    halves = 2 if tm % 512 == 0 else 1
    sub = tm // halves

    def body(dy_ref, w_ref, x_ref, res_ref, g_ref, dx_ref, dg_ref):
        total = None
        for r in range(halves):
            rows = slice(r * sub, (r + 1) * sub)
            acc = None
            for q in range(N_CHIPS):
                a = dy_ref[q, rows, :] if by_slot else dy_ref[rows, q * ncs:(q + 1) * ncs]
                part = _dot(a.astype(BF16), w_ref[q], NT)
                acc = part if acc is None else acc + part
            dx, row = _rmsnorm_grad(acc, x_ref[rows, :], res_ref[rows, :], g_ref[...])
            dx_ref[rows, :] = dx
            total = row if total is None else total + row
        first = pl.program_id(0) == 0

        @pl.when(first)
        def _():
            dg_ref[...] = total

        @pl.when(jnp.logical_not(first))
        def _():
            dg_ref[...] += total

    dy_spec = pl.BlockSpec((N_CHIPS, tm, ncs), lambda i: (0, i, 0)) if by_slot else pl.BlockSpec(
        (tm, N_CHIPS * ncs), lambda i: (i, 0))
    tok = pl.BlockSpec((tm, kw), lambda i: (i, 0))
    row_spec = pl.BlockSpec((1, kw), lambda i: (0, 0))
    outs, moved = _call_with_exchange(
        body, exchange, grid=(s // tm,),
        in_specs=[dy_spec, pl.BlockSpec((N_CHIPS, kw, ncs), lambda i: (0, 0, 0)), tok, tok, row_spec],
        out_specs=[tok, row_spec],
        out_shape=[jax.ShapeDtypeStruct((s, kw), F32), jax.ShapeDtypeStruct((1, kw), F32)],
        scratch_shapes=[], name=name, args=(dy, wg, x, dx_res, gain_row), sequential=True)
    return tuple(outs) if exchange is None else (tuple(outs), moved)


def _mm_bwd_rows(dy, wg, *, name, out_dtype=F32, epi_fn=None, epi_args=()):
    s, n = dy.shape
    w = _rows_joined(wg)
    rows = w.shape[0]
    tm, tn, tk = _pick(s, 1024), _pick(rows, 1024), _pick(n, 1024)
    return _matmul(dy, w, mode="nt", grid=(s // tm, rows // tn, n // tk),
                   a_block=(tm, tk), a_map=lambda i, j, kk: (i, kk),
                   b_block=(tn, tk), b_map=lambda i, j, kk: (j, kk),
                   o_block=(tm, tn), o_map=lambda i, j, kk: (i, j),
                   out_shape=(s, rows), out_dtype=out_dtype, name=name, epi_fn=epi_fn, epi_args=epi_args)


def _mm_dw_cols(xa, dy, *, name):
    s, kx = xa.shape
    by_slot = dy.ndim == 3
    ncs = dy.shape[2] if by_slot else dy.shape[1] // N_CHIPS
    tm, tn, tk = _pick(kx, 1024), _pick(ncs, 1024), _pick(s, 1024)
    npb = ncs // tn
    b_block, b_map = ((None, tk, tn), lambda i, j, kk: (j // npb, kk, j % npb)) if by_slot else (
        (tk, tn), lambda i, j, kk: (kk, j))
    return _matmul(xa, dy, mode="tn", grid=(kx // tm, N_CHIPS * npb, s // tk),
                   a_block=(tk, tm), a_map=lambda i, j, kk: (kk, i),
                   b_block=b_block, b_map=b_map,
                   o_block=(None, tm, tn), o_map=lambda i, j, kk: (j // npb, i, j % npb),
                   out_shape=(N_CHIPS, kx, ncs), out_dtype=GRAD_SLOT_DTYPE, name=name)


def _mm_dw_rows(xa, dy, *, name, a_fn=None):
    s, n = dy.shape
    rows = xa.shape[1]
    assert (rows // N_CHIPS) % 16 == 0, "splitting the rows into slots must not cut a tile of 16 rows"
    tm, tn, tk = _pick(rows, 1024), _pick(n, 1024), _pick(s, 1024)
    dw = _matmul(xa, dy, mode="tn", grid=(rows // tm, n // tn, s // tk),
                 a_block=(tk, tm), a_map=lambda i, j, kk: (kk, i),
                 b_block=(tk, tn), b_map=lambda i, j, kk: (kk, j),
                 o_block=(tm, tn), o_map=lambda i, j, kk: (i, j),
                 out_shape=(rows, n), out_dtype=GRAD_SLOT_DTYPE, name=name, a_fn=a_fn)
    return dw.reshape(N_CHIPS, rows // N_CHIPS, n)


def _rmsnorm_fwd(x, gain_row):
    s, d = x.shape
    ts = _pick(s, 1024)

    def body(x_ref, g_ref, h_ref):
        xv = x_ref[...]
        r = lax.rsqrt(jnp.mean(xv * xv, axis=-1, keepdims=True) + NORM_EPS)
        h_ref[...] = (xv * r * g_ref[...]).astype(h_ref.dtype)

    return pl.pallas_call(
        body, grid=(s // ts,),
        in_specs=[pl.BlockSpec((ts, d), lambda i: (i, 0)), pl.BlockSpec((1, d), lambda i: (0, 0))],
        out_specs=pl.BlockSpec((ts, d), lambda i: (i, 0)),
        out_shape=jax.ShapeDtypeStruct((s, d), BF16),
        compiler_params=_params(("parallel",)), name="rmsnorm_fwd",
    )(x, gain_row)


def _loss_head(y, target):
    s, d = y.shape
    ts = _pick(s, 512)

    def body(y_ref, t_ref, sq_ref, dy_ref):
        i = pl.program_id(0)
        err = y_ref[...] - t_ref[...]
        dy_ref[...] = err / d
        part = jnp.sum(err * err, axis=0, keepdims=True)

        @pl.when(i == 0)
        def _():
            sq_ref[...] = part

        @pl.when(i > 0)
        def _():
            sq_ref[...] += part

    return pl.pallas_call(
        body, grid=(s // ts,),
        in_specs=[pl.BlockSpec((ts, d), lambda i: (i, 0)), pl.BlockSpec((ts, d), lambda i: (i, 0))],
        out_specs=[pl.BlockSpec((1, d), lambda i: (0, 0)), pl.BlockSpec((ts, d), lambda i: (i, 0))],
        out_shape=[jax.ShapeDtypeStruct((1, d), F32), jax.ShapeDtypeStruct((s, d), F32)],
        compiler_params=_params(("arbitrary",)), name="loss_head",
    )(y, target)


def _pair_ones():
    lane = np.arange(LANES)
    same_half = (lane[:, None] // SB_HEAD_DIM == lane[None, :] // SB_HEAD_DIM).astype(np.float32)
    return jnp.asarray(np.concatenate([same_half, same_half], axis=0), BF16)


def _pair_mean(val, pair_ones):
    return _dot(jnp.concatenate(_split2(val), axis=1), pair_ones) * (1.0 / SB_HEAD_DIM)


def _pair_mean_lanes(val, low_half):
    s0 = jnp.sum(jnp.where(low_half, val, 0.0), axis=-1, keepdims=True)
    s1 = jnp.sum(jnp.where(low_half, 0.0, val), axis=-1, keepdims=True)
    return jnp.where(low_half, s0, s1) * (1.0 / SB_HEAD_DIM)


def _qk_norm_fwd(qkv, qgain_row, kgain_row):
    s, d3 = qkv.shape
    d = d3 // 3
    ts = _pick(s, 512)
    groups = d // LANES

    def body(q_ref, k_ref, v_ref, qg_ref, kg_ref, ones_ref, qn_ref, kn_ref, vb_ref):
        for src, gain, dst in ((q_ref, qg_ref, qn_ref), (k_ref, kg_ref, kn_ref)):
            for p in range(groups):
                cols = slice(p * LANES, (p + 1) * LANES)
                xp = src[:, cols]
                r = lax.rsqrt(_pair_mean(xp * xp, ones_ref[...]) + NORM_EPS)
                dst[:, cols] = (xp * r * gain[:, cols]).astype(dst.dtype)
        vb_ref[...] = v_ref[...].astype(vb_ref.dtype)

    tok = lambda c: pl.BlockSpec((ts, d), lambda i: (i, c))
    row = pl.BlockSpec((1, d), lambda i: (0, 0))
    return pl.pallas_call(
        body, grid=(s // ts,),
        in_specs=[tok(0), tok(1), tok(2), row, row, pl.BlockSpec((2 * LANES, LANES), lambda i: (0, 0))],
        out_specs=[tok(0), tok(0), tok(0)],
        out_shape=[jax.ShapeDtypeStruct((s, d), BF16)] * 3,
        compiler_params=_params(("parallel",)), name="qk_norm_fwd",
    )(qkv, qkv, qkv, qgain_row, kgain_row, _pair_ones())


def _qk_norm_bwd(qkv, qgain_row, kgain_row, dqn, dkn, dv):
    s, d3 = qkv.shape
    d = d3 // 3
    ts = _pick(s, 512)
    groups = d // LANES

    def body(q_ref, k_ref, qg_ref, kg_ref, dqn_ref, dkn_ref, dv_ref, dqkv_ref, dqg_ref, dkg_ref):
        i = pl.program_id(0)
        low_half = lax.broadcasted_iota(jnp.int32, (ts, LANES), 1) < SB_HEAD_DIM
        for which, (src, gain, dsrc, dgain) in enumerate(((q_ref, qg_ref, dqn_ref, dqg_ref),
                                                          (k_ref, kg_ref, dkn_ref, dkg_ref))):
            for p in range(groups):
                cols = slice(p * LANES, (p + 1) * LANES)
                xp = src[:, cols]
                r = lax.rsqrt(_pair_mean_lanes(xp * xp, low_half) + NORM_EPS)
                xhat = xp * r
                dy = dsrc[:, cols]
                dxhat = dy * gain[:, cols]
                dx = r * (dxhat - xhat * _pair_mean_lanes(dxhat * xhat, low_half))
                dqkv_ref[:, which * d + p * LANES: which * d + (p + 1) * LANES] = dx.astype(dqkv_ref.dtype)
                part = jnp.sum(dy * xhat, axis=0, keepdims=True)

                @pl.when(i == 0)
                def _():
                    dgain[:, cols] = part

                @pl.when(i > 0)
                def _():
                    dgain[:, cols] += part
        dqkv_ref[:, 2 * d:] = dv_ref[...].astype(dqkv_ref.dtype)

    tok = lambda c: pl.BlockSpec((ts, d), lambda i: (i, c))
    row = pl.BlockSpec((1, d), lambda i: (0, 0))
    return pl.pallas_call(
        body, grid=(s // ts,),
        in_specs=[tok(0), tok(1), row, row, tok(0), tok(0), tok(0)],
        out_specs=[pl.BlockSpec((ts, d3), lambda i: (i, 0)), row, row],
        out_shape=[jax.ShapeDtypeStruct((s, d3), BF16), jax.ShapeDtypeStruct((1, d), F32),
                   jax.ShapeDtypeStruct((1, d), F32)],
        compiler_params=_params(("arbitrary",)), name="qk_norm_bwd",
    )(qkv, qkv, qgain_row, kgain_row, dqn, dkn, dv)


def _split2(x):
    hi = x.astype(BF16)
    lo = (x - hi.astype(F32)).astype(BF16)
    return hi, lo


SB_TK = 128


def _sb_consts(tk):
    j = np.arange(tk)
    ones = np.ones((tk, tk), np.float32)
    out = []
    for tri in ((j[:, None] >= j[None, :]), (j[:, None] <= j[None, :])):
        half = np.concatenate([tri.astype(np.float32), ones], axis=1)
        out.append(jnp.asarray(np.concatenate([half, half], axis=0), BF16))
    return out


def _head_stack(blk, low_half):
    f = blk.astype(F32)
    return jnp.concatenate([jnp.where(low_half, f, 0.0), jnp.where(low_half, 0.0, f)], axis=0).astype(BF16)


def _sb_tile_sums(z, valid, tri2):
    e = jnp.exp(-jnp.abs(z))
    lstay = jnp.minimum(-z, 0.0) - jnp.log(1.0 + e)
    if valid is not None:
        lstay = jnp.where(valid, lstay, 0.0)
    hi, lo = _split2(lstay)
    return e, _dot(jnp.concatenate([hi, lo], axis=1), tri2)


def _sb_weights(z, c2, valid, run):
    w = jnp.exp(z + c2[:, :SB_TK] + run)
    return w if valid is None else jnp.where(valid, w, 0.0)


EXP_IS_ZERO_BELOW = -110.0


def _max_row_norm(x):
    f = x.astype(F32)
    return jnp.sqrt(jnp.max(jnp.sum(f * f, axis=-1, keepdims=True)))


def _sb_score_bound(qs, kmax_ref):
    return _max_row_norm(qs) * jnp.max(kmax_ref[...]) * 1.01 + 1.0


def _sb_rest_is_zero(run_ref, bound):
    return jnp.max(jnp.maximum(run_ref[0], run_ref[1])) + bound < EXP_IS_ZERO_BELOW


def _sb_attn_fwd(qn, kn, vb, exchange=None):
    s, d = qn.shape
    tk = SB_TK
    tq = _pick(s, 256)
    nq, ndiag = s // tq, tq // tk
    assert tq % (2 * tk) == 0, "tiles below the diagonal are taken two at a time"
    npairs = d // LANES
    scale = 1.0 / math.sqrt(SB_HEAD_DIM)
    tri_ge2, _ = _sb_consts(tk)

    def body(q_ref, k_ref, v_ref, tri_ref, o_ref, acc_ref, run_ref, kmax_ref):
        qi = pl.program_id(1)

        @pl.when(qi == 0)
        def _():
            kmax_ref[...] = jnp.full(kmax_ref.shape, _max_row_norm(k_ref[...]), F32)

        low_half = lax.broadcasted_iota(jnp.int32, (tk, LANES), 1) < SB_HEAD_DIM
        row = lax.broadcasted_iota(jnp.int32, (tq, tk), 0)
        col = lax.broadcasted_iota(jnp.int32, (tq, tk), 1)
        qs = (q_ref[...].astype(F32) * scale).astype(BF16)
        bound = _sb_score_bound(qs, kmax_ref)
        acc_ref[...] = jnp.zeros_like(acc_ref)
        run_ref[...] = jnp.zeros_like(run_ref)
        n_full = qi * ndiag

        def sums(kb, dd):
            koff = pl.multiple_of(kb * tk, tk)
            kcat = _head_stack(k_ref[pl.ds(koff, tk), :], low_half)
            vcat = _head_stack(v_ref[pl.ds(koff, tk), :], low_half)
            z2 = _dot(qs, kcat, NT)
            valid = None if dd is None else row > col + dd * tk
            zs = [z2[:, h * tk:(h + 1) * tk] for h in range(2)]
            return zs, [_sb_tile_sums(z, valid, tri_ref[...])[1] for z in zs], valid, vcat

        def finish(zs, c2s, valid, vcat):
            ws = []
            for h in range(2):
                ws.append(_sb_weights(zs[h], c2s[h], valid, run_ref[h]).astype(BF16))
                run_ref[h] += c2s[h][:, tk:]
            acc_ref[...] += _dot(jnp.concatenate(ws, axis=1), vcat)

        def first_tiles(below):
            pres = [sums(n_full + dd, dd) for dd in reversed(range(ndiag))]
            pres += [sums(n_full - 1 - n, None) for n in range(below)]
            for pre in pres:
                finish(*pre)

        @pl.when(qi == 0)
        def _():
            first_tiles(0)

        @pl.when(qi > 0)
        def _():
            first_tiles(2)

        def two_tiles(carry):
            it, _ = carry
            kb = n_full - 1 - 2 * it
            first, second = sums(kb, None), sums(kb - 1, None)
            finish(*first)
            finish(*second)
            return it + 1, _sb_rest_is_zero(run_ref, bound)

        lax.while_loop(lambda c: jnp.logical_and(c[0] < n_full // 2, jnp.logical_not(c[1])), two_tiles,
                       (jnp.minimum(qi, 1), _sb_rest_is_zero(run_ref, bound)))
        o_ref[...] = acc_ref[...]

    blk = pl.BlockSpec((tq, LANES), lambda p, i: (i, p))
    full = pl.BlockSpec((s, LANES), lambda p, i: (0, p))
    (o,), moved = _call_with_exchange(
        body, exchange, grid=(npairs, nq),
        in_specs=[blk, full, full, pl.BlockSpec((2 * tk, 2 * tk), lambda p, i: (0, 0))],
        out_specs=[blk], out_shape=[jax.ShapeDtypeStruct((s, d), F32)],
        scratch_shapes=[pltpu.VMEM((tq, LANES), F32), pltpu.VMEM((2, tq, tk), F32), pltpu.VMEM((8, LANES), F32)],
        name="sb_attn_fwd", args=(qn, kn, vb, tri_ge2))
    return o, moved


def _sb_attn_bwd(qn, kn, vb, do, exchange=None):
    s, d = qn.shape
    tk = SB_TK
    tq = _pick(s, 256)
    nq, ndiag = s // tq, tq // tk
    assert tq % (2 * tk) == 0, "tiles below the diagonal are taken two at a time"
    npairs = d // LANES
    scale = 1.0 / math.sqrt(SB_HEAD_DIM)
    tri_ge2, tri_le2 = _sb_consts(tk)

    def body(q_ref, k_ref, v_ref, do_ref, tge_ref, tle_ref, dq_ref, dk_ref, dv_ref,
             g_cache, s_cache, run_ref, dq_acc, kmax_ref):
        qi = pl.program_id(1)

        @pl.when(qi == 0)
        def _():
            dk_ref[...] = jnp.zeros_like(dk_ref)
            dv_ref[...] = jnp.zeros_like(dv_ref)
            kmax_ref[...] = jnp.full(kmax_ref.shape, _max_row_norm(k_ref[...]), F32)

        low_half = lax.broadcasted_iota(jnp.int32, (tk, LANES), 1) < SB_HEAD_DIM
        row = lax.broadcasted_iota(jnp.int32, (tq, tk), 0)
        col = lax.broadcasted_iota(jnp.int32, (tq, tk), 1)
        qs = (q_ref[...].astype(F32) * scale).astype(BF16)
        bound = _sb_score_bound(qs, kmax_ref)
        dob = do_ref[...].astype(BF16)
        n_full = qi * ndiag

        def a_sums(kb, dd):
            koff = pl.multiple_of(kb * tk, tk)
            kcat = _head_stack(k_ref[pl.ds(koff, tk), :], low_half)
            vcat = _head_stack(v_ref[pl.ds(koff, tk), :], low_half)
            z2 = _dot(qs, kcat, NT)
            dw2 = _dot(dob, vcat, NT)
            valid = None if dd is None else row > col + dd * tk
            c2s = []
            for h in range(2):
                cols = slice(h * tk, (h + 1) * tk)
                z = z2[:, cols]
                e, c2 = _sb_tile_sums(z, valid, tge_ref[...])
                s_cache[kb, :, cols] = jnp.where(z >= 0, 1.0, e) / (1.0 + e)
                c2s.append(c2)
            return kb, koff, z2, dw2, c2s, valid

        def a_finish(kb, koff, z2, dw2, c2s, valid):
            ws = []
            for h in range(2):
                cols = slice(h * tk, (h + 1) * tk)
                w = _sb_weights(z2[:, cols], c2s[h], valid, run_ref[h])
                run_ref[h] += c2s[h][:, tk:]
                g_cache[kb, :, cols] = w * dw2[:, cols]
                ws.append(w.astype(BF16))
            dv2 = _dot(jnp.concatenate(ws, axis=1), dob, TN)
            dv_ref[pl.ds(koff, tk), :] += jnp.where(low_half, dv2[:tk], dv2[tk:])

        def b_sums(kb, dd):
            gs = [g_cache[kb, :, h * tk:(h + 1) * tk] for h in range(2)]
            p2s = [_dot(jnp.concatenate(_split2(g), axis=1), tle_ref[...]) for g in gs]
            return kb, gs, p2s, (None if dd is None else row > col + dd * tk)

        def b_finish(kb, gs, p2s, valid):
            koff = pl.multiple_of(kb * tk, tk)
            dzs = []
            for h in range(2):
                dz = gs[h] - s_cache[kb, :, h * tk:(h + 1) * tk] * (p2s[h][:, :tk] + run_ref[h])
                if valid is not None:
                    dz = jnp.where(valid, dz, 0.0)
                run_ref[h] += p2s[h][:, tk:]
                dzs.append(dz.astype(BF16))
            dzcat = jnp.concatenate(dzs, axis=1)
            dq_acc[...] += _dot(dzcat, _head_stack(k_ref[pl.ds(koff, tk), :], low_half))
            dk2 = _dot(dzcat, qs, TN)
            dk_ref[pl.ds(koff, tk), :] += jnp.where(low_half, dk2[:tk], dk2[tk:])

        run_ref[...] = jnp.zeros_like(run_ref)
        near = jnp.minimum(qi, 1)

        def a_first_tiles(below):
            pres = [a_sums(n_full + dd, dd) for dd in reversed(range(ndiag))]
            pres += [a_sums(n_full - 1 - n, None) for n in range(below)]
            for pre in pres:
                a_finish(*pre)

        @pl.when(qi == 0)
        def _():
            a_first_tiles(0)

        @pl.when(qi > 0)
        def _():
            a_first_tiles(2)

        def two_a(carry):
            it, _ = carry
            kb = n_full - 1 - 2 * it
            first, second = a_sums(kb, None), a_sums(kb - 1, None)
            a_finish(*first)
            a_finish(*second)
            return it + 1, _sb_rest_is_zero(run_ref, bound)

        trips, _ = lax.while_loop(lambda c: jnp.logical_and(c[0] < n_full // 2, jnp.logical_not(c[1])), two_a,
                                  (near, _sb_rest_is_zero(run_ref, bound)))

        run_ref[...] = jnp.zeros_like(run_ref)
        dq_acc[...] = jnp.zeros_like(dq_acc)
        kb_first = n_full - 2 * trips

        def two_b(it, carry):
            first, second = b_sums(kb_first + 2 * it, None), b_sums(kb_first + 2 * it + 1, None)
            b_finish(*first)
            b_finish(*second)
            return carry

        lax.fori_loop(0, trips - near, two_b, 0)

        def b_last_tiles(below):
            pres = [b_sums(n_full - below + n, None) for n in range(below)]
            pres += [b_sums(n_full + dd, dd) for dd in range(ndiag)]
            for pre in pres:
                b_finish(*pre)

        @pl.when(qi == 0)
        def _():
            b_last_tiles(0)

        @pl.when(qi > 0)
        def _():
            b_last_tiles(2)

        dq_ref[...] = dq_acc[...] * scale

    blk = pl.BlockSpec((tq, LANES), lambda p, i: (i, p))
    full = pl.BlockSpec((s, LANES), lambda p, i: (0, p))
    tri = pl.BlockSpec((2 * tk, 2 * tk), lambda p, i: (0, 0))
    return _call_with_exchange(
        body, exchange, grid=(npairs, nq),
        in_specs=[blk, full, full, blk, tri, tri],
        out_specs=[blk, full, full],
        out_shape=[jax.ShapeDtypeStruct((s, d), F32)] * 3,
        scratch_shapes=[pltpu.VMEM((s // tk, tq, 2 * tk), F32), pltpu.VMEM((s // tk, tq, 2 * tk), F32),
                        pltpu.VMEM((2, tq, tk), F32), pltpu.VMEM((tq, LANES), F32), pltpu.VMEM((8, LANES), F32)],
        name="sb_attn_bwd", args=(qn, kn, vb, do, tri_ge2, tri_le2))


def _hg_consts(c):
    levels = []
    h = c // 2
    while h >= 1:
        levels.append(h)
        h //= 2
    t = np.arange(c)
    j = t[None, :]
    rows, masks = [], []
    for h in levels:
        blk = t // (2 * h)
        mid = blk * 2 * h + h - 1
        second = (t % (2 * h)) >= h
        rows.append(second[:, None] & (j > mid[:, None]) & (j <= t[:, None]))
        rows.append((~second)[:, None] & (j > t[:, None]) & (j <= mid[:, None]))
        masks.append((blk[:, None] == blk[None, :]) & second[:, None] & (~second)[None, :])
    rows.append(j <= t[:, None])
    rows.append(j > t[:, None])
    masks.append(t[:, None] == t[None, :])
    m_all = np.concatenate(rows, axis=0).astype(np.float32)
    mask_all = np.stack(masks, axis=0).astype(np.float32)
    suffix = (t[None, :] >= t[:, None]).astype(np.float32)
    return len(levels), jnp.asarray(m_all, BF16), jnp.asarray(mask_all, F32), jnp.asarray(suffix, BF16)


def _split3(x):
    hi = x.astype(BF16)
    r1 = x - hi.astype(F32)
    mid = r1.astype(BF16)
    lo = (r1 - mid.astype(F32)).astype(BF16)
    return jnp.concatenate([hi, mid, lo], axis=1)


def _join3(e):
    n = e.shape[1] // 3
    return e[:, :n] + e[:, n:2 * n] + e[:, 2 * n:]


def _hg_gates(qr, fr, lb):
    sq = _sigmoid(qr)
    sf = _sigmoid(fr)
    forget = lb + (1.0 - lb) * sf
    return qr * sq, sq, sf, forget, jnp.log(forget), 1.0 - forget


def _hg_scores(q, k, expo, masks, nlev, c):
    qb, kb = q.astype(BF16), k.astype(BF16)
    a = masks[nlev] * _dot(qb, kb, NT)
    scaled = []
    for li in range(nlev):
        fq = jnp.exp(expo[(2 * li) * c:(2 * li + 1) * c])
        fk = jnp.exp(expo[(2 * li + 1) * c:(2 * li + 2) * c])
        qs, ks = (q * fq).astype(BF16), (k * fk).astype(BF16)
        a = a + masks[li] * _dot(qs, ks, NT)
        scaled.append((qs, ks, fq, fk))
    return a, scaled, qb, kb


def _hg_heads_per_step(nh):
    return 2 if nh % 2 == 0 else 1


def _hg_fwd(proj, lb_row, gain_row, exchange=None):
    s, d4 = proj.shape
    d = d4 // 4
    nh = d // HG_HEAD_DIM
    c = min(HG_CHUNK, s)
    tb = _pick(s, 512)
    ncb = tb // c
    nlev, m_all, mask_all, _ = _hg_consts(c)
    nrow = m_all.shape[0]

    hp = _hg_heads_per_step(nh)
    wide = hp * HG_HEAD_DIM

    def body(q_ref, f_ref, i_ref, g_ref, lb_ref, gain_ref, mall_ref, mask_ref, y_ref, o_ref, st_out_ref, st_ref):
        b = pl.program_id(1)

        @pl.when(b == 0)
        def _():
            st_ref[...] = jnp.zeros_like(st_ref)

        gain = gain_ref[...]

        def inside(ci, hh):
            rows = pl.ds(pl.multiple_of(ci * c, c), c)
            cols = slice(hh * HG_HEAD_DIM, (hh + 1) * HG_HEAD_DIM)
            q, _, _, _, lf, k = _hg_gates(q_ref[rows, cols], f_ref[rows, cols], lb_ref[:, cols])
            v = i_ref[rows, cols].astype(BF16)
            expo = _join3(_dot(mall_ref[...], _split3(lf)))
            a, _, _, _ = _hg_scores(q, k, expo, mask_ref[...], nlev, c)
            b_cum = expo[2 * nlev * c:(2 * nlev + 1) * c]
            e_tail = expo[(2 * nlev + 1) * c:(2 * nlev + 2) * c]
            q_in = (q * jnp.exp(b_cum)).astype(BF16)
            k_dec = (k * jnp.exp(e_tail)).astype(BF16)
            return ci, hh, rows, cols, q_in, _dot(a.astype(BF16), v), jnp.exp(b_cum[c - 1:c, :]), _dot(v, k_dec, TN)

        def across(ci, hh, rows, cols, q_in, o_intra, decay, kv):
            st = st_ref[hh]
            st_out_ref[ci, hh] = st
            o = _dot(q_in, st.astype(BF16), NT) + o_intra
            st_ref[hh] = st * decay + kv
            o_ref[rows, cols] = o
            r = lax.rsqrt(jnp.mean(o * o, axis=-1, keepdims=True) + NORM_EPS)
            y_ref[rows, cols] = (o * r * gain * _sigmoid(g_ref[rows, cols])).astype(y_ref.dtype)

        per_trip = 2 if ncb % 2 == 0 else 1

        def trip(it, carry):
            ready = [inside(per_trip * it + n, hh) for n in range(per_trip) for hh in range(hp)]
            for r in ready:
                across(*r)
            return carry

        lax.fori_loop(0, ncb // per_trip, trip, 0)

    part = lambda k: pl.BlockSpec((tb, wide), lambda h, b: (b, k * (nh // hp) + h))
    head_row = pl.BlockSpec((1, wide), lambda h, b: (0, h))
    tok = pl.BlockSpec((tb, wide), lambda h, b: (b, h))
    return _call_with_exchange(
        body, exchange, grid=(nh // hp, s // tb),
        in_specs=[part(0), part(1), part(2), part(3), head_row,
                  pl.BlockSpec((1, HG_HEAD_DIM), lambda h, b: (0, 0)),
                  pl.BlockSpec((nrow, c), lambda h, b: (0, 0)),
                  pl.BlockSpec((nlev + 1, c, c), lambda h, b: (0, 0, 0))],
        out_specs=[tok, tok, pl.BlockSpec((ncb, hp, HG_HEAD_DIM, HG_HEAD_DIM), lambda h, b: (b, h, 0, 0))],
        out_shape=[jax.ShapeDtypeStruct((s, d), BF16), jax.ShapeDtypeStruct((s, d), F32),
                   jax.ShapeDtypeStruct((s // c, nh, HG_HEAD_DIM, HG_HEAD_DIM), F32)],
        scratch_shapes=[pltpu.VMEM((hp, HG_HEAD_DIM, HG_HEAD_DIM), F32)],
        name="hg_fwd", args=(proj, proj, proj, proj, lb_row, gain_row, m_all, mask_all))


def _hg_bwd(proj, lb_row, gain_row, o_saved, states, dy, exchange=None):
    s, d4 = proj.shape
    d = d4 // 4
    nh = d // HG_HEAD_DIM
    c = min(HG_CHUNK, s)
    tb = _pick(s, 512)
    ncb = tb // c
    nb = s // tb
    nlev, m_all, mask_all, suffix = _hg_consts(c)
    nrow = m_all.shape[0]
    hp = _hg_heads_per_step(nh)
    wide = hp * HG_HEAD_DIM

    def body(q_ref, f_ref, i_ref, g_ref, lb_ref, gain_ref, o_ref, st_in_ref, dy_ref, mall_ref, mask_ref, suf_ref,
             dproj_ref, dlb_ref, dgain_ref, dst_ref, run_ref):
        b = pl.program_id(1)

        @pl.when(b == 0)
        def _():
            dst_ref[...] = jnp.zeros_like(dst_ref)
            run_ref[...] = jnp.zeros_like(run_ref)
            dlb_ref[...] = jnp.zeros_like(dlb_ref)
            dgain_ref[...] = jnp.zeros_like(dgain_ref)

        gain = gain_ref[...]

        def head_chunk(ci, rows, hh, cols):
            lb = lb_ref[:, cols]
            qr, fr = q_ref[rows, cols], f_ref[rows, cols]
            q, sq, sf, forget, lf, k = _hg_gates(qr, fr, lb)
            v = i_ref[rows, cols].astype(BF16)
            expo = _join3(_dot(mall_ref[...], _split3(lf)))
            masks = mask_ref[...]
            o = o_ref[rows, cols]
            dyv = dy_ref[rows, cols]
            sg = _sigmoid(g_ref[rows, cols])
            r = lax.rsqrt(jnp.mean(o * o, axis=-1, keepdims=True) + NORM_EPS)
            ohat = o * r
            dyn = dyv * sg
            dproj_ref[3, rows, cols] = (dyv * ohat * gain * sg * (1.0 - sg)).astype(dproj_ref.dtype)
            dgain_ref[:, cols] += jnp.sum(dyn * ohat, axis=0, keepdims=True)
            dohat = dyn * gain
            do = (r * (dohat - ohat * jnp.mean(dohat * ohat, axis=-1, keepdims=True))).astype(BF16)
            dst = dst_ref[hh]
            dstb = dst.astype(BF16)
            qb, kb = q.astype(BF16), k.astype(BF16)
            f_cum = jnp.exp(expo[2 * nlev * c:(2 * nlev + 1) * c])
            f_tail = jnp.exp(expo[(2 * nlev + 1) * c:(2 * nlev + 2) * c])
            q_in = (q * f_cum).astype(BF16)
            k_dec = (k * f_tail).astype(BF16)
            t_in = _join3(_dot(do, _split3(st_in_ref[ci, hh])))
            t_st = _join3(_dot(v, _split3(dst)))
            da = _dot(do, v, NT)
            dam = (masks[nlev] * da).astype(BF16)
            a = masks[nlev] * _dot(qb, kb, NT)
            dq = t_in * f_cum + _dot(dam, kb)
            dk = t_st * f_tail + _dot(dam, qb, TN)
            db = q_in.astype(F32) * t_in - k_dec.astype(F32) * t_st
            for li in range(nlev):
                fq = jnp.exp(expo[(2 * li) * c:(2 * li + 1) * c])
                fk = jnp.exp(expo[(2 * li + 1) * c:(2 * li + 2) * c])
                qs, ks = (q * fq).astype(BF16), (k * fk).astype(BF16)
                a = a + masks[li] * _dot(qs, ks, NT)
                dam = (masks[li] * da).astype(BF16)
                t_q = _dot(dam, ks)
                t_k = _dot(dam, qs, TN)
                dq = dq + t_q * fq
                dk = dk + t_k * fk
                db = db + (qs.astype(F32) * t_q - ks.astype(F32) * t_k)
            dv = _dot(a.astype(BF16), do, TN) + _dot(k_dec, dstb, NT)
            dst_ref[hh] = dst * f_cum[c - 1:c, :] + _dot(do, q_in, TN)
            dlf = _join3(_dot(suf_ref[...], _split3(db))) + run_ref[hh]
            run_ref[hh] = dlf[0:1, :]
            dforget = dlf / forget - dk
            dlb_ref[:, cols] += jnp.sum(dforget * (1.0 - sf), axis=0, keepdims=True)
            dproj_ref[1, rows, cols] = (dforget * (1.0 - lb) * sf * (1.0 - sf)).astype(dproj_ref.dtype)
            dproj_ref[0, rows, cols] = (dq * sq * (1.0 + qr * (1.0 - sq))).astype(dproj_ref.dtype)
            dproj_ref[2, rows, cols] = dv.astype(dproj_ref.dtype)

        def chunk(it, carry):
            ci = ncb - 1 - it
            rows = pl.ds(pl.multiple_of(ci * c, c), c)
            for hh in range(hp):
                head_chunk(ci, rows, hh, slice(hh * HG_HEAD_DIM, (hh + 1) * HG_HEAD_DIM))
            return carry

        lax.fori_loop(0, ncb, chunk, 0)

    part = lambda k: pl.BlockSpec((tb, wide), lambda h, b: (nb - 1 - b, k * (nh // hp) + h))
    head_row = pl.BlockSpec((1, wide), lambda h, b: (0, h))
    tok = pl.BlockSpec((tb, wide), lambda h, b: (nb - 1 - b, h))
    const2 = lambda shape: pl.BlockSpec(shape, lambda h, b: (0, 0))
    return _call_with_exchange(
        body, exchange, grid=(nh // hp, nb),
        in_specs=[part(0), part(1), part(2), part(3), head_row, const2((1, HG_HEAD_DIM)), tok,
                  pl.BlockSpec((ncb, hp, HG_HEAD_DIM, HG_HEAD_DIM), lambda h, b: (nb - 1 - b, h, 0, 0)),
                  tok, const2((nrow, c)), pl.BlockSpec((nlev + 1, c, c), lambda h, b: (0, 0, 0)), const2((c, c))],
        out_specs=[pl.BlockSpec((4, tb, wide), lambda h, b: (0, nb - 1 - b, h)), head_row, head_row],
        out_shape=[jax.ShapeDtypeStruct((4, s, d), BF16)] + [jax.ShapeDtypeStruct((1, d), F32)] * 2,
        scratch_shapes=[pltpu.VMEM((hp, HG_HEAD_DIM, HG_HEAD_DIM), F32), pltpu.VMEM((hp, 1, HG_HEAD_DIM), F32)],
        name="hg_bwd", args=(proj, proj, proj, proj, lb_row, gain_row, o_saved, states, dy, m_all, mask_all, suffix))


def _lb_fwd(logits):
    n, d = logits.shape

    def body(l_ref, lb_ref, p_ref):
        rows = [l_ref[i:i + 1, :] for i in range(n)]
        m = functools.reduce(jnp.maximum, rows)
        es = [jnp.exp(r - m) for r in rows]
        tot = functools.reduce(lambda a, b: a + b, es)
        ps = [e / tot for e in es]
        run = jnp.zeros_like(ps[0])
        for i in range(n):
            run = run + ps[i]
            lb_ref[i:i + 1, :] = run - ps[0]
            p_ref[i:i + 1, :] = ps[i]

    return pl.pallas_call(
        body, out_shape=[jax.ShapeDtypeStruct((n, d), F32)] * 2, name="lb_fwd",
    )(logits)


def _lb_bwd(p, dlb):
    n, d = p.shape

    def body(p_ref, dlb_ref, dl_ref):
        ps = [p_ref[i:i + 1, :] for i in range(n)]
        ds = [dlb_ref[i:i + 1, :] for i in range(n)]
        total = functools.reduce(lambda a, b: a + b, ds)
        dps = []
        for i in range(n):
            dp = functools.reduce(lambda a, b: a + b, ds[i:])
            dps.append(dp - total if i == 0 else dp)
        inner = functools.reduce(lambda a, b: a + b, [pi * di for pi, di in zip(ps, dps)])
        for i in range(n):
            dl_ref[i:i + 1, :] = ps[i] * (dps[i] - inner)

    return pl.pallas_call(body, out_shape=jax.ShapeDtypeStruct((n, d), F32), name="lb_bwd")(p, dlb)


def _as2d(a):
    return a.reshape(-1, a.shape[-1])


def _adamw(w, m, v, grads, exchange=None):
    shape = w.shape
    w2, m2, v2 = _as2d(w), _as2d(m), _as2d(v)
    g2 = [_as2d(g) for g in grads]
    rows, cols = w2.shape
    tr = _pick(rows, 512)
    ng = len(g2)
    bc1 = 1.0 - ADAM_B1 ** ADAM_STEP
    bc2 = 1.0 - ADAM_B2 ** ADAM_STEP

    def body(w_ref, m_ref, v_ref, *rest):
        g = rest[0][...]
        for extra in rest[1:ng]:
            g = g + extra[...]
        g_out, d_out, m_out, v_out = rest[ng:]
        mn = ADAM_B1 * m_ref[...] + (1.0 - ADAM_B1) * g
        vn = ADAM_B2 * v_ref[...] + (1.0 - ADAM_B2) * (g * g)
        m_hat = mn / bc1
        v_hat = vn / bc2
        g_out[...] = g
        d_out[...] = -ADAM_LR * (m_hat / (jnp.sqrt(v_hat) + ADAM_EPS) + ADAM_WD * w_ref[...])
        m_out[...] = mn
        v_out[...] = vn

    spec = pl.BlockSpec((tr, cols), lambda i: (i, 0))
    outs, moved = _call_with_exchange(
        body, exchange, grid=(rows // tr,), in_specs=[spec] * (3 + ng), out_specs=[spec] * 4,
        out_shape=[jax.ShapeDtypeStruct((rows, cols), F32)] * 4, scratch_shapes=[], name="adamw",
        args=(w2, m2, v2, *g2))
    result = tuple(o.reshape(shape) for o in outs)
    return result if exchange is None else (result, moved)


def _sum_slots(parts, recv, chip, into, index):
    _, rows, cols = parts.shape
    tr = _pick(rows, 512)

    def body(chip_ref, own_ref, r0_ref, r1_ref, r2_ref, into_ref, o_ref):
        f = lambda r: r[...].astype(F32)
        o_ref[...] = ((f(own_ref) + f(r0_ref)) + f(r1_ref)) + f(r2_ref)

    grid_spec = pltpu.PrefetchScalarGridSpec(
        num_scalar_prefetch=1, grid=(rows // tr,),
        in_specs=[pl.BlockSpec((None, tr, cols), lambda i, chip_ref: (chip_ref[0], i, 0))]
        + [pl.BlockSpec((None, tr, cols), functools.partial(lambda i, chip_ref, k: (k, i, 0), k=k)) for k in range(3)]
        + [pl.BlockSpec(memory_space=pl.ANY)],
        out_specs=pl.BlockSpec((None, tr, cols), lambda i, chip_ref: (index, i, 0)))
    return pl.pallas_call(
        body, grid_spec=grid_spec, out_shape=jax.ShapeDtypeStruct(into.shape, F32),
        input_output_aliases={5: 0}, compiler_params=_params(("parallel",)), name="sum_slots",
    )(chip, parts, recv, recv, recv, into)


def _pack_rows(pieces):
    cols = pieces[0].shape[1]
    used = sum(p.shape[0] for p in pieces)
    rows = -(-used // 8) * 8

    def body(*refs):
        out_ref = refs[-1]
        at = 0
        for ref in refs[:-1]:
            out_ref[at:at + ref.shape[0], :] = ref[...]
            at += ref.shape[0]
        if at < rows:
            out_ref[at:rows, :] = jnp.zeros((rows - at, cols), F32)

    return pl.pallas_call(body, out_shape=jax.ShapeDtypeStruct((rows, cols), F32), name="pack_rows")(*pieces)


def _sum_devices(gathered):
    n, rows, cols = gathered.shape

    def body(g_ref, o_ref):
        acc = g_ref[0]
        for i in range(1, n):
            acc = acc + g_ref[i]
        o_ref[...] = acc

    return pl.pallas_call(body, out_shape=jax.ShapeDtypeStruct((rows, cols), F32), name="sum_devices")(gathered)


def _coords():
    return lax.axis_index("x"), lax.axis_index("y"), lax.axis_index("c")


def _chip_peers(x, y, c):
    out = []
    for fx, fy in ((0, 1), (1, 0), (1, 1)):
        px = 1 - x if fx else x
        py = 1 - y if fy else y
        out.append(((px, py, c), 2 * px + py))
    return out


class _ChipExchange:
    def __init__(self, kind, arrays):
        self.kind, self.kinds, self.arrays, self.n = kind, [kind] * len(arrays), list(arrays), len(arrays)
        self._shapes()

    def also(self, kind, arrays):
        self.kinds += [kind] * len(arrays)
        self.arrays += list(arrays)
        self.n = len(self.arrays)
        self._shapes()
        return self

    def _shapes(self):
        lead = {"gather": lambda a: (N_CHIPS,) + a.shape, "scatter": lambda a: (3,) + a.shape[1:],
                "swap": lambda a: a.shape}
        self.out_shape = [jax.ShapeDtypeStruct(lead[k](a), a.dtype) for k, a in zip(self.kinds, self.arrays)]
        for k, a in zip(self.kinds, self.arrays):
            assert k != "gather" or a.shape[0] % 2 == 0, "a gathered array is cut in two along its leading axis"
        self.scratch = [pltpu.SemaphoreType.DMA((6 * self.n,)), pltpu.SemaphoreType.DMA((6 * self.n,)),
                        pltpu.SemaphoreType.DMA((self.n,))]

    def copies(self, ins, outs, send_sems, recv_sems, local_sems):
        x, y, c = _coords()
        me = 2 * x + y
        sibling = (x, y, 1 - c)
        starts, waits, last = [], [], []
        for t, kind in enumerate(self.kinds):
            if kind == "swap":
                cp = pltpu.make_async_remote_copy(
                    src_ref=ins[t], dst_ref=outs[t], send_sem=send_sems.at[6 * t], recv_sem=recv_sems.at[6 * t],
                    device_id=sibling, device_id_type=MESH)
                starts.append(cp.start)
                waits += [cp.wait_send, cp.wait_recv]
                continue
            if kind == "gather":
                own = pltpu.make_async_copy(ins[t], outs[t].at[me], local_sems.at[t])
                starts.append(own.start)
                waits.append(own.wait)
                half_rows = ins[t].shape[0] // 2
                mine = pl.ds(c * half_rows, half_rows)
                theirs = pl.ds((1 - c) * half_rows, half_rows)
            for k, (peer, peer_chip) in enumerate(_chip_peers(x, y, c)):
                sems = dict(send_sem=send_sems.at[6 * t + k], recv_sem=recv_sems.at[6 * t + k],
                            device_id=peer, device_id_type=MESH)
                if kind == "scatter":
                    send = pltpu.make_async_remote_copy(src_ref=ins[t].at[peer_chip], dst_ref=outs[t].at[k], **sems)
                    starts.append(send.start)
                    waits += [send.wait_send, send.wait_recv]
                    continue
                send = pltpu.make_async_remote_copy(
                    src_ref=ins[t].at[mine], dst_ref=outs[t].at[me].at[mine], **sems)
                landed = outs[t].at[peer_chip].at[mine]
                recv = pltpu.make_async_remote_copy(src_ref=ins[t].at[mine], dst_ref=landed, **sems)
                pass_on = pltpu.make_async_remote_copy(
                    src_ref=landed, dst_ref=landed, send_sem=send_sems.at[6 * t + 3 + k],
                    recv_sem=recv_sems.at[6 * t + 3 + k], device_id=sibling, device_id_type=MESH)
                handed = pltpu.make_async_remote_copy(
                    src_ref=landed, dst_ref=outs[t].at[peer_chip].at[theirs], send_sem=send_sems.at[6 * t + 3 + k],
                    recv_sem=recv_sems.at[6 * t + 3 + k], device_id=sibling, device_id_type=MESH)
                starts.append(send.start)
                waits += [recv.wait_recv, pass_on.start]
                last += [send.wait_send, pass_on.wait_send, handed.wait_recv]
        return starts, waits + last

    def run(self, name):
        n = self.n

        def body(*refs):
            starts, waits = self.copies(refs[:n], refs[n:2 * n], *refs[2 * n:])
            for f in starts + waits:
                f()

        return pl.pallas_call(body, in_specs=[HBM_SPEC] * n, out_specs=[HBM_SPEC] * n, out_shape=self.out_shape,
                              scratch_shapes=self.scratch, name=name)(*self.arrays)


def _call_with_exchange(body, exchange, *, grid, in_specs, out_specs, out_shape, scratch_shapes, name, args,
                        sequential=False):
    if exchange is None:
        first_axis = "arbitrary" if sequential else "parallel"
        outs = pl.pallas_call(body, grid=grid, in_specs=in_specs, out_specs=out_specs, out_shape=out_shape,
                              scratch_shapes=scratch_shapes,
                              compiler_params=_params((first_axis,) + ("arbitrary",) * (len(grid) - 1)),
                              name=name)(*args)
        return outs, []
    n_in, n_out, n_scr, n = len(in_specs), len(out_specs), len(scratch_shapes), exchange.n

    def wrapped(*refs):
        ins, ex_in = refs[:n_in], refs[n_in:n_in + n]
        outs = refs[n_in + n:n_in + n + n_out]
        ex_out = refs[n_in + n + n_out:n_in + 2 * n + n_out]
        scr = refs[n_in + 2 * n + n_out:n_in + 2 * n + n_out + n_scr]
        sems = refs[n_in + 2 * n + n_out + n_scr:]
        ids = [pl.program_id(a) for a in range(len(grid))]
        first = functools.reduce(jnp.logical_and, [i == 0 for i in ids])
        last = functools.reduce(jnp.logical_and, [i == g - 1 for i, g in zip(ids, grid)])

        @pl.when(first)
        def _():
            for f in exchange.copies(ex_in, ex_out, *sems)[0]:
                f()

        body(*ins, *outs, *scr)

        @pl.when(last)
        def _():
            for f in exchange.copies(ex_in, ex_out, *sems)[1]:
                f()

    res = pl.pallas_call(
        wrapped, grid=grid, in_specs=list(in_specs) + [HBM_SPEC] * n, out_specs=list(out_specs) + [HBM_SPEC] * n,
        out_shape=list(out_shape) + exchange.out_shape, scratch_shapes=list(scratch_shapes) + exchange.scratch,
        compiler_params=_params(("arbitrary",) * len(grid)), name=name + "_" + exchange.kind,
    )(*args, *exchange.arrays)
    return res[:n_out], res[n_out:]


def _gather_devices(a):
    def body(in_ref, out_ref, send_sems, recv_sems, local_sem):
        x, y, c = _coords()
        me = 4 * x + 2 * y + c
        own = pltpu.make_async_copy(in_ref, out_ref.at[me], local_sem)
        own.start()
        waits = [own.wait]
        for k in range(1, N_DEVICES):
            px = 1 - x if k & 4 else x
            py = 1 - y if k & 2 else y
            pc = 1 - c if k & 1 else c
            peer = (px, py, pc)
            send = pltpu.make_async_remote_copy(
                src_ref=in_ref, dst_ref=out_ref.at[me], send_sem=send_sems.at[k - 1], recv_sem=recv_sems.at[k - 1],
                device_id=peer, device_id_type=MESH)
            send.start()
            recv = pltpu.make_async_remote_copy(
                src_ref=in_ref, dst_ref=out_ref.at[4 * px + 2 * py + pc], send_sem=send_sems.at[k - 1],
                recv_sem=recv_sems.at[k - 1], device_id=peer, device_id_type=MESH)
            waits += [send.wait_send, recv.wait_recv]
        for w in waits:
            w()

    return pl.pallas_call(
        body, in_specs=[HBM_SPEC], out_specs=HBM_SPEC,
        out_shape=jax.ShapeDtypeStruct((N_DEVICES,) + a.shape, a.dtype),
        scratch_shapes=[pltpu.SemaphoreType.DMA((N_DEVICES - 1,)), pltpu.SemaphoreType.DMA((N_DEVICES - 1,)),
                        pltpu.SemaphoreType.DMA],
        name="gather_devices",
    )(a)


def _mlp_grad_epilogue(r, u):
    return r * (2.0 * jnp.maximum(u, 0.0))


def kernel(x, norm_gains, sb_w_qkv, sb_q_gain, sb_k_gain, sb_w_o, hg_w_in, hg_lb_logits, hg_norm_gain, hg_w_o, mlp_w1, mlp_w2, loss_target, m_norm_gains, m_sb_w_qkv, m_sb_q_gain, m_sb_k_gain, m_sb_w_o, m_hg_w_in, m_hg_lb_logits, m_hg_norm_gain, m_hg_w_o, m_mlp_w1, m_mlp_w2, v_norm_gains, v_sb_w_qkv, v_sb_q_gain, v_sb_k_gain, v_sb_w_o, v_hg_w_in, v_hg_lb_logits, v_hg_norm_gain, v_hg_w_o, v_mlp_w1, v_mlp_w2):
    depth = norm_gains.shape[0]
    n_sb, n_hg = sb_w_qkv.shape[0], hg_w_in.shape[0]
    xs, tgt = x[0], loss_target[0]
    s, d = xs.shape
    dq = d // N_CHIPS
    cx, cy, cc = _coords()
    chip = 2 * cx + cy
    chip_arr = jnp.reshape(chip, (1,)).astype(jnp.int32)

    def mixer_weights(layer):
        j = layer // 2
        return (sb_w_qkv[j], sb_w_o[j]) if layer % 2 == 0 else (hg_w_in[j], hg_w_o[j])

    w_in_g, ng_g, lbl_g = _ChipExchange(
        "gather", [mixer_weights(0)[0].astype(BF16), norm_gains, hg_lb_logits]).run("gather_first")
    gains = jnp.transpose(ng_g, (1, 2, 0, 3)).reshape(depth, 2, d)
    logits = jnp.transpose(lbl_g, (1, 0, 2)).reshape(n_hg, d)
    lbs, lb_p = _lb_fwd(logits)
    qg_rows = [jnp.tile(sb_q_gain[j], d // SB_HEAD_DIM)[None] for j in range(n_sb)]
    kg_rows = [jnp.tile(sb_k_gain[j], d // SB_HEAD_DIM)[None] for j in range(n_sb)]

    saved, wts = [], []
    xc = xs
    for layer in range(depth):
        j = layer // 2
        ahead = [mixer_weights(layer)[1], mlp_w1[layer], mlp_w2[layer]]
        if layer + 1 < depth:
            ahead.append(mixer_weights(layer + 1)[0])
        gather = _ChipExchange("gather", [a.astype(BF16) for a in ahead])
        h1 = _rmsnorm_fwd(xc, gains[layer, 0][None])
        if layer % 2 == 0:
            qkv = _mm_fwd_cols(h1, w_in_g, name="sb_qkv")
            qn, kn, vb = _qk_norm_fwd(qkv, qg_rows[j], kg_rows[j])
            o, moved = _sb_attn_fwd(qn, kn, vb, gather)
            x_mid = _mm_fwd_rows(o, moved[0], residual=xc, name="sb_out")
            mix = (qkv, qn, kn, vb, o)
        else:
            proj = _mm_fwd_cols(h1, w_in_g, name="hg_in")
            (y, o, states), moved = _hg_fwd(proj, lbs[j][None], hg_norm_gain[j][None], gather)
            x_mid = _mm_fwd_rows(y, moved[0], residual=xc, name="hg_out")
            mix = (proj, y, o, states)
        w_out_g, w1_g, w2_g = moved[:3]
        h2 = _rmsnorm_fwd(x_mid, gains[layer, 1][None])
        u = _mm_fwd_cols(h2, w1_g, name="mlp_up")
        x_out = _mm_fwd_rows(u, w2_g, residual=x_mid, a_fn=_relu2, name="mlp_down")
        saved.append((xc, h1, mix, x_mid, h2, u))
        wts.append((w_in_g, w_out_g, w1_g, w2_g))
        w_in_g = moved[3] if layer + 1 < depth else None
        xc = x_out

    sq, dx = _loss_head(xc, tgt)
    loss = lax.psum(jnp.sum(sq) * (0.5 / d), ("x", "y", "c"))

    dgains = [[None, None] for _ in range(depth)]
    dqg, dkg = [None] * n_sb, [None] * n_sb
    dhgain, dlb = [None] * n_hg, [None] * n_hg
    grads, received, pending = {}, {}, []

    def ready(key, parts):
        grads[key] = parts
        pending.append(key)

    def scatter_of(keys):
        return _ChipExchange("scatter", [grads[k] for k in keys]) if keys else None

    def sent(keys, moved):
        for k, r in zip(keys, moved):
            received[k] = r
            pending.remove(k)

    def chip_sum(kind, layers):
        total = jnp.zeros((len(layers),) + grads[kind, layers[0]].shape[1:], F32)
        for index, l in enumerate(layers):
            total = _sum_slots(grads[kind, l], received[kind, l], chip_arr, total, index)
        return total

    sb_layers, hg_layers = range(0, depth, 2), range(1, depth, 2)
    tensors = [("in", sb_layers), ("out", sb_layers), ("in", hg_layers), ("out", hg_layers),
               ("w1", range(depth)), ("w2", range(depth))]

    for layer in reversed(range(depth)):
        j = layer // 2
        x_in, h1, mix, x_mid, h2, u = saved[layer]
        w_in_g, w_out_g, w1_g, w2_g = wts[layer]
        du = _mm_bwd_rows(dx, w2_g, name="mlp_down_dx", out_dtype=BF16, epi_fn=_mlp_grad_epilogue, epi_args=(u,))
        ready(("w2", layer), _mm_dw_rows(u, dx, a_fn=_relu2, name="mlp_down_dw"))
        ready(("w1", layer), _mm_dw_cols(h2, du, name="mlp_up_dw"))
        dx, dgains[layer][1] = _mm_bwd_cols_norm(du, w1_g, x_mid, gains[layer, 1][None], dx, name="mlp_up_dx")
        if layer % 2 == 0:
            qkv, qn, kn, vb, o = mix
            do = _mm_bwd_rows(dx, w_out_g, name="sb_out_dx")
            ready(("out", layer), _mm_dw_rows(o, dx, name="sb_out_dw"))
            keys = list(pending)
            (dqn, dkn, dv), moved = _sb_attn_bwd(qn, kn, vb, do, scatter_of(keys))
            sent(keys, moved)
            d_in, dqg[j], dkg[j] = _qk_norm_bwd(qkv, qg_rows[j], kg_rows[j], dqn, dkn, dv)
            ready(("in", layer), _mm_dw_cols(h1, d_in, name="sb_qkv_dw"))
            dx_name = "sb_qkv_dx"
        else:
            proj, y, o, states = mix
            dy = _mm_bwd_rows(dx, w_out_g, name="hg_out_dx")
            ready(("out", layer), _mm_dw_rows(y, dx, name="hg_out_dw"))
            keys = list(pending)
            (d_in, dlb[j], dhgain[j]), moved = _hg_bwd(
                proj, lbs[j][None], hg_norm_gain[j][None], o, states, dy, scatter_of(keys))
            sent(keys, moved)
            ready(("in", layer), _mm_dw_cols(h1, d_in, name="hg_in_dw"))
            dx_name = "hg_in_dx"
        if layer > 0:
            dx, dgains[layer][0] = _mm_bwd_cols_norm(d_in, w_in_g, x_in, gains[layer, 0][None], dx, name=dx_name)
        else:
            keys = list(pending)
            early_sums = [chip_sum(*t) for t in tensors[1:]]
            (dx, dgains[layer][0]), moved = _mm_bwd_cols_norm(
                d_in, w_in_g, x_in, gains[layer, 0][None], dx, name=dx_name,
                exchange=scatter_of(keys).also("swap", early_sums))
            sent(keys, moved[:len(keys)])
            early_other = moved[len(keys):]
    grad_x = dx[None]
    dlogits = _lb_bwd(lb_p, jnp.concatenate(dlb, axis=0))

    big_w = [sb_w_qkv, sb_w_o, hg_w_in, hg_w_o, mlp_w1, mlp_w2]
    big_m = [m_sb_w_qkv, m_sb_w_o, m_hg_w_in, m_hg_w_o, m_mlp_w1, m_mlp_w2]
    big_v = [v_sb_w_qkv, v_sb_w_o, v_hg_w_in, v_hg_w_o, v_mlp_w1, v_mlp_w2]
    late_sum = chip_sum(*tensors[0])
    big_second, late_other = _adamw(big_w[1], big_m[1], big_v[1], [early_sums[0], early_other[0]],
                                    _ChipExchange("swap", [late_sum]))
    big = [_adamw(big_w[0], big_m[0], big_v[0], [late_sum, late_other[0]]), big_second]
    big += [_adamw(w, m, v, [a, b]) for w, m, v, a, b in
            zip(big_w[2:], big_m[2:], big_v[2:], early_sums[1:], early_other[1:])]

    pieces = [r for pair in dgains for r in pair] + [dlogits] + dqg + dkg + dhgain
    small = _sum_devices(_gather_devices(_pack_rows(pieces)))
    my_cols = lambda a: lax.dynamic_slice_in_dim(a, chip * dq, dq, axis=1)
    fold = lambda rows, width: jnp.sum(rows.reshape(rows.shape[0], -1, width), axis=1)
    base = 2 * depth + n_hg
    g_ng = my_cols(small[0:2 * depth]).reshape(norm_gains.shape)
    g_lbl = my_cols(small[2 * depth:base])
    g_qg = fold(small[base:base + n_sb], SB_HEAD_DIM)
    g_kg = fold(small[base + n_sb:base + 2 * n_sb], SB_HEAD_DIM)
    g_hgn = fold(small[base + 2 * n_sb:base + 2 * n_sb + n_hg], HG_HEAD_DIM)
    r_ng = _adamw(norm_gains, m_norm_gains, v_norm_gains, [g_ng])
    r_qg = _adamw(sb_q_gain, m_sb_q_gain, v_sb_q_gain, [g_qg])
    r_kg = _adamw(sb_k_gain, m_sb_k_gain, v_sb_k_gain, [g_kg])
    r_lbl = _adamw(hg_lb_logits, m_hg_lb_logits, v_hg_lb_logits, [g_lbl])
    r_hgn = _adamw(hg_norm_gain, m_hg_norm_gain, v_hg_norm_gain, [g_hgn])

    per_weight = [r_ng, big[0], r_qg, r_kg, big[1], big[2], r_lbl, r_hgn, big[3], big[4], big[5]]
    outs = [loss, grad_x]
    for field in range(4):
        outs += [r[field] for r in per_weight]
    return tuple(outs)
```

```python
import functools
import math

import numpy as np
import jax
import jax.numpy as jnp
from jax import lax
from jax.experimental import pallas as pl
from jax.experimental.pallas import tpu as pltpu

F32 = jnp.float32
BF16 = jnp.bfloat16
GRAD_SLOT_DTYPE = jnp.bfloat16

NORM_EPS = 1e-6
SB_HEAD_DIM = 64
HG_HEAD_DIM = 128
HG_CHUNK = 128
LANES = 128
VMEM_LIMIT_BYTES = 56 * 2 ** 20
N_CHIPS = 4
N_DEVICES = 8

ADAM_LR = 0.001
ADAM_B1 = 0.9
ADAM_B2 = 0.999
ADAM_EPS = 1e-08
ADAM_WD = 0.01
ADAM_STEP = 10

MESH = pl.DeviceIdType.MESH
HBM_SPEC = pl.BlockSpec(memory_space=pltpu.HBM)

NN = (((1,), (0,)), ((), ()))
NT = (((1,), (1,)), ((), ()))
TN = (((0,), (0,)), ((), ()))


def _params(sem=None):
    return pltpu.CompilerParams(dimension_semantics=sem, vmem_limit_bytes=VMEM_LIMIT_BYTES)


def _pick(dim, pref):
    for t in (1024, 768, 512, 384, 256, 128, 64, 32, 16, 8):
        if t <= pref and dim % t == 0:
            return t
    return dim


def _dot(a, b, dims=NN):
    return lax.dot_general(a, b, dims, preferred_element_type=F32)


def _sigmoid(x):
    e = jnp.exp(-jnp.abs(x))
    return jnp.where(x >= 0, 1.0, e) / (1.0 + e)


def _matmul(a, b, *, mode, grid, a_block, a_map, b_block, b_map, o_block, o_map, out_shape, out_dtype, name,
            a_fn=None, epi_fn=None, epi_args=(), epi_row_args=(), col_sums=False, exchange=None):
    nk = grid[2]
    dims = {"nn": NN, "nt": NT, "tn": TN}[mode]
    n_epi = len(epi_args) + len(epi_row_args)
    n_out = 2 if col_sums else 1
    tn = o_block[-1]
    assert not col_sums or grid[1] == 1, "the column sums stay resident only with one tile along N"

    def body(a_ref, b_ref, *rest):
        epi_refs = rest[:n_epi]
        o_ref = rest[n_epi]
        kk = pl.program_id(2)

        def emit(r):
            if epi_fn is not None:
                r = epi_fn(r, *[e[...] for e in epi_refs])
            if col_sums:
                r, row = r
                sums_ref = rest[n_epi + 1]
                first = pl.program_id(0) == 0

                @pl.when(first)
                def _():
                    sums_ref[...] = row

                @pl.when(jnp.logical_not(first))
                def _():
                    sums_ref[...] += row
            o_ref[...] = r.astype(o_ref.dtype)

        av = a_ref[...]
        if a_fn is not None:
            av = a_fn(av)
        part = _dot(av.astype(BF16), b_ref[...].astype(BF16), dims)
        if nk == 1:
            emit(part)
            return
        acc_ref = rest[n_epi + n_out]

        @pl.when(kk == 0)
        def _():
            acc_ref[...] = part

        @pl.when(kk > 0)
        def _():
            acc_ref[...] += part

        @pl.when(kk == nk - 1)
        def _():
            emit(acc_ref[...])

    acc_shape = tuple(d for d in o_block if d is not None)
    row_spec = pl.BlockSpec((1, tn), lambda i, j, kk: (0, j))
    in_specs = [pl.BlockSpec(a_block, a_map), pl.BlockSpec(b_block, b_map)]
    in_specs += [pl.BlockSpec(o_block, o_map) for _ in epi_args] + [row_spec for _ in epi_row_args]
    out_specs, out_shapes = [pl.BlockSpec(o_block, o_map)], [jax.ShapeDtypeStruct(out_shape, out_dtype)]
    if col_sums:
        out_specs, out_shapes = out_specs + [row_spec], out_shapes + [jax.ShapeDtypeStruct((1, out_shape[-1]), F32)]
    outs, moved = _call_with_exchange(
        body, exchange, grid=grid, in_specs=in_specs, out_specs=out_specs, out_shape=out_shapes,
        scratch_shapes=[pltpu.VMEM(acc_shape, F32)] if nk > 1 else [], name=name,
        args=(a, b, *epi_args, *epi_row_args), sequential=col_sums)
    result = tuple(outs) if col_sums else outs[0]
    return result if exchange is None else (result, moved)


def _relu2(u):
    r = jnp.maximum(u, 0.0)
    return r * r


def _add(r, res):
    return r + res


def _mm_fwd_cols(a, wg, *, name):
    s, k = a.shape
    ncs = wg.shape[2]
    tm, tk, tn = _pick(s, 1024), _pick(k, 1024), _pick(ncs, 1024)
    npb = ncs // tn
    return _matmul(a, wg, mode="nn", grid=(s // tm, N_CHIPS * npb, k // tk),
                   a_block=(tm, tk), a_map=lambda i, j, kk: (i, kk),
                   b_block=(None, tk, tn), b_map=lambda i, j, kk: (j // npb, kk, j % npb),
                   o_block=(tm, tn), o_map=lambda i, j, kk: (i, j),
                   out_shape=(s, N_CHIPS * ncs), out_dtype=F32, name=name)


def _rows_joined(wg):
    assert wg.shape[1] % 16 == 0, "joining the leading axes must not cross a tile of 16 rows"
    return wg.reshape(wg.shape[0] * wg.shape[1], wg.shape[2])


def _mm_fwd_rows(a, wg, *, residual, name, a_fn=None):
    s = a.shape[0]
    w = _rows_joined(wg)
    k, n = w.shape
    tm, tk, tn = _pick(s, 1024), _pick(k, 1024), _pick(n, 1024)
    return _matmul(a, w, mode="nn", grid=(s // tm, n // tn, k // tk),
                   a_block=(tm, tk), a_map=lambda i, j, kk: (i, kk),
                   b_block=(tk, tn), b_map=lambda i, j, kk: (kk, j),
                   o_block=(tm, tn), o_map=lambda i, j, kk: (i, j),
                   out_shape=(s, n), out_dtype=F32, name=name, a_fn=a_fn, epi_fn=_add, epi_args=(residual,))


def _rmsnorm_grad(dh, x, dx_res, gain):
    r = lax.rsqrt(jnp.mean(x * x, axis=-1, keepdims=True) + NORM_EPS)
    xhat = x * r
    dxhat = dh * gain
    dx = r * (dxhat - xhat * jnp.mean(dxhat * xhat, axis=-1, keepdims=True))
    return dx_res + dx, jnp.sum(dh * xhat, axis=0, keepdims=True)


def _mm_bwd_cols_norm(dy, wg, x, gain_row, dx_res, *, name, exchange=None):
    by_slot = dy.ndim == 3
    s = dy.shape[1] if by_slot else dy.shape[0]
    kw, ncs = wg.shape[1], wg.shape[2]
    tm = _pick(s, 512)
    halves = 2 if tm % 512 == 0 else 1
    sub = tm // halves

    def body(dy_ref, w_ref, x_ref, res_ref, g_ref, dx_ref, dg_ref):
        total = None
        for r in range(halves):
            rows = slice(r * sub, (r + 1) * sub)
            acc = None
            for q in range(N_CHIPS):
                a = dy_ref[q, rows, :] if by_slot else dy_ref[rows, q * ncs:(q + 1) * ncs]
                part = _dot(a.astype(BF16), w_ref[q], NT)
                acc = part if acc is None else acc + part
            dx, row = _rmsnorm_grad(acc, x_ref[rows, :], res_ref[rows, :], g_ref[...])
            dx_ref[rows, :] = dx
            total = row if total is None else total + row
        first = pl.program_id(0) == 0

        @pl.when(first)
        def _():
            dg_ref[...] = total

        @pl.when(jnp.logical_not(first))
        def _():
            dg_ref[...] += total

    dy_spec = pl.BlockSpec((N_CHIPS, tm, ncs), lambda i: (0, i, 0)) if by_slot else pl.BlockSpec(
        (tm, N_CHIPS * ncs), lambda i: (i, 0))
    tok = pl.BlockSpec((tm, kw), lambda i: (i, 0))
    row_spec = pl.BlockSpec((1, kw), lambda i: (0, 0))
    outs, moved = _call_with_exchange(
        body, exchange, grid=(s // tm,),
        in_specs=[dy_spec, pl.BlockSpec((N_CHIPS, kw, ncs), lambda i: (0, 0, 0)), tok, tok, row_spec],
        out_specs=[tok, row_spec],
        out_shape=[jax.ShapeDtypeStruct((s, kw), F32), jax.ShapeDtypeStruct((1, kw), F32)],
        scratch_shapes=[], name=name, args=(dy, wg, x, dx_res, gain_row), sequential=True)
    return tuple(outs) if exchange is None else (tuple(outs), moved)


def _mm_bwd_rows(dy, wg, *, name, out_dtype=F32, epi_fn=None, epi_args=()):
    s, n = dy.shape
    w = _rows_joined(wg)
    rows = w.shape[0]
    tm, tn, tk = _pick(s, 1024), _pick(rows, 1024), _pick(n, 1024)
    return _matmul(dy, w, mode="nt", grid=(s // tm, rows // tn, n // tk),
                   a_block=(tm, tk), a_map=lambda i, j, kk: (i, kk),
                   b_block=(tn, tk), b_map=lambda i, j, kk: (j, kk),
                   o_block=(tm, tn), o_map=lambda i, j, kk: (i, j),
                   out_shape=(s, rows), out_dtype=out_dtype, name=name, epi_fn=epi_fn, epi_args=epi_args)


def _mm_dw_cols(xa, dy, *, name):
    s, kx = xa.shape
    by_slot = dy.ndim == 3
    ncs = dy.shape[2] if by_slot else dy.shape[1] // N_CHIPS
    tm, tn, tk = _pick(kx, 1024), _pick(ncs, 1024), _pick(s, 1024)
    npb = ncs // tn
    b_block, b_map = ((None, tk, tn), lambda i, j, kk: (j // npb, kk, j % npb)) if by_slot else (
        (tk, tn), lambda i, j, kk: (kk, j))
    return _matmul(xa, dy, mode="tn", grid=(kx // tm, N_CHIPS * npb, s // tk),
                   a_block=(tk, tm), a_map=lambda i, j, kk: (kk, i),
                   b_block=b_block, b_map=b_map,
                   o_block=(None, tm, tn), o_map=lambda i, j, kk: (j // npb, i, j % npb),
                   out_shape=(N_CHIPS, kx, ncs), out_dtype=GRAD_SLOT_DTYPE, name=name)


def _mm_dw_rows(xa, dy, *, name, a_fn=None):
    s, n = dy.shape
    rows = xa.shape[1]
    assert (rows // N_CHIPS) % 16 == 0, "splitting the rows into slots must not cut a tile of 16 rows"
    tm, tn, tk = _pick(rows, 1024), _pick(n, 1024), _pick(s, 1024)
    dw = _matmul(xa, dy, mode="tn", grid=(rows // tm, n // tn, s // tk),
                 a_block=(tk, tm), a_map=lambda i, j, kk: (kk, i),
                 b_block=(tk, tn), b_map=lambda i, j, kk: (kk, j),
                 o_block=(tm, tn), o_map=lambda i, j, kk: (i, j),
                 out_shape=(rows, n), out_dtype=GRAD_SLOT_DTYPE, name=name, a_fn=a_fn)
    return dw.reshape(N_CHIPS, rows // N_CHIPS, n)


def _rmsnorm_fwd(x, gain_row):
    s, d = x.shape
    ts = _pick(s, 1024)

    def body(x_ref, g_ref, h_ref):
        xv = x_ref[...]
        r = lax.rsqrt(jnp.mean(xv * xv, axis=-1, keepdims=True) + NORM_EPS)
        h_ref[...] = (xv * r * g_ref[...]).astype(h_ref.dtype)

    return pl.pallas_call(
        body, grid=(s // ts,),
        in_specs=[pl.BlockSpec((ts, d), lambda i: (i, 0)), pl.BlockSpec((1, d), lambda i: (0, 0))],
        out_specs=pl.BlockSpec((ts, d), lambda i: (i, 0)),
        out_shape=jax.ShapeDtypeStruct((s, d), BF16),
        compiler_params=_params(("parallel",)), name="rmsnorm_fwd",
    )(x, gain_row)


def _loss_head(y, target):
    s, d = y.shape
    ts = _pick(s, 512)

    def body(y_ref, t_ref, sq_ref, dy_ref):
        i = pl.program_id(0)
        err = y_ref[...] - t_ref[...]
        dy_ref[...] = err / d
        part = jnp.sum(err * err, axis=0, keepdims=True)

        @pl.when(i == 0)
        def _():
            sq_ref[...] = part

        @pl.when(i > 0)
        def _():
            sq_ref[...] += part

    return pl.pallas_call(
        body, grid=(s // ts,),
        in_specs=[pl.BlockSpec((ts, d), lambda i: (i, 0)), pl.BlockSpec((ts, d), lambda i: (i, 0))],
        out_specs=[pl.BlockSpec((1, d), lambda i: (0, 0)), pl.BlockSpec((ts, d), lambda i: (i, 0))],
        out_shape=[jax.ShapeDtypeStruct((1, d), F32), jax.ShapeDtypeStruct((s, d), F32)],
        compiler_params=_params(("arbitrary",)), name="loss_head",
    )(y, target)


def _pair_ones():
    lane = np.arange(LANES)
    same_half = (lane[:, None] // SB_HEAD_DIM == lane[None, :] // SB_HEAD_DIM).astype(np.float32)
    return jnp.asarray(np.concatenate([same_half, same_half], axis=0), BF16)


def _pair_mean(val, pair_ones):
    return _dot(jnp.concatenate(_split2(val), axis=1), pair_ones) * (1.0 / SB_HEAD_DIM)


def _pair_mean_lanes(val, low_half):
    s0 = jnp.sum(jnp.where(low_half, val, 0.0), axis=-1, keepdims=True)
    s1 = jnp.sum(jnp.where(low_half, 0.0, val), axis=-1, keepdims=True)
    return jnp.where(low_half, s0, s1) * (1.0 / SB_HEAD_DIM)


def _qk_norm_fwd(qkv, qgain_row, kgain_row):
    s, d3 = qkv.shape
    d = d3 // 3
    ts = _pick(s, 512)
    groups = d // LANES

    def body(q_ref, k_ref, v_ref, qg_ref, kg_ref, ones_ref, qn_ref, kn_ref, vb_ref):
        for src, gain, dst in ((q_ref, qg_ref, qn_ref), (k_ref, kg_ref, kn_ref)):
            for p in range(groups):
                cols = slice(p * LANES, (p + 1) * LANES)
                xp = src[:, cols]
                r = lax.rsqrt(_pair_mean(xp * xp, ones_ref[...]) + NORM_EPS)
                dst[:, cols] = (xp * r * gain[:, cols]).astype(dst.dtype)
        vb_ref[...] = v_ref[...].astype(vb_ref.dtype)

    tok = lambda c: pl.BlockSpec((ts, d), lambda i: (i, c))
    row = pl.BlockSpec((1, d), lambda i: (0, 0))
    return pl.pallas_call(
        body, grid=(s // ts,),
        in_specs=[tok(0), tok(1), tok(2), row, row, pl.BlockSpec((2 * LANES, LANES), lambda i: (0, 0))],
        out_specs=[tok(0), tok(0), tok(0)],
        out_shape=[jax.ShapeDtypeStruct((s, d), BF16)] * 3,
        compiler_params=_params(("parallel",)), name="qk_norm_fwd",
    )(qkv, qkv, qkv, qgain_row, kgain_row, _pair_ones())


def _qk_norm_bwd(qkv, qgain_row, kgain_row, dqn, dkn, dv):
    s, d3 = qkv.shape
    d = d3 // 3
    ts = _pick(s, 512)
    groups = d // LANES

    def body(q_ref, k_ref, qg_ref, kg_ref, dqn_ref, dkn_ref, dv_ref, dqkv_ref, dqg_ref, dkg_ref):
        i = pl.program_id(0)
        low_half = lax.broadcasted_iota(jnp.int32, (ts, LANES), 1) < SB_HEAD_DIM
        for which, (src, gain, dsrc, dgain) in enumerate(((q_ref, qg_ref, dqn_ref, dqg_ref),
                                                          (k_ref, kg_ref, dkn_ref, dkg_ref))):
            for p in range(groups):
                cols = slice(p * LANES, (p + 1) * LANES)
                xp = src[:, cols]
                r = lax.rsqrt(_pair_mean_lanes(xp * xp, low_half) + NORM_EPS)
                xhat = xp * r
                dy = dsrc[:, cols]
                dxhat = dy * gain[:, cols]
                dx = r * (dxhat - xhat * _pair_mean_lanes(dxhat * xhat, low_half))
                dqkv_ref[:, which * d + p * LANES: which * d + (p + 1) * LANES] = dx.astype(dqkv_ref.dtype)
                part = jnp.sum(dy * xhat, axis=0, keepdims=True)

                @pl.when(i == 0)
                def _():
                    dgain[:, cols] = part

                @pl.when(i > 0)
                def _():
                    dgain[:, cols] += part
        dqkv_ref[:, 2 * d:] = dv_ref[...].astype(dqkv_ref.dtype)

    tok = lambda c: pl.BlockSpec((ts, d), lambda i: (i, c))
    row = pl.BlockSpec((1, d), lambda i: (0, 0))
    return pl.pallas_call(
        body, grid=(s // ts,),
        in_specs=[tok(0), tok(1), row, row, tok(0), tok(0), tok(0)],
        out_specs=[pl.BlockSpec((ts, d3), lambda i: (i, 0)), row, row],
        out_shape=[jax.ShapeDtypeStruct((s, d3), BF16), jax.ShapeDtypeStruct((1, d), F32),
                   jax.ShapeDtypeStruct((1, d), F32)],
        compiler_params=_params(("arbitrary",)), name="qk_norm_bwd",
    )(qkv, qkv, qgain_row, kgain_row, dqn, dkn, dv)


def _split2(x):
    hi = x.astype(BF16)
    lo = (x - hi.astype(F32)).astype(BF16)
    return hi, lo


SB_TK = 128


def _sb_consts(tk):
    j = np.arange(tk)
    ones = np.ones((tk, tk), np.float32)
    out = []
    for tri in ((j[:, None] >= j[None, :]), (j[:, None] <= j[None, :])):
        half = np.concatenate([tri.astype(np.float32), ones], axis=1)
        out.append(jnp.asarray(np.concatenate([half, half], axis=0), BF16))
    return out


def _head_stack(blk, low_half):
    f = blk.astype(F32)
    return jnp.concatenate([jnp.where(low_half, f, 0.0), jnp.where(low_half, 0.0, f)], axis=0).astype(BF16)


def _sb_tile_sums(z, valid, tri2):
    e = jnp.exp(-jnp.abs(z))
    lstay = jnp.minimum(-z, 0.0) - jnp.log(1.0 + e)
    if valid is not None:
        lstay = jnp.where(valid, lstay, 0.0)
    hi, lo = _split2(lstay)
    return e, _dot(jnp.concatenate([hi, lo], axis=1), tri2)


def _sb_weights(z, c2, valid, run):
    w = jnp.exp(z + c2[:, :SB_TK] + run)
    return w if valid is None else jnp.where(valid, w, 0.0)


EXP_IS_ZERO_BELOW = -110.0


def _max_row_norm(x):
    f = x.astype(F32)
    return jnp.sqrt(jnp.max(jnp.sum(f * f, axis=-1, keepdims=True)))


def _sb_score_bound(qs, kmax_ref):
    return _max_row_norm(qs) * jnp.max(kmax_ref[...]) * 1.01 + 1.0


def _sb_rest_is_zero(run_ref, bound):
    return jnp.max(jnp.maximum(run_ref[0], run_ref[1])) + bound < EXP_IS_ZERO_BELOW


def _sb_attn_fwd(qn, kn, vb, exchange=None):
    s, d = qn.shape
    tk = SB_TK
    tq = _pick(s, 256)
    nq, ndiag = s // tq, tq // tk
    assert tq % (2 * tk) == 0, "tiles below the diagonal are taken two at a time"
    npairs = d // LANES
    scale = 1.0 / math.sqrt(SB_HEAD_DIM)
    tri_ge2, _ = _sb_consts(tk)

    def body(q_ref, k_ref, v_ref, tri_ref, o_ref, acc_ref, run_ref, kmax_ref):
        qi = pl.program_id(1)

        @pl.when(qi == 0)
        def _():
            kmax_ref[...] = jnp.full(kmax_ref.shape, _max_row_norm(k_ref[...]), F32)

        low_half = lax.broadcasted_iota(jnp.int32, (tk, LANES), 1) < SB_HEAD_DIM
        row = lax.broadcasted_iota(jnp.int32, (tq, tk), 0)
        col = lax.broadcasted_iota(jnp.int32, (tq, tk), 1)
        qs = (q_ref[...].astype(F32) * scale).astype(BF16)
        bound = _sb_score_bound(qs, kmax_ref)
        acc_ref[...] = jnp.zeros_like(acc_ref)
        run_ref[...] = jnp.zeros_like(run_ref)
        n_full = qi * ndiag

        def sums(kb, dd):
            koff = pl.multiple_of(kb * tk, tk)
            kcat = _head_stack(k_ref[pl.ds(koff, tk), :], low_half)
            vcat = _head_stack(v_ref[pl.ds(koff, tk), :], low_half)
            z2 = _dot(qs, kcat, NT)
            valid = None if dd is None else row > col + dd * tk
            zs = [z2[:, h * tk:(h + 1) * tk] for h in range(2)]
            return zs, [_sb_tile_sums(z, valid, tri_ref[...])[1] for z in zs], valid, vcat

        def finish(zs, c2s, valid, vcat):
            ws = []
            for h in range(2):
                ws.append(_sb_weights(zs[h], c2s[h], valid, run_ref[h]).astype(BF16))
                run_ref[h] += c2s[h][:, tk:]
            acc_ref[...] += _dot(jnp.concatenate(ws, axis=1), vcat)

        def first_tiles(below):
            pres = [sums(n_full + dd, dd) for dd in reversed(range(ndiag))]
            pres += [sums(n_full - 1 - n, None) for n in range(below)]
            for pre in pres:
                finish(*pre)

        @pl.when(qi == 0)
        def _():
            first_tiles(0)

        @pl.when(qi > 0)
        def _():
            first_tiles(2)

        def two_tiles(carry):
            it, _ = carry
            kb = n_full - 1 - 2 * it
            first, second = sums(kb, None), sums(kb - 1, None)
            finish(*first)
            finish(*second)
            return it + 1, _sb_rest_is_zero(run_ref, bound)

        lax.while_loop(lambda c: jnp.logical_and(c[0] < n_full // 2, jnp.logical_not(c[1])), two_tiles,
                       (jnp.minimum(qi, 1), _sb_rest_is_zero(run_ref, bound)))
        o_ref[...] = acc_ref[...]

    blk = pl.BlockSpec((tq, LANES), lambda p, i: (i, p))
    full = pl.BlockSpec((s, LANES), lambda p, i: (0, p))
    (o,), moved = _call_with_exchange(
        body, exchange, grid=(npairs, nq),
        in_specs=[blk, full, full, pl.BlockSpec((2 * tk, 2 * tk), lambda p, i: (0, 0))],
        out_specs=[blk], out_shape=[jax.ShapeDtypeStruct((s, d), F32)],
        scratch_shapes=[pltpu.VMEM((tq, LANES), F32), pltpu.VMEM((2, tq, tk), F32), pltpu.VMEM((8, LANES), F32)],
        name="sb_attn_fwd", args=(qn, kn, vb, tri_ge2))
    return o, moved


def _sb_attn_bwd(qn, kn, vb, do, exchange=None):
    s, d = qn.shape
    tk = SB_TK
    tq = _pick(s, 256)
    nq, ndiag = s // tq, tq // tk
    assert tq % (2 * tk) == 0, "tiles below the diagonal are taken two at a time"
    npairs = d // LANES
    scale = 1.0 / math.sqrt(SB_HEAD_DIM)
    tri_ge2, tri_le2 = _sb_consts(tk)

    def body(q_ref, k_ref, v_ref, do_ref, tge_ref, tle_ref, dq_ref, dk_ref, dv_ref,
             g_cache, s_cache, run_ref, dq_acc, kmax_ref):
        qi = pl.program_id(1)

        @pl.when(qi == 0)
        def _():
            dk_ref[...] = jnp.zeros_like(dk_ref)
            dv_ref[...] = jnp.zeros_like(dv_ref)
            kmax_ref[...] = jnp.full(kmax_ref.shape, _max_row_norm(k_ref[...]), F32)

        low_half = lax.broadcasted_iota(jnp.int32, (tk, LANES), 1) < SB_HEAD_DIM
        row = lax.broadcasted_iota(jnp.int32, (tq, tk), 0)
        col = lax.broadcasted_iota(jnp.int32, (tq, tk), 1)
        qs = (q_ref[...].astype(F32) * scale).astype(BF16)
        bound = _sb_score_bound(qs, kmax_ref)
        dob = do_ref[...].astype(BF16)
        n_full = qi * ndiag

        def a_sums(kb, dd):
            koff = pl.multiple_of(kb * tk, tk)
            kcat = _head_stack(k_ref[pl.ds(koff, tk), :], low_half)
            vcat = _head_stack(v_ref[pl.ds(koff, tk), :], low_half)
            z2 = _dot(qs, kcat, NT)
            dw2 = _dot(dob, vcat, NT)
            valid = None if dd is None else row > col + dd * tk
            c2s = []
            for h in range(2):
                cols = slice(h * tk, (h + 1) * tk)
                z = z2[:, cols]
                e, c2 = _sb_tile_sums(z, valid, tge_ref[...])
                s_cache[kb, :, cols] = jnp.where(z >= 0, 1.0, e) / (1.0 + e)
                c2s.append(c2)
            return kb, koff, z2, dw2, c2s, valid

        def a_finish(kb, koff, z2, dw2, c2s, valid):
            ws = []
            for h in range(2):
                cols = slice(h * tk, (h + 1) * tk)
                w = _sb_weights(z2[:, cols], c2s[h], valid, run_ref[h])
                run_ref[h] += c2s[h][:, tk:]
                g_cache[kb, :, cols] = w * dw2[:, cols]
                ws.append(w.astype(BF16))
            dv2 = _dot(jnp.concatenate(ws, axis=1), dob, TN)
            dv_ref[pl.ds(koff, tk), :] += jnp.where(low_half, dv2[:tk], dv2[tk:])

        def b_sums(kb, dd):
            gs = [g_cache[kb, :, h * tk:(h + 1) * tk] for h in range(2)]
            p2s = [_dot(jnp.concatenate(_split2(g), axis=1), tle_ref[...]) for g in gs]
            return kb, gs, p2s, (None if dd is None else row > col + dd * tk)

        def b_finish(kb, gs, p2s, valid):
            koff = pl.multiple_of(kb * tk, tk)
            dzs = []
            for h in range(2):
                dz = gs[h] - s_cache[kb, :, h * tk:(h + 1) * tk] * (p2s[h][:, :tk] + run_ref[h])
                if valid is not None:
                    dz = jnp.where(valid, dz, 0.0)
                run_ref[h] += p2s[h][:, tk:]
                dzs.append(dz.astype(BF16))
            dzcat = jnp.concatenate(dzs, axis=1)
            dq_acc[...] += _dot(dzcat, _head_stack(k_ref[pl.ds(koff, tk), :], low_half))
            dk2 = _dot(dzcat, qs, TN)
            dk_ref[pl.ds(koff, tk), :] += jnp.where(low_half, dk2[:tk], dk2[tk:])

        run_ref[...] = jnp.zeros_like(run_ref)
        near = jnp.minimum(qi, 1)

        def a_first_tiles(below):
            pres = [a_sums(n_full + dd, dd) for dd in reversed(range(ndiag))]
            pres += [a_sums(n_full - 1 - n, None) for n in range(below)]
            for pre in pres:
                a_finish(*pre)

        @pl.when(qi == 0)
        def _():
            a_first_tiles(0)

        @pl.when(qi > 0)
        def _():
            a_first_tiles(2)

        def two_a(carry):
            it, _ = carry
            kb = n_full - 1 - 2 * it
            first, second = a_sums(kb, None), a_sums(kb - 1, None)
            a_finish(*first)
            a_finish(*second)
            return it + 1, _sb_rest_is_zero(run_ref, bound)

        trips, _ = lax.while_loop(lambda c: jnp.logical_and(c[0] < n_full // 2, jnp.logical_not(c[1])), two_a,
                                  (near, _sb_rest_is_zero(run_ref, bound)))

        run_ref[...] = jnp.zeros_like(run_ref)
        dq_acc[...] = jnp.zeros_like(dq_acc)
        kb_first = n_full - 2 * trips

        def two_b(it, carry):
            first, second = b_sums(kb_first + 2 * it, None), b_sums(kb_first + 2 * it + 1, None)
            b_finish(*first)
            b_finish(*second)
            return carry

        lax.fori_loop(0, trips - near, two_b, 0)

        def b_last_tiles(below):
            pres = [b_sums(n_full - below + n, None) for n in range(below)]
            pres += [b_sums(n_full + dd, dd) for dd in range(ndiag)]
            for pre in pres:
                b_finish(*pre)

        @pl.when(qi == 0)
        def _():
            b_last_tiles(0)

        @pl.when(qi > 0)
        def _():
            b_last_tiles(2)

        dq_ref[...] = dq_acc[...] * scale

    blk = pl.BlockSpec((tq, LANES), lambda p, i: (i, p))
    full = pl.BlockSpec((s, LANES), lambda p, i: (0, p))
    tri = pl.BlockSpec((2 * tk, 2 * tk), lambda p, i: (0, 0))
    return _call_with_exchange(
        body, exchange, grid=(npairs, nq),
        in_specs=[blk, full, full, blk, tri, tri],
        out_specs=[blk, full, full],
        out_shape=[jax.ShapeDtypeStruct((s, d), F32)] * 3,
        scratch_shapes=[pltpu.VMEM((s // tk, tq, 2 * tk), F32), pltpu.VMEM((s // tk, tq, 2 * tk), F32),
                        pltpu.VMEM((2, tq, tk), F32), pltpu.VMEM((tq, LANES), F32), pltpu.VMEM((8, LANES), F32)],
        name="sb_attn_bwd", args=(qn, kn, vb, do, tri_ge2, tri_le2))


def _hg_consts(c):
    levels = []
    h = c // 2
    while h >= 1:
        levels.append(h)
        h //= 2
    t = np.arange(c)
    j = t[None, :]
    rows, masks = [], []
    for h in levels:
        blk = t // (2 * h)
        mid = blk * 2 * h + h - 1
        second = (t % (2 * h)) >= h
        rows.append(second[:, None] & (j > mid[:, None]) & (j <= t[:, None]))
        rows.append((~second)[:, None] & (j > t[:, None]) & (j <= mid[:, None]))
        masks.append((blk[:, None] == blk[None, :]) & second[:, None] & (~second)[None, :])
    rows.append(j <= t[:, None])
    rows.append(j > t[:, None])
    masks.append(t[:, None] == t[None, :])
    m_all = np.concatenate(rows, axis=0).astype(np.float32)
    mask_all = np.stack(masks, axis=0).astype(np.float32)
    suffix = (t[None, :] >= t[:, None]).astype(np.float32)
    return len(levels), jnp.asarray(m_all, BF16), jnp.asarray(mask_all, F32), jnp.asarray(suffix, BF16)


def _split3(x):
    hi = x.astype(BF16)
    r1 = x - hi.astype(F32)
    mid = r1.astype(BF16)
    lo = (r1 - mid.astype(F32)).astype(BF16)
    return jnp.concatenate([hi, mid, lo], axis=1)


def _join3(e):
    n = e.shape[1] // 3
    return e[:, :n] + e[:, n:2 * n] + e[:, 2 * n:]


def _hg_gates(qr, fr, lb):
    sq = _sigmoid(qr)
    sf = _sigmoid(fr)
    forget = lb + (1.0 - lb) * sf
    return qr * sq, sq, sf, forget, jnp.log(forget), 1.0 - forget


def _hg_scores(q, k, expo, masks, nlev, c):
    qb, kb = q.astype(BF16), k.astype(BF16)
    a = masks[nlev] * _dot(qb, kb, NT)
    scaled = []
    for li in range(nlev):
        fq = jnp.exp(expo[(2 * li) * c:(2 * li + 1) * c])
        fk = jnp.exp(expo[(2 * li + 1) * c:(2 * li + 2) * c])
        qs, ks = (q * fq).astype(BF16), (k * fk).astype(BF16)
        a = a + masks[li] * _dot(qs, ks, NT)
        scaled.append((qs, ks, fq, fk))
    return a, scaled, qb, kb


def _hg_heads_per_step(nh):
    return 2 if nh % 2 == 0 else 1


def _hg_fwd(proj, lb_row, gain_row, exchange=None):
    s, d4 = proj.shape
    d = d4 // 4
    nh = d // HG_HEAD_DIM
    c = min(HG_CHUNK, s)
    tb = _pick(s, 512)
    ncb = tb // c
    nlev, m_all, mask_all, _ = _hg_consts(c)
    nrow = m_all.shape[0]

    hp = 4 if nh % 4 == 0 else _hg_heads_per_step(nh)
    wide = hp * HG_HEAD_DIM

    def body(q_ref, f_ref, i_ref, g_ref, lb_ref, gain_ref, mall_ref, mask_ref, y_ref, o_ref, st_out_ref, st_ref):
        b = pl.program_id(1)

        @pl.when(b == 0)
        def _():
            st_ref[...] = jnp.zeros_like(st_ref)

        gain = gain_ref[...]

        def inside(ci, hh):
            rows = pl.ds(pl.multiple_of(ci * c, c), c)
            cols = slice(hh * HG_HEAD_DIM, (hh + 1) * HG_HEAD_DIM)
            q, _, _, _, lf, k = _hg_gates(q_ref[rows, cols], f_ref[rows, cols], lb_ref[:, cols])
            v = i_ref[rows, cols].astype(BF16)
            expo = _join3(_dot(mall_ref[...], _split3(lf)))
            a, _, _, _ = _hg_scores(q, k, expo, mask_ref[...], nlev, c)
            b_cum = expo[2 * nlev * c:(2 * nlev + 1) * c]
            e_tail = expo[(2 * nlev + 1) * c:(2 * nlev + 2) * c]
            q_in = (q * jnp.exp(b_cum)).astype(BF16)
            k_dec = (k * jnp.exp(e_tail)).astype(BF16)
            return ci, hh, rows, cols, q_in, _dot(a.astype(BF16), v), jnp.exp(b_cum[c - 1:c, :]), _dot(v, k_dec, TN)

        def across(ci, hh, rows, cols, q_in, o_intra, decay, kv):
            st = st_ref[hh]
            st_out_ref[ci, hh] = st
            o = _dot(q_in, st.astype(BF16), NT) + o_intra
            st_ref[hh] = st * decay + kv
            o_ref[rows, cols] = o
            r = lax.rsqrt(jnp.mean(o * o, axis=-1, keepdims=True) + NORM_EPS)
            y_ref[rows, cols] = (o * r * gain * _sigmoid(g_ref[rows, cols])).astype(y_ref.dtype)

        per_trip = 2 if ncb % 2 == 0 else 1

        def trip(it, carry):
            ready = [inside(per_trip * it + n, hh) for n in range(per_trip) for hh in range(hp)]
            for r in ready:
                across(*r)
            return carry

        lax.fori_loop(0, ncb // per_trip, trip, 0)

    part = lambda k: pl.BlockSpec((tb, wide), lambda h, b: (b, k * (nh // hp) + h))
    head_row = pl.BlockSpec((1, wide), lambda h, b: (0, h))
    tok = pl.BlockSpec((tb, wide), lambda h, b: (b, h))
    return _call_with_exchange(
        body, exchange, grid=(nh // hp, s // tb),
        in_specs=[part(0), part(1), part(2), part(3), head_row,
                  pl.BlockSpec((1, HG_HEAD_DIM), lambda h, b: (0, 0)),
                  pl.BlockSpec((nrow, c), lambda h, b: (0, 0)),
                  pl.BlockSpec((nlev + 1, c, c), lambda h, b: (0, 0, 0))],
        out_specs=[tok, tok, pl.BlockSpec((ncb, hp, HG_HEAD_DIM, HG_HEAD_DIM), lambda h, b: (b, h, 0, 0))],
        out_shape=[jax.ShapeDtypeStruct((s, d), BF16), jax.ShapeDtypeStruct((s, d), F32),
                   jax.ShapeDtypeStruct((s // c, nh, HG_HEAD_DIM, HG_HEAD_DIM), F32)],
        scratch_shapes=[pltpu.VMEM((hp, HG_HEAD_DIM, HG_HEAD_DIM), F32)],
        name="hg_fwd", args=(proj, proj, proj, proj, lb_row, gain_row, m_all, mask_all))


def _hg_bwd(proj, lb_row, gain_row, o_saved, states, dy, exchange=None):
    s, d4 = proj.shape
    d = d4 // 4
    nh = d // HG_HEAD_DIM
    c = min(HG_CHUNK, s)
    tb = _pick(s, 512)
    ncb = tb // c
    nb = s // tb
    nlev, m_all, mask_all, suffix = _hg_consts(c)
    nrow = m_all.shape[0]
    hp = _hg_heads_per_step(nh)
    wide = hp * HG_HEAD_DIM

    def body(q_ref, f_ref, i_ref, g_ref, lb_ref, gain_ref, o_ref, st_in_ref, dy_ref, mall_ref, mask_ref, suf_ref,
             dproj_ref, dlb_ref, dgain_ref, dst_ref, run_ref):
        b = pl.program_id(1)

        @pl.when(b == 0)
        def _():
            dst_ref[...] = jnp.zeros_like(dst_ref)
            run_ref[...] = jnp.zeros_like(run_ref)
            dlb_ref[...] = jnp.zeros_like(dlb_ref)
            dgain_ref[...] = jnp.zeros_like(dgain_ref)

        gain = gain_ref[...]

        def head_chunk(ci, rows, hh, cols):
            lb = lb_ref[:, cols]
            qr, fr = q_ref[rows, cols], f_ref[rows, cols]
            q, sq, sf, forget, lf, k = _hg_gates(qr, fr, lb)
            v = i_ref[rows, cols].astype(BF16)
            expo = _join3(_dot(mall_ref[...], _split3(lf)))
            masks = mask_ref[...]
            o = o_ref[rows, cols]
            dyv = dy_ref[rows, cols]
            sg = _sigmoid(g_ref[rows, cols])
            r = lax.rsqrt(jnp.mean(o * o, axis=-1, keepdims=True) + NORM_EPS)
            ohat = o * r
            dyn = dyv * sg
            dproj_ref[3, rows, cols] = (dyv * ohat * gain * sg * (1.0 - sg)).astype(dproj_ref.dtype)
            dgain_ref[:, cols] += jnp.sum(dyn * ohat, axis=0, keepdims=True)
            dohat = dyn * gain
            do = (r * (dohat - ohat * jnp.mean(dohat * ohat, axis=-1, keepdims=True))).astype(BF16)
            dst = dst_ref[hh]
            dstb = dst.astype(BF16)
            qb, kb = q.astype(BF16), k.astype(BF16)
            f_cum = jnp.exp(expo[2 * nlev * c:(2 * nlev + 1) * c])
            f_tail = jnp.exp(expo[(2 * nlev + 1) * c:(2 * nlev + 2) * c])
            q_in = (q * f_cum).astype(BF16)
            k_dec = (k * f_tail).astype(BF16)
            t_in = _join3(_dot(do, _split3(st_in_ref[ci, hh])))
            t_st = _join3(_dot(v, _split3(dst)))
            da = _dot(do, v, NT)
            dam = (masks[nlev] * da).astype(BF16)
            a = masks[nlev] * _dot(qb, kb, NT)
            dq = t_in * f_cum + _dot(dam, kb)
            dk = t_st * f_tail + _dot(dam, qb, TN)
            db = q_in.astype(F32) * t_in - k_dec.astype(F32) * t_st
            for li in range(nlev):
                fq = jnp.exp(expo[(2 * li) * c:(2 * li + 1) * c])
                fk = jnp.exp(expo[(2 * li + 1) * c:(2 * li + 2) * c])
                qs, ks = (q * fq).astype(BF16), (k * fk).astype(BF16)
                a = a + masks[li] * _dot(qs, ks, NT)
                dam = (masks[li] * da).astype(BF16)
                t_q = _dot(dam, ks)
                t_k = _dot(dam, qs, TN)
                dq = dq + t_q * fq
                dk = dk + t_k * fk
                db = db + (qs.astype(F32) * t_q - ks.astype(F32) * t_k)
            dv = _dot(a.astype(BF16), do, TN) + _dot(k_dec, dstb, NT)
            dst_ref[hh] = dst * f_cum[c - 1:c, :] + _dot(do, q_in, TN)
            dlf = _join3(_dot(suf_ref[...], _split3(db))) + run_ref[hh]
            run_ref[hh] = dlf[0:1, :]
            dforget = dlf / forget - dk
            dlb_ref[:, cols] += jnp.sum(dforget * (1.0 - sf), axis=0, keepdims=True)
            dproj_ref[1, rows, cols] = (dforget * (1.0 - lb) * sf * (1.0 - sf)).astype(dproj_ref.dtype)
            dproj_ref[0, rows, cols] = (dq * sq * (1.0 + qr * (1.0 - sq))).astype(dproj_ref.dtype)
            dproj_ref[2, rows, cols] = dv.astype(dproj_ref.dtype)

        def chunk(it, carry):
            ci = ncb - 1 - it
            rows = pl.ds(pl.multiple_of(ci * c, c), c)
            for hh in range(hp):
                head_chunk(ci, rows, hh, slice(hh * HG_HEAD_DIM, (hh + 1) * HG_HEAD_DIM))
            return carry

        lax.fori_loop(0, ncb, chunk, 0)

    part = lambda k: pl.BlockSpec((tb, wide), lambda h, b: (nb - 1 - b, k * (nh // hp) + h))
    head_row = pl.BlockSpec((1, wide), lambda h, b: (0, h))
    tok = pl.BlockSpec((tb, wide), lambda h, b: (nb - 1 - b, h))
    const2 = lambda shape: pl.BlockSpec(shape, lambda h, b: (0, 0))
    return _call_with_exchange(
        body, exchange, grid=(nh // hp, nb),
        in_specs=[part(0), part(1), part(2), part(3), head_row, const2((1, HG_HEAD_DIM)), tok,
                  pl.BlockSpec((ncb, hp, HG_HEAD_DIM, HG_HEAD_DIM), lambda h, b: (nb - 1 - b, h, 0, 0)),
                  tok, const2((nrow, c)), pl.BlockSpec((nlev + 1, c, c), lambda h, b: (0, 0, 0)), const2((c, c))],
        out_specs=[pl.BlockSpec((4, tb, wide), lambda h, b: (0, nb - 1 - b, h)), head_row, head_row],
        out_shape=[jax.ShapeDtypeStruct((4, s, d), BF16)] + [jax.ShapeDtypeStruct((1, d), F32)] * 2,
        scratch_shapes=[pltpu.VMEM((hp, HG_HEAD_DIM, HG_HEAD_DIM), F32), pltpu.VMEM((hp, 1, HG_HEAD_DIM), F32)],
        name="hg_bwd", args=(proj, proj, proj, proj, lb_row, gain_row, o_saved, states, dy, m_all, mask_all, suffix))


def _lb_fwd(logits):
    n, d = logits.shape

    def body(l_ref, lb_ref, p_ref):
        rows = [l_ref[i:i + 1, :] for i in range(n)]
        m = functools.reduce(jnp.maximum, rows)
        es = [jnp.exp(r - m) for r in rows]
        tot = functools.reduce(lambda a, b: a + b, es)
        ps = [e / tot for e in es]
        run = jnp.zeros_like(ps[0])
        for i in range(n):
            run = run + ps[i]
            lb_ref[i:i + 1, :] = run - ps[0]
            p_ref[i:i + 1, :] = ps[i]

    return pl.pallas_call(
        body, out_shape=[jax.ShapeDtypeStruct((n, d), F32)] * 2, name="lb_fwd",
    )(logits)


def _lb_bwd(p, dlb):
    n, d = p.shape

    def body(p_ref, dlb_ref, dl_ref):
        ps = [p_ref[i:i + 1, :] for i in range(n)]
        ds = [dlb_ref[i:i + 1, :] for i in range(n)]
        total = functools.reduce(lambda a, b: a + b, ds)
        dps = []
        for i in range(n):
            dp = functools.reduce(lambda a, b: a + b, ds[i:])
            dps.append(dp - total if i == 0 else dp)
        inner = functools.reduce(lambda a, b: a + b, [pi * di for pi, di in zip(ps, dps)])
        for i in range(n):
            dl_ref[i:i + 1, :] = ps[i] * (dps[i] - inner)

    return pl.pallas_call(body, out_shape=jax.ShapeDtypeStruct((n, d), F32), name="lb_bwd")(p, dlb)


def _as2d(a):
    return a.reshape(-1, a.shape[-1])


def _adamw(w, m, v, grads, exchange=None):
    shape = w.shape
    w2, m2, v2 = _as2d(w), _as2d(m), _as2d(v)
    g2 = [_as2d(g) for g in grads]
    rows, cols = w2.shape
    tr = _pick(rows, 512)
    ng = len(g2)
    bc1 = 1.0 - ADAM_B1 ** ADAM_STEP
    bc2 = 1.0 - ADAM_B2 ** ADAM_STEP

    def body(w_ref, m_ref, v_ref, *rest):
        g = rest[0][...]
        for extra in rest[1:ng]:
            g = g + extra[...]
        g_out, d_out, m_out, v_out = rest[ng:]
        mn = ADAM_B1 * m_ref[...] + (1.0 - ADAM_B1) * g
        vn = ADAM_B2 * v_ref[...] + (1.0 - ADAM_B2) * (g * g)
        m_hat = mn / bc1
        v_hat = vn / bc2
        g_out[...] = g
        d_out[...] = -ADAM_LR * (m_hat / (jnp.sqrt(v_hat) + ADAM_EPS) + ADAM_WD * w_ref[...])
        m_out[...] = mn
        v_out[...] = vn

    spec = pl.BlockSpec((tr, cols), lambda i: (i, 0))
    outs, moved = _call_with_exchange(
        body, exchange, grid=(rows // tr,), in_specs=[spec] * (3 + ng), out_specs=[spec] * 4,
        out_shape=[jax.ShapeDtypeStruct((rows, cols), F32)] * 4, scratch_shapes=[], name="adamw",
        args=(w2, m2, v2, *g2))
    result = tuple(o.reshape(shape) for o in outs)
    return result if exchange is None else (result, moved)


def _sum_slots(parts, recv, chip, into, index):
    _, rows, cols = parts.shape
    tr = _pick(rows, 512)

    def body(chip_ref, own_ref, r0_ref, r1_ref, r2_ref, into_ref, o_ref):
        f = lambda r: r[...].astype(F32)
        o_ref[...] = ((f(own_ref) + f(r0_ref)) + f(r1_ref)) + f(r2_ref)

    grid_spec = pltpu.PrefetchScalarGridSpec(
        num_scalar_prefetch=1, grid=(rows // tr,),
        in_specs=[pl.BlockSpec((None, tr, cols), lambda i, chip_ref: (chip_ref[0], i, 0))]
        + [pl.BlockSpec((None, tr, cols), functools.partial(lambda i, chip_ref, k: (k, i, 0), k=k)) for k in range(3)]
        + [pl.BlockSpec(memory_space=pl.ANY)],
        out_specs=pl.BlockSpec((None, tr, cols), lambda i, chip_ref: (index, i, 0)))
    return pl.pallas_call(
        body, grid_spec=grid_spec, out_shape=jax.ShapeDtypeStruct(into.shape, F32),
        input_output_aliases={5: 0}, compiler_params=_params(("parallel",)), name="sum_slots",
    )(chip, parts, recv, recv, recv, into)


def _pack_rows(pieces):
    cols = pieces[0].shape[1]
    used = sum(p.shape[0] for p in pieces)
    rows = -(-used // 8) * 8

    def body(*refs):
        out_ref = refs[-1]
        at = 0
        for ref in refs[:-1]:
            out_ref[at:at + ref.shape[0], :] = ref[...]
            at += ref.shape[0]
        if at < rows:
            out_ref[at:rows, :] = jnp.zeros((rows - at, cols), F32)

    return pl.pallas_call(body, out_shape=jax.ShapeDtypeStruct((rows, cols), F32), name="pack_rows")(*pieces)


def _sum_devices(gathered):
    n, rows, cols = gathered.shape

    def body(g_ref, o_ref):
        acc = g_ref[0]
        for i in range(1, n):
            acc = acc + g_ref[i]
        o_ref[...] = acc

    return pl.pallas_call(body, out_shape=jax.ShapeDtypeStruct((rows, cols), F32), name="sum_devices")(gathered)


def _coords():
    return lax.axis_index("x"), lax.axis_index("y"), lax.axis_index("c")


def _chip_peers(x, y, c):
    out = []
    for fx, fy in ((0, 1), (1, 0), (1, 1)):
        px = 1 - x if fx else x
        py = 1 - y if fy else y
        out.append(((px, py, c), 2 * px + py))
    return out


class _ChipExchange:
    def __init__(self, kind, arrays):
        self.kind, self.kinds, self.arrays, self.n = kind, [kind] * len(arrays), list(arrays), len(arrays)
        self._shapes()

    def also(self, kind, arrays):
        self.kinds += [kind] * len(arrays)
        self.arrays += list(arrays)
        self.n = len(self.arrays)
        self._shapes()
        return self

    def _shapes(self):
        lead = {"gather": lambda a: (N_CHIPS,) + a.shape, "scatter": lambda a: (3,) + a.shape[1:],
                "swap": lambda a: a.shape}
        self.out_shape = [jax.ShapeDtypeStruct(lead[k](a), a.dtype) for k, a in zip(self.kinds, self.arrays)]
        for k, a in zip(self.kinds, self.arrays):
            assert k != "gather" or a.shape[0] % 2 == 0, "a gathered array is cut in two along its leading axis"
        self.scratch = [pltpu.SemaphoreType.DMA((6 * self.n,)), pltpu.SemaphoreType.DMA((6 * self.n,)),
                        pltpu.SemaphoreType.DMA((self.n,))]

    def copies(self, ins, outs, send_sems, recv_sems, local_sems):
        x, y, c = _coords()
        me = 2 * x + y
        sibling = (x, y, 1 - c)
        starts, waits, last = [], [], []
        for t, kind in enumerate(self.kinds):
            if kind == "swap":
                cp = pltpu.make_async_remote_copy(
                    src_ref=ins[t], dst_ref=outs[t], send_sem=send_sems.at[6 * t], recv_sem=recv_sems.at[6 * t],
                    device_id=sibling, device_id_type=MESH)
                starts.append(cp.start)
                waits += [cp.wait_send, cp.wait_recv]
                continue
            if kind == "gather":
                own = pltpu.make_async_copy(ins[t], outs[t].at[me], local_sems.at[t])
                starts.append(own.start)
                waits.append(own.wait)
                half_rows = ins[t].shape[0] // 2
                mine = pl.ds(c * half_rows, half_rows)
                theirs = pl.ds((1 - c) * half_rows, half_rows)
            for k, (peer, peer_chip) in enumerate(_chip_peers(x, y, c)):
                sems = dict(send_sem=send_sems.at[6 * t + k], recv_sem=recv_sems.at[6 * t + k],
                            device_id=peer, device_id_type=MESH)
                if kind == "scatter":
                    send = pltpu.make_async_remote_copy(src_ref=ins[t].at[peer_chip], dst_ref=outs[t].at[k], **sems)
                    starts.append(send.start)
                    waits += [send.wait_send, send.wait_recv]
                    continue
                send = pltpu.make_async_remote_copy(
                    src_ref=ins[t].at[mine], dst_ref=outs[t].at[me].at[mine], **sems)
                landed = outs[t].at[peer_chip].at[mine]
                recv = pltpu.make_async_remote_copy(src_ref=ins[t].at[mine], dst_ref=landed, **sems)
                pass_on = pltpu.make_async_remote_copy(
                    src_ref=landed, dst_ref=landed, send_sem=send_sems.at[6 * t + 3 + k],
                    recv_sem=recv_sems.at[6 * t + 3 + k], device_id=sibling, device_id_type=MESH)
                handed = pltpu.make_async_remote_copy(
                    src_ref=landed, dst_ref=outs[t].at[peer_chip].at[theirs], send_sem=send_sems.at[6 * t + 3 + k],
                    recv_sem=recv_sems.at[6 * t + 3 + k], device_id=sibling, device_id_type=MESH)
                starts.append(send.start)
                waits += [recv.wait_recv, pass_on.start]
                last += [send.wait_send, pass_on.wait_send, handed.wait_recv]
        return starts, waits + last

    def run(self, name):
        n = self.n

        def body(*refs):
            starts, waits = self.copies(refs[:n], refs[n:2 * n], *refs[2 * n:])
            for f in starts + waits:
                f()

        return pl.pallas_call(body, in_specs=[HBM_SPEC] * n, out_specs=[HBM_SPEC] * n, out_shape=self.out_shape,
                              scratch_shapes=self.scratch, name=name)(*self.arrays)


def _call_with_exchange(body, exchange, *, grid, in_specs, out_specs, out_shape, scratch_shapes, name, args,
                        sequential=False):
    if exchange is None:
        first_axis = "arbitrary" if sequential else "parallel"
        outs = pl.pallas_call(body, grid=grid, in_specs=in_specs, out_specs=out_specs, out_shape=out_shape,
                              scratch_shapes=scratch_shapes,
                              compiler_params=_params((first_axis,) + ("arbitrary",) * (len(grid) - 1)),
                              name=name)(*args)
        return outs, []
    n_in, n_out, n_scr, n = len(in_specs), len(out_specs), len(scratch_shapes), exchange.n

    def wrapped(*refs):
        ins, ex_in = refs[:n_in], refs[n_in:n_in + n]
        outs = refs[n_in + n:n_in + n + n_out]
        ex_out = refs[n_in + n + n_out:n_in + 2 * n + n_out]
        scr = refs[n_in + 2 * n + n_out:n_in + 2 * n + n_out + n_scr]
        sems = refs[n_in + 2 * n + n_out + n_scr:]
        ids = [pl.program_id(a) for a in range(len(grid))]
        first = functools.reduce(jnp.logical_and, [i == 0 for i in ids])
        last = functools.reduce(jnp.logical_and, [i == g - 1 for i, g in zip(ids, grid)])

        @pl.when(first)
        def _():
            for f in exchange.copies(ex_in, ex_out, *sems)[0]:
                f()

        body(*ins, *outs, *scr)

        @pl.when(last)
        def _():
            for f in exchange.copies(ex_in, ex_out, *sems)[1]:
                f()

    res = pl.pallas_call(
        wrapped, grid=grid, in_specs=list(in_specs) + [HBM_SPEC] * n, out_specs=list(out_specs) + [HBM_SPEC] * n,
        out_shape=list(out_shape) + exchange.out_shape, scratch_shapes=list(scratch_shapes) + exchange.scratch,
        compiler_params=_params(("arbitrary",) * len(grid)), name=name + "_" + exchange.kind,
    )(*args, *exchange.arrays)
    return res[:n_out], res[n_out:]


def _gather_devices(a):
    def body(in_ref, out_ref, send_sems, recv_sems, local_sem):
        x, y, c = _coords()
        me = 4 * x + 2 * y + c
        own = pltpu.make_async_copy(in_ref, out_ref.at[me], local_sem)
        own.start()
        waits = [own.wait]
        for k in range(1, N_DEVICES):
            px = 1 - x if k & 4 else x
            py = 1 - y if k & 2 else y
            pc = 1 - c if k & 1 else c
            peer = (px, py, pc)
            send = pltpu.make_async_remote_copy(
                src_ref=in_ref, dst_ref=out_ref.at[me], send_sem=send_sems.at[k - 1], recv_sem=recv_sems.at[k - 1],
                device_id=peer, device_id_type=MESH)
            send.start()
            recv = pltpu.make_async_remote_copy(
                src_ref=in_ref, dst_ref=out_ref.at[4 * px + 2 * py + pc], send_sem=send_sems.at[k - 1],
                recv_sem=recv_sems.at[k - 1], device_id=peer, device_id_type=MESH)
            waits += [send.wait_send, recv.wait_recv]
        for w in waits:
            w()

    return pl.pallas_call(
        body, in_specs=[HBM_SPEC], out_specs=HBM_SPEC,
        out_shape=jax.ShapeDtypeStruct((N_DEVICES,) + a.shape, a.dtype),
        scratch_shapes=[pltpu.SemaphoreType.DMA((N_DEVICES - 1,)), pltpu.SemaphoreType.DMA((N_DEVICES - 1,)),
                        pltpu.SemaphoreType.DMA],
        name="gather_devices",
    )(a)


def _mlp_grad_epilogue(r, u):
    return r * (2.0 * jnp.maximum(u, 0.0))


def kernel(x, norm_gains, sb_w_qkv, sb_q_gain, sb_k_gain, sb_w_o, hg_w_in, hg_lb_logits, hg_norm_gain, hg_w_o, mlp_w1, mlp_w2, loss_target, m_norm_gains, m_sb_w_qkv, m_sb_q_gain, m_sb_k_gain, m_sb_w_o, m_hg_w_in, m_hg_lb_logits, m_hg_norm_gain, m_hg_w_o, m_mlp_w1, m_mlp_w2, v_norm_gains, v_sb_w_qkv, v_sb_q_gain, v_sb_k_gain, v_sb_w_o, v_hg_w_in, v_hg_lb_logits, v_hg_norm_gain, v_hg_w_o, v_mlp_w1, v_mlp_w2):
    depth = norm_gains.shape[0]
    n_sb, n_hg = sb_w_qkv.shape[0], hg_w_in.shape[0]
    xs, tgt = x[0], loss_target[0]
    s, d = xs.shape
    dq = d // N_CHIPS
    cx, cy, cc = _coords()
    chip = 2 * cx + cy
    chip_arr = jnp.reshape(chip, (1,)).astype(jnp.int32)

    def mixer_weights(layer):
        j = layer // 2
        return (sb_w_qkv[j], sb_w_o[j]) if layer % 2 == 0 else (hg_w_in[j], hg_w_o[j])

    w_in_g, ng_g, lbl_g = _ChipExchange(
        "gather", [mixer_weights(0)[0].astype(BF16), norm_gains, hg_lb_logits]).run("gather_first")
    gains = jnp.transpose(ng_g, (1, 2, 0, 3)).reshape(depth, 2, d)
    logits = jnp.transpose(lbl_g, (1, 0, 2)).reshape(n_hg, d)
    lbs, lb_p = _lb_fwd(logits)
    qg_rows = [jnp.tile(sb_q_gain[j], d // SB_HEAD_DIM)[None] for j in range(n_sb)]
    kg_rows = [jnp.tile(sb_k_gain[j], d // SB_HEAD_DIM)[None] for j in range(n_sb)]

    saved, wts = [], []
    xc = xs
    for layer in range(depth):
        j = layer // 2
        ahead = [mixer_weights(layer)[1], mlp_w1[layer], mlp_w2[layer]]
        if layer + 1 < depth:
            ahead.append(mixer_weights(layer + 1)[0])
        gather = _ChipExchange("gather", [a.astype(BF16) for a in ahead])
        h1 = _rmsnorm_fwd(xc, gains[layer, 0][None])
        if layer % 2 == 0:
            qkv = _mm_fwd_cols(h1, w_in_g, name="sb_qkv")
            qn, kn, vb = _qk_norm_fwd(qkv, qg_rows[j], kg_rows[j])
            o, moved = _sb_attn_fwd(qn, kn, vb, gather)
            x_mid = _mm_fwd_rows(o, moved[0], residual=xc, name="sb_out")
            mix = (qkv, qn, kn, vb, o)
        else:
            proj = _mm_fwd_cols(h1, w_in_g, name="hg_in")
            (y, o, states), moved = _hg_fwd(proj, lbs[j][None], hg_norm_gain[j][None], gather)
            x_mid = _mm_fwd_rows(y, moved[0], residual=xc, name="hg_out")
            mix = (proj, y, o, states)
        w_out_g, w1_g, w2_g = moved[:3]
        h2 = _rmsnorm_fwd(x_mid, gains[layer, 1][None])
        u = _mm_fwd_cols(h2, w1_g, name="mlp_up")
        x_out = _mm_fwd_rows(u, w2_g, residual=x_mid, a_fn=_relu2, name="mlp_down")
        saved.append((xc, h1, mix, x_mid, h2, u))
        wts.append((w_in_g, w_out_g, w1_g, w2_g))
        w_in_g = moved[3] if layer + 1 < depth else None
        xc = x_out

    sq, dx = _loss_head(xc, tgt)
    loss = lax.psum(jnp.sum(sq) * (0.5 / d), ("x", "y", "c"))

    dgains = [[None, None] for _ in range(depth)]
    dqg, dkg = [None] * n_sb, [None] * n_sb
    dhgain, dlb = [None] * n_hg, [None] * n_hg
    grads, received, pending = {}, {}, []

    def ready(key, parts):
        grads[key] = parts
        pending.append(key)

    def scatter_of(keys):
        return _ChipExchange("scatter", [grads[k] for k in keys]) if keys else None

    def sent(keys, moved):
        for k, r in zip(keys, moved):
            received[k] = r
            pending.remove(k)

    def chip_sum(kind, layers):
        total = jnp.zeros((len(layers),) + grads[kind, layers[0]].shape[1:], F32)
        for index, l in enumerate(layers):
            total = _sum_slots(grads[kind, l], received[kind, l], chip_arr, total, index)
        return total

    sb_layers, hg_layers = range(0, depth, 2), range(1, depth, 2)
    tensors = [("in", sb_layers), ("out", sb_layers), ("in", hg_layers), ("out", hg_layers),
               ("w1", range(depth)), ("w2", range(depth))]

    for layer in reversed(range(depth)):
        j = layer // 2
        x_in, h1, mix, x_mid, h2, u = saved[layer]
        w_in_g, w_out_g, w1_g, w2_g = wts[layer]
        du = _mm_bwd_rows(dx, w2_g, name="mlp_down_dx", out_dtype=BF16, epi_fn=_mlp_grad_epilogue, epi_args=(u,))
        ready(("w2", layer), _mm_dw_rows(u, dx, a_fn=_relu2, name="mlp_down_dw"))
        ready(("w1", layer), _mm_dw_cols(h2, du, name="mlp_up_dw"))
        dx, dgains[layer][1] = _mm_bwd_cols_norm(du, w1_g, x_mid, gains[layer, 1][None], dx, name="mlp_up_dx")
        if layer % 2 == 0:
            qkv, qn, kn, vb, o = mix
            do = _mm_bwd_rows(dx, w_out_g, name="sb_out_dx")
            ready(("out", layer), _mm_dw_rows(o, dx, name="sb_out_dw"))
            keys = list(pending)
            (dqn, dkn, dv), moved = _sb_attn_bwd(qn, kn, vb, do, scatter_of(keys))
            sent(keys, moved)
            d_in, dqg[j], dkg[j] = _qk_norm_bwd(qkv, qg_rows[j], kg_rows[j], dqn, dkn, dv)
            ready(("in", layer), _mm_dw_cols(h1, d_in, name="sb_qkv_dw"))
            dx_name = "sb_qkv_dx"
        else:
            proj, y, o, states = mix
            dy = _mm_bwd_rows(dx, w_out_g, name="hg_out_dx")
            ready(("out", layer), _mm_dw_rows(y, dx, name="hg_out_dw"))
            keys = list(pending)
            (d_in, dlb[j], dhgain[j]), moved = _hg_bwd(
                proj, lbs[j][None], hg_norm_gain[j][None], o, states, dy, scatter_of(keys))
            sent(keys, moved)
            ready(("in", layer), _mm_dw_cols(h1, d_in, name="hg_in_dw"))
            dx_name = "hg_in_dx"
        if layer > 0:
            dx, dgains[layer][0] = _mm_bwd_cols_norm(d_in, w_in_g, x_in, gains[layer, 0][None], dx, name=dx_name)
        else:
            keys = list(pending)
            early_sums = [chip_sum(*t) for t in tensors[1:]]
            (dx, dgains[layer][0]), moved = _mm_bwd_cols_norm(
                d_in, w_in_g, x_in, gains[layer, 0][None], dx, name=dx_name,
                exchange=scatter_of(keys).also("swap", early_sums))
            sent(keys, moved[:len(keys)])
            early_other = moved[len(keys):]
    grad_x = dx[None]
    dlogits = _lb_bwd(lb_p, jnp.concatenate(dlb, axis=0))

    big_w = [sb_w_qkv, sb_w_o, hg_w_in, hg_w_o, mlp_w1, mlp_w2]
    big_m = [m_sb_w_qkv, m_sb_w_o, m_hg_w_in, m_hg_w_o, m_mlp_w1, m_mlp_w2]
    big_v = [v_sb_w_qkv, v_sb_w_o, v_hg_w_in, v_hg_w_o, v_mlp_w1, v_mlp_w2]
    late_sum = chip_sum(*tensors[0])
    big_second, late_other = _adamw(big_w[1], big_m[1], big_v[1], [early_sums[0], early_other[0]],
                                    _ChipExchange("swap", [late_sum]))
    big = [_adamw(big_w[0], big_m[0], big_v[0], [late_sum, late_other[0]]), big_second]
    big += [_adamw(w, m, v, [a, b]) for w, m, v, a, b in
            zip(big_w[2:], big_m[2:], big_v[2:], early_sums[1:], early_other[1:])]

    pieces = [r for pair in dgains for r in pair] + [dlogits] + dqg + dkg + dhgain
    small = _sum_devices(_gather_devices(_pack_rows(pieces)))
    my_cols = lambda a: lax.dynamic_slice_in_dim(a, chip * dq, dq, axis=1)
    fold = lambda rows, width: jnp.sum(rows.reshape(rows.shape[0], -1, width), axis=1)
    base = 2 * depth + n_hg
    g_ng = my_cols(small[0:2 * depth]).reshape(norm_gains.shape)
    g_lbl = my_cols(small[2 * depth:base])
    g_qg = fold(small[base:base + n_sb], SB_HEAD_DIM)
    g_kg = fold(small[base + n_sb:base + 2 * n_sb], SB_HEAD_DIM)
    g_hgn = fold(small[base + 2 * n_sb:base + 2 * n_sb + n_hg], HG_HEAD_DIM)
    r_ng = _adamw(norm_gains, m_norm_gains, v_norm_gains, [g_ng])
    r_qg = _adamw(sb_q_gain, m_sb_q_gain, v_sb_q_gain, [g_qg])
    r_kg = _adamw(sb_k_gain, m_sb_k_gain, v_sb_k_gain, [g_kg])
    r_lbl = _adamw(hg_lb_logits, m_hg_lb_logits, v_hg_lb_logits, [g_lbl])
    r_hgn = _adamw(hg_norm_gain, m_hg_norm_gain, v_hg_norm_gain, [g_hgn])

    per_weight = [r_ng, big[0], r_qg, r_kg, big[1], big[2], r_lbl, r_hgn, big[3], big[4], big[5]]
    outs = [loss, grad_x]
    for field in range(4):
        outs += [r[field] for r in per_weight]
    return tuple(outs)
```
